```python
import jax, jax.numpy as jnp
from jax import lax
import numpy as np

D_MODEL = 1024
BATCH = 8
SEQ = 8192
DEPTH = 4

N_MIXERS = 3
EXPAND = 2
D_INNER = EXPAND * D_MODEL
CONF_KERNEL = 31
SHORT_KERNEL = 3
FOX_HEADS = 16
FOX_HEAD_DIM = D_INNER // FOX_HEADS
Q_BLOCK = 128
NORM_EPS = 1e-6
N_A = (DEPTH + 2) // 3
N_B = (DEPTH + 1) // 3
N_C = DEPTH // 3

kernel_name = "hybrid_conformer_fox_shortconv_trunk"


def rms_norm(x, g):
    xf = x.astype(jnp.float32)
    y = xf * lax.rsqrt(jnp.mean(xf * xf, axis=-1, keepdims=True) + NORM_EPS)
    return (y * g.astype(jnp.float32)).astype(x.dtype)


def layer_norm(x, g, b):
    xf = x.astype(jnp.float32)
    mu = jnp.mean(xf, axis=-1, keepdims=True)
    var = jnp.mean(jnp.square(xf - mu), axis=-1, keepdims=True)
    y = (xf - mu) * lax.rsqrt(var + NORM_EPS)
    return (y * g.astype(jnp.float32) + b.astype(jnp.float32)).astype(x.dtype)


def causal_depthwise_conv(x, w):
    k_width, channels = w.shape
    kern = w[:, None, :].astype(x.dtype)
    return lax.conv_general_dilated(
        x, kern, window_strides=(1,), padding=[(k_width - 1, 0)],
        dimension_numbers=("NWC", "WIO", "NWC"), feature_group_count=channels)


def conformer_conv_mixer(h, w_in, conv_w, conv_b, ln_g, ln_b, w_out):
    proj = h @ w_in
    val, glu_gate, z = jnp.split(proj, 3, axis=-1)
    u = val * jax.nn.sigmoid(glu_gate)
    u = causal_depthwise_conv(u, conv_w) + conv_b
    u = jax.nn.silu(layer_norm(u, ln_g, ln_b))
    return (u * jax.nn.silu(z)) @ w_out


def blocked_forgetting_attention(q, k, v, c):
    b, s_len, n_h, d_h = q.shape
    n_blocks = s_len // Q_BLOCK
    scale = d_h ** -0.5
    k_pos = jnp.arange(s_len)

    def one_block(i):
        start = i * Q_BLOCK
        q_blk = lax.dynamic_slice_in_dim(q, start, Q_BLOCK, axis=1)
        c_blk = lax.dynamic_slice_in_dim(c, start, Q_BLOCK, axis=2)
        logits = jnp.einsum("bqhd,bkhd->bhqk", q_blk, k,
                            preferred_element_type=jnp.float32) * scale
        logits = logits + c_blk[..., :, None] - c[..., None, :]
        q_pos = start + jnp.arange(Q_BLOCK)
        causal = k_pos[None, :] <= q_pos[:, None]
        logits = jnp.where(causal, logits, -jnp.inf)
        p = jax.nn.softmax(logits, axis=-1)
        return jnp.einsum("bhqk,bkhd->bqhd", p.astype(v.dtype), v)

    out = lax.map(one_block, jnp.arange(n_blocks))
    return out.transpose(1, 0, 2, 3, 4).reshape(b, s_len, n_h, d_h)


def forgetting_attention_mixer(h, w_in, f_bias, q_norm_g, k_norm_g, w_out):
    b, s_len, _ = h.shape
    proj = h @ w_in
    q, k, v, z, f_logit = jnp.split(
        proj, [D_INNER, 2 * D_INNER, 3 * D_INNER, 4 * D_INNER], axis=-1)
    q = rms_norm(q.reshape(b, s_len, FOX_HEADS, FOX_HEAD_DIM), q_norm_g)
    k = rms_norm(k.reshape(b, s_len, FOX_HEADS, FOX_HEAD_DIM), k_norm_g)
    v = v.reshape(b, s_len, FOX_HEADS, FOX_HEAD_DIM)
    log_f = jax.nn.log_sigmoid((f_logit + f_bias).astype(jnp.float32))
    c = jnp.cumsum(log_f, axis=1).transpose(0, 2, 1)
    o = blocked_forgetting_attention(q, k, v, c).reshape(b, s_len, D_INNER)
    return (o * jax.nn.silu(z)) @ w_out


def short_conv_mixer(h, w_in, conv_w, w_out):
    proj = h @ w_in
    u, b_gate, c_gate, z = jnp.split(proj, 4, axis=-1)
    y = b_gate * causal_depthwise_conv(c_gate * u, conv_w)
    return (y * jax.nn.silu(z)) @ w_out


def _fwd_setup_inputs(seed: int = 0) -> dict:
    key = jax.random.key(seed)
    ks = jax.random.split(key, 20)
    f32 = jnp.float32
    nrm = lambda k, shape, s: jax.random.normal(k, shape, f32) * s
    d, e, h = D_MODEL, D_INNER, FOX_HEADS
    return {
        "x": jax.random.normal(ks[0], (BATCH, SEQ, d), f32),
        "a_norm": 1.0 + nrm(ks[1], (N_A, d), 0.05),
        "a_w_in": nrm(ks[2], (N_A, d, 3 * e), d ** -0.5),
        "a_conv_w": nrm(ks[3], (N_A, CONF_KERNEL, e), CONF_KERNEL ** -0.5),
        "a_conv_b": nrm(ks[4], (N_A, e), 0.01),
        "a_ln_g": 1.0 + nrm(ks[5], (N_A, e), 0.05),
        "a_ln_b": nrm(ks[6], (N_A, e), 0.01),
        "a_w_out": nrm(ks[7], (N_A, e, d), e ** -0.5),
        "b_norm": 1.0 + nrm(ks[8], (N_B, d), 0.05),
        "b_w_in": nrm(ks[9], (N_B, d, 4 * e + h), d ** -0.5),
        "b_f_bias": 2.0 + nrm(ks[10], (N_B, h), 0.5),
        "b_q_norm": 1.0 + nrm(ks[11], (N_B, FOX_HEAD_DIM), 0.05),
        "b_k_norm": 1.0 + nrm(ks[12], (N_B, FOX_HEAD_DIM), 0.05),
        "b_w_out": nrm(ks[13], (N_B, e, d), e ** -0.5),
        "c_norm": 1.0 + nrm(ks[14], (N_C, d), 0.05),
        "c_w_in": nrm(ks[15], (N_C, d, 4 * e), d ** -0.5),
        "c_conv_w": nrm(ks[16], (N_C, SHORT_KERNEL, e), SHORT_KERNEL ** -0.5),
        "c_w_out": nrm(ks[17], (N_C, e, d), e ** -0.5),
    }


def _fwd_reference(x, a_norm, a_w_in, a_conv_w, a_conv_b, a_ln_g, a_ln_b, a_w_out,
              b_norm, b_w_in, b_f_bias, b_q_norm, b_k_norm, b_w_out,
              c_norm, c_w_in, c_conv_w, c_w_out):
    for i in range(DEPTH):
        kind, j = i % N_MIXERS, i // N_MIXERS
        if kind == 0:
            hn = rms_norm(x, a_norm[j])
            x = x + conformer_conv_mixer(hn, a_w_in[j], a_conv_w[j], a_conv_b[j],
                                         a_ln_g[j], a_ln_b[j], a_w_out[j])
        elif kind == 1:
            hn = rms_norm(x, b_norm[j])
            x = x + forgetting_attention_mixer(hn, b_w_in[j], b_f_bias[j], b_q_norm[j],
                                               b_k_norm[j], b_w_out[j])
        else:
            hn = rms_norm(x, c_norm[j])
            x = x + short_conv_mixer(hn, c_w_in[j], c_conv_w[j], c_w_out[j])
    return x


import jax as _jax
import jax.numpy as _jnp

TWIN_FORMAT = 'train_step'
FWD_PARAMS = ['x', 'a_norm', 'a_w_in', 'a_conv_w', 'a_conv_b', 'a_ln_g', 'a_ln_b', 'a_w_out', 'b_norm', 'b_w_in', 'b_f_bias', 'b_q_norm', 'b_k_norm', 'b_w_out', 'c_norm', 'c_w_in', 'c_conv_w', 'c_w_out']
TWIN_WEIGHTS = ['a_norm', 'a_w_in', 'a_conv_w', 'a_conv_b', 'a_ln_g', 'a_ln_b', 'a_w_out', 'b_norm', 'b_w_in', 'b_f_bias', 'b_q_norm', 'b_k_norm', 'b_w_out', 'c_norm', 'c_w_in', 'c_conv_w', 'c_w_out']
TWIN_DIFF_INPUT = 'x'
TWIN_INPUTS = ['x', 'a_norm', 'a_w_in', 'a_conv_w', 'a_conv_b', 'a_ln_g', 'a_ln_b', 'a_w_out', 'b_norm', 'b_w_in', 'b_f_bias', 'b_q_norm', 'b_k_norm', 'b_w_out', 'c_norm', 'c_w_in', 'c_conv_w', 'c_w_out', 'loss_target', 'm_a_norm', 'm_a_w_in', 'm_a_conv_w', 'm_a_conv_b', 'm_a_ln_g', 'm_a_ln_b', 'm_a_w_out', 'm_b_norm', 'm_b_w_in', 'm_b_f_bias', 'm_b_q_norm', 'm_b_k_norm', 'm_b_w_out', 'm_c_norm', 'm_c_w_in', 'm_c_conv_w', 'm_c_w_out', 'v_a_norm', 'v_a_w_in', 'v_a_conv_w', 'v_a_conv_b', 'v_a_ln_g', 'v_a_ln_b', 'v_a_w_out', 'v_b_norm', 'v_b_w_in', 'v_b_f_bias', 'v_b_q_norm', 'v_b_k_norm', 'v_b_w_out', 'v_c_norm', 'v_c_w_in', 'v_c_conv_w', 'v_c_w_out']
TWIN_OUTPUTS = ['loss', 'grad_x', 'grad_a_norm', 'grad_a_w_in', 'grad_a_conv_w', 'grad_a_conv_b', 'grad_a_ln_g', 'grad_a_ln_b', 'grad_a_w_out', 'grad_b_norm', 'grad_b_w_in', 'grad_b_f_bias', 'grad_b_q_norm', 'grad_b_k_norm', 'grad_b_w_out', 'grad_c_norm', 'grad_c_w_in', 'grad_c_conv_w', 'grad_c_w_out', 'delta_a_norm', 'delta_a_w_in', 'delta_a_conv_w', 'delta_a_conv_b', 'delta_a_ln_g', 'delta_a_ln_b', 'delta_a_w_out', 'delta_b_norm', 'delta_b_w_in', 'delta_b_f_bias', 'delta_b_q_norm', 'delta_b_k_norm', 'delta_b_w_out', 'delta_c_norm', 'delta_c_w_in', 'delta_c_conv_w', 'delta_c_w_out', 'new_m_a_norm', 'new_m_a_w_in', 'new_m_a_conv_w', 'new_m_a_conv_b', 'new_m_a_ln_g', 'new_m_a_ln_b', 'new_m_a_w_out', 'new_m_b_norm', 'new_m_b_w_in', 'new_m_b_f_bias', 'new_m_b_q_norm', 'new_m_b_k_norm', 'new_m_b_w_out', 'new_m_c_norm', 'new_m_c_w_in', 'new_m_c_conv_w', 'new_m_c_w_out', 'new_v_a_norm', 'new_v_a_w_in', 'new_v_a_conv_w', 'new_v_a_conv_b', 'new_v_a_ln_g', 'new_v_a_ln_b', 'new_v_a_w_out', 'new_v_b_norm', 'new_v_b_w_in', 'new_v_b_f_bias', 'new_v_b_q_norm', 'new_v_b_k_norm', 'new_v_b_w_out', 'new_v_c_norm', 'new_v_c_w_in', 'new_v_c_conv_w', 'new_v_c_w_out']
TWIN_LEAF_KINDS = {'loss': 'loss', 'grad_x': 'grad_x', 'grad_a_norm': 'grad_w', 'grad_a_w_in': 'grad_w', 'grad_a_conv_w': 'grad_w', 'grad_a_conv_b': 'grad_w', 'grad_a_ln_g': 'grad_w', 'grad_a_ln_b': 'grad_w', 'grad_a_w_out': 'grad_w', 'grad_b_norm': 'grad_w', 'grad_b_w_in': 'grad_w', 'grad_b_f_bias': 'grad_w', 'grad_b_q_norm': 'grad_w', 'grad_b_k_norm': 'grad_w', 'grad_b_w_out': 'grad_w', 'grad_c_norm': 'grad_w', 'grad_c_w_in': 'grad_w', 'grad_c_conv_w': 'grad_w', 'grad_c_w_out': 'grad_w', 'delta_a_norm': 'delta_w', 'delta_a_w_in': 'delta_w', 'delta_a_conv_w': 'delta_w', 'delta_a_conv_b': 'delta_w', 'delta_a_ln_g': 'delta_w', 'delta_a_ln_b': 'delta_w', 'delta_a_w_out': 'delta_w', 'delta_b_norm': 'delta_w', 'delta_b_w_in': 'delta_w', 'delta_b_f_bias': 'delta_w', 'delta_b_q_norm': 'delta_w', 'delta_b_k_norm': 'delta_w', 'delta_b_w_out': 'delta_w', 'delta_c_norm': 'delta_w', 'delta_c_w_in': 'delta_w', 'delta_c_conv_w': 'delta_w', 'delta_c_w_out': 'delta_w', 'new_m_a_norm': 'new_m', 'new_m_a_w_in': 'new_m', 'new_m_a_conv_w': 'new_m', 'new_m_a_conv_b': 'new_m', 'new_m_a_ln_g': 'new_m', 'new_m_a_ln_b': 'new_m', 'new_m_a_w_out': 'new_m', 'new_m_b_norm': 'new_m', 'new_m_b_w_in': 'new_m', 'new_m_b_f_bias': 'new_m', 'new_m_b_q_norm': 'new_m', 'new_m_b_k_norm': 'new_m', 'new_m_b_w_out': 'new_m', 'new_m_c_norm': 'new_m', 'new_m_c_w_in': 'new_m', 'new_m_c_conv_w': 'new_m', 'new_m_c_w_out': 'new_m', 'new_v_a_norm': 'new_v', 'new_v_a_w_in': 'new_v', 'new_v_a_conv_w': 'new_v', 'new_v_a_conv_b': 'new_v', 'new_v_a_ln_g': 'new_v', 'new_v_a_ln_b': 'new_v', 'new_v_a_w_out': 'new_v', 'new_v_b_norm': 'new_v', 'new_v_b_w_in': 'new_v', 'new_v_b_f_bias': 'new_v', 'new_v_b_q_norm': 'new_v', 'new_v_b_k_norm': 'new_v', 'new_v_b_w_out': 'new_v', 'new_v_c_norm': 'new_v', 'new_v_c_w_in': 'new_v', 'new_v_c_conv_w': 'new_v', 'new_v_c_w_out': 'new_v'}


def _forward(args):
    return _fwd_reference(*[args[k] for k in FWD_PARAMS])


def _output_shape():
    def fwd():
        inp = _fwd_setup_inputs(0)
        return _fwd_reference(*[inp[k] for k in FWD_PARAMS])
    out = _jax.eval_shape(fwd)
    return out.shape, out.dtype

N_MICROBATCH = 1
ADAM_LR = 0.001
ADAM_B1 = 0.9
ADAM_B2 = 0.999
ADAM_EPS = 1e-08
ADAM_WD = 0.01
ADAM_STEP = 10
PER_EXAMPLE_BATCH_AXIS = {'x': 0, 'loss_target': 0}
SHARED_INPUTS = []
_WEIGHT_DTYPES = {'a_norm': _jnp.float32, 'a_w_in': _jnp.float32, 'a_conv_w': _jnp.float32, 'a_conv_b': _jnp.float32, 'a_ln_g': _jnp.float32, 'a_ln_b': _jnp.float32, 'a_w_out': _jnp.float32, 'b_norm': _jnp.float32, 'b_w_in': _jnp.float32, 'b_f_bias': _jnp.float32, 'b_q_norm': _jnp.float32, 'b_k_norm': _jnp.float32, 'b_w_out': _jnp.float32, 'c_norm': _jnp.float32, 'c_w_in': _jnp.float32, 'c_conv_w': _jnp.float32, 'c_w_out': _jnp.float32}
MOMENT_SCALE = {'a_norm': 9.219879e+00, 'a_w_in': 2.467819e-01, 'a_conv_w': 2.925737e-01, 'a_conv_b': 1.689839e+00, 'a_ln_g': 4.799927e+00, 'a_ln_b': 3.152102e+00, 'a_w_out': 5.286090e-01, 'b_norm': 8.178742e+00, 'b_w_in': 2.202612e-01, 'b_f_bias': 7.345372e+01, 'b_q_norm': 1.099340e+01, 'b_k_norm': 1.096686e+01, 'b_w_out': 3.204875e-01, 'c_norm': 9.679900e+01, 'c_w_in': 5.469894e-01, 'c_conv_w': 6.418378e+00, 'c_w_out': 4.866691e-01}


def _to_microbatches(a, axis):
    t = _jnp.moveaxis(a, axis, 0)
    t = t.reshape((N_MICROBATCH, t.shape[0] // N_MICROBATCH) + t.shape[1:])
    return _jnp.moveaxis(t, 1, axis + 1)


def setup_inputs(seed: int = 0) -> dict:
    inp = _fwd_setup_inputs(seed)
    key = _jax.random.fold_in(_jax.random.key(seed), 7919)
    shape, _ = _output_shape()
    out = dict(inp)
    out["loss_target"] = _jax.random.normal(_jax.random.fold_in(key, 0), shape, _jnp.float32)
    for i, name in enumerate(TWIN_WEIGHTS):
        w = inp[name].astype(_jnp.float32)
        if MOMENT_SCALE is None:
            s = _jnp.sqrt(_jnp.mean(_jnp.square(w)) + 1e-30)
        else:
            s = MOMENT_SCALE[name]
        km, kv = _jax.random.split(_jax.random.fold_in(key, i + 1))
        out[name] = w
        out["m_" + name] = s * _jax.random.normal(km, w.shape, _jnp.float32)
        out["v_" + name] = (s * s) * _jax.random.uniform(kv, w.shape, _jnp.float32, 0.5, 1.5)
    if N_MICROBATCH > 1:
        for name, axis in PER_EXAMPLE_BATCH_AXIS.items():
            out[name] = _to_microbatches(out[name], axis)
    return {'x': out['x'], 'a_norm': out['a_norm'], 'a_w_in': out['a_w_in'], 'a_conv_w': out['a_conv_w'], 'a_conv_b': out['a_conv_b'], 'a_ln_g': out['a_ln_g'], 'a_ln_b': out['a_ln_b'], 'a_w_out': out['a_w_out'], 'b_norm': out['b_norm'], 'b_w_in': out['b_w_in'], 'b_f_bias': out['b_f_bias'], 'b_q_norm': out['b_q_norm'], 'b_k_norm': out['b_k_norm'], 'b_w_out': out['b_w_out'], 'c_norm': out['c_norm'], 'c_w_in': out['c_w_in'], 'c_conv_w': out['c_conv_w'], 'c_w_out': out['c_w_out'], 'loss_target': out['loss_target'], 'm_a_norm': out['m_a_norm'], 'm_a_w_in': out['m_a_w_in'], 'm_a_conv_w': out['m_a_conv_w'], 'm_a_conv_b': out['m_a_conv_b'], 'm_a_ln_g': out['m_a_ln_g'], 'm_a_ln_b': out['m_a_ln_b'], 'm_a_w_out': out['m_a_w_out'], 'm_b_norm': out['m_b_norm'], 'm_b_w_in': out['m_b_w_in'], 'm_b_f_bias': out['m_b_f_bias'], 'm_b_q_norm': out['m_b_q_norm'], 'm_b_k_norm': out['m_b_k_norm'], 'm_b_w_out': out['m_b_w_out'], 'm_c_norm': out['m_c_norm'], 'm_c_w_in': out['m_c_w_in'], 'm_c_conv_w': out['m_c_conv_w'], 'm_c_w_out': out['m_c_w_out'], 'v_a_norm': out['v_a_norm'], 'v_a_w_in': out['v_a_w_in'], 'v_a_conv_w': out['v_a_conv_w'], 'v_a_conv_b': out['v_a_conv_b'], 'v_a_ln_g': out['v_a_ln_g'], 'v_a_ln_b': out['v_a_ln_b'], 'v_a_w_out': out['v_a_w_out'], 'v_b_norm': out['v_b_norm'], 'v_b_w_in': out['v_b_w_in'], 'v_b_f_bias': out['v_b_f_bias'], 'v_b_q_norm': out['v_b_q_norm'], 'v_b_k_norm': out['v_b_k_norm'], 'v_b_w_out': out['v_b_w_out'], 'v_c_norm': out['v_c_norm'], 'v_c_w_in': out['v_c_w_in'], 'v_c_conv_w': out['v_c_conv_w'], 'v_c_w_out': out['v_c_w_out']}


def _loss(weights, diff, rest, loss_target):
    with _jax.named_scope("forward"):
        args = {**rest, TWIN_DIFF_INPUT: diff, **{k: w.astype(_WEIGHT_DTYPES[k]) for k, w in weights.items()}}
        y = _forward(args)
    with _jax.named_scope("loss_head"):
        err = _jnp.square(y.astype(_jnp.float32) - loss_target)
        return 0.5 * _jnp.sum(_jnp.mean(err, axis=-1)) if err.ndim else 0.5 * err


def _adamw(w, g, m, v):
    m = ADAM_B1 * m + (1.0 - ADAM_B1) * g
    v = ADAM_B2 * v + (1.0 - ADAM_B2) * _jnp.square(g)
    m_hat = m / (1.0 - ADAM_B1 ** ADAM_STEP)
    v_hat = v / (1.0 - ADAM_B2 ** ADAM_STEP)
    delta = -ADAM_LR * (m_hat / (_jnp.sqrt(v_hat) + ADAM_EPS) + ADAM_WD * w)
    return delta, m, v


def reference(x, a_norm, a_w_in, a_conv_w, a_conv_b, a_ln_g, a_ln_b, a_w_out, b_norm, b_w_in, b_f_bias, b_q_norm, b_k_norm, b_w_out, c_norm, c_w_in, c_conv_w, c_w_out, loss_target, m_a_norm, m_a_w_in, m_a_conv_w, m_a_conv_b, m_a_ln_g, m_a_ln_b, m_a_w_out, m_b_norm, m_b_w_in, m_b_f_bias, m_b_q_norm, m_b_k_norm, m_b_w_out, m_c_norm, m_c_w_in, m_c_conv_w, m_c_w_out, v_a_norm, v_a_w_in, v_a_conv_w, v_a_conv_b, v_a_ln_g, v_a_ln_b, v_a_w_out, v_b_norm, v_b_w_in, v_b_f_bias, v_b_q_norm, v_b_k_norm, v_b_w_out, v_c_norm, v_c_w_in, v_c_conv_w, v_c_w_out):
    given = dict(x=x, a_norm=a_norm, a_w_in=a_w_in, a_conv_w=a_conv_w, a_conv_b=a_conv_b, a_ln_g=a_ln_g, a_ln_b=a_ln_b, a_w_out=a_w_out, b_norm=b_norm, b_w_in=b_w_in, b_f_bias=b_f_bias, b_q_norm=b_q_norm, b_k_norm=b_k_norm, b_w_out=b_w_out, c_norm=c_norm, c_w_in=c_w_in, c_conv_w=c_conv_w, c_w_out=c_w_out, loss_target=loss_target, m_a_norm=m_a_norm, m_a_w_in=m_a_w_in, m_a_conv_w=m_a_conv_w, m_a_conv_b=m_a_conv_b, m_a_ln_g=m_a_ln_g, m_a_ln_b=m_a_ln_b, m_a_w_out=m_a_w_out, m_b_norm=m_b_norm, m_b_w_in=m_b_w_in, m_b_f_bias=m_b_f_bias, m_b_q_norm=m_b_q_norm, m_b_k_norm=m_b_k_norm, m_b_w_out=m_b_w_out, m_c_norm=m_c_norm, m_c_w_in=m_c_w_in, m_c_conv_w=m_c_conv_w, m_c_w_out=m_c_w_out, v_a_norm=v_a_norm, v_a_w_in=v_a_w_in, v_a_conv_w=v_a_conv_w, v_a_conv_b=v_a_conv_b, v_a_ln_g=v_a_ln_g, v_a_ln_b=v_a_ln_b, v_a_w_out=v_a_w_out, v_b_norm=v_b_norm, v_b_w_in=v_b_w_in, v_b_f_bias=v_b_f_bias, v_b_q_norm=v_b_q_norm, v_b_k_norm=v_b_k_norm, v_b_w_out=v_b_w_out, v_c_norm=v_c_norm, v_c_w_in=v_c_w_in, v_c_conv_w=v_c_conv_w, v_c_w_out=v_c_w_out)
    weights = {n: given[n] for n in TWIN_WEIGHTS}
    shared = {n: given[n] for n in SHARED_INPUTS}
    per_example = {n: given[n] for n in ['x']}
    grad_fn = _jax.value_and_grad(_loss, argnums=(0, 1))

    def one_microbatch(ex, loss_target):
        ex = dict(ex)
        diff = ex.pop(TWIN_DIFF_INPUT)
        return grad_fn(weights, diff, {**shared, **ex}, loss_target)

    if N_MICROBATCH == 1:
        loss, (grad_w, grad_x) = one_microbatch(per_example, given["loss_target"])
    else:
        def body(carry, xs):
            loss_sum, grad_sum = carry
            l_k, (gw_k, gx_k) = one_microbatch(xs[0], xs[1])
            with _jax.named_scope("update"):
                return (loss_sum + l_k, _jax.tree.map(_jnp.add, grad_sum, gw_k)), gx_k

        init = (_jnp.zeros((), _jnp.float32), _jax.tree.map(_jnp.zeros_like, weights))
        (loss, grad_w), grad_x = _jax.lax.scan(body, init, (per_example, given["loss_target"]))
    with _jax.named_scope("update"):
        delta_w, new_m, new_v = {}, {}, {}
        for n in TWIN_WEIGHTS:
            delta_w[n], new_m[n], new_v[n] = _adamw(weights[n], grad_w[n], given["m_" + n], given["v_" + n])
    return (loss, grad_x, *[grad_w[n] for n in TWIN_WEIGHTS], *[delta_w[n] for n in TWIN_WEIGHTS],
            *[new_m[n] for n in TWIN_WEIGHTS], *[new_v[n] for n in TWIN_WEIGHTS])
```

```python
import functools

import numpy as np
import jax
import jax.numpy as jnp
from jax import lax
from jax.experimental import pallas as pl
from jax.experimental.pallas import tpu as pltpu

F32 = jnp.float32
BF16 = jnp.bfloat16
NORM_EPS = 1e-6
HEAD_DIM = 128
LANES = 128
N_CHIPS = 4
N_DEV = 8
VMEM_CAP = 56 << 20
MESH_ID = pl.DeviceIdType.MESH

ADAM_LR = 0.001
ADAM_B1 = 0.9
ADAM_B2 = 0.999
ADAM_EPS = 1e-08
ADAM_WD = 0.01
ADAM_STEP = 10


def _sds(shape, dtype):
    return jax.ShapeDtypeStruct(tuple(shape), dtype)


def _tile(n, pref):
    if n <= pref:
        return n
    for t in range(pref - pref % 8, 0, -8):
        if n % t == 0:
            return t
    raise ValueError(f"no tile for {n} under {pref}")


def _params(vmem_bytes):
    return pltpu.CompilerParams(vmem_limit_bytes=int(min(max(vmem_bytes, 16 << 20), VMEM_CAP)))


def _sigmoid(v):
    return 1.0 / (1.0 + jnp.exp(-v))


def _silu(v):
    return v * _sigmoid(v)


def _dsilu(v):
    s = _sigmoid(v)
    return s * (1.0 + v * (1.0 - s))


def _rowsum8(v):
    r, c = v.shape
    return jnp.sum(v.reshape(r // 8, 8, c), axis=0)


def _col_spec(grouped, rows_block, tile, width, row_of, col_of):
    if grouped:
        per = width // tile
        return pl.BlockSpec((None, rows_block, tile), lambda *ids: (col_of(*ids) // per, row_of(*ids), col_of(*ids) % per))
    return pl.BlockSpec((rows_block, tile), lambda *ids: (row_of(*ids), col_of(*ids)))


def _mm_nn(a, b, *, name, tm=1024, tn=512, out_dtype=F32, add=None):
    m, k = a.shape
    b_grouped = b.ndim == 3
    n = b.shape[0] * b.shape[2] if b_grouped else b.shape[1]
    width = b.shape[2] if b_grouped else n
    tm, tn = _tile(m, tm), _tile(width, tn)

    def body(*refs):
        if add is None:
            a_ref, b_ref, o_ref = refs
        else:
            a_ref, b_ref, r_ref, o_ref = refs
        acc = jnp.dot(a_ref[...].astype(BF16), b_ref[...].astype(BF16), preferred_element_type=F32)
        if add is not None:
            acc = acc + r_ref[...]
        o_ref[...] = acc.astype(o_ref.dtype)

    in_specs = [pl.BlockSpec((tm, k), lambda i, j: (i, 0)),
                _col_spec(b_grouped, k, tn, width, lambda i, j: 0, lambda i, j: j)]
    args = [a, b]
    if add is not None:
        in_specs.append(pl.BlockSpec((tm, tn), lambda i, j: (i, j)))
        args.append(add)
    vmem = 2 * (tm * k * a.dtype.itemsize + k * tn * 2 + tm * tn * 4 * (2 if add is not None else 1)) + tm * tn * 8 + tm * k * 2
    return pl.pallas_call(
        body, name=name, grid=(m // tm, n // tn), in_specs=in_specs,
        out_specs=pl.BlockSpec((tm, tn), lambda i, j: (i, j)),
        out_shape=_sds((m, n), out_dtype), compiler_params=_params(vmem + (4 << 20)),
    )(*args)


def _mm_nt(a, b, *, name, tm=1024, tn=1024, tk=512):
    a_grouped, b_grouped = a.ndim == 3, b.ndim == 3
    m = a.shape[1] if a_grouped else a.shape[0]
    n = a.shape[0] * a.shape[2] if a_grouped else a.shape[1]
    kk = b.shape[1] if b_grouped else b.shape[0]
    wa = a.shape[2] if a_grouped else n
    wb = b.shape[2] if b_grouped else n
    tm, tn = _tile(m, tm), _tile(kk, tn)
    tk = _tile(int(np.gcd(wa, wb)), tk)
    steps = n // tk

    def body(a_ref, b_ref, o_ref):
        part = lax.dot_general(a_ref[...].astype(BF16), b_ref[...].astype(BF16), (((1,), (1,)), ((), ())),
                               preferred_element_type=F32)

        @pl.when(pl.program_id(2) == 0)
        def _():
            o_ref[...] = part

        @pl.when(pl.program_id(2) > 0)
        def _():
            o_ref[...] += part

    in_specs = [_col_spec(a_grouped, tm, tk, wa, lambda i, j, s: i, lambda i, j, s: s),
                _col_spec(b_grouped, tn, tk, wb, lambda i, j, s: j, lambda i, j, s: s)]
    vmem = 2 * (tm * tk * a.dtype.itemsize + tn * tk * b.dtype.itemsize + tm * tn * 4) + tm * tn * 4 + (tm + tn) * tk * 2
    return pl.pallas_call(
        body, name=name, grid=(m // tm, kk // tn, steps), in_specs=in_specs,
        out_specs=pl.BlockSpec((tm, tn), lambda i, j, s: (i, j)),
        out_shape=_sds((m, kk), F32), compiler_params=_params(vmem + (4 << 20)),
    )(a, b)


def _mm_tn(a, b, *, name, out_width=None, tk=1024, tn=512, ts=512):
    s_len, k = a.shape
    b_grouped = b.ndim == 3
    n = b.shape[0] * b.shape[2] if b_grouped else b.shape[1]
    wb = b.shape[2] if b_grouped else n
    wo = out_width if out_width is not None else n
    tk, ts = _tile(k, tk), _tile(s_len, ts)
    tn = _tile(int(np.gcd(wb, wo)), tn)

    def body(a_ref, b_ref, o_ref):
        part = lax.dot_general(a_ref[...].astype(BF16), b_ref[...].astype(BF16), (((0,), (0,)), ((), ())),
                               preferred_element_type=F32)

        @pl.when(pl.program_id(2) == 0)
        def _():
            o_ref[...] = part

        @pl.when(pl.program_id(2) > 0)
        def _():
            o_ref[...] += part

    in_specs = [pl.BlockSpec((ts, tk), lambda i, j, s: (s, i)),
                _col_spec(b_grouped, ts, tn, wb, lambda i, j, s: s, lambda i, j, s: j)]
    out_grouped = out_width is not None
    out_spec = _col_spec(out_grouped, tk, tn, wo, lambda i, j, s: i, lambda i, j, s: j)
    out_shape = _sds((n // wo, k, wo), F32) if out_grouped else _sds((k, n), F32)
    vmem = 2 * (ts * tk * a.dtype.itemsize + ts * tn * b.dtype.itemsize + tk * tn * 4) + tk * tn * 4 + ts * (tk + tn) * 4
    return pl.pallas_call(
        body, name=name, grid=(k // tk, n // tn, s_len // ts), in_specs=in_specs, out_specs=out_spec,
        out_shape=out_shape, compiler_params=_params(vmem + (4 << 20)),
    )(a, b)


def _rms_fwd(x, g, *, name, ts=512):
    s_len, d = x.shape
    ts = _tile(s_len, ts)

    def body(x_ref, g_ref, h_ref):
        xf = x_ref[...]
        r = lax.rsqrt(jnp.mean(xf * xf, axis=-1, keepdims=True) + NORM_EPS)
        h_ref[...] = ((xf * r) * g_ref[...]).astype(h_ref.dtype)

    return pl.pallas_call(
        body, name=name, grid=(s_len // ts,),
        in_specs=[pl.BlockSpec((ts, d), lambda i: (i, 0)), pl.BlockSpec((1, d), lambda i: (0, 0))],
        out_specs=pl.BlockSpec((ts, d), lambda i: (i, 0)), out_shape=_sds((s_len, d), BF16),
        compiler_params=_params(8 * ts * d * 4),
    )(x, g)


def _rms_bwd(x, g, dhs, dres, *, name, ts=512):
    s_len, d = x.shape
    ts = _tile(s_len, ts)
    n_dh = len(dhs)
    last = s_len // ts - 1

    def body(*refs):
        x_ref, g_ref = refs[0], refs[1]
        dh_refs = refs[2:2 + n_dh]
        dres_ref, dx_ref, dg_ref, acc = refs[2 + n_dh:]
        i = pl.program_id(0)

        @pl.when(i == 0)
        def _():
            acc[...] = jnp.zeros_like(acc)

        xf = x_ref[...]
        dy = dh_refs[0][...]
        for r_ in dh_refs[1:]:
            dy = dy + r_[...]
        r = lax.rsqrt(jnp.mean(xf * xf, axis=-1, keepdims=True) + NORM_EPS)
        gd = dy * g_ref[...]
        dot = jnp.mean(xf * gd, axis=-1, keepdims=True)
        dx_ref[...] = dres_ref[...] + r * gd - xf * (r * r * r * dot)
        acc[...] += _rowsum8(dy * (xf * r))

        @pl.when(i == last)
        def _():
            dg_ref[...] = jnp.sum(acc[...], axis=0, keepdims=True)

    row = pl.BlockSpec((ts, d), lambda i: (i, 0))
    vec = pl.BlockSpec((1, d), lambda i: (0, 0))
    return pl.pallas_call(
        body, name=name, grid=(s_len // ts,),
        in_specs=[row, vec] + [row] * n_dh + [row],
        out_specs=[row, vec], out_shape=[_sds((s_len, d), F32), _sds((1, d), F32)],
        scratch_shapes=[pltpu.VMEM((8, d), F32)],
        compiler_params=_params((2 * (3 + n_dh) + 6) * ts * d * 4),
    )(x, g, *dhs, dres)


def _loss_head(y, target, *, name, ts=512):
    s_len, d = y.shape
    ts = _tile(s_len, ts)
    last = s_len // ts - 1

    def body(y_ref, t_ref, dy_ref, loss_ref, acc):
        i = pl.program_id(0)

        @pl.when(i == 0)
        def _():
            acc[...] = jnp.zeros_like(acc)

        err = y_ref[...] - t_ref[...]
        dy_ref[...] = err / d
        acc[...] += _rowsum8(err * err)

        @pl.when(i == last)
        def _():
            loss_ref[...] = (0.5 * jnp.sum(acc[...]) / d).reshape(1, 1)

    row = pl.BlockSpec((ts, d), lambda i: (i, 0))
    return pl.pallas_call(
        body, name=name, grid=(s_len // ts,), in_specs=[row, row],
        out_specs=[row, pl.BlockSpec((1, 1), lambda i: (0, 0))],
        out_shape=[_sds((s_len, d), F32), _sds((1, 1), F32)],
        scratch_shapes=[pltpu.VMEM((8, d), F32)],
        compiler_params=_params(10 * ts * d * 4),
    )(y, target)


CONV_ROWS = 64
CONV_COLS = 256


def _conv_taps(buf, w_ref, k_width, base, ts, tc, init, emit, reverse=False):
    cw = min(tc, CONV_COLS)
    rw = min(ts, CONV_ROWS)
    for cb in range(tc // cw):
        cs = slice(cb * cw, (cb + 1) * cw)
        for rb in range(ts // rw):
            acc = init(slice(rb * rw, (rb + 1) * rw), cs, (rw, cw))
            for k in range(k_width):
                sh = (k_width - 1 - k) if reverse else k
                acc = acc + w_ref[k:k + 1, cs] * buf[base + rb * rw + sh:base + rb * rw + sh + rw, cs]
            emit(slice(rb * rw, (rb + 1) * rw), cs, acc)


def _conv_wgrad(buf, d_ref_val, acc_ref, k_width, base, ts, tc):
    cw = min(tc, CONV_COLS)
    rw = min(ts, CONV_ROWS)
    for cb in range(tc // cw):
        cs = slice(cb * cw, (cb + 1) * cw)
        for rb in range(ts // rw):
            dv = d_ref_val[rb * rw:(rb + 1) * rw, cs]
            for k in range(k_width):
                prod = dv * buf[base + rb * rw + k:base + rb * rw + k + rw, cs]
                acc_ref[8 * k:8 * k + 8, cs] += _rowsum8(prod)


A_HALO = 32


def _a_conv_fwd(proj, conv_w, conv_b, *, name, ts=256, tc=512):
    s_len, e3 = proj.shape
    e = e3 // 3
    k_width = conv_w.shape[0]
    ts, tc = _tile(s_len, ts), _tile(e, tc)
    nc = e // tc
    kp = 32

    def body(val, gate, valh, gateh, w_ref, b_ref, u1_ref, buf):
        i = pl.program_id(1)
        u0h = valh[...] * _sigmoid(gateh[...])
        buf[0:A_HALO, :] = jnp.where(i > 0, u0h, 0.0)
        buf[A_HALO:A_HALO + ts, :] = val[...] * _sigmoid(gate[...])

        def init(rows, cs, shape):
            return jnp.broadcast_to(b_ref[:, cs], shape)

        def emit(rows, cs, acc):
            u1_ref[rows, cs] = acc

        _conv_taps(buf, w_ref, k_width, A_HALO - (k_width - 1), ts, tc, init, emit)

    hb = ts // A_HALO
    in_specs = [
        pl.BlockSpec((ts, tc), lambda j, i: (i, j)),
        pl.BlockSpec((ts, tc), lambda j, i: (i, nc + j)),
        pl.BlockSpec((A_HALO, tc), lambda j, i: (jnp.maximum(i * hb - 1, 0), j)),
        pl.BlockSpec((A_HALO, tc), lambda j, i: (jnp.maximum(i * hb - 1, 0), nc + j)),
        pl.BlockSpec((kp, tc), lambda j, i: (0, j)),
        pl.BlockSpec((1, tc), lambda j, i: (0, j)),
    ]
    w_pad = jnp.zeros((kp, e), F32).at[:k_width].set(conv_w)
    return pl.pallas_call(
        body, name=name, grid=(nc, s_len // ts), in_specs=in_specs,
        out_specs=pl.BlockSpec((ts, tc), lambda j, i: (i, j)), out_shape=_sds((s_len, e), F32),
        scratch_shapes=[pltpu.VMEM((A_HALO + ts, tc), F32)],
        compiler_params=_params(12 * ts * tc * 4),
    )(proj, proj, proj, proj, w_pad, conv_b)


def _ln_rows(u1, g, b):
    mu = jnp.mean(u1, axis=-1, keepdims=True)
    xc = u1 - mu
    var = jnp.mean(xc * xc, axis=-1, keepdims=True)
    rstd = lax.rsqrt(var + NORM_EPS)
    xhat = xc * rstd
    return xhat, rstd, xhat * g + b


def _a_post_fwd(u1, proj, ln_g, ln_b, *, name, ts=256):
    s_len, e = u1.shape
    ts = _tile(s_len, ts)

    def body(u1_ref, z_ref, g_ref, b_ref, o_ref):
        _, _, u2 = _ln_rows(u1_ref[...], g_ref[...], b_ref[...])
        o_ref[...] = (_silu(u2) * _silu(z_ref[...])).astype(o_ref.dtype)

    row = pl.BlockSpec((ts, e), lambda i: (i, 0))
    vec = pl.BlockSpec((1, e), lambda i: (0, 0))
    return pl.pallas_call(
        body, name=name, grid=(s_len // ts,),
        in_specs=[row, pl.BlockSpec((ts, e), lambda i: (i, 2)), vec, vec],
        out_specs=row, out_shape=_sds((s_len, e), BF16), compiler_params=_params(12 * ts * e * 4),
    )(u1, proj, ln_g, ln_b)


def _a_post_bwd(dgated, u1, proj, ln_g, ln_b, *, name, ts=256):
    s_len, e = u1.shape
    ts = _tile(s_len, ts)
    last = s_len // ts - 1

    def body(dg_ref, u1_ref, z_ref, g_ref, b_ref, du1_ref, dz_ref, dlg_ref, dlb_ref, dcb_ref, a_g, a_b, a_c):
        i = pl.program_id(0)

        @pl.when(i == 0)
        def _():
            a_g[...] = jnp.zeros_like(a_g)
            a_b[...] = jnp.zeros_like(a_b)
            a_c[...] = jnp.zeros_like(a_c)

        g = g_ref[...]
        xhat, rstd, u2 = _ln_rows(u1_ref[...], g, b_ref[...])
        z = z_ref[...]
        dgt = dg_ref[...]
        dz_ref[...] = (dgt * _silu(u2) * _dsilu(z)).astype(dz_ref.dtype)
        du2 = dgt * _silu(z) * _dsilu(u2)
        a_g[...] += _rowsum8(du2 * xhat)
        a_b[...] += _rowsum8(du2)
        dxh = du2 * g
        m1 = jnp.mean(dxh, axis=-1, keepdims=True)
        m2 = jnp.mean(dxh * xhat, axis=-1, keepdims=True)
        du1 = rstd * (dxh - m1 - xhat * m2)
        du1_ref[...] = du1
        a_c[...] += _rowsum8(du1)

        @pl.when(i == last)
        def _():
            dlg_ref[...] = jnp.sum(a_g[...], axis=0, keepdims=True)
            dlb_ref[...] = jnp.sum(a_b[...], axis=0, keepdims=True)
            dcb_ref[...] = jnp.sum(a_c[...], axis=0, keepdims=True)

    row = pl.BlockSpec((ts, e), lambda i: (i, 0))
    vec = pl.BlockSpec((1, e), lambda i: (0, 0))
    return pl.pallas_call(
        body, name=name, grid=(s_len // ts,),
        in_specs=[row, row, pl.BlockSpec((ts, e), lambda i: (i, 2)), vec, vec],
        out_specs=[row, row, vec, vec, vec],
        out_shape=[_sds((s_len, e), F32), _sds((s_len, e), BF16), _sds((1, e), F32), _sds((1, e), F32), _sds((1, e), F32)],
        scratch_shapes=[pltpu.VMEM((8, e), F32)] * 3,
        compiler_params=_params(20 * ts * e * 4),
    )(dgated, u1, proj, ln_g, ln_b)


def _a_conv_bwd(du1, proj, dz, conv_w, *, name, ts=256, tc=512):
    s_len, e3 = proj.shape
    e = e3 // 3
    k_width = conv_w.shape[0]
    ts, tc = _tile(s_len, ts), _tile(e, tc)
    nc, nr = e // tc, s_len // ts
    kp = 32
    hb = ts // A_HALO

    def body(val, gate, valh, gateh, d_ref, dh_ref, dz_ref, w_ref, dp_ref, dw_ref, buf_u, buf_d, du0, acc):
        i = pl.program_id(1)

        @pl.when(i == 0)
        def _():
            acc[...] = jnp.zeros_like(acc)

        v = val[...]
        sg = _sigmoid(gate[...])
        buf_u[0:A_HALO, :] = jnp.where(i > 0, valh[...] * _sigmoid(gateh[...]), 0.0)
        buf_u[A_HALO:A_HALO + ts, :] = v * sg
        buf_d[0:ts, :] = d_ref[...]
        buf_d[ts:ts + A_HALO, :] = jnp.where(i < nr - 1, dh_ref[...], 0.0)

        def init(rows, cs, shape):
            return jnp.zeros(shape, F32)

        def emit(rows, cs, a):
            du0[rows, cs] = a

        _conv_taps(buf_d, w_ref, k_width, 0, ts, tc, init, emit, reverse=True)
        _conv_wgrad(buf_u, buf_d, acc, k_width, A_HALO - (k_width - 1), ts, tc)
        d0 = du0[...]
        dp_ref[0] = (d0 * sg).astype(dp_ref.dtype)
        dp_ref[1] = (d0 * v * sg * (1.0 - sg)).astype(dp_ref.dtype)
        dp_ref[2] = dz_ref[...]

        @pl.when(i == nr - 1)
        def _():
            for k in range(kp):
                dw_ref[k:k + 1, :] = jnp.sum(acc[8 * k:8 * k + 8, :], axis=0, keepdims=True)

    in_specs = [
        pl.BlockSpec((ts, tc), lambda j, i: (i, j)),
        pl.BlockSpec((ts, tc), lambda j, i: (i, nc + j)),
        pl.BlockSpec((A_HALO, tc), lambda j, i: (jnp.maximum(i * hb - 1, 0), j)),
        pl.BlockSpec((A_HALO, tc), lambda j, i: (jnp.maximum(i * hb - 1, 0), nc + j)),
        pl.BlockSpec((ts, tc), lambda j, i: (i, j)),
        pl.BlockSpec((A_HALO, tc), lambda j, i: (jnp.minimum((i + 1) * hb, nr * hb - 1), j)),
        pl.BlockSpec((ts, tc), lambda j, i: (i, j)),
        pl.BlockSpec((kp, tc), lambda j, i: (0, j)),
    ]
    w_pad = jnp.zeros((kp, e), F32).at[:k_width].set(conv_w)
    dproj, dw = pl.pallas_call(
        body, name=name, grid=(nc, nr), in_specs=in_specs,
        out_specs=[pl.BlockSpec((3, ts, tc), lambda j, i: (0, i, j)), pl.BlockSpec((kp, tc), lambda j, i: (0, j))],
        out_shape=[_sds((3, s_len, e), BF16), _sds((kp, e), F32)],
        scratch_shapes=[pltpu.VMEM((A_HALO + ts, tc), F32), pltpu.VMEM((ts + A_HALO, tc), F32),
                        pltpu.VMEM((ts, tc), F32), pltpu.VMEM((8 * kp, tc), F32)],
        compiler_params=_params(24 * ts * tc * 4),
    )(proj, proj, proj, proj, du1, du1, dz, w_pad)
    return dproj, dw[:k_width]


C_HALO = 8


def _c_mid_fwd(proj, conv_w, *, name, ts=256, tc=512):
    s_len, e4 = proj.shape
    e = e4 // 4
    k_width = conv_w.shape[0]
    ts, tc = _tile(s_len, ts), _tile(e, tc)
    nc = e // tc
    hb = ts // C_HALO

    def body(u, bg, cg, z, uh, cgh, w_ref, o_ref, buf, y):
        i = pl.program_id(1)
        buf[0:C_HALO, :] = jnp.where(i > 0, uh[...] * cgh[...], 0.0)
        buf[C_HALO:C_HALO + ts, :] = u[...] * cg[...]

        def init(rows, cs, shape):
            return jnp.zeros(shape, F32)

        def emit(rows, cs, a):
            y[rows, cs] = a

        _conv_taps(buf, w_ref, k_width, C_HALO - (k_width - 1), ts, tc, init, emit)
        o_ref[...] = (bg[...] * y[...] * _silu(z[...])).astype(o_ref.dtype)

    def grp(g):
        return pl.BlockSpec((ts, tc), lambda j, i: (i, g * nc + j))

    def halo(g):
        return pl.BlockSpec((C_HALO, tc), lambda j, i: (jnp.maximum(i * hb - 1, 0), g * nc + j))

    w_pad = jnp.zeros((8, e), F32).at[:k_width].set(conv_w)
    return pl.pallas_call(
        body, name=name, grid=(nc, s_len // ts),
        in_specs=[grp(0), grp(1), grp(2), grp(3), halo(0), halo(2), pl.BlockSpec((8, tc), lambda j, i: (0, j))],
        out_specs=pl.BlockSpec((ts, tc), lambda j, i: (i, j)), out_shape=_sds((s_len, e), BF16),
        scratch_shapes=[pltpu.VMEM((C_HALO + ts, tc), F32), pltpu.VMEM((ts, tc), F32)],
        compiler_params=_params(16 * ts * tc * 4),
    )(proj, proj, proj, proj, proj, proj, w_pad)


def _c_mid_bwd(dgated, proj, conv_w, *, name, ts=256, tc=512):
    s_len, e4 = proj.shape
    e = e4 // 4
    k_width = conv_w.shape[0]
    ts, tc = _tile(s_len, ts), _tile(e, tc)
    nc, nr = e // tc, s_len // ts
    hb = ts // C_HALO

    def body(u, bg, cg, z, uh, cgh, dg, dgh, bgh, zh, w_ref, dp_ref, dw_ref, buf_p, buf_d, y, dpv, acc):
        i = pl.program_id(1)

        @pl.when(i == 0)
        def _():
            acc[...] = jnp.zeros_like(acc)

        uv, bgv, cgv, zv, dgv = u[...], bg[...], cg[...], z[...], dg[...]
        buf_p[0:C_HALO, :] = jnp.where(i > 0, uh[...] * cgh[...], 0.0)
        buf_p[C_HALO:C_HALO + ts, :] = uv * cgv
        sz = _silu(zv)
        buf_d[0:ts, :] = dgv * sz * bgv
        buf_d[ts:ts + C_HALO, :] = jnp.where(i < nr - 1, dgh[...] * _silu(zh[...]) * bgh[...], 0.0)

        def init(rows, cs, shape):
            return jnp.zeros(shape, F32)

        def emit_y(rows, cs, a):
            y[rows, cs] = a

        def emit_dp(rows, cs, a):
            dpv[rows, cs] = a

        _conv_taps(buf_p, w_ref, k_width, C_HALO - (k_width - 1), ts, tc, init, emit_y)
        _conv_taps(buf_d, w_ref, k_width, 0, ts, tc, init, emit_dp, reverse=True)
        _conv_wgrad(buf_p, buf_d, acc, k_width, C_HALO - (k_width - 1), ts, tc)
        yv, dp = y[...], dpv[...]
        dp_ref[0] = (dp * cgv).astype(dp_ref.dtype)
        dp_ref[1] = (dgv * sz * yv).astype(dp_ref.dtype)
        dp_ref[2] = (dp * uv).astype(dp_ref.dtype)
        dp_ref[3] = (dgv * bgv * yv * _dsilu(zv)).astype(dp_ref.dtype)

        @pl.when(i == nr - 1)
        def _():
            for k in range(8):
                dw_ref[k:k + 1, :] = jnp.sum(acc[8 * k:8 * k + 8, :], axis=0, keepdims=True)

    def grp(g):
        return pl.BlockSpec((ts, tc), lambda j, i: (i, g * nc + j))

    def prev(g):
        return pl.BlockSpec((C_HALO, tc), lambda j, i: (jnp.maximum(i * hb - 1, 0), g * nc + j))

    def nxt(g):
        return pl.BlockSpec((C_HALO, tc), lambda j, i: (jnp.minimum((i + 1) * hb, nr * hb - 1), g * nc + j))

    w_pad = jnp.zeros((8, e), F32).at[:k_width].set(conv_w)
    dproj, dw = pl.pallas_call(
        body, name=name, grid=(nc, nr),
        in_specs=[grp(0), grp(1), grp(2), grp(3), prev(0), prev(2),
                  pl.BlockSpec((ts, tc), lambda j, i: (i, j)),
                  pl.BlockSpec((C_HALO, tc), lambda j, i: (jnp.minimum((i + 1) * hb, nr * hb - 1), j)),
                  nxt(1), nxt(3), pl.BlockSpec((8, tc), lambda j, i: (0, j))],
        out_specs=[pl.BlockSpec((4, ts, tc), lambda j, i: (0, i, j)), pl.BlockSpec((8, tc), lambda j, i: (0, j))],
        out_shape=[_sds((4, s_len, e), BF16), _sds((8, e), F32)],
        scratch_shapes=[pltpu.VMEM((C_HALO + ts, tc), F32), pltpu.VMEM((ts + C_HALO, tc), F32),
                        pltpu.VMEM((ts, tc), F32), pltpu.VMEM((ts, tc), F32), pltpu.VMEM((64, tc), F32)],
        compiler_params=_params(32 * ts * tc * 4),
    )(proj, proj, proj, proj, proj, proj, dgated, dgated, proj, proj, w_pad)
    return dproj, dw[:k_width]


def _head_rms(xv, g):
    r = lax.rsqrt(jnp.mean(xv * xv, axis=-1, keepdims=True) + NORM_EPS)
    return r, xv * r * g


def _b_qk_fwd(proj, gq, gk, *, name, ts=512, tc=512):
    s_len, e4 = proj.shape
    e = e4 // 4
    ts, tc = _tile(s_len, ts), _tile(e, tc)
    nc = e // tc

    def body(q, k, v, gq_ref, gk_ref, qn, kn, vb):
        for h in range(tc // HEAD_DIM):
            cs = slice(h * HEAD_DIM, (h + 1) * HEAD_DIM)
            qn[:, cs] = _head_rms(q[:, cs], gq_ref[...])[1].astype(qn.dtype)
            kn[:, cs] = _head_rms(k[:, cs], gk_ref[...])[1].astype(kn.dtype)
        vb[...] = v[...].astype(vb.dtype)

    def grp(g):
        return pl.BlockSpec((ts, tc), lambda i, j: (i, g * nc + j))

    vec = pl.BlockSpec((1, HEAD_DIM), lambda i, j: (0, 0))
    out = pl.BlockSpec((ts, tc), lambda i, j: (i, j))
    return pl.pallas_call(
        body, name=name, grid=(s_len // ts, nc), in_specs=[grp(0), grp(1), grp(2), vec, vec],
        out_specs=[out, out, out], out_shape=[_sds((s_len, e), BF16)] * 3,
        compiler_params=_params(16 * ts * tc * 4),
    )(proj, proj, proj, gq, gk)


def _b_qk_bwd(dqn, dkn, dv, dz, proj, gq, gk, *, name, ts=512, tc=512):
    s_len, e4 = proj.shape
    e = e4 // 4
    ts, tc = _tile(s_len, ts), _tile(e, tc)
    nc, nr = e // tc, s_len // ts

    def body(dq_ref, dk_ref, dv_ref, dz_ref, q, k, gq_ref, gk_ref, dp_ref, dgq_ref, dgk_ref, a_q, a_k):
        i, j = pl.program_id(0), pl.program_id(1)

        @pl.when((i == 0) & (j == 0))
        def _():
            a_q[...] = jnp.zeros_like(a_q)
            a_k[...] = jnp.zeros_like(a_k)

        for h in range(tc // HEAD_DIM):
            cs = slice(h * HEAD_DIM, (h + 1) * HEAD_DIM)
            for slot, src, d_ref, g_ref, acc in ((0, q, dq_ref, gq_ref, a_q), (1, k, dk_ref, gk_ref, a_k)):
                xv = src[:, cs]
                dy = d_ref[:, cs]
                r = lax.rsqrt(jnp.mean(xv * xv, axis=-1, keepdims=True) + NORM_EPS)
                gd = dy * g_ref[...]
                dot = jnp.mean(xv * gd, axis=-1, keepdims=True)
                dp_ref[slot, :, cs] = (r * gd - xv * (r * r * r * dot)).astype(dp_ref.dtype)
                acc[...] += _rowsum8(dy * (xv * r))
        dp_ref[2] = dv_ref[...].astype(dp_ref.dtype)
        dp_ref[3] = dz_ref[...]

        @pl.when((i == nr - 1) & (j == nc - 1))
        def _():
            dgq_ref[...] = jnp.sum(a_q[...], axis=0, keepdims=True)
            dgk_ref[...] = jnp.sum(a_k[...], axis=0, keepdims=True)

    blk = pl.BlockSpec((ts, tc), lambda i, j: (i, j))
    vec = pl.BlockSpec((1, HEAD_DIM), lambda i, j: (0, 0))

    def grp(g):
        return pl.BlockSpec((ts, tc), lambda i, j: (i, g * nc + j))

    return pl.pallas_call(
        body, name=name, grid=(nr, nc), in_specs=[blk, blk, blk, blk, grp(0), grp(1), vec, vec],
        out_specs=[pl.BlockSpec((4, ts, tc), lambda i, j: (0, i, j)), vec, vec],
        out_shape=[_sds((4, s_len, e), BF16), _sds((1, HEAD_DIM), F32), _sds((1, HEAD_DIM), F32)],
        scratch_shapes=[pltpu.VMEM((8, HEAD_DIM), F32)] * 2,
        compiler_params=_params(24 * ts * tc * 4),
    )(dqn, dkn, dv, dz, proj, proj, gq, gk)


def _log_sigmoid(x):
    y = jnp.exp(-jnp.abs(x))
    u = 1.0 + y
    log1p = jnp.where(u == 1.0, y, jnp.log(u) * (y / jnp.where(u == 1.0, 1.0, u - 1.0)))
    return jnp.minimum(x, 0.0) - log1p


def _split3(v):
    hi = v.astype(BF16)
    r1 = v - hi.astype(F32)
    mid = r1.astype(BF16)
    lo = (r1 - mid.astype(F32)).astype(BF16)
    return hi, mid, lo


def _tri_matmul(tri, v):
    hi, mid, lo = _split3(v)
    return (jnp.dot(tri, hi, preferred_element_type=F32) + jnp.dot(tri, mid, preferred_element_type=F32)
            + jnp.dot(tri, lo, preferred_element_type=F32))


def _b_cumsum(fl, bias, *, name, t=512):
    s_len, w = fl.shape
    t = _tile(s_len, t)

    def body(fl_ref, b_ref, c_ref, carry):
        @pl.when(pl.program_id(0) == 0)
        def _():
            carry[...] = jnp.zeros_like(carry)

        logf = _log_sigmoid(fl_ref[...] + b_ref[...])
        row = lax.broadcasted_iota(jnp.int32, (t, t), 0)
        col = lax.broadcasted_iota(jnp.int32, (t, t), 1)
        tri = jnp.where(col <= row, 1.0, 0.0).astype(BF16)
        c = _tri_matmul(tri, logf) + carry[...]
        c_ref[...] = c
        carry[...] = c[t - 1:t, :]

    return pl.pallas_call(
        body, name=name, grid=(s_len // t,),
        in_specs=[pl.BlockSpec((t, w), lambda i: (i, 0)), pl.BlockSpec((1, w), lambda i: (0, 0))],
        out_specs=pl.BlockSpec((t, w), lambda i: (i, 0)), out_shape=_sds((s_len, w), F32),
        scratch_shapes=[pltpu.VMEM((1, w), F32)], compiler_params=_params(16 << 20),
    )(fl, bias)


def _b_cumsum_bwd(dc, fl, bias, *, name, t=512):
    s_len, w = fl.shape
    t = _tile(s_len, t)
    n = s_len // t

    def body(dc_ref, fl_ref, b_ref, dfl_ref, db_ref, carry, acc):
        i = pl.program_id(0)

        @pl.when(i == 0)
        def _():
            carry[...] = jnp.zeros_like(carry)
            acc[...] = jnp.zeros_like(acc)

        row = lax.broadcasted_iota(jnp.int32, (t, t), 0)
        col = lax.broadcasted_iota(jnp.int32, (t, t), 1)
        tri = jnp.where(col >= row, 1.0, 0.0).astype(BF16)
        dlogf = _tri_matmul(tri, dc_ref[...]) + carry[...]
        carry[...] = dlogf[0:1, :]
        dfl = dlogf * _sigmoid(-(fl_ref[...] + b_ref[...]))
        dfl_ref[...] = dfl
        acc[...] += _rowsum8(dfl)

        @pl.when(i == n - 1)
        def _():
            db_ref[...] = jnp.sum(acc[...], axis=0, keepdims=True)

    rev = pl.BlockSpec((t, w), lambda i: (n - 1 - i, 0))
    vec = pl.BlockSpec((1, w), lambda i: (0, 0))
    return pl.pallas_call(
        body, name=name, grid=(n,), in_specs=[rev, rev, vec], out_specs=[rev, vec],
        out_shape=[_sds((s_len, w), F32), _sds((1, w), F32)],
        scratch_shapes=[pltpu.VMEM((1, w), F32), pltpu.VMEM((8, w), F32)], compiler_params=_params(16 << 20),
    )(dc, fl, bias)


def _b_gate_fwd(o, proj, *, name, ts=512, tc=512):
    s_len, e = o.shape
    ts, tc = _tile(s_len, ts), _tile(e, tc)
    nc = e // tc

    def body(o_ref, z_ref, g_ref):
        g_ref[...] = (o_ref[...] * _silu(z_ref[...])).astype(g_ref.dtype)

    blk = pl.BlockSpec((ts, tc), lambda i, j: (i, j))
    return pl.pallas_call(
        body, name=name, grid=(s_len // ts, nc),
        in_specs=[blk, pl.BlockSpec((ts, tc), lambda i, j: (i, 3 * nc + j))], out_specs=blk,
        out_shape=_sds((s_len, e), BF16), compiler_params=_params(12 * ts * tc * 4),
    )(o, proj)


def _b_gate_bwd(dgated, o, o_lo, proj, *, name, hg, ts=512):
    s_len, e = o.shape
    ts = _tile(s_len, ts)
    w = hg * HEAD_DIM
    ng = e // w

    def body(dg_ref, o_ref, olo_ref, z_ref, do_ref, dz_ref, dl_ref):
        dgt, ov, zv = dg_ref[...], o_ref[...], z_ref[...]
        dob = (dgt * _silu(zv)).astype(do_ref.dtype)
        do_ref[...] = dob
        dz_ref[...] = (dgt * ov * _dsilu(zv)).astype(dz_ref.dtype)
        prod = dob.astype(F32) * (ov + olo_ref[...])
        lane = lax.broadcasted_iota(jnp.int32, (ts, LANES), 1)
        d = jnp.zeros((ts, LANES), F32)
        for hh in range(hg):
            d = jnp.where(lane == hh, jnp.sum(prod[:, hh * HEAD_DIM:(hh + 1) * HEAD_DIM], axis=-1, keepdims=True), d)
        dl_ref[...] = d

    blk = pl.BlockSpec((ts, w), lambda i, j: (i, j))
    return pl.pallas_call(
        body, name=name, grid=(s_len // ts, ng),
        in_specs=[blk, blk, blk, pl.BlockSpec((ts, w), lambda i, j: (i, 3 * ng + j))],
        out_specs=[blk, blk, pl.BlockSpec((ts, LANES), lambda i, j: (i, j))],
        out_shape=[_sds((s_len, e), BF16), _sds((s_len, e), BF16), _sds((s_len, ng * LANES), F32)],
        compiler_params=_params(20 * ts * w * 4),
    )(dgated, o, o_lo, proj)


def _attn_scores(q, k, cc, cr, diag, t, scale):
    s = lax.dot_general(q, k, (((1,), (1,)), ((), ())), preferred_element_type=F32) * scale
    s = s + cc - cr
    if diag:
        row = lax.broadcasted_iota(jnp.int32, (t, t), 0)
        col = lax.broadcasted_iota(jnp.int32, (t, t), 1)
        s = jnp.where(col <= row, s, -jnp.inf)
    return s


def _fox_fwd(qn, kn, vb, c_col, c_row, *, name, hg, t=512):
    s_len, e = qn.shape
    t = _tile(s_len, t)
    w = hg * HEAD_DIM
    ng, n = e // w, s_len // t
    scale = HEAD_DIM ** -0.5

    def body(q_ref, k_ref, v_ref, cc_ref, cr_ref, o_ref, olo_ref, lse_ref, m_s, l_s, acc_s, lo_s):
        i, j = pl.program_id(1), pl.program_id(2)

        @pl.when(j == 0)
        def _():
            m_s[...] = jnp.full_like(m_s, -jnp.inf)
            l_s[...] = jnp.zeros_like(l_s)
            acc_s[...] = jnp.zeros_like(acc_s)
            lo_s[...] = jnp.zeros_like(lo_s)

        def step(diag):
            for hh in range(hg):
                cs = slice(hh * HEAD_DIM, (hh + 1) * HEAD_DIM)
                s = _attn_scores(q_ref[:, cs], k_ref[:, cs], cc_ref[:, hh:hh + 1], cr_ref[hh], diag, t, scale)
                m_prev = m_s[hh]
                m_new = jnp.maximum(m_prev, jnp.max(s, axis=-1, keepdims=True))
                alpha = jnp.exp(m_prev - m_new)
                p = jnp.exp(s - m_new)
                l_s[hh] = alpha * l_s[hh] + jnp.sum(p, axis=-1, keepdims=True)
                p_hi = p.astype(BF16)
                p_lo = (p - p_hi.astype(F32)).astype(BF16)
                acc_s[:, cs] = alpha * acc_s[:, cs] + jnp.dot(p_hi, v_ref[:, cs], preferred_element_type=F32)
                lo_s[:, cs] = alpha * lo_s[:, cs] + jnp.dot(p_lo, v_ref[:, cs], preferred_element_type=F32)
                m_s[hh] = m_new

        @pl.when(j < i)
        def _():
            step(False)

        @pl.when(j == i)
        def _():
            step(True)
            lane = lax.broadcasted_iota(jnp.int32, (t, LANES), 1)
            lse = jnp.zeros((t, LANES), F32)
            for hh in range(hg):
                cs = slice(hh * HEAD_DIM, (hh + 1) * HEAD_DIM)
                o_ref[:, cs] = acc_s[:, cs] / l_s[hh]
                olo_ref[:, cs] = lo_s[:, cs] / l_s[hh]
                lse = jnp.where(lane == hh, m_s[hh] + jnp.log(l_s[hh]), lse)
            lse_ref[...] = lse

    qspec = pl.BlockSpec((t, w), lambda g, i, j: (i, g))
    kspec = pl.BlockSpec((t, w), lambda g, i, j: (jnp.minimum(j, i), g))
    stat = pl.BlockSpec((t, LANES), lambda g, i, j: (i, g))
    crow = pl.BlockSpec((hg, 1, t), lambda g, i, j: (g, 0, jnp.minimum(j, i)))
    return pl.pallas_call(
        body, name=name, grid=(ng, n, n), in_specs=[qspec, kspec, kspec, stat, crow],
        out_specs=[qspec, qspec, stat],
        out_shape=[_sds((s_len, e), F32), _sds((s_len, e), F32), _sds((s_len, ng * LANES), F32)],
        scratch_shapes=[pltpu.VMEM((hg, t, 1), F32), pltpu.VMEM((hg, t, 1), F32), pltpu.VMEM((t, w), F32),
                        pltpu.VMEM((t, w), F32)],
        compiler_params=_params(12 * t * t * 4 + 24 * t * w * 4),
    )(qn, kn, vb, c_col, c_row)


def _fox_bwd(qn, kn, vb, do, lse, delta, c_col, c_row, *, name, hg, t=512):
    s_len, e = qn.shape
    t = _tile(s_len, t)
    w = hg * HEAD_DIM
    ng, n = e // w, s_len // t
    scale = HEAD_DIM ** -0.5

    def body(q_ref, k_ref, v_ref, do_ref, lse_ref, dl_ref, cc_ref, cr_ref, dq_ref, dk_ref, dv_ref, dc_ref,
             dk_s, dv_s, dc_s):
        j, i = pl.program_id(1), pl.program_id(2)

        @pl.when((j == 0) & (i == 0))
        def _():
            dq_ref[...] = jnp.zeros_like(dq_ref)

        @pl.when(i == j)
        def _():
            dk_s[...] = jnp.zeros_like(dk_s)
            dv_s[...] = jnp.zeros_like(dv_s)
            dc_s[...] = jnp.zeros_like(dc_s)

        def step(diag):
            rows = pl.ds(pl.multiple_of(i * t, t), t)
            for hh in range(hg):
                cs = slice(hh * HEAD_DIM, (hh + 1) * HEAD_DIM)
                q, k, v, dov = q_ref[:, cs], k_ref[:, cs], v_ref[:, cs], do_ref[:, cs]
                s = _attn_scores(q, k, cc_ref[:, hh:hh + 1], cr_ref[hh], diag, t, scale)
                p = jnp.exp(s - lse_ref[:, hh:hh + 1])
                dv_s[:, cs] += lax.dot_general(p.astype(BF16), dov, (((0,), (0,)), ((), ())), preferred_element_type=F32)
                dp = lax.dot_general(dov, v, (((1,), (1,)), ((), ())), preferred_element_type=F32)
                ds = p * (dp - dl_ref[:, hh:hh + 1])
                dc_s[hh] -= jnp.sum(ds, axis=0, keepdims=True)
                dsb = (ds * scale).astype(BF16)
                dk_s[:, cs] += lax.dot_general(dsb, q, (((0,), (0,)), ((), ())), preferred_element_type=F32)
                dq_ref[rows, cs] += jnp.dot(dsb, k, preferred_element_type=F32)

        @pl.when(i > j)
        def _():
            step(False)

        @pl.when(i == j)
        def _():
            step(True)

        @pl.when(i == n - 1)
        def _():
            dk_ref[...] = dk_s[...]
            dv_ref[...] = dv_s[...]
            dc_ref[...] = dc_s[...]

    qspec = pl.BlockSpec((t, w), lambda g, j, i: (jnp.maximum(i, j), g))
    kspec = pl.BlockSpec((t, w), lambda g, j, i: (j, g))
    stat = pl.BlockSpec((t, LANES), lambda g, j, i: (jnp.maximum(i, j), g))
    crow = pl.BlockSpec((hg, 1, t), lambda g, j, i: (g, 0, j))
    return pl.pallas_call(
        body, name=name, grid=(ng, n, n), in_specs=[qspec, kspec, kspec, qspec, stat, stat, stat, crow],
        out_specs=[pl.BlockSpec((s_len, w), lambda g, j, i: (0, g)), kspec, kspec, crow],
        out_shape=[_sds((s_len, e), F32), _sds((s_len, e), F32), _sds((s_len, e), F32), _sds((e // HEAD_DIM, 1, s_len), F32)],
        scratch_shapes=[pltpu.VMEM((t, w), F32), pltpu.VMEM((t, w), F32), pltpu.VMEM((hg, 1, t), F32)],
        compiler_params=_params(2 * s_len * w * 4 + 16 * t * t * 4 + 24 * t * w * 4),
    )(qn, kn, vb, do, lse, delta, c_col, c_row)


def _adamw(w, g, m, v, *, name, tr=256):
    r, c = w.shape
    tr = _tile(r, tr)

    def body(w_ref, g_ref, m_ref, v_ref, d_ref, nm_ref, nv_ref):
        gv = g_ref[...]
        m2 = ADAM_B1 * m_ref[...] + (1.0 - ADAM_B1) * gv
        v2 = ADAM_B2 * v_ref[...] + (1.0 - ADAM_B2) * (gv * gv)
        m_hat = m2 / (1.0 - ADAM_B1 ** ADAM_STEP)
        v_hat = v2 / (1.0 - ADAM_B2 ** ADAM_STEP)
        d_ref[...] = -ADAM_LR * (m_hat / (jnp.sqrt(v_hat) + ADAM_EPS) + ADAM_WD * w_ref[...])
        nm_ref[...] = m2
        nv_ref[...] = v2

    blk = pl.BlockSpec((tr, c), lambda i: (i, 0))
    return pl.pallas_call(
        body, name=name, grid=(r // tr,), in_specs=[blk] * 4, out_specs=[blk] * 3,
        out_shape=[_sds((r, c), F32)] * 3, compiler_params=_params(20 * tr * c * 4),
    )(w, g, m, v)


def _add2(a, b, *, name, tr=256):
    g, r, c = a.shape
    tr = _tile(r, tr)

    def body(a_ref, b_ref, o_ref):
        o_ref[...] = a_ref[...] + b_ref[...]

    blk = pl.BlockSpec((None, tr, c), lambda s, i: (s, i, 0))
    return pl.pallas_call(
        body, name=name, grid=(g, r // tr), in_specs=[blk, blk], out_specs=blk,
        out_shape=_sds((g, r, c), F32), compiler_params=_params(8 * tr * c * 4),
    )(a, b)


def _sum_slots(a, *, name, tr=256):
    n, r, c = a.shape
    tr = _tile(r, tr)

    def body(a_ref, o_ref):
        acc = a_ref[0]
        for k in range(1, n):
            acc = acc + a_ref[k]
        o_ref[...] = acc

    return pl.pallas_call(
        body, name=name, grid=(r // tr,), in_specs=[pl.BlockSpec((n, tr, c), lambda i: (0, i, 0))],
        out_specs=pl.BlockSpec((tr, c), lambda i: (i, 0)), out_shape=_sds((r, c), F32),
        compiler_params=_params(4 * (n + 2) * tr * c * 4),
    )(a)


_ANY = pl.BlockSpec(memory_space=pl.ANY)


def _me():
    return lax.axis_index("x"), lax.axis_index("y"), lax.axis_index("c")


def _chip_peers(x, y):
    return [(2 * (1 - x) + y, (1 - x, y)), (2 * x + (1 - y), (x, 1 - y)), (2 * (1 - x) + (1 - y), (1 - x, 1 - y))]


def _exchange(name, ins, out_shapes, plan):
    n_in, n_out = len(ins), len(out_shapes)

    def body(*refs):
        in_refs, out_refs = refs[:n_in], refs[n_in:n_in + n_out]
        send_sems, recv_sems, loc_sems = refs[n_in + n_out:]
        remote, local = plan(in_refs, out_refs)
        lcopies = [pltpu.make_async_copy(s, d, loc_sems.at[k]) for k, (s, d) in enumerate(local)]
        rcopies = [pltpu.make_async_remote_copy(src_ref=s, dst_ref=d, send_sem=send_sems.at[k], recv_sem=recv_sems.at[k],
                                                device_id=dev, device_id_type=MESH_ID)
                   for k, (s, d, dev) in enumerate(remote)]
        for cp in lcopies + rcopies:
            cp.start()
        for cp in rcopies + lcopies:
            cp.wait()

    n_remote, n_local = plan.n_remote, plan.n_local
    return pl.pallas_call(
        body, name=name, in_specs=[_ANY] * n_in, out_specs=[_ANY] * n_out, out_shape=list(out_shapes),
        scratch_shapes=[pltpu.SemaphoreType.DMA((n_remote,)), pltpu.SemaphoreType.DMA((n_remote,)),
                        pltpu.SemaphoreType.DMA((max(n_local, 1),))],
    )(*ins)


def _gather_weights(shards, *, name):
    n_t = len(shards)

    def body(*refs):
        in_refs, out_refs = refs[:n_t], refs[n_t:2 * n_t]
        send_sems, recv_sems, loc_sems = refs[2 * n_t:]
        x, y, c = _me()
        chip = 2 * x + y
        peers = _chip_peers(x, y)
        lcopies, first, passed = [], [], []
        for t in range(n_t):
            half = in_refs[t].shape[0] // 2
            rows = pl.ds(c * half, half)
            lcopies.append(pltpu.make_async_copy(in_refs[t], out_refs[t].at[chip], loc_sems.at[t]))
            for k, (pchip, (px, py)) in enumerate(peers):
                first.append(pltpu.make_async_remote_copy(
                    src_ref=in_refs[t].at[rows], dst_ref=out_refs[t].at[chip, rows],
                    send_sem=send_sems.at[6 * t + k], recv_sem=recv_sems.at[6 * t + k],
                    device_id=(px, py, c), device_id_type=MESH_ID))
                passed.append(pltpu.make_async_remote_copy(
                    src_ref=out_refs[t].at[pchip, rows], dst_ref=out_refs[t].at[pchip, rows],
                    send_sem=send_sems.at[6 * t + 3 + k], recv_sem=recv_sems.at[6 * t + 3 + k],
                    device_id=(x, y, 1 - c), device_id_type=MESH_ID))
        for cp in lcopies + first:
            cp.start()
        for a, b in zip(first, passed):
            a.wait_recv()
            b.start()
        for cp in passed:
            cp.wait_recv()
        for cp in first + passed:
            cp.wait_send()
        for cp in lcopies:
            cp.wait()

    outs = [_sds((N_CHIPS,) + s.shape, s.dtype) for s in shards]
    return pl.pallas_call(
        body, name=name, in_specs=[_ANY] * n_t, out_specs=[_ANY] * n_t, out_shape=outs,
        scratch_shapes=[pltpu.SemaphoreType.DMA((6 * n_t,)), pltpu.SemaphoreType.DMA((6 * n_t,)),
                        pltpu.SemaphoreType.DMA((n_t,))],
    )(*shards)


class _Plan:
    def __init__(self, fn, n_remote, n_local):
        self.fn, self.n_remote, self.n_local = fn, n_remote, n_local

    def __call__(self, in_refs, out_refs):
        return self.fn(in_refs, out_refs)


def _reduce_grads(grads, *, name):
    n_t = len(grads)

    def plan1(in_refs, out_refs):
        x, y, c = _me()
        remote, local = [], []
        for t in range(n_t):
            half = in_refs[t].shape[1] // 2
            mine, theirs = pl.ds(c * half, half), pl.ds((1 - c) * half, half)
            remote.append((in_refs[t].at[:, theirs], out_refs[2 * t], (x, y, 1 - c)))
            local.append((in_refs[t].at[:, mine], out_refs[2 * t + 1]))
        return remote, local

    halves = [_sds((N_CHIPS, g.shape[1] // 2, g.shape[2]), F32) for g in grads]
    out1 = _exchange(name + "_sib", grads, [h for h in halves for _ in range(2)], _Plan(plan1, n_t, n_t))
    pair = [_add2(out1[2 * t + 1], out1[2 * t], name=f"{name}_pair{t}") for t in range(n_t)]

    def plan2(in_refs, out_refs):
        x, y, c = _me()
        chip = 2 * x + y
        remote, local = [], []
        for t in range(n_t):
            for pchip, (px, py) in _chip_peers(x, y):
                remote.append((in_refs[t].at[pchip], out_refs[t].at[chip], (px, py, c)))
            local.append((in_refs[t].at[chip], out_refs[t].at[chip]))
        return remote, local

    out2 = _exchange(name + "_ici", pair, halves, _Plan(plan2, 3 * n_t, n_t))
    part = [_sum_slots(out2[t], name=f"{name}_sum{t}") for t in range(n_t)]

    def plan3(in_refs, out_refs):
        x, y, c = _me()
        remote, local = [], []
        for t in range(n_t):
            half = in_refs[t].shape[0]
            mine = pl.ds(c * half, half)
            remote.append((in_refs[t], out_refs[t].at[mine], (x, y, 1 - c)))
            local.append((in_refs[t], out_refs[t].at[mine]))
        return remote, local

    fulls = [_sds((g.shape[1], g.shape[2]), F32) for g in grads]
    return _exchange(name + "_swap", part, fulls, _Plan(plan3, n_t, n_t))


def _allreduce_small(pack, *, name):
    r, w = pack.shape

    def body(p_ref, o_ref, buf, send_sems, recv_sems):
        x, y, c = _me()
        me = 4 * x + 2 * y + c
        buf[me] = p_ref[...]
        copies = []
        for k in range(1, N_DEV):
            peer = (x ^ ((k >> 2) & 1), y ^ ((k >> 1) & 1), c ^ (k & 1))
            copies.append(pltpu.make_async_remote_copy(
                src_ref=p_ref, dst_ref=buf.at[me], send_sem=send_sems.at[k - 1], recv_sem=recv_sems.at[k - 1],
                device_id=peer, device_id_type=MESH_ID))
        for cp in copies:
            cp.start()
        for cp in copies:
            cp.wait()
        acc = buf[0]
        for k in range(1, N_DEV):
            acc = acc + buf[k]
        o_ref[...] = acc

    vm = pl.BlockSpec(memory_space=pltpu.VMEM)
    return pl.pallas_call(
        body, name=name, in_specs=[vm], out_specs=vm, out_shape=_sds((r, w), F32),
        scratch_shapes=[pltpu.VMEM((N_DEV, r, w), F32), pltpu.SemaphoreType.DMA((N_DEV - 1,)),
                        pltpu.SemaphoreType.DMA((N_DEV - 1,))],
        compiler_params=_params(12 * r * w * 4),
    )(pack)


def _pack(arrs, row_multiple=16):
    flat = jnp.concatenate([a.reshape(-1).astype(F32) for a in arrs])
    unit = row_multiple * LANES
    total = -(-flat.shape[0] // unit) * unit
    return jnp.pad(flat, (0, total - flat.shape[0])).reshape(total // LANES, LANES)


def _unpack(packed, shapes):
    flat = packed.reshape(-1)
    out, off = [], 0
    for shp in shapes:
        n = int(np.prod(shp))
        out.append(flat[off:off + n].reshape(shp))
        off += n
    return out


def _pad_cols(a, width):
    return jnp.pad(a, [(0, 0)] * (a.ndim - 1) + [(0, width - a.shape[-1])])


ATTN_HEADS_PER_STEP = 2
SMALL_SHARDED = ("a_norm", "a_conv_w", "a_conv_b", "a_ln_g", "a_ln_b", "c_norm", "c_conv_w")
SMALL_REPLICATED = ("b_norm", "b_f_bias", "b_q_norm", "b_k_norm")
BIG = ("a_w_in", "a_w_out", "b_w_in", "b_w_out", "c_w_in", "c_w_out")
WEIGHTS = ("a_norm", "a_w_in", "a_conv_w", "a_conv_b", "a_ln_g", "a_ln_b", "a_w_out", "b_norm", "b_w_in", "b_f_bias",
           "b_q_norm", "b_k_norm", "b_w_out", "c_norm", "c_w_in", "c_conv_w", "c_w_out")


def _mixer_a_fwd(x, p, l, tag):
    h = _rms_fwd(x, p["a_norm"][l][None], name=f"{tag}_rms")
    proj = _mm_nn(h, p["a_w_in"][l], name=f"{tag}_in", tn=p["a_w_in"][l].shape[2])
    u1 = _a_conv_fwd(proj, p["a_conv_w"][l], p["a_conv_b"][l][None], name=f"{tag}_conv")
    gated = _a_post_fwd(u1, proj, p["a_ln_g"][l][None], p["a_ln_b"][l][None], name=f"{tag}_post")
    y = _mm_nn(gated, p["a_w_out"][l], name=f"{tag}_out", tn=1024, add=x)
    return y, (x, h, proj, u1, gated)


def _mixer_a_bwd(dx, saved, p, l, tag):
    x, h, proj, u1, gated = saved
    g = {}
    g["a_w_out"] = _mm_tn(gated, dx, name=f"{tag}_dwout", tk=2048, tn=1024)
    dgated = _mm_nt(dx, p["a_w_out"][l], name=f"{tag}_dgated", tn=2048, tk=1024)
    du1, dz, g["a_ln_g"], g["a_ln_b"], g["a_conv_b"] = _a_post_bwd(
        dgated, u1, proj, p["a_ln_g"][l][None], p["a_ln_b"][l][None], name=f"{tag}_dpost")
    dproj, g["a_conv_w"] = _a_conv_bwd(du1, proj, dz, p["a_conv_w"][l], name=f"{tag}_dconv")
    g["a_w_in"] = _mm_tn(h, dproj, name=f"{tag}_dwin", out_width=p["a_w_in"][l].shape[2])
    dh = _mm_nt(dproj, p["a_w_in"][l], name=f"{tag}_dh")
    dx, g["a_norm"] = _rms_bwd(x, p["a_norm"][l][None], [dh], dx, name=f"{tag}_drms")
    return dx, g


def _mixer_c_fwd(x, p, tag):
    h = _rms_fwd(x, p["c_norm"][0][None], name=f"{tag}_rms")
    proj = _mm_nn(h, p["c_w_in"], name=f"{tag}_in", tn=1024)
    gated = _c_mid_fwd(proj, p["c_conv_w"][0], name=f"{tag}_mid")
    y = _mm_nn(gated, p["c_w_out"], name=f"{tag}_out", tn=1024, add=x)
    return y, (x, h, proj, gated)


def _mixer_c_bwd(dx, saved, p, tag):
    x, h, proj, gated = saved
    g = {}
    g["c_w_out"] = _mm_tn(gated, dx, name=f"{tag}_dwout", tk=2048, tn=1024)
    dgated = _mm_nt(dx, p["c_w_out"], name=f"{tag}_dgated", tn=2048, tk=1024)
    dproj, g["c_conv_w"] = _c_mid_bwd(dgated, proj, p["c_conv_w"][0], name=f"{tag}_dmid")
    g["c_w_in"] = _mm_tn(h, dproj, name=f"{tag}_dwin", out_width=p["c_w_in"].shape[2], tn=1024)
    dh = _mm_nt(dproj, p["c_w_in"], name=f"{tag}_dh", tk=1024)
    dx, g["c_norm"] = _rms_bwd(x, p["c_norm"][0][None], [dh], dx, name=f"{tag}_drms")
    return dx, g


def _mixer_b_fwd(x, p, tag):
    hg = ATTN_HEADS_PER_STEP
    s_len = x.shape[0]
    n_heads = p["b_f_bias"].shape[1]
    h = _rms_fwd(x, p["b_norm"], name=f"{tag}_rms")
    proj = _mm_nn(h, p["b_wq"], name=f"{tag}_in", tn=1024)
    fl = _mm_nn(h, p["b_wf"], name=f"{tag}_inf", tn=LANES)
    qn, kn, vb = _b_qk_fwd(proj, p["b_q_norm"], p["b_k_norm"], name=f"{tag}_qk")
    bias = _pad_cols(p["b_f_bias"], LANES)
    c = _b_cumsum(fl, bias, name=f"{tag}_cumsum")
    ch = c[:, :n_heads]
    c_col = _pad_cols(ch.reshape(s_len, n_heads // hg, hg), LANES).reshape(s_len, (n_heads // hg) * LANES)
    c_row = ch.T.reshape(n_heads, 1, s_len)
    o, o_lo, lse = _fox_fwd(qn, kn, vb, c_col, c_row, name=f"{tag}_attn", hg=hg)
    gated = _b_gate_fwd(o, proj, name=f"{tag}_gate")
    y = _mm_nn(gated, p["b_w_out"], name=f"{tag}_out", tn=1024, add=x)
    return y, (x, h, proj, fl, bias, qn, kn, vb, c_col, c_row, o, o_lo, lse, gated)


def _mixer_b_bwd(dx, saved, p, tag):
    hg = ATTN_HEADS_PER_STEP
    x, h, proj, fl, bias, qn, kn, vb, c_col, c_row, o, o_lo, lse, gated = saved
    s_len = x.shape[0]
    n_heads = p["b_f_bias"].shape[1]
    g = {}
    g["b_w_out"] = _mm_tn(gated, dx, name=f"{tag}_dwout", tk=2048, tn=1024)
    dgated = _mm_nt(dx, p["b_w_out"], name=f"{tag}_dgated", tn=2048, tk=1024)
    do, dz, delta = _b_gate_bwd(dgated, o, o_lo, proj, name=f"{tag}_dgate", hg=hg)
    dqn, dkn, dv, dc = _fox_bwd(qn, kn, vb, do, lse, delta, c_col, c_row, name=f"{tag}_dattn", hg=hg)
    dc_pad = _pad_cols(dc.reshape(n_heads, s_len).T, LANES)
    dfl, dbias = _b_cumsum_bwd(dc_pad, fl, bias, name=f"{tag}_dcumsum")
    g["b_f_bias"] = dbias[:, :n_heads]
    dproj, g["b_q_norm"], g["b_k_norm"] = _b_qk_bwd(dqn, dkn, dv, dz, proj, p["b_q_norm"], p["b_k_norm"], name=f"{tag}_dqk")
    dwq = _mm_tn(h, dproj, name=f"{tag}_dwin", tn=1024)
    dwf = _mm_tn(h, dfl, name=f"{tag}_dwinf", tn=LANES)
    g["b_w_in"] = jnp.concatenate([dwq, dwf[:, :n_heads]], axis=1)
    dh = _mm_nt(dproj, p["b_wq"], name=f"{tag}_dh", tk=1024)
    dhf = _mm_nt(dfl, p["b_wf"], name=f"{tag}_dhf", tk=LANES)
    dx, g["b_norm"] = _rms_bwd(x, p["b_norm"], [dh, dhf], dx, name=f"{tag}_drms")
    return dx, g


def kernel(x, a_norm, a_w_in, a_conv_w, a_conv_b, a_ln_g, a_ln_b, a_w_out, b_norm, b_w_in, b_f_bias, b_q_norm, b_k_norm, b_w_out, c_norm, c_w_in, c_conv_w, c_w_out, loss_target, m_a_norm, m_a_w_in, m_a_conv_w, m_a_conv_b, m_a_ln_g, m_a_ln_b, m_a_w_out, m_b_norm, m_b_w_in, m_b_f_bias, m_b_q_norm, m_b_k_norm, m_b_w_out, m_c_norm, m_c_w_in, m_c_conv_w, m_c_w_out, v_a_norm, v_a_w_in, v_a_conv_w, v_a_conv_b, v_a_ln_g, v_a_ln_b, v_a_w_out, v_b_norm, v_b_w_in, v_b_f_bias, v_b_q_norm, v_b_k_norm, v_b_w_out, v_c_norm, v_c_w_in, v_c_conv_w, v_c_w_out):
    w_loc = dict(a_norm=a_norm, a_w_in=a_w_in, a_conv_w=a_conv_w, a_conv_b=a_conv_b, a_ln_g=a_ln_g, a_ln_b=a_ln_b,
                 a_w_out=a_w_out, b_norm=b_norm, b_w_in=b_w_in, b_f_bias=b_f_bias, b_q_norm=b_q_norm, b_k_norm=b_k_norm,
                 b_w_out=b_w_out, c_norm=c_norm, c_w_in=c_w_in, c_conv_w=c_conv_w, c_w_out=c_w_out)
    m_loc = dict(a_norm=m_a_norm, a_w_in=m_a_w_in, a_conv_w=m_a_conv_w, a_conv_b=m_a_conv_b, a_ln_g=m_a_ln_g,
                 a_ln_b=m_a_ln_b, a_w_out=m_a_w_out, b_norm=m_b_norm, b_w_in=m_b_w_in, b_f_bias=m_b_f_bias,
                 b_q_norm=m_b_q_norm, b_k_norm=m_b_k_norm, b_w_out=m_b_w_out, c_norm=m_c_norm, c_w_in=m_c_w_in,
                 c_conv_w=m_c_conv_w, c_w_out=m_c_w_out)
    v_loc = dict(a_norm=v_a_norm, a_w_in=v_a_w_in, a_conv_w=v_a_conv_w, a_conv_b=v_a_conv_b, a_ln_g=v_a_ln_g,
                 a_ln_b=v_a_ln_b, a_w_out=v_a_w_out, b_norm=v_b_norm, b_w_in=v_b_w_in, b_f_bias=v_b_f_bias,
                 b_q_norm=v_b_q_norm, b_k_norm=v_b_k_norm, b_w_out=v_b_w_out, c_norm=v_c_norm, c_w_in=v_c_w_in,
                 c_conv_w=v_c_conv_w, c_w_out=v_c_w_out)
    n_a = a_w_in.shape[0]
    d_model = x.shape[2]
    e_inner = a_w_out.shape[1] * N_CHIPS
    n_heads = b_f_bias.shape[1]
    nb_loc = b_w_in.shape[2]
    nb_pad = -(-nb_loc // LANES) * LANES
    chip = 2 * lax.axis_index("x") + lax.axis_index("y")

    big_shards = ([a_w_in[l].astype(BF16) for l in range(n_a)] + [a_w_out[l].astype(BF16) for l in range(n_a)]
                  + [_pad_cols(b_w_in[0], nb_pad).astype(BF16), b_w_out[0].astype(BF16), c_w_in[0].astype(BF16),
                     c_w_out[0].astype(BF16)])
    small_pack = _pack([w_loc[n] for n in SMALL_SHARDED])
    gathered = _gather_weights(big_shards + [small_pack], name="gather_weights")
    p = {}
    p["a_w_in"] = gathered[0:n_a]
    p["a_w_out"] = [g.reshape(e_inner, d_model) for g in gathered[n_a:2 * n_a]]
    gb, gbo, gci, gco, gsmall = gathered[2 * n_a:]
    wb_full = jnp.concatenate([gb[k, :, :nb_loc] for k in range(N_CHIPS)], axis=1)
    p["b_wq"] = wb_full[:, :4 * e_inner]
    p["b_wf"] = _pad_cols(wb_full[:, 4 * e_inner:], LANES)
    p["b_w_out"] = gbo.reshape(e_inner, d_model)
    p["c_w_in"] = gci
    p["c_w_out"] = gco.reshape(e_inner, d_model)
    small_shapes = [w_loc[n].shape for n in SMALL_SHARDED]
    per_chip = [_unpack(gsmall[k], small_shapes) for k in range(N_CHIPS)]
    for idx, n in enumerate(SMALL_SHARDED):
        p[n] = jnp.concatenate([per_chip[k][idx] for k in range(N_CHIPS)], axis=-1)
    for n in SMALL_REPLICATED:
        p[n] = w_loc[n]

    x0 = x[0]
    x1, sv0 = _mixer_a_fwd(x0, p, 0, "a0")
    x2, sv1 = _mixer_b_fwd(x1, p, "b0")
    x3, sv2 = _mixer_c_fwd(x2, p, "c0")
    x4, sv3 = _mixer_a_fwd(x3, p, 1, "a1")
    dy, loss_part = _loss_head(x4, loss_target[0], name="loss_head")
    loss = lax.psum(loss_part[0, 0], ("x", "y", "c"))

    dx, g3 = _mixer_a_bwd(dy, sv3, p, 1, "a1")
    dx, g2 = _mixer_c_bwd(dx, sv2, p, "c0")
    dx, g1 = _mixer_b_bwd(dx, sv1, p, "b0")
    dx, g0 = _mixer_a_bwd(dx, sv0, p, 0, "a0")
    grad_x = dx[None]

    half_rows = e_inner // N_CHIPS
    gb_full = g1["b_w_in"].reshape(d_model, N_CHIPS, nb_loc).transpose(1, 0, 2)
    big_grads = ([g0["a_w_in"], g3["a_w_in"]]
                 + [g0["a_w_out"].reshape(N_CHIPS, half_rows, d_model), g3["a_w_out"].reshape(N_CHIPS, half_rows, d_model)]
                 + [_pad_cols(gb_full, nb_pad), g1["b_w_out"].reshape(N_CHIPS, half_rows, d_model), g2["c_w_in"],
                    g2["c_w_out"].reshape(N_CHIPS, half_rows, d_model)])
    red = _reduce_grads(big_grads, name="reduce_grads")
    grads = {}
    grads["a_w_in"] = jnp.stack(red[0:2])
    grads["a_w_out"] = jnp.stack(red[2:4])
    grads["b_w_in"] = red[4][None, :, :nb_loc]
    grads["b_w_out"] = red[5][None]
    grads["c_w_in"] = red[6][None]
    grads["c_w_out"] = red[7][None]

    small_full = {}
    for n in ("a_norm", "a_conv_w", "a_conv_b", "a_ln_g", "a_ln_b"):
        small_full[n] = jnp.stack([g0[n].reshape(p[n].shape[1:]), g3[n].reshape(p[n].shape[1:])])
    small_full["c_norm"] = g2["c_norm"].reshape(p["c_norm"].shape)
    small_full["c_conv_w"] = g2["c_conv_w"].reshape(p["c_conv_w"].shape)
    for n in SMALL_REPLICATED:
        small_full[n] = g1[n].reshape(w_loc[n].shape)
    small_names = SMALL_SHARDED + SMALL_REPLICATED
    summed = _unpack(_allreduce_small(_pack([small_full[n] for n in small_names], 8), name="reduce_small"),
                     [small_full[n].shape for n in small_names])
    for n, s in zip(small_names, summed):
        if n in SMALL_SHARDED:
            width = w_loc[n].shape[-1]
            grads[n] = lax.dynamic_slice_in_dim(s, chip * width, width, axis=s.ndim - 1)
        else:
            grads[n] = s

    delta, new_m, new_v = {}, {}, {}
    for n in BIG:
        shp = w_loc[n].shape
        cols = nb_pad if n == "b_w_in" else shp[-1]
        prep = lambda a: _pad_cols(a, cols).reshape(-1, cols)
        d_, m_, v_ = _adamw(prep(w_loc[n]), prep(grads[n]), prep(m_loc[n]), prep(v_loc[n]), name=f"adamw_{n}")
        delta[n], new_m[n], new_v[n] = [a.reshape(shp[:-1] + (cols,))[..., :shp[-1]] for a in (d_, m_, v_)]
    small_shapes_all = [w_loc[n].shape for n in small_names]
    d_, m_, v_ = _adamw(_pack([w_loc[n] for n in small_names], 8), _pack([grads[n] for n in small_names], 8),
                        _pack([m_loc[n] for n in small_names], 8), _pack([v_loc[n] for n in small_names], 8),
                        name="adamw_small")
    for n, a, b, c_ in zip(small_names, _unpack(d_, small_shapes_all), _unpack(m_, small_shapes_all),
                           _unpack(v_, small_shapes_all)):
        delta[n], new_m[n], new_v[n] = a, b, c_

    return (loss, grad_x, *[grads[n] for n in WEIGHTS], *[delta[n] for n in WEIGHTS],
            *[new_m[n] for n in WEIGHTS], *[new_v[n] for n in WEIGHTS])
```

```python
import functools

import numpy as np
import jax
import jax.numpy as jnp
from jax import lax
from jax.experimental import pallas as pl
from jax.experimental.pallas import tpu as pltpu

F32 = jnp.float32
BF16 = jnp.bfloat16
NORM_EPS = 1e-6
HEAD_DIM = 128
LANES = 128
N_CHIPS = 4
N_DEV = 8
VMEM_CAP = 56 << 20
MESH_ID = pl.DeviceIdType.MESH

ADAM_LR = 0.001
ADAM_B1 = 0.9
ADAM_B2 = 0.999
ADAM_EPS = 1e-08
ADAM_WD = 0.01
ADAM_STEP = 10


def _sds(shape, dtype):
    return jax.ShapeDtypeStruct(tuple(shape), dtype)


def _tile(n, pref):
    if n <= pref:
        return n
    for t in range(pref - pref % 8, 0, -8):
        if n % t == 0:
            return t
    raise ValueError(f"no tile for {n} under {pref}")


def _params(vmem_bytes):
    return pltpu.CompilerParams(vmem_limit_bytes=int(min(max(vmem_bytes, 16 << 20), VMEM_CAP)))


def _sigmoid(v):
    return 1.0 / (1.0 + jnp.exp(-v))


def _silu(v):
    return v * _sigmoid(v)


def _dsilu(v):
    s = _sigmoid(v)
    return s * (1.0 + v * (1.0 - s))


def _rowsum8(v):
    r, c = v.shape
    return jnp.sum(v.reshape(r // 8, 8, c), axis=0)


def _col_spec(grouped, rows_block, tile, width, row_of, col_of):
    if grouped:
        per = width // tile
        return pl.BlockSpec((None, rows_block, tile), lambda *ids: (col_of(*ids) // per, row_of(*ids), col_of(*ids) % per))
    return pl.BlockSpec((rows_block, tile), lambda *ids: (row_of(*ids), col_of(*ids)))


def _mm_nn(a, b, *, name, tm=1024, tn=512, out_dtype=F32, add=None):
    m, k = a.shape
    b_grouped = b.ndim == 3
    n = b.shape[0] * b.shape[2] if b_grouped else b.shape[1]
    width = b.shape[2] if b_grouped else n
    tm, tn = _tile(m, tm), _tile(width, tn)

    def body(*refs):
        if add is None:
            a_ref, b_ref, o_ref = refs
        else:
            a_ref, b_ref, r_ref, o_ref = refs
        acc = jnp.dot(a_ref[...].astype(BF16), b_ref[...].astype(BF16), preferred_element_type=F32)
        if add is not None:
            acc = acc + r_ref[...]
        o_ref[...] = acc.astype(o_ref.dtype)

    in_specs = [pl.BlockSpec((tm, k), lambda i, j: (i, 0)),
                _col_spec(b_grouped, k, tn, width, lambda i, j: 0, lambda i, j: j)]
    args = [a, b]
    if add is not None:
        in_specs.append(pl.BlockSpec((tm, tn), lambda i, j: (i, j)))
        args.append(add)
    vmem = 2 * (tm * k * a.dtype.itemsize + k * tn * 2 + tm * tn * 4 * (2 if add is not None else 1)) + tm * tn * 8 + tm * k * 2
    return pl.pallas_call(
        body, name=name, grid=(m // tm, n // tn), in_specs=in_specs,
        out_specs=pl.BlockSpec((tm, tn), lambda i, j: (i, j)),
        out_shape=_sds((m, n), out_dtype), compiler_params=_params(vmem + (4 << 20)),
    )(*args)


def _mm_nt(a, b, *, name, tm=1024, tn=1024, tk=512):
    a_grouped, b_grouped = a.ndim == 3, b.ndim == 3
    m = a.shape[1] if a_grouped else a.shape[0]
    n = a.shape[0] * a.shape[2] if a_grouped else a.shape[1]
    kk = b.shape[1] if b_grouped else b.shape[0]
    wa = a.shape[2] if a_grouped else n
    wb = b.shape[2] if b_grouped else n
    tm, tn = _tile(m, tm), _tile(kk, tn)
    tk = _tile(int(np.gcd(wa, wb)), tk)
    steps = n // tk

    def body(a_ref, b_ref, o_ref):
        part = lax.dot_general(a_ref[...].astype(BF16), b_ref[...].astype(BF16), (((1,), (1,)), ((), ())),
                               preferred_element_type=F32)

        @pl.when(pl.program_id(2) == 0)
        def _():
            o_ref[...] = part

        @pl.when(pl.program_id(2) > 0)
        def _():
            o_ref[...] += part

    in_specs = [_col_spec(a_grouped, tm, tk, wa, lambda i, j, s: i, lambda i, j, s: s),
                _col_spec(b_grouped, tn, tk, wb, lambda i, j, s: j, lambda i, j, s: s)]
    vmem = 2 * (tm * tk * a.dtype.itemsize + tn * tk * b.dtype.itemsize + tm * tn * 4) + tm * tn * 4 + (tm + tn) * tk * 2
    return pl.pallas_call(
        body, name=name, grid=(m // tm, kk // tn, steps), in_specs=in_specs,
        out_specs=pl.BlockSpec((tm, tn), lambda i, j, s: (i, j)),
        out_shape=_sds((m, kk), F32), compiler_params=_params(vmem + (4 << 20)),
    )(a, b)


def _mm_tn(a, b, *, name, out_width=None, tk=1024, tn=512, ts=512):
    s_len, k = a.shape
    b_grouped = b.ndim == 3
    n = b.shape[0] * b.shape[2] if b_grouped else b.shape[1]
    wb = b.shape[2] if b_grouped else n
    wo = out_width if out_width is not None else n
    tk, ts = _tile(k, tk), _tile(s_len, ts)
    tn = _tile(int(np.gcd(wb, wo)), tn)

    def body(a_ref, b_ref, o_ref):
        part = lax.dot_general(a_ref[...].astype(BF16), b_ref[...].astype(BF16), (((0,), (0,)), ((), ())),
                               preferred_element_type=F32)

        @pl.when(pl.program_id(2) == 0)
        def _():
            o_ref[...] = part

        @pl.when(pl.program_id(2) > 0)
        def _():
            o_ref[...] += part

    in_specs = [pl.BlockSpec((ts, tk), lambda i, j, s: (s, i)),
                _col_spec(b_grouped, ts, tn, wb, lambda i, j, s: s, lambda i, j, s: j)]
    out_grouped = out_width is not None
    out_spec = _col_spec(out_grouped, tk, tn, wo, lambda i, j, s: i, lambda i, j, s: j)
    out_shape = _sds((n // wo, k, wo), F32) if out_grouped else _sds((k, n), F32)
    vmem = 2 * (ts * tk * a.dtype.itemsize + ts * tn * b.dtype.itemsize + tk * tn * 4) + tk * tn * 4 + ts * (tk + tn) * 4
    return pl.pallas_call(
        body, name=name, grid=(k // tk, n // tn, s_len // ts), in_specs=in_specs, out_specs=out_spec,
        out_shape=out_shape, compiler_params=_params(vmem + (4 << 20)),
    )(a, b)


def _rms_fwd(x, g, *, name, ts=512):
    s_len, d = x.shape
    ts = _tile(s_len, ts)

    def body(x_ref, g_ref, h_ref):
        xf = x_ref[...]
        r = lax.rsqrt(jnp.mean(xf * xf, axis=-1, keepdims=True) + NORM_EPS)
        h_ref[...] = ((xf * r) * g_ref[...]).astype(h_ref.dtype)

    return pl.pallas_call(
        body, name=name, grid=(s_len // ts,),
        in_specs=[pl.BlockSpec((ts, d), lambda i: (i, 0)), pl.BlockSpec((1, d), lambda i: (0, 0))],
        out_specs=pl.BlockSpec((ts, d), lambda i: (i, 0)), out_shape=_sds((s_len, d), BF16),
        compiler_params=_params(8 * ts * d * 4),
    )(x, g)


def _rms_bwd(x, g, dhs, dres, *, name, ts=512):
    s_len, d = x.shape
    ts = _tile(s_len, ts)
    n_dh = len(dhs)
    last = s_len // ts - 1

    def body(*refs):
        x_ref, g_ref = refs[0], refs[1]
        dh_refs = refs[2:2 + n_dh]
        dres_ref, dx_ref, dg_ref, acc = refs[2 + n_dh:]
        i = pl.program_id(0)

        @pl.when(i == 0)
        def _():
            acc[...] = jnp.zeros_like(acc)

        xf = x_ref[...]
        dy = dh_refs[0][...]
        for r_ in dh_refs[1:]:
            dy = dy + r_[...]
        r = lax.rsqrt(jnp.mean(xf * xf, axis=-1, keepdims=True) + NORM_EPS)
        gd = dy * g_ref[...]
        dot = jnp.mean(xf * gd, axis=-1, keepdims=True)
        dx_ref[...] = dres_ref[...] + r * gd - xf * (r * r * r * dot)
        acc[...] += _rowsum8(dy * (xf * r))

        @pl.when(i == last)
        def _():
            dg_ref[...] = jnp.sum(acc[...], axis=0, keepdims=True)

    row = pl.BlockSpec((ts, d), lambda i: (i, 0))
    vec = pl.BlockSpec((1, d), lambda i: (0, 0))
    return pl.pallas_call(
        body, name=name, grid=(s_len // ts,),
        in_specs=[row, vec] + [row] * n_dh + [row],
        out_specs=[row, vec], out_shape=[_sds((s_len, d), F32), _sds((1, d), F32)],
        scratch_shapes=[pltpu.VMEM((8, d), F32)],
        compiler_params=_params((2 * (3 + n_dh) + 6) * ts * d * 4),
    )(x, g, *dhs, dres)


def _loss_head(y, target, *, name, ts=512):
    s_len, d = y.shape
    ts = _tile(s_len, ts)
    last = s_len // ts - 1

    def body(y_ref, t_ref, dy_ref, loss_ref, acc):
        i = pl.program_id(0)

        @pl.when(i == 0)
        def _():
            acc[...] = jnp.zeros_like(acc)

        err = y_ref[...] - t_ref[...]
        dy_ref[...] = err / d
        acc[...] += _rowsum8(err * err)

        @pl.when(i == last)
        def _():
            loss_ref[...] = (0.5 * jnp.sum(acc[...]) / d).reshape(1, 1)

    row = pl.BlockSpec((ts, d), lambda i: (i, 0))
    return pl.pallas_call(
        body, name=name, grid=(s_len // ts,), in_specs=[row, row],
        out_specs=[row, pl.BlockSpec((1, 1), lambda i: (0, 0))],
        out_shape=[_sds((s_len, d), F32), _sds((1, 1), F32)],
        scratch_shapes=[pltpu.VMEM((8, d), F32)],
        compiler_params=_params(10 * ts * d * 4),
    )(y, target)


CONV_ROWS = 64
CONV_COLS = 256


def _conv_taps(buf, w_ref, k_width, base, ts, tc, init, emit, reverse=False):
    cw = min(tc, CONV_COLS)
    rw = min(ts, CONV_ROWS)
    for cb in range(tc // cw):
        cs = slice(cb * cw, (cb + 1) * cw)
        for rb in range(ts // rw):
            acc = init(slice(rb * rw, (rb + 1) * rw), cs, (rw, cw))
            for k in range(k_width):
                sh = (k_width - 1 - k) if reverse else k
                acc = acc + w_ref[k:k + 1, cs] * buf[base + rb * rw + sh:base + rb * rw + sh + rw, cs]
            emit(slice(rb * rw, (rb + 1) * rw), cs, acc)


def _conv_wgrad(buf, d_ref_val, acc_ref, k_width, base, ts, tc):
    cw = min(tc, CONV_COLS)
    rw = min(ts, CONV_ROWS)
    for cb in range(tc // cw):
        cs = slice(cb * cw, (cb + 1) * cw)
        for rb in range(ts // rw):
            dv = d_ref_val[rb * rw:(rb + 1) * rw, cs]
            for k in range(k_width):
                prod = dv * buf[base + rb * rw + k:base + rb * rw + k + rw, cs]
                acc_ref[8 * k:8 * k + 8, cs] += _rowsum8(prod)


A_HALO = 32


def _a_conv_fwd(proj, conv_w, conv_b, *, name, ts=256, tc=512):
    s_len, e3 = proj.shape
    e = e3 // 3
    k_width = conv_w.shape[0]
    ts, tc = _tile(s_len, ts), _tile(e, tc)
    nc = e // tc
    kp = 32

    def body(val, gate, valh, gateh, w_ref, b_ref, u1_ref, buf):
        i = pl.program_id(1)
        u0h = valh[...] * _sigmoid(gateh[...])
        buf[0:A_HALO, :] = jnp.where(i > 0, u0h, 0.0)
        buf[A_HALO:A_HALO + ts, :] = val[...] * _sigmoid(gate[...])

        def init(rows, cs, shape):
            return jnp.broadcast_to(b_ref[:, cs], shape)

        def emit(rows, cs, acc):
            u1_ref[rows, cs] = acc

        _conv_taps(buf, w_ref, k_width, A_HALO - (k_width - 1), ts, tc, init, emit)

    hb = ts // A_HALO
    in_specs = [
        pl.BlockSpec((ts, tc), lambda j, i: (i, j)),
        pl.BlockSpec((ts, tc), lambda j, i: (i, nc + j)),
        pl.BlockSpec((A_HALO, tc), lambda j, i: (jnp.maximum(i * hb - 1, 0), j)),
        pl.BlockSpec((A_HALO, tc), lambda j, i: (jnp.maximum(i * hb - 1, 0), nc + j)),
        pl.BlockSpec((kp, tc), lambda j, i: (0, j)),
        pl.BlockSpec((1, tc), lambda j, i: (0, j)),
    ]
    w_pad = jnp.zeros((kp, e), F32).at[:k_width].set(conv_w)
    return pl.pallas_call(
        body, name=name, grid=(nc, s_len // ts), in_specs=in_specs,
        out_specs=pl.BlockSpec((ts, tc), lambda j, i: (i, j)), out_shape=_sds((s_len, e), F32),
        scratch_shapes=[pltpu.VMEM((A_HALO + ts, tc), F32)],
        compiler_params=_params(12 * ts * tc * 4),
    )(proj, proj, proj, proj, w_pad, conv_b)


def _ln_rows(u1, g, b):
    mu = jnp.mean(u1, axis=-1, keepdims=True)
    xc = u1 - mu
    var = jnp.mean(xc * xc, axis=-1, keepdims=True)
    rstd = lax.rsqrt(var + NORM_EPS)
    xhat = xc * rstd
    return xhat, rstd, xhat * g + b


def _a_post_fwd(u1, proj, ln_g, ln_b, *, name, ts=256):
    s_len, e = u1.shape
    ts = _tile(s_len, ts)

    def body(u1_ref, z_ref, g_ref, b_ref, o_ref):
        _, _, u2 = _ln_rows(u1_ref[...], g_ref[...], b_ref[...])
        o_ref[...] = (_silu(u2) * _silu(z_ref[...])).astype(o_ref.dtype)

    row = pl.BlockSpec((ts, e), lambda i: (i, 0))
    vec = pl.BlockSpec((1, e), lambda i: (0, 0))
    return pl.pallas_call(
        body, name=name, grid=(s_len // ts,),
        in_specs=[row, pl.BlockSpec((ts, e), lambda i: (i, 2)), vec, vec],
        out_specs=row, out_shape=_sds((s_len, e), BF16), compiler_params=_params(12 * ts * e * 4),
    )(u1, proj, ln_g, ln_b)


def _a_post_bwd(dgated, u1, proj, ln_g, ln_b, *, name, ts=256):
    s_len, e = u1.shape
    ts = _tile(s_len, ts)
    last = s_len // ts - 1

    def body(dg_ref, u1_ref, z_ref, g_ref, b_ref, du1_ref, dz_ref, dlg_ref, dlb_ref, dcb_ref, a_g, a_b, a_c):
        i = pl.program_id(0)

        @pl.when(i == 0)
        def _():
            a_g[...] = jnp.zeros_like(a_g)
            a_b[...] = jnp.zeros_like(a_b)
            a_c[...] = jnp.zeros_like(a_c)

        g = g_ref[...]
        xhat, rstd, u2 = _ln_rows(u1_ref[...], g, b_ref[...])
        z = z_ref[...]
        dgt = dg_ref[...]
        dz_ref[...] = (dgt * _silu(u2) * _dsilu(z)).astype(dz_ref.dtype)
        du2 = dgt * _silu(z) * _dsilu(u2)
        a_g[...] += _rowsum8(du2 * xhat)
        a_b[...] += _rowsum8(du2)
        dxh = du2 * g
        m1 = jnp.mean(dxh, axis=-1, keepdims=True)
        m2 = jnp.mean(dxh * xhat, axis=-1, keepdims=True)
        du1 = rstd * (dxh - m1 - xhat * m2)
        du1_ref[...] = du1
        a_c[...] += _rowsum8(du1)

        @pl.when(i == last)
        def _():
            dlg_ref[...] = jnp.sum(a_g[...], axis=0, keepdims=True)
            dlb_ref[...] = jnp.sum(a_b[...], axis=0, keepdims=True)
            dcb_ref[...] = jnp.sum(a_c[...], axis=0, keepdims=True)

    row = pl.BlockSpec((ts, e), lambda i: (i, 0))
    vec = pl.BlockSpec((1, e), lambda i: (0, 0))
    return pl.pallas_call(
        body, name=name, grid=(s_len // ts,),
        in_specs=[row, row, pl.BlockSpec((ts, e), lambda i: (i, 2)), vec, vec],
        out_specs=[row, row, vec, vec, vec],
        out_shape=[_sds((s_len, e), F32), _sds((s_len, e), BF16), _sds((1, e), F32), _sds((1, e), F32), _sds((1, e), F32)],
        scratch_shapes=[pltpu.VMEM((8, e), F32)] * 3,
        compiler_params=_params(20 * ts * e * 4),
    )(dgated, u1, proj, ln_g, ln_b)


def _a_conv_bwd(du1, proj, dz, conv_w, *, name, ts=256, tc=512):
    s_len, e3 = proj.shape
    e = e3 // 3
    k_width = conv_w.shape[0]
    ts, tc = _tile(s_len, ts), _tile(e, tc)
    nc, nr = e // tc, s_len // ts
    kp = 32
    hb = ts // A_HALO

    def body(val, gate, valh, gateh, d_ref, dh_ref, dz_ref, w_ref, dp_ref, dw_ref, buf_u, buf_d, du0, acc):
        i = pl.program_id(1)

        @pl.when(i == 0)
        def _():
            acc[...] = jnp.zeros_like(acc)

        v = val[...]
        sg = _sigmoid(gate[...])
        buf_u[0:A_HALO, :] = jnp.where(i > 0, valh[...] * _sigmoid(gateh[...]), 0.0)
        buf_u[A_HALO:A_HALO + ts, :] = v * sg
        buf_d[0:ts, :] = d_ref[...]
        buf_d[ts:ts + A_HALO, :] = jnp.where(i < nr - 1, dh_ref[...], 0.0)

        def init(rows, cs, shape):
            return jnp.zeros(shape, F32)

        def emit(rows, cs, a):
            du0[rows, cs] = a

        _conv_taps(buf_d, w_ref, k_width, 0, ts, tc, init, emit, reverse=True)
        _conv_wgrad(buf_u, buf_d, acc, k_width, A_HALO - (k_width - 1), ts, tc)
        d0 = du0[...]
        dp_ref[0] = (d0 * sg).astype(dp_ref.dtype)
        dp_ref[1] = (d0 * v * sg * (1.0 - sg)).astype(dp_ref.dtype)
        dp_ref[2] = dz_ref[...]

        @pl.when(i == nr - 1)
        def _():
            for k in range(kp):
                dw_ref[k:k + 1, :] = jnp.sum(acc[8 * k:8 * k + 8, :], axis=0, keepdims=True)

    in_specs = [
        pl.BlockSpec((ts, tc), lambda j, i: (i, j)),
        pl.BlockSpec((ts, tc), lambda j, i: (i, nc + j)),
        pl.BlockSpec((A_HALO, tc), lambda j, i: (jnp.maximum(i * hb - 1, 0), j)),
        pl.BlockSpec((A_HALO, tc), lambda j, i: (jnp.maximum(i * hb - 1, 0), nc + j)),
        pl.BlockSpec((ts, tc), lambda j, i: (i, j)),
        pl.BlockSpec((A_HALO, tc), lambda j, i: (jnp.minimum((i + 1) * hb, nr * hb - 1), j)),
        pl.BlockSpec((ts, tc), lambda j, i: (i, j)),
        pl.BlockSpec((kp, tc), lambda j, i: (0, j)),
    ]
    w_pad = jnp.zeros((kp, e), F32).at[:k_width].set(conv_w)
    dproj, dw = pl.pallas_call(
        body, name=name, grid=(nc, nr), in_specs=in_specs,
        out_specs=[pl.BlockSpec((3, ts, tc), lambda j, i: (0, i, j)), pl.BlockSpec((kp, tc), lambda j, i: (0, j))],
        out_shape=[_sds((3, s_len, e), BF16), _sds((kp, e), F32)],
        scratch_shapes=[pltpu.VMEM((A_HALO + ts, tc), F32), pltpu.VMEM((ts + A_HALO, tc), F32),
                        pltpu.VMEM((ts, tc), F32), pltpu.VMEM((8 * kp, tc), F32)],
        compiler_params=_params(24 * ts * tc * 4),
    )(proj, proj, proj, proj, du1, du1, dz, w_pad)
    return dproj, dw[:k_width]


C_HALO = 8


def _c_mid_fwd(proj, conv_w, *, name, ts=256, tc=512):
    s_len, e4 = proj.shape
    e = e4 // 4
    k_width = conv_w.shape[0]
    ts, tc = _tile(s_len, ts), _tile(e, tc)
    nc = e // tc
    hb = ts // C_HALO

    def body(u, bg, cg, z, uh, cgh, w_ref, o_ref, buf, y):
        i = pl.program_id(1)
        buf[0:C_HALO, :] = jnp.where(i > 0, uh[...] * cgh[...], 0.0)
        buf[C_HALO:C_HALO + ts, :] = u[...] * cg[...]

        def init(rows, cs, shape):
            return jnp.zeros(shape, F32)

        def emit(rows, cs, a):
            y[rows, cs] = a

        _conv_taps(buf, w_ref, k_width, C_HALO - (k_width - 1), ts, tc, init, emit)
        o_ref[...] = (bg[...] * y[...] * _silu(z[...])).astype(o_ref.dtype)

    def grp(g):
        return pl.BlockSpec((ts, tc), lambda j, i: (i, g * nc + j))

    def halo(g):
        return pl.BlockSpec((C_HALO, tc), lambda j, i: (jnp.maximum(i * hb - 1, 0), g * nc + j))

    w_pad = jnp.zeros((8, e), F32).at[:k_width].set(conv_w)
    return pl.pallas_call(
        body, name=name, grid=(nc, s_len // ts),
        in_specs=[grp(0), grp(1), grp(2), grp(3), halo(0), halo(2), pl.BlockSpec((8, tc), lambda j, i: (0, j))],
        out_specs=pl.BlockSpec((ts, tc), lambda j, i: (i, j)), out_shape=_sds((s_len, e), BF16),
        scratch_shapes=[pltpu.VMEM((C_HALO + ts, tc), F32), pltpu.VMEM((ts, tc), F32)],
        compiler_params=_params(16 * ts * tc * 4),
    )(proj, proj, proj, proj, proj, proj, w_pad)


def _c_mid_bwd(dgated, proj, conv_w, *, name, ts=256, tc=512):
    s_len, e4 = proj.shape
    e = e4 // 4
    k_width = conv_w.shape[0]
    ts, tc = _tile(s_len, ts), _tile(e, tc)
    nc, nr = e // tc, s_len // ts
    hb = ts // C_HALO

    def body(u, bg, cg, z, uh, cgh, dg, dgh, bgh, zh, w_ref, dp_ref, dw_ref, buf_p, buf_d, y, dpv, acc):
        i = pl.program_id(1)

        @pl.when(i == 0)
        def _():
            acc[...] = jnp.zeros_like(acc)

        uv, bgv, cgv, zv, dgv = u[...], bg[...], cg[...], z[...], dg[...]
        buf_p[0:C_HALO, :] = jnp.where(i > 0, uh[...] * cgh[...], 0.0)
        buf_p[C_HALO:C_HALO + ts, :] = uv * cgv
        sz = _silu(zv)
        buf_d[0:ts, :] = dgv * sz * bgv
        buf_d[ts:ts + C_HALO, :] = jnp.where(i < nr - 1, dgh[...] * _silu(zh[...]) * bgh[...], 0.0)

        def init(rows, cs, shape):
            return jnp.zeros(shape, F32)

        def emit_y(rows, cs, a):
            y[rows, cs] = a

        def emit_dp(rows, cs, a):
            dpv[rows, cs] = a

        _conv_taps(buf_p, w_ref, k_width, C_HALO - (k_width - 1), ts, tc, init, emit_y)
        _conv_taps(buf_d, w_ref, k_width, 0, ts, tc, init, emit_dp, reverse=True)
        _conv_wgrad(buf_p, buf_d, acc, k_width, C_HALO - (k_width - 1), ts, tc)
        yv, dp = y[...], dpv[...]
        dp_ref[0] = (dp * cgv).astype(dp_ref.dtype)
        dp_ref[1] = (dgv * sz * yv).astype(dp_ref.dtype)
        dp_ref[2] = (dp * uv).astype(dp_ref.dtype)
        dp_ref[3] = (dgv * bgv * yv * _dsilu(zv)).astype(dp_ref.dtype)

        @pl.when(i == nr - 1)
        def _():
            for k in range(8):
                dw_ref[k:k + 1, :] = jnp.sum(acc[8 * k:8 * k + 8, :], axis=0, keepdims=True)

    def grp(g):
        return pl.BlockSpec((ts, tc), lambda j, i: (i, g * nc + j))

    def prev(g):
        return pl.BlockSpec((C_HALO, tc), lambda j, i: (jnp.maximum(i * hb - 1, 0), g * nc + j))

    def nxt(g):
        return pl.BlockSpec((C_HALO, tc), lambda j, i: (jnp.minimum((i + 1) * hb, nr * hb - 1), g * nc + j))

    w_pad = jnp.zeros((8, e), F32).at[:k_width].set(conv_w)
    dproj, dw = pl.pallas_call(
        body, name=name, grid=(nc, nr),
        in_specs=[grp(0), grp(1), grp(2), grp(3), prev(0), prev(2),
                  pl.BlockSpec((ts, tc), lambda j, i: (i, j)),
                  pl.BlockSpec((C_HALO, tc), lambda j, i: (jnp.minimum((i + 1) * hb, nr * hb - 1), j)),
                  nxt(1), nxt(3), pl.BlockSpec((8, tc), lambda j, i: (0, j))],
        out_specs=[pl.BlockSpec((4, ts, tc), lambda j, i: (0, i, j)), pl.BlockSpec((8, tc), lambda j, i: (0, j))],
        out_shape=[_sds((4, s_len, e), BF16), _sds((8, e), F32)],
        scratch_shapes=[pltpu.VMEM((C_HALO + ts, tc), F32), pltpu.VMEM((ts + C_HALO, tc), F32),
                        pltpu.VMEM((ts, tc), F32), pltpu.VMEM((ts, tc), F32), pltpu.VMEM((64, tc), F32)],
        compiler_params=_params(32 * ts * tc * 4),
    )(proj, proj, proj, proj, proj, proj, dgated, dgated, proj, proj, w_pad)
    return dproj, dw[:k_width]


def _head_rms(xv, g):
    r = lax.rsqrt(jnp.mean(xv * xv, axis=-1, keepdims=True) + NORM_EPS)
    return r, xv * r * g


def _b_qk_fwd(proj, gq, gk, *, name, ts=512, tc=512):
    s_len, e4 = proj.shape
    e = e4 // 4
    ts, tc = _tile(s_len, ts), _tile(e, tc)
    nc = e // tc

    def body(q, k, v, gq_ref, gk_ref, qn, kn, vb):
        for h in range(tc // HEAD_DIM):
            cs = slice(h * HEAD_DIM, (h + 1) * HEAD_DIM)
            qn[:, cs] = _head_rms(q[:, cs], gq_ref[...])[1].astype(qn.dtype)
            kn[:, cs] = _head_rms(k[:, cs], gk_ref[...])[1].astype(kn.dtype)
        vb[...] = v[...].astype(vb.dtype)

    def grp(g):
        return pl.BlockSpec((ts, tc), lambda i, j: (i, g * nc + j))

    vec = pl.BlockSpec((1, HEAD_DIM), lambda i, j: (0, 0))
    out = pl.BlockSpec((ts, tc), lambda i, j: (i, j))
    return pl.pallas_call(
        body, name=name, grid=(s_len // ts, nc), in_specs=[grp(0), grp(1), grp(2), vec, vec],
        out_specs=[out, out, out], out_shape=[_sds((s_len, e), BF16)] * 3,
        compiler_params=_params(16 * ts * tc * 4),
    )(proj, proj, proj, gq, gk)


def _b_qk_bwd(dqn, dkn, dv, dz, proj, gq, gk, *, name, ts=512, tc=512):
    s_len, e4 = proj.shape
    e = e4 // 4
    ts, tc = _tile(s_len, ts), _tile(e, tc)
    nc, nr = e // tc, s_len // ts

    def body(dq_ref, dk_ref, dv_ref, dz_ref, q, k, gq_ref, gk_ref, dp_ref, dgq_ref, dgk_ref, a_q, a_k):
        i, j = pl.program_id(0), pl.program_id(1)

        @pl.when((i == 0) & (j == 0))
        def _():
            a_q[...] = jnp.zeros_like(a_q)
            a_k[...] = jnp.zeros_like(a_k)

        for h in range(tc // HEAD_DIM):
            cs = slice(h * HEAD_DIM, (h + 1) * HEAD_DIM)
            for slot, src, d_ref, g_ref, acc in ((0, q, dq_ref, gq_ref, a_q), (1, k, dk_ref, gk_ref, a_k)):
                xv = src[:, cs]
                dy = d_ref[:, cs]
                r = lax.rsqrt(jnp.mean(xv * xv, axis=-1, keepdims=True) + NORM_EPS)
                gd = dy * g_ref[...]
                dot = jnp.mean(xv * gd, axis=-1, keepdims=True)
                dp_ref[slot, :, cs] = (r * gd - xv * (r * r * r * dot)).astype(dp_ref.dtype)
                acc[...] += _rowsum8(dy * (xv * r))
        dp_ref[2] = dv_ref[...].astype(dp_ref.dtype)
        dp_ref[3] = dz_ref[...]

        @pl.when((i == nr - 1) & (j == nc - 1))
        def _():
            dgq_ref[...] = jnp.sum(a_q[...], axis=0, keepdims=True)
            dgk_ref[...] = jnp.sum(a_k[...], axis=0, keepdims=True)

    blk = pl.BlockSpec((ts, tc), lambda i, j: (i, j))
    vec = pl.BlockSpec((1, HEAD_DIM), lambda i, j: (0, 0))

    def grp(g):
        return pl.BlockSpec((ts, tc), lambda i, j: (i, g * nc + j))

    return pl.pallas_call(
        body, name=name, grid=(nr, nc), in_specs=[blk, blk, blk, blk, grp(0), grp(1), vec, vec],
        out_specs=[pl.BlockSpec((4, ts, tc), lambda i, j: (0, i, j)), vec, vec],
        out_shape=[_sds((4, s_len, e), BF16), _sds((1, HEAD_DIM), F32), _sds((1, HEAD_DIM), F32)],
        scratch_shapes=[pltpu.VMEM((8, HEAD_DIM), F32)] * 2,
        compiler_params=_params(24 * ts * tc * 4),
    )(dqn, dkn, dv, dz, proj, proj, gq, gk)


def _log_sigmoid(x):
    y = jnp.exp(-jnp.abs(x))
    u = 1.0 + y
    log1p = jnp.where(u == 1.0, y, jnp.log(u) * (y / jnp.where(u == 1.0, 1.0, u - 1.0)))
    return jnp.minimum(x, 0.0) - log1p


def _split3(v):
    hi = v.astype(BF16)
    r1 = v - hi.astype(F32)
    mid = r1.astype(BF16)
    lo = (r1 - mid.astype(F32)).astype(BF16)
    return hi, mid, lo


def _tri_matmul(tri, v):
    hi, mid, lo = _split3(v)
    return (jnp.dot(tri, hi, preferred_element_type=F32) + jnp.dot(tri, mid, preferred_element_type=F32)
            + jnp.dot(tri, lo, preferred_element_type=F32))


def _b_cumsum(fl, bias, *, name, t=512):
    s_len, w = fl.shape
    t = _tile(s_len, t)

    def body(fl_ref, b_ref, c_ref, carry):
        @pl.when(pl.program_id(0) == 0)
        def _():
            carry[...] = jnp.zeros_like(carry)

        logf = _log_sigmoid(fl_ref[...] + b_ref[...])
        row = lax.broadcasted_iota(jnp.int32, (t, t), 0)
        col = lax.broadcasted_iota(jnp.int32, (t, t), 1)
        tri = jnp.where(col <= row, 1.0, 0.0).astype(BF16)
        c = _tri_matmul(tri, logf) + carry[...]
        c_ref[...] = c
        carry[...] = c[t - 1:t, :]

    return pl.pallas_call(
        body, name=name, grid=(s_len // t,),
        in_specs=[pl.BlockSpec((t, w), lambda i: (i, 0)), pl.BlockSpec((1, w), lambda i: (0, 0))],
        out_specs=pl.BlockSpec((t, w), lambda i: (i, 0)), out_shape=_sds((s_len, w), F32),
        scratch_shapes=[pltpu.VMEM((1, w), F32)], compiler_params=_params(16 << 20),
    )(fl, bias)


def _b_cumsum_bwd(dc, fl, bias, *, name, t=512):
    s_len, w = fl.shape
    t = _tile(s_len, t)
    n = s_len // t

    def body(dc_ref, fl_ref, b_ref, dfl_ref, db_ref, carry, acc):
        i = pl.program_id(0)

        @pl.when(i == 0)
        def _():
            carry[...] = jnp.zeros_like(carry)
            acc[...] = jnp.zeros_like(acc)

        row = lax.broadcasted_iota(jnp.int32, (t, t), 0)
        col = lax.broadcasted_iota(jnp.int32, (t, t), 1)
        tri = jnp.where(col >= row, 1.0, 0.0).astype(BF16)
        dlogf = _tri_matmul(tri, dc_ref[...]) + carry[...]
        carry[...] = dlogf[0:1, :]
        dfl = dlogf * _sigmoid(-(fl_ref[...] + b_ref[...]))
        dfl_ref[...] = dfl
        acc[...] += _rowsum8(dfl)

        @pl.when(i == n - 1)
        def _():
            db_ref[...] = jnp.sum(acc[...], axis=0, keepdims=True)

    rev = pl.BlockSpec((t, w), lambda i: (n - 1 - i, 0))
    vec = pl.BlockSpec((1, w), lambda i: (0, 0))
    return pl.pallas_call(
        body, name=name, grid=(n,), in_specs=[rev, rev, vec], out_specs=[rev, vec],
        out_shape=[_sds((s_len, w), F32), _sds((1, w), F32)],
        scratch_shapes=[pltpu.VMEM((1, w), F32), pltpu.VMEM((8, w), F32)], compiler_params=_params(16 << 20),
    )(dc, fl, bias)


def _b_gate_fwd(o, proj, *, name, ts=512, tc=512):
    s_len, e = o.shape
    ts, tc = _tile(s_len, ts), _tile(e, tc)
    nc = e // tc

    def body(o_ref, z_ref, g_ref):
        g_ref[...] = (o_ref[...] * _silu(z_ref[...])).astype(g_ref.dtype)

    blk = pl.BlockSpec((ts, tc), lambda i, j: (i, j))
    return pl.pallas_call(
        body, name=name, grid=(s_len // ts, nc),
        in_specs=[blk, pl.BlockSpec((ts, tc), lambda i, j: (i, 3 * nc + j))], out_specs=blk,
        out_shape=_sds((s_len, e), BF16), compiler_params=_params(12 * ts * tc * 4),
    )(o, proj)


def _b_gate_bwd(dgated, o, o_lo, proj, *, name, hg, ts=512):
    s_len, e = o.shape
    ts = _tile(s_len, ts)
    w = hg * HEAD_DIM
    ng = e // w

    def body(dg_ref, o_ref, olo_ref, z_ref, do_ref, dz_ref, dl_ref):
        dgt, ov, zv = dg_ref[...], o_ref[...], z_ref[...]
        dob = (dgt * _silu(zv)).astype(do_ref.dtype)
        do_ref[...] = dob
        dz_ref[...] = (dgt * ov * _dsilu(zv)).astype(dz_ref.dtype)
        prod = dob.astype(F32) * (ov + olo_ref[...])
        lane = lax.broadcasted_iota(jnp.int32, (ts, LANES), 1)
        d = jnp.zeros((ts, LANES), F32)
        for hh in range(hg):
            d = jnp.where(lane == hh, jnp.sum(prod[:, hh * HEAD_DIM:(hh + 1) * HEAD_DIM], axis=-1, keepdims=True), d)
        dl_ref[...] = d

    blk = pl.BlockSpec((ts, w), lambda i, j: (i, j))
    return pl.pallas_call(
        body, name=name, grid=(s_len // ts, ng),
        in_specs=[blk, blk, blk, pl.BlockSpec((ts, w), lambda i, j: (i, 3 * ng + j))],
        out_specs=[blk, blk, pl.BlockSpec((ts, LANES), lambda i, j: (i, j))],
        out_shape=[_sds((s_len, e), BF16), _sds((s_len, e), BF16), _sds((s_len, ng * LANES), F32)],
        compiler_params=_params(20 * ts * w * 4),
    )(dgated, o, o_lo, proj)


def _attn_scores(q, k, cc, cr, diag, t, scale):
    s = lax.dot_general(q, k, (((1,), (1,)), ((), ())), preferred_element_type=F32) * scale
    s = s + cc - cr
    if diag:
        row = lax.broadcasted_iota(jnp.int32, (t, t), 0)
        col = lax.broadcasted_iota(jnp.int32, (t, t), 1)
        s = jnp.where(col <= row, s, -jnp.inf)
    return s


def _fox_fwd(qn, kn, vb, c_col, c_row, *, name, hg, t=512):
    s_len, e = qn.shape
    t = _tile(s_len, t)
    w = hg * HEAD_DIM
    ng, n = e // w, s_len // t
    scale = HEAD_DIM ** -0.5

    def body(q_ref, k_ref, v_ref, cc_ref, cr_ref, o_ref, olo_ref, lse_ref, m_s, l_s, acc_s, lo_s):
        i, j = pl.program_id(1), pl.program_id(2)

        @pl.when(j == 0)
        def _():
            m_s[...] = jnp.full_like(m_s, -jnp.inf)
            l_s[...] = jnp.zeros_like(l_s)
            acc_s[...] = jnp.zeros_like(acc_s)
            lo_s[...] = jnp.zeros_like(lo_s)

        def step(diag):
            for hh in range(hg):
                cs = slice(hh * HEAD_DIM, (hh + 1) * HEAD_DIM)
                s = _attn_scores(q_ref[:, cs], k_ref[:, cs], cc_ref[:, hh:hh + 1], cr_ref[hh], diag, t, scale)
                m_prev = m_s[hh]
                m_new = jnp.maximum(m_prev, jnp.max(s, axis=-1, keepdims=True))
                alpha = jnp.exp(m_prev - m_new)
                p = jnp.exp(s - m_new)
                l_s[hh] = alpha * l_s[hh] + jnp.sum(p, axis=-1, keepdims=True)
                p_hi = p.astype(BF16)
                p_lo = (p - p_hi.astype(F32)).astype(BF16)
                acc_s[:, cs] = alpha * acc_s[:, cs] + jnp.dot(p_hi, v_ref[:, cs], preferred_element_type=F32)
                lo_s[:, cs] = alpha * lo_s[:, cs] + jnp.dot(p_lo, v_ref[:, cs], preferred_element_type=F32)
                m_s[hh] = m_new

        @pl.when(j < i)
        def _():
            step(False)

        @pl.when(j == i)
        def _():
            step(True)
            lane = lax.broadcasted_iota(jnp.int32, (t, LANES), 1)
            lse = jnp.zeros((t, LANES), F32)
            for hh in range(hg):
                cs = slice(hh * HEAD_DIM, (hh + 1) * HEAD_DIM)
                o_ref[:, cs] = acc_s[:, cs] / l_s[hh]
                olo_ref[:, cs] = lo_s[:, cs] / l_s[hh]
                lse = jnp.where(lane == hh, m_s[hh] + jnp.log(l_s[hh]), lse)
            lse_ref[...] = lse

    qspec = pl.BlockSpec((t, w), lambda g, i, j: (i, g))
    kspec = pl.BlockSpec((t, w), lambda g, i, j: (jnp.minimum(j, i), g))
    stat = pl.BlockSpec((t, LANES), lambda g, i, j: (i, g))
    crow = pl.BlockSpec((hg, 1, t), lambda g, i, j: (g, 0, jnp.minimum(j, i)))
    return pl.pallas_call(
        body, name=name, grid=(ng, n, n), in_specs=[qspec, kspec, kspec, stat, crow],
        out_specs=[qspec, qspec, stat],
        out_shape=[_sds((s_len, e), F32), _sds((s_len, e), F32), _sds((s_len, ng * LANES), F32)],
        scratch_shapes=[pltpu.VMEM((hg, t, 1), F32), pltpu.VMEM((hg, t, 1), F32), pltpu.VMEM((t, w), F32),
                        pltpu.VMEM((t, w), F32)],
        compiler_params=_params(12 * t * t * 4 + 24 * t * w * 4),
    )(qn, kn, vb, c_col, c_row)


def _fox_bwd(qn, kn, vb, do, lse, delta, c_col, c_row, *, name, hg, t=512):
    s_len, e = qn.shape
    t = _tile(s_len, t)
    w = hg * HEAD_DIM
    ng, n = e // w, s_len // t
    scale = HEAD_DIM ** -0.5

    def body(q_ref, k_ref, v_ref, do_ref, lse_ref, dl_ref, cc_ref, cr_ref, dq_ref, dk_ref, dv_ref, dc_ref,
             dk_s, dv_s, dc_s):
        j, i = pl.program_id(1), pl.program_id(2)

        @pl.when((j == 0) & (i == 0))
        def _():
            dq_ref[...] = jnp.zeros_like(dq_ref)

        @pl.when(i == j)
        def _():
            dk_s[...] = jnp.zeros_like(dk_s)
            dv_s[...] = jnp.zeros_like(dv_s)
            dc_s[...] = jnp.zeros_like(dc_s)

        def step(diag):
            rows = pl.ds(pl.multiple_of(i * t, t), t)
            for hh in range(hg):
                cs = slice(hh * HEAD_DIM, (hh + 1) * HEAD_DIM)
                q, k, v, dov = q_ref[:, cs], k_ref[:, cs], v_ref[:, cs], do_ref[:, cs]
                s = _attn_scores(q, k, cc_ref[:, hh:hh + 1], cr_ref[hh], diag, t, scale)
                p = jnp.exp(s - lse_ref[:, hh:hh + 1])
                dv_s[:, cs] += lax.dot_general(p.astype(BF16), dov, (((0,), (0,)), ((), ())), preferred_element_type=F32)
                dp = lax.dot_general(dov, v, (((1,), (1,)), ((), ())), preferred_element_type=F32)
                ds = p * (dp - dl_ref[:, hh:hh + 1])
                dc_s[hh] -= jnp.sum(ds, axis=0, keepdims=True)
                dsb = (ds * scale).astype(BF16)
                dk_s[:, cs] += lax.dot_general(dsb, q, (((0,), (0,)), ((), ())), preferred_element_type=F32)
                dq_ref[rows, cs] += jnp.dot(dsb, k, preferred_element_type=F32)

        @pl.when(i > j)
        def _():
            step(False)

        @pl.when(i == j)
        def _():
            step(True)

        @pl.when(i == n - 1)
        def _():
            dk_ref[...] = dk_s[...]
            dv_ref[...] = dv_s[...]
            dc_ref[...] = dc_s[...]

    qspec = pl.BlockSpec((t, w), lambda g, j, i: (jnp.maximum(i, j), g))
    kspec = pl.BlockSpec((t, w), lambda g, j, i: (j, g))
    stat = pl.BlockSpec((t, LANES), lambda g, j, i: (jnp.maximum(i, j), g))
    crow = pl.BlockSpec((hg, 1, t), lambda g, j, i: (g, 0, j))
    return pl.pallas_call(
        body, name=name, grid=(ng, n, n), in_specs=[qspec, kspec, kspec, qspec, stat, stat, stat, crow],
        out_specs=[pl.BlockSpec((s_len, w), lambda g, j, i: (0, g)), kspec, kspec, crow],
        out_shape=[_sds((s_len, e), F32), _sds((s_len, e), F32), _sds((s_len, e), F32), _sds((e // HEAD_DIM, 1, s_len), F32)],
        scratch_shapes=[pltpu.VMEM((t, w), F32), pltpu.VMEM((t, w), F32), pltpu.VMEM((hg, 1, t), F32)],
        compiler_params=_params(2 * s_len * w * 4 + 16 * t * t * 4 + 24 * t * w * 4),
    )(qn, kn, vb, do, lse, delta, c_col, c_row)


def _adamw(w, g, m, v, *, name, tr=256):
    r, c = w.shape
    tr = _tile(r, tr)

    def body(w_ref, g_ref, m_ref, v_ref, d_ref, nm_ref, nv_ref):
        gv = g_ref[...]
        m2 = ADAM_B1 * m_ref[...] + (1.0 - ADAM_B1) * gv
        v2 = ADAM_B2 * v_ref[...] + (1.0 - ADAM_B2) * (gv * gv)
        m_hat = m2 / (1.0 - ADAM_B1 ** ADAM_STEP)
        v_hat = v2 / (1.0 - ADAM_B2 ** ADAM_STEP)
        d_ref[...] = -ADAM_LR * (m_hat / (jnp.sqrt(v_hat) + ADAM_EPS) + ADAM_WD * w_ref[...])
        nm_ref[...] = m2
        nv_ref[...] = v2

    blk = pl.BlockSpec((tr, c), lambda i: (i, 0))
    return pl.pallas_call(
        body, name=name, grid=(r // tr,), in_specs=[blk] * 4, out_specs=[blk] * 3,
        out_shape=[_sds((r, c), F32)] * 3, compiler_params=_params(20 * tr * c * 4),
    )(w, g, m, v)


def _add2(a, b, *, name, out_dtype=F32, tr=256):
    g, r, c = a.shape
    tr = _tile(r, tr)

    def body(a_ref, b_ref, o_ref):
        o_ref[...] = (a_ref[...] + b_ref[...]).astype(o_ref.dtype)

    blk = pl.BlockSpec((None, tr, c), lambda s, i: (s, i, 0))
    return pl.pallas_call(
        body, name=name, grid=(g, r // tr), in_specs=[blk, blk], out_specs=blk,
        out_shape=_sds((g, r, c), out_dtype), compiler_params=_params(8 * tr * c * 4),
    )(a, b)


def _sum_slots(a, *, name, tr=256):
    n, r, c = a.shape
    tr = _tile(r, tr)

    def body(a_ref, o_ref):
        acc = a_ref[0].astype(F32)
        for k in range(1, n):
            acc = acc + a_ref[k].astype(F32)
        o_ref[...] = acc

    return pl.pallas_call(
        body, name=name, grid=(r // tr,), in_specs=[pl.BlockSpec((n, tr, c), lambda i: (0, i, 0))],
        out_specs=pl.BlockSpec((tr, c), lambda i: (i, 0)), out_shape=_sds((r, c), F32),
        compiler_params=_params(4 * (n + 2) * tr * c * 4),
    )(a)


_ANY = pl.BlockSpec(memory_space=pl.ANY)
DMA_CHUNK_BYTES = 512 << 10


def _chunks(parts):
    out = []
    for src_at, dst_at, rows, row_bytes in parts:
        step = max(16, DMA_CHUNK_BYTES // row_bytes // 16 * 16)
        for r0 in range(0, rows, step):
            n = min(step, rows - r0)
            out.append((src_at(r0, n), dst_at(r0, n)))
    return out


def _row_bytes(ref):
    return ref.shape[-1] * ref.dtype.itemsize


def _me():
    return lax.axis_index("x"), lax.axis_index("y"), lax.axis_index("c")


def _chip_peers(x, y):
    return [(2 * (1 - x) + y, (1 - x, y)), (2 * x + (1 - y), (x, 1 - y)), (2 * (1 - x) + (1 - y), (1 - x, 1 - y))]


def _exchange(name, ins, out_shapes, plan):
    n_in, n_out = len(ins), len(out_shapes)

    def body(*refs):
        in_refs, out_refs = refs[:n_in], refs[n_in:n_in + n_out]
        send_sems, recv_sems, loc_sems = refs[n_in + n_out:]
        remote, local = plan(in_refs, out_refs)
        starts, waits = [], []
        for k, (ws, wd, dev, parts) in enumerate(remote):
            def mk(s, d, k=k, dev=dev):
                return pltpu.make_async_remote_copy(src_ref=s, dst_ref=d, send_sem=send_sems.at[k],
                                                    recv_sem=recv_sems.at[k], device_id=dev, device_id_type=MESH_ID)
            starts += [mk(s, d) for s, d in _chunks(parts)]
            waits.append(mk(ws, wd))
        for k, (ws, wd, _, parts) in enumerate(local):
            def mk(s, d, k=k):
                return pltpu.make_async_copy(s, d, loc_sems.at[k])
            starts += [mk(s, d) for s, d in _chunks(parts)]
            waits.append(mk(ws, wd))
        for cp in starts:
            cp.start()
        for cp in waits:
            cp.wait()

    n_remote, n_local = plan.n_remote, plan.n_local
    return pl.pallas_call(
        body, name=name, in_specs=[_ANY] * n_in, out_specs=[_ANY] * n_out, out_shape=list(out_shapes),
        scratch_shapes=[pltpu.SemaphoreType.DMA((n_remote,)), pltpu.SemaphoreType.DMA((n_remote,)),
                        pltpu.SemaphoreType.DMA((max(n_local, 1),))],
    )(*ins)


def _gather_weights(shards, *, name):
    n_t = len(shards)

    def body(*refs):
        in_refs, out_refs = refs[:n_t], refs[n_t:2 * n_t]
        send_sems, recv_sems, loc_sems = refs[2 * n_t:]
        x, y, c = _me()
        chip = 2 * x + y
        peers = _chip_peers(x, y)
        local, first, passed = [], [], []
        for t in range(n_t):
            src, dst = in_refs[t], out_refs[t]
            n_rows = src.shape[0]
            half = n_rows // 2
            rb = _row_bytes(src)
            rows = pl.ds(c * half, half)

            def mk_loc(s, d, t=t):
                return pltpu.make_async_copy(s, d, loc_sems.at[t])

            own = [(lambda r0, n, src=src: src.at[pl.ds(r0, n)],
                    lambda r0, n, dst=dst: dst.at[chip, pl.ds(r0, n)], n_rows, rb)]
            local.append((mk_loc(src, dst.at[chip]), [mk_loc(s, d) for s, d in _chunks(own)]))
            for k, (pchip, (px, py)) in enumerate(peers):
                def mk_ici(s, d, t=t, k=k, px=px, py=py):
                    return pltpu.make_async_remote_copy(
                        src_ref=s, dst_ref=d, send_sem=send_sems.at[6 * t + k], recv_sem=recv_sems.at[6 * t + k],
                        device_id=(px, py, c), device_id_type=MESH_ID)

                def mk_d2d(s, d, t=t, k=k):
                    return pltpu.make_async_remote_copy(
                        src_ref=s, dst_ref=d, send_sem=send_sems.at[6 * t + 3 + k], recv_sem=recv_sems.at[6 * t + 3 + k],
                        device_id=(x, y, 1 - c), device_id_type=MESH_ID)

                out_part = [(lambda r0, n, src=src: src.at[pl.ds(c * half + r0, n)],
                             lambda r0, n, dst=dst: dst.at[chip, pl.ds(c * half + r0, n)], half, rb)]
                fwd_part = [(lambda r0, n, dst=dst, pchip=pchip: dst.at[pchip, pl.ds(c * half + r0, n)],
                             lambda r0, n, dst=dst, pchip=pchip: dst.at[pchip, pl.ds(c * half + r0, n)], half, rb)]
                first.append((mk_ici(src.at[rows], dst.at[chip, rows]), [mk_ici(s, d) for s, d in _chunks(out_part)]))
                passed.append((mk_d2d(dst.at[pchip, rows], dst.at[pchip, rows]),
                               [mk_d2d(s, d) for s, d in _chunks(fwd_part)]))
        for _, chunk_copies in first + local:
            for cp in chunk_copies:
                cp.start()
        for (whole, _), (_, fwd_copies) in zip(first, passed):
            whole.wait_recv()
            for cp in fwd_copies:
                cp.start()
        for whole, _ in passed:
            whole.wait_recv()
        for whole, _ in first + passed:
            whole.wait_send()
        for whole, _ in local:
            whole.wait()

    outs = [_sds((N_CHIPS,) + s.shape, s.dtype) for s in shards]
    return pl.pallas_call(
        body, name=name, in_specs=[_ANY] * n_t, out_specs=[_ANY] * n_t, out_shape=outs,
        scratch_shapes=[pltpu.SemaphoreType.DMA((6 * n_t,)), pltpu.SemaphoreType.DMA((6 * n_t,)),
                        pltpu.SemaphoreType.DMA((n_t,))],
    )(*shards)


class _Plan:
    def __init__(self, fn, n_remote, n_local):
        self.fn, self.n_remote, self.n_local = fn, n_remote, n_local

    def __call__(self, in_refs, out_refs):
        return self.fn(in_refs, out_refs)


def _reduce_grads(grads, *, name):
    n_t = len(grads)

    def plan1(in_refs, out_refs):
        x, y, c = _me()
        remote, local = [], []
        for t in range(n_t):
            src, got, kept = in_refs[t], out_refs[2 * t], out_refs[2 * t + 1]
            half = src.shape[1] // 2
            rb = _row_bytes(src)
            mine, theirs = pl.ds(c * half, half), pl.ds((1 - c) * half, half)
            send = [(lambda r0, n, s=s, src=src, half=half: src.at[s, pl.ds((1 - c) * half + r0, n)],
                     lambda r0, n, s=s, got=got: got.at[s, pl.ds(r0, n)], half, rb) for s in range(N_CHIPS)]
            keep = [(lambda r0, n, s=s, src=src, half=half: src.at[s, pl.ds(c * half + r0, n)],
                     lambda r0, n, s=s, kept=kept: kept.at[s, pl.ds(r0, n)], half, rb) for s in range(N_CHIPS)]
            remote.append((src.at[:, theirs], got, (x, y, 1 - c), send))
            local.append((src.at[:, mine], kept, None, keep))
        return remote, local

    halves = [_sds((N_CHIPS, g.shape[1] // 2, g.shape[2]), F32) for g in grads]
    out1 = _exchange(name + "_sib", grads, [h for h in halves for _ in range(2)], _Plan(plan1, n_t, n_t))
    pair = [_add2(out1[2 * t + 1], out1[2 * t], name=f"{name}_pair{t}", out_dtype=BF16) for t in range(n_t)]

    def plan2(in_refs, out_refs):
        x, y, c = _me()
        chip = 2 * x + y
        remote, local = [], []
        for t in range(n_t):
            src, dst = in_refs[t], out_refs[t]
            rows, rb = src.shape[1], _row_bytes(src)

            def part(slot, src=src, dst=dst, rows=rows, rb=rb):
                return [(lambda r0, n: src.at[slot, pl.ds(r0, n)], lambda r0, n: dst.at[chip, pl.ds(r0, n)], rows, rb)]

            for pchip, (px, py) in _chip_peers(x, y):
                remote.append((src.at[pchip], dst.at[chip], (px, py, c), part(pchip)))
            local.append((src.at[chip], dst.at[chip], None, part(chip)))
        return remote, local

    halves_bf = [_sds(h.shape, BF16) for h in halves]
    out2 = _exchange(name + "_ici", pair, halves_bf, _Plan(plan2, 3 * n_t, n_t))
    part_sums = [_sum_slots(out2[t], name=f"{name}_sum{t}") for t in range(n_t)]

    def plan3(in_refs, out_refs):
        x, y, c = _me()
        remote, local = [], []
        for t in range(n_t):
            src, dst = in_refs[t], out_refs[t]
            half, rb = src.shape[0], _row_bytes(src)
            mine = pl.ds(c * half, half)
            rows = [(lambda r0, n, src=src: src.at[pl.ds(r0, n)],
                     lambda r0, n, dst=dst, half=half: dst.at[pl.ds(c * half + r0, n)], half, rb)]
            remote.append((src, dst.at[mine], (x, y, 1 - c), rows))
            local.append((src, dst.at[mine], None, rows))
        return remote, local

    fulls = [_sds((g.shape[1], g.shape[2]), F32) for g in grads]
    return _exchange(name + "_swap", part_sums, fulls, _Plan(plan3, n_t, n_t))


def _allreduce_small(pack, *, name):
    r, w = pack.shape

    def body(p_ref, o_ref, buf, send_sems, recv_sems):
        x, y, c = _me()
        me = 4 * x + 2 * y + c
        buf[me] = p_ref[...]
        copies = []
        for k in range(1, N_DEV):
            peer = (x ^ ((k >> 2) & 1), y ^ ((k >> 1) & 1), c ^ (k & 1))
            copies.append(pltpu.make_async_remote_copy(
                src_ref=p_ref, dst_ref=buf.at[me], send_sem=send_sems.at[k - 1], recv_sem=recv_sems.at[k - 1],
                device_id=peer, device_id_type=MESH_ID))
        for cp in copies:
            cp.start()
        for cp in copies:
            cp.wait()
        acc = buf[0]
        for k in range(1, N_DEV):
            acc = acc + buf[k]
        o_ref[...] = acc

    vm = pl.BlockSpec(memory_space=pltpu.VMEM)
    return pl.pallas_call(
        body, name=name, in_specs=[vm], out_specs=vm, out_shape=_sds((r, w), F32),
        scratch_shapes=[pltpu.VMEM((N_DEV, r, w), F32), pltpu.SemaphoreType.DMA((N_DEV - 1,)),
                        pltpu.SemaphoreType.DMA((N_DEV - 1,))],
        compiler_params=_params(12 * r * w * 4),
    )(pack)


def _pack(arrs, row_multiple=16):
    flat = jnp.concatenate([a.reshape(-1).astype(F32) for a in arrs])
    unit = row_multiple * LANES
    total = -(-flat.shape[0] // unit) * unit
    return jnp.pad(flat, (0, total - flat.shape[0])).reshape(total // LANES, LANES)


def _unpack(packed, shapes):
    flat = packed.reshape(-1)
    out, off = [], 0
    for shp in shapes:
        n = int(np.prod(shp))
        out.append(flat[off:off + n].reshape(shp))
        off += n
    return out


def _pad_cols(a, width):
    return jnp.pad(a, [(0, 0)] * (a.ndim - 1) + [(0, width - a.shape[-1])])


ATTN_HEADS_PER_STEP = 2
SMALL_SHARDED = ("a_norm", "a_conv_w", "a_conv_b", "a_ln_g", "a_ln_b", "c_norm", "c_conv_w")
SMALL_REPLICATED = ("b_norm", "b_f_bias", "b_q_norm", "b_k_norm")
BIG = ("a_w_in", "a_w_out", "b_w_in", "b_w_out", "c_w_in", "c_w_out")
WEIGHTS = ("a_norm", "a_w_in", "a_conv_w", "a_conv_b", "a_ln_g", "a_ln_b", "a_w_out", "b_norm", "b_w_in", "b_f_bias",
           "b_q_norm", "b_k_norm", "b_w_out", "c_norm", "c_w_in", "c_conv_w", "c_w_out")


def _mixer_a_fwd(x, p, l, tag):
    h = _rms_fwd(x, p["a_norm"][l][None], name=f"{tag}_rms")
    proj = _mm_nn(h, p["a_w_in"][l], name=f"{tag}_in", tn=p["a_w_in"][l].shape[2])
    u1 = _a_conv_fwd(proj, p["a_conv_w"][l], p["a_conv_b"][l][None], name=f"{tag}_conv")
    gated = _a_post_fwd(u1, proj, p["a_ln_g"][l][None], p["a_ln_b"][l][None], name=f"{tag}_post")
    y = _mm_nn(gated, p["a_w_out"][l], name=f"{tag}_out", tn=1024, add=x)
    return y, (x, h, proj, u1, gated)


def _mixer_a_bwd(dx, saved, p, l, tag):
    x, h, proj, u1, gated = saved
    g = {}
    g["a_w_out"] = _mm_tn(gated, dx, name=f"{tag}_dwout", tk=2048, tn=1024)
    dgated = _mm_nt(dx, p["a_w_out"][l], name=f"{tag}_dgated", tn=2048, tk=1024)
    du1, dz, g["a_ln_g"], g["a_ln_b"], g["a_conv_b"] = _a_post_bwd(
        dgated, u1, proj, p["a_ln_g"][l][None], p["a_ln_b"][l][None], name=f"{tag}_dpost")
    dproj, g["a_conv_w"] = _a_conv_bwd(du1, proj, dz, p["a_conv_w"][l], name=f"{tag}_dconv")
    g["a_w_in"] = _mm_tn(h, dproj, name=f"{tag}_dwin", out_width=p["a_w_in"][l].shape[2])
    dh = _mm_nt(dproj, p["a_w_in"][l], name=f"{tag}_dh")
    dx, g["a_norm"] = _rms_bwd(x, p["a_norm"][l][None], [dh], dx, name=f"{tag}_drms")
    return dx, g


def _mixer_c_fwd(x, p, tag):
    h = _rms_fwd(x, p["c_norm"][0][None], name=f"{tag}_rms")
    proj = _mm_nn(h, p["c_w_in"], name=f"{tag}_in", tn=1024)
    gated = _c_mid_fwd(proj, p["c_conv_w"][0], name=f"{tag}_mid")
    y = _mm_nn(gated, p["c_w_out"], name=f"{tag}_out", tn=1024, add=x)
    return y, (x, h, proj, gated)


def _mixer_c_bwd(dx, saved, p, tag):
    x, h, proj, gated = saved
    g = {}
    g["c_w_out"] = _mm_tn(gated, dx, name=f"{tag}_dwout", tk=2048, tn=1024)
    dgated = _mm_nt(dx, p["c_w_out"], name=f"{tag}_dgated", tn=2048, tk=1024)
    dproj, g["c_conv_w"] = _c_mid_bwd(dgated, proj, p["c_conv_w"][0], name=f"{tag}_dmid")
    g["c_w_in"] = _mm_tn(h, dproj, name=f"{tag}_dwin", out_width=p["c_w_in"].shape[2], tn=1024)
    dh = _mm_nt(dproj, p["c_w_in"], name=f"{tag}_dh", tk=1024)
    dx, g["c_norm"] = _rms_bwd(x, p["c_norm"][0][None], [dh], dx, name=f"{tag}_drms")
    return dx, g


def _mixer_b_fwd(x, p, tag):
    hg = ATTN_HEADS_PER_STEP
    s_len = x.shape[0]
    n_heads = p["b_f_bias"].shape[1]
    h = _rms_fwd(x, p["b_norm"], name=f"{tag}_rms")
    proj = _mm_nn(h, p["b_wq"], name=f"{tag}_in", tn=1024)
    fl = _mm_nn(h, p["b_wf"], name=f"{tag}_inf", tn=LANES)
    qn, kn, vb = _b_qk_fwd(proj, p["b_q_norm"], p["b_k_norm"], name=f"{tag}_qk")
    bias = _pad_cols(p["b_f_bias"], LANES)
    c = _b_cumsum(fl, bias, name=f"{tag}_cumsum")
    ch = c[:, :n_heads]
    c_col = _pad_cols(ch.reshape(s_len, n_heads // hg, hg), LANES).reshape(s_len, (n_heads // hg) * LANES)
    c_row = ch.T.reshape(n_heads, 1, s_len)
    o, o_lo, lse = _fox_fwd(qn, kn, vb, c_col, c_row, name=f"{tag}_attn", hg=hg)
    gated = _b_gate_fwd(o, proj, name=f"{tag}_gate")
    y = _mm_nn(gated, p["b_w_out"], name=f"{tag}_out", tn=1024, add=x)
    return y, (x, h, proj, fl, bias, qn, kn, vb, c_col, c_row, o, o_lo, lse, gated)


def _mixer_b_bwd(dx, saved, p, tag):
    hg = ATTN_HEADS_PER_STEP
    x, h, proj, fl, bias, qn, kn, vb, c_col, c_row, o, o_lo, lse, gated = saved
    s_len = x.shape[0]
    n_heads = p["b_f_bias"].shape[1]
    g = {}
    g["b_w_out"] = _mm_tn(gated, dx, name=f"{tag}_dwout", tk=2048, tn=1024)
    dgated = _mm_nt(dx, p["b_w_out"], name=f"{tag}_dgated", tn=2048, tk=1024)
    do, dz, delta = _b_gate_bwd(dgated, o, o_lo, proj, name=f"{tag}_dgate", hg=hg)
    dqn, dkn, dv, dc = _fox_bwd(qn, kn, vb, do, lse, delta, c_col, c_row, name=f"{tag}_dattn", hg=hg)
    dc_pad = _pad_cols(dc.reshape(n_heads, s_len).T, LANES)
    dfl, dbias = _b_cumsum_bwd(dc_pad, fl, bias, name=f"{tag}_dcumsum")
    g["b_f_bias"] = dbias[:, :n_heads]
    dproj, g["b_q_norm"], g["b_k_norm"] = _b_qk_bwd(dqn, dkn, dv, dz, proj, p["b_q_norm"], p["b_k_norm"], name=f"{tag}_dqk")
    dwq = _mm_tn(h, dproj, name=f"{tag}_dwin", tn=1024)
    dwf = _mm_tn(h, dfl, name=f"{tag}_dwinf", tn=LANES)
    g["b_w_in"] = jnp.concatenate([dwq, dwf[:, :n_heads]], axis=1)
    dh = _mm_nt(dproj, p["b_wq"], name=f"{tag}_dh", tk=1024)
    dhf = _mm_nt(dfl, p["b_wf"], name=f"{tag}_dhf", tk=LANES)
    dx, g["b_norm"] = _rms_bwd(x, p["b_norm"], [dh, dhf], dx, name=f"{tag}_drms")
    return dx, g


def kernel(x, a_norm, a_w_in, a_conv_w, a_conv_b, a_ln_g, a_ln_b, a_w_out, b_norm, b_w_in, b_f_bias, b_q_norm, b_k_norm, b_w_out, c_norm, c_w_in, c_conv_w, c_w_out, loss_target, m_a_norm, m_a_w_in, m_a_conv_w, m_a_conv_b, m_a_ln_g, m_a_ln_b, m_a_w_out, m_b_norm, m_b_w_in, m_b_f_bias, m_b_q_norm, m_b_k_norm, m_b_w_out, m_c_norm, m_c_w_in, m_c_conv_w, m_c_w_out, v_a_norm, v_a_w_in, v_a_conv_w, v_a_conv_b, v_a_ln_g, v_a_ln_b, v_a_w_out, v_b_norm, v_b_w_in, v_b_f_bias, v_b_q_norm, v_b_k_norm, v_b_w_out, v_c_norm, v_c_w_in, v_c_conv_w, v_c_w_out):
    w_loc = dict(a_norm=a_norm, a_w_in=a_w_in, a_conv_w=a_conv_w, a_conv_b=a_conv_b, a_ln_g=a_ln_g, a_ln_b=a_ln_b,
                 a_w_out=a_w_out, b_norm=b_norm, b_w_in=b_w_in, b_f_bias=b_f_bias, b_q_norm=b_q_norm, b_k_norm=b_k_norm,
                 b_w_out=b_w_out, c_norm=c_norm, c_w_in=c_w_in, c_conv_w=c_conv_w, c_w_out=c_w_out)
    m_loc = dict(a_norm=m_a_norm, a_w_in=m_a_w_in, a_conv_w=m_a_conv_w, a_conv_b=m_a_conv_b, a_ln_g=m_a_ln_g,
                 a_ln_b=m_a_ln_b, a_w_out=m_a_w_out, b_norm=m_b_norm, b_w_in=m_b_w_in, b_f_bias=m_b_f_bias,
                 b_q_norm=m_b_q_norm, b_k_norm=m_b_k_norm, b_w_out=m_b_w_out, c_norm=m_c_norm, c_w_in=m_c_w_in,
                 c_conv_w=m_c_conv_w, c_w_out=m_c_w_out)
    v_loc = dict(a_norm=v_a_norm, a_w_in=v_a_w_in, a_conv_w=v_a_conv_w, a_conv_b=v_a_conv_b, a_ln_g=v_a_ln_g,
                 a_ln_b=v_a_ln_b, a_w_out=v_a_w_out, b_norm=v_b_norm, b_w_in=v_b_w_in, b_f_bias=v_b_f_bias,
                 b_q_norm=v_b_q_norm, b_k_norm=v_b_k_norm, b_w_out=v_b_w_out, c_norm=v_c_norm, c_w_in=v_c_w_in,
                 c_conv_w=v_c_conv_w, c_w_out=v_c_w_out)
    n_a = a_w_in.shape[0]
    d_model = x.shape[2]
    e_inner = a_w_out.shape[1] * N_CHIPS
    n_heads = b_f_bias.shape[1]
    nb_loc = b_w_in.shape[2]
    nb_pad = -(-nb_loc // LANES) * LANES
    chip = 2 * lax.axis_index("x") + lax.axis_index("y")

    big_shards = ([a_w_in[l].astype(BF16) for l in range(n_a)] + [a_w_out[l].astype(BF16) for l in range(n_a)]
                  + [_pad_cols(b_w_in[0], nb_pad).astype(BF16), b_w_out[0].astype(BF16), c_w_in[0].astype(BF16),
                     c_w_out[0].astype(BF16)])
    small_pack = _pack([w_loc[n] for n in SMALL_SHARDED])
    gathered = _gather_weights(big_shards + [small_pack], name="gather_weights")
    p = {}
    p["a_w_in"] = gathered[0:n_a]
    p["a_w_out"] = [g.reshape(e_inner, d_model) for g in gathered[n_a:2 * n_a]]
    gb, gbo, gci, gco, gsmall = gathered[2 * n_a:]
    wb_full = jnp.concatenate([gb[k, :, :nb_loc] for k in range(N_CHIPS)], axis=1)
    p["b_wq"] = wb_full[:, :4 * e_inner]
    p["b_wf"] = _pad_cols(wb_full[:, 4 * e_inner:], LANES)
    p["b_w_out"] = gbo.reshape(e_inner, d_model)
    p["c_w_in"] = gci
    p["c_w_out"] = gco.reshape(e_inner, d_model)
    small_shapes = [w_loc[n].shape for n in SMALL_SHARDED]
    per_chip = [_unpack(gsmall[k], small_shapes) for k in range(N_CHIPS)]
    for idx, n in enumerate(SMALL_SHARDED):
        p[n] = jnp.concatenate([per_chip[k][idx] for k in range(N_CHIPS)], axis=-1)
    for n in SMALL_REPLICATED:
        p[n] = w_loc[n]

    x0 = x[0]
    x1, sv0 = _mixer_a_fwd(x0, p, 0, "a0")
    x2, sv1 = _mixer_b_fwd(x1, p, "b0")
    x3, sv2 = _mixer_c_fwd(x2, p, "c0")
    x4, sv3 = _mixer_a_fwd(x3, p, 1, "a1")
    dy, loss_part = _loss_head(x4, loss_target[0], name="loss_head")
    loss = lax.psum(loss_part[0, 0], ("x", "y", "c"))

    dx, g3 = _mixer_a_bwd(dy, sv3, p, 1, "a1")
    dx, g2 = _mixer_c_bwd(dx, sv2, p, "c0")
    dx, g1 = _mixer_b_bwd(dx, sv1, p, "b0")
    dx, g0 = _mixer_a_bwd(dx, sv0, p, 0, "a0")
    grad_x = dx[None]

    half_rows = e_inner // N_CHIPS
    gb_full = g1["b_w_in"].reshape(d_model, N_CHIPS, nb_loc).transpose(1, 0, 2)
    big_grads = ([g0["a_w_in"], g3["a_w_in"]]
                 + [g0["a_w_out"].reshape(N_CHIPS, half_rows, d_model), g3["a_w_out"].reshape(N_CHIPS, half_rows, d_model)]
                 + [_pad_cols(gb_full, nb_pad), g1["b_w_out"].reshape(N_CHIPS, half_rows, d_model), g2["c_w_in"],
                    g2["c_w_out"].reshape(N_CHIPS, half_rows, d_model)])
    red = _reduce_grads(big_grads, name="reduce_grads")
    grads = {}
    grads["a_w_in"] = jnp.stack(red[0:2])
    grads["a_w_out"] = jnp.stack(red[2:4])
    grads["b_w_in"] = red[4][None, :, :nb_loc]
    grads["b_w_out"] = red[5][None]
    grads["c_w_in"] = red[6][None]
    grads["c_w_out"] = red[7][None]

    small_full = {}
    for n in ("a_norm", "a_conv_w", "a_conv_b", "a_ln_g", "a_ln_b"):
        small_full[n] = jnp.stack([g0[n].reshape(p[n].shape[1:]), g3[n].reshape(p[n].shape[1:])])
    small_full["c_norm"] = g2["c_norm"].reshape(p["c_norm"].shape)
    small_full["c_conv_w"] = g2["c_conv_w"].reshape(p["c_conv_w"].shape)
    for n in SMALL_REPLICATED:
        small_full[n] = g1[n].reshape(w_loc[n].shape)
    small_names = SMALL_SHARDED + SMALL_REPLICATED
    summed = _unpack(_allreduce_small(_pack([small_full[n] for n in small_names], 8), name="reduce_small"),
                     [small_full[n].shape for n in small_names])
    for n, s in zip(small_names, summed):
        if n in SMALL_SHARDED:
            width = w_loc[n].shape[-1]
            grads[n] = lax.dynamic_slice_in_dim(s, chip * width, width, axis=s.ndim - 1)
        else:
            grads[n] = s

    delta, new_m, new_v = {}, {}, {}
    for n in BIG:
        shp = w_loc[n].shape
        cols = nb_pad if n == "b_w_in" else shp[-1]
        prep = lambda a: _pad_cols(a, cols).reshape(-1, cols)
        d_, m_, v_ = _adamw(prep(w_loc[n]), prep(grads[n]), prep(m_loc[n]), prep(v_loc[n]), name=f"adamw_{n}")
        delta[n], new_m[n], new_v[n] = [a.reshape(shp[:-1] + (cols,))[..., :shp[-1]] for a in (d_, m_, v_)]
    small_shapes_all = [w_loc[n].shape for n in small_names]
    d_, m_, v_ = _adamw(_pack([w_loc[n] for n in small_names], 8), _pack([grads[n] for n in small_names], 8),
                        _pack([m_loc[n] for n in small_names], 8), _pack([v_loc[n] for n in small_names], 8),
                        name="adamw_small")
    for n, a, b, c_ in zip(small_names, _unpack(d_, small_shapes_all), _unpack(m_, small_shapes_all),
                           _unpack(v_, small_shapes_all)):
        delta[n], new_m[n], new_v[n] = a, b, c_

    return (loss, grad_x, *[grads[n] for n in WEIGHTS], *[delta[n] for n in WEIGHTS],
            *[new_m[n] for n in WEIGHTS], *[new_v[n] for n in WEIGHTS])
```

```python
import functools

import numpy as np
import jax
import jax.numpy as jnp
from jax import lax
from jax.experimental import pallas as pl
from jax.experimental.pallas import tpu as pltpu

F32 = jnp.float32
BF16 = jnp.bfloat16
NORM_EPS = 1e-6
HEAD_DIM = 128
LANES = 128
N_CHIPS = 4
N_DEV = 8
VMEM_CAP = 56 << 20
MESH_ID = pl.DeviceIdType.MESH

ADAM_LR = 0.001
ADAM_B1 = 0.9
ADAM_B2 = 0.999
ADAM_EPS = 1e-08
ADAM_WD = 0.01
ADAM_STEP = 10


def _sds(shape, dtype):
    return jax.ShapeDtypeStruct(tuple(shape), dtype)


def _tile(n, pref):
    if n <= pref:
        return n
    for t in range(pref - pref % 8, 0, -8):
        if n % t == 0:
            return t
    raise ValueError(f"no tile for {n} under {pref}")


def _params(vmem_bytes):
    return pltpu.CompilerParams(vmem_limit_bytes=int(min(max(vmem_bytes, 16 << 20), VMEM_CAP)))


def _sigmoid(v):
    return 1.0 / (1.0 + jnp.exp(-v))


def _silu(v):
    return v * _sigmoid(v)


def _dsilu(v):
    s = _sigmoid(v)
    return s * (1.0 + v * (1.0 - s))


def _rowsum8(v):
    r, c = v.shape
    return jnp.sum(v.reshape(r // 8, 8, c), axis=0)


def _col_spec(grouped, rows_block, tile, width, row_of, col_of):
    if grouped:
        per = width // tile
        return pl.BlockSpec((None, rows_block, tile), lambda *ids: (col_of(*ids) // per, row_of(*ids), col_of(*ids) % per))
    return pl.BlockSpec((rows_block, tile), lambda *ids: (row_of(*ids), col_of(*ids)))


def _mm_nn(a, b, *, name, tm=1024, tn=512, out_dtype=F32, add=None):
    m, k = a.shape
    b_grouped = b.ndim == 3
    n = b.shape[0] * b.shape[2] if b_grouped else b.shape[1]
    width = b.shape[2] if b_grouped else n
    tm, tn = _tile(m, tm), _tile(width, tn)

    def body(*refs):
        if add is None:
            a_ref, b_ref, o_ref = refs
        else:
            a_ref, b_ref, r_ref, o_ref = refs
        acc = jnp.dot(a_ref[...].astype(BF16), b_ref[...].astype(BF16), preferred_element_type=F32)
        if add is not None:
            acc = acc + r_ref[...]
        o_ref[...] = acc.astype(o_ref.dtype)

    in_specs = [pl.BlockSpec((tm, k), lambda i, j: (i, 0)),
                _col_spec(b_grouped, k, tn, width, lambda i, j: 0, lambda i, j: j)]
    args = [a, b]
    if add is not None:
        in_specs.append(pl.BlockSpec((tm, tn), lambda i, j: (i, j)))
        args.append(add)
    vmem = 2 * (tm * k * a.dtype.itemsize + k * tn * 2 + tm * tn * 4 * (2 if add is not None else 1)) + tm * tn * 8 + tm * k * 2
    return pl.pallas_call(
        body, name=name, grid=(m // tm, n // tn), in_specs=in_specs,
        out_specs=pl.BlockSpec((tm, tn), lambda i, j: (i, j)),
        out_shape=_sds((m, n), out_dtype), compiler_params=_params(vmem + (4 << 20)),
    )(*args)


def _mm_nt(a, b, *, name, tm=1024, tn=1024, tk=512):
    a_grouped, b_grouped = a.ndim == 3, b.ndim == 3
    m = a.shape[1] if a_grouped else a.shape[0]
    n = a.shape[0] * a.shape[2] if a_grouped else a.shape[1]
    kk = b.shape[1] if b_grouped else b.shape[0]
    wa = a.shape[2] if a_grouped else n
    wb = b.shape[2] if b_grouped else n
    tm, tn = _tile(m, tm), _tile(kk, tn)
    tk = _tile(int(np.gcd(wa, wb)), tk)
    steps = n // tk

    def body(a_ref, b_ref, o_ref):
        part = lax.dot_general(a_ref[...].astype(BF16), b_ref[...].astype(BF16), (((1,), (1,)), ((), ())),
                               preferred_element_type=F32)

        @pl.when(pl.program_id(2) == 0)
        def _():
            o_ref[...] = part

        @pl.when(pl.program_id(2) > 0)
        def _():
            o_ref[...] += part

    in_specs = [_col_spec(a_grouped, tm, tk, wa, lambda i, j, s: i, lambda i, j, s: s),
                _col_spec(b_grouped, tn, tk, wb, lambda i, j, s: j, lambda i, j, s: s)]
    vmem = 2 * (tm * tk * a.dtype.itemsize + tn * tk * b.dtype.itemsize + tm * tn * 4) + tm * tn * 4 + (tm + tn) * tk * 2
    return pl.pallas_call(
        body, name=name, grid=(m // tm, kk // tn, steps), in_specs=in_specs,
        out_specs=pl.BlockSpec((tm, tn), lambda i, j, s: (i, j)),
        out_shape=_sds((m, kk), F32), compiler_params=_params(vmem + (4 << 20)),
    )(a, b)


def _mm_tn(a, b, *, name, out_width=None, out_dtype=F32, tk=1024, tn=512, ts=512):
    s_len, k = a.shape
    b_grouped = b.ndim == 3
    n = b.shape[0] * b.shape[2] if b_grouped else b.shape[1]
    wb = b.shape[2] if b_grouped else n
    wo = out_width if out_width is not None else n
    tk, ts = _tile(k, tk), _tile(s_len, ts)
    tn = _tile(int(np.gcd(wb, wo)), tn)
    last = s_len // ts - 1
    direct = out_dtype == F32

    def body(a_ref, b_ref, o_ref, *scratch):
        acc = o_ref if direct else scratch[0]
        part = lax.dot_general(a_ref[...].astype(BF16), b_ref[...].astype(BF16), (((0,), (0,)), ((), ())),
                               preferred_element_type=F32)

        @pl.when(pl.program_id(2) == 0)
        def _():
            acc[...] = part

        @pl.when(pl.program_id(2) > 0)
        def _():
            acc[...] += part

        if not direct:
            @pl.when(pl.program_id(2) == last)
            def _():
                o_ref[...] = acc[...].astype(o_ref.dtype)

    in_specs = [pl.BlockSpec((ts, tk), lambda i, j, s: (s, i)),
                _col_spec(b_grouped, ts, tn, wb, lambda i, j, s: s, lambda i, j, s: j)]
    out_grouped = out_width is not None
    out_spec = _col_spec(out_grouped, tk, tn, wo, lambda i, j, s: i, lambda i, j, s: j)
    out_shape = _sds((n // wo, k, wo), out_dtype) if out_grouped else _sds((k, n), out_dtype)
    vmem = 2 * (ts * tk * a.dtype.itemsize + ts * tn * b.dtype.itemsize + tk * tn * 4) + 2 * tk * tn * 4 + ts * (tk + tn) * 4
    return pl.pallas_call(
        body, name=name, grid=(k // tk, n // tn, s_len // ts), in_specs=in_specs, out_specs=out_spec,
        out_shape=out_shape, scratch_shapes=[] if direct else [pltpu.VMEM((tk, tn), F32)],
        compiler_params=_params(vmem + (4 << 20)),
    )(a, b)


def _rms_fwd(x, g, *, name, ts=512):
    s_len, d = x.shape
    ts = _tile(s_len, ts)

    def body(x_ref, g_ref, h_ref):
        xf = x_ref[...]
        r = lax.rsqrt(jnp.mean(xf * xf, axis=-1, keepdims=True) + NORM_EPS)
        h_ref[...] = ((xf * r) * g_ref[...]).astype(h_ref.dtype)

    return pl.pallas_call(
        body, name=name, grid=(s_len // ts,),
        in_specs=[pl.BlockSpec((ts, d), lambda i: (i, 0)), pl.BlockSpec((1, d), lambda i: (0, 0))],
        out_specs=pl.BlockSpec((ts, d), lambda i: (i, 0)), out_shape=_sds((s_len, d), BF16),
        compiler_params=_params(8 * ts * d * 4),
    )(x, g)


def _rms_bwd(x, g, dhs, dres, *, name, ts=512):
    s_len, d = x.shape
    ts = _tile(s_len, ts)
    n_dh = len(dhs)
    last = s_len // ts - 1

    def body(*refs):
        x_ref, g_ref = refs[0], refs[1]
        dh_refs = refs[2:2 + n_dh]
        dres_ref, dx_ref, dg_ref, acc = refs[2 + n_dh:]
        i = pl.program_id(0)

        @pl.when(i == 0)
        def _():
            acc[...] = jnp.zeros_like(acc)

        xf = x_ref[...]
        dy = dh_refs[0][...]
        for r_ in dh_refs[1:]:
            dy = dy + r_[...]
        r = lax.rsqrt(jnp.mean(xf * xf, axis=-1, keepdims=True) + NORM_EPS)
        gd = dy * g_ref[...]
        dot = jnp.mean(xf * gd, axis=-1, keepdims=True)
        dx_ref[...] = dres_ref[...] + r * gd - xf * (r * r * r * dot)
        acc[...] += _rowsum8(dy * (xf * r))

        @pl.when(i == last)
        def _():
            dg_ref[...] = jnp.sum(acc[...], axis=0, keepdims=True)

    row = pl.BlockSpec((ts, d), lambda i: (i, 0))
    vec = pl.BlockSpec((1, d), lambda i: (0, 0))
    return pl.pallas_call(
        body, name=name, grid=(s_len // ts,),
        in_specs=[row, vec] + [row] * n_dh + [row],
        out_specs=[row, vec], out_shape=[_sds((s_len, d), F32), _sds((1, d), F32)],
        scratch_shapes=[pltpu.VMEM((8, d), F32)],
        compiler_params=_params((2 * (3 + n_dh) + 6) * ts * d * 4),
    )(x, g, *dhs, dres)


def _loss_head(y, target, *, name, ts=512):
    s_len, d = y.shape
    ts = _tile(s_len, ts)
    last = s_len // ts - 1

    def body(y_ref, t_ref, dy_ref, loss_ref, acc):
        i = pl.program_id(0)

        @pl.when(i == 0)
        def _():
            acc[...] = jnp.zeros_like(acc)

        err = y_ref[...] - t_ref[...]
        dy_ref[...] = err / d
        acc[...] += _rowsum8(err * err)

        @pl.when(i == last)
        def _():
            loss_ref[...] = (0.5 * jnp.sum(acc[...]) / d).reshape(1, 1)

    row = pl.BlockSpec((ts, d), lambda i: (i, 0))
    return pl.pallas_call(
        body, name=name, grid=(s_len // ts,), in_specs=[row, row],
        out_specs=[row, pl.BlockSpec((1, 1), lambda i: (0, 0))],
        out_shape=[_sds((s_len, d), F32), _sds((1, 1), F32)],
        scratch_shapes=[pltpu.VMEM((8, d), F32)],
        compiler_params=_params(10 * ts * d * 4),
    )(y, target)


CONV_ROWS = 64
CONV_COLS = 256


def _conv_taps(buf, w_ref, k_width, base, ts, tc, init, emit, reverse=False):
    cw = min(tc, CONV_COLS)
    rw = min(ts, CONV_ROWS)
    for cb in range(tc // cw):
        cs = slice(cb * cw, (cb + 1) * cw)
        for rb in range(ts // rw):
            acc = init(slice(rb * rw, (rb + 1) * rw), cs, (rw, cw))
            for k in range(k_width):
                sh = (k_width - 1 - k) if reverse else k
                acc = acc + w_ref[k:k + 1, cs] * buf[base + rb * rw + sh:base + rb * rw + sh + rw, cs]
            emit(slice(rb * rw, (rb + 1) * rw), cs, acc)


def _conv_wgrad(buf, d_ref_val, acc_ref, k_width, base, ts, tc):
    cw = min(tc, CONV_COLS)
    rw = min(ts, CONV_ROWS)
    for cb in range(tc // cw):
        cs = slice(cb * cw, (cb + 1) * cw)
        for rb in range(ts // rw):
            dv = d_ref_val[rb * rw:(rb + 1) * rw, cs]
            for k in range(k_width):
                prod = dv * buf[base + rb * rw + k:base + rb * rw + k + rw, cs]
                acc_ref[8 * k:8 * k + 8, cs] += _rowsum8(prod)


A_HALO = 32


def _a_conv_fwd(proj, conv_w, conv_b, *, name, ts=256, tc=512):
    s_len, e3 = proj.shape
    e = e3 // 3
    k_width = conv_w.shape[0]
    ts, tc = _tile(s_len, ts), _tile(e, tc)
    nc = e // tc
    kp = 32

    def body(val, gate, valh, gateh, w_ref, b_ref, u1_ref, buf):
        i = pl.program_id(1)
        u0h = valh[...] * _sigmoid(gateh[...])
        buf[0:A_HALO, :] = jnp.where(i > 0, u0h, 0.0)
        buf[A_HALO:A_HALO + ts, :] = val[...] * _sigmoid(gate[...])

        def init(rows, cs, shape):
            return jnp.broadcast_to(b_ref[:, cs], shape)

        def emit(rows, cs, acc):
            u1_ref[rows, cs] = acc

        _conv_taps(buf, w_ref, k_width, A_HALO - (k_width - 1), ts, tc, init, emit)

    hb = ts // A_HALO
    in_specs = [
        pl.BlockSpec((ts, tc), lambda j, i: (i, j)),
        pl.BlockSpec((ts, tc), lambda j, i: (i, nc + j)),
        pl.BlockSpec((A_HALO, tc), lambda j, i: (jnp.maximum(i * hb - 1, 0), j)),
        pl.BlockSpec((A_HALO, tc), lambda j, i: (jnp.maximum(i * hb - 1, 0), nc + j)),
        pl.BlockSpec((kp, tc), lambda j, i: (0, j)),
        pl.BlockSpec((1, tc), lambda j, i: (0, j)),
    ]
    w_pad = jnp.zeros((kp, e), F32).at[:k_width].set(conv_w)
    return pl.pallas_call(
        body, name=name, grid=(nc, s_len // ts), in_specs=in_specs,
        out_specs=pl.BlockSpec((ts, tc), lambda j, i: (i, j)), out_shape=_sds((s_len, e), F32),
        scratch_shapes=[pltpu.VMEM((A_HALO + ts, tc), F32)],
        compiler_params=_params(12 * ts * tc * 4),
    )(proj, proj, proj, proj, w_pad, conv_b)


def _ln_rows(u1, g, b):
    mu = jnp.mean(u1, axis=-1, keepdims=True)
    xc = u1 - mu
    var = jnp.mean(xc * xc, axis=-1, keepdims=True)
    rstd = lax.rsqrt(var + NORM_EPS)
    xhat = xc * rstd
    return xhat, rstd, xhat * g + b


def _a_post_fwd(u1, proj, ln_g, ln_b, *, name, ts=256):
    s_len, e = u1.shape
    ts = _tile(s_len, ts)

    def body(u1_ref, z_ref, g_ref, b_ref, o_ref):
        _, _, u2 = _ln_rows(u1_ref[...], g_ref[...], b_ref[...])
        o_ref[...] = (_silu(u2) * _silu(z_ref[...])).astype(o_ref.dtype)

    row = pl.BlockSpec((ts, e), lambda i: (i, 0))
    vec = pl.BlockSpec((1, e), lambda i: (0, 0))
    return pl.pallas_call(
        body, name=name, grid=(s_len // ts,),
        in_specs=[row, pl.BlockSpec((ts, e), lambda i: (i, 2)), vec, vec],
        out_specs=row, out_shape=_sds((s_len, e), BF16), compiler_params=_params(12 * ts * e * 4),
    )(u1, proj, ln_g, ln_b)


def _a_post_bwd(dgated, u1, proj, ln_g, ln_b, *, name, ts=256):
    s_len, e = u1.shape
    ts = _tile(s_len, ts)
    last = s_len // ts - 1

    def body(dg_ref, u1_ref, z_ref, g_ref, b_ref, du1_ref, dz_ref, dlg_ref, dlb_ref, dcb_ref, a_g, a_b, a_c):
        i = pl.program_id(0)

        @pl.when(i == 0)
        def _():
            a_g[...] = jnp.zeros_like(a_g)
            a_b[...] = jnp.zeros_like(a_b)
            a_c[...] = jnp.zeros_like(a_c)

        g = g_ref[...]
        xhat, rstd, u2 = _ln_rows(u1_ref[...], g, b_ref[...])
        z = z_ref[...]
        dgt = dg_ref[...]
        dz_ref[...] = (dgt * _silu(u2) * _dsilu(z)).astype(dz_ref.dtype)
        du2 = dgt * _silu(z) * _dsilu(u2)
        a_g[...] += _rowsum8(du2 * xhat)
        a_b[...] += _rowsum8(du2)
        dxh = du2 * g
        m1 = jnp.mean(dxh, axis=-1, keepdims=True)
        m2 = jnp.mean(dxh * xhat, axis=-1, keepdims=True)
        du1 = rstd * (dxh - m1 - xhat * m2)
        du1_ref[...] = du1
        a_c[...] += _rowsum8(du1)

        @pl.when(i == last)
        def _():
            dlg_ref[...] = jnp.sum(a_g[...], axis=0, keepdims=True)
            dlb_ref[...] = jnp.sum(a_b[...], axis=0, keepdims=True)
            dcb_ref[...] = jnp.sum(a_c[...], axis=0, keepdims=True)

    row = pl.BlockSpec((ts, e), lambda i: (i, 0))
    vec = pl.BlockSpec((1, e), lambda i: (0, 0))
    return pl.pallas_call(
        body, name=name, grid=(s_len // ts,),
        in_specs=[row, row, pl.BlockSpec((ts, e), lambda i: (i, 2)), vec, vec],
        out_specs=[row, row, vec, vec, vec],
        out_shape=[_sds((s_len, e), F32), _sds((s_len, e), BF16), _sds((1, e), F32), _sds((1, e), F32), _sds((1, e), F32)],
        scratch_shapes=[pltpu.VMEM((8, e), F32)] * 3,
        compiler_params=_params(20 * ts * e * 4),
    )(dgated, u1, proj, ln_g, ln_b)


def _a_conv_bwd(du1, proj, dz, conv_w, *, name, ts=256, tc=512):
    s_len, e3 = proj.shape
    e = e3 // 3
    k_width = conv_w.shape[0]
    ts, tc = _tile(s_len, ts), _tile(e, tc)
    nc, nr = e // tc, s_len // ts
    kp = 32
    hb = ts // A_HALO

    def body(val, gate, valh, gateh, d_ref, dh_ref, dz_ref, w_ref, dp_ref, dw_ref, buf_u, buf_d, du0, acc):
        i = pl.program_id(1)

        @pl.when(i == 0)
        def _():
            acc[...] = jnp.zeros_like(acc)

        v = val[...]
        sg = _sigmoid(gate[...])
        buf_u[0:A_HALO, :] = jnp.where(i > 0, valh[...] * _sigmoid(gateh[...]), 0.0)
        buf_u[A_HALO:A_HALO + ts, :] = v * sg
        buf_d[0:ts, :] = d_ref[...]
        buf_d[ts:ts + A_HALO, :] = jnp.where(i < nr - 1, dh_ref[...], 0.0)

        def init(rows, cs, shape):
            return jnp.zeros(shape, F32)

        def emit(rows, cs, a):
            du0[rows, cs] = a

        _conv_taps(buf_d, w_ref, k_width, 0, ts, tc, init, emit, reverse=True)
        _conv_wgrad(buf_u, buf_d, acc, k_width, A_HALO - (k_width - 1), ts, tc)
        d0 = du0[...]
        dp_ref[0] = (d0 * sg).astype(dp_ref.dtype)
        dp_ref[1] = (d0 * v * sg * (1.0 - sg)).astype(dp_ref.dtype)
        dp_ref[2] = dz_ref[...]

        @pl.when(i == nr - 1)
        def _():
            for k in range(kp):
                dw_ref[k:k + 1, :] = jnp.sum(acc[8 * k:8 * k + 8, :], axis=0, keepdims=True)

    in_specs = [
        pl.BlockSpec((ts, tc), lambda j, i: (i, j)),
        pl.BlockSpec((ts, tc), lambda j, i: (i, nc + j)),
        pl.BlockSpec((A_HALO, tc), lambda j, i: (jnp.maximum(i * hb - 1, 0), j)),
        pl.BlockSpec((A_HALO, tc), lambda j, i: (jnp.maximum(i * hb - 1, 0), nc + j)),
        pl.BlockSpec((ts, tc), lambda j, i: (i, j)),
        pl.BlockSpec((A_HALO, tc), lambda j, i: (jnp.minimum((i + 1) * hb, nr * hb - 1), j)),
        pl.BlockSpec((ts, tc), lambda j, i: (i, j)),
        pl.BlockSpec((kp, tc), lambda j, i: (0, j)),
    ]
    w_pad = jnp.zeros((kp, e), F32).at[:k_width].set(conv_w)
    dproj, dw = pl.pallas_call(
        body, name=name, grid=(nc, nr), in_specs=in_specs,
        out_specs=[pl.BlockSpec((3, ts, tc), lambda j, i: (0, i, j)), pl.BlockSpec((kp, tc), lambda j, i: (0, j))],
        out_shape=[_sds((3, s_len, e), BF16), _sds((kp, e), F32)],
        scratch_shapes=[pltpu.VMEM((A_HALO + ts, tc), F32), pltpu.VMEM((ts + A_HALO, tc), F32),
                        pltpu.VMEM((ts, tc), F32), pltpu.VMEM((8 * kp, tc), F32)],
        compiler_params=_params(24 * ts * tc * 4),
    )(proj, proj, proj, proj, du1, du1, dz, w_pad)
    return dproj, dw[:k_width]


C_HALO = 8


def _c_mid_fwd(proj, conv_w, *, name, ts=256, tc=512):
    s_len, e4 = proj.shape
    e = e4 // 4
    k_width = conv_w.shape[0]
    ts, tc = _tile(s_len, ts), _tile(e, tc)
    nc = e // tc
    hb = ts // C_HALO

    def body(u, bg, cg, z, uh, cgh, w_ref, o_ref, buf, y):
        i = pl.program_id(1)
        buf[0:C_HALO, :] = jnp.where(i > 0, uh[...] * cgh[...], 0.0)
        buf[C_HALO:C_HALO + ts, :] = u[...] * cg[...]

        def init(rows, cs, shape):
            return jnp.zeros(shape, F32)

        def emit(rows, cs, a):
            y[rows, cs] = a

        _conv_taps(buf, w_ref, k_width, C_HALO - (k_width - 1), ts, tc, init, emit)
        o_ref[...] = (bg[...] * y[...] * _silu(z[...])).astype(o_ref.dtype)

    def grp(g):
        return pl.BlockSpec((ts, tc), lambda j, i: (i, g * nc + j))

    def halo(g):
        return pl.BlockSpec((C_HALO, tc), lambda j, i: (jnp.maximum(i * hb - 1, 0), g * nc + j))

    w_pad = jnp.zeros((8, e), F32).at[:k_width].set(conv_w)
    return pl.pallas_call(
        body, name=name, grid=(nc, s_len // ts),
        in_specs=[grp(0), grp(1), grp(2), grp(3), halo(0), halo(2), pl.BlockSpec((8, tc), lambda j, i: (0, j))],
        out_specs=pl.BlockSpec((ts, tc), lambda j, i: (i, j)), out_shape=_sds((s_len, e), BF16),
        scratch_shapes=[pltpu.VMEM((C_HALO + ts, tc), F32), pltpu.VMEM((ts, tc), F32)],
        compiler_params=_params(16 * ts * tc * 4),
    )(proj, proj, proj, proj, proj, proj, w_pad)


def _c_mid_bwd(dgated, proj, conv_w, *, name, ts=256, tc=512):
    s_len, e4 = proj.shape
    e = e4 // 4
    k_width = conv_w.shape[0]
    ts, tc = _tile(s_len, ts), _tile(e, tc)
    nc, nr = e // tc, s_len // ts
    hb = ts // C_HALO

    def body(u, bg, cg, z, uh, cgh, dg, dgh, bgh, zh, w_ref, dp_ref, dw_ref, buf_p, buf_d, y, dpv, acc):
        i = pl.program_id(1)

        @pl.when(i == 0)
        def _():
            acc[...] = jnp.zeros_like(acc)

        uv, bgv, cgv, zv, dgv = u[...], bg[...], cg[...], z[...], dg[...]
        buf_p[0:C_HALO, :] = jnp.where(i > 0, uh[...] * cgh[...], 0.0)
        buf_p[C_HALO:C_HALO + ts, :] = uv * cgv
        sz = _silu(zv)
        buf_d[0:ts, :] = dgv * sz * bgv
        buf_d[ts:ts + C_HALO, :] = jnp.where(i < nr - 1, dgh[...] * _silu(zh[...]) * bgh[...], 0.0)

        def init(rows, cs, shape):
            return jnp.zeros(shape, F32)

        def emit_y(rows, cs, a):
            y[rows, cs] = a

        def emit_dp(rows, cs, a):
            dpv[rows, cs] = a

        _conv_taps(buf_p, w_ref, k_width, C_HALO - (k_width - 1), ts, tc, init, emit_y)
        _conv_taps(buf_d, w_ref, k_width, 0, ts, tc, init, emit_dp, reverse=True)
        _conv_wgrad(buf_p, buf_d, acc, k_width, C_HALO - (k_width - 1), ts, tc)
        yv, dp = y[...], dpv[...]
        dp_ref[0] = (dp * cgv).astype(dp_ref.dtype)
        dp_ref[1] = (dgv * sz * yv).astype(dp_ref.dtype)
        dp_ref[2] = (dp * uv).astype(dp_ref.dtype)
        dp_ref[3] = (dgv * bgv * yv * _dsilu(zv)).astype(dp_ref.dtype)

        @pl.when(i == nr - 1)
        def _():
            for k in range(8):
                dw_ref[k:k + 1, :] = jnp.sum(acc[8 * k:8 * k + 8, :], axis=0, keepdims=True)

    def grp(g):
        return pl.BlockSpec((ts, tc), lambda j, i: (i, g * nc + j))

    def prev(g):
        return pl.BlockSpec((C_HALO, tc), lambda j, i: (jnp.maximum(i * hb - 1, 0), g * nc + j))

    def nxt(g):
        return pl.BlockSpec((C_HALO, tc), lambda j, i: (jnp.minimum((i + 1) * hb, nr * hb - 1), g * nc + j))

    w_pad = jnp.zeros((8, e), F32).at[:k_width].set(conv_w)
    dproj, dw = pl.pallas_call(
        body, name=name, grid=(nc, nr),
        in_specs=[grp(0), grp(1), grp(2), grp(3), prev(0), prev(2),
                  pl.BlockSpec((ts, tc), lambda j, i: (i, j)),
                  pl.BlockSpec((C_HALO, tc), lambda j, i: (jnp.minimum((i + 1) * hb, nr * hb - 1), j)),
                  nxt(1), nxt(3), pl.BlockSpec((8, tc), lambda j, i: (0, j))],
        out_specs=[pl.BlockSpec((4, ts, tc), lambda j, i: (0, i, j)), pl.BlockSpec((8, tc), lambda j, i: (0, j))],
        out_shape=[_sds((4, s_len, e), BF16), _sds((8, e), F32)],
        scratch_shapes=[pltpu.VMEM((C_HALO + ts, tc), F32), pltpu.VMEM((ts + C_HALO, tc), F32),
                        pltpu.VMEM((ts, tc), F32), pltpu.VMEM((ts, tc), F32), pltpu.VMEM((64, tc), F32)],
        compiler_params=_params(32 * ts * tc * 4),
    )(proj, proj, proj, proj, proj, proj, dgated, dgated, proj, proj, w_pad)
    return dproj, dw[:k_width]


def _head_rms(xv, g):
    r = lax.rsqrt(jnp.mean(xv * xv, axis=-1, keepdims=True) + NORM_EPS)
    return r, xv * r * g


def _b_qk_fwd(proj, gq, gk, *, name, ts=512, tc=512):
    s_len, e4 = proj.shape
    e = e4 // 4
    ts, tc = _tile(s_len, ts), _tile(e, tc)
    nc = e // tc

    def body(q, k, v, gq_ref, gk_ref, qn, kn, vb):
        for h in range(tc // HEAD_DIM):
            cs = slice(h * HEAD_DIM, (h + 1) * HEAD_DIM)
            qn[:, cs] = _head_rms(q[:, cs], gq_ref[...])[1].astype(qn.dtype)
            kn[:, cs] = _head_rms(k[:, cs], gk_ref[...])[1].astype(kn.dtype)
        vb[...] = v[...].astype(vb.dtype)

    def grp(g):
        return pl.BlockSpec((ts, tc), lambda i, j: (i, g * nc + j))

    vec = pl.BlockSpec((1, HEAD_DIM), lambda i, j: (0, 0))
    out = pl.BlockSpec((ts, tc), lambda i, j: (i, j))
    return pl.pallas_call(
        body, name=name, grid=(s_len // ts, nc), in_specs=[grp(0), grp(1), grp(2), vec, vec],
        out_specs=[out, out, out], out_shape=[_sds((s_len, e), BF16)] * 3,
        compiler_params=_params(16 * ts * tc * 4),
    )(proj, proj, proj, gq, gk)


def _b_qk_bwd(dqn, dkn, dv, dz, proj, gq, gk, *, name, ts=512, tc=512):
    s_len, e4 = proj.shape
    e = e4 // 4
    ts, tc = _tile(s_len, ts), _tile(e, tc)
    nc, nr = e // tc, s_len // ts

    def body(dq_ref, dk_ref, dv_ref, dz_ref, q, k, gq_ref, gk_ref, dp_ref, dgq_ref, dgk_ref, a_q, a_k):
        i, j = pl.program_id(0), pl.program_id(1)

        @pl.when((i == 0) & (j == 0))
        def _():
            a_q[...] = jnp.zeros_like(a_q)
            a_k[...] = jnp.zeros_like(a_k)

        for h in range(tc // HEAD_DIM):
            cs = slice(h * HEAD_DIM, (h + 1) * HEAD_DIM)
            for slot, src, d_ref, g_ref, acc in ((0, q, dq_ref, gq_ref, a_q), (1, k, dk_ref, gk_ref, a_k)):
                xv = src[:, cs]
                dy = d_ref[:, cs]
                r = lax.rsqrt(jnp.mean(xv * xv, axis=-1, keepdims=True) + NORM_EPS)
                gd = dy * g_ref[...]
                dot = jnp.mean(xv * gd, axis=-1, keepdims=True)
                dp_ref[slot, :, cs] = (r * gd - xv * (r * r * r * dot)).astype(dp_ref.dtype)
                acc[...] += _rowsum8(dy * (xv * r))
        dp_ref[2] = dv_ref[...].astype(dp_ref.dtype)
        dp_ref[3] = dz_ref[...]

        @pl.when((i == nr - 1) & (j == nc - 1))
        def _():
            dgq_ref[...] = jnp.sum(a_q[...], axis=0, keepdims=True)
            dgk_ref[...] = jnp.sum(a_k[...], axis=0, keepdims=True)

    blk = pl.BlockSpec((ts, tc), lambda i, j: (i, j))
    vec = pl.BlockSpec((1, HEAD_DIM), lambda i, j: (0, 0))

    def grp(g):
        return pl.BlockSpec((ts, tc), lambda i, j: (i, g * nc + j))

    return pl.pallas_call(
        body, name=name, grid=(nr, nc), in_specs=[blk, blk, blk, blk, grp(0), grp(1), vec, vec],
        out_specs=[pl.BlockSpec((4, ts, tc), lambda i, j: (0, i, j)), vec, vec],
        out_shape=[_sds((4, s_len, e), BF16), _sds((1, HEAD_DIM), F32), _sds((1, HEAD_DIM), F32)],
        scratch_shapes=[pltpu.VMEM((8, HEAD_DIM), F32)] * 2,
        compiler_params=_params(24 * ts * tc * 4),
    )(dqn, dkn, dv, dz, proj, proj, gq, gk)


def _log_sigmoid(x):
    y = jnp.exp(-jnp.abs(x))
    u = 1.0 + y
    log1p = jnp.where(u == 1.0, y, jnp.log(u) * (y / jnp.where(u == 1.0, 1.0, u - 1.0)))
    return jnp.minimum(x, 0.0) - log1p


def _split3(v):
    hi = v.astype(BF16)
    r1 = v - hi.astype(F32)
    mid = r1.astype(BF16)
    lo = (r1 - mid.astype(F32)).astype(BF16)
    return hi, mid, lo


def _tri_matmul(tri, v):
    hi, mid, lo = _split3(v)
    return (jnp.dot(tri, hi, preferred_element_type=F32) + jnp.dot(tri, mid, preferred_element_type=F32)
            + jnp.dot(tri, lo, preferred_element_type=F32))


def _b_cumsum(fl, bias, *, name, t=512):
    s_len, w = fl.shape
    t = _tile(s_len, t)

    def body(fl_ref, b_ref, c_ref, carry):
        @pl.when(pl.program_id(0) == 0)
        def _():
            carry[...] = jnp.zeros_like(carry)

        logf = _log_sigmoid(fl_ref[...] + b_ref[...])
        row = lax.broadcasted_iota(jnp.int32, (t, t), 0)
        col = lax.broadcasted_iota(jnp.int32, (t, t), 1)
        tri = jnp.where(col <= row, 1.0, 0.0).astype(BF16)
        c = _tri_matmul(tri, logf) + carry[...]
        c_ref[...] = c
        carry[...] = c[t - 1:t, :]

    return pl.pallas_call(
        body, name=name, grid=(s_len // t,),
        in_specs=[pl.BlockSpec((t, w), lambda i: (i, 0)), pl.BlockSpec((1, w), lambda i: (0, 0))],
        out_specs=pl.BlockSpec((t, w), lambda i: (i, 0)), out_shape=_sds((s_len, w), F32),
        scratch_shapes=[pltpu.VMEM((1, w), F32)], compiler_params=_params(16 << 20),
    )(fl, bias)


def _b_cumsum_bwd(dc, fl, bias, *, name, t=512):
    s_len, w = fl.shape
    t = _tile(s_len, t)
    n = s_len // t

    def body(dc_ref, fl_ref, b_ref, dfl_ref, db_ref, carry, acc):
        i = pl.program_id(0)

        @pl.when(i == 0)
        def _():
            carry[...] = jnp.zeros_like(carry)
            acc[...] = jnp.zeros_like(acc)

        row = lax.broadcasted_iota(jnp.int32, (t, t), 0)
        col = lax.broadcasted_iota(jnp.int32, (t, t), 1)
        tri = jnp.where(col >= row, 1.0, 0.0).astype(BF16)
        dlogf = _tri_matmul(tri, dc_ref[...]) + carry[...]
        carry[...] = dlogf[0:1, :]
        dfl = dlogf * _sigmoid(-(fl_ref[...] + b_ref[...]))
        dfl_ref[...] = dfl
        acc[...] += _rowsum8(dfl)

        @pl.when(i == n - 1)
        def _():
            db_ref[...] = jnp.sum(acc[...], axis=0, keepdims=True)

    rev = pl.BlockSpec((t, w), lambda i: (n - 1 - i, 0))
    vec = pl.BlockSpec((1, w), lambda i: (0, 0))
    return pl.pallas_call(
        body, name=name, grid=(n,), in_specs=[rev, rev, vec], out_specs=[rev, vec],
        out_shape=[_sds((s_len, w), F32), _sds((1, w), F32)],
        scratch_shapes=[pltpu.VMEM((1, w), F32), pltpu.VMEM((8, w), F32)], compiler_params=_params(16 << 20),
    )(dc, fl, bias)


def _b_gate_fwd(o, proj, *, name, ts=512, tc=512):
    s_len, e = o.shape
    ts, tc = _tile(s_len, ts), _tile(e, tc)
    nc = e // tc

    def body(o_ref, z_ref, g_ref):
        g_ref[...] = (o_ref[...] * _silu(z_ref[...])).astype(g_ref.dtype)

    blk = pl.BlockSpec((ts, tc), lambda i, j: (i, j))
    return pl.pallas_call(
        body, name=name, grid=(s_len // ts, nc),
        in_specs=[blk, pl.BlockSpec((ts, tc), lambda i, j: (i, 3 * nc + j))], out_specs=blk,
        out_shape=_sds((s_len, e), BF16), compiler_params=_params(12 * ts * tc * 4),
    )(o, proj)


def _b_gate_bwd(dgated, o, o_lo, proj, *, name, hg, ts=512):
    s_len, e = o.shape
    ts = _tile(s_len, ts)
    w = hg * HEAD_DIM
    ng = e // w

    def body(dg_ref, o_ref, olo_ref, z_ref, do_ref, dz_ref, dl_ref):
        dgt, ov, zv = dg_ref[...], o_ref[...], z_ref[...]
        dob = (dgt * _silu(zv)).astype(do_ref.dtype)
        do_ref[...] = dob
        dz_ref[...] = (dgt * ov * _dsilu(zv)).astype(dz_ref.dtype)
        prod = dob.astype(F32) * (ov + olo_ref[...])
        lane = lax.broadcasted_iota(jnp.int32, (ts, LANES), 1)
        d = jnp.zeros((ts, LANES), F32)
        for hh in range(hg):
            d = jnp.where(lane == hh, jnp.sum(prod[:, hh * HEAD_DIM:(hh + 1) * HEAD_DIM], axis=-1, keepdims=True), d)
        dl_ref[...] = d

    blk = pl.BlockSpec((ts, w), lambda i, j: (i, j))
    return pl.pallas_call(
        body, name=name, grid=(s_len // ts, ng),
        in_specs=[blk, blk, blk, pl.BlockSpec((ts, w), lambda i, j: (i, 3 * ng + j))],
        out_specs=[blk, blk, pl.BlockSpec((ts, LANES), lambda i, j: (i, j))],
        out_shape=[_sds((s_len, e), BF16), _sds((s_len, e), BF16), _sds((s_len, ng * LANES), F32)],
        compiler_params=_params(20 * ts * w * 4),
    )(dgated, o, o_lo, proj)


def _attn_scores(q, k, cc, cr, diag, t, scale):
    s = lax.dot_general(q, k, (((1,), (1,)), ((), ())), preferred_element_type=F32) * scale
    s = s + cc - cr
    if diag:
        row = lax.broadcasted_iota(jnp.int32, (t, t), 0)
        col = lax.broadcasted_iota(jnp.int32, (t, t), 1)
        s = jnp.where(col <= row, s, -jnp.inf)
    return s


def _fox_fwd(qn, kn, vb, c_col, c_row, *, name, hg, t=512):
    s_len, e = qn.shape
    t = _tile(s_len, t)
    w = hg * HEAD_DIM
    ng, n = e // w, s_len // t
    scale = HEAD_DIM ** -0.5

    def body(q_ref, k_ref, v_ref, cc_ref, cr_ref, o_ref, olo_ref, lse_ref, m_s, l_s, acc_s, lo_s):
        i, j = pl.program_id(1), pl.program_id(2)

        @pl.when(j == 0)
        def _():
            m_s[...] = jnp.full_like(m_s, -jnp.inf)
            l_s[...] = jnp.zeros_like(l_s)
            acc_s[...] = jnp.zeros_like(acc_s)
            lo_s[...] = jnp.zeros_like(lo_s)

        def step(diag):
            for hh in range(hg):
                cs = slice(hh * HEAD_DIM, (hh + 1) * HEAD_DIM)
                s = _attn_scores(q_ref[:, cs], k_ref[:, cs], cc_ref[:, hh:hh + 1], cr_ref[hh], diag, t, scale)
                m_prev = m_s[hh]
                m_new = jnp.maximum(m_prev, jnp.max(s, axis=-1, keepdims=True))
                alpha = jnp.exp(m_prev - m_new)
                p = jnp.exp(s - m_new)
                l_s[hh] = alpha * l_s[hh] + jnp.sum(p, axis=-1, keepdims=True)
                p_hi = p.astype(BF16)
                p_lo = (p - p_hi.astype(F32)).astype(BF16)
                acc_s[:, cs] = alpha * acc_s[:, cs] + jnp.dot(p_hi, v_ref[:, cs], preferred_element_type=F32)
                lo_s[:, cs] = alpha * lo_s[:, cs] + jnp.dot(p_lo, v_ref[:, cs], preferred_element_type=F32)
                m_s[hh] = m_new

        @pl.when(j < i)
        def _():
            step(False)

        @pl.when(j == i)
        def _():
            step(True)
            lane = lax.broadcasted_iota(jnp.int32, (t, LANES), 1)
            lse = jnp.zeros((t, LANES), F32)
            for hh in range(hg):
                cs = slice(hh * HEAD_DIM, (hh + 1) * HEAD_DIM)
                o_ref[:, cs] = acc_s[:, cs] / l_s[hh]
                olo_ref[:, cs] = lo_s[:, cs] / l_s[hh]
                lse = jnp.where(lane == hh, m_s[hh] + jnp.log(l_s[hh]), lse)
            lse_ref[...] = lse

    qspec = pl.BlockSpec((t, w), lambda g, i, j: (i, g))
    kspec = pl.BlockSpec((t, w), lambda g, i, j: (jnp.minimum(j, i), g))
    stat = pl.BlockSpec((t, LANES), lambda g, i, j: (i, g))
    crow = pl.BlockSpec((hg, 1, t), lambda g, i, j: (g, 0, jnp.minimum(j, i)))
    return pl.pallas_call(
        body, name=name, grid=(ng, n, n), in_specs=[qspec, kspec, kspec, stat, crow],
        out_specs=[qspec, qspec, stat],
        out_shape=[_sds((s_len, e), F32), _sds((s_len, e), F32), _sds((s_len, ng * LANES), F32)],
        scratch_shapes=[pltpu.VMEM((hg, t, 1), F32), pltpu.VMEM((hg, t, 1), F32), pltpu.VMEM((t, w), F32),
                        pltpu.VMEM((t, w), F32)],
        compiler_params=_params(12 * t * t * 4 + 24 * t * w * 4),
    )(qn, kn, vb, c_col, c_row)


def _fox_bwd(qn, kn, vb, do, lse, delta, c_col, c_row, *, name, hg, t=512):
    s_len, e = qn.shape
    t = _tile(s_len, t)
    w = hg * HEAD_DIM
    ng, n = e // w, s_len // t
    scale = HEAD_DIM ** -0.5

    def body(q_ref, k_ref, v_ref, do_ref, lse_ref, dl_ref, cc_ref, cr_ref, dq_ref, dk_ref, dv_ref, dc_ref,
             dk_s, dv_s, dc_s):
        j, i = pl.program_id(1), pl.program_id(2)

        @pl.when((j == 0) & (i == 0))
        def _():
            dq_ref[...] = jnp.zeros_like(dq_ref)

        @pl.when(i == j)
        def _():
            dk_s[...] = jnp.zeros_like(dk_s)
            dv_s[...] = jnp.zeros_like(dv_s)
            dc_s[...] = jnp.zeros_like(dc_s)

        def step(diag):
            rows = pl.ds(pl.multiple_of(i * t, t), t)
            for hh in range(hg):
                cs = slice(hh * HEAD_DIM, (hh + 1) * HEAD_DIM)
                q, k, v, dov = q_ref[:, cs], k_ref[:, cs], v_ref[:, cs], do_ref[:, cs]
                s = _attn_scores(q, k, cc_ref[:, hh:hh + 1], cr_ref[hh], diag, t, scale)
                p = jnp.exp(s - lse_ref[:, hh:hh + 1])
                dv_s[:, cs] += lax.dot_general(p.astype(BF16), dov, (((0,), (0,)), ((), ())), preferred_element_type=F32)
                dp = lax.dot_general(dov, v, (((1,), (1,)), ((), ())), preferred_element_type=F32)
                ds = p * (dp - dl_ref[:, hh:hh + 1])
                dc_s[hh] -= jnp.sum(ds, axis=0, keepdims=True)
                dsb = (ds * scale).astype(BF16)
                dk_s[:, cs] += lax.dot_general(dsb, q, (((0,), (0,)), ((), ())), preferred_element_type=F32)
                dq_ref[rows, cs] += jnp.dot(dsb, k, preferred_element_type=F32)

        @pl.when(i > j)
        def _():
            step(False)

        @pl.when(i == j)
        def _():
            step(True)

        @pl.when(i == n - 1)
        def _():
            dk_ref[...] = dk_s[...]
            dv_ref[...] = dv_s[...]
            dc_ref[...] = dc_s[...]

    qspec = pl.BlockSpec((t, w), lambda g, j, i: (jnp.maximum(i, j), g))
    kspec = pl.BlockSpec((t, w), lambda g, j, i: (j, g))
    stat = pl.BlockSpec((t, LANES), lambda g, j, i: (jnp.maximum(i, j), g))
    crow = pl.BlockSpec((hg, 1, t), lambda g, j, i: (g, 0, j))
    return pl.pallas_call(
        body, name=name, grid=(ng, n, n), in_specs=[qspec, kspec, kspec, qspec, stat, stat, stat, crow],
        out_specs=[pl.BlockSpec((s_len, w), lambda g, j, i: (0, g)), kspec, kspec, crow],
        out_shape=[_sds((s_len, e), F32), _sds((s_len, e), F32), _sds((s_len, e), F32), _sds((e // HEAD_DIM, 1, s_len), F32)],
        scratch_shapes=[pltpu.VMEM((t, w), F32), pltpu.VMEM((t, w), F32), pltpu.VMEM((hg, 1, t), F32)],
        compiler_params=_params(2 * s_len * w * 4 + 16 * t * t * 4 + 24 * t * w * 4),
    )(qn, kn, vb, do, lse, delta, c_col, c_row)


def _adamw(w, gs, m, v, *, name, tr=256):
    r, c = w.shape
    tr = _tile(r, tr)
    n_g = len(gs)

    def body(*refs):
        w_ref, g_refs = refs[0], refs[1:1 + n_g]
        m_ref, v_ref, go_ref, d_ref, nm_ref, nv_ref = refs[1 + n_g:]
        gv = g_refs[0][...].astype(F32)
        for g_ref in g_refs[1:]:
            gv = gv + g_ref[...].astype(F32)
        go_ref[...] = gv
        m2 = ADAM_B1 * m_ref[...] + (1.0 - ADAM_B1) * gv
        v2 = ADAM_B2 * v_ref[...] + (1.0 - ADAM_B2) * (gv * gv)
        m_hat = m2 / (1.0 - ADAM_B1 ** ADAM_STEP)
        v_hat = v2 / (1.0 - ADAM_B2 ** ADAM_STEP)
        d_ref[...] = -ADAM_LR * (m_hat / (jnp.sqrt(v_hat) + ADAM_EPS) + ADAM_WD * w_ref[...])
        nm_ref[...] = m2
        nv_ref[...] = v2

    blk = pl.BlockSpec((tr, c), lambda i: (i, 0))
    return pl.pallas_call(
        body, name=name, grid=(r // tr,), in_specs=[blk] * (3 + n_g), out_specs=[blk] * 4,
        out_shape=[_sds((r, c), F32)] * 4, compiler_params=_params(24 * tr * c * 4),
    )(w, *gs, m, v)


def _sum_slots(a, *, name, out_dtype, tr=256):
    n, r, c = a.shape
    tr = _tile(r, tr)

    def body(a_ref, o_ref):
        acc = a_ref[0].astype(F32)
        for k in range(1, n):
            acc = acc + a_ref[k].astype(F32)
        o_ref[...] = acc.astype(o_ref.dtype)

    return pl.pallas_call(
        body, name=name, grid=(r // tr,), in_specs=[pl.BlockSpec((n, tr, c), lambda i: (0, i, 0))],
        out_specs=pl.BlockSpec((tr, c), lambda i: (i, 0)), out_shape=_sds((r, c), out_dtype),
        compiler_params=_params(4 * (n + 2) * tr * c * 4),
    )(a)


_ANY = pl.BlockSpec(memory_space=pl.ANY)
DMA_CHUNK_BYTES = 512 << 10


def _chunks(parts):
    out = []
    for src_at, dst_at, rows, row_bytes in parts:
        step = max(16, DMA_CHUNK_BYTES // row_bytes // 16 * 16)
        for r0 in range(0, rows, step):
            n = min(step, rows - r0)
            out.append((src_at(r0, n), dst_at(r0, n)))
    return out


def _row_bytes(ref):
    return ref.shape[-1] * ref.dtype.itemsize


def _me():
    return lax.axis_index("x"), lax.axis_index("y"), lax.axis_index("c")


def _chip_peers(x, y):
    return [(2 * (1 - x) + y, (1 - x, y)), (2 * x + (1 - y), (x, 1 - y)), (2 * (1 - x) + (1 - y), (1 - x, 1 - y))]


def _exchange(name, ins, out_shapes, plan):
    n_in, n_out = len(ins), len(out_shapes)

    def body(*refs):
        in_refs, out_refs = refs[:n_in], refs[n_in:n_in + n_out]
        send_sems, recv_sems, loc_sems = refs[n_in + n_out:]
        remote, local = plan(in_refs, out_refs)
        starts, waits = [], []
        for k, (ws, wd, dev, parts) in enumerate(remote):
            def mk(s, d, k=k, dev=dev):
                return pltpu.make_async_remote_copy(src_ref=s, dst_ref=d, send_sem=send_sems.at[k],
                                                    recv_sem=recv_sems.at[k], device_id=dev, device_id_type=MESH_ID)
            starts += [mk(s, d) for s, d in _chunks(parts)]
            waits.append(mk(ws, wd))
        for k, (ws, wd, _, parts) in enumerate(local):
            def mk(s, d, k=k):
                return pltpu.make_async_copy(s, d, loc_sems.at[k])
            starts += [mk(s, d) for s, d in _chunks(parts)]
            waits.append(mk(ws, wd))
        for cp in starts:
            cp.start()
        for cp in waits:
            cp.wait()

    n_remote, n_local = plan.n_remote, plan.n_local
    return pl.pallas_call(
        body, name=name, in_specs=[_ANY] * n_in, out_specs=[_ANY] * n_out, out_shape=list(out_shapes),
        scratch_shapes=[pltpu.SemaphoreType.DMA((n_remote,)), pltpu.SemaphoreType.DMA((n_remote,)),
                        pltpu.SemaphoreType.DMA((max(n_local, 1),))],
    )(*ins)


def _gather_weights(shards, *, name):
    n_t = len(shards)

    def body(*refs):
        in_refs, out_refs = refs[:n_t], refs[n_t:2 * n_t]
        send_sems, recv_sems, loc_sems = refs[2 * n_t:]
        x, y, c = _me()
        chip = 2 * x + y
        peers = _chip_peers(x, y)
        local, first, passed = [], [], []
        for t in range(n_t):
            src, dst = in_refs[t], out_refs[t]
            n_rows = src.shape[0]
            half = n_rows // 2
            rb = _row_bytes(src)
            rows = pl.ds(c * half, half)

            def mk_loc(s, d, t=t):
                return pltpu.make_async_copy(s, d, loc_sems.at[t])

            own = [(lambda r0, n, src=src: src.at[pl.ds(r0, n)],
                    lambda r0, n, dst=dst: dst.at[chip, pl.ds(r0, n)], n_rows, rb)]
            local.append((mk_loc(src, dst.at[chip]), [mk_loc(s, d) for s, d in _chunks(own)]))
            for k, (pchip, (px, py)) in enumerate(peers):
                def mk_ici(s, d, t=t, k=k, px=px, py=py):
                    return pltpu.make_async_remote_copy(
                        src_ref=s, dst_ref=d, send_sem=send_sems.at[6 * t + k], recv_sem=recv_sems.at[6 * t + k],
                        device_id=(px, py, c), device_id_type=MESH_ID)

                def mk_d2d(s, d, t=t, k=k):
                    return pltpu.make_async_remote_copy(
                        src_ref=s, dst_ref=d, send_sem=send_sems.at[6 * t + 3 + k], recv_sem=recv_sems.at[6 * t + 3 + k],
                        device_id=(x, y, 1 - c), device_id_type=MESH_ID)

                out_part = [(lambda r0, n, src=src: src.at[pl.ds(c * half + r0, n)],
                             lambda r0, n, dst=dst: dst.at[chip, pl.ds(c * half + r0, n)], half, rb)]
                fwd_part = [(lambda r0, n, dst=dst, pchip=pchip: dst.at[pchip, pl.ds(c * half + r0, n)],
                             lambda r0, n, dst=dst, pchip=pchip: dst.at[pchip, pl.ds(c * half + r0, n)], half, rb)]
                first.append((mk_ici(src.at[rows], dst.at[chip, rows]), [mk_ici(s, d) for s, d in _chunks(out_part)]))
                passed.append((mk_d2d(dst.at[pchip, rows], dst.at[pchip, rows]),
                               [mk_d2d(s, d) for s, d in _chunks(fwd_part)]))
        for _, chunk_copies in first + local:
            for cp in chunk_copies:
                cp.start()
        for (whole, _), (_, fwd_copies) in zip(first, passed):
            whole.wait_recv()
            for cp in fwd_copies:
                cp.start()
        for whole, _ in passed:
            whole.wait_recv()
        for whole, _ in first + passed:
            whole.wait_send()
        for whole, _ in local:
            whole.wait()

    outs = [_sds((N_CHIPS,) + s.shape, s.dtype) for s in shards]
    return pl.pallas_call(
        body, name=name, in_specs=[_ANY] * n_t, out_specs=[_ANY] * n_t, out_shape=outs,
        scratch_shapes=[pltpu.SemaphoreType.DMA((6 * n_t,)), pltpu.SemaphoreType.DMA((6 * n_t,)),
                        pltpu.SemaphoreType.DMA((n_t,))],
    )(*shards)


class _Plan:
    def __init__(self, fn, n_remote, n_local):
        self.fn, self.n_remote, self.n_local = fn, n_remote, n_local

    def __call__(self, in_refs, out_refs):
        return self.fn(in_refs, out_refs)


def _reduce_grads(grads, *, name):
    n_t = len(grads)

    def plan2(in_refs, out_refs):
        x, y, c = _me()
        chip = 2 * x + y
        remote, local = [], []
        for t in range(n_t):
            src, dst = in_refs[t], out_refs[t]
            rows, rb = src.shape[1], _row_bytes(src)

            def part(slot, src=src, dst=dst, rows=rows, rb=rb):
                return [(lambda r0, n: src.at[slot, pl.ds(r0, n)], lambda r0, n: dst.at[chip, pl.ds(r0, n)], rows, rb)]

            for pchip, (px, py) in _chip_peers(x, y):
                remote.append((src.at[pchip], dst.at[chip], (px, py, c), part(pchip)))
            local.append((src.at[chip], dst.at[chip], None, part(chip)))
        return remote, local

    out2 = _exchange(name + "_ici", grads, [_sds(g.shape, g.dtype) for g in grads], _Plan(plan2, 3 * n_t, n_t))
    sums = [_sum_slots(out2[t], name=f"{name}_sum{t}", out_dtype=BF16) for t in range(n_t)]

    def plan3(in_refs, out_refs):
        x, y, c = _me()
        remote = []
        for t in range(n_t):
            src, dst = in_refs[t], out_refs[t]
            rows = [(lambda r0, n, src=src: src.at[pl.ds(r0, n)], lambda r0, n, dst=dst: dst.at[pl.ds(r0, n)],
                     src.shape[0], _row_bytes(src))]
            remote.append((src, dst, (x, y, 1 - c), rows))
        return remote, []

    others = _exchange(name + "_swap", sums, [_sds(s.shape, s.dtype) for s in sums], _Plan(plan3, n_t, 0))
    return list(zip(sums, others))


def _allreduce_small(pack, *, name):
    r, w = pack.shape

    def body(p_ref, o_ref, buf, send_sems, recv_sems):
        x, y, c = _me()
        me = 4 * x + 2 * y + c
        buf[me] = p_ref[...]
        copies = []
        for k in range(1, N_DEV):
            peer = (x ^ ((k >> 2) & 1), y ^ ((k >> 1) & 1), c ^ (k & 1))
            copies.append(pltpu.make_async_remote_copy(
                src_ref=p_ref, dst_ref=buf.at[me], send_sem=send_sems.at[k - 1], recv_sem=recv_sems.at[k - 1],
                device_id=peer, device_id_type=MESH_ID))
        for cp in copies:
            cp.start()
        for cp in copies:
            cp.wait()
        acc = buf[0]
        for k in range(1, N_DEV):
            acc = acc + buf[k]
        o_ref[...] = acc

    vm = pl.BlockSpec(memory_space=pltpu.VMEM)
    return pl.pallas_call(
        body, name=name, in_specs=[vm], out_specs=vm, out_shape=_sds((r, w), F32),
        scratch_shapes=[pltpu.VMEM((N_DEV, r, w), F32), pltpu.SemaphoreType.DMA((N_DEV - 1,)),
                        pltpu.SemaphoreType.DMA((N_DEV - 1,))],
        compiler_params=_params(12 * r * w * 4),
    )(pack)


def _pack(arrs, row_multiple=16):
    flat = jnp.concatenate([a.reshape(-1).astype(F32) for a in arrs])
    unit = row_multiple * LANES
    total = -(-flat.shape[0] // unit) * unit
    return jnp.pad(flat, (0, total - flat.shape[0])).reshape(total // LANES, LANES)


def _unpack(packed, shapes):
    flat = packed.reshape(-1)
    out, off = [], 0
    for shp in shapes:
        n = int(np.prod(shp))
        out.append(flat[off:off + n].reshape(shp))
        off += n
    return out


def _pad_cols(a, width):
    return jnp.pad(a, [(0, 0)] * (a.ndim - 1) + [(0, width - a.shape[-1])])


ATTN_HEADS_PER_STEP = 2
SMALL_SHARDED = ("a_norm", "a_conv_w", "a_conv_b", "a_ln_g", "a_ln_b", "c_norm", "c_conv_w")
SMALL_REPLICATED = ("b_norm", "b_f_bias", "b_q_norm", "b_k_norm")
BIG = ("a_w_in", "a_w_out", "b_w_in", "b_w_out", "c_w_in", "c_w_out")
WEIGHTS = ("a_norm", "a_w_in", "a_conv_w", "a_conv_b", "a_ln_g", "a_ln_b", "a_w_out", "b_norm", "b_w_in", "b_f_bias",
           "b_q_norm", "b_k_norm", "b_w_out", "c_norm", "c_w_in", "c_conv_w", "c_w_out")


def _mixer_a_fwd(x, p, l, tag):
    h = _rms_fwd(x, p["a_norm"][l][None], name=f"{tag}_rms")
    proj = _mm_nn(h, p["a_w_in"][l], name=f"{tag}_in", tn=p["a_w_in"][l].shape[2])
    u1 = _a_conv_fwd(proj, p["a_conv_w"][l], p["a_conv_b"][l][None], name=f"{tag}_conv")
    gated = _a_post_fwd(u1, proj, p["a_ln_g"][l][None], p["a_ln_b"][l][None], name=f"{tag}_post")
    y = _mm_nn(gated, p["a_w_out"][l], name=f"{tag}_out", tn=1024, add=x)
    return y, (x, h, proj, u1, gated)


def _mixer_a_bwd(dx, saved, p, l, tag):
    x, h, proj, u1, gated = saved
    g = {}
    g["a_w_out"] = _mm_tn(gated, dx, name=f"{tag}_dwout", out_dtype=BF16, tk=2048, tn=1024)
    dgated = _mm_nt(dx, p["a_w_out"][l], name=f"{tag}_dgated", tn=2048, tk=1024)
    du1, dz, g["a_ln_g"], g["a_ln_b"], g["a_conv_b"] = _a_post_bwd(
        dgated, u1, proj, p["a_ln_g"][l][None], p["a_ln_b"][l][None], name=f"{tag}_dpost")
    dproj, g["a_conv_w"] = _a_conv_bwd(du1, proj, dz, p["a_conv_w"][l], name=f"{tag}_dconv")
    g["a_w_in"] = _mm_tn(h, dproj, name=f"{tag}_dwin", out_dtype=BF16,out_width=p["a_w_in"][l].shape[2])
    dh = _mm_nt(dproj, p["a_w_in"][l], name=f"{tag}_dh")
    dx, g["a_norm"] = _rms_bwd(x, p["a_norm"][l][None], [dh], dx, name=f"{tag}_drms")
    return dx, g


def _mixer_c_fwd(x, p, tag):
    h = _rms_fwd(x, p["c_norm"][0][None], name=f"{tag}_rms")
    proj = _mm_nn(h, p["c_w_in"], name=f"{tag}_in", tn=1024)
    gated = _c_mid_fwd(proj, p["c_conv_w"][0], name=f"{tag}_mid")
    y = _mm_nn(gated, p["c_w_out"], name=f"{tag}_out", tn=1024, add=x)
    return y, (x, h, proj, gated)


def _mixer_c_bwd(dx, saved, p, tag):
    x, h, proj, gated = saved
    g = {}
    g["c_w_out"] = _mm_tn(gated, dx, name=f"{tag}_dwout", out_dtype=BF16, tk=2048, tn=1024)
    dgated = _mm_nt(dx, p["c_w_out"], name=f"{tag}_dgated", tn=2048, tk=1024)
    dproj, g["c_conv_w"] = _c_mid_bwd(dgated, proj, p["c_conv_w"][0], name=f"{tag}_dmid")
    g["c_w_in"] = _mm_tn(h, dproj, name=f"{tag}_dwin", out_dtype=BF16,out_width=p["c_w_in"].shape[2], tn=1024)
    dh = _mm_nt(dproj, p["c_w_in"], name=f"{tag}_dh", tk=1024)
    dx, g["c_norm"] = _rms_bwd(x, p["c_norm"][0][None], [dh], dx, name=f"{tag}_drms")
    return dx, g


def _mixer_b_fwd(x, p, tag):
    hg = ATTN_HEADS_PER_STEP
    s_len = x.shape[0]
    n_heads = p["b_f_bias"].shape[1]
    h = _rms_fwd(x, p["b_norm"], name=f"{tag}_rms")
    proj = _mm_nn(h, p["b_wq"], name=f"{tag}_in", tn=1024)
    fl = _mm_nn(h, p["b_wf"], name=f"{tag}_inf", tn=LANES)
    qn, kn, vb = _b_qk_fwd(proj, p["b_q_norm"], p["b_k_norm"], name=f"{tag}_qk")
    bias = _pad_cols(p["b_f_bias"], LANES)
    c = _b_cumsum(fl, bias, name=f"{tag}_cumsum")
    ch = c[:, :n_heads]
    c_col = _pad_cols(ch.reshape(s_len, n_heads // hg, hg), LANES).reshape(s_len, (n_heads // hg) * LANES)
    c_row = ch.T.reshape(n_heads, 1, s_len)
    o, o_lo, lse = _fox_fwd(qn, kn, vb, c_col, c_row, name=f"{tag}_attn", hg=hg)
    gated = _b_gate_fwd(o, proj, name=f"{tag}_gate")
    y = _mm_nn(gated, p["b_w_out"], name=f"{tag}_out", tn=1024, add=x)
    return y, (x, h, proj, fl, bias, qn, kn, vb, c_col, c_row, o, o_lo, lse, gated)


def _mixer_b_bwd(dx, saved, p, tag):
    hg = ATTN_HEADS_PER_STEP
    x, h, proj, fl, bias, qn, kn, vb, c_col, c_row, o, o_lo, lse, gated = saved
    s_len = x.shape[0]
    n_heads = p["b_f_bias"].shape[1]
    g = {}
    g["b_w_out"] = _mm_tn(gated, dx, name=f"{tag}_dwout", out_dtype=BF16, tk=2048, tn=1024)
    dgated = _mm_nt(dx, p["b_w_out"], name=f"{tag}_dgated", tn=2048, tk=1024)
    do, dz, delta = _b_gate_bwd(dgated, o, o_lo, proj, name=f"{tag}_dgate", hg=hg)
    dqn, dkn, dv, dc = _fox_bwd(qn, kn, vb, do, lse, delta, c_col, c_row, name=f"{tag}_dattn", hg=hg)
    dc_pad = _pad_cols(dc.reshape(n_heads, s_len).T, LANES)
    dfl, dbias = _b_cumsum_bwd(dc_pad, fl, bias, name=f"{tag}_dcumsum")
    g["b_f_bias"] = dbias[:, :n_heads]
    dproj, g["b_q_norm"], g["b_k_norm"] = _b_qk_bwd(dqn, dkn, dv, dz, proj, p["b_q_norm"], p["b_k_norm"], name=f"{tag}_dqk")
    dwq = _mm_tn(h, dproj, name=f"{tag}_dwin", out_dtype=BF16,tn=1024)
    dwf = _mm_tn(h, dfl, name=f"{tag}_dwinf", out_dtype=BF16, tn=LANES)
    g["b_w_in"] = jnp.concatenate([dwq, dwf[:, :n_heads]], axis=1)
    dh = _mm_nt(dproj, p["b_wq"], name=f"{tag}_dh", tk=1024)
    dhf = _mm_nt(dfl, p["b_wf"], name=f"{tag}_dhf", tk=LANES)
    dx, g["b_norm"] = _rms_bwd(x, p["b_norm"], [dh, dhf], dx, name=f"{tag}_drms")
    return dx, g


def kernel(x, a_norm, a_w_in, a_conv_w, a_conv_b, a_ln_g, a_ln_b, a_w_out, b_norm, b_w_in, b_f_bias, b_q_norm, b_k_norm, b_w_out, c_norm, c_w_in, c_conv_w, c_w_out, loss_target, m_a_norm, m_a_w_in, m_a_conv_w, m_a_conv_b, m_a_ln_g, m_a_ln_b, m_a_w_out, m_b_norm, m_b_w_in, m_b_f_bias, m_b_q_norm, m_b_k_norm, m_b_w_out, m_c_norm, m_c_w_in, m_c_conv_w, m_c_w_out, v_a_norm, v_a_w_in, v_a_conv_w, v_a_conv_b, v_a_ln_g, v_a_ln_b, v_a_w_out, v_b_norm, v_b_w_in, v_b_f_bias, v_b_q_norm, v_b_k_norm, v_b_w_out, v_c_norm, v_c_w_in, v_c_conv_w, v_c_w_out):
    w_loc = dict(a_norm=a_norm, a_w_in=a_w_in, a_conv_w=a_conv_w, a_conv_b=a_conv_b, a_ln_g=a_ln_g, a_ln_b=a_ln_b,
                 a_w_out=a_w_out, b_norm=b_norm, b_w_in=b_w_in, b_f_bias=b_f_bias, b_q_norm=b_q_norm, b_k_norm=b_k_norm,
                 b_w_out=b_w_out, c_norm=c_norm, c_w_in=c_w_in, c_conv_w=c_conv_w, c_w_out=c_w_out)
    m_loc = dict(a_norm=m_a_norm, a_w_in=m_a_w_in, a_conv_w=m_a_conv_w, a_conv_b=m_a_conv_b, a_ln_g=m_a_ln_g,
                 a_ln_b=m_a_ln_b, a_w_out=m_a_w_out, b_norm=m_b_norm, b_w_in=m_b_w_in, b_f_bias=m_b_f_bias,
                 b_q_norm=m_b_q_norm, b_k_norm=m_b_k_norm, b_w_out=m_b_w_out, c_norm=m_c_norm, c_w_in=m_c_w_in,
                 c_conv_w=m_c_conv_w, c_w_out=m_c_w_out)
    v_loc = dict(a_norm=v_a_norm, a_w_in=v_a_w_in, a_conv_w=v_a_conv_w, a_conv_b=v_a_conv_b, a_ln_g=v_a_ln_g,
                 a_ln_b=v_a_ln_b, a_w_out=v_a_w_out, b_norm=v_b_norm, b_w_in=v_b_w_in, b_f_bias=v_b_f_bias,
                 b_q_norm=v_b_q_norm, b_k_norm=v_b_k_norm, b_w_out=v_b_w_out, c_norm=v_c_norm, c_w_in=v_c_w_in,
                 c_conv_w=v_c_conv_w, c_w_out=v_c_w_out)
    n_a = a_w_in.shape[0]
    d_model = x.shape[2]
    e_inner = a_w_out.shape[1] * N_CHIPS
    n_heads = b_f_bias.shape[1]
    nb_loc = b_w_in.shape[2]
    nb_pad = -(-nb_loc // LANES) * LANES
    chip = 2 * lax.axis_index("x") + lax.axis_index("y")

    big_shards = ([a_w_in[l].astype(BF16) for l in range(n_a)] + [a_w_out[l].astype(BF16) for l in range(n_a)]
                  + [_pad_cols(b_w_in[0], nb_pad).astype(BF16), b_w_out[0].astype(BF16), c_w_in[0].astype(BF16),
                     c_w_out[0].astype(BF16)])
    small_pack = _pack([w_loc[n] for n in SMALL_SHARDED])
    gathered = _gather_weights(big_shards + [small_pack], name="gather_weights")
    p = {}
    p["a_w_in"] = gathered[0:n_a]
    p["a_w_out"] = [g.reshape(e_inner, d_model) for g in gathered[n_a:2 * n_a]]
    gb, gbo, gci, gco, gsmall = gathered[2 * n_a:]
    wb_full = jnp.concatenate([gb[k, :, :nb_loc] for k in range(N_CHIPS)], axis=1)
    p["b_wq"] = wb_full[:, :4 * e_inner]
    p["b_wf"] = _pad_cols(wb_full[:, 4 * e_inner:], LANES)
    p["b_w_out"] = gbo.reshape(e_inner, d_model)
    p["c_w_in"] = gci
    p["c_w_out"] = gco.reshape(e_inner, d_model)
    small_shapes = [w_loc[n].shape for n in SMALL_SHARDED]
    per_chip = [_unpack(gsmall[k], small_shapes) for k in range(N_CHIPS)]
    for idx, n in enumerate(SMALL_SHARDED):
        p[n] = jnp.concatenate([per_chip[k][idx] for k in range(N_CHIPS)], axis=-1)
    for n in SMALL_REPLICATED:
        p[n] = w_loc[n]

    x0 = x[0]
    x1, sv0 = _mixer_a_fwd(x0, p, 0, "a0")
    x2, sv1 = _mixer_b_fwd(x1, p, "b0")
    x3, sv2 = _mixer_c_fwd(x2, p, "c0")
    x4, sv3 = _mixer_a_fwd(x3, p, 1, "a1")
    dy, loss_part = _loss_head(x4, loss_target[0], name="loss_head")
    loss = lax.psum(loss_part[0, 0], ("x", "y", "c"))

    dx, g3 = _mixer_a_bwd(dy, sv3, p, 1, "a1")
    dx, g2 = _mixer_c_bwd(dx, sv2, p, "c0")
    dx, g1 = _mixer_b_bwd(dx, sv1, p, "b0")
    dx, g0 = _mixer_a_bwd(dx, sv0, p, 0, "a0")
    grad_x = dx[None]

    half_rows = e_inner // N_CHIPS
    gb_full = g1["b_w_in"].reshape(d_model, N_CHIPS, nb_loc).transpose(1, 0, 2)
    big_grads = ([g0["a_w_in"], g3["a_w_in"]]
                 + [g0["a_w_out"].reshape(N_CHIPS, half_rows, d_model), g3["a_w_out"].reshape(N_CHIPS, half_rows, d_model)]
                 + [_pad_cols(gb_full, nb_pad), g1["b_w_out"].reshape(N_CHIPS, half_rows, d_model), g2["c_w_in"],
                    g2["c_w_out"].reshape(N_CHIPS, half_rows, d_model)])
    red = _reduce_grads(big_grads, name="reduce_grads")
    big_order = [("a_w_in", 0), ("a_w_in", 1), ("a_w_out", 0), ("a_w_out", 1), ("b_w_in", 0), ("b_w_out", 0),
                 ("c_w_in", 0), ("c_w_out", 0)]

    grads, delta, new_m, new_v = {}, {}, {}, {}
    per_layer = {n: [] for n in BIG}
    for (n, l), pair in zip(big_order, red):
        cols = pair[0].shape[1]
        true_cols = w_loc[n].shape[-1]
        outs = _adamw(_pad_cols(w_loc[n][l], cols), list(pair), _pad_cols(m_loc[n][l], cols), _pad_cols(v_loc[n][l], cols),
                      name=f"adamw_{n}{l}")
        per_layer[n].append([a[:, :true_cols] for a in outs])
    for n in BIG:
        grads[n], delta[n], new_m[n], new_v[n] = [jnp.stack([layer[k] for layer in per_layer[n]]) for k in range(4)]

    small_full = {}
    for n in ("a_norm", "a_conv_w", "a_conv_b", "a_ln_g", "a_ln_b"):
        small_full[n] = jnp.stack([g0[n].reshape(p[n].shape[1:]), g3[n].reshape(p[n].shape[1:])])
    small_full["c_norm"] = g2["c_norm"].reshape(p["c_norm"].shape)
    small_full["c_conv_w"] = g2["c_conv_w"].reshape(p["c_conv_w"].shape)
    for n in SMALL_REPLICATED:
        small_full[n] = g1[n].reshape(w_loc[n].shape)
    small_names = SMALL_SHARDED + SMALL_REPLICATED
    summed = _unpack(_allreduce_small(_pack([small_full[n] for n in small_names], 8), name="reduce_small"),
                     [small_full[n].shape for n in small_names])
    for n, s in zip(small_names, summed):
        if n in SMALL_SHARDED:
            width = w_loc[n].shape[-1]
            grads[n] = lax.dynamic_slice_in_dim(s, chip * width, width, axis=s.ndim - 1)
        else:
            grads[n] = s

    small_shapes_all = [w_loc[n].shape for n in small_names]
    _, d_, m_, v_ = _adamw(_pack([w_loc[n] for n in small_names], 8), [_pack([grads[n] for n in small_names], 8)],
                           _pack([m_loc[n] for n in small_names], 8), _pack([v_loc[n] for n in small_names], 8),
                           name="adamw_small")
    for n, a, b, c_ in zip(small_names, _unpack(d_, small_shapes_all), _unpack(m_, small_shapes_all),
                           _unpack(v_, small_shapes_all)):
        delta[n], new_m[n], new_v[n] = a, b, c_

    return (loss, grad_x, *[grads[n] for n in WEIGHTS], *[delta[n] for n in WEIGHTS],
            *[new_m[n] for n in WEIGHTS], *[new_v[n] for n in WEIGHTS])
```

```python
import functools

import numpy as np
import jax
import jax.numpy as jnp
from jax import lax
from jax.experimental import pallas as pl
from jax.experimental.pallas import tpu as pltpu

F32 = jnp.float32
BF16 = jnp.bfloat16
NORM_EPS = 1e-6
HEAD_DIM = 128
LANES = 128
N_CHIPS = 4
N_DEV = 8
VMEM_CAP = 56 << 20
MESH_ID = pl.DeviceIdType.MESH

ADAM_LR = 0.001
ADAM_B1 = 0.9
ADAM_B2 = 0.999
ADAM_EPS = 1e-08
ADAM_WD = 0.01
ADAM_STEP = 10


def _sds(shape, dtype):
    return jax.ShapeDtypeStruct(tuple(shape), dtype)


def _tile(n, pref):
    if n <= pref:
        return n
    for t in range(pref - pref % 8, 0, -8):
        if n % t == 0:
            return t
    raise ValueError(f"no tile for {n} under {pref}")


def _params(vmem_bytes):
    return pltpu.CompilerParams(vmem_limit_bytes=int(min(max(vmem_bytes, 16 << 20), VMEM_CAP)))


def _sigmoid(v):
    return 1.0 / (1.0 + jnp.exp(-v))


def _silu(v):
    return v * _sigmoid(v)


def _dsilu(v):
    s = _sigmoid(v)
    return s * (1.0 + v * (1.0 - s))


def _rowsum8(v):
    r, c = v.shape
    return jnp.sum(v.reshape(r // 8, 8, c), axis=0)


def _col_spec(grouped, rows_block, tile, width, row_of, col_of):
    if grouped:
        per = width // tile
        return pl.BlockSpec((None, rows_block, tile), lambda *ids: (col_of(*ids) // per, row_of(*ids), col_of(*ids) % per))
    return pl.BlockSpec((rows_block, tile), lambda *ids: (row_of(*ids), col_of(*ids)))


def _mm_nn(a, b, *, name, tm=1024, tn=512, out_dtype=F32, add=None):
    m, k = a.shape
    b_grouped = b.ndim == 3
    n = b.shape[0] * b.shape[2] if b_grouped else b.shape[1]
    width = b.shape[2] if b_grouped else n
    tm, tn = _tile(m, tm), _tile(width, tn)

    def body(*refs):
        if add is None:
            a_ref, b_ref, o_ref = refs
        else:
            a_ref, b_ref, r_ref, o_ref = refs
        acc = jnp.dot(a_ref[...].astype(BF16), b_ref[...].astype(BF16), preferred_element_type=F32)
        if add is not None:
            acc = acc + r_ref[...]
        o_ref[...] = acc.astype(o_ref.dtype)

    in_specs = [pl.BlockSpec((tm, k), lambda i, j: (i, 0)),
                _col_spec(b_grouped, k, tn, width, lambda i, j: 0, lambda i, j: j)]
    args = [a, b]
    if add is not None:
        in_specs.append(pl.BlockSpec((tm, tn), lambda i, j: (i, j)))
        args.append(add)
    vmem = 2 * (tm * k * a.dtype.itemsize + k * tn * 2 + tm * tn * 4 * (2 if add is not None else 1)) + tm * tn * 8 + tm * k * 2
    return pl.pallas_call(
        body, name=name, grid=(m // tm, n // tn), in_specs=in_specs,
        out_specs=pl.BlockSpec((tm, tn), lambda i, j: (i, j)),
        out_shape=_sds((m, n), out_dtype), compiler_params=_params(vmem + (4 << 20)),
    )(*args)


def _mm_nt(a, b, *, name, tm=1024, tn=1024, tk=512, ksub=1):
    a_grouped, b_grouped = a.ndim == 3, b.ndim == 3
    m = a.shape[1] if a_grouped else a.shape[0]
    n = a.shape[0] * a.shape[2] if a_grouped else a.shape[1]
    kk = b.shape[1] if b_grouped else b.shape[0]
    wa = a.shape[2] if a_grouped else n
    wb = b.shape[2] if b_grouped else n
    tm, tn = _tile(m, tm), _tile(kk, tn)
    tk = _tile(int(np.gcd(wa, wb)), tk)
    ksub = min(ksub, n // tk)
    assert (n // tk) % ksub == 0, (n, tk, ksub)
    steps = n // (tk * ksub)

    def body(*refs):
        a_refs, b_refs, o_ref = refs[:ksub], refs[ksub:2 * ksub], refs[2 * ksub]
        part = None
        for a_ref, b_ref in zip(a_refs, b_refs):
            d = lax.dot_general(a_ref[...].astype(BF16), b_ref[...].astype(BF16), (((1,), (1,)), ((), ())),
                                preferred_element_type=F32)
            part = d if part is None else part + d

        @pl.when(pl.program_id(2) == 0)
        def _():
            o_ref[...] = part

        @pl.when(pl.program_id(2) > 0)
        def _():
            o_ref[...] += part

    def sub(u):
        return lambda i, j, s: s * ksub + u

    in_specs = ([_col_spec(a_grouped, tm, tk, wa, lambda i, j, s: i, sub(u)) for u in range(ksub)]
                + [_col_spec(b_grouped, tn, tk, wb, lambda i, j, s: j, sub(u)) for u in range(ksub)])
    vmem = (2 * ksub * (tm * tk * a.dtype.itemsize + tn * tk * b.dtype.itemsize) + 2 * tm * tn * 4 + 2 * tm * tn * 4
            + (tm + tn) * tk * 2)
    return pl.pallas_call(
        body, name=name, grid=(m // tm, kk // tn, steps), in_specs=in_specs,
        out_specs=pl.BlockSpec((tm, tn), lambda i, j, s: (i, j)),
        out_shape=_sds((m, kk), F32), compiler_params=_params(vmem + (4 << 20)),
    )(*([a] * ksub), *([b] * ksub))


def _mm_tn(a, b, *, name, out_width=None, out_dtype=F32, tk=1024, tn=512, ts=512):
    s_len, k = a.shape
    b_grouped = b.ndim == 3
    n = b.shape[0] * b.shape[2] if b_grouped else b.shape[1]
    wb = b.shape[2] if b_grouped else n
    wo = out_width if out_width is not None else n
    tk, ts = _tile(k, tk), _tile(s_len, ts)
    tn = _tile(int(np.gcd(wb, wo)), tn)
    last = s_len // ts - 1
    direct = out_dtype == F32

    def body(a_ref, b_ref, o_ref, *scratch):
        acc = o_ref if direct else scratch[0]
        part = lax.dot_general(a_ref[...].astype(BF16), b_ref[...].astype(BF16), (((0,), (0,)), ((), ())),
                               preferred_element_type=F32)

        @pl.when(pl.program_id(2) == 0)
        def _():
            acc[...] = part

        @pl.when(pl.program_id(2) > 0)
        def _():
            acc[...] += part

        if not direct:
            @pl.when(pl.program_id(2) == last)
            def _():
                o_ref[...] = acc[...].astype(o_ref.dtype)

    in_specs = [pl.BlockSpec((ts, tk), lambda i, j, s: (s, i)),
                _col_spec(b_grouped, ts, tn, wb, lambda i, j, s: s, lambda i, j, s: j)]
    out_grouped = out_width is not None
    out_spec = _col_spec(out_grouped, tk, tn, wo, lambda i, j, s: i, lambda i, j, s: j)
    out_shape = _sds((n // wo, k, wo), out_dtype) if out_grouped else _sds((k, n), out_dtype)
    vmem = 2 * (ts * tk * a.dtype.itemsize + ts * tn * b.dtype.itemsize + tk * tn * 4) + 2 * tk * tn * 4 + ts * (tk + tn) * 4
    return pl.pallas_call(
        body, name=name, grid=(k // tk, n // tn, s_len // ts), in_specs=in_specs, out_specs=out_spec,
        out_shape=out_shape, scratch_shapes=[] if direct else [pltpu.VMEM((tk, tn), F32)],
        compiler_params=_params(vmem + (4 << 20)),
    )(a, b)


def _rms_fwd(x, g, *, name, ts=512):
    s_len, d = x.shape
    ts = _tile(s_len, ts)

    def body(x_ref, g_ref, h_ref):
        xf = x_ref[...]
        r = lax.rsqrt(jnp.mean(xf * xf, axis=-1, keepdims=True) + NORM_EPS)
        h_ref[...] = ((xf * r) * g_ref[...]).astype(h_ref.dtype)

    return pl.pallas_call(
        body, name=name, grid=(s_len // ts,),
        in_specs=[pl.BlockSpec((ts, d), lambda i: (i, 0)), pl.BlockSpec((1, d), lambda i: (0, 0))],
        out_specs=pl.BlockSpec((ts, d), lambda i: (i, 0)), out_shape=_sds((s_len, d), BF16),
        compiler_params=_params(8 * ts * d * 4),
    )(x, g)


def _rms_bwd(x, g, dhs, dres, *, name, ts=512):
    s_len, d = x.shape
    ts = _tile(s_len, ts)
    n_dh = len(dhs)
    last = s_len // ts - 1

    def body(*refs):
        x_ref, g_ref = refs[0], refs[1]
        dh_refs = refs[2:2 + n_dh]
        dres_ref, dx_ref, dg_ref, acc = refs[2 + n_dh:]
        i = pl.program_id(0)

        @pl.when(i == 0)
        def _():
            acc[...] = jnp.zeros_like(acc)

        xf = x_ref[...]
        dy = dh_refs[0][...]
        for r_ in dh_refs[1:]:
            dy = dy + r_[...]
        r = lax.rsqrt(jnp.mean(xf * xf, axis=-1, keepdims=True) + NORM_EPS)
        gd = dy * g_ref[...]
        dot = jnp.mean(xf * gd, axis=-1, keepdims=True)
        dx_ref[...] = dres_ref[...] + r * gd - xf * (r * r * r * dot)
        acc[...] += _rowsum8(dy * (xf * r))

        @pl.when(i == last)
        def _():
            dg_ref[...] = jnp.sum(acc[...], axis=0, keepdims=True)

    row = pl.BlockSpec((ts, d), lambda i: (i, 0))
    vec = pl.BlockSpec((1, d), lambda i: (0, 0))
    return pl.pallas_call(
        body, name=name, grid=(s_len // ts,),
        in_specs=[row, vec] + [row] * n_dh + [row],
        out_specs=[row, vec], out_shape=[_sds((s_len, d), F32), _sds((1, d), F32)],
        scratch_shapes=[pltpu.VMEM((8, d), F32)],
        compiler_params=_params((2 * (3 + n_dh) + 6) * ts * d * 4),
    )(x, g, *dhs, dres)


def _loss_head(y, target, *, name, ts=512):
    s_len, d = y.shape
    ts = _tile(s_len, ts)
    last = s_len // ts - 1

    def body(y_ref, t_ref, dy_ref, loss_ref, acc):
        i = pl.program_id(0)

        @pl.when(i == 0)
        def _():
            acc[...] = jnp.zeros_like(acc)

        err = y_ref[...] - t_ref[...]
        dy_ref[...] = err / d
        acc[...] += _rowsum8(err * err)

        @pl.when(i == last)
        def _():
            loss_ref[...] = (0.5 * jnp.sum(acc[...]) / d).reshape(1, 1)

    row = pl.BlockSpec((ts, d), lambda i: (i, 0))
    return pl.pallas_call(
        body, name=name, grid=(s_len // ts,), in_specs=[row, row],
        out_specs=[row, pl.BlockSpec((1, 1), lambda i: (0, 0))],
        out_shape=[_sds((s_len, d), F32), _sds((1, 1), F32)],
        scratch_shapes=[pltpu.VMEM((8, d), F32)],
        compiler_params=_params(10 * ts * d * 4),
    )(y, target)


CONV_ROWS = 64
CONV_COLS = 256


def _conv_taps(buf, w_ref, k_width, base, ts, tc, init, emit, reverse=False):
    cw = min(tc, CONV_COLS)
    rw = min(ts, CONV_ROWS)
    for cb in range(tc // cw):
        cs = slice(cb * cw, (cb + 1) * cw)
        for rb in range(ts // rw):
            acc = init(slice(rb * rw, (rb + 1) * rw), cs, (rw, cw))
            for k in range(k_width):
                sh = (k_width - 1 - k) if reverse else k
                acc = acc + w_ref[k:k + 1, cs] * buf[base + rb * rw + sh:base + rb * rw + sh + rw, cs]
            emit(slice(rb * rw, (rb + 1) * rw), cs, acc)


def _conv_wgrad(buf, d_ref_val, acc_ref, k_width, base, ts, tc):
    cw = min(tc, CONV_COLS)
    rw = min(ts, CONV_ROWS)
    for cb in range(tc // cw):
        cs = slice(cb * cw, (cb + 1) * cw)
        for rb in range(ts // rw):
            dv = d_ref_val[rb * rw:(rb + 1) * rw, cs]
            for k in range(k_width):
                prod = dv * buf[base + rb * rw + k:base + rb * rw + k + rw, cs]
                acc_ref[8 * k:8 * k + 8, cs] += _rowsum8(prod)


A_HALO = 32


def _a_conv_fwd(proj, conv_w, conv_b, *, name, ts=256, tc=512):
    s_len, e3 = proj.shape
    e = e3 // 3
    k_width = conv_w.shape[0]
    ts, tc = _tile(s_len, ts), _tile(e, tc)
    nc = e // tc
    kp = 32

    def body(val, gate, valh, gateh, w_ref, b_ref, u1_ref, buf):
        i = pl.program_id(1)
        u0h = valh[...] * _sigmoid(gateh[...])
        buf[0:A_HALO, :] = jnp.where(i > 0, u0h, 0.0)
        buf[A_HALO:A_HALO + ts, :] = val[...] * _sigmoid(gate[...])

        def init(rows, cs, shape):
            return jnp.broadcast_to(b_ref[:, cs], shape)

        def emit(rows, cs, acc):
            u1_ref[rows, cs] = acc

        _conv_taps(buf, w_ref, k_width, A_HALO - (k_width - 1), ts, tc, init, emit)

    hb = ts // A_HALO
    in_specs = [
        pl.BlockSpec((ts, tc), lambda j, i: (i, j)),
        pl.BlockSpec((ts, tc), lambda j, i: (i, nc + j)),
        pl.BlockSpec((A_HALO, tc), lambda j, i: (jnp.maximum(i * hb - 1, 0), j)),
        pl.BlockSpec((A_HALO, tc), lambda j, i: (jnp.maximum(i * hb - 1, 0), nc + j)),
        pl.BlockSpec((kp, tc), lambda j, i: (0, j)),
        pl.BlockSpec((1, tc), lambda j, i: (0, j)),
    ]
    w_pad = jnp.zeros((kp, e), F32).at[:k_width].set(conv_w)
    return pl.pallas_call(
        body, name=name, grid=(nc, s_len // ts), in_specs=in_specs,
        out_specs=pl.BlockSpec((ts, tc), lambda j, i: (i, j)), out_shape=_sds((s_len, e), F32),
        scratch_shapes=[pltpu.VMEM((A_HALO + ts, tc), F32)],
        compiler_params=_params(12 * ts * tc * 4),
    )(proj, proj, proj, proj, w_pad, conv_b)


def _ln_rows(u1, g, b):
    mu = jnp.mean(u1, axis=-1, keepdims=True)
    xc = u1 - mu
    var = jnp.mean(xc * xc, axis=-1, keepdims=True)
    rstd = lax.rsqrt(var + NORM_EPS)
    xhat = xc * rstd
    return xhat, rstd, xhat * g + b


def _a_post_fwd(u1, proj, ln_g, ln_b, *, name, ts=256):
    s_len, e = u1.shape
    ts = _tile(s_len, ts)

    def body(u1_ref, z_ref, g_ref, b_ref, o_ref):
        _, _, u2 = _ln_rows(u1_ref[...], g_ref[...], b_ref[...])
        o_ref[...] = (_silu(u2) * _silu(z_ref[...])).astype(o_ref.dtype)

    row = pl.BlockSpec((ts, e), lambda i: (i, 0))
    vec = pl.BlockSpec((1, e), lambda i: (0, 0))
    return pl.pallas_call(
        body, name=name, grid=(s_len // ts,),
        in_specs=[row, pl.BlockSpec((ts, e), lambda i: (i, 2)), vec, vec],
        out_specs=row, out_shape=_sds((s_len, e), BF16), compiler_params=_params(12 * ts * e * 4),
    )(u1, proj, ln_g, ln_b)


def _a_post_bwd(dgated, u1, proj, ln_g, ln_b, *, name, ts=256):
    s_len, e = u1.shape
    ts = _tile(s_len, ts)
    last = s_len // ts - 1

    def body(dg_ref, u1_ref, z_ref, g_ref, b_ref, du1_ref, dz_ref, dlg_ref, dlb_ref, dcb_ref, a_g, a_b, a_c):
        i = pl.program_id(0)

        @pl.when(i == 0)
        def _():
            a_g[...] = jnp.zeros_like(a_g)
            a_b[...] = jnp.zeros_like(a_b)
            a_c[...] = jnp.zeros_like(a_c)

        g = g_ref[...]
        xhat, rstd, u2 = _ln_rows(u1_ref[...], g, b_ref[...])
        z = z_ref[...]
        dgt = dg_ref[...]
        dz_ref[...] = (dgt * _silu(u2) * _dsilu(z)).astype(dz_ref.dtype)
        du2 = dgt * _silu(z) * _dsilu(u2)
        a_g[...] += _rowsum8(du2 * xhat)
        a_b[...] += _rowsum8(du2)
        dxh = du2 * g
        m1 = jnp.mean(dxh, axis=-1, keepdims=True)
        m2 = jnp.mean(dxh * xhat, axis=-1, keepdims=True)
        du1 = rstd * (dxh - m1 - xhat * m2)
        du1_ref[...] = du1
        a_c[...] += _rowsum8(du1)

        @pl.when(i == last)
        def _():
            dlg_ref[...] = jnp.sum(a_g[...], axis=0, keepdims=True)
            dlb_ref[...] = jnp.sum(a_b[...], axis=0, keepdims=True)
            dcb_ref[...] = jnp.sum(a_c[...], axis=0, keepdims=True)

    row = pl.BlockSpec((ts, e), lambda i: (i, 0))
    vec = pl.BlockSpec((1, e), lambda i: (0, 0))
    return pl.pallas_call(
        body, name=name, grid=(s_len // ts,),
        in_specs=[row, row, pl.BlockSpec((ts, e), lambda i: (i, 2)), vec, vec],
        out_specs=[row, row, vec, vec, vec],
        out_shape=[_sds((s_len, e), F32), _sds((s_len, e), BF16), _sds((1, e), F32), _sds((1, e), F32), _sds((1, e), F32)],
        scratch_shapes=[pltpu.VMEM((8, e), F32)] * 3,
        compiler_params=_params(20 * ts * e * 4),
    )(dgated, u1, proj, ln_g, ln_b)


def _a_conv_bwd(du1, proj, dz, conv_w, *, name, ts=256, tc=512):
    s_len, e3 = proj.shape
    e = e3 // 3
    k_width = conv_w.shape[0]
    ts, tc = _tile(s_len, ts), _tile(e, tc)
    nc, nr = e // tc, s_len // ts
    kp = 32
    hb = ts // A_HALO

    def body(val, gate, valh, gateh, d_ref, dh_ref, dz_ref, w_ref, dp_ref, dw_ref, buf_u, buf_d, du0, acc):
        i = pl.program_id(1)

        @pl.when(i == 0)
        def _():
            acc[...] = jnp.zeros_like(acc)

        v = val[...]
        sg = _sigmoid(gate[...])
        buf_u[0:A_HALO, :] = jnp.where(i > 0, valh[...] * _sigmoid(gateh[...]), 0.0)
        buf_u[A_HALO:A_HALO + ts, :] = v * sg
        buf_d[0:ts, :] = d_ref[...]
        buf_d[ts:ts + A_HALO, :] = jnp.where(i < nr - 1, dh_ref[...], 0.0)

        def init(rows, cs, shape):
            return jnp.zeros(shape, F32)

        def emit(rows, cs, a):
            du0[rows, cs] = a

        _conv_taps(buf_d, w_ref, k_width, 0, ts, tc, init, emit, reverse=True)
        _conv_wgrad(buf_u, buf_d, acc, k_width, A_HALO - (k_width - 1), ts, tc)
        d0 = du0[...]
        dp_ref[0] = (d0 * sg).astype(dp_ref.dtype)
        dp_ref[1] = (d0 * v * sg * (1.0 - sg)).astype(dp_ref.dtype)
        dp_ref[2] = dz_ref[...]

        @pl.when(i == nr - 1)
        def _():
            for k in range(kp):
                dw_ref[k:k + 1, :] = jnp.sum(acc[8 * k:8 * k + 8, :], axis=0, keepdims=True)

    in_specs = [
        pl.BlockSpec((ts, tc), lambda j, i: (i, j)),
        pl.BlockSpec((ts, tc), lambda j, i: (i, nc + j)),
        pl.BlockSpec((A_HALO, tc), lambda j, i: (jnp.maximum(i * hb - 1, 0), j)),
        pl.BlockSpec((A_HALO, tc), lambda j, i: (jnp.maximum(i * hb - 1, 0), nc + j)),
        pl.BlockSpec((ts, tc), lambda j, i: (i, j)),
        pl.BlockSpec((A_HALO, tc), lambda j, i: (jnp.minimum((i + 1) * hb, nr * hb - 1), j)),
        pl.BlockSpec((ts, tc), lambda j, i: (i, j)),
        pl.BlockSpec((kp, tc), lambda j, i: (0, j)),
    ]
    w_pad = jnp.zeros((kp, e), F32).at[:k_width].set(conv_w)
    dproj, dw = pl.pallas_call(
        body, name=name, grid=(nc, nr), in_specs=in_specs,
        out_specs=[pl.BlockSpec((3, ts, tc), lambda j, i: (0, i, j)), pl.BlockSpec((kp, tc), lambda j, i: (0, j))],
        out_shape=[_sds((3, s_len, e), BF16), _sds((kp, e), F32)],
        scratch_shapes=[pltpu.VMEM((A_HALO + ts, tc), F32), pltpu.VMEM((ts + A_HALO, tc), F32),
                        pltpu.VMEM((ts, tc), F32), pltpu.VMEM((8 * kp, tc), F32)],
        compiler_params=_params(24 * ts * tc * 4),
    )(proj, proj, proj, proj, du1, du1, dz, w_pad)
    return dproj, dw[:k_width]


C_HALO = 8


def _c_mid_fwd(proj, conv_w, *, name, ts=256, tc=512):
    s_len, e4 = proj.shape
    e = e4 // 4
    k_width = conv_w.shape[0]
    ts, tc = _tile(s_len, ts), _tile(e, tc)
    nc = e // tc
    hb = ts // C_HALO

    def body(u, bg, cg, z, uh, cgh, w_ref, o_ref, buf, y):
        i = pl.program_id(1)
        buf[0:C_HALO, :] = jnp.where(i > 0, uh[...] * cgh[...], 0.0)
        buf[C_HALO:C_HALO + ts, :] = u[...] * cg[...]

        def init(rows, cs, shape):
            return jnp.zeros(shape, F32)

        def emit(rows, cs, a):
            y[rows, cs] = a

        _conv_taps(buf, w_ref, k_width, C_HALO - (k_width - 1), ts, tc, init, emit)
        o_ref[...] = (bg[...] * y[...] * _silu(z[...])).astype(o_ref.dtype)

    def grp(g):
        return pl.BlockSpec((ts, tc), lambda j, i: (i, g * nc + j))

    def halo(g):
        return pl.BlockSpec((C_HALO, tc), lambda j, i: (jnp.maximum(i * hb - 1, 0), g * nc + j))

    w_pad = jnp.zeros((8, e), F32).at[:k_width].set(conv_w)
    return pl.pallas_call(
        body, name=name, grid=(nc, s_len // ts),
        in_specs=[grp(0), grp(1), grp(2), grp(3), halo(0), halo(2), pl.BlockSpec((8, tc), lambda j, i: (0, j))],
        out_specs=pl.BlockSpec((ts, tc), lambda j, i: (i, j)), out_shape=_sds((s_len, e), BF16),
        scratch_shapes=[pltpu.VMEM((C_HALO + ts, tc), F32), pltpu.VMEM((ts, tc), F32)],
        compiler_params=_params(16 * ts * tc * 4),
    )(proj, proj, proj, proj, proj, proj, w_pad)


def _c_mid_bwd(dgated, proj, conv_w, *, name, ts=256, tc=512):
    s_len, e4 = proj.shape
    e = e4 // 4
    k_width = conv_w.shape[0]
    ts, tc = _tile(s_len, ts), _tile(e, tc)
    nc, nr = e // tc, s_len // ts
    hb = ts // C_HALO

    def body(u, bg, cg, z, uh, cgh, dg, dgh, bgh, zh, w_ref, dp_ref, dw_ref, buf_p, buf_d, y, dpv, acc):
        i = pl.program_id(1)

        @pl.when(i == 0)
        def _():
            acc[...] = jnp.zeros_like(acc)

        uv, bgv, cgv, zv, dgv = u[...], bg[...], cg[...], z[...], dg[...]
        buf_p[0:C_HALO, :] = jnp.where(i > 0, uh[...] * cgh[...], 0.0)
        buf_p[C_HALO:C_HALO + ts, :] = uv * cgv
        sz = _silu(zv)
        buf_d[0:ts, :] = dgv * sz * bgv
        buf_d[ts:ts + C_HALO, :] = jnp.where(i < nr - 1, dgh[...] * _silu(zh[...]) * bgh[...], 0.0)

        def init(rows, cs, shape):
            return jnp.zeros(shape, F32)

        def emit_y(rows, cs, a):
            y[rows, cs] = a

        def emit_dp(rows, cs, a):
            dpv[rows, cs] = a

        _conv_taps(buf_p, w_ref, k_width, C_HALO - (k_width - 1), ts, tc, init, emit_y)
        _conv_taps(buf_d, w_ref, k_width, 0, ts, tc, init, emit_dp, reverse=True)
        _conv_wgrad(buf_p, buf_d, acc, k_width, C_HALO - (k_width - 1), ts, tc)
        yv, dp = y[...], dpv[...]
        dp_ref[0] = (dp * cgv).astype(dp_ref.dtype)
        dp_ref[1] = (dgv * sz * yv).astype(dp_ref.dtype)
        dp_ref[2] = (dp * uv).astype(dp_ref.dtype)
        dp_ref[3] = (dgv * bgv * yv * _dsilu(zv)).astype(dp_ref.dtype)

        @pl.when(i == nr - 1)
        def _():
            for k in range(8):
                dw_ref[k:k + 1, :] = jnp.sum(acc[8 * k:8 * k + 8, :], axis=0, keepdims=True)

    def grp(g):
        return pl.BlockSpec((ts, tc), lambda j, i: (i, g * nc + j))

    def prev(g):
        return pl.BlockSpec((C_HALO, tc), lambda j, i: (jnp.maximum(i * hb - 1, 0), g * nc + j))

    def nxt(g):
        return pl.BlockSpec((C_HALO, tc), lambda j, i: (jnp.minimum((i + 1) * hb, nr * hb - 1), g * nc + j))

    w_pad = jnp.zeros((8, e), F32).at[:k_width].set(conv_w)
    dproj, dw = pl.pallas_call(
        body, name=name, grid=(nc, nr),
        in_specs=[grp(0), grp(1), grp(2), grp(3), prev(0), prev(2),
                  pl.BlockSpec((ts, tc), lambda j, i: (i, j)),
                  pl.BlockSpec((C_HALO, tc), lambda j, i: (jnp.minimum((i + 1) * hb, nr * hb - 1), j)),
                  nxt(1), nxt(3), pl.BlockSpec((8, tc), lambda j, i: (0, j))],
        out_specs=[pl.BlockSpec((4, ts, tc), lambda j, i: (0, i, j)), pl.BlockSpec((8, tc), lambda j, i: (0, j))],
        out_shape=[_sds((4, s_len, e), BF16), _sds((8, e), F32)],
        scratch_shapes=[pltpu.VMEM((C_HALO + ts, tc), F32), pltpu.VMEM((ts + C_HALO, tc), F32),
                        pltpu.VMEM((ts, tc), F32), pltpu.VMEM((ts, tc), F32), pltpu.VMEM((64, tc), F32)],
        compiler_params=_params(32 * ts * tc * 4),
    )(proj, proj, proj, proj, proj, proj, dgated, dgated, proj, proj, w_pad)
    return dproj, dw[:k_width]


def _head_rms(xv, g):
    r = lax.rsqrt(jnp.mean(xv * xv, axis=-1, keepdims=True) + NORM_EPS)
    return r, xv * r * g


def _b_qk_fwd(proj, gq, gk, *, name, ts=512, tc=512):
    s_len, e4 = proj.shape
    e = e4 // 4
    ts, tc = _tile(s_len, ts), _tile(e, tc)
    nc = e // tc

    def body(q, k, v, gq_ref, gk_ref, qn, kn, vb):
        for h in range(tc // HEAD_DIM):
            cs = slice(h * HEAD_DIM, (h + 1) * HEAD_DIM)
            qn[:, cs] = _head_rms(q[:, cs], gq_ref[...])[1].astype(qn.dtype)
            kn[:, cs] = _head_rms(k[:, cs], gk_ref[...])[1].astype(kn.dtype)
        vb[...] = v[...].astype(vb.dtype)

    def grp(g):
        return pl.BlockSpec((ts, tc), lambda i, j: (i, g * nc + j))

    vec = pl.BlockSpec((1, HEAD_DIM), lambda i, j: (0, 0))
    out = pl.BlockSpec((ts, tc), lambda i, j: (i, j))
    return pl.pallas_call(
        body, name=name, grid=(s_len // ts, nc), in_specs=[grp(0), grp(1), grp(2), vec, vec],
        out_specs=[out, out, out], out_shape=[_sds((s_len, e), BF16)] * 3,
        compiler_params=_params(16 * ts * tc * 4),
    )(proj, proj, proj, gq, gk)


def _b_qk_bwd(dqn, dkn, dv, dz, proj, gq, gk, *, name, ts=512, tc=512):
    s_len, e4 = proj.shape
    e = e4 // 4
    ts, tc = _tile(s_len, ts), _tile(e, tc)
    nc, nr = e // tc, s_len // ts

    def body(dq_ref, dk_ref, dv_ref, dz_ref, q, k, gq_ref, gk_ref, dp_ref, dgq_ref, dgk_ref, a_q, a_k):
        i, j = pl.program_id(0), pl.program_id(1)

        @pl.when((i == 0) & (j == 0))
        def _():
            a_q[...] = jnp.zeros_like(a_q)
            a_k[...] = jnp.zeros_like(a_k)

        for h in range(tc // HEAD_DIM):
            cs = slice(h * HEAD_DIM, (h + 1) * HEAD_DIM)
            for slot, src, d_ref, g_ref, acc in ((0, q, dq_ref, gq_ref, a_q), (1, k, dk_ref, gk_ref, a_k)):
                xv = src[:, cs]
                dy = d_ref[:, cs]
                r = lax.rsqrt(jnp.mean(xv * xv, axis=-1, keepdims=True) + NORM_EPS)
                gd = dy * g_ref[...]
                dot = jnp.mean(xv * gd, axis=-1, keepdims=True)
                dp_ref[slot, :, cs] = (r * gd - xv * (r * r * r * dot)).astype(dp_ref.dtype)
                acc[...] += _rowsum8(dy * (xv * r))
        dp_ref[2] = dv_ref[...].astype(dp_ref.dtype)
        dp_ref[3] = dz_ref[...]

        @pl.when((i == nr - 1) & (j == nc - 1))
        def _():
            dgq_ref[...] = jnp.sum(a_q[...], axis=0, keepdims=True)
            dgk_ref[...] = jnp.sum(a_k[...], axis=0, keepdims=True)

    blk = pl.BlockSpec((ts, tc), lambda i, j: (i, j))
    vec = pl.BlockSpec((1, HEAD_DIM), lambda i, j: (0, 0))

    def grp(g):
        return pl.BlockSpec((ts, tc), lambda i, j: (i, g * nc + j))

    return pl.pallas_call(
        body, name=name, grid=(nr, nc), in_specs=[blk, blk, blk, blk, grp(0), grp(1), vec, vec],
        out_specs=[pl.BlockSpec((4, ts, tc), lambda i, j: (0, i, j)), vec, vec],
        out_shape=[_sds((4, s_len, e), BF16), _sds((1, HEAD_DIM), F32), _sds((1, HEAD_DIM), F32)],
        scratch_shapes=[pltpu.VMEM((8, HEAD_DIM), F32)] * 2,
        compiler_params=_params(24 * ts * tc * 4),
    )(dqn, dkn, dv, dz, proj, proj, gq, gk)


def _log_sigmoid(x):
    y = jnp.exp(-jnp.abs(x))
    u = 1.0 + y
    log1p = jnp.where(u == 1.0, y, jnp.log(u) * (y / jnp.where(u == 1.0, 1.0, u - 1.0)))
    return jnp.minimum(x, 0.0) - log1p


def _split3(v):
    hi = v.astype(BF16)
    r1 = v - hi.astype(F32)
    mid = r1.astype(BF16)
    lo = (r1 - mid.astype(F32)).astype(BF16)
    return hi, mid, lo


def _tri_matmul(tri, v):
    hi, mid, lo = _split3(v)
    return (jnp.dot(tri, hi, preferred_element_type=F32) + jnp.dot(tri, mid, preferred_element_type=F32)
            + jnp.dot(tri, lo, preferred_element_type=F32))


def _b_cumsum(fl, bias, *, name, t=512):
    s_len, w = fl.shape
    t = _tile(s_len, t)

    def body(fl_ref, b_ref, c_ref, carry):
        @pl.when(pl.program_id(0) == 0)
        def _():
            carry[...] = jnp.zeros_like(carry)

        logf = _log_sigmoid(fl_ref[...] + b_ref[...])
        row = lax.broadcasted_iota(jnp.int32, (t, t), 0)
        col = lax.broadcasted_iota(jnp.int32, (t, t), 1)
        tri = jnp.where(col <= row, 1.0, 0.0).astype(BF16)
        c = _tri_matmul(tri, logf) + carry[...]
        c_ref[...] = c
        carry[...] = c[t - 1:t, :]

    return pl.pallas_call(
        body, name=name, grid=(s_len // t,),
        in_specs=[pl.BlockSpec((t, w), lambda i: (i, 0)), pl.BlockSpec((1, w), lambda i: (0, 0))],
        out_specs=pl.BlockSpec((t, w), lambda i: (i, 0)), out_shape=_sds((s_len, w), F32),
        scratch_shapes=[pltpu.VMEM((1, w), F32)], compiler_params=_params(16 << 20),
    )(fl, bias)


def _b_cumsum_bwd(dc, fl, bias, *, name, t=512):
    s_len, w = fl.shape
    t = _tile(s_len, t)
    n = s_len // t

    def body(dc_ref, fl_ref, b_ref, dfl_ref, db_ref, carry, acc):
        i = pl.program_id(0)

        @pl.when(i == 0)
        def _():
            carry[...] = jnp.zeros_like(carry)
            acc[...] = jnp.zeros_like(acc)

        row = lax.broadcasted_iota(jnp.int32, (t, t), 0)
        col = lax.broadcasted_iota(jnp.int32, (t, t), 1)
        tri = jnp.where(col >= row, 1.0, 0.0).astype(BF16)
        dlogf = _tri_matmul(tri, dc_ref[...]) + carry[...]
        carry[...] = dlogf[0:1, :]
        dfl = dlogf * _sigmoid(-(fl_ref[...] + b_ref[...]))
        dfl_ref[...] = dfl
        acc[...] += _rowsum8(dfl)

        @pl.when(i == n - 1)
        def _():
            db_ref[...] = jnp.sum(acc[...], axis=0, keepdims=True)

    rev = pl.BlockSpec((t, w), lambda i: (n - 1 - i, 0))
    vec = pl.BlockSpec((1, w), lambda i: (0, 0))
    return pl.pallas_call(
        body, name=name, grid=(n,), in_specs=[rev, rev, vec], out_specs=[rev, vec],
        out_shape=[_sds((s_len, w), F32), _sds((1, w), F32)],
        scratch_shapes=[pltpu.VMEM((1, w), F32), pltpu.VMEM((8, w), F32)], compiler_params=_params(16 << 20),
    )(dc, fl, bias)


def _b_gate_fwd(o, proj, *, name, ts=512, tc=512):
    s_len, e = o.shape
    ts, tc = _tile(s_len, ts), _tile(e, tc)
    nc = e // tc

    def body(o_ref, z_ref, g_ref):
        g_ref[...] = (o_ref[...] * _silu(z_ref[...])).astype(g_ref.dtype)

    blk = pl.BlockSpec((ts, tc), lambda i, j: (i, j))
    return pl.pallas_call(
        body, name=name, grid=(s_len // ts, nc),
        in_specs=[blk, pl.BlockSpec((ts, tc), lambda i, j: (i, 3 * nc + j))], out_specs=blk,
        out_shape=_sds((s_len, e), BF16), compiler_params=_params(12 * ts * tc * 4),
    )(o, proj)


def _b_gate_bwd(dgated, o, o_lo, proj, *, name, hg, ts=512):
    s_len, e = o.shape
    ts = _tile(s_len, ts)
    w = hg * HEAD_DIM
    ng = e // w

    def body(dg_ref, o_ref, olo_ref, z_ref, do_ref, dz_ref, dl_ref):
        dgt, ov, zv = dg_ref[...], o_ref[...], z_ref[...]
        dob = (dgt * _silu(zv)).astype(do_ref.dtype)
        do_ref[...] = dob
        dz_ref[...] = (dgt * ov * _dsilu(zv)).astype(dz_ref.dtype)
        prod = dob.astype(F32) * (ov + olo_ref[...])
        lane = lax.broadcasted_iota(jnp.int32, (ts, LANES), 1)
        d = jnp.zeros((ts, LANES), F32)
        for hh in range(hg):
            d = jnp.where(lane == hh, jnp.sum(prod[:, hh * HEAD_DIM:(hh + 1) * HEAD_DIM], axis=-1, keepdims=True), d)
        dl_ref[...] = d

    blk = pl.BlockSpec((ts, w), lambda i, j: (i, j))
    return pl.pallas_call(
        body, name=name, grid=(s_len // ts, ng),
        in_specs=[blk, blk, blk, pl.BlockSpec((ts, w), lambda i, j: (i, 3 * ng + j))],
        out_specs=[blk, blk, pl.BlockSpec((ts, LANES), lambda i, j: (i, j))],
        out_shape=[_sds((s_len, e), BF16), _sds((s_len, e), BF16), _sds((s_len, ng * LANES), F32)],
        compiler_params=_params(20 * ts * w * 4),
    )(dgated, o, o_lo, proj)


LOG2E = 1.4426950408889634
ATTN_ROW_CHUNK = 128


def _tri_tables(n, k_major):
    pairs = [(i, j) for j in range(n) for i in range(j, n)] if k_major else [(i, j) for i in range(n) for j in range(i + 1)]
    return (jnp.asarray(np.array([p[0] for p in pairs], np.int32)), jnp.asarray(np.array([p[1] for p in pairs], np.int32)))


def _attn_logits2(s_raw, cr2, diag, row0, c1):
    s2 = s_raw * c1 - cr2
    if diag:
        rc, t = s_raw.shape
        row = lax.broadcasted_iota(jnp.int32, (rc, t), 0) + row0
        col = lax.broadcasted_iota(jnp.int32, (rc, t), 1)
        s2 = jnp.where(col <= row, s2, -jnp.inf)
    return s2


def _fox_fwd(qn, kn, vb, c_row, *, name, hg, t=512):
    s_len, e = qn.shape
    t = _tile(s_len, t)
    rc = _tile(t, ATTN_ROW_CHUNK)
    w = hg * HEAD_DIM
    ng, n = e // w, s_len // t
    c1 = HEAD_DIM ** -0.5 * LOG2E
    qi_tab, kj_tab = _tri_tables(n, k_major=False)

    def body(qi_ref, kj_ref, q_ref, k_ref, v_ref, cr_ref, o_ref, olo_ref, lse_ref, m_s, l_s, acc_s, lo_s, s_scr, p_scr,
             a_scr):
        pid = pl.program_id(1)
        i, j = qi_ref[pid], kj_ref[pid]

        @pl.when(j == 0)
        def _():
            m_s[...] = jnp.full_like(m_s, -jnp.inf)
            l_s[...] = jnp.zeros_like(l_s)
            acc_s[...] = jnp.zeros_like(acc_s)
            lo_s[...] = jnp.zeros_like(lo_s)

        def step(diag):
            for hh in range(hg):
                cs = slice(hh * HEAD_DIM, (hh + 1) * HEAD_DIM)
                s_scr[...] = lax.dot_general(q_ref[:, cs], k_ref[:, cs], (((1,), (1,)), ((), ())),
                                             preferred_element_type=F32)
                cr2 = cr_ref[hh] * LOG2E
                for r in range(t // rc):
                    rows = slice(r * rc, (r + 1) * rc)
                    s2 = _attn_logits2(s_scr[rows, :], cr2, diag, r * rc, c1)
                    m_prev = m_s[hh, rows]
                    m_new = jnp.maximum(m_prev, jnp.max(s2, axis=-1, keepdims=True))
                    alpha = jnp.exp2(m_prev - m_new)
                    p = jnp.exp2(s2 - m_new)
                    l_s[hh, rows] = alpha * l_s[hh, rows] + jnp.sum(p, axis=-1, keepdims=True)
                    m_s[hh, rows] = m_new
                    a_scr[rows] = alpha
                    p_hi = p.astype(BF16)
                    p_scr[rows, :] = p_hi
                    p_scr[t + r * rc:t + (r + 1) * rc, :] = (p - p_hi.astype(F32)).astype(BF16)
                pv = jnp.dot(p_scr[...], v_ref[:, cs], preferred_element_type=F32)
                al = a_scr[...]
                acc_s[:, cs] = al * acc_s[:, cs] + pv[:t]
                lo_s[:, cs] = al * lo_s[:, cs] + pv[t:]

        @pl.when(j < i)
        def _():
            step(False)

        @pl.when(j == i)
        def _():
            step(True)
            lane = lax.broadcasted_iota(jnp.int32, (t, LANES), 1)
            lse = jnp.zeros((t, LANES), F32)
            for hh in range(hg):
                cs = slice(hh * HEAD_DIM, (hh + 1) * HEAD_DIM)
                o_ref[:, cs] = acc_s[:, cs] / l_s[hh]
                olo_ref[:, cs] = lo_s[:, cs] / l_s[hh]
                lse = jnp.where(lane == hh, m_s[hh] + jnp.log2(l_s[hh]), lse)
            lse_ref[...] = lse

    qspec = pl.BlockSpec((t, w), lambda g, p, qi, kj: (qi[p], g))
    kspec = pl.BlockSpec((t, w), lambda g, p, qi, kj: (kj[p], g))
    stat = pl.BlockSpec((t, LANES), lambda g, p, qi, kj: (qi[p], g))
    crow = pl.BlockSpec((hg, 1, t), lambda g, p, qi, kj: (g, 0, kj[p]))
    grid_spec = pltpu.PrefetchScalarGridSpec(
        num_scalar_prefetch=2, grid=(ng, int(qi_tab.shape[0])), in_specs=[qspec, kspec, kspec, crow],
        out_specs=[qspec, qspec, stat],
        scratch_shapes=[pltpu.VMEM((hg, t, 1), F32), pltpu.VMEM((hg, t, 1), F32), pltpu.VMEM((t, w), F32),
                        pltpu.VMEM((t, w), F32), pltpu.VMEM((t, t), F32), pltpu.VMEM((2 * t, t), BF16),
                        pltpu.VMEM((t, 1), F32)])
    return pl.pallas_call(
        body, name=name, grid_spec=grid_spec,
        out_shape=[_sds((s_len, e), F32), _sds((s_len, e), F32), _sds((s_len, ng * LANES), F32)],
        compiler_params=_params(12 * t * t * 4 + 24 * t * w * 4),
    )(qi_tab, kj_tab, qn, kn, vb, c_row)


def _fox_bwd(qn, kn, vb, do, lse, delta, c_row, *, name, hg, t=512):
    s_len, e = qn.shape
    t = _tile(s_len, t)
    w = hg * HEAD_DIM
    ng, n = e // w, s_len // t
    scale = HEAD_DIM ** -0.5
    c1 = scale * LOG2E
    qi_tab, kj_tab = _tri_tables(n, k_major=True)

    def body(qi_ref, kj_ref, q_ref, k_ref, v_ref, do_ref, lse_ref, dl_ref, cr_ref, dq_ref, dk_ref, dv_ref, dc_ref,
             dk_s, dv_s, dc_s):
        pid = pl.program_id(1)
        i, j = qi_ref[pid], kj_ref[pid]

        @pl.when(pid == 0)
        def _():
            dq_ref[...] = jnp.zeros_like(dq_ref)

        @pl.when(i == j)
        def _():
            dk_s[...] = jnp.zeros_like(dk_s)
            dv_s[...] = jnp.zeros_like(dv_s)
            dc_s[...] = jnp.zeros_like(dc_s)

        def step(diag):
            rows = pl.ds(pl.multiple_of(i * t, t), t)
            for hh in range(hg):
                cs = slice(hh * HEAD_DIM, (hh + 1) * HEAD_DIM)
                q, k, v, dov = q_ref[:, cs], k_ref[:, cs], v_ref[:, cs], do_ref[:, cs]
                s_raw = lax.dot_general(q, k, (((1,), (1,)), ((), ())), preferred_element_type=F32)
                s2 = _attn_logits2(s_raw, cr_ref[hh] * LOG2E, diag, 0, c1)
                p = jnp.exp2(s2 - lse_ref[:, hh:hh + 1])
                dv_s[:, cs] += lax.dot_general(p.astype(BF16), dov, (((0,), (0,)), ((), ())), preferred_element_type=F32)
                dp = lax.dot_general(dov, v, (((1,), (1,)), ((), ())), preferred_element_type=F32)
                ds = p * (dp - dl_ref[:, hh:hh + 1])
                dc_s[hh] -= jnp.sum(ds, axis=0, keepdims=True)
                dsb = (ds * scale).astype(BF16)
                dk_s[:, cs] += lax.dot_general(dsb, q, (((0,), (0,)), ((), ())), preferred_element_type=F32)
                dq_ref[rows, cs] += jnp.dot(dsb, k, preferred_element_type=F32)

        @pl.when(i > j)
        def _():
            step(False)

        @pl.when(i == j)
        def _():
            step(True)

        @pl.when(i == n - 1)
        def _():
            dk_ref[...] = dk_s[...]
            dv_ref[...] = dv_s[...]
            dc_ref[...] = dc_s[...]

    qspec = pl.BlockSpec((t, w), lambda g, p, qi, kj: (qi[p], g))
    kspec = pl.BlockSpec((t, w), lambda g, p, qi, kj: (kj[p], g))
    stat = pl.BlockSpec((t, LANES), lambda g, p, qi, kj: (qi[p], g))
    crow = pl.BlockSpec((hg, 1, t), lambda g, p, qi, kj: (g, 0, kj[p]))
    grid_spec = pltpu.PrefetchScalarGridSpec(
        num_scalar_prefetch=2, grid=(ng, int(qi_tab.shape[0])),
        in_specs=[qspec, kspec, kspec, qspec, stat, stat, crow],
        out_specs=[pl.BlockSpec((s_len, w), lambda g, p, qi, kj: (0, g)), kspec, kspec, crow],
        scratch_shapes=[pltpu.VMEM((t, w), F32), pltpu.VMEM((t, w), F32), pltpu.VMEM((hg, 1, t), F32)])
    return pl.pallas_call(
        body, name=name, grid_spec=grid_spec,
        out_shape=[_sds((s_len, e), F32), _sds((s_len, e), F32), _sds((s_len, e), F32), _sds((e // HEAD_DIM, 1, s_len), F32)],
        compiler_params=_params(2 * s_len * w * 4 + 16 * t * t * 4 + 24 * t * w * 4),
    )(qi_tab, kj_tab, qn, kn, vb, do, lse, delta, c_row)


def _adamw(w, gs, m, v, *, name, tr=256):
    r, c = w.shape
    tr = _tile(r, tr)
    n_g = len(gs)

    def body(*refs):
        w_ref, g_refs = refs[0], refs[1:1 + n_g]
        m_ref, v_ref, go_ref, d_ref, nm_ref, nv_ref = refs[1 + n_g:]
        gv = g_refs[0][...].astype(F32)
        for g_ref in g_refs[1:]:
            gv = gv + g_ref[...].astype(F32)
        go_ref[...] = gv
        m2 = ADAM_B1 * m_ref[...] + (1.0 - ADAM_B1) * gv
        v2 = ADAM_B2 * v_ref[...] + (1.0 - ADAM_B2) * (gv * gv)
        m_hat = m2 / (1.0 - ADAM_B1 ** ADAM_STEP)
        v_hat = v2 / (1.0 - ADAM_B2 ** ADAM_STEP)
        d_ref[...] = -ADAM_LR * (m_hat / (jnp.sqrt(v_hat) + ADAM_EPS) + ADAM_WD * w_ref[...])
        nm_ref[...] = m2
        nv_ref[...] = v2

    blk = pl.BlockSpec((tr, c), lambda i: (i, 0))
    return pl.pallas_call(
        body, name=name, grid=(r // tr,), in_specs=[blk] * (3 + n_g), out_specs=[blk] * 4,
        out_shape=[_sds((r, c), F32)] * 4, compiler_params=_params(24 * tr * c * 4),
    )(w, *gs, m, v)


def _sum_slots(a, *, name, out_dtype, tr=256):
    n, r, c = a.shape
    tr = _tile(r, tr)

    def body(a_ref, o_ref):
        acc = a_ref[0].astype(F32)
        for k in range(1, n):
            acc = acc + a_ref[k].astype(F32)
        o_ref[...] = acc.astype(o_ref.dtype)

    return pl.pallas_call(
        body, name=name, grid=(r // tr,), in_specs=[pl.BlockSpec((n, tr, c), lambda i: (0, i, 0))],
        out_specs=pl.BlockSpec((tr, c), lambda i: (i, 0)), out_shape=_sds((r, c), out_dtype),
        compiler_params=_params(4 * (n + 2) * tr * c * 4),
    )(a)


_ANY = pl.BlockSpec(memory_space=pl.ANY)
DMA_CHUNK_BYTES = 512 << 10


def _chunks(parts):
    out = []
    for src_at, dst_at, rows, row_bytes in parts:
        step = max(16, DMA_CHUNK_BYTES // row_bytes // 16 * 16)
        for r0 in range(0, rows, step):
            n = min(step, rows - r0)
            out.append((src_at(r0, n), dst_at(r0, n)))
    return out


def _row_bytes(ref):
    return ref.shape[-1] * ref.dtype.itemsize


def _me():
    return lax.axis_index("x"), lax.axis_index("y"), lax.axis_index("c")


def _chip_peers(x, y):
    return [(2 * (1 - x) + y, (1 - x, y)), (2 * x + (1 - y), (x, 1 - y)), (2 * (1 - x) + (1 - y), (1 - x, 1 - y))]


def _exchange(name, ins, out_shapes, plan):
    n_in, n_out = len(ins), len(out_shapes)

    def body(*refs):
        in_refs, out_refs = refs[:n_in], refs[n_in:n_in + n_out]
        send_sems, recv_sems, loc_sems = refs[n_in + n_out:]
        remote, local = plan(in_refs, out_refs)
        starts, waits = [], []
        for k, (ws, wd, dev, parts) in enumerate(remote):
            def mk(s, d, k=k, dev=dev):
                return pltpu.make_async_remote_copy(src_ref=s, dst_ref=d, send_sem=send_sems.at[k],
                                                    recv_sem=recv_sems.at[k], device_id=dev, device_id_type=MESH_ID)
            starts += [mk(s, d) for s, d in _chunks(parts)]
            waits.append(mk(ws, wd))
        for k, (ws, wd, _, parts) in enumerate(local):
            def mk(s, d, k=k):
                return pltpu.make_async_copy(s, d, loc_sems.at[k])
            starts += [mk(s, d) for s, d in _chunks(parts)]
            waits.append(mk(ws, wd))
        for cp in starts:
            cp.start()
        for cp in waits:
            cp.wait()

    n_remote, n_local = plan.n_remote, plan.n_local
    return pl.pallas_call(
        body, name=name, in_specs=[_ANY] * n_in, out_specs=[_ANY] * n_out, out_shape=list(out_shapes),
        scratch_shapes=[pltpu.SemaphoreType.DMA((n_remote,)), pltpu.SemaphoreType.DMA((n_remote,)),
                        pltpu.SemaphoreType.DMA((max(n_local, 1),))],
    )(*ins)


def _gather_weights(shards, *, name):
    n_t = len(shards)

    def body(*refs):
        in_refs, out_refs = refs[:n_t], refs[n_t:2 * n_t]
        send_sems, recv_sems = refs[2 * n_t:]
        x, y, c = _me()
        chip = 2 * x + y
        peers = _chip_peers(x, y)
        local, first, passed = [], [], []
        for t in range(n_t):
            src, dst = in_refs[t], out_refs[t]
            n_rows = src.shape[0]
            half = n_rows // 2
            rb = _row_bytes(src)
            rows = pl.ds(c * half, half)

            def mk_own(s, d, t=t):
                return pltpu.make_async_remote_copy(
                    src_ref=s, dst_ref=d, send_sem=send_sems.at[7 * t + 6], recv_sem=recv_sems.at[7 * t + 6],
                    device_id=(x, y, 1 - c), device_id_type=MESH_ID)

            own = [(lambda r0, n, src=src: src.at[pl.ds(r0, n)],
                    lambda r0, n, dst=dst: dst.at[chip, pl.ds(r0, n)], n_rows, rb)]
            local.append((mk_own(src, dst.at[chip]), [mk_own(s, d) for s, d in _chunks(own)]))
            for k, (pchip, (px, py)) in enumerate(peers):
                def mk_ici(s, d, t=t, k=k, px=px, py=py):
                    return pltpu.make_async_remote_copy(
                        src_ref=s, dst_ref=d, send_sem=send_sems.at[7 * t + k], recv_sem=recv_sems.at[7 * t + k],
                        device_id=(px, py, c), device_id_type=MESH_ID)

                def mk_d2d(s, d, t=t, k=k):
                    return pltpu.make_async_remote_copy(
                        src_ref=s, dst_ref=d, send_sem=send_sems.at[7 * t + 3 + k], recv_sem=recv_sems.at[7 * t + 3 + k],
                        device_id=(x, y, 1 - c), device_id_type=MESH_ID)

                out_part = [(lambda r0, n, src=src: src.at[pl.ds(c * half + r0, n)],
                             lambda r0, n, dst=dst: dst.at[chip, pl.ds(c * half + r0, n)], half, rb)]
                fwd_part = [(lambda r0, n, dst=dst, pchip=pchip: dst.at[pchip, pl.ds(c * half + r0, n)],
                             lambda r0, n, dst=dst, pchip=pchip: dst.at[pchip, pl.ds(c * half + r0, n)], half, rb)]
                first.append((mk_ici(src.at[rows], dst.at[chip, rows]), [mk_ici(s, d) for s, d in _chunks(out_part)]))
                passed.append((mk_d2d(dst.at[pchip, rows], dst.at[pchip, rows]),
                               [mk_d2d(s, d) for s, d in _chunks(fwd_part)]))
        for _, chunk_copies in first + local:
            for cp in chunk_copies:
                cp.start()
        for (whole, _), (_, fwd_copies) in zip(first, passed):
            whole.wait_recv()
            for cp in fwd_copies:
                cp.start()
        for whole, _ in passed:
            whole.wait_recv()
        for whole, _ in first + passed:
            whole.wait_send()
        for whole, _ in local:
            whole.wait()

    outs = [_sds((N_CHIPS,) + s.shape, s.dtype) for s in shards]
    return pl.pallas_call(
        body, name=name, in_specs=[_ANY] * n_t, out_specs=[_ANY] * n_t, out_shape=outs,
        scratch_shapes=[pltpu.SemaphoreType.DMA((7 * n_t,)), pltpu.SemaphoreType.DMA((7 * n_t,))],
    )(*shards)


class _Plan:
    def __init__(self, fn, n_remote, n_local):
        self.fn, self.n_remote, self.n_local = fn, n_remote, n_local

    def __call__(self, in_refs, out_refs):
        return self.fn(in_refs, out_refs)


def _reduce_grads(grads, *, name):
    n_t = len(grads)

    def plan2(in_refs, out_refs):
        x, y, c = _me()
        chip = 2 * x + y
        remote, local = [], []
        for t in range(n_t):
            src, dst = in_refs[t], out_refs[t]
            rows, rb = src.shape[1], _row_bytes(src)

            def part(slot, src=src, dst=dst, rows=rows, rb=rb):
                return [(lambda r0, n: src.at[slot, pl.ds(r0, n)], lambda r0, n: dst.at[chip, pl.ds(r0, n)], rows, rb)]

            for pchip, (px, py) in _chip_peers(x, y):
                remote.append((src.at[pchip], dst.at[chip], (px, py, c), part(pchip)))
            local.append((src.at[chip], dst.at[chip], None, part(chip)))
        return remote, local

    out2 = _exchange(name + "_ici", grads, [_sds(g.shape, g.dtype) for g in grads], _Plan(plan2, 3 * n_t, n_t))
    sums = [_sum_slots(out2[t], name=f"{name}_sum{t}", out_dtype=BF16) for t in range(n_t)]

    def plan3(in_refs, out_refs):
        x, y, c = _me()
        remote = []
        for t in range(n_t):
            src, dst = in_refs[t], out_refs[t]
            rows = [(lambda r0, n, src=src: src.at[pl.ds(r0, n)], lambda r0, n, dst=dst: dst.at[pl.ds(r0, n)],
                     src.shape[0], _row_bytes(src))]
            remote.append((src, dst, (x, y, 1 - c), rows))
        return remote, []

    others = _exchange(name + "_swap", sums, [_sds(s.shape, s.dtype) for s in sums], _Plan(plan3, n_t, 0))
    return list(zip(sums, others))


def _allreduce_small(pack, *, name):
    r, w = pack.shape

    def body(p_ref, o_ref, buf, send_sems, recv_sems):
        x, y, c = _me()
        me = 4 * x + 2 * y + c
        buf[me] = p_ref[...]
        copies = []
        for k in range(1, N_DEV):
            peer = (x ^ ((k >> 2) & 1), y ^ ((k >> 1) & 1), c ^ (k & 1))
            copies.append(pltpu.make_async_remote_copy(
                src_ref=p_ref, dst_ref=buf.at[me], send_sem=send_sems.at[k - 1], recv_sem=recv_sems.at[k - 1],
                device_id=peer, device_id_type=MESH_ID))
        for cp in copies:
            cp.start()
        for cp in copies:
            cp.wait()
        acc = buf[0]
        for k in range(1, N_DEV):
            acc = acc + buf[k]
        o_ref[...] = acc

    vm = pl.BlockSpec(memory_space=pltpu.VMEM)
    return pl.pallas_call(
        body, name=name, in_specs=[vm], out_specs=vm, out_shape=_sds((r, w), F32),
        scratch_shapes=[pltpu.VMEM((N_DEV, r, w), F32), pltpu.SemaphoreType.DMA((N_DEV - 1,)),
                        pltpu.SemaphoreType.DMA((N_DEV - 1,))],
        compiler_params=_params(12 * r * w * 4),
    )(pack)


def _pack(arrs, row_multiple=16):
    flat = jnp.concatenate([a.reshape(-1).astype(F32) for a in arrs])
    unit = row_multiple * LANES
    total = -(-flat.shape[0] // unit) * unit
    return jnp.pad(flat, (0, total - flat.shape[0])).reshape(total // LANES, LANES)


def _unpack(packed, shapes):
    flat = packed.reshape(-1)
    out, off = [], 0
    for shp in shapes:
        n = int(np.prod(shp))
        out.append(flat[off:off + n].reshape(shp))
        off += n
    return out


def _pad_cols(a, width):
    return jnp.pad(a, [(0, 0)] * (a.ndim - 1) + [(0, width - a.shape[-1])])


ATTN_HEADS_PER_STEP = 2
SMALL_SHARDED = ("a_norm", "a_conv_w", "a_conv_b", "a_ln_g", "a_ln_b", "c_norm", "c_conv_w")
SMALL_REPLICATED = ("b_norm", "b_f_bias", "b_q_norm", "b_k_norm")
BIG = ("a_w_in", "a_w_out", "b_w_in", "b_w_out", "c_w_in", "c_w_out")
WEIGHTS = ("a_norm", "a_w_in", "a_conv_w", "a_conv_b", "a_ln_g", "a_ln_b", "a_w_out", "b_norm", "b_w_in", "b_f_bias",
           "b_q_norm", "b_k_norm", "b_w_out", "c_norm", "c_w_in", "c_conv_w", "c_w_out")


def _mixer_a_fwd(x, p, l, tag):
    h = _rms_fwd(x, p["a_norm"][l][None], name=f"{tag}_rms")
    proj = _mm_nn(h, p["a_w_in"][l], name=f"{tag}_in", tn=p["a_w_in"][l].shape[2])
    u1 = _a_conv_fwd(proj, p["a_conv_w"][l], p["a_conv_b"][l][None], name=f"{tag}_conv")
    gated = _a_post_fwd(u1, proj, p["a_ln_g"][l][None], p["a_ln_b"][l][None], name=f"{tag}_post")
    y = _mm_nn(gated, p["a_w_out"][l], name=f"{tag}_out", tn=1024, add=x)
    return y, (x, h, proj, u1, gated)


def _mixer_a_bwd(dx, saved, p, l, tag):
    x, h, proj, u1, gated = saved
    g = {}
    g["a_w_out"] = _mm_tn(gated, dx, name=f"{tag}_dwout", out_dtype=BF16, tk=2048, tn=1024, ts=1024)
    dgated = _mm_nt(dx, p["a_w_out"][l], name=f"{tag}_dgated", tn=2048, tk=1024)
    du1, dz, g["a_ln_g"], g["a_ln_b"], g["a_conv_b"] = _a_post_bwd(
        dgated, u1, proj, p["a_ln_g"][l][None], p["a_ln_b"][l][None], name=f"{tag}_dpost")
    dproj, g["a_conv_w"] = _a_conv_bwd(du1, proj, dz, p["a_conv_w"][l], name=f"{tag}_dconv")
    g["a_w_in"] = _mm_tn(h, dproj, name=f"{tag}_dwin", out_dtype=BF16, out_width=p["a_w_in"][l].shape[2], ts=2048)
    dh = _mm_nt(dproj, p["a_w_in"][l], name=f"{tag}_dh", ksub=4)
    dx, g["a_norm"] = _rms_bwd(x, p["a_norm"][l][None], [dh], dx, name=f"{tag}_drms")
    return dx, g


def _mixer_c_fwd(x, p, tag):
    h = _rms_fwd(x, p["c_norm"][0][None], name=f"{tag}_rms")
    proj = _mm_nn(h, p["c_w_in"], name=f"{tag}_in", tn=1024)
    gated = _c_mid_fwd(proj, p["c_conv_w"][0], name=f"{tag}_mid")
    y = _mm_nn(gated, p["c_w_out"], name=f"{tag}_out", tn=1024, add=x)
    return y, (x, h, proj, gated)


def _mixer_c_bwd(dx, saved, p, tag):
    x, h, proj, gated = saved
    g = {}
    g["c_w_out"] = _mm_tn(gated, dx, name=f"{tag}_dwout", out_dtype=BF16, tk=2048, tn=1024, ts=1024)
    dgated = _mm_nt(dx, p["c_w_out"], name=f"{tag}_dgated", tn=2048, tk=1024)
    dproj, g["c_conv_w"] = _c_mid_bwd(dgated, proj, p["c_conv_w"][0], name=f"{tag}_dmid")
    g["c_w_in"] = _mm_tn(h, dproj, name=f"{tag}_dwin", out_dtype=BF16, out_width=p["c_w_in"].shape[2], tn=1024, ts=2048)
    dh = _mm_nt(dproj, p["c_w_in"], name=f"{tag}_dh", tk=1024, ksub=2)
    dx, g["c_norm"] = _rms_bwd(x, p["c_norm"][0][None], [dh], dx, name=f"{tag}_drms")
    return dx, g


def _mixer_b_fwd(x, p, tag):
    hg = ATTN_HEADS_PER_STEP
    s_len = x.shape[0]
    n_heads = p["b_f_bias"].shape[1]
    h = _rms_fwd(x, p["b_norm"], name=f"{tag}_rms")
    proj = _mm_nn(h, p["b_wq"], name=f"{tag}_in", tn=1024)
    fl = _mm_nn(h, p["b_wf"], name=f"{tag}_inf", tn=LANES)
    qn, kn, vb = _b_qk_fwd(proj, p["b_q_norm"], p["b_k_norm"], name=f"{tag}_qk")
    bias = _pad_cols(p["b_f_bias"], LANES)
    c = _b_cumsum(fl, bias, name=f"{tag}_cumsum")
    ch = c[:, :n_heads]
    c_row = ch.T.reshape(n_heads, 1, s_len)
    o, o_lo, lse = _fox_fwd(qn, kn, vb, c_row, name=f"{tag}_attn", hg=hg)
    gated = _b_gate_fwd(o, proj, name=f"{tag}_gate")
    y = _mm_nn(gated, p["b_w_out"], name=f"{tag}_out", tn=1024, add=x)
    return y, (x, h, proj, fl, bias, qn, kn, vb, c_row, o, o_lo, lse, gated)


def _mixer_b_bwd(dx, saved, p, tag):
    hg = ATTN_HEADS_PER_STEP
    x, h, proj, fl, bias, qn, kn, vb, c_row, o, o_lo, lse, gated = saved
    s_len = x.shape[0]
    n_heads = p["b_f_bias"].shape[1]
    g = {}
    g["b_w_out"] = _mm_tn(gated, dx, name=f"{tag}_dwout", out_dtype=BF16, tk=2048, tn=1024, ts=1024)
    dgated = _mm_nt(dx, p["b_w_out"], name=f"{tag}_dgated", tn=2048, tk=1024)
    do, dz, delta = _b_gate_bwd(dgated, o, o_lo, proj, name=f"{tag}_dgate", hg=hg)
    dqn, dkn, dv, dc = _fox_bwd(qn, kn, vb, do, lse, delta, c_row, name=f"{tag}_dattn", hg=hg)
    dc_pad = _pad_cols(dc.reshape(n_heads, s_len).T, LANES)
    dfl, dbias = _b_cumsum_bwd(dc_pad, fl, bias, name=f"{tag}_dcumsum")
    g["b_f_bias"] = dbias[:, :n_heads]
    dproj, g["b_q_norm"], g["b_k_norm"] = _b_qk_bwd(dqn, dkn, dv, dz, proj, p["b_q_norm"], p["b_k_norm"], name=f"{tag}_dqk")
    dwq = _mm_tn(h, dproj, name=f"{tag}_dwin", out_dtype=BF16, tn=1024, ts=2048)
    dwf = _mm_tn(h, dfl, name=f"{tag}_dwinf", out_dtype=BF16, tn=LANES)
    g["b_w_in"] = jnp.concatenate([dwq, dwf[:, :n_heads]], axis=1)
    dh = _mm_nt(dproj, p["b_wq"], name=f"{tag}_dh", tk=1024, ksub=2)
    dhf = _mm_nt(dfl, p["b_wf"], name=f"{tag}_dhf", tk=LANES)
    dx, g["b_norm"] = _rms_bwd(x, p["b_norm"], [dh, dhf], dx, name=f"{tag}_drms")
    return dx, g


def kernel(x, a_norm, a_w_in, a_conv_w, a_conv_b, a_ln_g, a_ln_b, a_w_out, b_norm, b_w_in, b_f_bias, b_q_norm, b_k_norm, b_w_out, c_norm, c_w_in, c_conv_w, c_w_out, loss_target, m_a_norm, m_a_w_in, m_a_conv_w, m_a_conv_b, m_a_ln_g, m_a_ln_b, m_a_w_out, m_b_norm, m_b_w_in, m_b_f_bias, m_b_q_norm, m_b_k_norm, m_b_w_out, m_c_norm, m_c_w_in, m_c_conv_w, m_c_w_out, v_a_norm, v_a_w_in, v_a_conv_w, v_a_conv_b, v_a_ln_g, v_a_ln_b, v_a_w_out, v_b_norm, v_b_w_in, v_b_f_bias, v_b_q_norm, v_b_k_norm, v_b_w_out, v_c_norm, v_c_w_in, v_c_conv_w, v_c_w_out):
    w_loc = dict(a_norm=a_norm, a_w_in=a_w_in, a_conv_w=a_conv_w, a_conv_b=a_conv_b, a_ln_g=a_ln_g, a_ln_b=a_ln_b,
                 a_w_out=a_w_out, b_norm=b_norm, b_w_in=b_w_in, b_f_bias=b_f_bias, b_q_norm=b_q_norm, b_k_norm=b_k_norm,
                 b_w_out=b_w_out, c_norm=c_norm, c_w_in=c_w_in, c_conv_w=c_conv_w, c_w_out=c_w_out)
    m_loc = dict(a_norm=m_a_norm, a_w_in=m_a_w_in, a_conv_w=m_a_conv_w, a_conv_b=m_a_conv_b, a_ln_g=m_a_ln_g,
                 a_ln_b=m_a_ln_b, a_w_out=m_a_w_out, b_norm=m_b_norm, b_w_in=m_b_w_in, b_f_bias=m_b_f_bias,
                 b_q_norm=m_b_q_norm, b_k_norm=m_b_k_norm, b_w_out=m_b_w_out, c_norm=m_c_norm, c_w_in=m_c_w_in,
                 c_conv_w=m_c_conv_w, c_w_out=m_c_w_out)
    v_loc = dict(a_norm=v_a_norm, a_w_in=v_a_w_in, a_conv_w=v_a_conv_w, a_conv_b=v_a_conv_b, a_ln_g=v_a_ln_g,
                 a_ln_b=v_a_ln_b, a_w_out=v_a_w_out, b_norm=v_b_norm, b_w_in=v_b_w_in, b_f_bias=v_b_f_bias,
                 b_q_norm=v_b_q_norm, b_k_norm=v_b_k_norm, b_w_out=v_b_w_out, c_norm=v_c_norm, c_w_in=v_c_w_in,
                 c_conv_w=v_c_conv_w, c_w_out=v_c_w_out)
    n_a = a_w_in.shape[0]
    d_model = x.shape[2]
    e_inner = a_w_out.shape[1] * N_CHIPS
    n_heads = b_f_bias.shape[1]
    nb_loc = b_w_in.shape[2]
    nb_pad = -(-nb_loc // LANES) * LANES
    chip = 2 * lax.axis_index("x") + lax.axis_index("y")

    big_shards = ([a_w_in[l].astype(BF16) for l in range(n_a)] + [a_w_out[l].astype(BF16) for l in range(n_a)]
                  + [_pad_cols(b_w_in[0], nb_pad).astype(BF16), b_w_out[0].astype(BF16), c_w_in[0].astype(BF16),
                     c_w_out[0].astype(BF16)])
    small_pack = _pack([w_loc[n] for n in SMALL_SHARDED])
    gathered = _gather_weights(big_shards + [small_pack], name="gather_weights")
    p = {}
    p["a_w_in"] = gathered[0:n_a]
    p["a_w_out"] = [g.reshape(e_inner, d_model) for g in gathered[n_a:2 * n_a]]
    gb, gbo, gci, gco, gsmall = gathered[2 * n_a:]
    wb_full = jnp.concatenate([gb[k, :, :nb_loc] for k in range(N_CHIPS)], axis=1)
    p["b_wq"] = wb_full[:, :4 * e_inner]
    p["b_wf"] = _pad_cols(wb_full[:, 4 * e_inner:], LANES)
    p["b_w_out"] = gbo.reshape(e_inner, d_model)
    p["c_w_in"] = gci
    p["c_w_out"] = gco.reshape(e_inner, d_model)
    small_shapes = [w_loc[n].shape for n in SMALL_SHARDED]
    per_chip = [_unpack(gsmall[k], small_shapes) for k in range(N_CHIPS)]
    for idx, n in enumerate(SMALL_SHARDED):
        p[n] = jnp.concatenate([per_chip[k][idx] for k in range(N_CHIPS)], axis=-1)
    for n in SMALL_REPLICATED:
        p[n] = w_loc[n]

    x0 = x[0]
    x1, sv0 = _mixer_a_fwd(x0, p, 0, "a0")
    x2, sv1 = _mixer_b_fwd(x1, p, "b0")
    x3, sv2 = _mixer_c_fwd(x2, p, "c0")
    x4, sv3 = _mixer_a_fwd(x3, p, 1, "a1")
    dy, loss_part = _loss_head(x4, loss_target[0], name="loss_head")
    loss = lax.psum(loss_part[0, 0], ("x", "y", "c"))

    dx, g3 = _mixer_a_bwd(dy, sv3, p, 1, "a1")
    dx, g2 = _mixer_c_bwd(dx, sv2, p, "c0")
    dx, g1 = _mixer_b_bwd(dx, sv1, p, "b0")
    dx, g0 = _mixer_a_bwd(dx, sv0, p, 0, "a0")
    grad_x = dx[None]

    half_rows = e_inner // N_CHIPS
    gb_full = g1["b_w_in"].reshape(d_model, N_CHIPS, nb_loc).transpose(1, 0, 2)
    big_grads = ([g0["a_w_in"], g3["a_w_in"]]
                 + [g0["a_w_out"].reshape(N_CHIPS, half_rows, d_model), g3["a_w_out"].reshape(N_CHIPS, half_rows, d_model)]
                 + [_pad_cols(gb_full, nb_pad), g1["b_w_out"].reshape(N_CHIPS, half_rows, d_model), g2["c_w_in"],
                    g2["c_w_out"].reshape(N_CHIPS, half_rows, d_model)])
    red = _reduce_grads(big_grads, name="reduce_grads")
    big_order = [("a_w_in", 0), ("a_w_in", 1), ("a_w_out", 0), ("a_w_out", 1), ("b_w_in", 0), ("b_w_out", 0),
                 ("c_w_in", 0), ("c_w_out", 0)]

    grads, delta, new_m, new_v = {}, {}, {}, {}
    per_layer = {n: [] for n in BIG}
    for (n, l), pair in zip(big_order, red):
        cols = pair[0].shape[1]
        true_cols = w_loc[n].shape[-1]
        outs = _adamw(_pad_cols(w_loc[n][l], cols), list(pair), _pad_cols(m_loc[n][l], cols), _pad_cols(v_loc[n][l], cols),
                      name=f"adamw_{n}{l}")
        per_layer[n].append([a[:, :true_cols] for a in outs])
    for n in BIG:
        grads[n], delta[n], new_m[n], new_v[n] = [jnp.stack([layer[k] for layer in per_layer[n]]) for k in range(4)]

    small_full = {}
    for n in ("a_norm", "a_conv_w", "a_conv_b", "a_ln_g", "a_ln_b"):
        small_full[n] = jnp.stack([g0[n].reshape(p[n].shape[1:]), g3[n].reshape(p[n].shape[1:])])
    small_full["c_norm"] = g2["c_norm"].reshape(p["c_norm"].shape)
    small_full["c_conv_w"] = g2["c_conv_w"].reshape(p["c_conv_w"].shape)
    for n in SMALL_REPLICATED:
        small_full[n] = g1[n].reshape(w_loc[n].shape)
    small_names = SMALL_SHARDED + SMALL_REPLICATED
    summed = _unpack(_allreduce_small(_pack([small_full[n] for n in small_names], 8), name="reduce_small"),
                     [small_full[n].shape for n in small_names])
    for n, s in zip(small_names, summed):
        if n in SMALL_SHARDED:
            width = w_loc[n].shape[-1]
            grads[n] = lax.dynamic_slice_in_dim(s, chip * width, width, axis=s.ndim - 1)
        else:
            grads[n] = s

    small_shapes_all = [w_loc[n].shape for n in small_names]
    _, d_, m_, v_ = _adamw(_pack([w_loc[n] for n in small_names], 8), [_pack([grads[n] for n in small_names], 8)],
                           _pack([m_loc[n] for n in small_names], 8), _pack([v_loc[n] for n in small_names], 8),
                           name="adamw_small")
    for n, a, b, c_ in zip(small_names, _unpack(d_, small_shapes_all), _unpack(m_, small_shapes_all),
                           _unpack(v_, small_shapes_all)):
        delta[n], new_m[n], new_v[n] = a, b, c_

    return (loss, grad_x, *[grads[n] for n in WEIGHTS], *[delta[n] for n in WEIGHTS],
            *[new_m[n] for n in WEIGHTS], *[new_v[n] for n in WEIGHTS])
```

```python
import functools

import numpy as np
import jax
import jax.numpy as jnp
from jax import lax
from jax.experimental import pallas as pl
from jax.experimental.pallas import tpu as pltpu

F32 = jnp.float32
BF16 = jnp.bfloat16
NORM_EPS = 1e-6
HEAD_DIM = 128
LANES = 128
N_CHIPS = 4
N_DEV = 8
VMEM_CAP = 56 << 20
MESH_ID = pl.DeviceIdType.MESH

ADAM_LR = 0.001
ADAM_B1 = 0.9
ADAM_B2 = 0.999
ADAM_EPS = 1e-08
ADAM_WD = 0.01
ADAM_STEP = 10


def _sds(shape, dtype):
    return jax.ShapeDtypeStruct(tuple(shape), dtype)


def _tile(n, pref):
    if n <= pref:
        return n
    for t in range(pref - pref % 8, 0, -8):
        if n % t == 0:
            return t
    raise ValueError(f"no tile for {n} under {pref}")


def _params(vmem_bytes):
    return pltpu.CompilerParams(vmem_limit_bytes=int(min(max(vmem_bytes, 16 << 20), VMEM_CAP)))


def _sigmoid(v):
    return 1.0 / (1.0 + jnp.exp(-v))


def _silu(v):
    return v * _sigmoid(v)


def _dsilu(v):
    s = _sigmoid(v)
    return s * (1.0 + v * (1.0 - s))


def _rowsum8(v):
    r, c = v.shape
    return jnp.sum(v.reshape(r // 8, 8, c), axis=0)


def _col_spec(grouped, rows_block, tile, width, row_of, col_of):
    if grouped:
        per = width // tile
        return pl.BlockSpec((None, rows_block, tile), lambda *ids: (col_of(*ids) // per, row_of(*ids), col_of(*ids) % per))
    return pl.BlockSpec((rows_block, tile), lambda *ids: (row_of(*ids), col_of(*ids)))


def _mm_nn(a, b, *, name, tm=1024, tn=512, out_dtype=F32, add=None):
    m, k = a.shape
    b_grouped = b.ndim == 3
    n = b.shape[0] * b.shape[2] if b_grouped else b.shape[1]
    width = b.shape[2] if b_grouped else n
    tm, tn = _tile(m, tm), _tile(width, tn)

    def body(*refs):
        if add is None:
            a_ref, b_ref, o_ref = refs
        else:
            a_ref, b_ref, r_ref, o_ref = refs
        acc = jnp.dot(a_ref[...].astype(BF16), b_ref[...].astype(BF16), preferred_element_type=F32)
        if add is not None:
            acc = acc + r_ref[...]
        o_ref[...] = acc.astype(o_ref.dtype)

    in_specs = [pl.BlockSpec((tm, k), lambda i, j: (i, 0)),
                _col_spec(b_grouped, k, tn, width, lambda i, j: 0, lambda i, j: j)]
    args = [a, b]
    if add is not None:
        in_specs.append(pl.BlockSpec((tm, tn), lambda i, j: (i, j)))
        args.append(add)
    vmem = 2 * (tm * k * a.dtype.itemsize + k * tn * 2 + tm * tn * 4 * (2 if add is not None else 1)) + tm * tn * 8 + tm * k * 2
    return pl.pallas_call(
        body, name=name, grid=(m // tm, n // tn), in_specs=in_specs,
        out_specs=pl.BlockSpec((tm, tn), lambda i, j: (i, j)),
        out_shape=_sds((m, n), out_dtype), compiler_params=_params(vmem + (4 << 20)),
    )(*args)


def _mm_nt(a, b, *, name, tm=1024, tn=1024, tk=512, ksub=1):
    a_grouped, b_grouped = a.ndim == 3, b.ndim == 3
    m = a.shape[1] if a_grouped else a.shape[0]
    n = a.shape[0] * a.shape[2] if a_grouped else a.shape[1]
    kk = b.shape[1] if b_grouped else b.shape[0]
    wa = a.shape[2] if a_grouped else n
    wb = b.shape[2] if b_grouped else n
    tm, tn = _tile(m, tm), _tile(kk, tn)
    tk = _tile(int(np.gcd(wa, wb)), tk)
    ksub = min(ksub, n // tk)
    assert (n // tk) % ksub == 0, (n, tk, ksub)
    steps = n // (tk * ksub)

    def body(*refs):
        a_refs, b_refs, o_ref = refs[:ksub], refs[ksub:2 * ksub], refs[2 * ksub]
        part = None
        for a_ref, b_ref in zip(a_refs, b_refs):
            d = lax.dot_general(a_ref[...].astype(BF16), b_ref[...].astype(BF16), (((1,), (1,)), ((), ())),
                                preferred_element_type=F32)
            part = d if part is None else part + d

        @pl.when(pl.program_id(2) == 0)
        def _():
            o_ref[...] = part

        @pl.when(pl.program_id(2) > 0)
        def _():
            o_ref[...] += part

    def sub(u):
        return lambda i, j, s: s * ksub + u

    in_specs = ([_col_spec(a_grouped, tm, tk, wa, lambda i, j, s: i, sub(u)) for u in range(ksub)]
                + [_col_spec(b_grouped, tn, tk, wb, lambda i, j, s: j, sub(u)) for u in range(ksub)])
    vmem = (2 * ksub * (tm * tk * a.dtype.itemsize + tn * tk * b.dtype.itemsize) + 2 * tm * tn * 4 + 2 * tm * tn * 4
            + (tm + tn) * tk * 2)
    return pl.pallas_call(
        body, name=name, grid=(m // tm, kk // tn, steps), in_specs=in_specs,
        out_specs=pl.BlockSpec((tm, tn), lambda i, j, s: (i, j)),
        out_shape=_sds((m, kk), F32), compiler_params=_params(vmem + (4 << 20)),
    )(*([a] * ksub), *([b] * ksub))


def _mm_tn(a, b, *, name, out_width=None, out_dtype=F32, tk=1024, tn=512, ts=512):
    s_len, k = a.shape
    b_grouped = b.ndim == 3
    n = b.shape[0] * b.shape[2] if b_grouped else b.shape[1]
    wb = b.shape[2] if b_grouped else n
    wo = out_width if out_width is not None else n
    tk, ts = _tile(k, tk), _tile(s_len, ts)
    tn = _tile(int(np.gcd(wb, wo)), tn)
    last = s_len // ts - 1
    direct = out_dtype == F32

    def body(a_ref, b_ref, o_ref, *scratch):
        acc = o_ref if direct else scratch[0]
        part = lax.dot_general(a_ref[...].astype(BF16), b_ref[...].astype(BF16), (((0,), (0,)), ((), ())),
                               preferred_element_type=F32)

        @pl.when(pl.program_id(2) == 0)
        def _():
            acc[...] = part

        @pl.when(pl.program_id(2) > 0)
        def _():
            acc[...] += part

        if not direct:
            @pl.when(pl.program_id(2) == last)
            def _():
                o_ref[...] = acc[...].astype(o_ref.dtype)

    in_specs = [pl.BlockSpec((ts, tk), lambda i, j, s: (s, i)),
                _col_spec(b_grouped, ts, tn, wb, lambda i, j, s: s, lambda i, j, s: j)]
    out_grouped = out_width is not None
    out_spec = _col_spec(out_grouped, tk, tn, wo, lambda i, j, s: i, lambda i, j, s: j)
    out_shape = _sds((n // wo, k, wo), out_dtype) if out_grouped else _sds((k, n), out_dtype)
    vmem = 2 * (ts * tk * a.dtype.itemsize + ts * tn * b.dtype.itemsize + tk * tn * 4) + 2 * tk * tn * 4 + ts * (tk + tn) * 4
    return pl.pallas_call(
        body, name=name, grid=(k // tk, n // tn, s_len // ts), in_specs=in_specs, out_specs=out_spec,
        out_shape=out_shape, scratch_shapes=[] if direct else [pltpu.VMEM((tk, tn), F32)],
        compiler_params=_params(vmem + (4 << 20)),
    )(a, b)


def _rms_fwd(x, g, *, name, ts=512):
    s_len, d = x.shape
    ts = _tile(s_len, ts)

    def body(x_ref, g_ref, h_ref):
        xf = x_ref[...]
        r = lax.rsqrt(jnp.mean(xf * xf, axis=-1, keepdims=True) + NORM_EPS)
        h_ref[...] = ((xf * r) * g_ref[...]).astype(h_ref.dtype)

    return pl.pallas_call(
        body, name=name, grid=(s_len // ts,),
        in_specs=[pl.BlockSpec((ts, d), lambda i: (i, 0)), pl.BlockSpec((1, d), lambda i: (0, 0))],
        out_specs=pl.BlockSpec((ts, d), lambda i: (i, 0)), out_shape=_sds((s_len, d), BF16),
        compiler_params=_params(8 * ts * d * 4),
    )(x, g)


def _rms_bwd(x, g, dhs, dres, *, name, ts=512):
    s_len, d = x.shape
    ts = _tile(s_len, ts)
    n_dh = len(dhs)
    last = s_len // ts - 1

    def body(*refs):
        x_ref, g_ref = refs[0], refs[1]
        dh_refs = refs[2:2 + n_dh]
        dres_ref, dx_ref, dg_ref, acc = refs[2 + n_dh:]
        i = pl.program_id(0)

        @pl.when(i == 0)
        def _():
            acc[...] = jnp.zeros_like(acc)

        xf = x_ref[...]
        dy = dh_refs[0][...]
        for r_ in dh_refs[1:]:
            dy = dy + r_[...]
        r = lax.rsqrt(jnp.mean(xf * xf, axis=-1, keepdims=True) + NORM_EPS)
        gd = dy * g_ref[...]
        dot = jnp.mean(xf * gd, axis=-1, keepdims=True)
        dx_ref[...] = dres_ref[...] + r * gd - xf * (r * r * r * dot)
        acc[...] += _rowsum8(dy * (xf * r))

        @pl.when(i == last)
        def _():
            dg_ref[...] = jnp.sum(acc[...], axis=0, keepdims=True)

    row = pl.BlockSpec((ts, d), lambda i: (i, 0))
    vec = pl.BlockSpec((1, d), lambda i: (0, 0))
    return pl.pallas_call(
        body, name=name, grid=(s_len // ts,),
        in_specs=[row, vec] + [row] * n_dh + [row],
        out_specs=[row, vec], out_shape=[_sds((s_len, d), F32), _sds((1, d), F32)],
        scratch_shapes=[pltpu.VMEM((8, d), F32)],
        compiler_params=_params((2 * (3 + n_dh) + 6) * ts * d * 4),
    )(x, g, *dhs, dres)


def _loss_head(y, target, *, name, ts=512):
    s_len, d = y.shape
    ts = _tile(s_len, ts)
    last = s_len // ts - 1

    def body(y_ref, t_ref, dy_ref, loss_ref, acc):
        i = pl.program_id(0)

        @pl.when(i == 0)
        def _():
            acc[...] = jnp.zeros_like(acc)

        err = y_ref[...] - t_ref[...]
        dy_ref[...] = err / d
        acc[...] += _rowsum8(err * err)

        @pl.when(i == last)
        def _():
            loss_ref[...] = (0.5 * jnp.sum(acc[...]) / d).reshape(1, 1)

    row = pl.BlockSpec((ts, d), lambda i: (i, 0))
    return pl.pallas_call(
        body, name=name, grid=(s_len // ts,), in_specs=[row, row],
        out_specs=[row, pl.BlockSpec((1, 1), lambda i: (0, 0))],
        out_shape=[_sds((s_len, d), F32), _sds((1, 1), F32)],
        scratch_shapes=[pltpu.VMEM((8, d), F32)],
        compiler_params=_params(10 * ts * d * 4),
    )(y, target)


CONV_ROWS = 64
CONV_COLS = 256


SUBLANES = 8


def _fill_shifts(buf, sh_ref):
    n = sh_ref.shape[1]
    for s in range(1, SUBLANES):
        sh_ref[s - 1, :, :] = buf[s:s + n, :]


def _shifted_rows(buf, sh_ref, start, rw, cs):
    s = start % SUBLANES
    if sh_ref is None or s == 0:
        return buf[start:start + rw, cs]
    return sh_ref[s - 1, start - s:start - s + rw, cs]


def _conv_taps(buf, sh_ref, w_ref, k_width, base, ts, tc, init, emit, reverse=False):
    cw = min(tc, CONV_COLS)
    rw = min(ts, CONV_ROWS)
    for cb in range(tc // cw):
        cs = slice(cb * cw, (cb + 1) * cw)
        for rb in range(ts // rw):
            acc = init(slice(rb * rw, (rb + 1) * rw), cs, (rw, cw))
            for k in range(k_width):
                sh = (k_width - 1 - k) if reverse else k
                acc = acc + w_ref[k:k + 1, cs] * _shifted_rows(buf, sh_ref, base + rb * rw + sh, rw, cs)
            emit(slice(rb * rw, (rb + 1) * rw), cs, acc)


def _conv_wgrad(buf, sh_ref, d_ref_val, acc_ref, k_width, base, ts, tc):
    cw = min(tc, CONV_COLS)
    rw = min(ts, CONV_ROWS)
    for cb in range(tc // cw):
        cs = slice(cb * cw, (cb + 1) * cw)
        for rb in range(ts // rw):
            dv = d_ref_val[rb * rw:(rb + 1) * rw, cs]
            for k in range(k_width):
                prod = dv * _shifted_rows(buf, sh_ref, base + rb * rw + k, rw, cs)
                acc_ref[8 * k:8 * k + 8, cs] += _rowsum8(prod)


A_HALO = 32


def _a_conv_fwd(proj, conv_w, conv_b, *, name, ts=256, tc=512):
    s_len, e3 = proj.shape
    e = e3 // 3
    k_width = conv_w.shape[0]
    ts, tc = _tile(s_len, ts), _tile(e, tc)
    nc = e // tc
    kp = 32

    def body(val, gate, valh, gateh, w_ref, b_ref, u1_ref, buf, sh):
        i = pl.program_id(1)
        u0h = valh[...] * _sigmoid(gateh[...])
        buf[0:A_HALO, :] = jnp.where(i > 0, u0h, 0.0)
        buf[A_HALO:A_HALO + ts, :] = val[...] * _sigmoid(gate[...])
        _fill_shifts(buf, sh)

        def init(rows, cs, shape):
            return jnp.broadcast_to(b_ref[:, cs], shape)

        def emit(rows, cs, acc):
            u1_ref[rows, cs] = acc

        _conv_taps(buf, sh, w_ref, k_width, A_HALO - (k_width - 1), ts, tc, init, emit)

    hb = ts // A_HALO
    in_specs = [
        pl.BlockSpec((ts, tc), lambda j, i: (i, j)),
        pl.BlockSpec((ts, tc), lambda j, i: (i, nc + j)),
        pl.BlockSpec((A_HALO, tc), lambda j, i: (jnp.maximum(i * hb - 1, 0), j)),
        pl.BlockSpec((A_HALO, tc), lambda j, i: (jnp.maximum(i * hb - 1, 0), nc + j)),
        pl.BlockSpec((kp, tc), lambda j, i: (0, j)),
        pl.BlockSpec((1, tc), lambda j, i: (0, j)),
    ]
    w_pad = jnp.zeros((kp, e), F32).at[:k_width].set(conv_w)
    return pl.pallas_call(
        body, name=name, grid=(nc, s_len // ts), in_specs=in_specs,
        out_specs=pl.BlockSpec((ts, tc), lambda j, i: (i, j)), out_shape=_sds((s_len, e), F32),
        scratch_shapes=[pltpu.VMEM((A_HALO + ts, tc), F32), pltpu.VMEM((SUBLANES - 1, A_HALO + ts - SUBLANES, tc), F32)],
        compiler_params=_params(24 * ts * tc * 4),
    )(proj, proj, proj, proj, w_pad, conv_b)


def _ln_rows(u1, g, b):
    mu = jnp.mean(u1, axis=-1, keepdims=True)
    xc = u1 - mu
    var = jnp.mean(xc * xc, axis=-1, keepdims=True)
    rstd = lax.rsqrt(var + NORM_EPS)
    xhat = xc * rstd
    return xhat, rstd, xhat * g + b


def _a_post_fwd(u1, proj, ln_g, ln_b, *, name, ts=256):
    s_len, e = u1.shape
    ts = _tile(s_len, ts)

    def body(u1_ref, z_ref, g_ref, b_ref, o_ref):
        _, _, u2 = _ln_rows(u1_ref[...], g_ref[...], b_ref[...])
        o_ref[...] = (_silu(u2) * _silu(z_ref[...])).astype(o_ref.dtype)

    row = pl.BlockSpec((ts, e), lambda i: (i, 0))
    vec = pl.BlockSpec((1, e), lambda i: (0, 0))
    return pl.pallas_call(
        body, name=name, grid=(s_len // ts,),
        in_specs=[row, pl.BlockSpec((ts, e), lambda i: (i, 2)), vec, vec],
        out_specs=row, out_shape=_sds((s_len, e), BF16), compiler_params=_params(12 * ts * e * 4),
    )(u1, proj, ln_g, ln_b)


def _a_post_bwd(dgated, u1, proj, ln_g, ln_b, *, name, ts=256):
    s_len, e = u1.shape
    ts = _tile(s_len, ts)
    last = s_len // ts - 1

    def body(dg_ref, u1_ref, z_ref, g_ref, b_ref, du1_ref, dz_ref, dlg_ref, dlb_ref, dcb_ref, a_g, a_b, a_c):
        i = pl.program_id(0)

        @pl.when(i == 0)
        def _():
            a_g[...] = jnp.zeros_like(a_g)
            a_b[...] = jnp.zeros_like(a_b)
            a_c[...] = jnp.zeros_like(a_c)

        g = g_ref[...]
        xhat, rstd, u2 = _ln_rows(u1_ref[...], g, b_ref[...])
        z = z_ref[...]
        dgt = dg_ref[...]
        dz_ref[...] = (dgt * _silu(u2) * _dsilu(z)).astype(dz_ref.dtype)
        du2 = dgt * _silu(z) * _dsilu(u2)
        a_g[...] += _rowsum8(du2 * xhat)
        a_b[...] += _rowsum8(du2)
        dxh = du2 * g
        m1 = jnp.mean(dxh, axis=-1, keepdims=True)
        m2 = jnp.mean(dxh * xhat, axis=-1, keepdims=True)
        du1 = rstd * (dxh - m1 - xhat * m2)
        du1_ref[...] = du1
        a_c[...] += _rowsum8(du1)

        @pl.when(i == last)
        def _():
            dlg_ref[...] = jnp.sum(a_g[...], axis=0, keepdims=True)
            dlb_ref[...] = jnp.sum(a_b[...], axis=0, keepdims=True)
            dcb_ref[...] = jnp.sum(a_c[...], axis=0, keepdims=True)

    row = pl.BlockSpec((ts, e), lambda i: (i, 0))
    vec = pl.BlockSpec((1, e), lambda i: (0, 0))
    return pl.pallas_call(
        body, name=name, grid=(s_len // ts,),
        in_specs=[row, row, pl.BlockSpec((ts, e), lambda i: (i, 2)), vec, vec],
        out_specs=[row, row, vec, vec, vec],
        out_shape=[_sds((s_len, e), F32), _sds((s_len, e), BF16), _sds((1, e), F32), _sds((1, e), F32), _sds((1, e), F32)],
        scratch_shapes=[pltpu.VMEM((8, e), F32)] * 3,
        compiler_params=_params(20 * ts * e * 4),
    )(dgated, u1, proj, ln_g, ln_b)


def _a_conv_bwd(du1, proj, dz, conv_w, *, name, ts=256, tc=512):
    s_len, e3 = proj.shape
    e = e3 // 3
    k_width = conv_w.shape[0]
    ts, tc = _tile(s_len, ts), _tile(e, tc)
    nc, nr = e // tc, s_len // ts
    kp = 32
    hb = ts // A_HALO

    def body(val, gate, valh, gateh, d_ref, dh_ref, dz_ref, w_ref, dp_ref, dw_ref, buf_u, buf_d, du0, acc, sh_u, sh_d):
        i = pl.program_id(1)

        @pl.when(i == 0)
        def _():
            acc[...] = jnp.zeros_like(acc)

        buf_u[0:A_HALO, :] = jnp.where(i > 0, valh[...] * _sigmoid(gateh[...]), 0.0)
        buf_u[A_HALO:A_HALO + ts, :] = val[...] * _sigmoid(gate[...])
        buf_d[0:ts, :] = d_ref[...]
        buf_d[ts:ts + A_HALO, :] = jnp.where(i < nr - 1, dh_ref[...], 0.0)
        _fill_shifts(buf_u, sh_u)
        _fill_shifts(buf_d, sh_d)

        def init(rows, cs, shape):
            return jnp.zeros(shape, F32)

        def emit(rows, cs, a):
            du0[rows, cs] = a

        _conv_taps(buf_d, sh_d, w_ref, k_width, 0, ts, tc, init, emit, reverse=True)
        _conv_wgrad(buf_u, sh_u, buf_d, acc, k_width, A_HALO - (k_width - 1), ts, tc)
        rw = min(ts, CONV_ROWS)
        for rb in range(ts // rw):
            rows = slice(rb * rw, (rb + 1) * rw)
            d0 = du0[rows, :]
            sg = _sigmoid(gate[rows, :])
            dp_ref[0, rows, :] = (d0 * sg).astype(dp_ref.dtype)
            dp_ref[1, rows, :] = (d0 * val[rows, :] * sg * (1.0 - sg)).astype(dp_ref.dtype)
        dp_ref[2] = dz_ref[...]

        @pl.when(i == nr - 1)
        def _():
            for k in range(kp):
                dw_ref[k:k + 1, :] = jnp.sum(acc[8 * k:8 * k + 8, :], axis=0, keepdims=True)

    in_specs = [
        pl.BlockSpec((ts, tc), lambda j, i: (i, j)),
        pl.BlockSpec((ts, tc), lambda j, i: (i, nc + j)),
        pl.BlockSpec((A_HALO, tc), lambda j, i: (jnp.maximum(i * hb - 1, 0), j)),
        pl.BlockSpec((A_HALO, tc), lambda j, i: (jnp.maximum(i * hb - 1, 0), nc + j)),
        pl.BlockSpec((ts, tc), lambda j, i: (i, j)),
        pl.BlockSpec((A_HALO, tc), lambda j, i: (jnp.minimum((i + 1) * hb, nr * hb - 1), j)),
        pl.BlockSpec((ts, tc), lambda j, i: (i, j)),
        pl.BlockSpec((kp, tc), lambda j, i: (0, j)),
    ]
    w_pad = jnp.zeros((kp, e), F32).at[:k_width].set(conv_w)
    dproj, dw = pl.pallas_call(
        body, name=name, grid=(nc, nr), in_specs=in_specs,
        out_specs=[pl.BlockSpec((3, ts, tc), lambda j, i: (0, i, j)), pl.BlockSpec((kp, tc), lambda j, i: (0, j))],
        out_shape=[_sds((3, s_len, e), BF16), _sds((kp, e), F32)],
        scratch_shapes=[pltpu.VMEM((A_HALO + ts, tc), F32), pltpu.VMEM((ts + A_HALO, tc), F32),
                        pltpu.VMEM((ts, tc), F32), pltpu.VMEM((8 * kp, tc), F32),
                        pltpu.VMEM((SUBLANES - 1, A_HALO + ts - SUBLANES, tc), F32),
                        pltpu.VMEM((SUBLANES - 1, A_HALO + ts - SUBLANES, tc), F32)],
        compiler_params=_params(56 * ts * tc * 4),
    )(proj, proj, proj, proj, du1, du1, dz, w_pad)
    return dproj, dw[:k_width]


C_HALO = 8


def _c_mid_fwd(proj, conv_w, *, name, ts=256, tc=512):
    s_len, e4 = proj.shape
    e = e4 // 4
    k_width = conv_w.shape[0]
    ts, tc = _tile(s_len, ts), _tile(e, tc)
    nc = e // tc
    hb = ts // C_HALO

    def body(u, bg, cg, z, uh, cgh, w_ref, o_ref, buf, y):
        i = pl.program_id(1)
        buf[0:C_HALO, :] = jnp.where(i > 0, uh[...] * cgh[...], 0.0)
        buf[C_HALO:C_HALO + ts, :] = u[...] * cg[...]

        def init(rows, cs, shape):
            return jnp.zeros(shape, F32)

        def emit(rows, cs, a):
            y[rows, cs] = a

        _conv_taps(buf, None, w_ref, k_width, C_HALO - (k_width - 1), ts, tc, init, emit)
        o_ref[...] = (bg[...] * y[...] * _silu(z[...])).astype(o_ref.dtype)

    def grp(g):
        return pl.BlockSpec((ts, tc), lambda j, i: (i, g * nc + j))

    def halo(g):
        return pl.BlockSpec((C_HALO, tc), lambda j, i: (jnp.maximum(i * hb - 1, 0), g * nc + j))

    w_pad = jnp.zeros((8, e), F32).at[:k_width].set(conv_w)
    return pl.pallas_call(
        body, name=name, grid=(nc, s_len // ts),
        in_specs=[grp(0), grp(1), grp(2), grp(3), halo(0), halo(2), pl.BlockSpec((8, tc), lambda j, i: (0, j))],
        out_specs=pl.BlockSpec((ts, tc), lambda j, i: (i, j)), out_shape=_sds((s_len, e), BF16),
        scratch_shapes=[pltpu.VMEM((C_HALO + ts, tc), F32), pltpu.VMEM((ts, tc), F32)],
        compiler_params=_params(16 * ts * tc * 4),
    )(proj, proj, proj, proj, proj, proj, w_pad)


def _c_mid_bwd(dgated, proj, conv_w, *, name, ts=256, tc=512):
    s_len, e4 = proj.shape
    e = e4 // 4
    k_width = conv_w.shape[0]
    ts, tc = _tile(s_len, ts), _tile(e, tc)
    nc, nr = e // tc, s_len // ts
    hb = ts // C_HALO

    def body(u, bg, cg, z, uh, cgh, dg, dgh, bgh, zh, w_ref, dp_ref, dw_ref, buf_p, buf_d, y, dpv, acc):
        i = pl.program_id(1)

        @pl.when(i == 0)
        def _():
            acc[...] = jnp.zeros_like(acc)

        uv, bgv, cgv, zv, dgv = u[...], bg[...], cg[...], z[...], dg[...]
        buf_p[0:C_HALO, :] = jnp.where(i > 0, uh[...] * cgh[...], 0.0)
        buf_p[C_HALO:C_HALO + ts, :] = uv * cgv
        sz = _silu(zv)
        buf_d[0:ts, :] = dgv * sz * bgv
        buf_d[ts:ts + C_HALO, :] = jnp.where(i < nr - 1, dgh[...] * _silu(zh[...]) * bgh[...], 0.0)

        def init(rows, cs, shape):
            return jnp.zeros(shape, F32)

        def emit_y(rows, cs, a):
            y[rows, cs] = a

        def emit_dp(rows, cs, a):
            dpv[rows, cs] = a

        _conv_taps(buf_p, None, w_ref, k_width, C_HALO - (k_width - 1), ts, tc, init, emit_y)
        _conv_taps(buf_d, None, w_ref, k_width, 0, ts, tc, init, emit_dp, reverse=True)
        _conv_wgrad(buf_p, None, buf_d, acc, k_width, C_HALO - (k_width - 1), ts, tc)
        yv, dp = y[...], dpv[...]
        dp_ref[0] = (dp * cgv).astype(dp_ref.dtype)
        dp_ref[1] = (dgv * sz * yv).astype(dp_ref.dtype)
        dp_ref[2] = (dp * uv).astype(dp_ref.dtype)
        dp_ref[3] = (dgv * bgv * yv * _dsilu(zv)).astype(dp_ref.dtype)

        @pl.when(i == nr - 1)
        def _():
            for k in range(8):
                dw_ref[k:k + 1, :] = jnp.sum(acc[8 * k:8 * k + 8, :], axis=0, keepdims=True)

    def grp(g):
        return pl.BlockSpec((ts, tc), lambda j, i: (i, g * nc + j))

    def prev(g):
        return pl.BlockSpec((C_HALO, tc), lambda j, i: (jnp.maximum(i * hb - 1, 0), g * nc + j))

    def nxt(g):
        return pl.BlockSpec((C_HALO, tc), lambda j, i: (jnp.minimum((i + 1) * hb, nr * hb - 1), g * nc + j))

    w_pad = jnp.zeros((8, e), F32).at[:k_width].set(conv_w)
    dproj, dw = pl.pallas_call(
        body, name=name, grid=(nc, nr),
        in_specs=[grp(0), grp(1), grp(2), grp(3), prev(0), prev(2),
                  pl.BlockSpec((ts, tc), lambda j, i: (i, j)),
                  pl.BlockSpec((C_HALO, tc), lambda j, i: (jnp.minimum((i + 1) * hb, nr * hb - 1), j)),
                  nxt(1), nxt(3), pl.BlockSpec((8, tc), lambda j, i: (0, j))],
        out_specs=[pl.BlockSpec((4, ts, tc), lambda j, i: (0, i, j)), pl.BlockSpec((8, tc), lambda j, i: (0, j))],
        out_shape=[_sds((4, s_len, e), BF16), _sds((8, e), F32)],
        scratch_shapes=[pltpu.VMEM((C_HALO + ts, tc), F32), pltpu.VMEM((ts + C_HALO, tc), F32),
                        pltpu.VMEM((ts, tc), F32), pltpu.VMEM((ts, tc), F32), pltpu.VMEM((64, tc), F32)],
        compiler_params=_params(32 * ts * tc * 4),
    )(proj, proj, proj, proj, proj, proj, dgated, dgated, proj, proj, w_pad)
    return dproj, dw[:k_width]


def _head_rms(xv, g):
    r = lax.rsqrt(jnp.mean(xv * xv, axis=-1, keepdims=True) + NORM_EPS)
    return r, xv * r * g


def _b_qk_fwd(proj, gq, gk, *, name, ts=512, tc=512):
    s_len, e4 = proj.shape
    e = e4 // 4
    ts, tc = _tile(s_len, ts), _tile(e, tc)
    nc = e // tc

    def body(q, k, v, gq_ref, gk_ref, qn, kn, vb):
        for h in range(tc // HEAD_DIM):
            cs = slice(h * HEAD_DIM, (h + 1) * HEAD_DIM)
            qn[:, cs] = _head_rms(q[:, cs], gq_ref[...])[1].astype(qn.dtype)
            kn[:, cs] = _head_rms(k[:, cs], gk_ref[...])[1].astype(kn.dtype)
        vb[...] = v[...].astype(vb.dtype)

    def grp(g):
        return pl.BlockSpec((ts, tc), lambda i, j: (i, g * nc + j))

    vec = pl.BlockSpec((1, HEAD_DIM), lambda i, j: (0, 0))
    out = pl.BlockSpec((ts, tc), lambda i, j: (i, j))
    return pl.pallas_call(
        body, name=name, grid=(s_len // ts, nc), in_specs=[grp(0), grp(1), grp(2), vec, vec],
        out_specs=[out, out, out], out_shape=[_sds((s_len, e), BF16)] * 3,
        compiler_params=_params(16 * ts * tc * 4),
    )(proj, proj, proj, gq, gk)


def _b_qk_bwd(dqn, dkn, dv, dz, proj, gq, gk, *, name, ts=512, tc=512):
    s_len, e4 = proj.shape
    e = e4 // 4
    ts, tc = _tile(s_len, ts), _tile(e, tc)
    nc, nr = e // tc, s_len // ts

    def body(dq_ref, dk_ref, dv_ref, dz_ref, q, k, gq_ref, gk_ref, dp_ref, dgq_ref, dgk_ref, a_q, a_k):
        i, j = pl.program_id(0), pl.program_id(1)

        @pl.when((i == 0) & (j == 0))
        def _():
            a_q[...] = jnp.zeros_like(a_q)
            a_k[...] = jnp.zeros_like(a_k)

        for h in range(tc // HEAD_DIM):
            cs = slice(h * HEAD_DIM, (h + 1) * HEAD_DIM)
            for slot, src, d_ref, g_ref, acc in ((0, q, dq_ref, gq_ref, a_q), (1, k, dk_ref, gk_ref, a_k)):
                xv = src[:, cs]
                dy = d_ref[:, cs]
                r = lax.rsqrt(jnp.mean(xv * xv, axis=-1, keepdims=True) + NORM_EPS)
                gd = dy * g_ref[...]
                dot = jnp.mean(xv * gd, axis=-1, keepdims=True)
                dp_ref[slot, :, cs] = (r * gd - xv * (r * r * r * dot)).astype(dp_ref.dtype)
                acc[...] += _rowsum8(dy * (xv * r))
        dp_ref[2] = dv_ref[...].astype(dp_ref.dtype)
        dp_ref[3] = dz_ref[...]

        @pl.when((i == nr - 1) & (j == nc - 1))
        def _():
            dgq_ref[...] = jnp.sum(a_q[...], axis=0, keepdims=True)
            dgk_ref[...] = jnp.sum(a_k[...], axis=0, keepdims=True)

    blk = pl.BlockSpec((ts, tc), lambda i, j: (i, j))
    vec = pl.BlockSpec((1, HEAD_DIM), lambda i, j: (0, 0))

    def grp(g):
        return pl.BlockSpec((ts, tc), lambda i, j: (i, g * nc + j))

    return pl.pallas_call(
        body, name=name, grid=(nr, nc), in_specs=[blk, blk, blk, blk, grp(0), grp(1), vec, vec],
        out_specs=[pl.BlockSpec((4, ts, tc), lambda i, j: (0, i, j)), vec, vec],
        out_shape=[_sds((4, s_len, e), BF16), _sds((1, HEAD_DIM), F32), _sds((1, HEAD_DIM), F32)],
        scratch_shapes=[pltpu.VMEM((8, HEAD_DIM), F32)] * 2,
        compiler_params=_params(24 * ts * tc * 4),
    )(dqn, dkn, dv, dz, proj, proj, gq, gk)


def _log_sigmoid(x):
    y = jnp.exp(-jnp.abs(x))
    u = 1.0 + y
    log1p = jnp.where(u == 1.0, y, jnp.log(u) * (y / jnp.where(u == 1.0, 1.0, u - 1.0)))
    return jnp.minimum(x, 0.0) - log1p


def _split3(v):
    hi = v.astype(BF16)
    r1 = v - hi.astype(F32)
    mid = r1.astype(BF16)
    lo = (r1 - mid.astype(F32)).astype(BF16)
    return hi, mid, lo


def _tri_matmul(tri, v):
    hi, mid, lo = _split3(v)
    return (jnp.dot(tri, hi, preferred_element_type=F32) + jnp.dot(tri, mid, preferred_element_type=F32)
            + jnp.dot(tri, lo, preferred_element_type=F32))


def _b_cumsum(fl, bias, *, name, t=512):
    s_len, w = fl.shape
    t = _tile(s_len, t)

    def body(fl_ref, b_ref, c_ref, carry):
        @pl.when(pl.program_id(0) == 0)
        def _():
            carry[...] = jnp.zeros_like(carry)

        logf = _log_sigmoid(fl_ref[...] + b_ref[...])
        row = lax.broadcasted_iota(jnp.int32, (t, t), 0)
        col = lax.broadcasted_iota(jnp.int32, (t, t), 1)
        tri = jnp.where(col <= row, 1.0, 0.0).astype(BF16)
        c = _tri_matmul(tri, logf) + carry[...]
        c_ref[...] = c
        carry[...] = c[t - 1:t, :]

    return pl.pallas_call(
        body, name=name, grid=(s_len // t,),
        in_specs=[pl.BlockSpec((t, w), lambda i: (i, 0)), pl.BlockSpec((1, w), lambda i: (0, 0))],
        out_specs=pl.BlockSpec((t, w), lambda i: (i, 0)), out_shape=_sds((s_len, w), F32),
        scratch_shapes=[pltpu.VMEM((1, w), F32)], compiler_params=_params(16 << 20),
    )(fl, bias)


def _b_cumsum_bwd(dc, fl, bias, *, name, t=512):
    s_len, w = fl.shape
    t = _tile(s_len, t)
    n = s_len // t

    def body(dc_ref, fl_ref, b_ref, dfl_ref, db_ref, carry, acc):
        i = pl.program_id(0)

        @pl.when(i == 0)
        def _():
            carry[...] = jnp.zeros_like(carry)
            acc[...] = jnp.zeros_like(acc)

        row = lax.broadcasted_iota(jnp.int32, (t, t), 0)
        col = lax.broadcasted_iota(jnp.int32, (t, t), 1)
        tri = jnp.where(col >= row, 1.0, 0.0).astype(BF16)
        dlogf = _tri_matmul(tri, dc_ref[...]) + carry[...]
        carry[...] = dlogf[0:1, :]
        dfl = dlogf * _sigmoid(-(fl_ref[...] + b_ref[...]))
        dfl_ref[...] = dfl
        acc[...] += _rowsum8(dfl)

        @pl.when(i == n - 1)
        def _():
            db_ref[...] = jnp.sum(acc[...], axis=0, keepdims=True)

    rev = pl.BlockSpec((t, w), lambda i: (n - 1 - i, 0))
    vec = pl.BlockSpec((1, w), lambda i: (0, 0))
    return pl.pallas_call(
        body, name=name, grid=(n,), in_specs=[rev, rev, vec], out_specs=[rev, vec],
        out_shape=[_sds((s_len, w), F32), _sds((1, w), F32)],
        scratch_shapes=[pltpu.VMEM((1, w), F32), pltpu.VMEM((8, w), F32)], compiler_params=_params(16 << 20),
    )(dc, fl, bias)


def _b_gate_fwd(o, proj, *, name, ts=512, tc=512):
    s_len, e = o.shape
    ts, tc = _tile(s_len, ts), _tile(e, tc)
    nc = e // tc

    def body(o_ref, z_ref, g_ref):
        g_ref[...] = (o_ref[...] * _silu(z_ref[...])).astype(g_ref.dtype)

    blk = pl.BlockSpec((ts, tc), lambda i, j: (i, j))
    return pl.pallas_call(
        body, name=name, grid=(s_len // ts, nc),
        in_specs=[blk, pl.BlockSpec((ts, tc), lambda i, j: (i, 3 * nc + j))], out_specs=blk,
        out_shape=_sds((s_len, e), BF16), compiler_params=_params(12 * ts * tc * 4),
    )(o, proj)


def _b_gate_bwd(dgated, o, o_lo, proj, *, name, hg, ts=512):
    s_len, e = o.shape
    ts = _tile(s_len, ts)
    w = hg * HEAD_DIM
    ng = e // w

    def body(dg_ref, o_ref, olo_ref, z_ref, do_ref, dz_ref, dl_ref):
        dgt, ov, zv = dg_ref[...], o_ref[...], z_ref[...]
        dob = (dgt * _silu(zv)).astype(do_ref.dtype)
        do_ref[...] = dob
        dz_ref[...] = (dgt * ov * _dsilu(zv)).astype(dz_ref.dtype)
        prod = dob.astype(F32) * (ov + olo_ref[...])
        lane = lax.broadcasted_iota(jnp.int32, (ts, LANES), 1)
        d = jnp.zeros((ts, LANES), F32)
        for hh in range(hg):
            d = jnp.where(lane == hh, jnp.sum(prod[:, hh * HEAD_DIM:(hh + 1) * HEAD_DIM], axis=-1, keepdims=True), d)
        dl_ref[...] = d

    blk = pl.BlockSpec((ts, w), lambda i, j: (i, j))
    return pl.pallas_call(
        body, name=name, grid=(s_len // ts, ng),
        in_specs=[blk, blk, blk, pl.BlockSpec((ts, w), lambda i, j: (i, 3 * ng + j))],
        out_specs=[blk, blk, pl.BlockSpec((ts, LANES), lambda i, j: (i, j))],
        out_shape=[_sds((s_len, e), BF16), _sds((s_len, e), BF16), _sds((s_len, ng * LANES), F32)],
        compiler_params=_params(20 * ts * w * 4),
    )(dgated, o, o_lo, proj)


LOG2E = 1.4426950408889634
ATTN_ROW_CHUNK = 128


def _tri_tables(n, k_major):
    pairs = [(i, j) for j in range(n) for i in range(j, n)] if k_major else [(i, j) for i in range(n) for j in range(i + 1)]
    return (jnp.asarray(np.array([p[0] for p in pairs], np.int32)), jnp.asarray(np.array([p[1] for p in pairs], np.int32)))


def _attn_logits2(s_raw, cr2, diag, row0, c1):
    s2 = s_raw * c1 - cr2
    if diag:
        rc, t = s_raw.shape
        row = lax.broadcasted_iota(jnp.int32, (rc, t), 0) + row0
        col = lax.broadcasted_iota(jnp.int32, (rc, t), 1)
        s2 = jnp.where(col <= row, s2, -jnp.inf)
    return s2


def _fox_fwd(qn, kn, vb, c_row, *, name, hg, t=512):
    s_len, e = qn.shape
    t = _tile(s_len, t)
    rc = _tile(t, ATTN_ROW_CHUNK)
    w = hg * HEAD_DIM
    ng, n = e // w, s_len // t
    c1 = HEAD_DIM ** -0.5 * LOG2E
    qi_tab, kj_tab = _tri_tables(n, k_major=False)

    def body(qi_ref, kj_ref, q_ref, k_ref, v_ref, cr_ref, o_ref, olo_ref, lse_ref, m_s, l_s, acc_s, lo_s, s_scr, p_scr,
             a_scr):
        pid = pl.program_id(1)
        i, j = qi_ref[pid], kj_ref[pid]

        @pl.when(j == 0)
        def _():
            m_s[...] = jnp.full_like(m_s, -jnp.inf)
            l_s[...] = jnp.zeros_like(l_s)
            acc_s[...] = jnp.zeros_like(acc_s)
            lo_s[...] = jnp.zeros_like(lo_s)

        def step(diag):
            for hh in range(hg):
                cs = slice(hh * HEAD_DIM, (hh + 1) * HEAD_DIM)
                s_scr[...] = lax.dot_general(q_ref[:, cs], k_ref[:, cs], (((1,), (1,)), ((), ())),
                                             preferred_element_type=F32)
                cr2 = cr_ref[hh] * LOG2E
                for r in range(t // rc):
                    rows = slice(r * rc, (r + 1) * rc)
                    s2 = _attn_logits2(s_scr[rows, :], cr2, diag, r * rc, c1)
                    m_prev = m_s[hh, rows]
                    m_new = jnp.maximum(m_prev, jnp.max(s2, axis=-1, keepdims=True))
                    alpha = jnp.exp2(m_prev - m_new)
                    p = jnp.exp2(s2 - m_new)
                    l_s[hh, rows] = alpha * l_s[hh, rows] + jnp.sum(p, axis=-1, keepdims=True)
                    m_s[hh, rows] = m_new
                    a_scr[rows] = alpha
                    p_hi = p.astype(BF16)
                    p_scr[rows, :] = p_hi
                    p_scr[t + r * rc:t + (r + 1) * rc, :] = (p - p_hi.astype(F32)).astype(BF16)
                pv = jnp.dot(p_scr[...], v_ref[:, cs], preferred_element_type=F32)
                al = a_scr[...]
                acc_s[:, cs] = al * acc_s[:, cs] + pv[:t]
                lo_s[:, cs] = al * lo_s[:, cs] + pv[t:]

        @pl.when(j < i)
        def _():
            step(False)

        @pl.when(j == i)
        def _():
            step(True)
            lane = lax.broadcasted_iota(jnp.int32, (t, LANES), 1)
            lse = jnp.zeros((t, LANES), F32)
            for hh in range(hg):
                cs = slice(hh * HEAD_DIM, (hh + 1) * HEAD_DIM)
                o_ref[:, cs] = acc_s[:, cs] / l_s[hh]
                olo_ref[:, cs] = lo_s[:, cs] / l_s[hh]
                lse = jnp.where(lane == hh, m_s[hh] + jnp.log2(l_s[hh]), lse)
            lse_ref[...] = lse

    qspec = pl.BlockSpec((t, w), lambda g, p, qi, kj: (qi[p], g))
    kspec = pl.BlockSpec((t, w), lambda g, p, qi, kj: (kj[p], g))
    stat = pl.BlockSpec((t, LANES), lambda g, p, qi, kj: (qi[p], g))
    crow = pl.BlockSpec((hg, 1, t), lambda g, p, qi, kj: (g, 0, kj[p]))
    grid_spec = pltpu.PrefetchScalarGridSpec(
        num_scalar_prefetch=2, grid=(ng, int(qi_tab.shape[0])), in_specs=[qspec, kspec, kspec, crow],
        out_specs=[qspec, qspec, stat],
        scratch_shapes=[pltpu.VMEM((hg, t, 1), F32), pltpu.VMEM((hg, t, 1), F32), pltpu.VMEM((t, w), F32),
                        pltpu.VMEM((t, w), F32), pltpu.VMEM((t, t), F32), pltpu.VMEM((2 * t, t), BF16),
                        pltpu.VMEM((t, 1), F32)])
    return pl.pallas_call(
        body, name=name, grid_spec=grid_spec,
        out_shape=[_sds((s_len, e), F32), _sds((s_len, e), F32), _sds((s_len, ng * LANES), F32)],
        compiler_params=_params(12 * t * t * 4 + 24 * t * w * 4),
    )(qi_tab, kj_tab, qn, kn, vb, c_row)


def _fox_bwd(qn, kn, vb, do, lse, delta, c_row, *, name, hg, t=512):
    s_len, e = qn.shape
    t = _tile(s_len, t)
    w = hg * HEAD_DIM
    ng, n = e // w, s_len // t
    scale = HEAD_DIM ** -0.5
    c1 = scale * LOG2E
    qi_tab, kj_tab = _tri_tables(n, k_major=True)

    def body(qi_ref, kj_ref, q_ref, k_ref, v_ref, do_ref, lse_ref, dl_ref, cr_ref, dq_ref, dk_ref, dv_ref, dc_ref,
             dk_s, dv_s, dc_s):
        pid = pl.program_id(1)
        i, j = qi_ref[pid], kj_ref[pid]

        @pl.when(pid == 0)
        def _():
            dq_ref[...] = jnp.zeros_like(dq_ref)

        @pl.when(i == j)
        def _():
            dk_s[...] = jnp.zeros_like(dk_s)
            dv_s[...] = jnp.zeros_like(dv_s)
            dc_s[...] = jnp.zeros_like(dc_s)

        def step(diag):
            rows = pl.ds(pl.multiple_of(i * t, t), t)
            for hh in range(hg):
                cs = slice(hh * HEAD_DIM, (hh + 1) * HEAD_DIM)
                q, k, v, dov = q_ref[:, cs], k_ref[:, cs], v_ref[:, cs], do_ref[:, cs]
                s_raw = lax.dot_general(q, k, (((1,), (1,)), ((), ())), preferred_element_type=F32)
                s2 = _attn_logits2(s_raw, cr_ref[hh] * LOG2E, diag, 0, c1)
                p = jnp.exp2(s2 - lse_ref[:, hh:hh + 1])
                dv_s[:, cs] += lax.dot_general(p.astype(BF16), dov, (((0,), (0,)), ((), ())), preferred_element_type=F32)
                dp = lax.dot_general(dov, v, (((1,), (1,)), ((), ())), preferred_element_type=F32)
                ds = p * (dp - dl_ref[:, hh:hh + 1])
                dc_s[hh] -= jnp.sum(ds, axis=0, keepdims=True)
                dsb = (ds * scale).astype(BF16)
                dk_s[:, cs] += lax.dot_general(dsb, q, (((0,), (0,)), ((), ())), preferred_element_type=F32)
                dq_ref[rows, cs] += jnp.dot(dsb, k, preferred_element_type=F32)

        @pl.when(i > j)
        def _():
            step(False)

        @pl.when(i == j)
        def _():
            step(True)

        @pl.when(i == n - 1)
        def _():
            dk_ref[...] = dk_s[...]
            dv_ref[...] = dv_s[...]
            dc_ref[...] = dc_s[...]

    qspec = pl.BlockSpec((t, w), lambda g, p, qi, kj: (qi[p], g))
    kspec = pl.BlockSpec((t, w), lambda g, p, qi, kj: (kj[p], g))
    stat = pl.BlockSpec((t, LANES), lambda g, p, qi, kj: (qi[p], g))
    crow = pl.BlockSpec((hg, 1, t), lambda g, p, qi, kj: (g, 0, kj[p]))
    grid_spec = pltpu.PrefetchScalarGridSpec(
        num_scalar_prefetch=2, grid=(ng, int(qi_tab.shape[0])),
        in_specs=[qspec, kspec, kspec, qspec, stat, stat, crow],
        out_specs=[pl.BlockSpec((s_len, w), lambda g, p, qi, kj: (0, g)), kspec, kspec, crow],
        scratch_shapes=[pltpu.VMEM((t, w), F32), pltpu.VMEM((t, w), F32), pltpu.VMEM((hg, 1, t), F32)])
    return pl.pallas_call(
        body, name=name, grid_spec=grid_spec,
        out_shape=[_sds((s_len, e), F32), _sds((s_len, e), F32), _sds((s_len, e), F32), _sds((e // HEAD_DIM, 1, s_len), F32)],
        compiler_params=_params(2 * s_len * w * 4 + 16 * t * t * 4 + 24 * t * w * 4),
    )(qi_tab, kj_tab, qn, kn, vb, do, lse, delta, c_row)


def _adamw(w, gs, m, v, *, name, tr=256):
    r, c = w.shape
    tr = _tile(r, tr)
    n_g = len(gs)

    def body(*refs):
        w_ref, g_refs = refs[0], refs[1:1 + n_g]
        m_ref, v_ref, go_ref, d_ref, nm_ref, nv_ref = refs[1 + n_g:]
        gv = g_refs[0][...].astype(F32)
        for g_ref in g_refs[1:]:
            gv = gv + g_ref[...].astype(F32)
        go_ref[...] = gv
        m2 = ADAM_B1 * m_ref[...] + (1.0 - ADAM_B1) * gv
        v2 = ADAM_B2 * v_ref[...] + (1.0 - ADAM_B2) * (gv * gv)
        m_hat = m2 / (1.0 - ADAM_B1 ** ADAM_STEP)
        v_hat = v2 / (1.0 - ADAM_B2 ** ADAM_STEP)
        d_ref[...] = -ADAM_LR * (m_hat / (jnp.sqrt(v_hat) + ADAM_EPS) + ADAM_WD * w_ref[...])
        nm_ref[...] = m2
        nv_ref[...] = v2

    blk = pl.BlockSpec((tr, c), lambda i: (i, 0))
    return pl.pallas_call(
        body, name=name, grid=(r // tr,), in_specs=[blk] * (3 + n_g), out_specs=[blk] * 4,
        out_shape=[_sds((r, c), F32)] * 4, compiler_params=_params(24 * tr * c * 4),
    )(w, *gs, m, v)


def _adamw_halves(w, mine, other, m, v, c_arr, *, name, tr=256):
    r, c = w.shape
    half = r // 2
    tr = _tile(half, tr)
    nh = half // tr

    def body(c_ref, w_ref, mine_ref, other_ref, m_ref, v_ref, go_ref, d_ref, nm_ref, nv_ref):
        is_mine = pl.program_id(0) == c_ref[0]
        gv = jnp.where(is_mine, mine_ref[...], other_ref[...])
        go_ref[...] = gv
        m2 = ADAM_B1 * m_ref[...] + (1.0 - ADAM_B1) * gv
        v2 = ADAM_B2 * v_ref[...] + (1.0 - ADAM_B2) * (gv * gv)
        m_hat = m2 / (1.0 - ADAM_B1 ** ADAM_STEP)
        v_hat = v2 / (1.0 - ADAM_B2 ** ADAM_STEP)
        d_ref[...] = -ADAM_LR * (m_hat / (jnp.sqrt(v_hat) + ADAM_EPS) + ADAM_WD * w_ref[...])
        nm_ref[...] = m2
        nv_ref[...] = v2

    full = pl.BlockSpec((tr, c), lambda h, j, cref: (h * nh + j, 0))
    part = pl.BlockSpec((tr, c), lambda h, j, cref: (j, 0))
    grid_spec = pltpu.PrefetchScalarGridSpec(num_scalar_prefetch=1, grid=(2, nh), in_specs=[full, part, part, full, full],
                                             out_specs=[full] * 4)
    return pl.pallas_call(
        body, name=name, grid_spec=grid_spec, out_shape=[_sds((r, c), F32)] * 4,
        compiler_params=_params(28 * tr * c * 4),
    )(c_arr, w, mine, other, m, v)


def _pair_add(g, got, c_arr, *, name, tr=256):
    n, r, c = g.shape
    half = r // 2
    tr = _tile(half, tr)
    nh = half // tr

    def body(c_ref, g_ref, got_ref, o_ref):
        o_ref[...] = (g_ref[...].astype(F32) + got_ref[...].astype(F32)).astype(o_ref.dtype)

    grid_spec = pltpu.PrefetchScalarGridSpec(
        num_scalar_prefetch=1, grid=(n, nh),
        in_specs=[pl.BlockSpec((None, tr, c), lambda s, i, cref: (s, cref[0] * nh + i, 0)),
                  pl.BlockSpec((None, tr, c), lambda s, i, cref: (s, i, 0))],
        out_specs=pl.BlockSpec((None, tr, c), lambda s, i, cref: (s, i, 0)))
    return pl.pallas_call(
        body, name=name, grid_spec=grid_spec, out_shape=_sds((n, half, c), BF16), compiler_params=_params(16 * tr * c * 4),
    )(c_arr, g, got)


def _sum_own_recv(pair, recv, chip_arr, *, name, tr=256):
    _, r, c = pair.shape
    tr = _tile(r, tr)
    n_recv = recv.shape[0]

    def body(chip_ref, own_ref, recv_ref, o_ref):
        acc = own_ref[...].astype(F32)
        for k in range(n_recv):
            acc = acc + recv_ref[k].astype(F32)
        o_ref[...] = acc

    grid_spec = pltpu.PrefetchScalarGridSpec(
        num_scalar_prefetch=1, grid=(r // tr,),
        in_specs=[pl.BlockSpec((None, tr, c), lambda i, chip: (chip[0], i, 0)),
                  pl.BlockSpec((n_recv, tr, c), lambda i, chip: (0, i, 0))],
        out_specs=pl.BlockSpec((tr, c), lambda i, chip: (i, 0)))
    return pl.pallas_call(
        body, name=name, grid_spec=grid_spec, out_shape=_sds((r, c), F32), compiler_params=_params(16 * tr * c * 4),
    )(chip_arr, pair, recv)


_ANY = pl.BlockSpec(memory_space=pl.ANY)
DMA_CHUNK_BYTES = 512 << 10


def _chunks(parts):
    out = []
    for src_at, dst_at, rows, row_bytes in parts:
        step = max(16, DMA_CHUNK_BYTES // row_bytes // 16 * 16)
        for r0 in range(0, rows, step):
            n = min(step, rows - r0)
            out.append((src_at(r0, n), dst_at(r0, n)))
    return out


def _row_bytes(ref):
    return ref.shape[-1] * ref.dtype.itemsize


def _me():
    return lax.axis_index("x"), lax.axis_index("y"), lax.axis_index("c")


def _chip_peers(x, y):
    return [(2 * (1 - x) + y, (1 - x, y)), (2 * x + (1 - y), (x, 1 - y)), (2 * (1 - x) + (1 - y), (1 - x, 1 - y))]


def _exchange(name, ins, out_shapes, plan):
    n_in, n_out = len(ins), len(out_shapes)

    def body(*refs):
        in_refs, out_refs = refs[:n_in], refs[n_in:n_in + n_out]
        send_sems, recv_sems, loc_sems = refs[n_in + n_out:]
        remote, local = plan(in_refs, out_refs)
        starts, waits = [], []
        for k, (ws, wd, dev, parts) in enumerate(remote):
            def mk(s, d, k=k, dev=dev):
                return pltpu.make_async_remote_copy(src_ref=s, dst_ref=d, send_sem=send_sems.at[k],
                                                    recv_sem=recv_sems.at[k], device_id=dev, device_id_type=MESH_ID)
            starts += [mk(s, d) for s, d in _chunks(parts)]
            waits.append(mk(ws, wd))
        for k, (ws, wd, _, parts) in enumerate(local):
            def mk(s, d, k=k):
                return pltpu.make_async_copy(s, d, loc_sems.at[k])
            starts += [mk(s, d) for s, d in _chunks(parts)]
            waits.append(mk(ws, wd))
        for cp in starts:
            cp.start()
        for cp in waits:
            cp.wait()

    n_remote, n_local = plan.n_remote, plan.n_local
    return pl.pallas_call(
        body, name=name, in_specs=[_ANY] * n_in, out_specs=[_ANY] * n_out, out_shape=list(out_shapes),
        scratch_shapes=[pltpu.SemaphoreType.DMA((n_remote,)), pltpu.SemaphoreType.DMA((n_remote,)),
                        pltpu.SemaphoreType.DMA((max(n_local, 1),))],
    )(*ins)


def _gather_weights(shards, *, name):
    n_t = len(shards)

    def body(*refs):
        in_refs, out_refs = refs[:n_t], refs[n_t:2 * n_t]
        send_sems, recv_sems = refs[2 * n_t:]
        x, y, c = _me()
        chip = 2 * x + y
        peers = _chip_peers(x, y)
        local, first, passed = [], [], []
        for t in range(n_t):
            src, dst = in_refs[t], out_refs[t]
            n_rows = src.shape[0]
            half = n_rows // 2
            rb = _row_bytes(src)
            rows = pl.ds(c * half, half)

            def mk_own(s, d, t=t):
                return pltpu.make_async_remote_copy(
                    src_ref=s, dst_ref=d, send_sem=send_sems.at[7 * t + 6], recv_sem=recv_sems.at[7 * t + 6],
                    device_id=(x, y, 1 - c), device_id_type=MESH_ID)

            own = [(lambda r0, n, src=src: src.at[pl.ds(r0, n)],
                    lambda r0, n, dst=dst: dst.at[chip, pl.ds(r0, n)], n_rows, rb)]
            local.append((mk_own(src, dst.at[chip]), [mk_own(s, d) for s, d in _chunks(own)]))
            for k, (pchip, (px, py)) in enumerate(peers):
                def mk_ici(s, d, t=t, k=k, px=px, py=py):
                    return pltpu.make_async_remote_copy(
                        src_ref=s, dst_ref=d, send_sem=send_sems.at[7 * t + k], recv_sem=recv_sems.at[7 * t + k],
                        device_id=(px, py, c), device_id_type=MESH_ID)

                def mk_d2d(s, d, t=t, k=k):
                    return pltpu.make_async_remote_copy(
                        src_ref=s, dst_ref=d, send_sem=send_sems.at[7 * t + 3 + k], recv_sem=recv_sems.at[7 * t + 3 + k],
                        device_id=(x, y, 1 - c), device_id_type=MESH_ID)

                out_part = [(lambda r0, n, src=src: src.at[pl.ds(c * half + r0, n)],
                             lambda r0, n, dst=dst: dst.at[chip, pl.ds(c * half + r0, n)], half, rb)]
                fwd_part = [(lambda r0, n, dst=dst, pchip=pchip: dst.at[pchip, pl.ds(c * half + r0, n)],
                             lambda r0, n, dst=dst, pchip=pchip: dst.at[pchip, pl.ds(c * half + r0, n)], half, rb)]
                first.append((mk_ici(src.at[rows], dst.at[chip, rows]), [mk_ici(s, d) for s, d in _chunks(out_part)]))
                passed.append((mk_d2d(dst.at[pchip, rows], dst.at[pchip, rows]),
                               [mk_d2d(s, d) for s, d in _chunks(fwd_part)]))
        for _, chunk_copies in first + local:
            for cp in chunk_copies:
                cp.start()
        for (whole, _), (_, fwd_copies) in zip(first, passed):
            whole.wait_recv()
            for cp in fwd_copies:
                cp.start()
        for whole, _ in passed:
            whole.wait_recv()
        for whole, _ in first + passed:
            whole.wait_send()
        for whole, _ in local:
            whole.wait()

    outs = [_sds((N_CHIPS,) + s.shape, s.dtype) for s in shards]
    return pl.pallas_call(
        body, name=name, in_specs=[_ANY] * n_t, out_specs=[_ANY] * n_t, out_shape=outs,
        scratch_shapes=[pltpu.SemaphoreType.DMA((7 * n_t,)), pltpu.SemaphoreType.DMA((7 * n_t,))],
    )(*shards)


class _Plan:
    def __init__(self, fn, n_remote, n_local):
        self.fn, self.n_remote, self.n_local = fn, n_remote, n_local

    def __call__(self, in_refs, out_refs):
        return self.fn(in_refs, out_refs)


def _reduce_grads(grads, *, name):
    n_t = len(grads)
    c_arr = lax.axis_index("c").astype(jnp.int32).reshape(1)
    chip_arr = (2 * lax.axis_index("x") + lax.axis_index("y")).astype(jnp.int32).reshape(1)

    def plan1(in_refs, out_refs):
        x, y, c = _me()
        remote = []
        for t in range(n_t):
            src, got = in_refs[t], out_refs[t]
            half = src.shape[1] // 2
            send = [(lambda r0, n, s=s, src=src, half=half: src.at[s, pl.ds((1 - c) * half + r0, n)],
                     lambda r0, n, s=s, got=got: got.at[s, pl.ds(r0, n)], half, _row_bytes(src)) for s in range(N_CHIPS)]
            remote.append((src.at[:, pl.ds((1 - c) * half, half)], got, (x, y, 1 - c), send))
        return remote, []

    halves = [_sds((N_CHIPS, g.shape[1] // 2, g.shape[2]), g.dtype) for g in grads]
    got = _exchange(name + "_sib", grads, halves, _Plan(plan1, n_t, 0))
    pair = [_pair_add(grads[t], got[t], c_arr, name=f"{name}_pair{t}") for t in range(n_t)]

    def plan2(in_refs, out_refs):
        x, y, c = _me()
        remote = []
        for t in range(n_t):
            src, dst = in_refs[t], out_refs[t]
            rows, rb = src.shape[1], _row_bytes(src)
            for k, (pchip, (px, py)) in enumerate(_chip_peers(x, y)):
                part = [(lambda r0, n, src=src, pchip=pchip: src.at[pchip, pl.ds(r0, n)],
                         lambda r0, n, dst=dst, k=k: dst.at[k, pl.ds(r0, n)], rows, rb)]
                remote.append((src.at[pchip], dst.at[k], (px, py, c), part))
        return remote, []

    recv_shapes = [_sds((N_CHIPS - 1,) + h.shape[1:], h.dtype) for h in halves]
    recv = _exchange(name + "_ici", pair, recv_shapes, _Plan(plan2, 3 * n_t, 0))
    mine = [_sum_own_recv(pair[t], recv[t], chip_arr, name=f"{name}_sum{t}") for t in range(n_t)]

    def plan3(in_refs, out_refs):
        x, y, c = _me()
        remote = []
        for t in range(n_t):
            src, dst = in_refs[t], out_refs[t]
            rows = [(lambda r0, n, src=src: src.at[pl.ds(r0, n)], lambda r0, n, dst=dst: dst.at[pl.ds(r0, n)],
                     src.shape[0], _row_bytes(src))]
            remote.append((src, dst, (x, y, 1 - c), rows))
        return remote, []

    other = _exchange(name + "_swap", mine, [_sds(s.shape, s.dtype) for s in mine], _Plan(plan3, n_t, 0))
    return list(zip(mine, other)), c_arr


def _allreduce_small(pack, *, name):
    r, w = pack.shape

    def body(p_ref, o_ref, buf, send_sems, recv_sems):
        x, y, c = _me()
        me = 4 * x + 2 * y + c
        buf[me] = p_ref[...]
        copies = []
        for k in range(1, N_DEV):
            peer = (x ^ ((k >> 2) & 1), y ^ ((k >> 1) & 1), c ^ (k & 1))
            copies.append(pltpu.make_async_remote_copy(
                src_ref=p_ref, dst_ref=buf.at[me], send_sem=send_sems.at[k - 1], recv_sem=recv_sems.at[k - 1],
                device_id=peer, device_id_type=MESH_ID))
        for cp in copies:
            cp.start()
        for cp in copies:
            cp.wait()
        acc = buf[0]
        for k in range(1, N_DEV):
            acc = acc + buf[k]
        o_ref[...] = acc

    vm = pl.BlockSpec(memory_space=pltpu.VMEM)
    return pl.pallas_call(
        body, name=name, in_specs=[vm], out_specs=vm, out_shape=_sds((r, w), F32),
        scratch_shapes=[pltpu.VMEM((N_DEV, r, w), F32), pltpu.SemaphoreType.DMA((N_DEV - 1,)),
                        pltpu.SemaphoreType.DMA((N_DEV - 1,))],
        compiler_params=_params(12 * r * w * 4),
    )(pack)


def _pack(arrs, row_multiple=16):
    flat = jnp.concatenate([a.reshape(-1).astype(F32) for a in arrs])
    unit = row_multiple * LANES
    total = -(-flat.shape[0] // unit) * unit
    return jnp.pad(flat, (0, total - flat.shape[0])).reshape(total // LANES, LANES)


def _unpack(packed, shapes):
    flat = packed.reshape(-1)
    out, off = [], 0
    for shp in shapes:
        n = int(np.prod(shp))
        out.append(flat[off:off + n].reshape(shp))
        off += n
    return out


def _pad_cols(a, width):
    return jnp.pad(a, [(0, 0)] * (a.ndim - 1) + [(0, width - a.shape[-1])])


ATTN_HEADS_PER_STEP = 2
SMALL_SHARDED = ("a_norm", "a_conv_w", "a_conv_b", "a_ln_g", "a_ln_b", "c_norm", "c_conv_w")
SMALL_REPLICATED = ("b_norm", "b_f_bias", "b_q_norm", "b_k_norm")
BIG = ("a_w_in", "a_w_out", "b_w_in", "b_w_out", "c_w_in", "c_w_out")
WEIGHTS = ("a_norm", "a_w_in", "a_conv_w", "a_conv_b", "a_ln_g", "a_ln_b", "a_w_out", "b_norm", "b_w_in", "b_f_bias",
           "b_q_norm", "b_k_norm", "b_w_out", "c_norm", "c_w_in", "c_conv_w", "c_w_out")


def _mixer_a_fwd(x, p, l, tag):
    h = _rms_fwd(x, p["a_norm"][l][None], name=f"{tag}_rms")
    proj = _mm_nn(h, p["a_w_in"][l], name=f"{tag}_in", tn=p["a_w_in"][l].shape[2])
    u1 = _a_conv_fwd(proj, p["a_conv_w"][l], p["a_conv_b"][l][None], name=f"{tag}_conv")
    gated = _a_post_fwd(u1, proj, p["a_ln_g"][l][None], p["a_ln_b"][l][None], name=f"{tag}_post")
    y = _mm_nn(gated, p["a_w_out"][l], name=f"{tag}_out", tn=1024, add=x)
    return y, (x, h, proj, u1, gated)


def _mixer_a_bwd(dx, saved, p, l, tag):
    x, h, proj, u1, gated = saved
    g = {}
    g["a_w_out"] = _mm_tn(gated, dx, name=f"{tag}_dwout", out_dtype=BF16, tk=2048, tn=1024, ts=1024)
    dgated = _mm_nt(dx, p["a_w_out"][l], name=f"{tag}_dgated", tn=2048, tk=1024)
    du1, dz, g["a_ln_g"], g["a_ln_b"], g["a_conv_b"] = _a_post_bwd(
        dgated, u1, proj, p["a_ln_g"][l][None], p["a_ln_b"][l][None], name=f"{tag}_dpost")
    dproj, g["a_conv_w"] = _a_conv_bwd(du1, proj, dz, p["a_conv_w"][l], name=f"{tag}_dconv")
    g["a_w_in"] = _mm_tn(h, dproj, name=f"{tag}_dwin", out_dtype=BF16, out_width=p["a_w_in"][l].shape[2], ts=2048)
    dh = _mm_nt(dproj, p["a_w_in"][l], name=f"{tag}_dh", ksub=4)
    dx, g["a_norm"] = _rms_bwd(x, p["a_norm"][l][None], [dh], dx, name=f"{tag}_drms")
    return dx, g


def _mixer_c_fwd(x, p, tag):
    h = _rms_fwd(x, p["c_norm"][0][None], name=f"{tag}_rms")
    proj = _mm_nn(h, p["c_w_in"], name=f"{tag}_in", tn=1024)
    gated = _c_mid_fwd(proj, p["c_conv_w"][0], name=f"{tag}_mid")
    y = _mm_nn(gated, p["c_w_out"], name=f"{tag}_out", tn=1024, add=x)
    return y, (x, h, proj, gated)


def _mixer_c_bwd(dx, saved, p, tag):
    x, h, proj, gated = saved
    g = {}
    g["c_w_out"] = _mm_tn(gated, dx, name=f"{tag}_dwout", out_dtype=BF16, tk=2048, tn=1024, ts=1024)
    dgated = _mm_nt(dx, p["c_w_out"], name=f"{tag}_dgated", tn=2048, tk=1024)
    dproj, g["c_conv_w"] = _c_mid_bwd(dgated, proj, p["c_conv_w"][0], name=f"{tag}_dmid")
    g["c_w_in"] = _mm_tn(h, dproj, name=f"{tag}_dwin", out_dtype=BF16, out_width=p["c_w_in"].shape[2], tn=1024, ts=2048)
    dh = _mm_nt(dproj, p["c_w_in"], name=f"{tag}_dh", tk=1024, ksub=2)
    dx, g["c_norm"] = _rms_bwd(x, p["c_norm"][0][None], [dh], dx, name=f"{tag}_drms")
    return dx, g


def _mixer_b_fwd(x, p, tag):
    hg = ATTN_HEADS_PER_STEP
    s_len = x.shape[0]
    n_heads = p["b_f_bias"].shape[1]
    h = _rms_fwd(x, p["b_norm"], name=f"{tag}_rms")
    proj = _mm_nn(h, p["b_wq"], name=f"{tag}_in", tn=1024)
    fl = _mm_nn(h, p["b_wf"], name=f"{tag}_inf", tn=LANES)
    qn, kn, vb = _b_qk_fwd(proj, p["b_q_norm"], p["b_k_norm"], name=f"{tag}_qk")
    bias = _pad_cols(p["b_f_bias"], LANES)
    c = _b_cumsum(fl, bias, name=f"{tag}_cumsum")
    ch = c[:, :n_heads]
    c_row = ch.T.reshape(n_heads, 1, s_len)
    o, o_lo, lse = _fox_fwd(qn, kn, vb, c_row, name=f"{tag}_attn", hg=hg)
    gated = _b_gate_fwd(o, proj, name=f"{tag}_gate")
    y = _mm_nn(gated, p["b_w_out"], name=f"{tag}_out", tn=1024, add=x)
    return y, (x, h, proj, fl, bias, qn, kn, vb, c_row, o, o_lo, lse, gated)


def _mixer_b_bwd(dx, saved, p, tag):
    hg = ATTN_HEADS_PER_STEP
    x, h, proj, fl, bias, qn, kn, vb, c_row, o, o_lo, lse, gated = saved
    s_len = x.shape[0]
    n_heads = p["b_f_bias"].shape[1]
    g = {}
    g["b_w_out"] = _mm_tn(gated, dx, name=f"{tag}_dwout", out_dtype=BF16, tk=2048, tn=1024, ts=1024)
    dgated = _mm_nt(dx, p["b_w_out"], name=f"{tag}_dgated", tn=2048, tk=1024)
    do, dz, delta = _b_gate_bwd(dgated, o, o_lo, proj, name=f"{tag}_dgate", hg=hg)
    dqn, dkn, dv, dc = _fox_bwd(qn, kn, vb, do, lse, delta, c_row, name=f"{tag}_dattn", hg=hg)
    dc_pad = _pad_cols(dc.reshape(n_heads, s_len).T, LANES)
    dfl, dbias = _b_cumsum_bwd(dc_pad, fl, bias, name=f"{tag}_dcumsum")
    g["b_f_bias"] = dbias[:, :n_heads]
    dproj, g["b_q_norm"], g["b_k_norm"] = _b_qk_bwd(dqn, dkn, dv, dz, proj, p["b_q_norm"], p["b_k_norm"], name=f"{tag}_dqk")
    dwq = _mm_tn(h, dproj, name=f"{tag}_dwin", out_dtype=BF16, tn=1024, ts=2048)
    dwf = _mm_tn(h, dfl, name=f"{tag}_dwinf", out_dtype=BF16, tn=LANES)
    g["b_w_in"] = jnp.concatenate([dwq, dwf[:, :n_heads]], axis=1)
    dh = _mm_nt(dproj, p["b_wq"], name=f"{tag}_dh", tk=1024, ksub=2)
    dhf = _mm_nt(dfl, p["b_wf"], name=f"{tag}_dhf", tk=LANES)
    dx, g["b_norm"] = _rms_bwd(x, p["b_norm"], [dh, dhf], dx, name=f"{tag}_drms")
    return dx, g


def kernel(x, a_norm, a_w_in, a_conv_w, a_conv_b, a_ln_g, a_ln_b, a_w_out, b_norm, b_w_in, b_f_bias, b_q_norm, b_k_norm, b_w_out, c_norm, c_w_in, c_conv_w, c_w_out, loss_target, m_a_norm, m_a_w_in, m_a_conv_w, m_a_conv_b, m_a_ln_g, m_a_ln_b, m_a_w_out, m_b_norm, m_b_w_in, m_b_f_bias, m_b_q_norm, m_b_k_norm, m_b_w_out, m_c_norm, m_c_w_in, m_c_conv_w, m_c_w_out, v_a_norm, v_a_w_in, v_a_conv_w, v_a_conv_b, v_a_ln_g, v_a_ln_b, v_a_w_out, v_b_norm, v_b_w_in, v_b_f_bias, v_b_q_norm, v_b_k_norm, v_b_w_out, v_c_norm, v_c_w_in, v_c_conv_w, v_c_w_out):
    w_loc = dict(a_norm=a_norm, a_w_in=a_w_in, a_conv_w=a_conv_w, a_conv_b=a_conv_b, a_ln_g=a_ln_g, a_ln_b=a_ln_b,
                 a_w_out=a_w_out, b_norm=b_norm, b_w_in=b_w_in, b_f_bias=b_f_bias, b_q_norm=b_q_norm, b_k_norm=b_k_norm,
                 b_w_out=b_w_out, c_norm=c_norm, c_w_in=c_w_in, c_conv_w=c_conv_w, c_w_out=c_w_out)
    m_loc = dict(a_norm=m_a_norm, a_w_in=m_a_w_in, a_conv_w=m_a_conv_w, a_conv_b=m_a_conv_b, a_ln_g=m_a_ln_g,
                 a_ln_b=m_a_ln_b, a_w_out=m_a_w_out, b_norm=m_b_norm, b_w_in=m_b_w_in, b_f_bias=m_b_f_bias,
                 b_q_norm=m_b_q_norm, b_k_norm=m_b_k_norm, b_w_out=m_b_w_out, c_norm=m_c_norm, c_w_in=m_c_w_in,
                 c_conv_w=m_c_conv_w, c_w_out=m_c_w_out)
    v_loc = dict(a_norm=v_a_norm, a_w_in=v_a_w_in, a_conv_w=v_a_conv_w, a_conv_b=v_a_conv_b, a_ln_g=v_a_ln_g,
                 a_ln_b=v_a_ln_b, a_w_out=v_a_w_out, b_norm=v_b_norm, b_w_in=v_b_w_in, b_f_bias=v_b_f_bias,
                 b_q_norm=v_b_q_norm, b_k_norm=v_b_k_norm, b_w_out=v_b_w_out, c_norm=v_c_norm, c_w_in=v_c_w_in,
                 c_conv_w=v_c_conv_w, c_w_out=v_c_w_out)
    n_a = a_w_in.shape[0]
    d_model = x.shape[2]
    e_inner = a_w_out.shape[1] * N_CHIPS
    n_heads = b_f_bias.shape[1]
    nb_loc = b_w_in.shape[2]
    nb_pad = -(-nb_loc // LANES) * LANES
    chip = 2 * lax.axis_index("x") + lax.axis_index("y")

    big_shards = ([a_w_in[l].astype(BF16) for l in range(n_a)] + [a_w_out[l].astype(BF16) for l in range(n_a)]
                  + [_pad_cols(b_w_in[0], nb_pad).astype(BF16), b_w_out[0].astype(BF16), c_w_in[0].astype(BF16),
                     c_w_out[0].astype(BF16)])
    small_pack = _pack([w_loc[n] for n in SMALL_SHARDED])
    gathered = _gather_weights(big_shards + [small_pack], name="gather_weights")
    p = {}
    p["a_w_in"] = gathered[0:n_a]
    p["a_w_out"] = [g.reshape(e_inner, d_model) for g in gathered[n_a:2 * n_a]]
    gb, gbo, gci, gco, gsmall = gathered[2 * n_a:]
    wb_full = jnp.concatenate([gb[k, :, :nb_loc] for k in range(N_CHIPS)], axis=1)
    p["b_wq"] = wb_full[:, :4 * e_inner]
    p["b_wf"] = _pad_cols(wb_full[:, 4 * e_inner:], LANES)
    p["b_w_out"] = gbo.reshape(e_inner, d_model)
    p["c_w_in"] = gci
    p["c_w_out"] = gco.reshape(e_inner, d_model)
    small_shapes = [w_loc[n].shape for n in SMALL_SHARDED]
    per_chip = [_unpack(gsmall[k], small_shapes) for k in range(N_CHIPS)]
    for idx, n in enumerate(SMALL_SHARDED):
        p[n] = jnp.concatenate([per_chip[k][idx] for k in range(N_CHIPS)], axis=-1)
    for n in SMALL_REPLICATED:
        p[n] = w_loc[n]

    x0 = x[0]
    x1, sv0 = _mixer_a_fwd(x0, p, 0, "a0")
    x2, sv1 = _mixer_b_fwd(x1, p, "b0")
    x3, sv2 = _mixer_c_fwd(x2, p, "c0")
    x4, sv3 = _mixer_a_fwd(x3, p, 1, "a1")
    dy, loss_part = _loss_head(x4, loss_target[0], name="loss_head")
    loss = lax.psum(loss_part[0, 0], ("x", "y", "c"))

    dx, g3 = _mixer_a_bwd(dy, sv3, p, 1, "a1")
    dx, g2 = _mixer_c_bwd(dx, sv2, p, "c0")
    dx, g1 = _mixer_b_bwd(dx, sv1, p, "b0")
    dx, g0 = _mixer_a_bwd(dx, sv0, p, 0, "a0")
    grad_x = dx[None]

    half_rows = e_inner // N_CHIPS
    gb_full = g1["b_w_in"].reshape(d_model, N_CHIPS, nb_loc).transpose(1, 0, 2)
    big_grads = ([g0["a_w_in"], g3["a_w_in"]]
                 + [g0["a_w_out"].reshape(N_CHIPS, half_rows, d_model), g3["a_w_out"].reshape(N_CHIPS, half_rows, d_model)]
                 + [_pad_cols(gb_full, nb_pad), g1["b_w_out"].reshape(N_CHIPS, half_rows, d_model), g2["c_w_in"],
                    g2["c_w_out"].reshape(N_CHIPS, half_rows, d_model)])
    red, c_arr = _reduce_grads(big_grads, name="reduce_grads")
    big_order = [("a_w_in", 0), ("a_w_in", 1), ("a_w_out", 0), ("a_w_out", 1), ("b_w_in", 0), ("b_w_out", 0),
                 ("c_w_in", 0), ("c_w_out", 0)]

    grads, delta, new_m, new_v = {}, {}, {}, {}
    per_layer = {n: [] for n in BIG}
    for (n, l), (mine, other) in zip(big_order, red):
        cols = mine.shape[1]
        true_cols = w_loc[n].shape[-1]
        outs = _adamw_halves(_pad_cols(w_loc[n][l], cols), mine, other, _pad_cols(m_loc[n][l], cols),
                             _pad_cols(v_loc[n][l], cols), c_arr, name=f"adamw_{n}{l}")
        per_layer[n].append([a[:, :true_cols] for a in outs])
    for n in BIG:
        grads[n], delta[n], new_m[n], new_v[n] = [jnp.stack([layer[k] for layer in per_layer[n]]) for k in range(4)]

    small_full = {}
    for n in ("a_norm", "a_conv_w", "a_conv_b", "a_ln_g", "a_ln_b"):
        small_full[n] = jnp.stack([g0[n].reshape(p[n].shape[1:]), g3[n].reshape(p[n].shape[1:])])
    small_full["c_norm"] = g2["c_norm"].reshape(p["c_norm"].shape)
    small_full["c_conv_w"] = g2["c_conv_w"].reshape(p["c_conv_w"].shape)
    for n in SMALL_REPLICATED:
        small_full[n] = g1[n].reshape(w_loc[n].shape)
    small_names = SMALL_SHARDED + SMALL_REPLICATED
    summed = _unpack(_allreduce_small(_pack([small_full[n] for n in small_names], 8), name="reduce_small"),
                     [small_full[n].shape for n in small_names])
    for n, s in zip(small_names, summed):
        if n in SMALL_SHARDED:
            width = w_loc[n].shape[-1]
            grads[n] = lax.dynamic_slice_in_dim(s, chip * width, width, axis=s.ndim - 1)
        else:
            grads[n] = s

    small_shapes_all = [w_loc[n].shape for n in small_names]
    _, d_, m_, v_ = _adamw(_pack([w_loc[n] for n in small_names], 8), [_pack([grads[n] for n in small_names], 8)],
                           _pack([m_loc[n] for n in small_names], 8), _pack([v_loc[n] for n in small_names], 8),
                           name="adamw_small")
    for n, a, b, c_ in zip(small_names, _unpack(d_, small_shapes_all), _unpack(m_, small_shapes_all),
                           _unpack(v_, small_shapes_all)):
        delta[n], new_m[n], new_v[n] = a, b, c_

    return (loss, grad_x, *[grads[n] for n in WEIGHTS], *[delta[n] for n in WEIGHTS],
            *[new_m[n] for n in WEIGHTS], *[new_v[n] for n in WEIGHTS])
```

```python
import functools

import numpy as np
import jax
import jax.numpy as jnp
from jax import lax
from jax.experimental import pallas as pl
from jax.experimental.pallas import tpu as pltpu

F32 = jnp.float32
BF16 = jnp.bfloat16
NORM_EPS = 1e-6
HEAD_DIM = 128
LANES = 128
N_CHIPS = 4
N_DEV = 8
VMEM_CAP = 56 << 20
MESH_ID = pl.DeviceIdType.MESH

ADAM_LR = 0.001
ADAM_B1 = 0.9
ADAM_B2 = 0.999
ADAM_EPS = 1e-08
ADAM_WD = 0.01
ADAM_STEP = 10


def _sds(shape, dtype):
    return jax.ShapeDtypeStruct(tuple(shape), dtype)


def _tile(n, pref):
    if n <= pref:
        return n
    for t in range(pref - pref % 8, 0, -8):
        if n % t == 0:
            return t
    raise ValueError(f"no tile for {n} under {pref}")


def _params(vmem_bytes):
    return pltpu.CompilerParams(vmem_limit_bytes=int(min(max(vmem_bytes, 16 << 20), VMEM_CAP)))


def _sigmoid(v):
    return 1.0 / (1.0 + jnp.exp(-v))


def _silu(v):
    return v * _sigmoid(v)


def _dsilu(v):
    s = _sigmoid(v)
    return s * (1.0 + v * (1.0 - s))


def _rowsum8(v):
    r, c = v.shape
    return jnp.sum(v.reshape(r // 8, 8, c), axis=0)


def _col_spec(grouped, rows_block, tile, width, row_of, col_of):
    if grouped:
        per = width // tile
        return pl.BlockSpec((None, rows_block, tile), lambda *ids: (col_of(*ids) // per, row_of(*ids), col_of(*ids) % per))
    return pl.BlockSpec((rows_block, tile), lambda *ids: (row_of(*ids), col_of(*ids)))


def _mm_nn(a, b, *, name, tm=1024, tn=512, out_dtype=F32, add=None):
    m, k = a.shape
    b_grouped = b.ndim == 3
    n = b.shape[0] * b.shape[2] if b_grouped else b.shape[1]
    width = b.shape[2] if b_grouped else n
    tm, tn = _tile(m, tm), _tile(width, tn)

    def body(*refs):
        if add is None:
            a_ref, b_ref, o_ref = refs
        else:
            a_ref, b_ref, r_ref, o_ref = refs
        acc = jnp.dot(a_ref[...].astype(BF16), b_ref[...].astype(BF16), preferred_element_type=F32)
        if add is not None:
            acc = acc + r_ref[...]
        o_ref[...] = acc.astype(o_ref.dtype)

    in_specs = [pl.BlockSpec((tm, k), lambda i, j: (i, 0)),
                _col_spec(b_grouped, k, tn, width, lambda i, j: 0, lambda i, j: j)]
    args = [a, b]
    if add is not None:
        in_specs.append(pl.BlockSpec((tm, tn), lambda i, j: (i, j)))
        args.append(add)
    vmem = 2 * (tm * k * a.dtype.itemsize + k * tn * 2 + tm * tn * 4 * (2 if add is not None else 1)) + tm * tn * 8 + tm * k * 2
    return pl.pallas_call(
        body, name=name, grid=(m // tm, n // tn), in_specs=in_specs,
        out_specs=pl.BlockSpec((tm, tn), lambda i, j: (i, j)),
        out_shape=_sds((m, n), out_dtype), compiler_params=_params(vmem + (4 << 20)),
    )(*args)


def _mm_nt(a, b, *, name, tm=1024, tn=1024, tk=512, ksub=1):
    a_grouped, b_grouped = a.ndim == 3, b.ndim == 3
    m = a.shape[1] if a_grouped else a.shape[0]
    n = a.shape[0] * a.shape[2] if a_grouped else a.shape[1]
    kk = b.shape[1] if b_grouped else b.shape[0]
    wa = a.shape[2] if a_grouped else n
    wb = b.shape[2] if b_grouped else n
    tm, tn = _tile(m, tm), _tile(kk, tn)
    tk = _tile(int(np.gcd(wa, wb)), tk)
    ksub = min(ksub, n // tk)
    assert (n // tk) % ksub == 0, (n, tk, ksub)
    steps = n // (tk * ksub)

    def body(*refs):
        a_refs, b_refs, o_ref = refs[:ksub], refs[ksub:2 * ksub], refs[2 * ksub]
        part = None
        for a_ref, b_ref in zip(a_refs, b_refs):
            d = lax.dot_general(a_ref[...].astype(BF16), b_ref[...].astype(BF16), (((1,), (1,)), ((), ())),
                                preferred_element_type=F32)
            part = d if part is None else part + d

        @pl.when(pl.program_id(2) == 0)
        def _():
            o_ref[...] = part

        @pl.when(pl.program_id(2) > 0)
        def _():
            o_ref[...] += part

    def sub(u):
        return lambda i, j, s: s * ksub + u

    in_specs = ([_col_spec(a_grouped, tm, tk, wa, lambda i, j, s: i, sub(u)) for u in range(ksub)]
                + [_col_spec(b_grouped, tn, tk, wb, lambda i, j, s: j, sub(u)) for u in range(ksub)])
    vmem = (2 * ksub * (tm * tk * a.dtype.itemsize + tn * tk * b.dtype.itemsize) + 2 * tm * tn * 4 + 2 * tm * tn * 4
            + (tm + tn) * tk * 2)
    return pl.pallas_call(
        body, name=name, grid=(m // tm, kk // tn, steps), in_specs=in_specs,
        out_specs=pl.BlockSpec((tm, tn), lambda i, j, s: (i, j)),
        out_shape=_sds((m, kk), F32), compiler_params=_params(vmem + (4 << 20)),
    )(*([a] * ksub), *([b] * ksub))


def _mm_tn(a, b, *, name, out_width=None, out_dtype=F32, tk=1024, tn=512, ts=512):
    s_len, k = a.shape
    b_grouped = b.ndim == 3
    n = b.shape[0] * b.shape[2] if b_grouped else b.shape[1]
    wb = b.shape[2] if b_grouped else n
    wo = out_width if out_width is not None else n
    tk, ts = _tile(k, tk), _tile(s_len, ts)
    tn = _tile(int(np.gcd(wb, wo)), tn)
    last = s_len // ts - 1
    direct = out_dtype == F32

    def body(a_ref, b_ref, o_ref, *scratch):
        acc = o_ref if direct else scratch[0]
        part = lax.dot_general(a_ref[...].astype(BF16), b_ref[...].astype(BF16), (((0,), (0,)), ((), ())),
                               preferred_element_type=F32)

        @pl.when(pl.program_id(2) == 0)
        def _():
            acc[...] = part

        @pl.when(pl.program_id(2) > 0)
        def _():
            acc[...] += part

        if not direct:
            @pl.when(pl.program_id(2) == last)
            def _():
                o_ref[...] = acc[...].astype(o_ref.dtype)

    in_specs = [pl.BlockSpec((ts, tk), lambda i, j, s: (s, i)),
                _col_spec(b_grouped, ts, tn, wb, lambda i, j, s: s, lambda i, j, s: j)]
    out_grouped = out_width is not None
    out_spec = _col_spec(out_grouped, tk, tn, wo, lambda i, j, s: i, lambda i, j, s: j)
    out_shape = _sds((n // wo, k, wo), out_dtype) if out_grouped else _sds((k, n), out_dtype)
    vmem = 2 * (ts * tk * a.dtype.itemsize + ts * tn * b.dtype.itemsize + tk * tn * 4) + 2 * tk * tn * 4 + ts * (tk + tn) * 4
    return pl.pallas_call(
        body, name=name, grid=(k // tk, n // tn, s_len // ts), in_specs=in_specs, out_specs=out_spec,
        out_shape=out_shape, scratch_shapes=[] if direct else [pltpu.VMEM((tk, tn), F32)],
        compiler_params=_params(vmem + (4 << 20)),
    )(a, b)


def _rms_fwd(x, g, *, name, ts=512):
    s_len, d = x.shape
    ts = _tile(s_len, ts)

    def body(x_ref, g_ref, h_ref):
        xf = x_ref[...]
        r = lax.rsqrt(jnp.mean(xf * xf, axis=-1, keepdims=True) + NORM_EPS)
        h_ref[...] = ((xf * r) * g_ref[...]).astype(h_ref.dtype)

    return pl.pallas_call(
        body, name=name, grid=(s_len // ts,),
        in_specs=[pl.BlockSpec((ts, d), lambda i: (i, 0)), pl.BlockSpec((1, d), lambda i: (0, 0))],
        out_specs=pl.BlockSpec((ts, d), lambda i: (i, 0)), out_shape=_sds((s_len, d), BF16),
        compiler_params=_params(8 * ts * d * 4),
    )(x, g)


def _rms_bwd(x, g, dhs, dres, *, name, ts=512):
    s_len, d = x.shape
    ts = _tile(s_len, ts)
    n_dh = len(dhs)
    last = s_len // ts - 1

    def body(*refs):
        x_ref, g_ref = refs[0], refs[1]
        dh_refs = refs[2:2 + n_dh]
        dres_ref, dx_ref, dg_ref, acc = refs[2 + n_dh:]
        i = pl.program_id(0)

        @pl.when(i == 0)
        def _():
            acc[...] = jnp.zeros_like(acc)

        xf = x_ref[...]
        dy = dh_refs[0][...]
        for r_ in dh_refs[1:]:
            dy = dy + r_[...]
        r = lax.rsqrt(jnp.mean(xf * xf, axis=-1, keepdims=True) + NORM_EPS)
        gd = dy * g_ref[...]
        dot = jnp.mean(xf * gd, axis=-1, keepdims=True)
        dx_ref[...] = dres_ref[...] + r * gd - xf * (r * r * r * dot)
        acc[...] += _rowsum8(dy * (xf * r))

        @pl.when(i == last)
        def _():
            dg_ref[...] = jnp.sum(acc[...], axis=0, keepdims=True)

    row = pl.BlockSpec((ts, d), lambda i: (i, 0))
    vec = pl.BlockSpec((1, d), lambda i: (0, 0))
    return pl.pallas_call(
        body, name=name, grid=(s_len // ts,),
        in_specs=[row, vec] + [row] * n_dh + [row],
        out_specs=[row, vec], out_shape=[_sds((s_len, d), F32), _sds((1, d), F32)],
        scratch_shapes=[pltpu.VMEM((8, d), F32)],
        compiler_params=_params((2 * (3 + n_dh) + 6) * ts * d * 4),
    )(x, g, *dhs, dres)


def _loss_head(y, target, *, name, ts=512):
    s_len, d = y.shape
    ts = _tile(s_len, ts)
    last = s_len // ts - 1

    def body(y_ref, t_ref, dy_ref, loss_ref, acc):
        i = pl.program_id(0)

        @pl.when(i == 0)
        def _():
            acc[...] = jnp.zeros_like(acc)

        err = y_ref[...] - t_ref[...]
        dy_ref[...] = err / d
        acc[...] += _rowsum8(err * err)

        @pl.when(i == last)
        def _():
            loss_ref[...] = (0.5 * jnp.sum(acc[...]) / d).reshape(1, 1)

    row = pl.BlockSpec((ts, d), lambda i: (i, 0))
    return pl.pallas_call(
        body, name=name, grid=(s_len // ts,), in_specs=[row, row],
        out_specs=[row, pl.BlockSpec((1, 1), lambda i: (0, 0))],
        out_shape=[_sds((s_len, d), F32), _sds((1, 1), F32)],
        scratch_shapes=[pltpu.VMEM((8, d), F32)],
        compiler_params=_params(10 * ts * d * 4),
    )(y, target)


CONV_ROWS = 64
CONV_COLS = 256


SUBLANES = 8


def _fill_shifts(buf, sh_ref):
    n = sh_ref.shape[1]
    for s in range(1, SUBLANES):
        sh_ref[s - 1, :, :] = buf[s:s + n, :]


def _shifted_rows(buf, sh_ref, start, rw, cs):
    s = start % SUBLANES
    if sh_ref is None or s == 0:
        return buf[start:start + rw, cs]
    return sh_ref[s - 1, start - s:start - s + rw, cs]


def _conv_taps(buf, sh_ref, w_ref, k_width, base, ts, tc, init, emit, reverse=False):
    cw = min(tc, CONV_COLS)
    rw = min(ts, CONV_ROWS)
    for cb in range(tc // cw):
        cs = slice(cb * cw, (cb + 1) * cw)
        for rb in range(ts // rw):
            acc = init(slice(rb * rw, (rb + 1) * rw), cs, (rw, cw))
            for k in range(k_width):
                sh = (k_width - 1 - k) if reverse else k
                acc = acc + w_ref[k:k + 1, cs] * _shifted_rows(buf, sh_ref, base + rb * rw + sh, rw, cs)
            emit(slice(rb * rw, (rb + 1) * rw), cs, acc)


def _conv_wgrad(buf, sh_ref, d_ref_val, acc_ref, k_width, base, ts, tc):
    cw = min(tc, CONV_COLS)
    rw = min(ts, CONV_ROWS)
    for cb in range(tc // cw):
        cs = slice(cb * cw, (cb + 1) * cw)
        for rb in range(ts // rw):
            dv = d_ref_val[rb * rw:(rb + 1) * rw, cs]
            for k in range(k_width):
                prod = dv * _shifted_rows(buf, sh_ref, base + rb * rw + k, rw, cs)
                acc_ref[8 * k:8 * k + 8, cs] += _rowsum8(prod)


A_HALO = 32


def _a_conv_fwd(proj, conv_w, conv_b, *, name, ts=256, tc=512):
    s_len, e3 = proj.shape
    e = e3 // 3
    k_width = conv_w.shape[0]
    ts, tc = _tile(s_len, ts), _tile(e, tc)
    nc = e // tc
    kp = 32

    def body(val, gate, valh, gateh, w_ref, b_ref, u1_ref, buf, sh):
        i = pl.program_id(1)
        u0h = valh[...] * _sigmoid(gateh[...])
        buf[0:A_HALO, :] = jnp.where(i > 0, u0h, 0.0)
        buf[A_HALO:A_HALO + ts, :] = val[...] * _sigmoid(gate[...])
        _fill_shifts(buf, sh)

        def init(rows, cs, shape):
            return jnp.broadcast_to(b_ref[:, cs], shape)

        def emit(rows, cs, acc):
            u1_ref[rows, cs] = acc

        _conv_taps(buf, sh, w_ref, k_width, A_HALO - (k_width - 1), ts, tc, init, emit)

    hb = ts // A_HALO
    in_specs = [
        pl.BlockSpec((ts, tc), lambda j, i: (i, j)),
        pl.BlockSpec((ts, tc), lambda j, i: (i, nc + j)),
        pl.BlockSpec((A_HALO, tc), lambda j, i: (jnp.maximum(i * hb - 1, 0), j)),
        pl.BlockSpec((A_HALO, tc), lambda j, i: (jnp.maximum(i * hb - 1, 0), nc + j)),
        pl.BlockSpec((kp, tc), lambda j, i: (0, j)),
        pl.BlockSpec((1, tc), lambda j, i: (0, j)),
    ]
    w_pad = jnp.zeros((kp, e), F32).at[:k_width].set(conv_w)
    return pl.pallas_call(
        body, name=name, grid=(nc, s_len // ts), in_specs=in_specs,
        out_specs=pl.BlockSpec((ts, tc), lambda j, i: (i, j)), out_shape=_sds((s_len, e), F32),
        scratch_shapes=[pltpu.VMEM((A_HALO + ts, tc), F32), pltpu.VMEM((SUBLANES - 1, A_HALO + ts - SUBLANES, tc), F32)],
        compiler_params=_params(24 * ts * tc * 4),
    )(proj, proj, proj, proj, w_pad, conv_b)


def _ln_rows(u1, g, b):
    mu = jnp.mean(u1, axis=-1, keepdims=True)
    xc = u1 - mu
    var = jnp.mean(xc * xc, axis=-1, keepdims=True)
    rstd = lax.rsqrt(var + NORM_EPS)
    xhat = xc * rstd
    return xhat, rstd, xhat * g + b


def _a_post_fwd(u1, proj, ln_g, ln_b, *, name, ts=256):
    s_len, e = u1.shape
    ts = _tile(s_len, ts)

    def body(u1_ref, z_ref, g_ref, b_ref, o_ref):
        _, _, u2 = _ln_rows(u1_ref[...], g_ref[...], b_ref[...])
        o_ref[...] = (_silu(u2) * _silu(z_ref[...])).astype(o_ref.dtype)

    row = pl.BlockSpec((ts, e), lambda i: (i, 0))
    vec = pl.BlockSpec((1, e), lambda i: (0, 0))
    return pl.pallas_call(
        body, name=name, grid=(s_len // ts,),
        in_specs=[row, pl.BlockSpec((ts, e), lambda i: (i, 2)), vec, vec],
        out_specs=row, out_shape=_sds((s_len, e), BF16), compiler_params=_params(12 * ts * e * 4),
    )(u1, proj, ln_g, ln_b)


def _a_post_bwd(dgated, u1, proj, ln_g, ln_b, *, name, ts=256):
    s_len, e = u1.shape
    ts = _tile(s_len, ts)
    last = s_len // ts - 1

    def body(dg_ref, u1_ref, z_ref, g_ref, b_ref, du1_ref, dz_ref, dlg_ref, dlb_ref, dcb_ref, a_g, a_b, a_c):
        i = pl.program_id(0)

        @pl.when(i == 0)
        def _():
            a_g[...] = jnp.zeros_like(a_g)
            a_b[...] = jnp.zeros_like(a_b)
            a_c[...] = jnp.zeros_like(a_c)

        g = g_ref[...]
        xhat, rstd, u2 = _ln_rows(u1_ref[...], g, b_ref[...])
        z = z_ref[...]
        dgt = dg_ref[...]
        dz_ref[...] = (dgt * _silu(u2) * _dsilu(z)).astype(dz_ref.dtype)
        du2 = dgt * _silu(z) * _dsilu(u2)
        a_g[...] += _rowsum8(du2 * xhat)
        a_b[...] += _rowsum8(du2)
        dxh = du2 * g
        m1 = jnp.mean(dxh, axis=-1, keepdims=True)
        m2 = jnp.mean(dxh * xhat, axis=-1, keepdims=True)
        du1 = rstd * (dxh - m1 - xhat * m2)
        du1_ref[...] = du1
        a_c[...] += _rowsum8(du1)

        @pl.when(i == last)
        def _():
            dlg_ref[...] = jnp.sum(a_g[...], axis=0, keepdims=True)
            dlb_ref[...] = jnp.sum(a_b[...], axis=0, keepdims=True)
            dcb_ref[...] = jnp.sum(a_c[...], axis=0, keepdims=True)

    row = pl.BlockSpec((ts, e), lambda i: (i, 0))
    vec = pl.BlockSpec((1, e), lambda i: (0, 0))
    return pl.pallas_call(
        body, name=name, grid=(s_len // ts,),
        in_specs=[row, row, pl.BlockSpec((ts, e), lambda i: (i, 2)), vec, vec],
        out_specs=[row, row, vec, vec, vec],
        out_shape=[_sds((s_len, e), F32), _sds((s_len, e), BF16), _sds((1, e), F32), _sds((1, e), F32), _sds((1, e), F32)],
        scratch_shapes=[pltpu.VMEM((8, e), F32)] * 3,
        compiler_params=_params(20 * ts * e * 4),
    )(dgated, u1, proj, ln_g, ln_b)


def _a_conv_bwd(du1, proj, dz, conv_w, *, name, ts=256, tc=512):
    s_len, e3 = proj.shape
    e = e3 // 3
    k_width = conv_w.shape[0]
    ts, tc = _tile(s_len, ts), _tile(e, tc)
    nc, nr = e // tc, s_len // ts
    kp = 32
    hb = ts // A_HALO

    def body(val, gate, valh, gateh, d_ref, dh_ref, dz_ref, w_ref, dp_ref, dw_ref, buf_u, buf_d, du0, acc, sh_u, sh_d):
        i = pl.program_id(1)

        @pl.when(i == 0)
        def _():
            acc[...] = jnp.zeros_like(acc)

        buf_u[0:A_HALO, :] = jnp.where(i > 0, valh[...] * _sigmoid(gateh[...]), 0.0)
        buf_u[A_HALO:A_HALO + ts, :] = val[...] * _sigmoid(gate[...])
        buf_d[0:ts, :] = d_ref[...]
        buf_d[ts:ts + A_HALO, :] = jnp.where(i < nr - 1, dh_ref[...], 0.0)
        _fill_shifts(buf_u, sh_u)
        _fill_shifts(buf_d, sh_d)

        def init(rows, cs, shape):
            return jnp.zeros(shape, F32)

        def emit(rows, cs, a):
            du0[rows, cs] = a

        _conv_taps(buf_d, sh_d, w_ref, k_width, 0, ts, tc, init, emit, reverse=True)
        _conv_wgrad(buf_u, sh_u, buf_d, acc, k_width, A_HALO - (k_width - 1), ts, tc)
        rw = min(ts, CONV_ROWS)
        for rb in range(ts // rw):
            rows = slice(rb * rw, (rb + 1) * rw)
            d0 = du0[rows, :]
            sg = _sigmoid(gate[rows, :])
            dp_ref[0, rows, :] = (d0 * sg).astype(dp_ref.dtype)
            dp_ref[1, rows, :] = (d0 * val[rows, :] * sg * (1.0 - sg)).astype(dp_ref.dtype)
        dp_ref[2] = dz_ref[...]

        @pl.when(i == nr - 1)
        def _():
            for k in range(kp):
                dw_ref[k:k + 1, :] = jnp.sum(acc[8 * k:8 * k + 8, :], axis=0, keepdims=True)

    in_specs = [
        pl.BlockSpec((ts, tc), lambda j, i: (i, j)),
        pl.BlockSpec((ts, tc), lambda j, i: (i, nc + j)),
        pl.BlockSpec((A_HALO, tc), lambda j, i: (jnp.maximum(i * hb - 1, 0), j)),
        pl.BlockSpec((A_HALO, tc), lambda j, i: (jnp.maximum(i * hb - 1, 0), nc + j)),
        pl.BlockSpec((ts, tc), lambda j, i: (i, j)),
        pl.BlockSpec((A_HALO, tc), lambda j, i: (jnp.minimum((i + 1) * hb, nr * hb - 1), j)),
        pl.BlockSpec((ts, tc), lambda j, i: (i, j)),
        pl.BlockSpec((kp, tc), lambda j, i: (0, j)),
    ]
    w_pad = jnp.zeros((kp, e), F32).at[:k_width].set(conv_w)
    dproj, dw = pl.pallas_call(
        body, name=name, grid=(nc, nr), in_specs=in_specs,
        out_specs=[pl.BlockSpec((3, ts, tc), lambda j, i: (0, i, j)), pl.BlockSpec((kp, tc), lambda j, i: (0, j))],
        out_shape=[_sds((3, s_len, e), BF16), _sds((kp, e), F32)],
        scratch_shapes=[pltpu.VMEM((A_HALO + ts, tc), F32), pltpu.VMEM((ts + A_HALO, tc), F32),
                        pltpu.VMEM((ts, tc), F32), pltpu.VMEM((8 * kp, tc), F32),
                        pltpu.VMEM((SUBLANES - 1, A_HALO + ts - SUBLANES, tc), F32),
                        pltpu.VMEM((SUBLANES - 1, A_HALO + ts - SUBLANES, tc), F32)],
        compiler_params=_params(56 * ts * tc * 4),
    )(proj, proj, proj, proj, du1, du1, dz, w_pad)
    return dproj, dw[:k_width]


C_HALO = 8


def _c_mid_fwd(proj, conv_w, *, name, ts=256, tc=512):
    s_len, e4 = proj.shape
    e = e4 // 4
    k_width = conv_w.shape[0]
    ts, tc = _tile(s_len, ts), _tile(e, tc)
    nc = e // tc
    hb = ts // C_HALO

    def body(u, bg, cg, z, uh, cgh, w_ref, o_ref, buf, y):
        i = pl.program_id(1)
        buf[0:C_HALO, :] = jnp.where(i > 0, uh[...] * cgh[...], 0.0)
        buf[C_HALO:C_HALO + ts, :] = u[...] * cg[...]

        def init(rows, cs, shape):
            return jnp.zeros(shape, F32)

        def emit(rows, cs, a):
            y[rows, cs] = a

        _conv_taps(buf, None, w_ref, k_width, C_HALO - (k_width - 1), ts, tc, init, emit)
        o_ref[...] = (bg[...] * y[...] * _silu(z[...])).astype(o_ref.dtype)

    def grp(g):
        return pl.BlockSpec((ts, tc), lambda j, i: (i, g * nc + j))

    def halo(g):
        return pl.BlockSpec((C_HALO, tc), lambda j, i: (jnp.maximum(i * hb - 1, 0), g * nc + j))

    w_pad = jnp.zeros((8, e), F32).at[:k_width].set(conv_w)
    return pl.pallas_call(
        body, name=name, grid=(nc, s_len // ts),
        in_specs=[grp(0), grp(1), grp(2), grp(3), halo(0), halo(2), pl.BlockSpec((8, tc), lambda j, i: (0, j))],
        out_specs=pl.BlockSpec((ts, tc), lambda j, i: (i, j)), out_shape=_sds((s_len, e), BF16),
        scratch_shapes=[pltpu.VMEM((C_HALO + ts, tc), F32), pltpu.VMEM((ts, tc), F32)],
        compiler_params=_params(16 * ts * tc * 4),
    )(proj, proj, proj, proj, proj, proj, w_pad)


def _c_mid_bwd(dgated, proj, conv_w, *, name, ts=256, tc=512):
    s_len, e4 = proj.shape
    e = e4 // 4
    k_width = conv_w.shape[0]
    ts, tc = _tile(s_len, ts), _tile(e, tc)
    nc, nr = e // tc, s_len // ts
    hb = ts // C_HALO

    def body(u, bg, cg, z, uh, cgh, dg, dgh, bgh, zh, w_ref, dp_ref, dw_ref, buf_p, buf_d, y, dpv, acc):
        i = pl.program_id(1)

        @pl.when(i == 0)
        def _():
            acc[...] = jnp.zeros_like(acc)

        uv, bgv, cgv, zv, dgv = u[...], bg[...], cg[...], z[...], dg[...]
        buf_p[0:C_HALO, :] = jnp.where(i > 0, uh[...] * cgh[...], 0.0)
        buf_p[C_HALO:C_HALO + ts, :] = uv * cgv
        sz = _silu(zv)
        buf_d[0:ts, :] = dgv * sz * bgv
        buf_d[ts:ts + C_HALO, :] = jnp.where(i < nr - 1, dgh[...] * _silu(zh[...]) * bgh[...], 0.0)

        def init(rows, cs, shape):
            return jnp.zeros(shape, F32)

        def emit_y(rows, cs, a):
            y[rows, cs] = a

        def emit_dp(rows, cs, a):
            dpv[rows, cs] = a

        _conv_taps(buf_p, None, w_ref, k_width, C_HALO - (k_width - 1), ts, tc, init, emit_y)
        _conv_taps(buf_d, None, w_ref, k_width, 0, ts, tc, init, emit_dp, reverse=True)
        _conv_wgrad(buf_p, None, buf_d, acc, k_width, C_HALO - (k_width - 1), ts, tc)
        yv, dp = y[...], dpv[...]
        dp_ref[0] = (dp * cgv).astype(dp_ref.dtype)
        dp_ref[1] = (dgv * sz * yv).astype(dp_ref.dtype)
        dp_ref[2] = (dp * uv).astype(dp_ref.dtype)
        dp_ref[3] = (dgv * bgv * yv * _dsilu(zv)).astype(dp_ref.dtype)

        @pl.when(i == nr - 1)
        def _():
            for k in range(8):
                dw_ref[k:k + 1, :] = jnp.sum(acc[8 * k:8 * k + 8, :], axis=0, keepdims=True)

    def grp(g):
        return pl.BlockSpec((ts, tc), lambda j, i: (i, g * nc + j))

    def prev(g):
        return pl.BlockSpec((C_HALO, tc), lambda j, i: (jnp.maximum(i * hb - 1, 0), g * nc + j))

    def nxt(g):
        return pl.BlockSpec((C_HALO, tc), lambda j, i: (jnp.minimum((i + 1) * hb, nr * hb - 1), g * nc + j))

    w_pad = jnp.zeros((8, e), F32).at[:k_width].set(conv_w)
    dproj, dw = pl.pallas_call(
        body, name=name, grid=(nc, nr),
        in_specs=[grp(0), grp(1), grp(2), grp(3), prev(0), prev(2),
                  pl.BlockSpec((ts, tc), lambda j, i: (i, j)),
                  pl.BlockSpec((C_HALO, tc), lambda j, i: (jnp.minimum((i + 1) * hb, nr * hb - 1), j)),
                  nxt(1), nxt(3), pl.BlockSpec((8, tc), lambda j, i: (0, j))],
        out_specs=[pl.BlockSpec((4, ts, tc), lambda j, i: (0, i, j)), pl.BlockSpec((8, tc), lambda j, i: (0, j))],
        out_shape=[_sds((4, s_len, e), BF16), _sds((8, e), F32)],
        scratch_shapes=[pltpu.VMEM((C_HALO + ts, tc), F32), pltpu.VMEM((ts + C_HALO, tc), F32),
                        pltpu.VMEM((ts, tc), F32), pltpu.VMEM((ts, tc), F32), pltpu.VMEM((64, tc), F32)],
        compiler_params=_params(32 * ts * tc * 4),
    )(proj, proj, proj, proj, proj, proj, dgated, dgated, proj, proj, w_pad)
    return dproj, dw[:k_width]


def _head_rms(xv, g):
    r = lax.rsqrt(jnp.mean(xv * xv, axis=-1, keepdims=True) + NORM_EPS)
    return r, xv * r * g


def _b_qk_fwd(proj, gq, gk, *, name, ts=512, tc=512):
    s_len, e4 = proj.shape
    e = e4 // 4
    ts, tc = _tile(s_len, ts), _tile(e, tc)
    nc = e // tc

    def body(q, k, v, gq_ref, gk_ref, qn, kn, vb):
        for h in range(tc // HEAD_DIM):
            cs = slice(h * HEAD_DIM, (h + 1) * HEAD_DIM)
            qn[:, cs] = _head_rms(q[:, cs], gq_ref[...])[1].astype(qn.dtype)
            kn[:, cs] = _head_rms(k[:, cs], gk_ref[...])[1].astype(kn.dtype)
        vb[...] = v[...].astype(vb.dtype)

    def grp(g):
        return pl.BlockSpec((ts, tc), lambda i, j: (i, g * nc + j))

    vec = pl.BlockSpec((1, HEAD_DIM), lambda i, j: (0, 0))
    out = pl.BlockSpec((ts, tc), lambda i, j: (i, j))
    return pl.pallas_call(
        body, name=name, grid=(s_len // ts, nc), in_specs=[grp(0), grp(1), grp(2), vec, vec],
        out_specs=[out, out, out], out_shape=[_sds((s_len, e), BF16)] * 3,
        compiler_params=_params(16 * ts * tc * 4),
    )(proj, proj, proj, gq, gk)


def _b_qk_bwd(dqn, dkn, dv, dz, proj, gq, gk, *, name, ts=512, tc=512):
    s_len, e4 = proj.shape
    e = e4 // 4
    ts, tc = _tile(s_len, ts), _tile(e, tc)
    nc, nr = e // tc, s_len // ts

    def body(dq_ref, dk_ref, dv_ref, dz_ref, q, k, gq_ref, gk_ref, dp_ref, dgq_ref, dgk_ref, a_q, a_k):
        i, j = pl.program_id(0), pl.program_id(1)

        @pl.when((i == 0) & (j == 0))
        def _():
            a_q[...] = jnp.zeros_like(a_q)
            a_k[...] = jnp.zeros_like(a_k)

        for h in range(tc // HEAD_DIM):
            cs = slice(h * HEAD_DIM, (h + 1) * HEAD_DIM)
            for slot, src, d_ref, g_ref, acc in ((0, q, dq_ref, gq_ref, a_q), (1, k, dk_ref, gk_ref, a_k)):
                xv = src[:, cs]
                dy = d_ref[:, cs]
                r = lax.rsqrt(jnp.mean(xv * xv, axis=-1, keepdims=True) + NORM_EPS)
                gd = dy * g_ref[...]
                dot = jnp.mean(xv * gd, axis=-1, keepdims=True)
                dp_ref[slot, :, cs] = (r * gd - xv * (r * r * r * dot)).astype(dp_ref.dtype)
                acc[...] += _rowsum8(dy * (xv * r))
        dp_ref[2] = dv_ref[...].astype(dp_ref.dtype)
        dp_ref[3] = dz_ref[...]

        @pl.when((i == nr - 1) & (j == nc - 1))
        def _():
            dgq_ref[...] = jnp.sum(a_q[...], axis=0, keepdims=True)
            dgk_ref[...] = jnp.sum(a_k[...], axis=0, keepdims=True)

    blk = pl.BlockSpec((ts, tc), lambda i, j: (i, j))
    vec = pl.BlockSpec((1, HEAD_DIM), lambda i, j: (0, 0))

    def grp(g):
        return pl.BlockSpec((ts, tc), lambda i, j: (i, g * nc + j))

    return pl.pallas_call(
        body, name=name, grid=(nr, nc), in_specs=[blk, blk, blk, blk, grp(0), grp(1), vec, vec],
        out_specs=[pl.BlockSpec((4, ts, tc), lambda i, j: (0, i, j)), vec, vec],
        out_shape=[_sds((4, s_len, e), BF16), _sds((1, HEAD_DIM), F32), _sds((1, HEAD_DIM), F32)],
        scratch_shapes=[pltpu.VMEM((8, HEAD_DIM), F32)] * 2,
        compiler_params=_params(24 * ts * tc * 4),
    )(dqn, dkn, dv, dz, proj, proj, gq, gk)


def _log_sigmoid(x):
    y = jnp.exp(-jnp.abs(x))
    u = 1.0 + y
    log1p = jnp.where(u == 1.0, y, jnp.log(u) * (y / jnp.where(u == 1.0, 1.0, u - 1.0)))
    return jnp.minimum(x, 0.0) - log1p


def _split3(v):
    hi = v.astype(BF16)
    r1 = v - hi.astype(F32)
    mid = r1.astype(BF16)
    lo = (r1 - mid.astype(F32)).astype(BF16)
    return hi, mid, lo


def _tri_matmul(tri, v):
    hi, mid, lo = _split3(v)
    return (jnp.dot(tri, hi, preferred_element_type=F32) + jnp.dot(tri, mid, preferred_element_type=F32)
            + jnp.dot(tri, lo, preferred_element_type=F32))


def _b_cumsum(fl, bias, *, name, t=512):
    s_len, w = fl.shape
    t = _tile(s_len, t)

    def body(fl_ref, b_ref, c_ref, carry):
        @pl.when(pl.program_id(0) == 0)
        def _():
            carry[...] = jnp.zeros_like(carry)

        logf = _log_sigmoid(fl_ref[...] + b_ref[...])
        row = lax.broadcasted_iota(jnp.int32, (t, t), 0)
        col = lax.broadcasted_iota(jnp.int32, (t, t), 1)
        tri = jnp.where(col <= row, 1.0, 0.0).astype(BF16)
        c = _tri_matmul(tri, logf) + carry[...]
        c_ref[...] = c
        carry[...] = c[t - 1:t, :]

    return pl.pallas_call(
        body, name=name, grid=(s_len // t,),
        in_specs=[pl.BlockSpec((t, w), lambda i: (i, 0)), pl.BlockSpec((1, w), lambda i: (0, 0))],
        out_specs=pl.BlockSpec((t, w), lambda i: (i, 0)), out_shape=_sds((s_len, w), F32),
        scratch_shapes=[pltpu.VMEM((1, w), F32)], compiler_params=_params(16 << 20),
    )(fl, bias)


def _b_cumsum_bwd(dc, fl, bias, *, name, t=512):
    s_len, w = fl.shape
    t = _tile(s_len, t)
    n = s_len // t

    def body(dc_ref, fl_ref, b_ref, dfl_ref, db_ref, carry, acc):
        i = pl.program_id(0)

        @pl.when(i == 0)
        def _():
            carry[...] = jnp.zeros_like(carry)
            acc[...] = jnp.zeros_like(acc)

        row = lax.broadcasted_iota(jnp.int32, (t, t), 0)
        col = lax.broadcasted_iota(jnp.int32, (t, t), 1)
        tri = jnp.where(col >= row, 1.0, 0.0).astype(BF16)
        dlogf = _tri_matmul(tri, dc_ref[...]) + carry[...]
        carry[...] = dlogf[0:1, :]
        dfl = dlogf * _sigmoid(-(fl_ref[...] + b_ref[...]))
        dfl_ref[...] = dfl
        acc[...] += _rowsum8(dfl)

        @pl.when(i == n - 1)
        def _():
            db_ref[...] = jnp.sum(acc[...], axis=0, keepdims=True)

    rev = pl.BlockSpec((t, w), lambda i: (n - 1 - i, 0))
    vec = pl.BlockSpec((1, w), lambda i: (0, 0))
    return pl.pallas_call(
        body, name=name, grid=(n,), in_specs=[rev, rev, vec], out_specs=[rev, vec],
        out_shape=[_sds((s_len, w), F32), _sds((1, w), F32)],
        scratch_shapes=[pltpu.VMEM((1, w), F32), pltpu.VMEM((8, w), F32)], compiler_params=_params(16 << 20),
    )(dc, fl, bias)


def _b_gate_fwd(o, proj, *, name, ts=512, tc=512):
    s_len, e = o.shape
    ts, tc = _tile(s_len, ts), _tile(e, tc)
    nc = e // tc

    def body(o_ref, z_ref, g_ref):
        g_ref[...] = (o_ref[...] * _silu(z_ref[...])).astype(g_ref.dtype)

    blk = pl.BlockSpec((ts, tc), lambda i, j: (i, j))
    return pl.pallas_call(
        body, name=name, grid=(s_len // ts, nc),
        in_specs=[blk, pl.BlockSpec((ts, tc), lambda i, j: (i, 3 * nc + j))], out_specs=blk,
        out_shape=_sds((s_len, e), BF16), compiler_params=_params(12 * ts * tc * 4),
    )(o, proj)


def _b_gate_bwd(dgated, o, o_lo, proj, *, name, hg, ts=512):
    s_len, e = o.shape
    ts = _tile(s_len, ts)
    w = hg * HEAD_DIM
    ng = e // w

    def body(dg_ref, o_ref, olo_ref, z_ref, do_ref, dz_ref, dl_ref):
        dgt, ov, zv = dg_ref[...], o_ref[...], z_ref[...]
        dob = (dgt * _silu(zv)).astype(do_ref.dtype)
        do_ref[...] = dob
        dz_ref[...] = (dgt * ov * _dsilu(zv)).astype(dz_ref.dtype)
        prod = dob.astype(F32) * (ov + olo_ref[...])
        for hh in range(hg):
            cs = slice(hh * HEAD_DIM, (hh + 1) * HEAD_DIM)
            dl_ref[:, cs] = jnp.broadcast_to(jnp.sum(prod[:, cs], axis=-1, keepdims=True), (ts, HEAD_DIM))

    blk = pl.BlockSpec((ts, w), lambda i, j: (i, j))
    return pl.pallas_call(
        body, name=name, grid=(s_len // ts, ng),
        in_specs=[blk, blk, blk, pl.BlockSpec((ts, w), lambda i, j: (i, 3 * ng + j))],
        out_specs=[blk, blk, blk],
        out_shape=[_sds((s_len, e), BF16), _sds((s_len, e), BF16), _sds((s_len, e), F32)],
        compiler_params=_params(24 * ts * w * 4),
    )(dgated, o, o_lo, proj)


LOG2E = 1.4426950408889634
ATTN_ROW_CHUNK = 128


def _tri_tables(n, k_major):
    pairs = [(i, j) for j in range(n) for i in range(j, n)] if k_major else [(i, j) for i in range(n) for j in range(i + 1)]
    return (jnp.asarray(np.array([p[0] for p in pairs], np.int32)), jnp.asarray(np.array([p[1] for p in pairs], np.int32)))


def _attn_logits2(s_raw, cr2, diag, row0, c1):
    s2 = s_raw * c1 - cr2
    if diag:
        rc, t = s_raw.shape
        row = lax.broadcasted_iota(jnp.int32, (rc, t), 0) + row0
        col = lax.broadcasted_iota(jnp.int32, (rc, t), 1)
        s2 = jnp.where(col <= row, s2, -jnp.inf)
    return s2


def _fox_fwd(qn, kn, vb, c_row, *, name, hg, t=512):
    s_len, e = qn.shape
    t = _tile(s_len, t)
    rc = _tile(t, ATTN_ROW_CHUNK)
    w = hg * HEAD_DIM
    ng, n = e // w, s_len // t
    c1 = HEAD_DIM ** -0.5 * LOG2E
    qi_tab, kj_tab = _tri_tables(n, k_major=False)

    def body(qi_ref, kj_ref, q_ref, k_ref, v_ref, cr_ref, o_ref, olo_ref, lse_ref, m_s, l_s, acc_s, lo_s, s_scr, p_scr,
             a_scr):
        pid = pl.program_id(1)
        i, j = qi_ref[pid], kj_ref[pid]

        @pl.when(j == 0)
        def _():
            m_s[...] = jnp.full_like(m_s, -jnp.inf)
            l_s[...] = jnp.zeros_like(l_s)
            acc_s[...] = jnp.zeros_like(acc_s)
            lo_s[...] = jnp.zeros_like(lo_s)

        def step(diag):
            for hh in range(hg):
                cs = slice(hh * HEAD_DIM, (hh + 1) * HEAD_DIM)
                s_scr[...] = lax.dot_general(q_ref[:, cs], k_ref[:, cs], (((1,), (1,)), ((), ())),
                                             preferred_element_type=F32)
                cr2 = cr_ref[hh] * LOG2E
                for r in range(t // rc):
                    rows = slice(r * rc, (r + 1) * rc)
                    s2 = _attn_logits2(s_scr[rows, :], cr2, diag, r * rc, c1)
                    m_prev = m_s[hh, rows]
                    m_new = jnp.maximum(m_prev, jnp.max(s2, axis=-1, keepdims=True))
                    alpha = jnp.exp2(m_prev - m_new)
                    p = jnp.exp2(s2 - jnp.tile(m_new, (1, t // LANES)))
                    l_s[hh, rows] = alpha * l_s[hh, rows] + jnp.sum(p, axis=-1, keepdims=True)
                    m_s[hh, rows] = m_new
                    a_scr[rows] = alpha
                    p_hi = p.astype(BF16)
                    p_scr[rows, :] = p_hi
                    p_scr[t + r * rc:t + (r + 1) * rc, :] = (p - p_hi.astype(F32)).astype(BF16)
                pv = jnp.dot(p_scr[...], v_ref[:, cs], preferred_element_type=F32)
                al = a_scr[...]
                acc_s[:, cs] = al * acc_s[:, cs] + pv[:t]
                lo_s[:, cs] = al * lo_s[:, cs] + pv[t:]

        @pl.when(j < i)
        def _():
            step(False)

        @pl.when(j == i)
        def _():
            step(True)
            for hh in range(hg):
                cs = slice(hh * HEAD_DIM, (hh + 1) * HEAD_DIM)
                o_ref[:, cs] = acc_s[:, cs] / l_s[hh]
                olo_ref[:, cs] = lo_s[:, cs] / l_s[hh]
                lse_ref[:, cs] = m_s[hh] + jnp.log2(l_s[hh])

    qspec = pl.BlockSpec((t, w), lambda g, p, qi, kj: (qi[p], g))
    kspec = pl.BlockSpec((t, w), lambda g, p, qi, kj: (kj[p], g))
    crow = pl.BlockSpec((hg, 1, t), lambda g, p, qi, kj: (g, 0, kj[p]))
    grid_spec = pltpu.PrefetchScalarGridSpec(
        num_scalar_prefetch=2, grid=(ng, int(qi_tab.shape[0])), in_specs=[qspec, kspec, kspec, crow],
        out_specs=[qspec, qspec, qspec],
        scratch_shapes=[pltpu.VMEM((hg, t, LANES), F32), pltpu.VMEM((hg, t, LANES), F32), pltpu.VMEM((t, w), F32),
                        pltpu.VMEM((t, w), F32), pltpu.VMEM((t, t), F32), pltpu.VMEM((2 * t, t), BF16),
                        pltpu.VMEM((t, LANES), F32)])
    return pl.pallas_call(
        body, name=name, grid_spec=grid_spec,
        out_shape=[_sds((s_len, e), F32), _sds((s_len, e), F32), _sds((s_len, e), F32)],
        compiler_params=_params(12 * t * t * 4 + 32 * t * w * 4),
    )(qi_tab, kj_tab, qn, kn, vb, c_row)


def _fox_bwd(qn, kn, vb, do, lse, delta, c_row, *, name, hg, t=512):
    s_len, e = qn.shape
    t = _tile(s_len, t)
    w = hg * HEAD_DIM
    ng, n = e // w, s_len // t
    scale = HEAD_DIM ** -0.5
    c1 = scale * LOG2E
    qi_tab, kj_tab = _tri_tables(n, k_major=True)

    def body(qi_ref, kj_ref, q_ref, k_ref, v_ref, do_ref, lse_ref, dl_ref, cr_ref, dq_ref, dk_ref, dv_ref, dc_ref,
             dk_s, dv_s, dc_s):
        pid = pl.program_id(1)
        i, j = qi_ref[pid], kj_ref[pid]

        @pl.when(pid == 0)
        def _():
            dq_ref[...] = jnp.zeros_like(dq_ref)

        @pl.when(i == j)
        def _():
            dk_s[...] = jnp.zeros_like(dk_s)
            dv_s[...] = jnp.zeros_like(dv_s)
            dc_s[...] = jnp.zeros_like(dc_s)

        def step(diag):
            rows = pl.ds(pl.multiple_of(i * t, t), t)
            for hh in range(hg):
                cs = slice(hh * HEAD_DIM, (hh + 1) * HEAD_DIM)
                q, k, v, dov = q_ref[:, cs], k_ref[:, cs], v_ref[:, cs], do_ref[:, cs]
                s_raw = lax.dot_general(q, k, (((1,), (1,)), ((), ())), preferred_element_type=F32)
                s2 = _attn_logits2(s_raw, cr_ref[hh] * LOG2E, diag, 0, c1)
                p = jnp.exp2(s2 - jnp.tile(lse_ref[:, cs], (1, t // LANES)))
                dv_s[:, cs] += lax.dot_general(p.astype(BF16), dov, (((0,), (0,)), ((), ())), preferred_element_type=F32)
                dp = lax.dot_general(dov, v, (((1,), (1,)), ((), ())), preferred_element_type=F32)
                ds = p * (dp - jnp.tile(dl_ref[:, cs], (1, t // LANES)))
                dc_s[hh] -= jnp.sum(ds, axis=0, keepdims=True)
                dsb = (ds * scale).astype(BF16)
                dk_s[:, cs] += lax.dot_general(dsb, q, (((0,), (0,)), ((), ())), preferred_element_type=F32)
                dq_ref[rows, cs] += jnp.dot(dsb, k, preferred_element_type=F32)

        @pl.when(i > j)
        def _():
            step(False)

        @pl.when(i == j)
        def _():
            step(True)

        @pl.when(i == n - 1)
        def _():
            dk_ref[...] = dk_s[...]
            dv_ref[...] = dv_s[...]
            dc_ref[...] = dc_s[...]

    qspec = pl.BlockSpec((t, w), lambda g, p, qi, kj: (qi[p], g))
    kspec = pl.BlockSpec((t, w), lambda g, p, qi, kj: (kj[p], g))
    crow = pl.BlockSpec((hg, 1, t), lambda g, p, qi, kj: (g, 0, kj[p]))
    grid_spec = pltpu.PrefetchScalarGridSpec(
        num_scalar_prefetch=2, grid=(ng, int(qi_tab.shape[0])),
        in_specs=[qspec, kspec, kspec, qspec, qspec, qspec, crow],
        out_specs=[pl.BlockSpec((s_len, w), lambda g, p, qi, kj: (0, g)), kspec, kspec, crow],
        scratch_shapes=[pltpu.VMEM((t, w), F32), pltpu.VMEM((t, w), F32), pltpu.VMEM((hg, 1, t), F32)])
    return pl.pallas_call(
        body, name=name, grid_spec=grid_spec,
        out_shape=[_sds((s_len, e), F32), _sds((s_len, e), F32), _sds((s_len, e), F32), _sds((e // HEAD_DIM, 1, s_len), F32)],
        compiler_params=_params(2 * s_len * w * 4 + 16 * t * t * 4 + 24 * t * w * 4),
    )(qi_tab, kj_tab, qn, kn, vb, do, lse, delta, c_row)


def _adamw(w, gs, m, v, *, name, tr=256):
    r, c = w.shape
    tr = _tile(r, tr)
    n_g = len(gs)

    def body(*refs):
        w_ref, g_refs = refs[0], refs[1:1 + n_g]
        m_ref, v_ref, go_ref, d_ref, nm_ref, nv_ref = refs[1 + n_g:]
        gv = g_refs[0][...].astype(F32)
        for g_ref in g_refs[1:]:
            gv = gv + g_ref[...].astype(F32)
        go_ref[...] = gv
        m2 = ADAM_B1 * m_ref[...] + (1.0 - ADAM_B1) * gv
        v2 = ADAM_B2 * v_ref[...] + (1.0 - ADAM_B2) * (gv * gv)
        m_hat = m2 / (1.0 - ADAM_B1 ** ADAM_STEP)
        v_hat = v2 / (1.0 - ADAM_B2 ** ADAM_STEP)
        d_ref[...] = -ADAM_LR * (m_hat / (jnp.sqrt(v_hat) + ADAM_EPS) + ADAM_WD * w_ref[...])
        nm_ref[...] = m2
        nv_ref[...] = v2

    blk = pl.BlockSpec((tr, c), lambda i: (i, 0))
    return pl.pallas_call(
        body, name=name, grid=(r // tr,), in_specs=[blk] * (3 + n_g), out_specs=[blk] * 4,
        out_shape=[_sds((r, c), F32)] * 4, compiler_params=_params(24 * tr * c * 4),
    )(w, *gs, m, v)


def _adamw_halves(w, mine, other, m, v, c_arr, *, name, tr=256):
    r, c = w.shape
    half = r // 2
    tr = _tile(half, tr)
    nh = half // tr

    def body(c_ref, w_ref, mine_ref, other_ref, m_ref, v_ref, go_ref, d_ref, nm_ref, nv_ref):
        is_mine = pl.program_id(0) == c_ref[0]
        gv = jnp.where(is_mine, mine_ref[...], other_ref[...])
        go_ref[...] = gv
        m2 = ADAM_B1 * m_ref[...] + (1.0 - ADAM_B1) * gv
        v2 = ADAM_B2 * v_ref[...] + (1.0 - ADAM_B2) * (gv * gv)
        m_hat = m2 / (1.0 - ADAM_B1 ** ADAM_STEP)
        v_hat = v2 / (1.0 - ADAM_B2 ** ADAM_STEP)
        d_ref[...] = -ADAM_LR * (m_hat / (jnp.sqrt(v_hat) + ADAM_EPS) + ADAM_WD * w_ref[...])
        nm_ref[...] = m2
        nv_ref[...] = v2

    full = pl.BlockSpec((tr, c), lambda h, j, cref: (h * nh + j, 0))
    part = pl.BlockSpec((tr, c), lambda h, j, cref: (j, 0))
    grid_spec = pltpu.PrefetchScalarGridSpec(num_scalar_prefetch=1, grid=(2, nh), in_specs=[full, part, part, full, full],
                                             out_specs=[full] * 4)
    return pl.pallas_call(
        body, name=name, grid_spec=grid_spec, out_shape=[_sds((r, c), F32)] * 4,
        compiler_params=_params(28 * tr * c * 4),
    )(c_arr, w, mine, other, m, v)


def _pair_add(g, got, c_arr, *, name, tr=256):
    n, r, c = g.shape
    half = r // 2
    tr = _tile(half, tr)
    nh = half // tr

    def body(c_ref, g_ref, got_ref, o_ref):
        o_ref[...] = (g_ref[...].astype(F32) + got_ref[...].astype(F32)).astype(o_ref.dtype)

    grid_spec = pltpu.PrefetchScalarGridSpec(
        num_scalar_prefetch=1, grid=(n, nh),
        in_specs=[pl.BlockSpec((None, tr, c), lambda s, i, cref: (s, cref[0] * nh + i, 0)),
                  pl.BlockSpec((None, tr, c), lambda s, i, cref: (s, i, 0))],
        out_specs=pl.BlockSpec((None, tr, c), lambda s, i, cref: (s, i, 0)))
    return pl.pallas_call(
        body, name=name, grid_spec=grid_spec, out_shape=_sds((n, half, c), BF16), compiler_params=_params(16 * tr * c * 4),
    )(c_arr, g, got)


def _sum_own_recv(pair, recv, chip_arr, *, name, tr=256):
    _, r, c = pair.shape
    tr = _tile(r, tr)
    n_recv = recv.shape[0]

    def body(chip_ref, own_ref, recv_ref, o_ref):
        acc = own_ref[...].astype(F32)
        for k in range(n_recv):
            acc = acc + recv_ref[k].astype(F32)
        o_ref[...] = acc

    grid_spec = pltpu.PrefetchScalarGridSpec(
        num_scalar_prefetch=1, grid=(r // tr,),
        in_specs=[pl.BlockSpec((None, tr, c), lambda i, chip: (chip[0], i, 0)),
                  pl.BlockSpec((n_recv, tr, c), lambda i, chip: (0, i, 0))],
        out_specs=pl.BlockSpec((tr, c), lambda i, chip: (i, 0)))
    return pl.pallas_call(
        body, name=name, grid_spec=grid_spec, out_shape=_sds((r, c), F32), compiler_params=_params(16 * tr * c * 4),
    )(chip_arr, pair, recv)


_ANY = pl.BlockSpec(memory_space=pl.ANY)
DMA_CHUNK_BYTES = 512 << 10


def _chunks(parts):
    out = []
    for src_at, dst_at, rows, row_bytes in parts:
        step = max(16, DMA_CHUNK_BYTES // row_bytes // 16 * 16)
        for r0 in range(0, rows, step):
            n = min(step, rows - r0)
            out.append((src_at(r0, n), dst_at(r0, n)))
    return out


def _row_bytes(ref):
    return ref.shape[-1] * ref.dtype.itemsize


def _me():
    return lax.axis_index("x"), lax.axis_index("y"), lax.axis_index("c")


def _chip_peers(x, y):
    return [(2 * (1 - x) + y, (1 - x, y)), (2 * x + (1 - y), (x, 1 - y)), (2 * (1 - x) + (1 - y), (1 - x, 1 - y))]


def _exchange(name, ins, out_shapes, plan):
    n_in, n_out = len(ins), len(out_shapes)

    def body(*refs):
        in_refs, out_refs = refs[:n_in], refs[n_in:n_in + n_out]
        send_sems, recv_sems, loc_sems = refs[n_in + n_out:]
        remote, local = plan(in_refs, out_refs)
        starts, waits = [], []
        for k, (ws, wd, dev, parts) in enumerate(remote):
            def mk(s, d, k=k, dev=dev):
                return pltpu.make_async_remote_copy(src_ref=s, dst_ref=d, send_sem=send_sems.at[k],
                                                    recv_sem=recv_sems.at[k], device_id=dev, device_id_type=MESH_ID)
            starts += [mk(s, d) for s, d in _chunks(parts)]
            waits.append(mk(ws, wd))
        for k, (ws, wd, _, parts) in enumerate(local):
            def mk(s, d, k=k):
                return pltpu.make_async_copy(s, d, loc_sems.at[k])
            starts += [mk(s, d) for s, d in _chunks(parts)]
            waits.append(mk(ws, wd))
        for cp in starts:
            cp.start()
        for cp in waits:
            cp.wait()

    n_remote, n_local = plan.n_remote, plan.n_local
    return pl.pallas_call(
        body, name=name, in_specs=[_ANY] * n_in, out_specs=[_ANY] * n_out, out_shape=list(out_shapes),
        scratch_shapes=[pltpu.SemaphoreType.DMA((n_remote,)), pltpu.SemaphoreType.DMA((n_remote,)),
                        pltpu.SemaphoreType.DMA((max(n_local, 1),))],
    )(*ins)


def _gather_weights(shards, *, name):
    n_t = len(shards)

    def body(*refs):
        in_refs, out_refs = refs[:n_t], refs[n_t:2 * n_t]
        send_sems, recv_sems = refs[2 * n_t:]
        x, y, c = _me()
        chip = 2 * x + y
        peers = _chip_peers(x, y)
        local, first, passed = [], [], []
        for t in range(n_t):
            src, dst = in_refs[t], out_refs[t]
            n_rows = src.shape[0]
            half = n_rows // 2
            rb = _row_bytes(src)
            rows = pl.ds(c * half, half)

            def mk_own(s, d, t=t):
                return pltpu.make_async_remote_copy(
                    src_ref=s, dst_ref=d, send_sem=send_sems.at[7 * t + 6], recv_sem=recv_sems.at[7 * t + 6],
                    device_id=(x, y, 1 - c), device_id_type=MESH_ID)

            own = [(lambda r0, n, src=src: src.at[pl.ds(r0, n)],
                    lambda r0, n, dst=dst: dst.at[chip, pl.ds(r0, n)], n_rows, rb)]
            local.append((mk_own(src, dst.at[chip]), [mk_own(s, d) for s, d in _chunks(own)]))
            for k, (pchip, (px, py)) in enumerate(peers):
                def mk_ici(s, d, t=t, k=k, px=px, py=py):
                    return pltpu.make_async_remote_copy(
                        src_ref=s, dst_ref=d, send_sem=send_sems.at[7 * t + k], recv_sem=recv_sems.at[7 * t + k],
                        device_id=(px, py, c), device_id_type=MESH_ID)

                def mk_d2d(s, d, t=t, k=k):
                    return pltpu.make_async_remote_copy(
                        src_ref=s, dst_ref=d, send_sem=send_sems.at[7 * t + 3 + k], recv_sem=recv_sems.at[7 * t + 3 + k],
                        device_id=(x, y, 1 - c), device_id_type=MESH_ID)

                out_part = [(lambda r0, n, src=src: src.at[pl.ds(c * half + r0, n)],
                             lambda r0, n, dst=dst: dst.at[chip, pl.ds(c * half + r0, n)], half, rb)]
                fwd_part = [(lambda r0, n, dst=dst, pchip=pchip: dst.at[pchip, pl.ds(c * half + r0, n)],
                             lambda r0, n, dst=dst, pchip=pchip: dst.at[pchip, pl.ds(c * half + r0, n)], half, rb)]
                first.append((mk_ici(src.at[rows], dst.at[chip, rows]), [mk_ici(s, d) for s, d in _chunks(out_part)]))
                passed.append((mk_d2d(dst.at[pchip, rows], dst.at[pchip, rows]),
                               [mk_d2d(s, d) for s, d in _chunks(fwd_part)]))
        for _, chunk_copies in first + local:
            for cp in chunk_copies:
                cp.start()
        for (whole, _), (_, fwd_copies) in zip(first, passed):
            whole.wait_recv()
            for cp in fwd_copies:
                cp.start()
        for whole, _ in passed:
            whole.wait_recv()
        for whole, _ in first + passed:
            whole.wait_send()
        for whole, _ in local:
            whole.wait()

    outs = [_sds((N_CHIPS,) + s.shape, s.dtype) for s in shards]
    return pl.pallas_call(
        body, name=name, in_specs=[_ANY] * n_t, out_specs=[_ANY] * n_t, out_shape=outs,
        scratch_shapes=[pltpu.SemaphoreType.DMA((7 * n_t,)), pltpu.SemaphoreType.DMA((7 * n_t,))],
    )(*shards)


class _Plan:
    def __init__(self, fn, n_remote, n_local):
        self.fn, self.n_remote, self.n_local = fn, n_remote, n_local

    def __call__(self, in_refs, out_refs):
        return self.fn(in_refs, out_refs)


def _reduce_grads(grads, *, name):
    n_t = len(grads)
    c_arr = lax.axis_index("c").astype(jnp.int32).reshape(1)
    chip_arr = (2 * lax.axis_index("x") + lax.axis_index("y")).astype(jnp.int32).reshape(1)

    def plan1(in_refs, out_refs):
        x, y, c = _me()
        remote = []
        for t in range(n_t):
            src, got = in_refs[t], out_refs[t]
            half = src.shape[1] // 2
            send = [(lambda r0, n, s=s, src=src, half=half: src.at[s, pl.ds((1 - c) * half + r0, n)],
                     lambda r0, n, s=s, got=got: got.at[s, pl.ds(r0, n)], half, _row_bytes(src)) for s in range(N_CHIPS)]
            remote.append((src.at[:, pl.ds((1 - c) * half, half)], got, (x, y, 1 - c), send))
        return remote, []

    halves = [_sds((N_CHIPS, g.shape[1] // 2, g.shape[2]), g.dtype) for g in grads]
    got = _exchange(name + "_sib", grads, halves, _Plan(plan1, n_t, 0))
    pair = [_pair_add(grads[t], got[t], c_arr, name=f"{name}_pair{t}") for t in range(n_t)]

    def plan2(in_refs, out_refs):
        x, y, c = _me()
        remote = []
        for t in range(n_t):
            src, dst = in_refs[t], out_refs[t]
            rows, rb = src.shape[1], _row_bytes(src)
            for k, (pchip, (px, py)) in enumerate(_chip_peers(x, y)):
                part = [(lambda r0, n, src=src, pchip=pchip: src.at[pchip, pl.ds(r0, n)],
                         lambda r0, n, dst=dst, k=k: dst.at[k, pl.ds(r0, n)], rows, rb)]
                remote.append((src.at[pchip], dst.at[k], (px, py, c), part))
        return remote, []

    recv_shapes = [_sds((N_CHIPS - 1,) + h.shape[1:], h.dtype) for h in halves]
    recv = _exchange(name + "_ici", pair, recv_shapes, _Plan(plan2, 3 * n_t, 0))
    mine = [_sum_own_recv(pair[t], recv[t], chip_arr, name=f"{name}_sum{t}") for t in range(n_t)]

    def plan3(in_refs, out_refs):
        x, y, c = _me()
        remote = []
        for t in range(n_t):
            src, dst = in_refs[t], out_refs[t]
            rows = [(lambda r0, n, src=src: src.at[pl.ds(r0, n)], lambda r0, n, dst=dst: dst.at[pl.ds(r0, n)],
                     src.shape[0], _row_bytes(src))]
            remote.append((src, dst, (x, y, 1 - c), rows))
        return remote, []

    other = _exchange(name + "_swap", mine, [_sds(s.shape, s.dtype) for s in mine], _Plan(plan3, n_t, 0))
    return list(zip(mine, other)), c_arr


def _allreduce_small(pack, *, name):
    r, w = pack.shape

    def body(p_ref, o_ref, buf, send_sems, recv_sems):
        x, y, c = _me()
        me = 4 * x + 2 * y + c
        buf[me] = p_ref[...]
        copies = []
        for k in range(1, N_DEV):
            peer = (x ^ ((k >> 2) & 1), y ^ ((k >> 1) & 1), c ^ (k & 1))
            copies.append(pltpu.make_async_remote_copy(
                src_ref=p_ref, dst_ref=buf.at[me], send_sem=send_sems.at[k - 1], recv_sem=recv_sems.at[k - 1],
                device_id=peer, device_id_type=MESH_ID))
        for cp in copies:
            cp.start()
        for cp in copies:
            cp.wait()
        acc = buf[0]
        for k in range(1, N_DEV):
            acc = acc + buf[k]
        o_ref[...] = acc

    vm = pl.BlockSpec(memory_space=pltpu.VMEM)
    return pl.pallas_call(
        body, name=name, in_specs=[vm], out_specs=vm, out_shape=_sds((r, w), F32),
        scratch_shapes=[pltpu.VMEM((N_DEV, r, w), F32), pltpu.SemaphoreType.DMA((N_DEV - 1,)),
                        pltpu.SemaphoreType.DMA((N_DEV - 1,))],
        compiler_params=_params(12 * r * w * 4),
    )(pack)


def _pack(arrs, row_multiple=16):
    flat = jnp.concatenate([a.reshape(-1).astype(F32) for a in arrs])
    unit = row_multiple * LANES
    total = -(-flat.shape[0] // unit) * unit
    return jnp.pad(flat, (0, total - flat.shape[0])).reshape(total // LANES, LANES)


def _unpack(packed, shapes):
    flat = packed.reshape(-1)
    out, off = [], 0
    for shp in shapes:
        n = int(np.prod(shp))
        out.append(flat[off:off + n].reshape(shp))
        off += n
    return out


def _pad_cols(a, width):
    return jnp.pad(a, [(0, 0)] * (a.ndim - 1) + [(0, width - a.shape[-1])])


ATTN_HEADS_PER_STEP = 2
SMALL_SHARDED = ("a_norm", "a_conv_w", "a_conv_b", "a_ln_g", "a_ln_b", "c_norm", "c_conv_w")
SMALL_REPLICATED = ("b_norm", "b_f_bias", "b_q_norm", "b_k_norm")
BIG = ("a_w_in", "a_w_out", "b_w_in", "b_w_out", "c_w_in", "c_w_out")
WEIGHTS = ("a_norm", "a_w_in", "a_conv_w", "a_conv_b", "a_ln_g", "a_ln_b", "a_w_out", "b_norm", "b_w_in", "b_f_bias",
           "b_q_norm", "b_k_norm", "b_w_out", "c_norm", "c_w_in", "c_conv_w", "c_w_out")


def _mixer_a_fwd(x, p, l, tag):
    h = _rms_fwd(x, p["a_norm"][l][None], name=f"{tag}_rms")
    proj = _mm_nn(h, p["a_w_in"][l], name=f"{tag}_in", tn=p["a_w_in"][l].shape[2])
    u1 = _a_conv_fwd(proj, p["a_conv_w"][l], p["a_conv_b"][l][None], name=f"{tag}_conv")
    gated = _a_post_fwd(u1, proj, p["a_ln_g"][l][None], p["a_ln_b"][l][None], name=f"{tag}_post")
    y = _mm_nn(gated, p["a_w_out"][l], name=f"{tag}_out", tn=1024, add=x)
    return y, (x, h, proj, u1, gated)


def _mixer_a_bwd(dx, saved, p, l, tag):
    x, h, proj, u1, gated = saved
    g = {}
    g["a_w_out"] = _mm_tn(gated, dx, name=f"{tag}_dwout", out_dtype=BF16, tk=2048, tn=1024, ts=1024)
    dgated = _mm_nt(dx, p["a_w_out"][l], name=f"{tag}_dgated", tn=2048, tk=1024)
    du1, dz, g["a_ln_g"], g["a_ln_b"], g["a_conv_b"] = _a_post_bwd(
        dgated, u1, proj, p["a_ln_g"][l][None], p["a_ln_b"][l][None], name=f"{tag}_dpost")
    dproj, g["a_conv_w"] = _a_conv_bwd(du1, proj, dz, p["a_conv_w"][l], name=f"{tag}_dconv")
    g["a_w_in"] = _mm_tn(h, dproj, name=f"{tag}_dwin", out_dtype=BF16, out_width=p["a_w_in"][l].shape[2], ts=2048)
    dh = _mm_nt(dproj, p["a_w_in"][l], name=f"{tag}_dh", ksub=4)
    dx, g["a_norm"] = _rms_bwd(x, p["a_norm"][l][None], [dh], dx, name=f"{tag}_drms")
    return dx, g


def _mixer_c_fwd(x, p, tag):
    h = _rms_fwd(x, p["c_norm"][0][None], name=f"{tag}_rms")
    proj = _mm_nn(h, p["c_w_in"], name=f"{tag}_in", tn=1024)
    gated = _c_mid_fwd(proj, p["c_conv_w"][0], name=f"{tag}_mid")
    y = _mm_nn(gated, p["c_w_out"], name=f"{tag}_out", tn=1024, add=x)
    return y, (x, h, proj, gated)


def _mixer_c_bwd(dx, saved, p, tag):
    x, h, proj, gated = saved
    g = {}
    g["c_w_out"] = _mm_tn(gated, dx, name=f"{tag}_dwout", out_dtype=BF16, tk=2048, tn=1024, ts=1024)
    dgated = _mm_nt(dx, p["c_w_out"], name=f"{tag}_dgated", tn=2048, tk=1024)
    dproj, g["c_conv_w"] = _c_mid_bwd(dgated, proj, p["c_conv_w"][0], name=f"{tag}_dmid")
    g["c_w_in"] = _mm_tn(h, dproj, name=f"{tag}_dwin", out_dtype=BF16, out_width=p["c_w_in"].shape[2], tn=1024, ts=2048)
    dh = _mm_nt(dproj, p["c_w_in"], name=f"{tag}_dh", tk=1024, ksub=2)
    dx, g["c_norm"] = _rms_bwd(x, p["c_norm"][0][None], [dh], dx, name=f"{tag}_drms")
    return dx, g


def _mixer_b_fwd(x, p, tag):
    hg = ATTN_HEADS_PER_STEP
    s_len = x.shape[0]
    n_heads = p["b_f_bias"].shape[1]
    h = _rms_fwd(x, p["b_norm"], name=f"{tag}_rms")
    proj = _mm_nn(h, p["b_wq"], name=f"{tag}_in", tn=1024)
    fl = _mm_nn(h, p["b_wf"], name=f"{tag}_inf", tn=LANES)
    qn, kn, vb = _b_qk_fwd(proj, p["b_q_norm"], p["b_k_norm"], name=f"{tag}_qk")
    bias = _pad_cols(p["b_f_bias"], LANES)
    c = _b_cumsum(fl, bias, name=f"{tag}_cumsum")
    ch = c[:, :n_heads]
    c_row = ch.T.reshape(n_heads, 1, s_len)
    o, o_lo, lse = _fox_fwd(qn, kn, vb, c_row, name=f"{tag}_attn", hg=hg)
    gated = _b_gate_fwd(o, proj, name=f"{tag}_gate")
    y = _mm_nn(gated, p["b_w_out"], name=f"{tag}_out", tn=1024, add=x)
    return y, (x, h, proj, fl, bias, qn, kn, vb, c_row, o, o_lo, lse, gated)


def _mixer_b_bwd(dx, saved, p, tag):
    hg = ATTN_HEADS_PER_STEP
    x, h, proj, fl, bias, qn, kn, vb, c_row, o, o_lo, lse, gated = saved
    s_len = x.shape[0]
    n_heads = p["b_f_bias"].shape[1]
    g = {}
    g["b_w_out"] = _mm_tn(gated, dx, name=f"{tag}_dwout", out_dtype=BF16, tk=2048, tn=1024, ts=1024)
    dgated = _mm_nt(dx, p["b_w_out"], name=f"{tag}_dgated", tn=2048, tk=1024)
    do, dz, delta = _b_gate_bwd(dgated, o, o_lo, proj, name=f"{tag}_dgate", hg=hg)
    dqn, dkn, dv, dc = _fox_bwd(qn, kn, vb, do, lse, delta, c_row, name=f"{tag}_dattn", hg=hg)
    dc_pad = _pad_cols(dc.reshape(n_heads, s_len).T, LANES)
    dfl, dbias = _b_cumsum_bwd(dc_pad, fl, bias, name=f"{tag}_dcumsum")
    g["b_f_bias"] = dbias[:, :n_heads]
    dproj, g["b_q_norm"], g["b_k_norm"] = _b_qk_bwd(dqn, dkn, dv, dz, proj, p["b_q_norm"], p["b_k_norm"], name=f"{tag}_dqk")
    dwq = _mm_tn(h, dproj, name=f"{tag}_dwin", out_dtype=BF16, tn=1024, ts=2048)
    dwf = _mm_tn(h, dfl, name=f"{tag}_dwinf", out_dtype=BF16, tn=LANES)
    g["b_w_in"] = jnp.concatenate([dwq, dwf[:, :n_heads]], axis=1)
    dh = _mm_nt(dproj, p["b_wq"], name=f"{tag}_dh", tk=1024, ksub=2)
    dhf = _mm_nt(dfl, p["b_wf"], name=f"{tag}_dhf", tk=LANES)
    dx, g["b_norm"] = _rms_bwd(x, p["b_norm"], [dh, dhf], dx, name=f"{tag}_drms")
    return dx, g


def kernel(x, a_norm, a_w_in, a_conv_w, a_conv_b, a_ln_g, a_ln_b, a_w_out, b_norm, b_w_in, b_f_bias, b_q_norm, b_k_norm, b_w_out, c_norm, c_w_in, c_conv_w, c_w_out, loss_target, m_a_norm, m_a_w_in, m_a_conv_w, m_a_conv_b, m_a_ln_g, m_a_ln_b, m_a_w_out, m_b_norm, m_b_w_in, m_b_f_bias, m_b_q_norm, m_b_k_norm, m_b_w_out, m_c_norm, m_c_w_in, m_c_conv_w, m_c_w_out, v_a_norm, v_a_w_in, v_a_conv_w, v_a_conv_b, v_a_ln_g, v_a_ln_b, v_a_w_out, v_b_norm, v_b_w_in, v_b_f_bias, v_b_q_norm, v_b_k_norm, v_b_w_out, v_c_norm, v_c_w_in, v_c_conv_w, v_c_w_out):
    w_loc = dict(a_norm=a_norm, a_w_in=a_w_in, a_conv_w=a_conv_w, a_conv_b=a_conv_b, a_ln_g=a_ln_g, a_ln_b=a_ln_b,
                 a_w_out=a_w_out, b_norm=b_norm, b_w_in=b_w_in, b_f_bias=b_f_bias, b_q_norm=b_q_norm, b_k_norm=b_k_norm,
                 b_w_out=b_w_out, c_norm=c_norm, c_w_in=c_w_in, c_conv_w=c_conv_w, c_w_out=c_w_out)
    m_loc = dict(a_norm=m_a_norm, a_w_in=m_a_w_in, a_conv_w=m_a_conv_w, a_conv_b=m_a_conv_b, a_ln_g=m_a_ln_g,
                 a_ln_b=m_a_ln_b, a_w_out=m_a_w_out, b_norm=m_b_norm, b_w_in=m_b_w_in, b_f_bias=m_b_f_bias,
                 b_q_norm=m_b_q_norm, b_k_norm=m_b_k_norm, b_w_out=m_b_w_out, c_norm=m_c_norm, c_w_in=m_c_w_in,
                 c_conv_w=m_c_conv_w, c_w_out=m_c_w_out)
    v_loc = dict(a_norm=v_a_norm, a_w_in=v_a_w_in, a_conv_w=v_a_conv_w, a_conv_b=v_a_conv_b, a_ln_g=v_a_ln_g,
                 a_ln_b=v_a_ln_b, a_w_out=v_a_w_out, b_norm=v_b_norm, b_w_in=v_b_w_in, b_f_bias=v_b_f_bias,
                 b_q_norm=v_b_q_norm, b_k_norm=v_b_k_norm, b_w_out=v_b_w_out, c_norm=v_c_norm, c_w_in=v_c_w_in,
                 c_conv_w=v_c_conv_w, c_w_out=v_c_w_out)
    n_a = a_w_in.shape[0]
    d_model = x.shape[2]
    e_inner = a_w_out.shape[1] * N_CHIPS
    n_heads = b_f_bias.shape[1]
    nb_loc = b_w_in.shape[2]
    nb_pad = -(-nb_loc // LANES) * LANES
    chip = 2 * lax.axis_index("x") + lax.axis_index("y")

    big_shards = ([a_w_in[l].astype(BF16) for l in range(n_a)] + [a_w_out[l].astype(BF16) for l in range(n_a)]
                  + [_pad_cols(b_w_in[0], nb_pad).astype(BF16), b_w_out[0].astype(BF16), c_w_in[0].astype(BF16),
                     c_w_out[0].astype(BF16)])
    small_pack = _pack([w_loc[n] for n in SMALL_SHARDED])
    gathered = _gather_weights(big_shards + [small_pack], name="gather_weights")
    p = {}
    p["a_w_in"] = gathered[0:n_a]
    p["a_w_out"] = [g.reshape(e_inner, d_model) for g in gathered[n_a:2 * n_a]]
    gb, gbo, gci, gco, gsmall = gathered[2 * n_a:]
    wb_full = jnp.concatenate([gb[k, :, :nb_loc] for k in range(N_CHIPS)], axis=1)
    p["b_wq"] = wb_full[:, :4 * e_inner]
    p["b_wf"] = _pad_cols(wb_full[:, 4 * e_inner:], LANES)
    p["b_w_out"] = gbo.reshape(e_inner, d_model)
    p["c_w_in"] = gci
    p["c_w_out"] = gco.reshape(e_inner, d_model)
    small_shapes = [w_loc[n].shape for n in SMALL_SHARDED]
    per_chip = [_unpack(gsmall[k], small_shapes) for k in range(N_CHIPS)]
    for idx, n in enumerate(SMALL_SHARDED):
        p[n] = jnp.concatenate([per_chip[k][idx] for k in range(N_CHIPS)], axis=-1)
    for n in SMALL_REPLICATED:
        p[n] = w_loc[n]

    x0 = x[0]
    x1, sv0 = _mixer_a_fwd(x0, p, 0, "a0")
    x2, sv1 = _mixer_b_fwd(x1, p, "b0")
    x3, sv2 = _mixer_c_fwd(x2, p, "c0")
    x4, sv3 = _mixer_a_fwd(x3, p, 1, "a1")
    dy, loss_part = _loss_head(x4, loss_target[0], name="loss_head")
    loss = lax.psum(loss_part[0, 0], ("x", "y", "c"))

    dx, g3 = _mixer_a_bwd(dy, sv3, p, 1, "a1")
    dx, g2 = _mixer_c_bwd(dx, sv2, p, "c0")
    dx, g1 = _mixer_b_bwd(dx, sv1, p, "b0")
    dx, g0 = _mixer_a_bwd(dx, sv0, p, 0, "a0")
    grad_x = dx[None]

    half_rows = e_inner // N_CHIPS
    gb_full = g1["b_w_in"].reshape(d_model, N_CHIPS, nb_loc).transpose(1, 0, 2)
    big_grads = ([g0["a_w_in"], g3["a_w_in"]]
                 + [g0["a_w_out"].reshape(N_CHIPS, half_rows, d_model), g3["a_w_out"].reshape(N_CHIPS, half_rows, d_model)]
                 + [_pad_cols(gb_full, nb_pad), g1["b_w_out"].reshape(N_CHIPS, half_rows, d_model), g2["c_w_in"],
                    g2["c_w_out"].reshape(N_CHIPS, half_rows, d_model)])
    red, c_arr = _reduce_grads(big_grads, name="reduce_grads")
    big_order = [("a_w_in", 0), ("a_w_in", 1), ("a_w_out", 0), ("a_w_out", 1), ("b_w_in", 0), ("b_w_out", 0),
                 ("c_w_in", 0), ("c_w_out", 0)]

    grads, delta, new_m, new_v = {}, {}, {}, {}
    per_layer = {n: [] for n in BIG}
    for (n, l), (mine, other) in zip(big_order, red):
        cols = mine.shape[1]
        true_cols = w_loc[n].shape[-1]
        outs = _adamw_halves(_pad_cols(w_loc[n][l], cols), mine, other, _pad_cols(m_loc[n][l], cols),
                             _pad_cols(v_loc[n][l], cols), c_arr, name=f"adamw_{n}{l}")
        per_layer[n].append([a[:, :true_cols] for a in outs])
    for n in BIG:
        grads[n], delta[n], new_m[n], new_v[n] = [jnp.stack([layer[k] for layer in per_layer[n]]) for k in range(4)]

    small_full = {}
    for n in ("a_norm", "a_conv_w", "a_conv_b", "a_ln_g", "a_ln_b"):
        small_full[n] = jnp.stack([g0[n].reshape(p[n].shape[1:]), g3[n].reshape(p[n].shape[1:])])
    small_full["c_norm"] = g2["c_norm"].reshape(p["c_norm"].shape)
    small_full["c_conv_w"] = g2["c_conv_w"].reshape(p["c_conv_w"].shape)
    for n in SMALL_REPLICATED:
        small_full[n] = g1[n].reshape(w_loc[n].shape)
    small_names = SMALL_SHARDED + SMALL_REPLICATED
    summed = _unpack(_allreduce_small(_pack([small_full[n] for n in small_names], 8), name="reduce_small"),
                     [small_full[n].shape for n in small_names])
    for n, s in zip(small_names, summed):
        if n in SMALL_SHARDED:
            width = w_loc[n].shape[-1]
            grads[n] = lax.dynamic_slice_in_dim(s, chip * width, width, axis=s.ndim - 1)
        else:
            grads[n] = s

    small_shapes_all = [w_loc[n].shape for n in small_names]
    _, d_, m_, v_ = _adamw(_pack([w_loc[n] for n in small_names], 8), [_pack([grads[n] for n in small_names], 8)],
                           _pack([m_loc[n] for n in small_names], 8), _pack([v_loc[n] for n in small_names], 8),
                           name="adamw_small")
    for n, a, b, c_ in zip(small_names, _unpack(d_, small_shapes_all), _unpack(m_, small_shapes_all),
                           _unpack(v_, small_shapes_all)):
        delta[n], new_m[n], new_v[n] = a, b, c_

    return (loss, grad_x, *[grads[n] for n in WEIGHTS], *[delta[n] for n in WEIGHTS],
            *[new_m[n] for n in WEIGHTS], *[new_v[n] for n in WEIGHTS])
```

```python
import functools

import numpy as np
import jax
import jax.numpy as jnp
from jax import lax
from jax.experimental import pallas as pl
from jax.experimental.pallas import tpu as pltpu

F32 = jnp.float32
BF16 = jnp.bfloat16
NORM_EPS = 1e-6
HEAD_DIM = 128
LANES = 128
N_CHIPS = 4
N_DEV = 8
VMEM_CAP = 56 << 20
MESH_ID = pl.DeviceIdType.MESH

ADAM_LR = 0.001
ADAM_B1 = 0.9
ADAM_B2 = 0.999
ADAM_EPS = 1e-08
ADAM_WD = 0.01
ADAM_STEP = 10


def _sds(shape, dtype):
    return jax.ShapeDtypeStruct(tuple(shape), dtype)


def _tile(n, pref):
    if n <= pref:
        return n
    for t in range(pref - pref % 8, 0, -8):
        if n % t == 0:
            return t
    raise ValueError(f"no tile for {n} under {pref}")


def _params(vmem_bytes):
    return pltpu.CompilerParams(vmem_limit_bytes=int(min(max(vmem_bytes, 16 << 20), VMEM_CAP)))


def _sigmoid(v):
    return 1.0 / (1.0 + jnp.exp(-v))


def _silu(v):
    return v * _sigmoid(v)


def _dsilu(v):
    s = _sigmoid(v)
    return s * (1.0 + v * (1.0 - s))


def _rowsum8(v):
    r, c = v.shape
    return jnp.sum(v.reshape(r // 8, 8, c), axis=0)


def _col_spec(grouped, rows_block, tile, width, row_of, col_of):
    if grouped:
        per = width // tile
        return pl.BlockSpec((None, rows_block, tile), lambda *ids: (col_of(*ids) // per, row_of(*ids), col_of(*ids) % per))
    return pl.BlockSpec((rows_block, tile), lambda *ids: (row_of(*ids), col_of(*ids)))


def _mm_nn(a, b, *, name, tm=1024, tn=512, out_dtype=F32, add=None):
    m, k = a.shape
    b_grouped = b.ndim == 3
    n = b.shape[0] * b.shape[2] if b_grouped else b.shape[1]
    width = b.shape[2] if b_grouped else n
    tm, tn = _tile(m, tm), _tile(width, tn)

    def body(*refs):
        if add is None:
            a_ref, b_ref, o_ref = refs
        else:
            a_ref, b_ref, r_ref, o_ref = refs
        acc = jnp.dot(a_ref[...].astype(BF16), b_ref[...].astype(BF16), preferred_element_type=F32)
        if add is not None:
            acc = acc + r_ref[...]
        o_ref[...] = acc.astype(o_ref.dtype)

    in_specs = [pl.BlockSpec((tm, k), lambda i, j: (i, 0)),
                _col_spec(b_grouped, k, tn, width, lambda i, j: 0, lambda i, j: j)]
    args = [a, b]
    if add is not None:
        in_specs.append(pl.BlockSpec((tm, tn), lambda i, j: (i, j)))
        args.append(add)
    vmem = 2 * (tm * k * a.dtype.itemsize + k * tn * 2 + tm * tn * 4 * (2 if add is not None else 1)) + tm * tn * 8 + tm * k * 2
    return pl.pallas_call(
        body, name=name, grid=(m // tm, n // tn), in_specs=in_specs,
        out_specs=pl.BlockSpec((tm, tn), lambda i, j: (i, j)),
        out_shape=_sds((m, n), out_dtype), compiler_params=_params(vmem + (4 << 20)),
    )(*args)


def _mm_nt(a, b, *, name, tm=1024, tn=1024, tk=512, ksub=1):
    a_grouped, b_grouped = a.ndim == 3, b.ndim == 3
    m = a.shape[1] if a_grouped else a.shape[0]
    n = a.shape[0] * a.shape[2] if a_grouped else a.shape[1]
    kk = b.shape[1] if b_grouped else b.shape[0]
    wa = a.shape[2] if a_grouped else n
    wb = b.shape[2] if b_grouped else n
    tm, tn = _tile(m, tm), _tile(kk, tn)
    tk = _tile(int(np.gcd(wa, wb)), tk)
    ksub = min(ksub, n // tk)
    assert (n // tk) % ksub == 0, (n, tk, ksub)
    steps = n // (tk * ksub)

    def body(*refs):
        a_refs, b_refs, o_ref = refs[:ksub], refs[ksub:2 * ksub], refs[2 * ksub]
        part = None
        for a_ref, b_ref in zip(a_refs, b_refs):
            d = lax.dot_general(a_ref[...].astype(BF16), b_ref[...].astype(BF16), (((1,), (1,)), ((), ())),
                                preferred_element_type=F32)
            part = d if part is None else part + d

        @pl.when(pl.program_id(2) == 0)
        def _():
            o_ref[...] = part

        @pl.when(pl.program_id(2) > 0)
        def _():
            o_ref[...] += part

    def sub(u):
        return lambda i, j, s: s * ksub + u

    in_specs = ([_col_spec(a_grouped, tm, tk, wa, lambda i, j, s: i, sub(u)) for u in range(ksub)]
                + [_col_spec(b_grouped, tn, tk, wb, lambda i, j, s: j, sub(u)) for u in range(ksub)])
    vmem = (2 * ksub * (tm * tk * a.dtype.itemsize + tn * tk * b.dtype.itemsize) + 2 * tm * tn * 4 + 2 * tm * tn * 4
            + (tm + tn) * tk * 2)
    return pl.pallas_call(
        body, name=name, grid=(m // tm, kk // tn, steps), in_specs=in_specs,
        out_specs=pl.BlockSpec((tm, tn), lambda i, j, s: (i, j)),
        out_shape=_sds((m, kk), F32), compiler_params=_params(vmem + (4 << 20)),
    )(*([a] * ksub), *([b] * ksub))


def _mm_tn(a, b, *, name, out_width=None, out_dtype=F32, tk=1024, tn=512, ts=512):
    s_len, k = a.shape
    b_grouped = b.ndim == 3
    n = b.shape[0] * b.shape[2] if b_grouped else b.shape[1]
    wb = b.shape[2] if b_grouped else n
    wo = out_width if out_width is not None else n
    tk, ts = _tile(k, tk), _tile(s_len, ts)
    tn = _tile(int(np.gcd(wb, wo)), tn)
    last = s_len // ts - 1
    direct = out_dtype == F32

    def body(a_ref, b_ref, o_ref, *scratch):
        acc = o_ref if direct else scratch[0]
        part = lax.dot_general(a_ref[...].astype(BF16), b_ref[...].astype(BF16), (((0,), (0,)), ((), ())),
                               preferred_element_type=F32)

        @pl.when(pl.program_id(2) == 0)
        def _():
            acc[...] = part

        @pl.when(pl.program_id(2) > 0)
        def _():
            acc[...] += part

        if not direct:
            @pl.when(pl.program_id(2) == last)
            def _():
                o_ref[...] = acc[...].astype(o_ref.dtype)

    in_specs = [pl.BlockSpec((ts, tk), lambda i, j, s: (s, i)),
                _col_spec(b_grouped, ts, tn, wb, lambda i, j, s: s, lambda i, j, s: j)]
    out_grouped = out_width is not None
    out_spec = _col_spec(out_grouped, tk, tn, wo, lambda i, j, s: i, lambda i, j, s: j)
    out_shape = _sds((n // wo, k, wo), out_dtype) if out_grouped else _sds((k, n), out_dtype)
    vmem = 2 * (ts * tk * a.dtype.itemsize + ts * tn * b.dtype.itemsize + tk * tn * 4) + 2 * tk * tn * 4 + ts * (tk + tn) * 4
    return pl.pallas_call(
        body, name=name, grid=(k // tk, n // tn, s_len // ts), in_specs=in_specs, out_specs=out_spec,
        out_shape=out_shape, scratch_shapes=[] if direct else [pltpu.VMEM((tk, tn), F32)],
        compiler_params=_params(vmem + (4 << 20)),
    )(a, b)


def _rms_fwd(x, g, *, name, ts=512):
    s_len, d = x.shape
    ts = _tile(s_len, ts)

    def body(x_ref, g_ref, h_ref):
        xf = x_ref[...]
        r = lax.rsqrt(jnp.mean(xf * xf, axis=-1, keepdims=True) + NORM_EPS)
        h_ref[...] = ((xf * r) * g_ref[...]).astype(h_ref.dtype)

    return pl.pallas_call(
        body, name=name, grid=(s_len // ts,),
        in_specs=[pl.BlockSpec((ts, d), lambda i: (i, 0)), pl.BlockSpec((1, d), lambda i: (0, 0))],
        out_specs=pl.BlockSpec((ts, d), lambda i: (i, 0)), out_shape=_sds((s_len, d), BF16),
        compiler_params=_params(8 * ts * d * 4),
    )(x, g)


def _rms_bwd(x, g, dhs, dres, *, name, ts=512):
    s_len, d = x.shape
    ts = _tile(s_len, ts)
    n_dh = len(dhs)
    last = s_len // ts - 1

    def body(*refs):
        x_ref, g_ref = refs[0], refs[1]
        dh_refs = refs[2:2 + n_dh]
        dres_ref, dx_ref, dg_ref, acc = refs[2 + n_dh:]
        i = pl.program_id(0)

        @pl.when(i == 0)
        def _():
            acc[...] = jnp.zeros_like(acc)

        xf = x_ref[...]
        dy = dh_refs[0][...]
        for r_ in dh_refs[1:]:
            dy = dy + r_[...]
        r = lax.rsqrt(jnp.mean(xf * xf, axis=-1, keepdims=True) + NORM_EPS)
        gd = dy * g_ref[...]
        dot = jnp.mean(xf * gd, axis=-1, keepdims=True)
        dx_ref[...] = dres_ref[...] + r * gd - xf * (r * r * r * dot)
        acc[...] += _rowsum8(dy * (xf * r))

        @pl.when(i == last)
        def _():
            dg_ref[...] = jnp.sum(acc[...], axis=0, keepdims=True)

    row = pl.BlockSpec((ts, d), lambda i: (i, 0))
    vec = pl.BlockSpec((1, d), lambda i: (0, 0))
    return pl.pallas_call(
        body, name=name, grid=(s_len // ts,),
        in_specs=[row, vec] + [row] * n_dh + [row],
        out_specs=[row, vec], out_shape=[_sds((s_len, d), F32), _sds((1, d), F32)],
        scratch_shapes=[pltpu.VMEM((8, d), F32)],
        compiler_params=_params((2 * (3 + n_dh) + 6) * ts * d * 4),
    )(x, g, *dhs, dres)


def _loss_head(y, target, *, name, ts=512):
    s_len, d = y.shape
    ts = _tile(s_len, ts)
    last = s_len // ts - 1

    def body(y_ref, t_ref, dy_ref, loss_ref, acc):
        i = pl.program_id(0)

        @pl.when(i == 0)
        def _():
            acc[...] = jnp.zeros_like(acc)

        err = y_ref[...] - t_ref[...]
        dy_ref[...] = err / d
        acc[...] += _rowsum8(err * err)

        @pl.when(i == last)
        def _():
            loss_ref[...] = (0.5 * jnp.sum(acc[...]) / d).reshape(1, 1)

    row = pl.BlockSpec((ts, d), lambda i: (i, 0))
    return pl.pallas_call(
        body, name=name, grid=(s_len // ts,), in_specs=[row, row],
        out_specs=[row, pl.BlockSpec((1, 1), lambda i: (0, 0))],
        out_shape=[_sds((s_len, d), F32), _sds((1, 1), F32)],
        scratch_shapes=[pltpu.VMEM((8, d), F32)],
        compiler_params=_params(10 * ts * d * 4),
    )(y, target)


CONV_ROWS = 64
CONV_COLS = 256


SUBLANES = 8


def _fill_shifts(buf, sh_ref):
    n = sh_ref.shape[1]
    for s in range(1, SUBLANES):
        sh_ref[s - 1, :, :] = buf[s:s + n, :]


def _shifted_rows(buf, sh_ref, start, rw, cs):
    s = start % SUBLANES
    if sh_ref is None or s == 0:
        return buf[start:start + rw, cs]
    return sh_ref[s - 1, start - s:start - s + rw, cs]


def _conv_taps(buf, sh_ref, w_ref, k_width, base, ts, tc, init, emit, reverse=False):
    cw = min(tc, CONV_COLS)
    rw = min(ts, CONV_ROWS)
    for cb in range(tc // cw):
        cs = slice(cb * cw, (cb + 1) * cw)
        for rb in range(ts // rw):
            acc = init(slice(rb * rw, (rb + 1) * rw), cs, (rw, cw))
            for k in range(k_width):
                sh = (k_width - 1 - k) if reverse else k
                acc = acc + w_ref[k:k + 1, cs] * _shifted_rows(buf, sh_ref, base + rb * rw + sh, rw, cs)
            emit(slice(rb * rw, (rb + 1) * rw), cs, acc)


def _conv_wgrad(buf, sh_ref, d_ref_val, acc_ref, k_width, base, ts, tc):
    cw = min(tc, CONV_COLS)
    rw = min(ts, CONV_ROWS)
    for cb in range(tc // cw):
        cs = slice(cb * cw, (cb + 1) * cw)
        for rb in range(ts // rw):
            dv = d_ref_val[rb * rw:(rb + 1) * rw, cs]
            for k in range(k_width):
                prod = dv * _shifted_rows(buf, sh_ref, base + rb * rw + k, rw, cs)
                acc_ref[8 * k:8 * k + 8, cs] += _rowsum8(prod)


A_HALO = 32


def _a_conv_fwd(proj, conv_w, conv_b, *, name, ts=256, tc=512):
    s_len, e3 = proj.shape
    e = e3 // 3
    k_width = conv_w.shape[0]
    ts, tc = _tile(s_len, ts), _tile(e, tc)
    nc = e // tc
    kp = 32

    def body(val, gate, valh, gateh, w_ref, b_ref, u1_ref, buf, sh):
        i = pl.program_id(1)
        u0h = valh[...] * _sigmoid(gateh[...])
        buf[0:A_HALO, :] = jnp.where(i > 0, u0h, 0.0)
        buf[A_HALO:A_HALO + ts, :] = val[...] * _sigmoid(gate[...])
        _fill_shifts(buf, sh)

        def init(rows, cs, shape):
            return jnp.broadcast_to(b_ref[:, cs], shape)

        def emit(rows, cs, acc):
            u1_ref[rows, cs] = acc

        _conv_taps(buf, sh, w_ref, k_width, A_HALO - (k_width - 1), ts, tc, init, emit)

    hb = ts // A_HALO
    in_specs = [
        pl.BlockSpec((ts, tc), lambda j, i: (i, j)),
        pl.BlockSpec((ts, tc), lambda j, i: (i, nc + j)),
        pl.BlockSpec((A_HALO, tc), lambda j, i: (jnp.maximum(i * hb - 1, 0), j)),
        pl.BlockSpec((A_HALO, tc), lambda j, i: (jnp.maximum(i * hb - 1, 0), nc + j)),
        pl.BlockSpec((kp, tc), lambda j, i: (0, j)),
        pl.BlockSpec((1, tc), lambda j, i: (0, j)),
    ]
    w_pad = jnp.zeros((kp, e), F32).at[:k_width].set(conv_w)
    return pl.pallas_call(
        body, name=name, grid=(nc, s_len // ts), in_specs=in_specs,
        out_specs=pl.BlockSpec((ts, tc), lambda j, i: (i, j)), out_shape=_sds((s_len, e), F32),
        scratch_shapes=[pltpu.VMEM((A_HALO + ts, tc), F32), pltpu.VMEM((SUBLANES - 1, A_HALO + ts - SUBLANES, tc), F32)],
        compiler_params=_params(24 * ts * tc * 4),
    )(proj, proj, proj, proj, w_pad, conv_b)


def _ln_rows(u1, g, b):
    mu = jnp.mean(u1, axis=-1, keepdims=True)
    xc = u1 - mu
    var = jnp.mean(xc * xc, axis=-1, keepdims=True)
    rstd = lax.rsqrt(var + NORM_EPS)
    xhat = xc * rstd
    return xhat, rstd, xhat * g + b


def _a_post_fwd(u1, proj, ln_g, ln_b, *, name, ts=256):
    s_len, e = u1.shape
    ts = _tile(s_len, ts)

    def body(u1_ref, z_ref, g_ref, b_ref, o_ref):
        _, _, u2 = _ln_rows(u1_ref[...], g_ref[...], b_ref[...])
        o_ref[...] = (_silu(u2) * _silu(z_ref[...])).astype(o_ref.dtype)

    row = pl.BlockSpec((ts, e), lambda i: (i, 0))
    vec = pl.BlockSpec((1, e), lambda i: (0, 0))
    return pl.pallas_call(
        body, name=name, grid=(s_len // ts,),
        in_specs=[row, pl.BlockSpec((ts, e), lambda i: (i, 2)), vec, vec],
        out_specs=row, out_shape=_sds((s_len, e), BF16), compiler_params=_params(12 * ts * e * 4),
    )(u1, proj, ln_g, ln_b)


def _a_post_bwd(dgated, u1, proj, ln_g, ln_b, *, name, ts=256):
    s_len, e = u1.shape
    ts = _tile(s_len, ts)
    last = s_len // ts - 1

    def body(dg_ref, u1_ref, z_ref, g_ref, b_ref, du1_ref, dz_ref, dlg_ref, dlb_ref, dcb_ref, a_g, a_b, a_c):
        i = pl.program_id(0)

        @pl.when(i == 0)
        def _():
            a_g[...] = jnp.zeros_like(a_g)
            a_b[...] = jnp.zeros_like(a_b)
            a_c[...] = jnp.zeros_like(a_c)

        g = g_ref[...]
        xhat, rstd, u2 = _ln_rows(u1_ref[...], g, b_ref[...])
        z = z_ref[...]
        dgt = dg_ref[...]
        dz_ref[...] = (dgt * _silu(u2) * _dsilu(z)).astype(dz_ref.dtype)
        du2 = dgt * _silu(z) * _dsilu(u2)
        a_g[...] += _rowsum8(du2 * xhat)
        a_b[...] += _rowsum8(du2)
        dxh = du2 * g
        m1 = jnp.mean(dxh, axis=-1, keepdims=True)
        m2 = jnp.mean(dxh * xhat, axis=-1, keepdims=True)
        du1 = rstd * (dxh - m1 - xhat * m2)
        du1_ref[...] = du1
        a_c[...] += _rowsum8(du1)

        @pl.when(i == last)
        def _():
            dlg_ref[...] = jnp.sum(a_g[...], axis=0, keepdims=True)
            dlb_ref[...] = jnp.sum(a_b[...], axis=0, keepdims=True)
            dcb_ref[...] = jnp.sum(a_c[...], axis=0, keepdims=True)

    row = pl.BlockSpec((ts, e), lambda i: (i, 0))
    vec = pl.BlockSpec((1, e), lambda i: (0, 0))
    return pl.pallas_call(
        body, name=name, grid=(s_len // ts,),
        in_specs=[row, row, pl.BlockSpec((ts, e), lambda i: (i, 2)), vec, vec],
        out_specs=[row, row, vec, vec, vec],
        out_shape=[_sds((s_len, e), F32), _sds((s_len, e), BF16), _sds((1, e), F32), _sds((1, e), F32), _sds((1, e), F32)],
        scratch_shapes=[pltpu.VMEM((8, e), F32)] * 3,
        compiler_params=_params(20 * ts * e * 4),
    )(dgated, u1, proj, ln_g, ln_b)


def _a_conv_bwd(du1, proj, dz, conv_w, *, name, ts=256, tc=512):
    s_len, e3 = proj.shape
    e = e3 // 3
    k_width = conv_w.shape[0]
    ts, tc = _tile(s_len, ts), _tile(e, tc)
    nc, nr = e // tc, s_len // ts
    kp = 32
    hb = ts // A_HALO

    def body(val, gate, valh, gateh, d_ref, dh_ref, dz_ref, w_ref, dp_ref, dw_ref, buf_u, buf_d, du0, acc, sh_u, sh_d):
        i = pl.program_id(1)

        @pl.when(i == 0)
        def _():
            acc[...] = jnp.zeros_like(acc)

        buf_u[0:A_HALO, :] = jnp.where(i > 0, valh[...] * _sigmoid(gateh[...]), 0.0)
        buf_u[A_HALO:A_HALO + ts, :] = val[...] * _sigmoid(gate[...])
        buf_d[0:ts, :] = d_ref[...]
        buf_d[ts:ts + A_HALO, :] = jnp.where(i < nr - 1, dh_ref[...], 0.0)
        _fill_shifts(buf_u, sh_u)
        _fill_shifts(buf_d, sh_d)

        def init(rows, cs, shape):
            return jnp.zeros(shape, F32)

        def emit(rows, cs, a):
            du0[rows, cs] = a

        _conv_taps(buf_d, sh_d, w_ref, k_width, 0, ts, tc, init, emit, reverse=True)
        _conv_wgrad(buf_u, sh_u, buf_d, acc, k_width, A_HALO - (k_width - 1), ts, tc)
        rw = min(ts, CONV_ROWS)
        for rb in range(ts // rw):
            rows = slice(rb * rw, (rb + 1) * rw)
            d0 = du0[rows, :]
            sg = _sigmoid(gate[rows, :])
            dp_ref[0, rows, :] = (d0 * sg).astype(dp_ref.dtype)
            dp_ref[1, rows, :] = (d0 * val[rows, :] * sg * (1.0 - sg)).astype(dp_ref.dtype)
        dp_ref[2] = dz_ref[...]

        @pl.when(i == nr - 1)
        def _():
            for k in range(kp):
                dw_ref[k:k + 1, :] = jnp.sum(acc[8 * k:8 * k + 8, :], axis=0, keepdims=True)

    in_specs = [
        pl.BlockSpec((ts, tc), lambda j, i: (i, j)),
        pl.BlockSpec((ts, tc), lambda j, i: (i, nc + j)),
        pl.BlockSpec((A_HALO, tc), lambda j, i: (jnp.maximum(i * hb - 1, 0), j)),
        pl.BlockSpec((A_HALO, tc), lambda j, i: (jnp.maximum(i * hb - 1, 0), nc + j)),
        pl.BlockSpec((ts, tc), lambda j, i: (i, j)),
        pl.BlockSpec((A_HALO, tc), lambda j, i: (jnp.minimum((i + 1) * hb, nr * hb - 1), j)),
        pl.BlockSpec((ts, tc), lambda j, i: (i, j)),
        pl.BlockSpec((kp, tc), lambda j, i: (0, j)),
    ]
    w_pad = jnp.zeros((kp, e), F32).at[:k_width].set(conv_w)
    dproj, dw = pl.pallas_call(
        body, name=name, grid=(nc, nr), in_specs=in_specs,
        out_specs=[pl.BlockSpec((3, ts, tc), lambda j, i: (0, i, j)), pl.BlockSpec((kp, tc), lambda j, i: (0, j))],
        out_shape=[_sds((3, s_len, e), BF16), _sds((kp, e), F32)],
        scratch_shapes=[pltpu.VMEM((A_HALO + ts, tc), F32), pltpu.VMEM((ts + A_HALO, tc), F32),
                        pltpu.VMEM((ts, tc), F32), pltpu.VMEM((8 * kp, tc), F32),
                        pltpu.VMEM((SUBLANES - 1, A_HALO + ts - SUBLANES, tc), F32),
                        pltpu.VMEM((SUBLANES - 1, A_HALO + ts - SUBLANES, tc), F32)],
        compiler_params=_params(56 * ts * tc * 4),
    )(proj, proj, proj, proj, du1, du1, dz, w_pad)
    return dproj, dw[:k_width]


C_HALO = 8


def _c_mid_fwd(proj, conv_w, *, name, ts=256, tc=512):
    s_len, e4 = proj.shape
    e = e4 // 4
    k_width = conv_w.shape[0]
    ts, tc = _tile(s_len, ts), _tile(e, tc)
    nc = e // tc
    hb = ts // C_HALO

    def body(u, bg, cg, z, uh, cgh, w_ref, o_ref, buf, y):
        i = pl.program_id(1)
        buf[0:C_HALO, :] = jnp.where(i > 0, uh[...] * cgh[...], 0.0)
        buf[C_HALO:C_HALO + ts, :] = u[...] * cg[...]

        def init(rows, cs, shape):
            return jnp.zeros(shape, F32)

        def emit(rows, cs, a):
            y[rows, cs] = a

        _conv_taps(buf, None, w_ref, k_width, C_HALO - (k_width - 1), ts, tc, init, emit)
        o_ref[...] = (bg[...] * y[...] * _silu(z[...])).astype(o_ref.dtype)

    def grp(g):
        return pl.BlockSpec((ts, tc), lambda j, i: (i, g * nc + j))

    def halo(g):
        return pl.BlockSpec((C_HALO, tc), lambda j, i: (jnp.maximum(i * hb - 1, 0), g * nc + j))

    w_pad = jnp.zeros((8, e), F32).at[:k_width].set(conv_w)
    return pl.pallas_call(
        body, name=name, grid=(nc, s_len // ts),
        in_specs=[grp(0), grp(1), grp(2), grp(3), halo(0), halo(2), pl.BlockSpec((8, tc), lambda j, i: (0, j))],
        out_specs=pl.BlockSpec((ts, tc), lambda j, i: (i, j)), out_shape=_sds((s_len, e), BF16),
        scratch_shapes=[pltpu.VMEM((C_HALO + ts, tc), F32), pltpu.VMEM((ts, tc), F32)],
        compiler_params=_params(16 * ts * tc * 4),
    )(proj, proj, proj, proj, proj, proj, w_pad)


def _c_mid_bwd(dgated, proj, conv_w, *, name, ts=256, tc=512):
    s_len, e4 = proj.shape
    e = e4 // 4
    k_width = conv_w.shape[0]
    ts, tc = _tile(s_len, ts), _tile(e, tc)
    nc, nr = e // tc, s_len // ts
    hb = ts // C_HALO

    def body(u, bg, cg, z, uh, cgh, dg, dgh, bgh, zh, w_ref, dp_ref, dw_ref, buf_p, buf_d, y, dpv, acc):
        i = pl.program_id(1)

        @pl.when(i == 0)
        def _():
            acc[...] = jnp.zeros_like(acc)

        uv, bgv, cgv, zv, dgv = u[...], bg[...], cg[...], z[...], dg[...]
        buf_p[0:C_HALO, :] = jnp.where(i > 0, uh[...] * cgh[...], 0.0)
        buf_p[C_HALO:C_HALO + ts, :] = uv * cgv
        sz = _silu(zv)
        buf_d[0:ts, :] = dgv * sz * bgv
        buf_d[ts:ts + C_HALO, :] = jnp.where(i < nr - 1, dgh[...] * _silu(zh[...]) * bgh[...], 0.0)

        def init(rows, cs, shape):
            return jnp.zeros(shape, F32)

        def emit_y(rows, cs, a):
            y[rows, cs] = a

        def emit_dp(rows, cs, a):
            dpv[rows, cs] = a

        _conv_taps(buf_p, None, w_ref, k_width, C_HALO - (k_width - 1), ts, tc, init, emit_y)
        _conv_taps(buf_d, None, w_ref, k_width, 0, ts, tc, init, emit_dp, reverse=True)
        _conv_wgrad(buf_p, None, buf_d, acc, k_width, C_HALO - (k_width - 1), ts, tc)
        yv, dp = y[...], dpv[...]
        dp_ref[0] = (dp * cgv).astype(dp_ref.dtype)
        dp_ref[1] = (dgv * sz * yv).astype(dp_ref.dtype)
        dp_ref[2] = (dp * uv).astype(dp_ref.dtype)
        dp_ref[3] = (dgv * bgv * yv * _dsilu(zv)).astype(dp_ref.dtype)

        @pl.when(i == nr - 1)
        def _():
            for k in range(8):
                dw_ref[k:k + 1, :] = jnp.sum(acc[8 * k:8 * k + 8, :], axis=0, keepdims=True)

    def grp(g):
        return pl.BlockSpec((ts, tc), lambda j, i: (i, g * nc + j))

    def prev(g):
        return pl.BlockSpec((C_HALO, tc), lambda j, i: (jnp.maximum(i * hb - 1, 0), g * nc + j))

    def nxt(g):
        return pl.BlockSpec((C_HALO, tc), lambda j, i: (jnp.minimum((i + 1) * hb, nr * hb - 1), g * nc + j))

    w_pad = jnp.zeros((8, e), F32).at[:k_width].set(conv_w)
    dproj, dw = pl.pallas_call(
        body, name=name, grid=(nc, nr),
        in_specs=[grp(0), grp(1), grp(2), grp(3), prev(0), prev(2),
                  pl.BlockSpec((ts, tc), lambda j, i: (i, j)),
                  pl.BlockSpec((C_HALO, tc), lambda j, i: (jnp.minimum((i + 1) * hb, nr * hb - 1), j)),
                  nxt(1), nxt(3), pl.BlockSpec((8, tc), lambda j, i: (0, j))],
        out_specs=[pl.BlockSpec((4, ts, tc), lambda j, i: (0, i, j)), pl.BlockSpec((8, tc), lambda j, i: (0, j))],
        out_shape=[_sds((4, s_len, e), BF16), _sds((8, e), F32)],
        scratch_shapes=[pltpu.VMEM((C_HALO + ts, tc), F32), pltpu.VMEM((ts + C_HALO, tc), F32),
                        pltpu.VMEM((ts, tc), F32), pltpu.VMEM((ts, tc), F32), pltpu.VMEM((64, tc), F32)],
        compiler_params=_params(32 * ts * tc * 4),
    )(proj, proj, proj, proj, proj, proj, dgated, dgated, proj, proj, w_pad)
    return dproj, dw[:k_width]


def _head_rms(xv, g):
    r = lax.rsqrt(jnp.mean(xv * xv, axis=-1, keepdims=True) + NORM_EPS)
    return r, xv * r * g


def _b_qk_fwd(proj, gq, gk, *, name, ts=512, tc=512):
    s_len, e4 = proj.shape
    e = e4 // 4
    ts, tc = _tile(s_len, ts), _tile(e, tc)
    nc = e // tc

    def body(q, k, v, gq_ref, gk_ref, qn, kn, vb):
        for h in range(tc // HEAD_DIM):
            cs = slice(h * HEAD_DIM, (h + 1) * HEAD_DIM)
            qn[:, cs] = _head_rms(q[:, cs], gq_ref[...])[1].astype(qn.dtype)
            kn[:, cs] = _head_rms(k[:, cs], gk_ref[...])[1].astype(kn.dtype)
        vb[...] = v[...].astype(vb.dtype)

    def grp(g):
        return pl.BlockSpec((ts, tc), lambda i, j: (i, g * nc + j))

    vec = pl.BlockSpec((1, HEAD_DIM), lambda i, j: (0, 0))
    out = pl.BlockSpec((ts, tc), lambda i, j: (i, j))
    return pl.pallas_call(
        body, name=name, grid=(s_len // ts, nc), in_specs=[grp(0), grp(1), grp(2), vec, vec],
        out_specs=[out, out, out], out_shape=[_sds((s_len, e), BF16)] * 3,
        compiler_params=_params(16 * ts * tc * 4),
    )(proj, proj, proj, gq, gk)


def _b_qk_bwd(dqn, dkn, dv, dz, proj, gq, gk, *, name, ts=512, tc=512):
    s_len, e4 = proj.shape
    e = e4 // 4
    ts, tc = _tile(s_len, ts), _tile(e, tc)
    nc, nr = e // tc, s_len // ts

    def body(dq_ref, dk_ref, dv_ref, dz_ref, q, k, gq_ref, gk_ref, dp_ref, dgq_ref, dgk_ref, a_q, a_k):
        i, j = pl.program_id(0), pl.program_id(1)

        @pl.when((i == 0) & (j == 0))
        def _():
            a_q[...] = jnp.zeros_like(a_q)
            a_k[...] = jnp.zeros_like(a_k)

        for h in range(tc // HEAD_DIM):
            cs = slice(h * HEAD_DIM, (h + 1) * HEAD_DIM)
            for slot, src, d_ref, g_ref, acc in ((0, q, dq_ref, gq_ref, a_q), (1, k, dk_ref, gk_ref, a_k)):
                xv = src[:, cs]
                dy = d_ref[:, cs]
                r = lax.rsqrt(jnp.mean(xv * xv, axis=-1, keepdims=True) + NORM_EPS)
                gd = dy * g_ref[...]
                dot = jnp.mean(xv * gd, axis=-1, keepdims=True)
                dp_ref[slot, :, cs] = (r * gd - xv * (r * r * r * dot)).astype(dp_ref.dtype)
                acc[...] += _rowsum8(dy * (xv * r))
        dp_ref[2] = dv_ref[...].astype(dp_ref.dtype)
        dp_ref[3] = dz_ref[...]

        @pl.when((i == nr - 1) & (j == nc - 1))
        def _():
            dgq_ref[...] = jnp.sum(a_q[...], axis=0, keepdims=True)
            dgk_ref[...] = jnp.sum(a_k[...], axis=0, keepdims=True)

    blk = pl.BlockSpec((ts, tc), lambda i, j: (i, j))
    vec = pl.BlockSpec((1, HEAD_DIM), lambda i, j: (0, 0))

    def grp(g):
        return pl.BlockSpec((ts, tc), lambda i, j: (i, g * nc + j))

    return pl.pallas_call(
        body, name=name, grid=(nr, nc), in_specs=[blk, blk, blk, blk, grp(0), grp(1), vec, vec],
        out_specs=[pl.BlockSpec((4, ts, tc), lambda i, j: (0, i, j)), vec, vec],
        out_shape=[_sds((4, s_len, e), BF16), _sds((1, HEAD_DIM), F32), _sds((1, HEAD_DIM), F32)],
        scratch_shapes=[pltpu.VMEM((8, HEAD_DIM), F32)] * 2,
        compiler_params=_params(24 * ts * tc * 4),
    )(dqn, dkn, dv, dz, proj, proj, gq, gk)


def _log_sigmoid(x):
    y = jnp.exp(-jnp.abs(x))
    u = 1.0 + y
    log1p = jnp.where(u == 1.0, y, jnp.log(u) * (y / jnp.where(u == 1.0, 1.0, u - 1.0)))
    return jnp.minimum(x, 0.0) - log1p


def _split3(v):
    hi = v.astype(BF16)
    r1 = v - hi.astype(F32)
    mid = r1.astype(BF16)
    lo = (r1 - mid.astype(F32)).astype(BF16)
    return hi, mid, lo


def _tri_matmul(tri, v):
    hi, mid, lo = _split3(v)
    return (jnp.dot(tri, hi, preferred_element_type=F32) + jnp.dot(tri, mid, preferred_element_type=F32)
            + jnp.dot(tri, lo, preferred_element_type=F32))


def _b_cumsum(fl, bias, *, name, t=512):
    s_len, w = fl.shape
    t = _tile(s_len, t)

    def body(fl_ref, b_ref, c_ref, carry):
        @pl.when(pl.program_id(0) == 0)
        def _():
            carry[...] = jnp.zeros_like(carry)

        logf = _log_sigmoid(fl_ref[...] + b_ref[...])
        row = lax.broadcasted_iota(jnp.int32, (t, t), 0)
        col = lax.broadcasted_iota(jnp.int32, (t, t), 1)
        tri = jnp.where(col <= row, 1.0, 0.0).astype(BF16)
        c = _tri_matmul(tri, logf) + carry[...]
        c_ref[...] = c
        carry[...] = c[t - 1:t, :]

    return pl.pallas_call(
        body, name=name, grid=(s_len // t,),
        in_specs=[pl.BlockSpec((t, w), lambda i: (i, 0)), pl.BlockSpec((1, w), lambda i: (0, 0))],
        out_specs=pl.BlockSpec((t, w), lambda i: (i, 0)), out_shape=_sds((s_len, w), F32),
        scratch_shapes=[pltpu.VMEM((1, w), F32)], compiler_params=_params(16 << 20),
    )(fl, bias)


def _b_cumsum_bwd(dc, fl, bias, *, name, t=512):
    s_len, w = fl.shape
    t = _tile(s_len, t)
    n = s_len // t

    def body(dc_ref, fl_ref, b_ref, dfl_ref, db_ref, carry, acc):
        i = pl.program_id(0)

        @pl.when(i == 0)
        def _():
            carry[...] = jnp.zeros_like(carry)
            acc[...] = jnp.zeros_like(acc)

        row = lax.broadcasted_iota(jnp.int32, (t, t), 0)
        col = lax.broadcasted_iota(jnp.int32, (t, t), 1)
        tri = jnp.where(col >= row, 1.0, 0.0).astype(BF16)
        dlogf = _tri_matmul(tri, dc_ref[...]) + carry[...]
        carry[...] = dlogf[0:1, :]
        dfl = dlogf * _sigmoid(-(fl_ref[...] + b_ref[...]))
        dfl_ref[...] = dfl
        acc[...] += _rowsum8(dfl)

        @pl.when(i == n - 1)
        def _():
            db_ref[...] = jnp.sum(acc[...], axis=0, keepdims=True)

    rev = pl.BlockSpec((t, w), lambda i: (n - 1 - i, 0))
    vec = pl.BlockSpec((1, w), lambda i: (0, 0))
    return pl.pallas_call(
        body, name=name, grid=(n,), in_specs=[rev, rev, vec], out_specs=[rev, vec],
        out_shape=[_sds((s_len, w), F32), _sds((1, w), F32)],
        scratch_shapes=[pltpu.VMEM((1, w), F32), pltpu.VMEM((8, w), F32)], compiler_params=_params(16 << 20),
    )(dc, fl, bias)


def _b_gate_fwd(o, proj, *, name, ts=512, tc=512):
    s_len, e = o.shape
    ts, tc = _tile(s_len, ts), _tile(e, tc)
    nc = e // tc

    def body(o_ref, z_ref, g_ref):
        g_ref[...] = (o_ref[...] * _silu(z_ref[...])).astype(g_ref.dtype)

    blk = pl.BlockSpec((ts, tc), lambda i, j: (i, j))
    return pl.pallas_call(
        body, name=name, grid=(s_len // ts, nc),
        in_specs=[blk, pl.BlockSpec((ts, tc), lambda i, j: (i, 3 * nc + j))], out_specs=blk,
        out_shape=_sds((s_len, e), BF16), compiler_params=_params(12 * ts * tc * 4),
    )(o, proj)


def _b_gate_bwd(dgated, o, o_lo, proj, *, name, hg, ts=512):
    s_len, e = o.shape
    ts = _tile(s_len, ts)
    w = hg * HEAD_DIM
    ng = e // w

    def body(dg_ref, o_ref, olo_ref, z_ref, do_ref, dz_ref, dl_ref):
        dgt, ov, zv = dg_ref[...], o_ref[...], z_ref[...]
        dob = (dgt * _silu(zv)).astype(do_ref.dtype)
        do_ref[...] = dob
        dz_ref[...] = (dgt * ov * _dsilu(zv)).astype(dz_ref.dtype)
        prod = dob.astype(F32) * (ov + olo_ref[...])
        for hh in range(hg):
            cs = slice(hh * HEAD_DIM, (hh + 1) * HEAD_DIM)
            dl_ref[:, cs] = jnp.broadcast_to(jnp.sum(prod[:, cs], axis=-1, keepdims=True), (ts, HEAD_DIM))

    blk = pl.BlockSpec((ts, w), lambda i, j: (i, j))
    return pl.pallas_call(
        body, name=name, grid=(s_len // ts, ng),
        in_specs=[blk, blk, blk, pl.BlockSpec((ts, w), lambda i, j: (i, 3 * ng + j))],
        out_specs=[blk, blk, blk],
        out_shape=[_sds((s_len, e), BF16), _sds((s_len, e), BF16), _sds((s_len, e), F32)],
        compiler_params=_params(24 * ts * w * 4),
    )(dgated, o, o_lo, proj)


LOG2E = 1.4426950408889634
ATTN_ROW_CHUNK = 128


def _tri_tables(n, k_major):
    pairs = [(i, j) for j in range(n) for i in range(j, n)] if k_major else [(i, j) for i in range(n) for j in range(i + 1)]
    return (jnp.asarray(np.array([p[0] for p in pairs], np.int32)), jnp.asarray(np.array([p[1] for p in pairs], np.int32)))


def _attn_logits2(s_raw, cr2, diag, row0, c1):
    s2 = s_raw * c1 - cr2
    if diag:
        rc, t = s_raw.shape
        row = lax.broadcasted_iota(jnp.int32, (rc, t), 0) + row0
        col = lax.broadcasted_iota(jnp.int32, (rc, t), 1)
        s2 = jnp.where(col <= row, s2, -jnp.inf)
    return s2


def _fox_fwd(qn, kn, vb, c_row, *, name, hg, t=512):
    s_len, e = qn.shape
    t = _tile(s_len, t)
    rc = _tile(t, ATTN_ROW_CHUNK)
    w = hg * HEAD_DIM
    ng, n = e // w, s_len // t
    c1 = HEAD_DIM ** -0.5 * LOG2E
    qi_tab, kj_tab = _tri_tables(n, k_major=False)

    def body(qi_ref, kj_ref, q_ref, k_ref, v_ref, cr_ref, o_ref, olo_ref, lse_ref, m_s, l_s, acc_s, lo_s, s_scr, p_scr,
             a_scr):
        pid = pl.program_id(1)
        i, j = qi_ref[pid], kj_ref[pid]

        @pl.when(j == 0)
        def _():
            m_s[...] = jnp.full_like(m_s, -jnp.inf)
            l_s[...] = jnp.zeros_like(l_s)
            acc_s[...] = jnp.zeros_like(acc_s)
            lo_s[...] = jnp.zeros_like(lo_s)

        def step(diag):
            for hh in range(hg):
                cs = slice(hh * HEAD_DIM, (hh + 1) * HEAD_DIM)
                s_scr[...] = lax.dot_general(q_ref[:, cs], k_ref[:, cs], (((1,), (1,)), ((), ())),
                                             preferred_element_type=F32)
                cr2 = cr_ref[hh] * LOG2E
                for r in range(t // rc):
                    rows = slice(r * rc, (r + 1) * rc)
                    s2 = _attn_logits2(s_scr[rows, :], cr2, diag, r * rc, c1)
                    m_prev = m_s[hh, rows]
                    m_new = jnp.maximum(m_prev, jnp.max(s2, axis=-1, keepdims=True))
                    alpha = jnp.exp2(m_prev - m_new)
                    p = jnp.exp2(s2 - jnp.tile(m_new, (1, t // LANES)))
                    l_s[hh, rows] = alpha * l_s[hh, rows] + jnp.sum(p, axis=-1, keepdims=True)
                    m_s[hh, rows] = m_new
                    a_scr[rows] = alpha
                    p_hi = p.astype(BF16)
                    p_scr[rows, :] = p_hi
                    p_scr[t + r * rc:t + (r + 1) * rc, :] = (p - p_hi.astype(F32)).astype(BF16)
                pv = jnp.dot(p_scr[...], v_ref[:, cs], preferred_element_type=F32)
                al = a_scr[...]
                acc_s[:, cs] = al * acc_s[:, cs] + pv[:t]
                lo_s[:, cs] = al * lo_s[:, cs] + pv[t:]

        @pl.when(j < i)
        def _():
            step(False)

        @pl.when(j == i)
        def _():
            step(True)
            for hh in range(hg):
                cs = slice(hh * HEAD_DIM, (hh + 1) * HEAD_DIM)
                o_ref[:, cs] = acc_s[:, cs] / l_s[hh]
                olo_ref[:, cs] = lo_s[:, cs] / l_s[hh]
                lse_ref[:, cs] = m_s[hh] + jnp.log2(l_s[hh])

    qspec = pl.BlockSpec((t, w), lambda g, p, qi, kj: (qi[p], g))
    kspec = pl.BlockSpec((t, w), lambda g, p, qi, kj: (kj[p], g))
    crow = pl.BlockSpec((hg, 1, t), lambda g, p, qi, kj: (g, 0, kj[p]))
    grid_spec = pltpu.PrefetchScalarGridSpec(
        num_scalar_prefetch=2, grid=(ng, int(qi_tab.shape[0])), in_specs=[qspec, kspec, kspec, crow],
        out_specs=[qspec, qspec, qspec],
        scratch_shapes=[pltpu.VMEM((hg, t, LANES), F32), pltpu.VMEM((hg, t, LANES), F32), pltpu.VMEM((t, w), F32),
                        pltpu.VMEM((t, w), F32), pltpu.VMEM((t, t), F32), pltpu.VMEM((2 * t, t), BF16),
                        pltpu.VMEM((t, LANES), F32)])
    return pl.pallas_call(
        body, name=name, grid_spec=grid_spec,
        out_shape=[_sds((s_len, e), F32), _sds((s_len, e), F32), _sds((s_len, e), F32)],
        compiler_params=_params(12 * t * t * 4 + 32 * t * w * 4),
    )(qi_tab, kj_tab, qn, kn, vb, c_row)


def _fox_bwd(qn, kn, vb, do, lse, delta, c_row, *, name, hg, t=512):
    s_len, e = qn.shape
    t = _tile(s_len, t)
    w = hg * HEAD_DIM
    ng, n = e // w, s_len // t
    rc = _tile(t, ATTN_ROW_CHUNK)
    scale = HEAD_DIM ** -0.5
    c1 = scale * LOG2E
    qi_tab, kj_tab = _tri_tables(n, k_major=True)

    def body(qi_ref, kj_ref, q_ref, k_ref, v_ref, do_ref, lse_ref, dl_ref, cr_ref, dq_ref, dk_ref, dv_ref, dc_ref,
             dk_s, dv_s, dc_s, s_scr, dp_scr, p_scr, ds_scr):
        pid = pl.program_id(1)
        i, j = qi_ref[pid], kj_ref[pid]

        @pl.when(pid == 0)
        def _():
            dq_ref[...] = jnp.zeros_like(dq_ref)

        @pl.when(i == j)
        def _():
            dk_s[...] = jnp.zeros_like(dk_s)
            dv_s[...] = jnp.zeros_like(dv_s)
            dc_s[...] = jnp.zeros_like(dc_s)

        def step(diag):
            qrows = pl.ds(pl.multiple_of(i * t, t), t)
            for hh in range(hg):
                cs = slice(hh * HEAD_DIM, (hh + 1) * HEAD_DIM)
                s_scr[...] = lax.dot_general(q_ref[:, cs], k_ref[:, cs], (((1,), (1,)), ((), ())),
                                             preferred_element_type=F32)
                dp_scr[...] = lax.dot_general(do_ref[:, cs], v_ref[:, cs], (((1,), (1,)), ((), ())),
                                              preferred_element_type=F32)
                cr2 = cr_ref[hh] * LOG2E
                dcol = jnp.zeros((SUBLANES, t), F32)
                for r in range(t // rc):
                    rows = slice(r * rc, (r + 1) * rc)
                    s2 = _attn_logits2(s_scr[rows, :], cr2, diag, r * rc, c1)
                    p = jnp.exp2(s2 - jnp.tile(lse_ref[rows, cs], (1, t // LANES)))
                    ds = p * (dp_scr[rows, :] - jnp.tile(dl_ref[rows, cs], (1, t // LANES)))
                    dcol = dcol + _rowsum8(ds)
                    p_scr[rows, :] = p.astype(BF16)
                    ds_scr[rows, :] = (ds * scale).astype(BF16)
                dc_s[hh] -= jnp.sum(dcol, axis=0, keepdims=True)
                dv_s[:, cs] += lax.dot_general(p_scr[...], do_ref[:, cs], (((0,), (0,)), ((), ())),
                                               preferred_element_type=F32)
                dk_s[:, cs] += lax.dot_general(ds_scr[...], q_ref[:, cs], (((0,), (0,)), ((), ())),
                                               preferred_element_type=F32)
                dq_ref[qrows, cs] += jnp.dot(ds_scr[...], k_ref[:, cs], preferred_element_type=F32)

        @pl.when(i > j)
        def _():
            step(False)

        @pl.when(i == j)
        def _():
            step(True)

        @pl.when(i == n - 1)
        def _():
            dk_ref[...] = dk_s[...]
            dv_ref[...] = dv_s[...]
            dc_ref[...] = dc_s[...]

    qspec = pl.BlockSpec((t, w), lambda g, p, qi, kj: (qi[p], g))
    kspec = pl.BlockSpec((t, w), lambda g, p, qi, kj: (kj[p], g))
    crow = pl.BlockSpec((hg, 1, t), lambda g, p, qi, kj: (g, 0, kj[p]))
    grid_spec = pltpu.PrefetchScalarGridSpec(
        num_scalar_prefetch=2, grid=(ng, int(qi_tab.shape[0])),
        in_specs=[qspec, kspec, kspec, qspec, qspec, qspec, crow],
        out_specs=[pl.BlockSpec((s_len, w), lambda g, p, qi, kj: (0, g)), kspec, kspec, crow],
        scratch_shapes=[pltpu.VMEM((t, w), F32), pltpu.VMEM((t, w), F32), pltpu.VMEM((hg, 1, t), F32),
                        pltpu.VMEM((t, t), F32), pltpu.VMEM((t, t), F32), pltpu.VMEM((t, t), BF16),
                        pltpu.VMEM((t, t), BF16)])
    return pl.pallas_call(
        body, name=name, grid_spec=grid_spec,
        out_shape=[_sds((s_len, e), F32), _sds((s_len, e), F32), _sds((s_len, e), F32), _sds((e // HEAD_DIM, 1, s_len), F32)],
        compiler_params=_params(2 * s_len * w * 4 + 16 * t * t * 4 + 24 * t * w * 4),
    )(qi_tab, kj_tab, qn, kn, vb, do, lse, delta, c_row)


def _adamw(w, gs, m, v, *, name, tr=256):
    r, c = w.shape
    tr = _tile(r, tr)
    n_g = len(gs)

    def body(*refs):
        w_ref, g_refs = refs[0], refs[1:1 + n_g]
        m_ref, v_ref, go_ref, d_ref, nm_ref, nv_ref = refs[1 + n_g:]
        gv = g_refs[0][...].astype(F32)
        for g_ref in g_refs[1:]:
            gv = gv + g_ref[...].astype(F32)
        go_ref[...] = gv
        m2 = ADAM_B1 * m_ref[...] + (1.0 - ADAM_B1) * gv
        v2 = ADAM_B2 * v_ref[...] + (1.0 - ADAM_B2) * (gv * gv)
        m_hat = m2 / (1.0 - ADAM_B1 ** ADAM_STEP)
        v_hat = v2 / (1.0 - ADAM_B2 ** ADAM_STEP)
        d_ref[...] = -ADAM_LR * (m_hat / (jnp.sqrt(v_hat) + ADAM_EPS) + ADAM_WD * w_ref[...])
        nm_ref[...] = m2
        nv_ref[...] = v2

    blk = pl.BlockSpec((tr, c), lambda i: (i, 0))
    return pl.pallas_call(
        body, name=name, grid=(r // tr,), in_specs=[blk] * (3 + n_g), out_specs=[blk] * 4,
        out_shape=[_sds((r, c), F32)] * 4, compiler_params=_params(24 * tr * c * 4),
    )(w, *gs, m, v)


def _adamw_halves(w, mine, other, m, v, c_arr, *, name, tr=256):
    r, c = w.shape
    half = r // 2
    tr = _tile(half, tr)
    nh = half // tr

    def body(c_ref, w_ref, mine_ref, other_ref, m_ref, v_ref, go_ref, d_ref, nm_ref, nv_ref):
        is_mine = pl.program_id(0) == c_ref[0]
        gv = jnp.where(is_mine, mine_ref[...], other_ref[...])
        go_ref[...] = gv
        m2 = ADAM_B1 * m_ref[...] + (1.0 - ADAM_B1) * gv
        v2 = ADAM_B2 * v_ref[...] + (1.0 - ADAM_B2) * (gv * gv)
        m_hat = m2 / (1.0 - ADAM_B1 ** ADAM_STEP)
        v_hat = v2 / (1.0 - ADAM_B2 ** ADAM_STEP)
        d_ref[...] = -ADAM_LR * (m_hat / (jnp.sqrt(v_hat) + ADAM_EPS) + ADAM_WD * w_ref[...])
        nm_ref[...] = m2
        nv_ref[...] = v2

    full = pl.BlockSpec((tr, c), lambda h, j, cref: (h * nh + j, 0))
    part = pl.BlockSpec((tr, c), lambda h, j, cref: (j, 0))
    grid_spec = pltpu.PrefetchScalarGridSpec(num_scalar_prefetch=1, grid=(2, nh), in_specs=[full, part, part, full, full],
                                             out_specs=[full] * 4)
    return pl.pallas_call(
        body, name=name, grid_spec=grid_spec, out_shape=[_sds((r, c), F32)] * 4,
        compiler_params=_params(28 * tr * c * 4),
    )(c_arr, w, mine, other, m, v)


def _pair_add(g, got, c_arr, *, name, tr=256):
    n, r, c = g.shape
    half = r // 2
    tr = _tile(half, tr)
    nh = half // tr

    def body(c_ref, g_ref, got_ref, o_ref):
        o_ref[...] = (g_ref[...].astype(F32) + got_ref[...].astype(F32)).astype(o_ref.dtype)

    grid_spec = pltpu.PrefetchScalarGridSpec(
        num_scalar_prefetch=1, grid=(n, nh),
        in_specs=[pl.BlockSpec((None, tr, c), lambda s, i, cref: (s, cref[0] * nh + i, 0)),
                  pl.BlockSpec((None, tr, c), lambda s, i, cref: (s, i, 0))],
        out_specs=pl.BlockSpec((None, tr, c), lambda s, i, cref: (s, i, 0)))
    return pl.pallas_call(
        body, name=name, grid_spec=grid_spec, out_shape=_sds((n, half, c), BF16), compiler_params=_params(16 * tr * c * 4),
    )(c_arr, g, got)


def _sum_own_recv(pair, recv, chip_arr, *, name, tr=256):
    _, r, c = pair.shape
    tr = _tile(r, tr)
    n_recv = recv.shape[0]

    def body(chip_ref, own_ref, recv_ref, o_ref):
        acc = own_ref[...].astype(F32)
        for k in range(n_recv):
            acc = acc + recv_ref[k].astype(F32)
        o_ref[...] = acc

    grid_spec = pltpu.PrefetchScalarGridSpec(
        num_scalar_prefetch=1, grid=(r // tr,),
        in_specs=[pl.BlockSpec((None, tr, c), lambda i, chip: (chip[0], i, 0)),
                  pl.BlockSpec((n_recv, tr, c), lambda i, chip: (0, i, 0))],
        out_specs=pl.BlockSpec((tr, c), lambda i, chip: (i, 0)))
    return pl.pallas_call(
        body, name=name, grid_spec=grid_spec, out_shape=_sds((r, c), F32), compiler_params=_params(16 * tr * c * 4),
    )(chip_arr, pair, recv)


_ANY = pl.BlockSpec(memory_space=pl.ANY)
DMA_CHUNK_BYTES = 512 << 10


def _chunks(parts):
    out = []
    for src_at, dst_at, rows, row_bytes in parts:
        step = max(16, DMA_CHUNK_BYTES // row_bytes // 16 * 16)
        for r0 in range(0, rows, step):
            n = min(step, rows - r0)
            out.append((src_at(r0, n), dst_at(r0, n)))
    return out


def _row_bytes(ref):
    return ref.shape[-1] * ref.dtype.itemsize


def _me():
    return lax.axis_index("x"), lax.axis_index("y"), lax.axis_index("c")


def _chip_peers(x, y):
    return [(2 * (1 - x) + y, (1 - x, y)), (2 * x + (1 - y), (x, 1 - y)), (2 * (1 - x) + (1 - y), (1 - x, 1 - y))]


def _exchange(name, ins, out_shapes, plan):
    n_in, n_out = len(ins), len(out_shapes)

    def body(*refs):
        in_refs, out_refs = refs[:n_in], refs[n_in:n_in + n_out]
        send_sems, recv_sems, loc_sems = refs[n_in + n_out:]
        remote, local = plan(in_refs, out_refs)
        starts, waits = [], []
        for k, (ws, wd, dev, parts) in enumerate(remote):
            def mk(s, d, k=k, dev=dev):
                return pltpu.make_async_remote_copy(src_ref=s, dst_ref=d, send_sem=send_sems.at[k],
                                                    recv_sem=recv_sems.at[k], device_id=dev, device_id_type=MESH_ID)
            starts += [mk(s, d) for s, d in _chunks(parts)]
            waits.append(mk(ws, wd))
        for k, (ws, wd, _, parts) in enumerate(local):
            def mk(s, d, k=k):
                return pltpu.make_async_copy(s, d, loc_sems.at[k])
            starts += [mk(s, d) for s, d in _chunks(parts)]
            waits.append(mk(ws, wd))
        for cp in starts:
            cp.start()
        for cp in waits:
            cp.wait()

    n_remote, n_local = plan.n_remote, plan.n_local
    return pl.pallas_call(
        body, name=name, in_specs=[_ANY] * n_in, out_specs=[_ANY] * n_out, out_shape=list(out_shapes),
        scratch_shapes=[pltpu.SemaphoreType.DMA((n_remote,)), pltpu.SemaphoreType.DMA((n_remote,)),
                        pltpu.SemaphoreType.DMA((max(n_local, 1),))],
    )(*ins)


def _gather_weights(shards, *, name):
    n_t = len(shards)

    def body(*refs):
        in_refs, out_refs = refs[:n_t], refs[n_t:2 * n_t]
        send_sems, recv_sems = refs[2 * n_t:]
        x, y, c = _me()
        chip = 2 * x + y
        peers = _chip_peers(x, y)
        local, first, passed = [], [], []
        for t in range(n_t):
            src, dst = in_refs[t], out_refs[t]
            n_rows = src.shape[0]
            half = n_rows // 2
            rb = _row_bytes(src)
            rows = pl.ds(c * half, half)

            def mk_own(s, d, t=t):
                return pltpu.make_async_remote_copy(
                    src_ref=s, dst_ref=d, send_sem=send_sems.at[7 * t + 6], recv_sem=recv_sems.at[7 * t + 6],
                    device_id=(x, y, 1 - c), device_id_type=MESH_ID)

            own = [(lambda r0, n, src=src: src.at[pl.ds(r0, n)],
                    lambda r0, n, dst=dst: dst.at[chip, pl.ds(r0, n)], n_rows, rb)]
            local.append((mk_own(src, dst.at[chip]), [mk_own(s, d) for s, d in _chunks(own)]))
            for k, (pchip, (px, py)) in enumerate(peers):
                def mk_ici(s, d, t=t, k=k, px=px, py=py):
                    return pltpu.make_async_remote_copy(
                        src_ref=s, dst_ref=d, send_sem=send_sems.at[7 * t + k], recv_sem=recv_sems.at[7 * t + k],
                        device_id=(px, py, c), device_id_type=MESH_ID)

                def mk_d2d(s, d, t=t, k=k):
                    return pltpu.make_async_remote_copy(
                        src_ref=s, dst_ref=d, send_sem=send_sems.at[7 * t + 3 + k], recv_sem=recv_sems.at[7 * t + 3 + k],
                        device_id=(x, y, 1 - c), device_id_type=MESH_ID)

                out_part = [(lambda r0, n, src=src: src.at[pl.ds(c * half + r0, n)],
                             lambda r0, n, dst=dst: dst.at[chip, pl.ds(c * half + r0, n)], half, rb)]
                fwd_part = [(lambda r0, n, dst=dst, pchip=pchip: dst.at[pchip, pl.ds(c * half + r0, n)],
                             lambda r0, n, dst=dst, pchip=pchip: dst.at[pchip, pl.ds(c * half + r0, n)], half, rb)]
                first.append((mk_ici(src.at[rows], dst.at[chip, rows]), [mk_ici(s, d) for s, d in _chunks(out_part)]))
                passed.append((mk_d2d(dst.at[pchip, rows], dst.at[pchip, rows]),
                               [mk_d2d(s, d) for s, d in _chunks(fwd_part)]))
        for _, chunk_copies in first + local:
            for cp in chunk_copies:
                cp.start()
        for (whole, _), (_, fwd_copies) in zip(first, passed):
            whole.wait_recv()
            for cp in fwd_copies:
                cp.start()
        for whole, _ in passed:
            whole.wait_recv()
        for whole, _ in first + passed:
            whole.wait_send()
        for whole, _ in local:
            whole.wait()

    outs = [_sds((N_CHIPS,) + s.shape, s.dtype) for s in shards]
    return pl.pallas_call(
        body, name=name, in_specs=[_ANY] * n_t, out_specs=[_ANY] * n_t, out_shape=outs,
        scratch_shapes=[pltpu.SemaphoreType.DMA((7 * n_t,)), pltpu.SemaphoreType.DMA((7 * n_t,))],
    )(*shards)


class _Plan:
    def __init__(self, fn, n_remote, n_local):
        self.fn, self.n_remote, self.n_local = fn, n_remote, n_local

    def __call__(self, in_refs, out_refs):
        return self.fn(in_refs, out_refs)


def _reduce_grads(grads, *, name):
    n_t = len(grads)
    c_arr = lax.axis_index("c").astype(jnp.int32).reshape(1)
    chip_arr = (2 * lax.axis_index("x") + lax.axis_index("y")).astype(jnp.int32).reshape(1)

    def plan1(in_refs, out_refs):
        x, y, c = _me()
        remote = []
        for t in range(n_t):
            src, got = in_refs[t], out_refs[t]
            half = src.shape[1] // 2
            send = [(lambda r0, n, s=s, src=src, half=half: src.at[s, pl.ds((1 - c) * half + r0, n)],
                     lambda r0, n, s=s, got=got: got.at[s, pl.ds(r0, n)], half, _row_bytes(src)) for s in range(N_CHIPS)]
            remote.append((src.at[:, pl.ds((1 - c) * half, half)], got, (x, y, 1 - c), send))
        return remote, []

    halves = [_sds((N_CHIPS, g.shape[1] // 2, g.shape[2]), g.dtype) for g in grads]
    got = _exchange(name + "_sib", grads, halves, _Plan(plan1, n_t, 0))
    pair = [_pair_add(grads[t], got[t], c_arr, name=f"{name}_pair{t}") for t in range(n_t)]

    def plan2(in_refs, out_refs):
        x, y, c = _me()
        remote = []
        for t in range(n_t):
            src, dst = in_refs[t], out_refs[t]
            rows, rb = src.shape[1], _row_bytes(src)
            for k, (pchip, (px, py)) in enumerate(_chip_peers(x, y)):
                part = [(lambda r0, n, src=src, pchip=pchip: src.at[pchip, pl.ds(r0, n)],
                         lambda r0, n, dst=dst, k=k: dst.at[k, pl.ds(r0, n)], rows, rb)]
                remote.append((src.at[pchip], dst.at[k], (px, py, c), part))
        return remote, []

    recv_shapes = [_sds((N_CHIPS - 1,) + h.shape[1:], h.dtype) for h in halves]
    recv = _exchange(name + "_ici", pair, recv_shapes, _Plan(plan2, 3 * n_t, 0))
    mine = [_sum_own_recv(pair[t], recv[t], chip_arr, name=f"{name}_sum{t}") for t in range(n_t)]

    def plan3(in_refs, out_refs):
        x, y, c = _me()
        remote = []
        for t in range(n_t):
            src, dst = in_refs[t], out_refs[t]
            rows = [(lambda r0, n, src=src: src.at[pl.ds(r0, n)], lambda r0, n, dst=dst: dst.at[pl.ds(r0, n)],
                     src.shape[0], _row_bytes(src))]
            remote.append((src, dst, (x, y, 1 - c), rows))
        return remote, []

    other = _exchange(name + "_swap", mine, [_sds(s.shape, s.dtype) for s in mine], _Plan(plan3, n_t, 0))
    return list(zip(mine, other)), c_arr


def _allreduce_small(pack, *, name):
    r, w = pack.shape

    def body(p_ref, o_ref, buf, send_sems, recv_sems):
        x, y, c = _me()
        me = 4 * x + 2 * y + c
        buf[me] = p_ref[...]
        copies = []
        for k in range(1, N_DEV):
            peer = (x ^ ((k >> 2) & 1), y ^ ((k >> 1) & 1), c ^ (k & 1))
            copies.append(pltpu.make_async_remote_copy(
                src_ref=p_ref, dst_ref=buf.at[me], send_sem=send_sems.at[k - 1], recv_sem=recv_sems.at[k - 1],
                device_id=peer, device_id_type=MESH_ID))
        for cp in copies:
            cp.start()
        for cp in copies:
            cp.wait()
        acc = buf[0]
        for k in range(1, N_DEV):
            acc = acc + buf[k]
        o_ref[...] = acc

    vm = pl.BlockSpec(memory_space=pltpu.VMEM)
    return pl.pallas_call(
        body, name=name, in_specs=[vm], out_specs=vm, out_shape=_sds((r, w), F32),
        scratch_shapes=[pltpu.VMEM((N_DEV, r, w), F32), pltpu.SemaphoreType.DMA((N_DEV - 1,)),
                        pltpu.SemaphoreType.DMA((N_DEV - 1,))],
        compiler_params=_params(12 * r * w * 4),
    )(pack)


def _pack(arrs, row_multiple=16):
    flat = jnp.concatenate([a.reshape(-1).astype(F32) for a in arrs])
    unit = row_multiple * LANES
    total = -(-flat.shape[0] // unit) * unit
    return jnp.pad(flat, (0, total - flat.shape[0])).reshape(total // LANES, LANES)


def _unpack(packed, shapes):
    flat = packed.reshape(-1)
    out, off = [], 0
    for shp in shapes:
        n = int(np.prod(shp))
        out.append(flat[off:off + n].reshape(shp))
        off += n
    return out


def _pad_cols(a, width):
    return jnp.pad(a, [(0, 0)] * (a.ndim - 1) + [(0, width - a.shape[-1])])


ATTN_HEADS_PER_STEP = 4
SMALL_SHARDED = ("a_norm", "a_conv_w", "a_conv_b", "a_ln_g", "a_ln_b", "c_norm", "c_conv_w")
SMALL_REPLICATED = ("b_norm", "b_f_bias", "b_q_norm", "b_k_norm")
BIG = ("a_w_in", "a_w_out", "b_w_in", "b_w_out", "c_w_in", "c_w_out")
WEIGHTS = ("a_norm", "a_w_in", "a_conv_w", "a_conv_b", "a_ln_g", "a_ln_b", "a_w_out", "b_norm", "b_w_in", "b_f_bias",
           "b_q_norm", "b_k_norm", "b_w_out", "c_norm", "c_w_in", "c_conv_w", "c_w_out")


def _mixer_a_fwd(x, p, l, tag):
    h = _rms_fwd(x, p["a_norm"][l][None], name=f"{tag}_rms")
    proj = _mm_nn(h, p["a_w_in"][l], name=f"{tag}_in", tn=p["a_w_in"][l].shape[2])
    u1 = _a_conv_fwd(proj, p["a_conv_w"][l], p["a_conv_b"][l][None], name=f"{tag}_conv")
    gated = _a_post_fwd(u1, proj, p["a_ln_g"][l][None], p["a_ln_b"][l][None], name=f"{tag}_post")
    y = _mm_nn(gated, p["a_w_out"][l], name=f"{tag}_out", tn=1024, add=x)
    return y, (x, h, proj, u1, gated)


def _mixer_a_bwd(dx, saved, p, l, tag):
    x, h, proj, u1, gated = saved
    g = {}
    g["a_w_out"] = _mm_tn(gated, dx, name=f"{tag}_dwout", out_dtype=BF16, tk=2048, tn=1024, ts=1024)
    dgated = _mm_nt(dx, p["a_w_out"][l], name=f"{tag}_dgated", tn=2048, tk=1024)
    du1, dz, g["a_ln_g"], g["a_ln_b"], g["a_conv_b"] = _a_post_bwd(
        dgated, u1, proj, p["a_ln_g"][l][None], p["a_ln_b"][l][None], name=f"{tag}_dpost")
    dproj, g["a_conv_w"] = _a_conv_bwd(du1, proj, dz, p["a_conv_w"][l], name=f"{tag}_dconv")
    g["a_w_in"] = _mm_tn(h, dproj, name=f"{tag}_dwin", out_dtype=BF16, out_width=p["a_w_in"][l].shape[2], ts=2048)
    dh = _mm_nt(dproj, p["a_w_in"][l], name=f"{tag}_dh", ksub=4)
    dx, g["a_norm"] = _rms_bwd(x, p["a_norm"][l][None], [dh], dx, name=f"{tag}_drms")
    return dx, g


def _mixer_c_fwd(x, p, tag):
    h = _rms_fwd(x, p["c_norm"][0][None], name=f"{tag}_rms")
    proj = _mm_nn(h, p["c_w_in"], name=f"{tag}_in", tn=1024)
    gated = _c_mid_fwd(proj, p["c_conv_w"][0], name=f"{tag}_mid")
    y = _mm_nn(gated, p["c_w_out"], name=f"{tag}_out", tn=1024, add=x)
    return y, (x, h, proj, gated)


def _mixer_c_bwd(dx, saved, p, tag):
    x, h, proj, gated = saved
    g = {}
    g["c_w_out"] = _mm_tn(gated, dx, name=f"{tag}_dwout", out_dtype=BF16, tk=2048, tn=1024, ts=1024)
    dgated = _mm_nt(dx, p["c_w_out"], name=f"{tag}_dgated", tn=2048, tk=1024)
    dproj, g["c_conv_w"] = _c_mid_bwd(dgated, proj, p["c_conv_w"][0], name=f"{tag}_dmid")
    g["c_w_in"] = _mm_tn(h, dproj, name=f"{tag}_dwin", out_dtype=BF16, out_width=p["c_w_in"].shape[2], tn=1024, ts=2048)
    dh = _mm_nt(dproj, p["c_w_in"], name=f"{tag}_dh", tk=1024, ksub=2)
    dx, g["c_norm"] = _rms_bwd(x, p["c_norm"][0][None], [dh], dx, name=f"{tag}_drms")
    return dx, g


def _mixer_b_fwd(x, p, tag):
    hg = ATTN_HEADS_PER_STEP
    s_len = x.shape[0]
    n_heads = p["b_f_bias"].shape[1]
    h = _rms_fwd(x, p["b_norm"], name=f"{tag}_rms")
    proj = _mm_nn(h, p["b_wq"], name=f"{tag}_in", tn=1024)
    fl = _mm_nn(h, p["b_wf"], name=f"{tag}_inf", tn=LANES)
    qn, kn, vb = _b_qk_fwd(proj, p["b_q_norm"], p["b_k_norm"], name=f"{tag}_qk")
    bias = _pad_cols(p["b_f_bias"], LANES)
    c = _b_cumsum(fl, bias, name=f"{tag}_cumsum")
    ch = c[:, :n_heads]
    c_row = ch.T.reshape(n_heads, 1, s_len)
    o, o_lo, lse = _fox_fwd(qn, kn, vb, c_row, name=f"{tag}_attn", hg=hg)
    gated = _b_gate_fwd(o, proj, name=f"{tag}_gate")
    y = _mm_nn(gated, p["b_w_out"], name=f"{tag}_out", tn=1024, add=x)
    return y, (x, h, proj, fl, bias, qn, kn, vb, c_row, o, o_lo, lse, gated)


def _mixer_b_bwd(dx, saved, p, tag):
    hg = ATTN_HEADS_PER_STEP
    x, h, proj, fl, bias, qn, kn, vb, c_row, o, o_lo, lse, gated = saved
    s_len = x.shape[0]
    n_heads = p["b_f_bias"].shape[1]
    g = {}
    g["b_w_out"] = _mm_tn(gated, dx, name=f"{tag}_dwout", out_dtype=BF16, tk=2048, tn=1024, ts=1024)
    dgated = _mm_nt(dx, p["b_w_out"], name=f"{tag}_dgated", tn=2048, tk=1024)
    do, dz, delta = _b_gate_bwd(dgated, o, o_lo, proj, name=f"{tag}_dgate", hg=hg)
    dqn, dkn, dv, dc = _fox_bwd(qn, kn, vb, do, lse, delta, c_row, name=f"{tag}_dattn", hg=hg)
    dc_pad = _pad_cols(dc.reshape(n_heads, s_len).T, LANES)
    dfl, dbias = _b_cumsum_bwd(dc_pad, fl, bias, name=f"{tag}_dcumsum")
    g["b_f_bias"] = dbias[:, :n_heads]
    dproj, g["b_q_norm"], g["b_k_norm"] = _b_qk_bwd(dqn, dkn, dv, dz, proj, p["b_q_norm"], p["b_k_norm"], name=f"{tag}_dqk")
    dwq = _mm_tn(h, dproj, name=f"{tag}_dwin", out_dtype=BF16, tn=1024, ts=2048)
    dwf = _mm_tn(h, dfl, name=f"{tag}_dwinf", out_dtype=BF16, tn=LANES)
    g["b_w_in"] = jnp.concatenate([dwq, dwf[:, :n_heads]], axis=1)
    dh = _mm_nt(dproj, p["b_wq"], name=f"{tag}_dh", tk=1024, ksub=2)
    dhf = _mm_nt(dfl, p["b_wf"], name=f"{tag}_dhf", tk=LANES)
    dx, g["b_norm"] = _rms_bwd(x, p["b_norm"], [dh, dhf], dx, name=f"{tag}_drms")
    return dx, g


def kernel(x, a_norm, a_w_in, a_conv_w, a_conv_b, a_ln_g, a_ln_b, a_w_out, b_norm, b_w_in, b_f_bias, b_q_norm, b_k_norm, b_w_out, c_norm, c_w_in, c_conv_w, c_w_out, loss_target, m_a_norm, m_a_w_in, m_a_conv_w, m_a_conv_b, m_a_ln_g, m_a_ln_b, m_a_w_out, m_b_norm, m_b_w_in, m_b_f_bias, m_b_q_norm, m_b_k_norm, m_b_w_out, m_c_norm, m_c_w_in, m_c_conv_w, m_c_w_out, v_a_norm, v_a_w_in, v_a_conv_w, v_a_conv_b, v_a_ln_g, v_a_ln_b, v_a_w_out, v_b_norm, v_b_w_in, v_b_f_bias, v_b_q_norm, v_b_k_norm, v_b_w_out, v_c_norm, v_c_w_in, v_c_conv_w, v_c_w_out):
    w_loc = dict(a_norm=a_norm, a_w_in=a_w_in, a_conv_w=a_conv_w, a_conv_b=a_conv_b, a_ln_g=a_ln_g, a_ln_b=a_ln_b,
                 a_w_out=a_w_out, b_norm=b_norm, b_w_in=b_w_in, b_f_bias=b_f_bias, b_q_norm=b_q_norm, b_k_norm=b_k_norm,
                 b_w_out=b_w_out, c_norm=c_norm, c_w_in=c_w_in, c_conv_w=c_conv_w, c_w_out=c_w_out)
    m_loc = dict(a_norm=m_a_norm, a_w_in=m_a_w_in, a_conv_w=m_a_conv_w, a_conv_b=m_a_conv_b, a_ln_g=m_a_ln_g,
                 a_ln_b=m_a_ln_b, a_w_out=m_a_w_out, b_norm=m_b_norm, b_w_in=m_b_w_in, b_f_bias=m_b_f_bias,
                 b_q_norm=m_b_q_norm, b_k_norm=m_b_k_norm, b_w_out=m_b_w_out, c_norm=m_c_norm, c_w_in=m_c_w_in,
                 c_conv_w=m_c_conv_w, c_w_out=m_c_w_out)
    v_loc = dict(a_norm=v_a_norm, a_w_in=v_a_w_in, a_conv_w=v_a_conv_w, a_conv_b=v_a_conv_b, a_ln_g=v_a_ln_g,
                 a_ln_b=v_a_ln_b, a_w_out=v_a_w_out, b_norm=v_b_norm, b_w_in=v_b_w_in, b_f_bias=v_b_f_bias,
                 b_q_norm=v_b_q_norm, b_k_norm=v_b_k_norm, b_w_out=v_b_w_out, c_norm=v_c_norm, c_w_in=v_c_w_in,
                 c_conv_w=v_c_conv_w, c_w_out=v_c_w_out)
    n_a = a_w_in.shape[0]
    d_model = x.shape[2]
    e_inner = a_w_out.shape[1] * N_CHIPS
    n_heads = b_f_bias.shape[1]
    nb_loc = b_w_in.shape[2]
    nb_pad = -(-nb_loc // LANES) * LANES
    chip = 2 * lax.axis_index("x") + lax.axis_index("y")

    big_shards = ([a_w_in[l].astype(BF16) for l in range(n_a)] + [a_w_out[l].astype(BF16) for l in range(n_a)]
                  + [_pad_cols(b_w_in[0], nb_pad).astype(BF16), b_w_out[0].astype(BF16), c_w_in[0].astype(BF16),
                     c_w_out[0].astype(BF16)])
    small_pack = _pack([w_loc[n] for n in SMALL_SHARDED])
    gathered = _gather_weights(big_shards + [small_pack], name="gather_weights")
    p = {}
    p["a_w_in"] = gathered[0:n_a]
    p["a_w_out"] = [g.reshape(e_inner, d_model) for g in gathered[n_a:2 * n_a]]
    gb, gbo, gci, gco, gsmall = gathered[2 * n_a:]
    wb_full = jnp.concatenate([gb[k, :, :nb_loc] for k in range(N_CHIPS)], axis=1)
    p["b_wq"] = wb_full[:, :4 * e_inner]
    p["b_wf"] = _pad_cols(wb_full[:, 4 * e_inner:], LANES)
    p["b_w_out"] = gbo.reshape(e_inner, d_model)
    p["c_w_in"] = gci
    p["c_w_out"] = gco.reshape(e_inner, d_model)
    small_shapes = [w_loc[n].shape for n in SMALL_SHARDED]
    per_chip = [_unpack(gsmall[k], small_shapes) for k in range(N_CHIPS)]
    for idx, n in enumerate(SMALL_SHARDED):
        p[n] = jnp.concatenate([per_chip[k][idx] for k in range(N_CHIPS)], axis=-1)
    for n in SMALL_REPLICATED:
        p[n] = w_loc[n]

    x0 = x[0]
    x1, sv0 = _mixer_a_fwd(x0, p, 0, "a0")
    x2, sv1 = _mixer_b_fwd(x1, p, "b0")
    x3, sv2 = _mixer_c_fwd(x2, p, "c0")
    x4, sv3 = _mixer_a_fwd(x3, p, 1, "a1")
    dy, loss_part = _loss_head(x4, loss_target[0], name="loss_head")
    loss = lax.psum(loss_part[0, 0], ("x", "y", "c"))

    dx, g3 = _mixer_a_bwd(dy, sv3, p, 1, "a1")
    dx, g2 = _mixer_c_bwd(dx, sv2, p, "c0")
    dx, g1 = _mixer_b_bwd(dx, sv1, p, "b0")
    dx, g0 = _mixer_a_bwd(dx, sv0, p, 0, "a0")
    grad_x = dx[None]

    half_rows = e_inner // N_CHIPS
    gb_full = g1["b_w_in"].reshape(d_model, N_CHIPS, nb_loc).transpose(1, 0, 2)
    big_grads = ([g0["a_w_in"], g3["a_w_in"]]
                 + [g0["a_w_out"].reshape(N_CHIPS, half_rows, d_model), g3["a_w_out"].reshape(N_CHIPS, half_rows, d_model)]
                 + [_pad_cols(gb_full, nb_pad), g1["b_w_out"].reshape(N_CHIPS, half_rows, d_model), g2["c_w_in"],
                    g2["c_w_out"].reshape(N_CHIPS, half_rows, d_model)])
    red, c_arr = _reduce_grads(big_grads, name="reduce_grads")
    big_order = [("a_w_in", 0), ("a_w_in", 1), ("a_w_out", 0), ("a_w_out", 1), ("b_w_in", 0), ("b_w_out", 0),
                 ("c_w_in", 0), ("c_w_out", 0)]

    grads, delta, new_m, new_v = {}, {}, {}, {}
    per_layer = {n: [] for n in BIG}
    for (n, l), (mine, other) in zip(big_order, red):
        cols = mine.shape[1]
        true_cols = w_loc[n].shape[-1]
        outs = _adamw_halves(_pad_cols(w_loc[n][l], cols), mine, other, _pad_cols(m_loc[n][l], cols),
                             _pad_cols(v_loc[n][l], cols), c_arr, name=f"adamw_{n}{l}")
        per_layer[n].append([a[:, :true_cols] for a in outs])
    for n in BIG:
        grads[n], delta[n], new_m[n], new_v[n] = [jnp.stack([layer[k] for layer in per_layer[n]]) for k in range(4)]

    small_full = {}
    for n in ("a_norm", "a_conv_w", "a_conv_b", "a_ln_g", "a_ln_b"):
        small_full[n] = jnp.stack([g0[n].reshape(p[n].shape[1:]), g3[n].reshape(p[n].shape[1:])])
    small_full["c_norm"] = g2["c_norm"].reshape(p["c_norm"].shape)
    small_full["c_conv_w"] = g2["c_conv_w"].reshape(p["c_conv_w"].shape)
    for n in SMALL_REPLICATED:
        small_full[n] = g1[n].reshape(w_loc[n].shape)
    small_names = SMALL_SHARDED + SMALL_REPLICATED
    summed = _unpack(_allreduce_small(_pack([small_full[n] for n in small_names], 8), name="reduce_small"),
                     [small_full[n].shape for n in small_names])
    for n, s in zip(small_names, summed):
        if n in SMALL_SHARDED:
            width = w_loc[n].shape[-1]
            grads[n] = lax.dynamic_slice_in_dim(s, chip * width, width, axis=s.ndim - 1)
        else:
            grads[n] = s

    small_shapes_all = [w_loc[n].shape for n in small_names]
    _, d_, m_, v_ = _adamw(_pack([w_loc[n] for n in small_names], 8), [_pack([grads[n] for n in small_names], 8)],
                           _pack([m_loc[n] for n in small_names], 8), _pack([v_loc[n] for n in small_names], 8),
                           name="adamw_small")
    for n, a, b, c_ in zip(small_names, _unpack(d_, small_shapes_all), _unpack(m_, small_shapes_all),
                           _unpack(v_, small_shapes_all)):
        delta[n], new_m[n], new_v[n] = a, b, c_

    return (loss, grad_x, *[grads[n] for n in WEIGHTS], *[delta[n] for n in WEIGHTS],
            *[new_m[n] for n in WEIGHTS], *[new_v[n] for n in WEIGHTS])
```

```python
import functools

import numpy as np
import jax
import jax.numpy as jnp
from jax import lax
from jax.experimental import pallas as pl
from jax.experimental.pallas import tpu as pltpu

F32 = jnp.float32
BF16 = jnp.bfloat16
NORM_EPS = 1e-6
HEAD_DIM = 128
LANES = 128
N_CHIPS = 4
N_DEV = 8
VMEM_CAP = 56 << 20
MESH_ID = pl.DeviceIdType.MESH

ADAM_LR = 0.001
ADAM_B1 = 0.9
ADAM_B2 = 0.999
ADAM_EPS = 1e-08
ADAM_WD = 0.01
ADAM_STEP = 10


def _sds(shape, dtype):
    return jax.ShapeDtypeStruct(tuple(shape), dtype)


def _tile(n, pref):
    if n <= pref:
        return n
    for t in range(pref - pref % 8, 0, -8):
        if n % t == 0:
            return t
    raise ValueError(f"no tile for {n} under {pref}")


def _params(vmem_bytes):
    return pltpu.CompilerParams(vmem_limit_bytes=int(min(max(vmem_bytes, 16 << 20), VMEM_CAP)))


def _sigmoid(v):
    return 1.0 / (1.0 + jnp.exp(-v))


def _silu(v):
    return v * _sigmoid(v)


def _silu_pair(v):
    s = _sigmoid(v)
    return v * s, s * (1.0 + v * (1.0 - s))


def _rowsum8(v):
    r, c = v.shape
    return jnp.sum(v.reshape(r // 8, 8, c), axis=0)


def _col_spec(grouped, rows_block, tile, width, row_of, col_of):
    if grouped:
        per = width // tile
        return pl.BlockSpec((None, rows_block, tile), lambda *ids: (col_of(*ids) // per, row_of(*ids), col_of(*ids) % per))
    return pl.BlockSpec((rows_block, tile), lambda *ids: (row_of(*ids), col_of(*ids)))


def _mm_nn(a, b, *, name, tm=1024, tn=512, out_dtype=F32, add=None):
    m, k = a.shape
    b_grouped = b.ndim == 3
    n = b.shape[0] * b.shape[2] if b_grouped else b.shape[1]
    width = b.shape[2] if b_grouped else n
    tm, tn = _tile(m, tm), _tile(width, tn)

    def body(*refs):
        if add is None:
            a_ref, b_ref, o_ref = refs
        else:
            a_ref, b_ref, r_ref, o_ref = refs
        acc = jnp.dot(a_ref[...].astype(BF16), b_ref[...].astype(BF16), preferred_element_type=F32)
        if add is not None:
            acc = acc + r_ref[...]
        o_ref[...] = acc.astype(o_ref.dtype)

    in_specs = [pl.BlockSpec((tm, k), lambda i, j: (i, 0)),
                _col_spec(b_grouped, k, tn, width, lambda i, j: 0, lambda i, j: j)]
    args = [a, b]
    if add is not None:
        in_specs.append(pl.BlockSpec((tm, tn), lambda i, j: (i, j)))
        args.append(add)
    vmem = 2 * (tm * k * a.dtype.itemsize + k * tn * 2 + tm * tn * 4 * (2 if add is not None else 1)) + tm * tn * 8 + tm * k * 2
    return pl.pallas_call(
        body, name=name, grid=(m // tm, n // tn), in_specs=in_specs,
        out_specs=pl.BlockSpec((tm, tn), lambda i, j: (i, j)),
        out_shape=_sds((m, n), out_dtype), compiler_params=_params(vmem + (4 << 20)),
    )(*args)


def _mm_nt(a, b, *, name, tm=1024, tn=1024, tk=512, ksub=1):
    a_grouped, b_grouped = a.ndim == 3, b.ndim == 3
    m = a.shape[1] if a_grouped else a.shape[0]
    n = a.shape[0] * a.shape[2] if a_grouped else a.shape[1]
    kk = b.shape[1] if b_grouped else b.shape[0]
    wa = a.shape[2] if a_grouped else n
    wb = b.shape[2] if b_grouped else n
    tm, tn = _tile(m, tm), _tile(kk, tn)
    tk = _tile(int(np.gcd(wa, wb)), tk)
    ksub = min(ksub, n // tk)
    assert (n // tk) % ksub == 0, (n, tk, ksub)
    steps = n // (tk * ksub)

    def body(*refs):
        a_refs, b_refs, o_ref = refs[:ksub], refs[ksub:2 * ksub], refs[2 * ksub]
        part = None
        for a_ref, b_ref in zip(a_refs, b_refs):
            d = lax.dot_general(a_ref[...].astype(BF16), b_ref[...].astype(BF16), (((1,), (1,)), ((), ())),
                                preferred_element_type=F32)
            part = d if part is None else part + d

        @pl.when(pl.program_id(2) == 0)
        def _():
            o_ref[...] = part

        @pl.when(pl.program_id(2) > 0)
        def _():
            o_ref[...] += part

    def sub(u):
        return lambda i, j, s: s * ksub + u

    in_specs = ([_col_spec(a_grouped, tm, tk, wa, lambda i, j, s: i, sub(u)) for u in range(ksub)]
                + [_col_spec(b_grouped, tn, tk, wb, lambda i, j, s: j, sub(u)) for u in range(ksub)])
    vmem = (2 * ksub * (tm * tk * a.dtype.itemsize + tn * tk * b.dtype.itemsize) + 2 * tm * tn * 4 + 2 * tm * tn * 4
            + (tm + tn) * tk * 2)
    return pl.pallas_call(
        body, name=name, grid=(m // tm, kk // tn, steps), in_specs=in_specs,
        out_specs=pl.BlockSpec((tm, tn), lambda i, j, s: (i, j)),
        out_shape=_sds((m, kk), F32), compiler_params=_params(vmem + (4 << 20)),
    )(*([a] * ksub), *([b] * ksub))


def _mm_tn(a, b, *, name, out_width=None, out_dtype=F32, tk=1024, tn=512, ts=512):
    s_len, k = a.shape
    b_grouped = b.ndim == 3
    n = b.shape[0] * b.shape[2] if b_grouped else b.shape[1]
    wb = b.shape[2] if b_grouped else n
    wo = out_width if out_width is not None else n
    tk, ts = _tile(k, tk), _tile(s_len, ts)
    tn = _tile(int(np.gcd(wb, wo)), tn)
    last = s_len // ts - 1
    direct = out_dtype == F32

    def body(a_ref, b_ref, o_ref, *scratch):
        acc = o_ref if direct else scratch[0]
        part = lax.dot_general(a_ref[...].astype(BF16), b_ref[...].astype(BF16), (((0,), (0,)), ((), ())),
                               preferred_element_type=F32)

        @pl.when(pl.program_id(2) == 0)
        def _():
            acc[...] = part

        @pl.when(pl.program_id(2) > 0)
        def _():
            acc[...] += part

        if not direct:
            @pl.when(pl.program_id(2) == last)
            def _():
                o_ref[...] = acc[...].astype(o_ref.dtype)

    in_specs = [pl.BlockSpec((ts, tk), lambda i, j, s: (s, i)),
                _col_spec(b_grouped, ts, tn, wb, lambda i, j, s: s, lambda i, j, s: j)]
    out_grouped = out_width is not None
    out_spec = _col_spec(out_grouped, tk, tn, wo, lambda i, j, s: i, lambda i, j, s: j)
    out_shape = _sds((n // wo, k, wo), out_dtype) if out_grouped else _sds((k, n), out_dtype)
    vmem = 2 * (ts * tk * a.dtype.itemsize + ts * tn * b.dtype.itemsize + tk * tn * 4) + 2 * tk * tn * 4 + ts * (tk + tn) * 4
    return pl.pallas_call(
        body, name=name, grid=(k // tk, n // tn, s_len // ts), in_specs=in_specs, out_specs=out_spec,
        out_shape=out_shape, scratch_shapes=[] if direct else [pltpu.VMEM((tk, tn), F32)],
        compiler_params=_params(vmem + (4 << 20)),
    )(a, b)


def _rms_fwd(x, g, *, name, ts=512):
    s_len, d = x.shape
    ts = _tile(s_len, ts)

    def body(x_ref, g_ref, h_ref):
        xf = x_ref[...]
        r = lax.rsqrt(jnp.mean(xf * xf, axis=-1, keepdims=True) + NORM_EPS)
        h_ref[...] = ((xf * r) * g_ref[...]).astype(h_ref.dtype)

    return pl.pallas_call(
        body, name=name, grid=(s_len // ts,),
        in_specs=[pl.BlockSpec((ts, d), lambda i: (i, 0)), pl.BlockSpec((1, d), lambda i: (0, 0))],
        out_specs=pl.BlockSpec((ts, d), lambda i: (i, 0)), out_shape=_sds((s_len, d), BF16),
        compiler_params=_params(8 * ts * d * 4),
    )(x, g)


def _rms_bwd(x, g, dhs, dres, *, name, ts=512):
    s_len, d = x.shape
    ts = _tile(s_len, ts)
    n_dh = len(dhs)
    last = s_len // ts - 1

    def body(*refs):
        x_ref, g_ref = refs[0], refs[1]
        dh_refs = refs[2:2 + n_dh]
        dres_ref, dx_ref, dg_ref, acc = refs[2 + n_dh:]
        i = pl.program_id(0)

        @pl.when(i == 0)
        def _():
            acc[...] = jnp.zeros_like(acc)

        xf = x_ref[...]
        dy = dh_refs[0][...]
        for r_ in dh_refs[1:]:
            dy = dy + r_[...]
        r = lax.rsqrt(jnp.mean(xf * xf, axis=-1, keepdims=True) + NORM_EPS)
        gd = dy * g_ref[...]
        dot = jnp.mean(xf * gd, axis=-1, keepdims=True)
        dx_ref[...] = dres_ref[...] + r * gd - xf * (r * r * r * dot)
        acc[...] += _rowsum8(dy * (xf * r))

        @pl.when(i == last)
        def _():
            dg_ref[...] = jnp.sum(acc[...], axis=0, keepdims=True)

    row = pl.BlockSpec((ts, d), lambda i: (i, 0))
    vec = pl.BlockSpec((1, d), lambda i: (0, 0))
    return pl.pallas_call(
        body, name=name, grid=(s_len // ts,),
        in_specs=[row, vec] + [row] * n_dh + [row],
        out_specs=[row, vec], out_shape=[_sds((s_len, d), F32), _sds((1, d), F32)],
        scratch_shapes=[pltpu.VMEM((8, d), F32)],
        compiler_params=_params((2 * (3 + n_dh) + 6) * ts * d * 4),
    )(x, g, *dhs, dres)


def _loss_head(y, target, *, name, ts=512):
    s_len, d = y.shape
    ts = _tile(s_len, ts)
    last = s_len // ts - 1

    def body(y_ref, t_ref, dy_ref, loss_ref, acc):
        i = pl.program_id(0)

        @pl.when(i == 0)
        def _():
            acc[...] = jnp.zeros_like(acc)

        err = y_ref[...] - t_ref[...]
        dy_ref[...] = err / d
        acc[...] += _rowsum8(err * err)

        @pl.when(i == last)
        def _():
            loss_ref[...] = (0.5 * jnp.sum(acc[...]) / d).reshape(1, 1)

    row = pl.BlockSpec((ts, d), lambda i: (i, 0))
    return pl.pallas_call(
        body, name=name, grid=(s_len // ts,), in_specs=[row, row],
        out_specs=[row, pl.BlockSpec((1, 1), lambda i: (0, 0))],
        out_shape=[_sds((s_len, d), F32), _sds((1, 1), F32)],
        scratch_shapes=[pltpu.VMEM((8, d), F32)],
        compiler_params=_params(10 * ts * d * 4),
    )(y, target)


CONV_ROWS = 64
CONV_COLS = 256


SUBLANES = 8


def _fill_shifts(buf, sh_ref):
    n = sh_ref.shape[1]
    for s in range(1, SUBLANES):
        sh_ref[s - 1, :, :] = buf[s:s + n, :]


def _shifted_rows(buf, sh_ref, start, rw, cs):
    s = start % SUBLANES
    if sh_ref is None or s == 0:
        return buf[start:start + rw, cs]
    return sh_ref[s - 1, start - s:start - s + rw, cs]


def _conv_taps(buf, sh_ref, w_ref, k_width, base, ts, tc, init, emit, reverse=False):
    cw = min(tc, CONV_COLS)
    rw = min(ts, CONV_ROWS)
    for cb in range(tc // cw):
        cs = slice(cb * cw, (cb + 1) * cw)
        for rb in range(ts // rw):
            acc = init(slice(rb * rw, (rb + 1) * rw), cs, (rw, cw))
            for k in range(k_width):
                sh = (k_width - 1 - k) if reverse else k
                acc = acc + w_ref[k:k + 1, cs] * _shifted_rows(buf, sh_ref, base + rb * rw + sh, rw, cs)
            emit(slice(rb * rw, (rb + 1) * rw), cs, acc)


def _conv_wgrad(buf, sh_ref, d_ref_val, acc_ref, k_width, base, ts, tc):
    cw = min(tc, CONV_COLS)
    rw = min(ts, CONV_ROWS)
    for cb in range(tc // cw):
        cs = slice(cb * cw, (cb + 1) * cw)
        for rb in range(ts // rw):
            dv = d_ref_val[rb * rw:(rb + 1) * rw, cs]
            for k in range(k_width):
                prod = dv * _shifted_rows(buf, sh_ref, base + rb * rw + k, rw, cs)
                acc_ref[8 * k:8 * k + 8, cs] += _rowsum8(prod)


A_HALO = 32


def _a_conv_fwd(proj, conv_w, conv_b, *, name, ts=1024, tc=128):
    s_len, e3 = proj.shape
    e = e3 // 3
    k_width = conv_w.shape[0]
    ts, tc = _tile(s_len, ts), _tile(e, tc)
    nc = e // tc
    kp = 32

    def body(val, gate, valh, gateh, w_ref, b_ref, u1_ref, buf, sh):
        i = pl.program_id(1)
        u0h = valh[...] * _sigmoid(gateh[...])
        buf[0:A_HALO, :] = jnp.where(i > 0, u0h, 0.0)
        buf[A_HALO:A_HALO + ts, :] = val[...] * _sigmoid(gate[...])
        _fill_shifts(buf, sh)

        def init(rows, cs, shape):
            return jnp.broadcast_to(b_ref[:, cs], shape)

        def emit(rows, cs, acc):
            u1_ref[rows, cs] = acc

        _conv_taps(buf, sh, w_ref, k_width, A_HALO - (k_width - 1), ts, tc, init, emit)

    hb = ts // A_HALO
    in_specs = [
        pl.BlockSpec((ts, tc), lambda j, i: (i, j)),
        pl.BlockSpec((ts, tc), lambda j, i: (i, nc + j)),
        pl.BlockSpec((A_HALO, tc), lambda j, i: (jnp.maximum(i * hb - 1, 0), j)),
        pl.BlockSpec((A_HALO, tc), lambda j, i: (jnp.maximum(i * hb - 1, 0), nc + j)),
        pl.BlockSpec((kp, tc), lambda j, i: (0, j)),
        pl.BlockSpec((1, tc), lambda j, i: (0, j)),
    ]
    w_pad = jnp.zeros((kp, e), F32).at[:k_width].set(conv_w)
    return pl.pallas_call(
        body, name=name, grid=(nc, s_len // ts), in_specs=in_specs,
        out_specs=pl.BlockSpec((ts, tc), lambda j, i: (i, j)), out_shape=_sds((s_len, e), F32),
        scratch_shapes=[pltpu.VMEM((A_HALO + ts, tc), F32), pltpu.VMEM((SUBLANES - 1, A_HALO + ts - SUBLANES, tc), F32)],
        compiler_params=_params(24 * ts * tc * 4),
    )(proj, proj, proj, proj, w_pad, conv_b)


def _ln_rows(u1, g, b):
    mu = jnp.mean(u1, axis=-1, keepdims=True)
    xc = u1 - mu
    var = jnp.mean(xc * xc, axis=-1, keepdims=True)
    rstd = lax.rsqrt(var + NORM_EPS)
    xhat = xc * rstd
    return xhat, rstd, xhat * g + b


def _a_post_fwd(u1, proj, ln_g, ln_b, *, name, ts=256):
    s_len, e = u1.shape
    ts = _tile(s_len, ts)

    def body(u1_ref, z_ref, g_ref, b_ref, o_ref):
        _, _, u2 = _ln_rows(u1_ref[...], g_ref[...], b_ref[...])
        o_ref[...] = (_silu(u2) * _silu(z_ref[...])).astype(o_ref.dtype)

    row = pl.BlockSpec((ts, e), lambda i: (i, 0))
    vec = pl.BlockSpec((1, e), lambda i: (0, 0))
    return pl.pallas_call(
        body, name=name, grid=(s_len // ts,),
        in_specs=[row, pl.BlockSpec((ts, e), lambda i: (i, 2)), vec, vec],
        out_specs=row, out_shape=_sds((s_len, e), BF16), compiler_params=_params(12 * ts * e * 4),
    )(u1, proj, ln_g, ln_b)


def _a_post_bwd(dgated, u1, proj, ln_g, ln_b, *, name, ts=256):
    s_len, e = u1.shape
    ts = _tile(s_len, ts)
    last = s_len // ts - 1

    def body(dg_ref, u1_ref, z_ref, g_ref, b_ref, du1_ref, dz_ref, dlg_ref, dlb_ref, dcb_ref, a_g, a_b, a_c):
        i = pl.program_id(0)

        @pl.when(i == 0)
        def _():
            a_g[...] = jnp.zeros_like(a_g)
            a_b[...] = jnp.zeros_like(a_b)
            a_c[...] = jnp.zeros_like(a_c)

        g = g_ref[...]
        xhat, rstd, u2 = _ln_rows(u1_ref[...], g, b_ref[...])
        z = z_ref[...]
        dgt = dg_ref[...]
        su, dsu = _silu_pair(u2)
        sz, dsz = _silu_pair(z)
        dz_ref[...] = (dgt * su * dsz).astype(dz_ref.dtype)
        du2 = dgt * sz * dsu
        a_g[...] += _rowsum8(du2 * xhat)
        a_b[...] += _rowsum8(du2)
        dxh = du2 * g
        m1 = jnp.mean(dxh, axis=-1, keepdims=True)
        m2 = jnp.mean(dxh * xhat, axis=-1, keepdims=True)
        du1 = rstd * (dxh - m1 - xhat * m2)
        du1_ref[...] = du1
        a_c[...] += _rowsum8(du1)

        @pl.when(i == last)
        def _():
            dlg_ref[...] = jnp.sum(a_g[...], axis=0, keepdims=True)
            dlb_ref[...] = jnp.sum(a_b[...], axis=0, keepdims=True)
            dcb_ref[...] = jnp.sum(a_c[...], axis=0, keepdims=True)

    row = pl.BlockSpec((ts, e), lambda i: (i, 0))
    vec = pl.BlockSpec((1, e), lambda i: (0, 0))
    return pl.pallas_call(
        body, name=name, grid=(s_len // ts,),
        in_specs=[row, row, pl.BlockSpec((ts, e), lambda i: (i, 2)), vec, vec],
        out_specs=[row, row, vec, vec, vec],
        out_shape=[_sds((s_len, e), F32), _sds((s_len, e), BF16), _sds((1, e), F32), _sds((1, e), F32), _sds((1, e), F32)],
        scratch_shapes=[pltpu.VMEM((8, e), F32)] * 3,
        compiler_params=_params(20 * ts * e * 4),
    )(dgated, u1, proj, ln_g, ln_b)


def _a_conv_bwd(du1, proj, dz, conv_w, *, name, ts=1024, tc=128):
    s_len, e3 = proj.shape
    e = e3 // 3
    k_width = conv_w.shape[0]
    ts, tc = _tile(s_len, ts), _tile(e, tc)
    nc, nr = e // tc, s_len // ts
    kp = 32
    hb = ts // A_HALO

    def body(val, gate, valh, gateh, d_ref, dh_ref, dz_ref, w_ref, dp_ref, dw_ref, buf_u, buf_d, du0, acc, sh_u, sh_d):
        i = pl.program_id(1)

        @pl.when(i == 0)
        def _():
            acc[...] = jnp.zeros_like(acc)

        buf_u[0:A_HALO, :] = jnp.where(i > 0, valh[...] * _sigmoid(gateh[...]), 0.0)
        buf_u[A_HALO:A_HALO + ts, :] = val[...] * _sigmoid(gate[...])
        buf_d[0:ts, :] = d_ref[...]
        buf_d[ts:ts + A_HALO, :] = jnp.where(i < nr - 1, dh_ref[...], 0.0)
        _fill_shifts(buf_u, sh_u)
        _fill_shifts(buf_d, sh_d)

        def init(rows, cs, shape):
            return jnp.zeros(shape, F32)

        def emit(rows, cs, a):
            du0[rows, cs] = a

        _conv_taps(buf_d, sh_d, w_ref, k_width, 0, ts, tc, init, emit, reverse=True)
        _conv_wgrad(buf_u, sh_u, buf_d, acc, k_width, A_HALO - (k_width - 1), ts, tc)
        rw = min(ts, CONV_ROWS)
        for rb in range(ts // rw):
            rows = slice(rb * rw, (rb + 1) * rw)
            d0 = du0[rows, :]
            sg = _sigmoid(gate[rows, :])
            dp_ref[0, rows, :] = (d0 * sg).astype(dp_ref.dtype)
            dp_ref[1, rows, :] = (d0 * val[rows, :] * sg * (1.0 - sg)).astype(dp_ref.dtype)
        dp_ref[2] = dz_ref[...]

        @pl.when(i == nr - 1)
        def _():
            for k in range(kp):
                dw_ref[k:k + 1, :] = jnp.sum(acc[8 * k:8 * k + 8, :], axis=0, keepdims=True)

    in_specs = [
        pl.BlockSpec((ts, tc), lambda j, i: (i, j)),
        pl.BlockSpec((ts, tc), lambda j, i: (i, nc + j)),
        pl.BlockSpec((A_HALO, tc), lambda j, i: (jnp.maximum(i * hb - 1, 0), j)),
        pl.BlockSpec((A_HALO, tc), lambda j, i: (jnp.maximum(i * hb - 1, 0), nc + j)),
        pl.BlockSpec((ts, tc), lambda j, i: (i, j)),
        pl.BlockSpec((A_HALO, tc), lambda j, i: (jnp.minimum((i + 1) * hb, nr * hb - 1), j)),
        pl.BlockSpec((ts, tc), lambda j, i: (i, j)),
        pl.BlockSpec((kp, tc), lambda j, i: (0, j)),
    ]
    w_pad = jnp.zeros((kp, e), F32).at[:k_width].set(conv_w)
    dproj, dw = pl.pallas_call(
        body, name=name, grid=(nc, nr), in_specs=in_specs,
        out_specs=[pl.BlockSpec((3, ts, tc), lambda j, i: (0, i, j)), pl.BlockSpec((kp, tc), lambda j, i: (0, j))],
        out_shape=[_sds((3, s_len, e), BF16), _sds((kp, e), F32)],
        scratch_shapes=[pltpu.VMEM((A_HALO + ts, tc), F32), pltpu.VMEM((ts + A_HALO, tc), F32),
                        pltpu.VMEM((ts, tc), F32), pltpu.VMEM((8 * kp, tc), F32),
                        pltpu.VMEM((SUBLANES - 1, A_HALO + ts - SUBLANES, tc), F32),
                        pltpu.VMEM((SUBLANES - 1, A_HALO + ts - SUBLANES, tc), F32)],
        compiler_params=_params(56 * ts * tc * 4),
    )(proj, proj, proj, proj, du1, du1, dz, w_pad)
    return dproj, dw[:k_width]


C_HALO = 8


def _c_mid_fwd(proj, conv_w, *, name, ts=1024, tc=128):
    s_len, e4 = proj.shape
    e = e4 // 4
    k_width = conv_w.shape[0]
    ts, tc = _tile(s_len, ts), _tile(e, tc)
    nc = e // tc
    hb = ts // C_HALO

    def body(u, bg, cg, z, uh, cgh, w_ref, o_ref, buf, y):
        i = pl.program_id(1)
        buf[0:C_HALO, :] = jnp.where(i > 0, uh[...] * cgh[...], 0.0)
        buf[C_HALO:C_HALO + ts, :] = u[...] * cg[...]

        def init(rows, cs, shape):
            return jnp.zeros(shape, F32)

        def emit(rows, cs, a):
            y[rows, cs] = a

        _conv_taps(buf, None, w_ref, k_width, C_HALO - (k_width - 1), ts, tc, init, emit)
        o_ref[...] = (bg[...] * y[...] * _silu(z[...])).astype(o_ref.dtype)

    def grp(g):
        return pl.BlockSpec((ts, tc), lambda j, i: (i, g * nc + j))

    def halo(g):
        return pl.BlockSpec((C_HALO, tc), lambda j, i: (jnp.maximum(i * hb - 1, 0), g * nc + j))

    w_pad = jnp.zeros((8, e), F32).at[:k_width].set(conv_w)
    return pl.pallas_call(
        body, name=name, grid=(nc, s_len // ts),
        in_specs=[grp(0), grp(1), grp(2), grp(3), halo(0), halo(2), pl.BlockSpec((8, tc), lambda j, i: (0, j))],
        out_specs=pl.BlockSpec((ts, tc), lambda j, i: (i, j)), out_shape=_sds((s_len, e), BF16),
        scratch_shapes=[pltpu.VMEM((C_HALO + ts, tc), F32), pltpu.VMEM((ts, tc), F32)],
        compiler_params=_params(16 * ts * tc * 4),
    )(proj, proj, proj, proj, proj, proj, w_pad)


def _c_mid_bwd(dgated, proj, conv_w, *, name, ts=1024, tc=128):
    s_len, e4 = proj.shape
    e = e4 // 4
    k_width = conv_w.shape[0]
    ts, tc = _tile(s_len, ts), _tile(e, tc)
    nc, nr = e // tc, s_len // ts
    hb = ts // C_HALO

    def body(u, bg, cg, z, uh, cgh, dg, dgh, bgh, zh, w_ref, dp_ref, dw_ref, buf_p, buf_d, y, dpv, acc):
        i = pl.program_id(1)

        @pl.when(i == 0)
        def _():
            acc[...] = jnp.zeros_like(acc)

        uv, bgv, cgv, zv, dgv = u[...], bg[...], cg[...], z[...], dg[...]
        buf_p[0:C_HALO, :] = jnp.where(i > 0, uh[...] * cgh[...], 0.0)
        buf_p[C_HALO:C_HALO + ts, :] = uv * cgv
        sz, dsz = _silu_pair(zv)
        buf_d[0:ts, :] = dgv * sz * bgv
        buf_d[ts:ts + C_HALO, :] = jnp.where(i < nr - 1, dgh[...] * _silu(zh[...]) * bgh[...], 0.0)

        def init(rows, cs, shape):
            return jnp.zeros(shape, F32)

        def emit_y(rows, cs, a):
            y[rows, cs] = a

        def emit_dp(rows, cs, a):
            dpv[rows, cs] = a

        _conv_taps(buf_p, None, w_ref, k_width, C_HALO - (k_width - 1), ts, tc, init, emit_y)
        _conv_taps(buf_d, None, w_ref, k_width, 0, ts, tc, init, emit_dp, reverse=True)
        _conv_wgrad(buf_p, None, buf_d, acc, k_width, C_HALO - (k_width - 1), ts, tc)
        yv, dp = y[...], dpv[...]
        dp_ref[0] = (dp * cgv).astype(dp_ref.dtype)
        dp_ref[1] = (dgv * sz * yv).astype(dp_ref.dtype)
        dp_ref[2] = (dp * uv).astype(dp_ref.dtype)
        dp_ref[3] = (dgv * bgv * yv * dsz).astype(dp_ref.dtype)

        @pl.when(i == nr - 1)
        def _():
            for k in range(8):
                dw_ref[k:k + 1, :] = jnp.sum(acc[8 * k:8 * k + 8, :], axis=0, keepdims=True)

    def grp(g):
        return pl.BlockSpec((ts, tc), lambda j, i: (i, g * nc + j))

    def prev(g):
        return pl.BlockSpec((C_HALO, tc), lambda j, i: (jnp.maximum(i * hb - 1, 0), g * nc + j))

    def nxt(g):
        return pl.BlockSpec((C_HALO, tc), lambda j, i: (jnp.minimum((i + 1) * hb, nr * hb - 1), g * nc + j))

    w_pad = jnp.zeros((8, e), F32).at[:k_width].set(conv_w)
    dproj, dw = pl.pallas_call(
        body, name=name, grid=(nc, nr),
        in_specs=[grp(0), grp(1), grp(2), grp(3), prev(0), prev(2),
                  pl.BlockSpec((ts, tc), lambda j, i: (i, j)),
                  pl.BlockSpec((C_HALO, tc), lambda j, i: (jnp.minimum((i + 1) * hb, nr * hb - 1), j)),
                  nxt(1), nxt(3), pl.BlockSpec((8, tc), lambda j, i: (0, j))],
        out_specs=[pl.BlockSpec((4, ts, tc), lambda j, i: (0, i, j)), pl.BlockSpec((8, tc), lambda j, i: (0, j))],
        out_shape=[_sds((4, s_len, e), BF16), _sds((8, e), F32)],
        scratch_shapes=[pltpu.VMEM((C_HALO + ts, tc), F32), pltpu.VMEM((ts + C_HALO, tc), F32),
                        pltpu.VMEM((ts, tc), F32), pltpu.VMEM((ts, tc), F32), pltpu.VMEM((64, tc), F32)],
        compiler_params=_params(32 * ts * tc * 4),
    )(proj, proj, proj, proj, proj, proj, dgated, dgated, proj, proj, w_pad)
    return dproj, dw[:k_width]


def _head_rms(xv, g):
    r = lax.rsqrt(jnp.mean(xv * xv, axis=-1, keepdims=True) + NORM_EPS)
    return r, xv * r * g


def _b_qk_fwd(proj, gq, gk, *, name, ts=512, tc=512):
    s_len, e4 = proj.shape
    e = e4 // 4
    ts, tc = _tile(s_len, ts), _tile(e, tc)
    nc = e // tc

    def body(q, k, v, gq_ref, gk_ref, qn, kn, vb):
        for h in range(tc // HEAD_DIM):
            cs = slice(h * HEAD_DIM, (h + 1) * HEAD_DIM)
            qn[:, cs] = _head_rms(q[:, cs], gq_ref[...])[1].astype(qn.dtype)
            kn[:, cs] = _head_rms(k[:, cs], gk_ref[...])[1].astype(kn.dtype)
        vb[...] = v[...].astype(vb.dtype)

    def grp(g):
        return pl.BlockSpec((ts, tc), lambda i, j: (i, g * nc + j))

    vec = pl.BlockSpec((1, HEAD_DIM), lambda i, j: (0, 0))
    out = pl.BlockSpec((ts, tc), lambda i, j: (i, j))
    return pl.pallas_call(
        body, name=name, grid=(s_len // ts, nc), in_specs=[grp(0), grp(1), grp(2), vec, vec],
        out_specs=[out, out, out], out_shape=[_sds((s_len, e), BF16)] * 3,
        compiler_params=_params(16 * ts * tc * 4),
    )(proj, proj, proj, gq, gk)


def _b_qk_bwd(dqn, dkn, dv, dz, proj, gq, gk, *, name, ts=512, tc=512):
    s_len, e4 = proj.shape
    e = e4 // 4
    ts, tc = _tile(s_len, ts), _tile(e, tc)
    nc, nr = e // tc, s_len // ts

    def body(dq_ref, dk_ref, dv_ref, dz_ref, q, k, gq_ref, gk_ref, dp_ref, dgq_ref, dgk_ref, a_q, a_k):
        i, j = pl.program_id(0), pl.program_id(1)

        @pl.when((i == 0) & (j == 0))
        def _():
            a_q[...] = jnp.zeros_like(a_q)
            a_k[...] = jnp.zeros_like(a_k)

        for h in range(tc // HEAD_DIM):
            cs = slice(h * HEAD_DIM, (h + 1) * HEAD_DIM)
            for slot, src, d_ref, g_ref, acc in ((0, q, dq_ref, gq_ref, a_q), (1, k, dk_ref, gk_ref, a_k)):
                xv = src[:, cs]
                dy = d_ref[:, cs]
                r = lax.rsqrt(jnp.mean(xv * xv, axis=-1, keepdims=True) + NORM_EPS)
                gd = dy * g_ref[...]
                dot = jnp.mean(xv * gd, axis=-1, keepdims=True)
                dp_ref[slot, :, cs] = (r * gd - xv * (r * r * r * dot)).astype(dp_ref.dtype)
                acc[...] += _rowsum8(dy * (xv * r))
        dp_ref[2] = dv_ref[...].astype(dp_ref.dtype)
        dp_ref[3] = dz_ref[...]

        @pl.when((i == nr - 1) & (j == nc - 1))
        def _():
            dgq_ref[...] = jnp.sum(a_q[...], axis=0, keepdims=True)
            dgk_ref[...] = jnp.sum(a_k[...], axis=0, keepdims=True)

    blk = pl.BlockSpec((ts, tc), lambda i, j: (i, j))
    vec = pl.BlockSpec((1, HEAD_DIM), lambda i, j: (0, 0))

    def grp(g):
        return pl.BlockSpec((ts, tc), lambda i, j: (i, g * nc + j))

    return pl.pallas_call(
        body, name=name, grid=(nr, nc), in_specs=[blk, blk, blk, blk, grp(0), grp(1), vec, vec],
        out_specs=[pl.BlockSpec((4, ts, tc), lambda i, j: (0, i, j)), vec, vec],
        out_shape=[_sds((4, s_len, e), BF16), _sds((1, HEAD_DIM), F32), _sds((1, HEAD_DIM), F32)],
        scratch_shapes=[pltpu.VMEM((8, HEAD_DIM), F32)] * 2,
        compiler_params=_params(24 * ts * tc * 4),
    )(dqn, dkn, dv, dz, proj, proj, gq, gk)


def _log_sigmoid(x):
    y = jnp.exp(-jnp.abs(x))
    u = 1.0 + y
    log1p = jnp.where(u == 1.0, y, jnp.log(u) * (y / jnp.where(u == 1.0, 1.0, u - 1.0)))
    return jnp.minimum(x, 0.0) - log1p


def _split3(v):
    hi = v.astype(BF16)
    r1 = v - hi.astype(F32)
    mid = r1.astype(BF16)
    lo = (r1 - mid.astype(F32)).astype(BF16)
    return hi, mid, lo


def _tri_matmul(tri, v):
    hi, mid, lo = _split3(v)
    return (jnp.dot(tri, hi, preferred_element_type=F32) + jnp.dot(tri, mid, preferred_element_type=F32)
            + jnp.dot(tri, lo, preferred_element_type=F32))


def _b_cumsum(fl, bias, *, name, t=512):
    s_len, w = fl.shape
    t = _tile(s_len, t)

    def body(fl_ref, b_ref, c_ref, carry):
        @pl.when(pl.program_id(0) == 0)
        def _():
            carry[...] = jnp.zeros_like(carry)

        logf = _log_sigmoid(fl_ref[...] + b_ref[...])
        row = lax.broadcasted_iota(jnp.int32, (t, t), 0)
        col = lax.broadcasted_iota(jnp.int32, (t, t), 1)
        tri = jnp.where(col <= row, 1.0, 0.0).astype(BF16)
        c = _tri_matmul(tri, logf) + carry[...]
        c_ref[...] = c
        carry[...] = c[t - 1:t, :]

    return pl.pallas_call(
        body, name=name, grid=(s_len // t,),
        in_specs=[pl.BlockSpec((t, w), lambda i: (i, 0)), pl.BlockSpec((1, w), lambda i: (0, 0))],
        out_specs=pl.BlockSpec((t, w), lambda i: (i, 0)), out_shape=_sds((s_len, w), F32),
        scratch_shapes=[pltpu.VMEM((1, w), F32)], compiler_params=_params(16 << 20),
    )(fl, bias)


def _b_cumsum_bwd(dc, fl, bias, *, name, t=512):
    s_len, w = fl.shape
    t = _tile(s_len, t)
    n = s_len // t

    def body(dc_ref, fl_ref, b_ref, dfl_ref, db_ref, carry, acc):
        i = pl.program_id(0)

        @pl.when(i == 0)
        def _():
            carry[...] = jnp.zeros_like(carry)
            acc[...] = jnp.zeros_like(acc)

        row = lax.broadcasted_iota(jnp.int32, (t, t), 0)
        col = lax.broadcasted_iota(jnp.int32, (t, t), 1)
        tri = jnp.where(col >= row, 1.0, 0.0).astype(BF16)
        dlogf = _tri_matmul(tri, dc_ref[...]) + carry[...]
        carry[...] = dlogf[0:1, :]
        dfl = dlogf * _sigmoid(-(fl_ref[...] + b_ref[...]))
        dfl_ref[...] = dfl
        acc[...] += _rowsum8(dfl)

        @pl.when(i == n - 1)
        def _():
            db_ref[...] = jnp.sum(acc[...], axis=0, keepdims=True)

    rev = pl.BlockSpec((t, w), lambda i: (n - 1 - i, 0))
    vec = pl.BlockSpec((1, w), lambda i: (0, 0))
    return pl.pallas_call(
        body, name=name, grid=(n,), in_specs=[rev, rev, vec], out_specs=[rev, vec],
        out_shape=[_sds((s_len, w), F32), _sds((1, w), F32)],
        scratch_shapes=[pltpu.VMEM((1, w), F32), pltpu.VMEM((8, w), F32)], compiler_params=_params(16 << 20),
    )(dc, fl, bias)


def _b_gate_fwd(o, proj, *, name, ts=512, tc=512):
    s_len, e = o.shape
    ts, tc = _tile(s_len, ts), _tile(e, tc)
    nc = e // tc

    def body(o_ref, z_ref, g_ref):
        g_ref[...] = (o_ref[...] * _silu(z_ref[...])).astype(g_ref.dtype)

    blk = pl.BlockSpec((ts, tc), lambda i, j: (i, j))
    return pl.pallas_call(
        body, name=name, grid=(s_len // ts, nc),
        in_specs=[blk, pl.BlockSpec((ts, tc), lambda i, j: (i, 3 * nc + j))], out_specs=blk,
        out_shape=_sds((s_len, e), BF16), compiler_params=_params(12 * ts * tc * 4),
    )(o, proj)


def _b_gate_bwd(dgated, o, o_lo, proj, *, name, hg, ts=512):
    s_len, e = o.shape
    ts = _tile(s_len, ts)
    w = hg * HEAD_DIM
    ng = e // w

    def body(dg_ref, o_ref, olo_ref, z_ref, do_ref, dz_ref, dl_ref):
        dgt, ov, zv = dg_ref[...], o_ref[...], z_ref[...]
        sz, dsz = _silu_pair(zv)
        dob = (dgt * sz).astype(do_ref.dtype)
        do_ref[...] = dob
        dz_ref[...] = (dgt * ov * dsz).astype(dz_ref.dtype)
        prod = dob.astype(F32) * (ov + olo_ref[...])
        for hh in range(hg):
            cs = slice(hh * HEAD_DIM, (hh + 1) * HEAD_DIM)
            dl_ref[:, cs] = jnp.broadcast_to(jnp.sum(prod[:, cs], axis=-1, keepdims=True), (ts, HEAD_DIM))

    blk = pl.BlockSpec((ts, w), lambda i, j: (i, j))
    return pl.pallas_call(
        body, name=name, grid=(s_len // ts, ng),
        in_specs=[blk, blk, blk, pl.BlockSpec((ts, w), lambda i, j: (i, 3 * ng + j))],
        out_specs=[blk, blk, blk],
        out_shape=[_sds((s_len, e), BF16), _sds((s_len, e), BF16), _sds((s_len, e), F32)],
        compiler_params=_params(24 * ts * w * 4),
    )(dgated, o, o_lo, proj)


LOG2E = 1.4426950408889634
ATTN_ROW_CHUNK = 128


def _tri_tables(n, k_major):
    pairs = [(i, j) for j in range(n) for i in range(j, n)] if k_major else [(i, j) for i in range(n) for j in range(i + 1)]
    return (jnp.asarray(np.array([p[0] for p in pairs], np.int32)), jnp.asarray(np.array([p[1] for p in pairs], np.int32)))


def _attn_logits2(s_raw, cr2, diag, row0, c1):
    s2 = s_raw * c1 - cr2
    if diag:
        rc, t = s_raw.shape
        row = lax.broadcasted_iota(jnp.int32, (rc, t), 0) + row0
        col = lax.broadcasted_iota(jnp.int32, (rc, t), 1)
        s2 = jnp.where(col <= row, s2, -jnp.inf)
    return s2


def _fox_fwd(qn, kn, vb, c_row, *, name, hg, t=512):
    s_len, e = qn.shape
    t = _tile(s_len, t)
    rc = _tile(t, ATTN_ROW_CHUNK)
    w = hg * HEAD_DIM
    ng, n = e // w, s_len // t
    c1 = HEAD_DIM ** -0.5 * LOG2E
    qi_tab, kj_tab = _tri_tables(n, k_major=False)

    def body(qi_ref, kj_ref, q_ref, k_ref, v_ref, cr_ref, o_ref, olo_ref, lse_ref, m_s, l_s, acc_s, lo_s, s_scr, p_scr,
             a_scr):
        pid = pl.program_id(1)
        i, j = qi_ref[pid], kj_ref[pid]

        @pl.when(j == 0)
        def _():
            m_s[...] = jnp.full_like(m_s, -jnp.inf)
            l_s[...] = jnp.zeros_like(l_s)
            acc_s[...] = jnp.zeros_like(acc_s)
            lo_s[...] = jnp.zeros_like(lo_s)

        def step(diag):
            for hh in range(hg):
                cs = slice(hh * HEAD_DIM, (hh + 1) * HEAD_DIM)
                s_scr[...] = lax.dot_general(q_ref[:, cs], k_ref[:, cs], (((1,), (1,)), ((), ())),
                                             preferred_element_type=F32)
                cr2 = cr_ref[hh] * LOG2E
                for r in range(t // rc):
                    rows = slice(r * rc, (r + 1) * rc)
                    s2 = _attn_logits2(s_scr[rows, :], cr2, diag, r * rc, c1)
                    m_prev = m_s[hh, rows]
                    m_new = jnp.maximum(m_prev, jnp.max(s2, axis=-1, keepdims=True))
                    alpha = jnp.exp2(m_prev - m_new)
                    p = jnp.exp2(s2 - jnp.tile(m_new, (1, t // LANES)))
                    l_s[hh, rows] = alpha * l_s[hh, rows] + jnp.sum(p, axis=-1, keepdims=True)
                    m_s[hh, rows] = m_new
                    a_scr[rows] = alpha
                    p_hi = p.astype(BF16)
                    p_scr[rows, :] = p_hi
                    p_scr[t + r * rc:t + (r + 1) * rc, :] = (p - p_hi.astype(F32)).astype(BF16)
                pv = jnp.dot(p_scr[...], v_ref[:, cs], preferred_element_type=F32)
                al = a_scr[...]
                acc_s[:, cs] = al * acc_s[:, cs] + pv[:t]
                lo_s[:, cs] = al * lo_s[:, cs] + pv[t:]

        @pl.when(j < i)
        def _():
            step(False)

        @pl.when(j == i)
        def _():
            step(True)
            for hh in range(hg):
                cs = slice(hh * HEAD_DIM, (hh + 1) * HEAD_DIM)
                o_ref[:, cs] = acc_s[:, cs] / l_s[hh]
                olo_ref[:, cs] = lo_s[:, cs] / l_s[hh]
                lse_ref[:, cs] = m_s[hh] + jnp.log2(l_s[hh])

    qspec = pl.BlockSpec((t, w), lambda g, p, qi, kj: (qi[p], g))
    kspec = pl.BlockSpec((t, w), lambda g, p, qi, kj: (kj[p], g))
    crow = pl.BlockSpec((hg, 1, t), lambda g, p, qi, kj: (g, 0, kj[p]))
    grid_spec = pltpu.PrefetchScalarGridSpec(
        num_scalar_prefetch=2, grid=(ng, int(qi_tab.shape[0])), in_specs=[qspec, kspec, kspec, crow],
        out_specs=[qspec, qspec, qspec],
        scratch_shapes=[pltpu.VMEM((hg, t, LANES), F32), pltpu.VMEM((hg, t, LANES), F32), pltpu.VMEM((t, w), F32),
                        pltpu.VMEM((t, w), F32), pltpu.VMEM((t, t), F32), pltpu.VMEM((2 * t, t), BF16),
                        pltpu.VMEM((t, LANES), F32)])
    return pl.pallas_call(
        body, name=name, grid_spec=grid_spec,
        out_shape=[_sds((s_len, e), F32), _sds((s_len, e), F32), _sds((s_len, e), F32)],
        compiler_params=_params(12 * t * t * 4 + 32 * t * w * 4),
    )(qi_tab, kj_tab, qn, kn, vb, c_row)


def _fox_bwd(qn, kn, vb, do, lse, delta, c_row, *, name, hg, t=512):
    s_len, e = qn.shape
    t = _tile(s_len, t)
    w = hg * HEAD_DIM
    ng, n = e // w, s_len // t
    rc = _tile(t, ATTN_ROW_CHUNK)
    scale = HEAD_DIM ** -0.5
    c1 = scale * LOG2E
    qi_tab, kj_tab = _tri_tables(n, k_major=True)

    def body(qi_ref, kj_ref, q_ref, k_ref, v_ref, do_ref, lse_ref, dl_ref, cr_ref, dq_ref, dk_ref, dv_ref, dc_ref,
             dk_s, dv_s, dc_s, s_scr, dp_scr, p_scr, ds_scr):
        pid = pl.program_id(1)
        i, j = qi_ref[pid], kj_ref[pid]

        @pl.when(pid == 0)
        def _():
            dq_ref[...] = jnp.zeros_like(dq_ref)

        @pl.when(i == j)
        def _():
            dk_s[...] = jnp.zeros_like(dk_s)
            dv_s[...] = jnp.zeros_like(dv_s)
            dc_s[...] = jnp.zeros_like(dc_s)

        def step(diag):
            qrows = pl.ds(pl.multiple_of(i * t, t), t)
            for hh in range(hg):
                cs = slice(hh * HEAD_DIM, (hh + 1) * HEAD_DIM)
                s_scr[...] = lax.dot_general(q_ref[:, cs], k_ref[:, cs], (((1,), (1,)), ((), ())),
                                             preferred_element_type=F32)
                dp_scr[...] = lax.dot_general(do_ref[:, cs], v_ref[:, cs], (((1,), (1,)), ((), ())),
                                              preferred_element_type=F32)
                cr2 = cr_ref[hh] * LOG2E
                dcol = jnp.zeros((SUBLANES, t), F32)
                for r in range(t // rc):
                    rows = slice(r * rc, (r + 1) * rc)
                    s2 = _attn_logits2(s_scr[rows, :], cr2, diag, r * rc, c1)
                    p = jnp.exp2(s2 - jnp.tile(lse_ref[rows, cs], (1, t // LANES)))
                    ds = p * (dp_scr[rows, :] - jnp.tile(dl_ref[rows, cs], (1, t // LANES)))
                    dcol = dcol + _rowsum8(ds)
                    p_scr[rows, :] = p.astype(BF16)
                    ds_scr[rows, :] = (ds * scale).astype(BF16)
                dc_s[hh] -= jnp.sum(dcol, axis=0, keepdims=True)
                dv_s[:, cs] += lax.dot_general(p_scr[...], do_ref[:, cs], (((0,), (0,)), ((), ())),
                                               preferred_element_type=F32)
                dk_s[:, cs] += lax.dot_general(ds_scr[...], q_ref[:, cs], (((0,), (0,)), ((), ())),
                                               preferred_element_type=F32)
                dq_ref[qrows, cs] += jnp.dot(ds_scr[...], k_ref[:, cs], preferred_element_type=F32)

        @pl.when(i > j)
        def _():
            step(False)

        @pl.when(i == j)
        def _():
            step(True)

        @pl.when(i == n - 1)
        def _():
            dk_ref[...] = dk_s[...]
            dv_ref[...] = dv_s[...]
            dc_ref[...] = dc_s[...]

    qspec = pl.BlockSpec((t, w), lambda g, p, qi, kj: (qi[p], g))
    kspec = pl.BlockSpec((t, w), lambda g, p, qi, kj: (kj[p], g))
    crow = pl.BlockSpec((hg, 1, t), lambda g, p, qi, kj: (g, 0, kj[p]))
    grid_spec = pltpu.PrefetchScalarGridSpec(
        num_scalar_prefetch=2, grid=(ng, int(qi_tab.shape[0])),
        in_specs=[qspec, kspec, kspec, qspec, qspec, qspec, crow],
        out_specs=[pl.BlockSpec((s_len, w), lambda g, p, qi, kj: (0, g)), kspec, kspec, crow],
        scratch_shapes=[pltpu.VMEM((t, w), F32), pltpu.VMEM((t, w), F32), pltpu.VMEM((hg, 1, t), F32),
                        pltpu.VMEM((t, t), F32), pltpu.VMEM((t, t), F32), pltpu.VMEM((t, t), BF16),
                        pltpu.VMEM((t, t), BF16)])
    return pl.pallas_call(
        body, name=name, grid_spec=grid_spec,
        out_shape=[_sds((s_len, e), F32), _sds((s_len, e), F32), _sds((s_len, e), F32), _sds((e // HEAD_DIM, 1, s_len), F32)],
        compiler_params=_params(2 * s_len * w * 4 + 16 * t * t * 4 + 24 * t * w * 4),
    )(qi_tab, kj_tab, qn, kn, vb, do, lse, delta, c_row)


def _adamw(w, gs, m, v, *, name, tr=256):
    r, c = w.shape
    tr = _tile(r, tr)
    n_g = len(gs)

    def body(*refs):
        w_ref, g_refs = refs[0], refs[1:1 + n_g]
        m_ref, v_ref, go_ref, d_ref, nm_ref, nv_ref = refs[1 + n_g:]
        gv = g_refs[0][...].astype(F32)
        for g_ref in g_refs[1:]:
            gv = gv + g_ref[...].astype(F32)
        go_ref[...] = gv
        m2 = ADAM_B1 * m_ref[...] + (1.0 - ADAM_B1) * gv
        v2 = ADAM_B2 * v_ref[...] + (1.0 - ADAM_B2) * (gv * gv)
        m_hat = m2 / (1.0 - ADAM_B1 ** ADAM_STEP)
        v_hat = v2 / (1.0 - ADAM_B2 ** ADAM_STEP)
        d_ref[...] = -ADAM_LR * (m_hat / (jnp.sqrt(v_hat) + ADAM_EPS) + ADAM_WD * w_ref[...])
        nm_ref[...] = m2
        nv_ref[...] = v2

    blk = pl.BlockSpec((tr, c), lambda i: (i, 0))
    return pl.pallas_call(
        body, name=name, grid=(r // tr,), in_specs=[blk] * (3 + n_g), out_specs=[blk] * 4,
        out_shape=[_sds((r, c), F32)] * 4, compiler_params=_params(24 * tr * c * 4),
    )(w, *gs, m, v)


def _adamw_halves(w, mine, other, m, v, c_arr, *, name, tr=256):
    r, c = w.shape
    half = r // 2
    tr = _tile(half, tr)
    nh = half // tr

    def body(c_ref, w_ref, mine_ref, other_ref, m_ref, v_ref, go_ref, d_ref, nm_ref, nv_ref):
        is_mine = pl.program_id(0) == c_ref[0]
        gv = jnp.where(is_mine, mine_ref[...], other_ref[...])
        go_ref[...] = gv
        m2 = ADAM_B1 * m_ref[...] + (1.0 - ADAM_B1) * gv
        v2 = ADAM_B2 * v_ref[...] + (1.0 - ADAM_B2) * (gv * gv)
        m_hat = m2 / (1.0 - ADAM_B1 ** ADAM_STEP)
        v_hat = v2 / (1.0 - ADAM_B2 ** ADAM_STEP)
        d_ref[...] = -ADAM_LR * (m_hat / (jnp.sqrt(v_hat) + ADAM_EPS) + ADAM_WD * w_ref[...])
        nm_ref[...] = m2
        nv_ref[...] = v2

    full = pl.BlockSpec((tr, c), lambda h, j, cref: (h * nh + j, 0))
    part = pl.BlockSpec((tr, c), lambda h, j, cref: (j, 0))
    grid_spec = pltpu.PrefetchScalarGridSpec(num_scalar_prefetch=1, grid=(2, nh), in_specs=[full, part, part, full, full],
                                             out_specs=[full] * 4)
    return pl.pallas_call(
        body, name=name, grid_spec=grid_spec, out_shape=[_sds((r, c), F32)] * 4,
        compiler_params=_params(28 * tr * c * 4),
    )(c_arr, w, mine, other, m, v)


def _pair_add(g, got, c_arr, *, name, tr=256):
    n, r, c = g.shape
    half = r // 2
    tr = _tile(half, tr)
    nh = half // tr

    def body(c_ref, g_ref, got_ref, o_ref):
        o_ref[...] = (g_ref[...].astype(F32) + got_ref[...].astype(F32)).astype(o_ref.dtype)

    grid_spec = pltpu.PrefetchScalarGridSpec(
        num_scalar_prefetch=1, grid=(n, nh),
        in_specs=[pl.BlockSpec((None, tr, c), lambda s, i, cref: (s, cref[0] * nh + i, 0)),
                  pl.BlockSpec((None, tr, c), lambda s, i, cref: (s, i, 0))],
        out_specs=pl.BlockSpec((None, tr, c), lambda s, i, cref: (s, i, 0)))
    return pl.pallas_call(
        body, name=name, grid_spec=grid_spec, out_shape=_sds((n, half, c), BF16), compiler_params=_params(16 * tr * c * 4),
    )(c_arr, g, got)


def _sum_own_recv(pair, recv, chip_arr, *, name, tr=256):
    _, r, c = pair.shape
    tr = _tile(r, tr)
    n_recv = recv.shape[0]

    def body(chip_ref, own_ref, recv_ref, o_ref):
        acc = own_ref[...].astype(F32)
        for k in range(n_recv):
            acc = acc + recv_ref[k].astype(F32)
        o_ref[...] = acc

    grid_spec = pltpu.PrefetchScalarGridSpec(
        num_scalar_prefetch=1, grid=(r // tr,),
        in_specs=[pl.BlockSpec((None, tr, c), lambda i, chip: (chip[0], i, 0)),
                  pl.BlockSpec((n_recv, tr, c), lambda i, chip: (0, i, 0))],
        out_specs=pl.BlockSpec((tr, c), lambda i, chip: (i, 0)))
    return pl.pallas_call(
        body, name=name, grid_spec=grid_spec, out_shape=_sds((r, c), F32), compiler_params=_params(16 * tr * c * 4),
    )(chip_arr, pair, recv)


_ANY = pl.BlockSpec(memory_space=pl.ANY)
DMA_CHUNK_BYTES = 512 << 10


def _chunks(parts):
    out = []
    for src_at, dst_at, rows, row_bytes in parts:
        step = max(16, DMA_CHUNK_BYTES // row_bytes // 16 * 16)
        for r0 in range(0, rows, step):
            n = min(step, rows - r0)
            out.append((src_at(r0, n), dst_at(r0, n)))
    return out


def _row_bytes(ref):
    return ref.shape[-1] * ref.dtype.itemsize


def _me():
    return lax.axis_index("x"), lax.axis_index("y"), lax.axis_index("c")


def _chip_peers(x, y):
    return [(2 * (1 - x) + y, (1 - x, y)), (2 * x + (1 - y), (x, 1 - y)), (2 * (1 - x) + (1 - y), (1 - x, 1 - y))]


def _exchange(name, ins, out_shapes, plan):
    n_in, n_out = len(ins), len(out_shapes)

    def body(*refs):
        in_refs, out_refs = refs[:n_in], refs[n_in:n_in + n_out]
        send_sems, recv_sems, loc_sems = refs[n_in + n_out:]
        remote, local = plan(in_refs, out_refs)
        starts, waits = [], []
        for k, (ws, wd, dev, parts) in enumerate(remote):
            def mk(s, d, k=k, dev=dev):
                return pltpu.make_async_remote_copy(src_ref=s, dst_ref=d, send_sem=send_sems.at[k],
                                                    recv_sem=recv_sems.at[k], device_id=dev, device_id_type=MESH_ID)
            starts += [mk(s, d) for s, d in _chunks(parts)]
            waits.append(mk(ws, wd))
        for k, (ws, wd, _, parts) in enumerate(local):
            def mk(s, d, k=k):
                return pltpu.make_async_copy(s, d, loc_sems.at[k])
            starts += [mk(s, d) for s, d in _chunks(parts)]
            waits.append(mk(ws, wd))
        for cp in starts:
            cp.start()
        for cp in waits:
            cp.wait()

    n_remote, n_local = plan.n_remote, plan.n_local
    return pl.pallas_call(
        body, name=name, in_specs=[_ANY] * n_in, out_specs=[_ANY] * n_out, out_shape=list(out_shapes),
        scratch_shapes=[pltpu.SemaphoreType.DMA((n_remote,)), pltpu.SemaphoreType.DMA((n_remote,)),
                        pltpu.SemaphoreType.DMA((max(n_local, 1),))],
    )(*ins)


def _gather_weights(shards, *, name):
    n_t = len(shards)

    def body(*refs):
        in_refs, out_refs = refs[:n_t], refs[n_t:2 * n_t]
        send_sems, recv_sems = refs[2 * n_t:]
        x, y, c = _me()
        chip = 2 * x + y
        peers = _chip_peers(x, y)
        local, first, passed = [], [], []
        for t in range(n_t):
            src, dst = in_refs[t], out_refs[t]
            n_rows = src.shape[0]
            half = n_rows // 2
            rb = _row_bytes(src)
            rows = pl.ds(c * half, half)

            def mk_own(s, d, t=t):
                return pltpu.make_async_remote_copy(
                    src_ref=s, dst_ref=d, send_sem=send_sems.at[7 * t + 6], recv_sem=recv_sems.at[7 * t + 6],
                    device_id=(x, y, 1 - c), device_id_type=MESH_ID)

            own = [(lambda r0, n, src=src: src.at[pl.ds(r0, n)],
                    lambda r0, n, dst=dst: dst.at[chip, pl.ds(r0, n)], n_rows, rb)]
            local.append((mk_own(src, dst.at[chip]), [mk_own(s, d) for s, d in _chunks(own)]))
            for k, (pchip, (px, py)) in enumerate(peers):
                def mk_ici(s, d, t=t, k=k, px=px, py=py):
                    return pltpu.make_async_remote_copy(
                        src_ref=s, dst_ref=d, send_sem=send_sems.at[7 * t + k], recv_sem=recv_sems.at[7 * t + k],
                        device_id=(px, py, c), device_id_type=MESH_ID)

                def mk_d2d(s, d, t=t, k=k):
                    return pltpu.make_async_remote_copy(
                        src_ref=s, dst_ref=d, send_sem=send_sems.at[7 * t + 3 + k], recv_sem=recv_sems.at[7 * t + 3 + k],
                        device_id=(x, y, 1 - c), device_id_type=MESH_ID)

                out_part = [(lambda r0, n, src=src: src.at[pl.ds(c * half + r0, n)],
                             lambda r0, n, dst=dst: dst.at[chip, pl.ds(c * half + r0, n)], half, rb)]
                fwd_part = [(lambda r0, n, dst=dst, pchip=pchip: dst.at[pchip, pl.ds(c * half + r0, n)],
                             lambda r0, n, dst=dst, pchip=pchip: dst.at[pchip, pl.ds(c * half + r0, n)], half, rb)]
                first.append((mk_ici(src.at[rows], dst.at[chip, rows]), [mk_ici(s, d) for s, d in _chunks(out_part)]))
                passed.append((mk_d2d(dst.at[pchip, rows], dst.at[pchip, rows]),
                               [mk_d2d(s, d) for s, d in _chunks(fwd_part)]))
        for _, chunk_copies in first + local:
            for cp in chunk_copies:
                cp.start()
        for (whole, _), (_, fwd_copies) in zip(first, passed):
            whole.wait_recv()
            for cp in fwd_copies:
                cp.start()
        for whole, _ in passed:
            whole.wait_recv()
        for whole, _ in first + passed:
            whole.wait_send()
        for whole, _ in local:
            whole.wait()

    outs = [_sds((N_CHIPS,) + s.shape, s.dtype) for s in shards]
    return pl.pallas_call(
        body, name=name, in_specs=[_ANY] * n_t, out_specs=[_ANY] * n_t, out_shape=outs,
        scratch_shapes=[pltpu.SemaphoreType.DMA((7 * n_t,)), pltpu.SemaphoreType.DMA((7 * n_t,))],
    )(*shards)


class _Plan:
    def __init__(self, fn, n_remote, n_local):
        self.fn, self.n_remote, self.n_local = fn, n_remote, n_local

    def __call__(self, in_refs, out_refs):
        return self.fn(in_refs, out_refs)


def _reduce_grads(grads, *, name):
    n_t = len(grads)
    c_arr = lax.axis_index("c").astype(jnp.int32).reshape(1)
    chip_arr = (2 * lax.axis_index("x") + lax.axis_index("y")).astype(jnp.int32).reshape(1)

    def plan1(in_refs, out_refs):
        x, y, c = _me()
        remote = []
        for t in range(n_t):
            src, got = in_refs[t], out_refs[t]
            half = src.shape[1] // 2
            send = [(lambda r0, n, s=s, src=src, half=half: src.at[s, pl.ds((1 - c) * half + r0, n)],
                     lambda r0, n, s=s, got=got: got.at[s, pl.ds(r0, n)], half, _row_bytes(src)) for s in range(N_CHIPS)]
            remote.append((src.at[:, pl.ds((1 - c) * half, half)], got, (x, y, 1 - c), send))
        return remote, []

    halves = [_sds((N_CHIPS, g.shape[1] // 2, g.shape[2]), g.dtype) for g in grads]
    got = _exchange(name + "_sib", grads, halves, _Plan(plan1, n_t, 0))
    pair = [_pair_add(grads[t], got[t], c_arr, name=f"{name}_pair{t}") for t in range(n_t)]

    def plan2(in_refs, out_refs):
        x, y, c = _me()
        remote = []
        for t in range(n_t):
            src, dst = in_refs[t], out_refs[t]
            rows, rb = src.shape[1], _row_bytes(src)
            for k, (pchip, (px, py)) in enumerate(_chip_peers(x, y)):
                part = [(lambda r0, n, src=src, pchip=pchip: src.at[pchip, pl.ds(r0, n)],
                         lambda r0, n, dst=dst, k=k: dst.at[k, pl.ds(r0, n)], rows, rb)]
                remote.append((src.at[pchip], dst.at[k], (px, py, c), part))
        return remote, []

    recv_shapes = [_sds((N_CHIPS - 1,) + h.shape[1:], h.dtype) for h in halves]
    recv = _exchange(name + "_ici", pair, recv_shapes, _Plan(plan2, 3 * n_t, 0))
    mine = [_sum_own_recv(pair[t], recv[t], chip_arr, name=f"{name}_sum{t}") for t in range(n_t)]

    def plan3(in_refs, out_refs):
        x, y, c = _me()
        remote = []
        for t in range(n_t):
            src, dst = in_refs[t], out_refs[t]
            rows = [(lambda r0, n, src=src: src.at[pl.ds(r0, n)], lambda r0, n, dst=dst: dst.at[pl.ds(r0, n)],
                     src.shape[0], _row_bytes(src))]
            remote.append((src, dst, (x, y, 1 - c), rows))
        return remote, []

    other = _exchange(name + "_swap", mine, [_sds(s.shape, s.dtype) for s in mine], _Plan(plan3, n_t, 0))
    return list(zip(mine, other)), c_arr


def _allreduce_small(pack, *, name):
    r, w = pack.shape

    def body(p_ref, o_ref, buf, send_sems, recv_sems):
        x, y, c = _me()
        me = 4 * x + 2 * y + c
        buf[me] = p_ref[...]
        copies = []
        for k in range(1, N_DEV):
            peer = (x ^ ((k >> 2) & 1), y ^ ((k >> 1) & 1), c ^ (k & 1))
            copies.append(pltpu.make_async_remote_copy(
                src_ref=p_ref, dst_ref=buf.at[me], send_sem=send_sems.at[k - 1], recv_sem=recv_sems.at[k - 1],
                device_id=peer, device_id_type=MESH_ID))
        for cp in copies:
            cp.start()
        for cp in copies:
            cp.wait()
        acc = buf[0]
        for k in range(1, N_DEV):
            acc = acc + buf[k]
        o_ref[...] = acc

    vm = pl.BlockSpec(memory_space=pltpu.VMEM)
    return pl.pallas_call(
        body, name=name, in_specs=[vm], out_specs=vm, out_shape=_sds((r, w), F32),
        scratch_shapes=[pltpu.VMEM((N_DEV, r, w), F32), pltpu.SemaphoreType.DMA((N_DEV - 1,)),
                        pltpu.SemaphoreType.DMA((N_DEV - 1,))],
        compiler_params=_params(12 * r * w * 4),
    )(pack)


def _pack(arrs, row_multiple=16):
    flat = jnp.concatenate([a.reshape(-1).astype(F32) for a in arrs])
    unit = row_multiple * LANES
    total = -(-flat.shape[0] // unit) * unit
    return jnp.pad(flat, (0, total - flat.shape[0])).reshape(total // LANES, LANES)


def _unpack(packed, shapes):
    flat = packed.reshape(-1)
    out, off = [], 0
    for shp in shapes:
        n = int(np.prod(shp))
        out.append(flat[off:off + n].reshape(shp))
        off += n
    return out


def _pad_cols(a, width):
    return jnp.pad(a, [(0, 0)] * (a.ndim - 1) + [(0, width - a.shape[-1])])


ATTN_HEADS_PER_STEP = 4
SMALL_SHARDED = ("a_norm", "a_conv_w", "a_conv_b", "a_ln_g", "a_ln_b", "c_norm", "c_conv_w")
SMALL_REPLICATED = ("b_norm", "b_f_bias", "b_q_norm", "b_k_norm")
BIG = ("a_w_in", "a_w_out", "b_w_in", "b_w_out", "c_w_in", "c_w_out")
WEIGHTS = ("a_norm", "a_w_in", "a_conv_w", "a_conv_b", "a_ln_g", "a_ln_b", "a_w_out", "b_norm", "b_w_in", "b_f_bias",
           "b_q_norm", "b_k_norm", "b_w_out", "c_norm", "c_w_in", "c_conv_w", "c_w_out")


def _mixer_a_fwd(x, p, l, tag):
    h = _rms_fwd(x, p["a_norm"][l][None], name=f"{tag}_rms")
    proj = _mm_nn(h, p["a_w_in"][l], name=f"{tag}_in", tn=p["a_w_in"][l].shape[2])
    u1 = _a_conv_fwd(proj, p["a_conv_w"][l], p["a_conv_b"][l][None], name=f"{tag}_conv")
    gated = _a_post_fwd(u1, proj, p["a_ln_g"][l][None], p["a_ln_b"][l][None], name=f"{tag}_post")
    y = _mm_nn(gated, p["a_w_out"][l], name=f"{tag}_out", tn=1024, add=x)
    return y, (x, h, proj, u1, gated)


def _mixer_a_bwd(dx, saved, p, l, tag):
    x, h, proj, u1, gated = saved
    g = {}
    g["a_w_out"] = _mm_tn(gated, dx, name=f"{tag}_dwout", out_dtype=BF16, tk=2048, tn=1024, ts=1024)
    dgated = _mm_nt(dx, p["a_w_out"][l], name=f"{tag}_dgated", tn=2048, tk=1024)
    du1, dz, g["a_ln_g"], g["a_ln_b"], g["a_conv_b"] = _a_post_bwd(
        dgated, u1, proj, p["a_ln_g"][l][None], p["a_ln_b"][l][None], name=f"{tag}_dpost")
    dproj, g["a_conv_w"] = _a_conv_bwd(du1, proj, dz, p["a_conv_w"][l], name=f"{tag}_dconv")
    g["a_w_in"] = _mm_tn(h, dproj, name=f"{tag}_dwin", out_dtype=BF16, out_width=p["a_w_in"][l].shape[2], ts=2048)
    dh = _mm_nt(dproj, p["a_w_in"][l], name=f"{tag}_dh", ksub=4)
    dx, g["a_norm"] = _rms_bwd(x, p["a_norm"][l][None], [dh], dx, name=f"{tag}_drms")
    return dx, g


def _mixer_c_fwd(x, p, tag):
    h = _rms_fwd(x, p["c_norm"][0][None], name=f"{tag}_rms")
    proj = _mm_nn(h, p["c_w_in"], name=f"{tag}_in", tn=1024)
    gated = _c_mid_fwd(proj, p["c_conv_w"][0], name=f"{tag}_mid")
    y = _mm_nn(gated, p["c_w_out"], name=f"{tag}_out", tn=1024, add=x)
    return y, (x, h, proj, gated)


def _mixer_c_bwd(dx, saved, p, tag):
    x, h, proj, gated = saved
    g = {}
    g["c_w_out"] = _mm_tn(gated, dx, name=f"{tag}_dwout", out_dtype=BF16, tk=2048, tn=1024, ts=1024)
    dgated = _mm_nt(dx, p["c_w_out"], name=f"{tag}_dgated", tn=2048, tk=1024)
    dproj, g["c_conv_w"] = _c_mid_bwd(dgated, proj, p["c_conv_w"][0], name=f"{tag}_dmid")
    g["c_w_in"] = _mm_tn(h, dproj, name=f"{tag}_dwin", out_dtype=BF16, out_width=p["c_w_in"].shape[2], tn=1024, ts=2048)
    dh = _mm_nt(dproj, p["c_w_in"], name=f"{tag}_dh", tk=1024, ksub=2)
    dx, g["c_norm"] = _rms_bwd(x, p["c_norm"][0][None], [dh], dx, name=f"{tag}_drms")
    return dx, g


def _mixer_b_fwd(x, p, tag):
    hg = ATTN_HEADS_PER_STEP
    s_len = x.shape[0]
    n_heads = p["b_f_bias"].shape[1]
    h = _rms_fwd(x, p["b_norm"], name=f"{tag}_rms")
    proj = _mm_nn(h, p["b_wq"], name=f"{tag}_in", tn=1024)
    fl = _mm_nn(h, p["b_wf"], name=f"{tag}_inf", tn=LANES)
    qn, kn, vb = _b_qk_fwd(proj, p["b_q_norm"], p["b_k_norm"], name=f"{tag}_qk")
    bias = _pad_cols(p["b_f_bias"], LANES)
    c = _b_cumsum(fl, bias, name=f"{tag}_cumsum")
    ch = c[:, :n_heads]
    c_row = ch.T.reshape(n_heads, 1, s_len)
    o, o_lo, lse = _fox_fwd(qn, kn, vb, c_row, name=f"{tag}_attn", hg=hg)
    gated = _b_gate_fwd(o, proj, name=f"{tag}_gate")
    y = _mm_nn(gated, p["b_w_out"], name=f"{tag}_out", tn=1024, add=x)
    return y, (x, h, proj, fl, bias, qn, kn, vb, c_row, o, o_lo, lse, gated)


def _mixer_b_bwd(dx, saved, p, tag):
    hg = ATTN_HEADS_PER_STEP
    x, h, proj, fl, bias, qn, kn, vb, c_row, o, o_lo, lse, gated = saved
    s_len = x.shape[0]
    n_heads = p["b_f_bias"].shape[1]
    g = {}
    g["b_w_out"] = _mm_tn(gated, dx, name=f"{tag}_dwout", out_dtype=BF16, tk=2048, tn=1024, ts=1024)
    dgated = _mm_nt(dx, p["b_w_out"], name=f"{tag}_dgated", tn=2048, tk=1024)
    do, dz, delta = _b_gate_bwd(dgated, o, o_lo, proj, name=f"{tag}_dgate", hg=hg)
    dqn, dkn, dv, dc = _fox_bwd(qn, kn, vb, do, lse, delta, c_row, name=f"{tag}_dattn", hg=hg)
    dc_pad = _pad_cols(dc.reshape(n_heads, s_len).T, LANES)
    dfl, dbias = _b_cumsum_bwd(dc_pad, fl, bias, name=f"{tag}_dcumsum")
    g["b_f_bias"] = dbias[:, :n_heads]
    dproj, g["b_q_norm"], g["b_k_norm"] = _b_qk_bwd(dqn, dkn, dv, dz, proj, p["b_q_norm"], p["b_k_norm"], name=f"{tag}_dqk")
    dwq = _mm_tn(h, dproj, name=f"{tag}_dwin", out_dtype=BF16, tn=1024, ts=2048)
    dwf = _mm_tn(h, dfl, name=f"{tag}_dwinf", out_dtype=BF16, tn=LANES)
    g["b_w_in"] = jnp.concatenate([dwq, dwf[:, :n_heads]], axis=1)
    dh = _mm_nt(dproj, p["b_wq"], name=f"{tag}_dh", tk=1024, ksub=2)
    dhf = _mm_nt(dfl, p["b_wf"], name=f"{tag}_dhf", tk=LANES)
    dx, g["b_norm"] = _rms_bwd(x, p["b_norm"], [dh, dhf], dx, name=f"{tag}_drms")
    return dx, g


def kernel(x, a_norm, a_w_in, a_conv_w, a_conv_b, a_ln_g, a_ln_b, a_w_out, b_norm, b_w_in, b_f_bias, b_q_norm, b_k_norm, b_w_out, c_norm, c_w_in, c_conv_w, c_w_out, loss_target, m_a_norm, m_a_w_in, m_a_conv_w, m_a_conv_b, m_a_ln_g, m_a_ln_b, m_a_w_out, m_b_norm, m_b_w_in, m_b_f_bias, m_b_q_norm, m_b_k_norm, m_b_w_out, m_c_norm, m_c_w_in, m_c_conv_w, m_c_w_out, v_a_norm, v_a_w_in, v_a_conv_w, v_a_conv_b, v_a_ln_g, v_a_ln_b, v_a_w_out, v_b_norm, v_b_w_in, v_b_f_bias, v_b_q_norm, v_b_k_norm, v_b_w_out, v_c_norm, v_c_w_in, v_c_conv_w, v_c_w_out):
    w_loc = dict(a_norm=a_norm, a_w_in=a_w_in, a_conv_w=a_conv_w, a_conv_b=a_conv_b, a_ln_g=a_ln_g, a_ln_b=a_ln_b,
                 a_w_out=a_w_out, b_norm=b_norm, b_w_in=b_w_in, b_f_bias=b_f_bias, b_q_norm=b_q_norm, b_k_norm=b_k_norm,
                 b_w_out=b_w_out, c_norm=c_norm, c_w_in=c_w_in, c_conv_w=c_conv_w, c_w_out=c_w_out)
    m_loc = dict(a_norm=m_a_norm, a_w_in=m_a_w_in, a_conv_w=m_a_conv_w, a_conv_b=m_a_conv_b, a_ln_g=m_a_ln_g,
                 a_ln_b=m_a_ln_b, a_w_out=m_a_w_out, b_norm=m_b_norm, b_w_in=m_b_w_in, b_f_bias=m_b_f_bias,
                 b_q_norm=m_b_q_norm, b_k_norm=m_b_k_norm, b_w_out=m_b_w_out, c_norm=m_c_norm, c_w_in=m_c_w_in,
                 c_conv_w=m_c_conv_w, c_w_out=m_c_w_out)
    v_loc = dict(a_norm=v_a_norm, a_w_in=v_a_w_in, a_conv_w=v_a_conv_w, a_conv_b=v_a_conv_b, a_ln_g=v_a_ln_g,
                 a_ln_b=v_a_ln_b, a_w_out=v_a_w_out, b_norm=v_b_norm, b_w_in=v_b_w_in, b_f_bias=v_b_f_bias,
                 b_q_norm=v_b_q_norm, b_k_norm=v_b_k_norm, b_w_out=v_b_w_out, c_norm=v_c_norm, c_w_in=v_c_w_in,
                 c_conv_w=v_c_conv_w, c_w_out=v_c_w_out)
    n_a = a_w_in.shape[0]
    d_model = x.shape[2]
    e_inner = a_w_out.shape[1] * N_CHIPS
    n_heads = b_f_bias.shape[1]
    nb_loc = b_w_in.shape[2]
    nb_pad = -(-nb_loc // LANES) * LANES
    chip = 2 * lax.axis_index("x") + lax.axis_index("y")

    big_shards = ([a_w_in[l].astype(BF16) for l in range(n_a)] + [a_w_out[l].astype(BF16) for l in range(n_a)]
                  + [_pad_cols(b_w_in[0], nb_pad).astype(BF16), b_w_out[0].astype(BF16), c_w_in[0].astype(BF16),
                     c_w_out[0].astype(BF16)])
    small_pack = _pack([w_loc[n] for n in SMALL_SHARDED])
    gathered = _gather_weights(big_shards + [small_pack], name="gather_weights")
    p = {}
    p["a_w_in"] = gathered[0:n_a]
    p["a_w_out"] = [g.reshape(e_inner, d_model) for g in gathered[n_a:2 * n_a]]
    gb, gbo, gci, gco, gsmall = gathered[2 * n_a:]
    wb_full = jnp.concatenate([gb[k, :, :nb_loc] for k in range(N_CHIPS)], axis=1)
    p["b_wq"] = wb_full[:, :4 * e_inner]
    p["b_wf"] = _pad_cols(wb_full[:, 4 * e_inner:], LANES)
    p["b_w_out"] = gbo.reshape(e_inner, d_model)
    p["c_w_in"] = gci
    p["c_w_out"] = gco.reshape(e_inner, d_model)
    small_shapes = [w_loc[n].shape for n in SMALL_SHARDED]
    per_chip = [_unpack(gsmall[k], small_shapes) for k in range(N_CHIPS)]
    for idx, n in enumerate(SMALL_SHARDED):
        p[n] = jnp.concatenate([per_chip[k][idx] for k in range(N_CHIPS)], axis=-1)
    for n in SMALL_REPLICATED:
        p[n] = w_loc[n]

    x0 = x[0]
    x1, sv0 = _mixer_a_fwd(x0, p, 0, "a0")
    x2, sv1 = _mixer_b_fwd(x1, p, "b0")
    x3, sv2 = _mixer_c_fwd(x2, p, "c0")
    x4, sv3 = _mixer_a_fwd(x3, p, 1, "a1")
    dy, loss_part = _loss_head(x4, loss_target[0], name="loss_head")
    loss = lax.psum(loss_part[0, 0], ("x", "y", "c"))

    dx, g3 = _mixer_a_bwd(dy, sv3, p, 1, "a1")
    dx, g2 = _mixer_c_bwd(dx, sv2, p, "c0")
    dx, g1 = _mixer_b_bwd(dx, sv1, p, "b0")
    dx, g0 = _mixer_a_bwd(dx, sv0, p, 0, "a0")
    grad_x = dx[None]

    half_rows = e_inner // N_CHIPS
    gb_full = g1["b_w_in"].reshape(d_model, N_CHIPS, nb_loc).transpose(1, 0, 2)
    big_grads = ([g0["a_w_in"], g3["a_w_in"]]
                 + [g0["a_w_out"].reshape(N_CHIPS, half_rows, d_model), g3["a_w_out"].reshape(N_CHIPS, half_rows, d_model)]
                 + [_pad_cols(gb_full, nb_pad), g1["b_w_out"].reshape(N_CHIPS, half_rows, d_model), g2["c_w_in"],
                    g2["c_w_out"].reshape(N_CHIPS, half_rows, d_model)])
    red, c_arr = _reduce_grads(big_grads, name="reduce_grads")
    big_order = [("a_w_in", 0), ("a_w_in", 1), ("a_w_out", 0), ("a_w_out", 1), ("b_w_in", 0), ("b_w_out", 0),
                 ("c_w_in", 0), ("c_w_out", 0)]

    grads, delta, new_m, new_v = {}, {}, {}, {}
    per_layer = {n: [] for n in BIG}
    for (n, l), (mine, other) in zip(big_order, red):
        cols = mine.shape[1]
        true_cols = w_loc[n].shape[-1]
        outs = _adamw_halves(_pad_cols(w_loc[n][l], cols), mine, other, _pad_cols(m_loc[n][l], cols),
                             _pad_cols(v_loc[n][l], cols), c_arr, name=f"adamw_{n}{l}")
        per_layer[n].append([a[:, :true_cols] for a in outs])
    for n in BIG:
        grads[n], delta[n], new_m[n], new_v[n] = [jnp.stack([layer[k] for layer in per_layer[n]]) for k in range(4)]

    small_full = {}
    for n in ("a_norm", "a_conv_w", "a_conv_b", "a_ln_g", "a_ln_b"):
        small_full[n] = jnp.stack([g0[n].reshape(p[n].shape[1:]), g3[n].reshape(p[n].shape[1:])])
    small_full["c_norm"] = g2["c_norm"].reshape(p["c_norm"].shape)
    small_full["c_conv_w"] = g2["c_conv_w"].reshape(p["c_conv_w"].shape)
    for n in SMALL_REPLICATED:
        small_full[n] = g1[n].reshape(w_loc[n].shape)
    small_names = SMALL_SHARDED + SMALL_REPLICATED
    summed = _unpack(_allreduce_small(_pack([small_full[n] for n in small_names], 8), name="reduce_small"),
                     [small_full[n].shape for n in small_names])
    for n, s in zip(small_names, summed):
        if n in SMALL_SHARDED:
            width = w_loc[n].shape[-1]
            grads[n] = lax.dynamic_slice_in_dim(s, chip * width, width, axis=s.ndim - 1)
        else:
            grads[n] = s

    small_shapes_all = [w_loc[n].shape for n in small_names]
    _, d_, m_, v_ = _adamw(_pack([w_loc[n] for n in small_names], 8), [_pack([grads[n] for n in small_names], 8)],
                           _pack([m_loc[n] for n in small_names], 8), _pack([v_loc[n] for n in small_names], 8),
                           name="adamw_small")
    for n, a, b, c_ in zip(small_names, _unpack(d_, small_shapes_all), _unpack(m_, small_shapes_all),
                           _unpack(v_, small_shapes_all)):
        delta[n], new_m[n], new_v[n] = a, b, c_

    return (loss, grad_x, *[grads[n] for n in WEIGHTS], *[delta[n] for n in WEIGHTS],
            *[new_m[n] for n in WEIGHTS], *[new_v[n] for n in WEIGHTS])
```

```python
import functools

import numpy as np
import jax
import jax.numpy as jnp
from jax import lax
from jax.experimental import pallas as pl
from jax.experimental.pallas import tpu as pltpu

F32 = jnp.float32
BF16 = jnp.bfloat16
NORM_EPS = 1e-6
HEAD_DIM = 128
LANES = 128
N_CHIPS = 4
N_DEV = 8
VMEM_CAP = 56 << 20
MESH_ID = pl.DeviceIdType.MESH

ADAM_LR = 0.001
ADAM_B1 = 0.9
ADAM_B2 = 0.999
ADAM_EPS = 1e-08
ADAM_WD = 0.01
ADAM_STEP = 10


def _sds(shape, dtype):
    return jax.ShapeDtypeStruct(tuple(shape), dtype)


def _tile(n, pref):
    if n <= pref:
        return n
    for t in range(pref - pref % 8, 0, -8):
        if n % t == 0:
            return t
    raise ValueError(f"no tile for {n} under {pref}")


def _params(vmem_bytes):
    return pltpu.CompilerParams(vmem_limit_bytes=int(min(max(vmem_bytes, 16 << 20), VMEM_CAP)))


def _sigmoid(v):
    return 1.0 / (1.0 + jnp.exp(-v))


def _silu(v):
    return v * _sigmoid(v)


def _silu_pair(v):
    s = _sigmoid(v)
    return v * s, s * (1.0 + v * (1.0 - s))


def _rowsum8(v):
    r, c = v.shape
    return jnp.sum(v.reshape(r // 8, 8, c), axis=0)


def _col_spec(grouped, rows_block, tile, width, row_of, col_of):
    if grouped:
        per = width // tile
        return pl.BlockSpec((None, rows_block, tile), lambda *ids: (col_of(*ids) // per, row_of(*ids), col_of(*ids) % per))
    return pl.BlockSpec((rows_block, tile), lambda *ids: (row_of(*ids), col_of(*ids)))


def _mm_nn(a, b, *, name, tm=1024, tn=512, out_dtype=F32, add=None):
    m, k = a.shape
    b_grouped = b.ndim == 3
    n = b.shape[0] * b.shape[2] if b_grouped else b.shape[1]
    width = b.shape[2] if b_grouped else n
    tm, tn = _tile(m, tm), _tile(width, tn)

    def body(*refs):
        if add is None:
            a_ref, b_ref, o_ref = refs
        else:
            a_ref, b_ref, r_ref, o_ref = refs
        acc = jnp.dot(a_ref[...].astype(BF16), b_ref[...].astype(BF16), preferred_element_type=F32)
        if add is not None:
            acc = acc + r_ref[...]
        o_ref[...] = acc.astype(o_ref.dtype)

    in_specs = [pl.BlockSpec((tm, k), lambda i, j: (i, 0)),
                _col_spec(b_grouped, k, tn, width, lambda i, j: 0, lambda i, j: j)]
    args = [a, b]
    if add is not None:
        in_specs.append(pl.BlockSpec((tm, tn), lambda i, j: (i, j)))
        args.append(add)
    vmem = 2 * (tm * k * a.dtype.itemsize + k * tn * 2 + tm * tn * 4 * (2 if add is not None else 1)) + tm * tn * 8 + tm * k * 2
    return pl.pallas_call(
        body, name=name, grid=(m // tm, n // tn), in_specs=in_specs,
        out_specs=pl.BlockSpec((tm, tn), lambda i, j: (i, j)),
        out_shape=_sds((m, n), out_dtype), compiler_params=_params(vmem + (4 << 20)),
    )(*args)


def _mm_nt(a, b, *, name, tm=1024, tn=1024, tk=512, ksub=1):
    a_grouped, b_grouped = a.ndim == 3, b.ndim == 3
    m = a.shape[1] if a_grouped else a.shape[0]
    n = a.shape[0] * a.shape[2] if a_grouped else a.shape[1]
    kk = b.shape[1] if b_grouped else b.shape[0]
    wa = a.shape[2] if a_grouped else n
    wb = b.shape[2] if b_grouped else n
    tm, tn = _tile(m, tm), _tile(kk, tn)
    tk = _tile(int(np.gcd(wa, wb)), tk)
    ksub = min(ksub, n // tk)
    assert (n // tk) % ksub == 0, (n, tk, ksub)
    steps = n // (tk * ksub)

    def body(*refs):
        a_refs, b_refs, o_ref = refs[:ksub], refs[ksub:2 * ksub], refs[2 * ksub]
        part = None
        for a_ref, b_ref in zip(a_refs, b_refs):
            d = lax.dot_general(a_ref[...].astype(BF16), b_ref[...].astype(BF16), (((1,), (1,)), ((), ())),
                                preferred_element_type=F32)
            part = d if part is None else part + d

        @pl.when(pl.program_id(2) == 0)
        def _():
            o_ref[...] = part

        @pl.when(pl.program_id(2) > 0)
        def _():
            o_ref[...] += part

    def sub(u):
        return lambda i, j, s: s * ksub + u

    in_specs = ([_col_spec(a_grouped, tm, tk, wa, lambda i, j, s: i, sub(u)) for u in range(ksub)]
                + [_col_spec(b_grouped, tn, tk, wb, lambda i, j, s: j, sub(u)) for u in range(ksub)])
    vmem = (2 * ksub * (tm * tk * a.dtype.itemsize + tn * tk * b.dtype.itemsize) + 2 * tm * tn * 4 + 2 * tm * tn * 4
            + (tm + tn) * tk * 2)
    return pl.pallas_call(
        body, name=name, grid=(m // tm, kk // tn, steps), in_specs=in_specs,
        out_specs=pl.BlockSpec((tm, tn), lambda i, j, s: (i, j)),
        out_shape=_sds((m, kk), F32), compiler_params=_params(vmem + (4 << 20)),
    )(*([a] * ksub), *([b] * ksub))


def _mm_tn(a, b, *, name, out_width=None, out_dtype=F32, tk=1024, tn=512, ts=512):
    s_len, k = a.shape
    b_grouped = b.ndim == 3
    n = b.shape[0] * b.shape[2] if b_grouped else b.shape[1]
    wb = b.shape[2] if b_grouped else n
    wo = out_width if out_width is not None else n
    tk, ts = _tile(k, tk), _tile(s_len, ts)
    tn = _tile(int(np.gcd(wb, wo)), tn)
    last = s_len // ts - 1
    direct = out_dtype == F32

    def body(a_ref, b_ref, o_ref, *scratch):
        acc = o_ref if direct else scratch[0]
        part = lax.dot_general(a_ref[...].astype(BF16), b_ref[...].astype(BF16), (((0,), (0,)), ((), ())),
                               preferred_element_type=F32)

        @pl.when(pl.program_id(2) == 0)
        def _():
            acc[...] = part

        @pl.when(pl.program_id(2) > 0)
        def _():
            acc[...] += part

        if not direct:
            @pl.when(pl.program_id(2) == last)
            def _():
                o_ref[...] = acc[...].astype(o_ref.dtype)

    in_specs = [pl.BlockSpec((ts, tk), lambda i, j, s: (s, i)),
                _col_spec(b_grouped, ts, tn, wb, lambda i, j, s: s, lambda i, j, s: j)]
    out_grouped = out_width is not None
    out_spec = _col_spec(out_grouped, tk, tn, wo, lambda i, j, s: i, lambda i, j, s: j)
    out_shape = _sds((n // wo, k, wo), out_dtype) if out_grouped else _sds((k, n), out_dtype)
    vmem = 2 * (ts * tk * a.dtype.itemsize + ts * tn * b.dtype.itemsize + tk * tn * 4) + 2 * tk * tn * 4 + ts * (tk + tn) * 4
    return pl.pallas_call(
        body, name=name, grid=(k // tk, n // tn, s_len // ts), in_specs=in_specs, out_specs=out_spec,
        out_shape=out_shape, scratch_shapes=[] if direct else [pltpu.VMEM((tk, tn), F32)],
        compiler_params=_params(vmem + (4 << 20)),
    )(a, b)


def _rms_fwd(x, g, *, name, ts=512):
    s_len, d = x.shape
    ts = _tile(s_len, ts)

    def body(x_ref, g_ref, h_ref):
        xf = x_ref[...]
        r = lax.rsqrt(jnp.mean(xf * xf, axis=-1, keepdims=True) + NORM_EPS)
        h_ref[...] = ((xf * r) * g_ref[...]).astype(h_ref.dtype)

    return pl.pallas_call(
        body, name=name, grid=(s_len // ts,),
        in_specs=[pl.BlockSpec((ts, d), lambda i: (i, 0)), pl.BlockSpec((1, d), lambda i: (0, 0))],
        out_specs=pl.BlockSpec((ts, d), lambda i: (i, 0)), out_shape=_sds((s_len, d), BF16),
        compiler_params=_params(8 * ts * d * 4),
    )(x, g)


def _rms_bwd(x, g, dhs, dres, *, name, ts=512):
    s_len, d = x.shape
    ts = _tile(s_len, ts)
    n_dh = len(dhs)
    last = s_len // ts - 1

    def body(*refs):
        x_ref, g_ref = refs[0], refs[1]
        dh_refs = refs[2:2 + n_dh]
        dres_ref, dx_ref, dg_ref, acc = refs[2 + n_dh:]
        i = pl.program_id(0)

        @pl.when(i == 0)
        def _():
            acc[...] = jnp.zeros_like(acc)

        xf = x_ref[...]
        dy = dh_refs[0][...]
        for r_ in dh_refs[1:]:
            dy = dy + r_[...]
        r = lax.rsqrt(jnp.mean(xf * xf, axis=-1, keepdims=True) + NORM_EPS)
        gd = dy * g_ref[...]
        dot = jnp.mean(xf * gd, axis=-1, keepdims=True)
        dx_ref[...] = dres_ref[...] + r * gd - xf * (r * r * r * dot)
        acc[...] += _rowsum8(dy * (xf * r))

        @pl.when(i == last)
        def _():
            dg_ref[...] = jnp.sum(acc[...], axis=0, keepdims=True)

    row = pl.BlockSpec((ts, d), lambda i: (i, 0))
    vec = pl.BlockSpec((1, d), lambda i: (0, 0))
    return pl.pallas_call(
        body, name=name, grid=(s_len // ts,),
        in_specs=[row, vec] + [row] * n_dh + [row],
        out_specs=[row, vec], out_shape=[_sds((s_len, d), F32), _sds((1, d), F32)],
        scratch_shapes=[pltpu.VMEM((8, d), F32)],
        compiler_params=_params((2 * (3 + n_dh) + 6) * ts * d * 4),
    )(x, g, *dhs, dres)


def _loss_head(y, target, *, name, ts=512):
    s_len, d = y.shape
    ts = _tile(s_len, ts)
    last = s_len // ts - 1

    def body(y_ref, t_ref, dy_ref, loss_ref, acc):
        i = pl.program_id(0)

        @pl.when(i == 0)
        def _():
            acc[...] = jnp.zeros_like(acc)

        err = y_ref[...] - t_ref[...]
        dy_ref[...] = err / d
        acc[...] += _rowsum8(err * err)

        @pl.when(i == last)
        def _():
            loss_ref[...] = (0.5 * jnp.sum(acc[...]) / d).reshape(1, 1)

    row = pl.BlockSpec((ts, d), lambda i: (i, 0))
    return pl.pallas_call(
        body, name=name, grid=(s_len // ts,), in_specs=[row, row],
        out_specs=[row, pl.BlockSpec((1, 1), lambda i: (0, 0))],
        out_shape=[_sds((s_len, d), F32), _sds((1, 1), F32)],
        scratch_shapes=[pltpu.VMEM((8, d), F32)],
        compiler_params=_params(10 * ts * d * 4),
    )(y, target)


CONV_ROWS = 64
CONV_COLS = 256


SUBLANES = 8


def _fill_shifts(buf, sh_ref):
    n = sh_ref.shape[1]
    for s in range(1, SUBLANES):
        sh_ref[s - 1, :, :] = buf[s:s + n, :]


def _shifted_rows(buf, sh_ref, start, rw, cs):
    s = start % SUBLANES
    if sh_ref is None or s == 0:
        return buf[start:start + rw, cs]
    return sh_ref[s - 1, start - s:start - s + rw, cs]


def _conv_taps(buf, sh_ref, w_ref, k_width, base, ts, tc, init, emit, reverse=False):
    cw = min(tc, CONV_COLS)
    rw = min(ts, CONV_ROWS)
    for cb in range(tc // cw):
        cs = slice(cb * cw, (cb + 1) * cw)
        for rb in range(ts // rw):
            acc = init(slice(rb * rw, (rb + 1) * rw), cs, (rw, cw))
            for k in range(k_width):
                sh = (k_width - 1 - k) if reverse else k
                acc = acc + w_ref[k:k + 1, cs] * _shifted_rows(buf, sh_ref, base + rb * rw + sh, rw, cs)
            emit(slice(rb * rw, (rb + 1) * rw), cs, acc)


def _conv_wgrad(buf, sh_ref, d_ref_val, acc_ref, k_width, base, ts, tc):
    cw = min(tc, CONV_COLS)
    rw = min(ts, CONV_ROWS)
    for cb in range(tc // cw):
        cs = slice(cb * cw, (cb + 1) * cw)
        for rb in range(ts // rw):
            dv = d_ref_val[rb * rw:(rb + 1) * rw, cs]
            for k in range(k_width):
                prod = dv * _shifted_rows(buf, sh_ref, base + rb * rw + k, rw, cs)
                acc_ref[8 * k:8 * k + 8, cs] += _rowsum8(prod)


A_HALO = 32


def _a_conv_fwd(proj, conv_w, conv_b, *, name, ts=1024, tc=128):
    s_len, e3 = proj.shape
    e = e3 // 3
    k_width = conv_w.shape[0]
    ts, tc = _tile(s_len, ts), _tile(e, tc)
    nc = e // tc
    kp = 32

    def body(val, gate, valh, gateh, w_ref, b_ref, u1_ref, buf, sh):
        i = pl.program_id(1)
        u0h = valh[...] * _sigmoid(gateh[...])
        buf[0:A_HALO, :] = jnp.where(i > 0, u0h, 0.0)
        buf[A_HALO:A_HALO + ts, :] = val[...] * _sigmoid(gate[...])
        _fill_shifts(buf, sh)

        def init(rows, cs, shape):
            return jnp.broadcast_to(b_ref[:, cs], shape)

        def emit(rows, cs, acc):
            u1_ref[rows, cs] = acc

        _conv_taps(buf, sh, w_ref, k_width, A_HALO - (k_width - 1), ts, tc, init, emit)

    hb = ts // A_HALO
    in_specs = [
        pl.BlockSpec((ts, tc), lambda j, i: (i, j)),
        pl.BlockSpec((ts, tc), lambda j, i: (i, nc + j)),
        pl.BlockSpec((A_HALO, tc), lambda j, i: (jnp.maximum(i * hb - 1, 0), j)),
        pl.BlockSpec((A_HALO, tc), lambda j, i: (jnp.maximum(i * hb - 1, 0), nc + j)),
        pl.BlockSpec((kp, tc), lambda j, i: (0, j)),
        pl.BlockSpec((1, tc), lambda j, i: (0, j)),
    ]
    w_pad = jnp.zeros((kp, e), F32).at[:k_width].set(conv_w)
    return pl.pallas_call(
        body, name=name, grid=(nc, s_len // ts), in_specs=in_specs,
        out_specs=pl.BlockSpec((ts, tc), lambda j, i: (i, j)), out_shape=_sds((s_len, e), F32),
        scratch_shapes=[pltpu.VMEM((A_HALO + ts, tc), F32), pltpu.VMEM((SUBLANES - 1, A_HALO + ts - SUBLANES, tc), F32)],
        compiler_params=_params(24 * ts * tc * 4),
    )(proj, proj, proj, proj, w_pad, conv_b)


def _ln_rows(u1, g, b):
    mu = jnp.mean(u1, axis=-1, keepdims=True)
    xc = u1 - mu
    var = jnp.mean(xc * xc, axis=-1, keepdims=True)
    rstd = lax.rsqrt(var + NORM_EPS)
    xhat = xc * rstd
    return xhat, rstd, xhat * g + b


def _a_post_fwd(u1, proj, ln_g, ln_b, *, name, ts=256):
    s_len, e = u1.shape
    ts = _tile(s_len, ts)

    def body(u1_ref, z_ref, g_ref, b_ref, o_ref):
        _, _, u2 = _ln_rows(u1_ref[...], g_ref[...], b_ref[...])
        o_ref[...] = (_silu(u2) * _silu(z_ref[...])).astype(o_ref.dtype)

    row = pl.BlockSpec((ts, e), lambda i: (i, 0))
    vec = pl.BlockSpec((1, e), lambda i: (0, 0))
    return pl.pallas_call(
        body, name=name, grid=(s_len // ts,),
        in_specs=[row, pl.BlockSpec((ts, e), lambda i: (i, 2)), vec, vec],
        out_specs=row, out_shape=_sds((s_len, e), BF16), compiler_params=_params(12 * ts * e * 4),
    )(u1, proj, ln_g, ln_b)


def _a_post_bwd(dgated, u1, proj, ln_g, ln_b, *, name, ts=256):
    s_len, e = u1.shape
    ts = _tile(s_len, ts)
    last = s_len // ts - 1

    def body(dg_ref, u1_ref, z_ref, g_ref, b_ref, du1_ref, dz_ref, dlg_ref, dlb_ref, dcb_ref, a_g, a_b, a_c):
        i = pl.program_id(0)

        @pl.when(i == 0)
        def _():
            a_g[...] = jnp.zeros_like(a_g)
            a_b[...] = jnp.zeros_like(a_b)
            a_c[...] = jnp.zeros_like(a_c)

        g = g_ref[...]
        xhat, rstd, u2 = _ln_rows(u1_ref[...], g, b_ref[...])
        z = z_ref[...]
        dgt = dg_ref[...]
        su, dsu = _silu_pair(u2)
        sz, dsz = _silu_pair(z)
        dz_ref[...] = (dgt * su * dsz).astype(dz_ref.dtype)
        du2 = dgt * sz * dsu
        a_g[...] += _rowsum8(du2 * xhat)
        a_b[...] += _rowsum8(du2)
        dxh = du2 * g
        m1 = jnp.mean(dxh, axis=-1, keepdims=True)
        m2 = jnp.mean(dxh * xhat, axis=-1, keepdims=True)
        du1 = rstd * (dxh - m1 - xhat * m2)
        du1_ref[...] = du1
        a_c[...] += _rowsum8(du1)

        @pl.when(i == last)
        def _():
            dlg_ref[...] = jnp.sum(a_g[...], axis=0, keepdims=True)
            dlb_ref[...] = jnp.sum(a_b[...], axis=0, keepdims=True)
            dcb_ref[...] = jnp.sum(a_c[...], axis=0, keepdims=True)

    row = pl.BlockSpec((ts, e), lambda i: (i, 0))
    vec = pl.BlockSpec((1, e), lambda i: (0, 0))
    return pl.pallas_call(
        body, name=name, grid=(s_len // ts,),
        in_specs=[row, row, pl.BlockSpec((ts, e), lambda i: (i, 2)), vec, vec],
        out_specs=[row, row, vec, vec, vec],
        out_shape=[_sds((s_len, e), F32), _sds((s_len, e), BF16), _sds((1, e), F32), _sds((1, e), F32), _sds((1, e), F32)],
        scratch_shapes=[pltpu.VMEM((8, e), F32)] * 3,
        compiler_params=_params(20 * ts * e * 4),
    )(dgated, u1, proj, ln_g, ln_b)


def _a_conv_bwd(du1, proj, dz, conv_w, *, name, ts=1024, tc=128):
    s_len, e3 = proj.shape
    e = e3 // 3
    k_width = conv_w.shape[0]
    ts, tc = _tile(s_len, ts), _tile(e, tc)
    nc, nr = e // tc, s_len // ts
    kp = 32
    hb = ts // A_HALO

    def body(val, gate, valh, gateh, d_ref, dh_ref, dz_ref, w_ref, dp_ref, dw_ref, buf_u, buf_d, du0, acc, sh_u, sh_d):
        i = pl.program_id(1)

        @pl.when(i == 0)
        def _():
            acc[...] = jnp.zeros_like(acc)

        buf_u[0:A_HALO, :] = jnp.where(i > 0, valh[...] * _sigmoid(gateh[...]), 0.0)
        buf_u[A_HALO:A_HALO + ts, :] = val[...] * _sigmoid(gate[...])
        buf_d[0:ts, :] = d_ref[...]
        buf_d[ts:ts + A_HALO, :] = jnp.where(i < nr - 1, dh_ref[...], 0.0)
        _fill_shifts(buf_u, sh_u)
        _fill_shifts(buf_d, sh_d)

        def init(rows, cs, shape):
            return jnp.zeros(shape, F32)

        def emit(rows, cs, a):
            du0[rows, cs] = a

        _conv_taps(buf_d, sh_d, w_ref, k_width, 0, ts, tc, init, emit, reverse=True)
        _conv_wgrad(buf_u, sh_u, buf_d, acc, k_width, A_HALO - (k_width - 1), ts, tc)
        rw = min(ts, CONV_ROWS)
        for rb in range(ts // rw):
            rows = slice(rb * rw, (rb + 1) * rw)
            d0 = du0[rows, :]
            sg = _sigmoid(gate[rows, :])
            dp_ref[0, rows, :] = (d0 * sg).astype(dp_ref.dtype)
            dp_ref[1, rows, :] = (d0 * val[rows, :] * sg * (1.0 - sg)).astype(dp_ref.dtype)
        dp_ref[2] = dz_ref[...]

        @pl.when(i == nr - 1)
        def _():
            for k in range(kp):
                dw_ref[k:k + 1, :] = jnp.sum(acc[8 * k:8 * k + 8, :], axis=0, keepdims=True)

    in_specs = [
        pl.BlockSpec((ts, tc), lambda j, i: (i, j)),
        pl.BlockSpec((ts, tc), lambda j, i: (i, nc + j)),
        pl.BlockSpec((A_HALO, tc), lambda j, i: (jnp.maximum(i * hb - 1, 0), j)),
        pl.BlockSpec((A_HALO, tc), lambda j, i: (jnp.maximum(i * hb - 1, 0), nc + j)),
        pl.BlockSpec((ts, tc), lambda j, i: (i, j)),
        pl.BlockSpec((A_HALO, tc), lambda j, i: (jnp.minimum((i + 1) * hb, nr * hb - 1), j)),
        pl.BlockSpec((ts, tc), lambda j, i: (i, j)),
        pl.BlockSpec((kp, tc), lambda j, i: (0, j)),
    ]
    w_pad = jnp.zeros((kp, e), F32).at[:k_width].set(conv_w)
    dproj, dw = pl.pallas_call(
        body, name=name, grid=(nc, nr), in_specs=in_specs,
        out_specs=[pl.BlockSpec((3, ts, tc), lambda j, i: (0, i, j)), pl.BlockSpec((kp, tc), lambda j, i: (0, j))],
        out_shape=[_sds((3, s_len, e), BF16), _sds((kp, e), F32)],
        scratch_shapes=[pltpu.VMEM((A_HALO + ts, tc), F32), pltpu.VMEM((ts + A_HALO, tc), F32),
                        pltpu.VMEM((ts, tc), F32), pltpu.VMEM((8 * kp, tc), F32),
                        pltpu.VMEM((SUBLANES - 1, A_HALO + ts - SUBLANES, tc), F32),
                        pltpu.VMEM((SUBLANES - 1, A_HALO + ts - SUBLANES, tc), F32)],
        compiler_params=_params(56 * ts * tc * 4),
    )(proj, proj, proj, proj, du1, du1, dz, w_pad)
    return dproj, dw[:k_width]


C_HALO = 8


def _c_mid_fwd(proj, conv_w, *, name, ts=1024, tc=128):
    s_len, e4 = proj.shape
    e = e4 // 4
    k_width = conv_w.shape[0]
    ts, tc = _tile(s_len, ts), _tile(e, tc)
    nc = e // tc
    hb = ts // C_HALO

    def body(u, bg, cg, z, uh, cgh, w_ref, o_ref, buf, y):
        i = pl.program_id(1)
        buf[0:C_HALO, :] = jnp.where(i > 0, uh[...] * cgh[...], 0.0)
        buf[C_HALO:C_HALO + ts, :] = u[...] * cg[...]

        def init(rows, cs, shape):
            return jnp.zeros(shape, F32)

        def emit(rows, cs, a):
            y[rows, cs] = a

        _conv_taps(buf, None, w_ref, k_width, C_HALO - (k_width - 1), ts, tc, init, emit)
        o_ref[...] = (bg[...] * y[...] * _silu(z[...])).astype(o_ref.dtype)

    def grp(g):
        return pl.BlockSpec((ts, tc), lambda j, i: (i, g * nc + j))

    def halo(g):
        return pl.BlockSpec((C_HALO, tc), lambda j, i: (jnp.maximum(i * hb - 1, 0), g * nc + j))

    w_pad = jnp.zeros((8, e), F32).at[:k_width].set(conv_w)
    return pl.pallas_call(
        body, name=name, grid=(nc, s_len // ts),
        in_specs=[grp(0), grp(1), grp(2), grp(3), halo(0), halo(2), pl.BlockSpec((8, tc), lambda j, i: (0, j))],
        out_specs=pl.BlockSpec((ts, tc), lambda j, i: (i, j)), out_shape=_sds((s_len, e), BF16),
        scratch_shapes=[pltpu.VMEM((C_HALO + ts, tc), F32), pltpu.VMEM((ts, tc), F32)],
        compiler_params=_params(16 * ts * tc * 4),
    )(proj, proj, proj, proj, proj, proj, w_pad)


def _c_mid_bwd(dgated, proj, conv_w, *, name, ts=1024, tc=128):
    s_len, e4 = proj.shape
    e = e4 // 4
    k_width = conv_w.shape[0]
    ts, tc = _tile(s_len, ts), _tile(e, tc)
    nc, nr = e // tc, s_len // ts
    hb = ts // C_HALO

    def body(u, bg, cg, z, uh, cgh, dg, dgh, bgh, zh, w_ref, dp_ref, dw_ref, buf_p, buf_d, y, dpv, acc):
        i = pl.program_id(1)

        @pl.when(i == 0)
        def _():
            acc[...] = jnp.zeros_like(acc)

        uv, bgv, cgv, zv, dgv = u[...], bg[...], cg[...], z[...], dg[...]
        buf_p[0:C_HALO, :] = jnp.where(i > 0, uh[...] * cgh[...], 0.0)
        buf_p[C_HALO:C_HALO + ts, :] = uv * cgv
        sz, dsz = _silu_pair(zv)
        buf_d[0:ts, :] = dgv * sz * bgv
        buf_d[ts:ts + C_HALO, :] = jnp.where(i < nr - 1, dgh[...] * _silu(zh[...]) * bgh[...], 0.0)

        def init(rows, cs, shape):
            return jnp.zeros(shape, F32)

        def emit_y(rows, cs, a):
            y[rows, cs] = a

        def emit_dp(rows, cs, a):
            dpv[rows, cs] = a

        _conv_taps(buf_p, None, w_ref, k_width, C_HALO - (k_width - 1), ts, tc, init, emit_y)
        _conv_taps(buf_d, None, w_ref, k_width, 0, ts, tc, init, emit_dp, reverse=True)
        _conv_wgrad(buf_p, None, buf_d, acc, k_width, C_HALO - (k_width - 1), ts, tc)
        yv, dp = y[...], dpv[...]
        dp_ref[0] = (dp * cgv).astype(dp_ref.dtype)
        dp_ref[1] = (dgv * sz * yv).astype(dp_ref.dtype)
        dp_ref[2] = (dp * uv).astype(dp_ref.dtype)
        dp_ref[3] = (dgv * bgv * yv * dsz).astype(dp_ref.dtype)

        @pl.when(i == nr - 1)
        def _():
            for k in range(8):
                dw_ref[k:k + 1, :] = jnp.sum(acc[8 * k:8 * k + 8, :], axis=0, keepdims=True)

    def grp(g):
        return pl.BlockSpec((ts, tc), lambda j, i: (i, g * nc + j))

    def prev(g):
        return pl.BlockSpec((C_HALO, tc), lambda j, i: (jnp.maximum(i * hb - 1, 0), g * nc + j))

    def nxt(g):
        return pl.BlockSpec((C_HALO, tc), lambda j, i: (jnp.minimum((i + 1) * hb, nr * hb - 1), g * nc + j))

    w_pad = jnp.zeros((8, e), F32).at[:k_width].set(conv_w)
    dproj, dw = pl.pallas_call(
        body, name=name, grid=(nc, nr),
        in_specs=[grp(0), grp(1), grp(2), grp(3), prev(0), prev(2),
                  pl.BlockSpec((ts, tc), lambda j, i: (i, j)),
                  pl.BlockSpec((C_HALO, tc), lambda j, i: (jnp.minimum((i + 1) * hb, nr * hb - 1), j)),
                  nxt(1), nxt(3), pl.BlockSpec((8, tc), lambda j, i: (0, j))],
        out_specs=[pl.BlockSpec((4, ts, tc), lambda j, i: (0, i, j)), pl.BlockSpec((8, tc), lambda j, i: (0, j))],
        out_shape=[_sds((4, s_len, e), BF16), _sds((8, e), F32)],
        scratch_shapes=[pltpu.VMEM((C_HALO + ts, tc), F32), pltpu.VMEM((ts + C_HALO, tc), F32),
                        pltpu.VMEM((ts, tc), F32), pltpu.VMEM((ts, tc), F32), pltpu.VMEM((64, tc), F32)],
        compiler_params=_params(32 * ts * tc * 4),
    )(proj, proj, proj, proj, proj, proj, dgated, dgated, proj, proj, w_pad)
    return dproj, dw[:k_width]


def _head_rms(xv, g):
    r = lax.rsqrt(jnp.mean(xv * xv, axis=-1, keepdims=True) + NORM_EPS)
    return r, xv * r * g


def _b_qk_fwd(proj, gq, gk, *, name, ts=512, tc=512):
    s_len, e4 = proj.shape
    e = e4 // 4
    ts, tc = _tile(s_len, ts), _tile(e, tc)
    nc = e // tc

    def body(q, k, v, gq_ref, gk_ref, qn, kn, vb):
        for h in range(tc // HEAD_DIM):
            cs = slice(h * HEAD_DIM, (h + 1) * HEAD_DIM)
            qn[:, cs] = _head_rms(q[:, cs], gq_ref[...])[1].astype(qn.dtype)
            kn[:, cs] = _head_rms(k[:, cs], gk_ref[...])[1].astype(kn.dtype)
        vb[...] = v[...].astype(vb.dtype)

    def grp(g):
        return pl.BlockSpec((ts, tc), lambda i, j: (i, g * nc + j))

    vec = pl.BlockSpec((1, HEAD_DIM), lambda i, j: (0, 0))
    out = pl.BlockSpec((ts, tc), lambda i, j: (i, j))
    return pl.pallas_call(
        body, name=name, grid=(s_len // ts, nc), in_specs=[grp(0), grp(1), grp(2), vec, vec],
        out_specs=[out, out, out], out_shape=[_sds((s_len, e), BF16)] * 3,
        compiler_params=_params(16 * ts * tc * 4),
    )(proj, proj, proj, gq, gk)


def _b_qk_bwd(dqn, dkn, dv, dz, proj, gq, gk, *, name, ts=512, tc=512):
    s_len, e4 = proj.shape
    e = e4 // 4
    ts, tc = _tile(s_len, ts), _tile(e, tc)
    nc, nr = e // tc, s_len // ts

    def body(dq_ref, dk_ref, dv_ref, dz_ref, q, k, gq_ref, gk_ref, dp_ref, dgq_ref, dgk_ref, a_q, a_k):
        i, j = pl.program_id(0), pl.program_id(1)

        @pl.when((i == 0) & (j == 0))
        def _():
            a_q[...] = jnp.zeros_like(a_q)
            a_k[...] = jnp.zeros_like(a_k)

        for h in range(tc // HEAD_DIM):
            cs = slice(h * HEAD_DIM, (h + 1) * HEAD_DIM)
            for slot, src, d_ref, g_ref, acc in ((0, q, dq_ref, gq_ref, a_q), (1, k, dk_ref, gk_ref, a_k)):
                xv = src[:, cs]
                dy = d_ref[:, cs]
                r = lax.rsqrt(jnp.mean(xv * xv, axis=-1, keepdims=True) + NORM_EPS)
                gd = dy * g_ref[...]
                dot = jnp.mean(xv * gd, axis=-1, keepdims=True)
                dp_ref[slot, :, cs] = (r * gd - xv * (r * r * r * dot)).astype(dp_ref.dtype)
                acc[...] += _rowsum8(dy * (xv * r))
        dp_ref[2] = dv_ref[...].astype(dp_ref.dtype)
        dp_ref[3] = dz_ref[...]

        @pl.when((i == nr - 1) & (j == nc - 1))
        def _():
            dgq_ref[...] = jnp.sum(a_q[...], axis=0, keepdims=True)
            dgk_ref[...] = jnp.sum(a_k[...], axis=0, keepdims=True)

    blk = pl.BlockSpec((ts, tc), lambda i, j: (i, j))
    vec = pl.BlockSpec((1, HEAD_DIM), lambda i, j: (0, 0))

    def grp(g):
        return pl.BlockSpec((ts, tc), lambda i, j: (i, g * nc + j))

    return pl.pallas_call(
        body, name=name, grid=(nr, nc), in_specs=[blk, blk, blk, blk, grp(0), grp(1), vec, vec],
        out_specs=[pl.BlockSpec((4, ts, tc), lambda i, j: (0, i, j)), vec, vec],
        out_shape=[_sds((4, s_len, e), BF16), _sds((1, HEAD_DIM), F32), _sds((1, HEAD_DIM), F32)],
        scratch_shapes=[pltpu.VMEM((8, HEAD_DIM), F32)] * 2,
        compiler_params=_params(24 * ts * tc * 4),
    )(dqn, dkn, dv, dz, proj, proj, gq, gk)


def _log_sigmoid(x):
    y = jnp.exp(-jnp.abs(x))
    u = 1.0 + y
    log1p = jnp.where(u == 1.0, y, jnp.log(u) * (y / jnp.where(u == 1.0, 1.0, u - 1.0)))
    return jnp.minimum(x, 0.0) - log1p


def _split3(v):
    hi = v.astype(BF16)
    r1 = v - hi.astype(F32)
    mid = r1.astype(BF16)
    lo = (r1 - mid.astype(F32)).astype(BF16)
    return hi, mid, lo


def _tri_matmul(tri, v):
    hi, mid, lo = _split3(v)
    return (jnp.dot(tri, hi, preferred_element_type=F32) + jnp.dot(tri, mid, preferred_element_type=F32)
            + jnp.dot(tri, lo, preferred_element_type=F32))


def _b_cumsum(fl, bias, *, name, t=512):
    s_len, w = fl.shape
    t = _tile(s_len, t)

    def body(fl_ref, b_ref, c_ref, carry):
        @pl.when(pl.program_id(0) == 0)
        def _():
            carry[...] = jnp.zeros_like(carry)

        logf = _log_sigmoid(fl_ref[...] + b_ref[...])
        row = lax.broadcasted_iota(jnp.int32, (t, t), 0)
        col = lax.broadcasted_iota(jnp.int32, (t, t), 1)
        tri = jnp.where(col <= row, 1.0, 0.0).astype(BF16)
        c = _tri_matmul(tri, logf) + carry[...]
        c_ref[...] = c
        carry[...] = c[t - 1:t, :]

    return pl.pallas_call(
        body, name=name, grid=(s_len // t,),
        in_specs=[pl.BlockSpec((t, w), lambda i: (i, 0)), pl.BlockSpec((1, w), lambda i: (0, 0))],
        out_specs=pl.BlockSpec((t, w), lambda i: (i, 0)), out_shape=_sds((s_len, w), F32),
        scratch_shapes=[pltpu.VMEM((1, w), F32)], compiler_params=_params(16 << 20),
    )(fl, bias)


def _b_cumsum_bwd(dc, fl, bias, *, name, t=512):
    s_len, w = fl.shape
    t = _tile(s_len, t)
    n = s_len // t

    def body(dc_ref, fl_ref, b_ref, dfl_ref, db_ref, carry, acc):
        i = pl.program_id(0)

        @pl.when(i == 0)
        def _():
            carry[...] = jnp.zeros_like(carry)
            acc[...] = jnp.zeros_like(acc)

        row = lax.broadcasted_iota(jnp.int32, (t, t), 0)
        col = lax.broadcasted_iota(jnp.int32, (t, t), 1)
        tri = jnp.where(col >= row, 1.0, 0.0).astype(BF16)
        dlogf = _tri_matmul(tri, dc_ref[...]) + carry[...]
        carry[...] = dlogf[0:1, :]
        dfl = dlogf * _sigmoid(-(fl_ref[...] + b_ref[...]))
        dfl_ref[...] = dfl
        acc[...] += _rowsum8(dfl)

        @pl.when(i == n - 1)
        def _():
            db_ref[...] = jnp.sum(acc[...], axis=0, keepdims=True)

    rev = pl.BlockSpec((t, w), lambda i: (n - 1 - i, 0))
    vec = pl.BlockSpec((1, w), lambda i: (0, 0))
    return pl.pallas_call(
        body, name=name, grid=(n,), in_specs=[rev, rev, vec], out_specs=[rev, vec],
        out_shape=[_sds((s_len, w), F32), _sds((1, w), F32)],
        scratch_shapes=[pltpu.VMEM((1, w), F32), pltpu.VMEM((8, w), F32)], compiler_params=_params(16 << 20),
    )(dc, fl, bias)


def _b_gate_fwd(o, proj, *, name, ts=512, tc=512):
    s_len, e = o.shape
    ts, tc = _tile(s_len, ts), _tile(e, tc)
    nc = e // tc

    def body(o_ref, z_ref, g_ref):
        g_ref[...] = (o_ref[...] * _silu(z_ref[...])).astype(g_ref.dtype)

    blk = pl.BlockSpec((ts, tc), lambda i, j: (i, j))
    return pl.pallas_call(
        body, name=name, grid=(s_len // ts, nc),
        in_specs=[blk, pl.BlockSpec((ts, tc), lambda i, j: (i, 3 * nc + j))], out_specs=blk,
        out_shape=_sds((s_len, e), BF16), compiler_params=_params(12 * ts * tc * 4),
    )(o, proj)


def _b_gate_bwd(dgated, o, o_lo, proj, *, name, hg, ts=512):
    s_len, e = o.shape
    ts = _tile(s_len, ts)
    w = hg * HEAD_DIM
    ng = e // w

    def body(dg_ref, o_ref, olo_ref, z_ref, do_ref, dz_ref, dl_ref):
        dgt, ov, zv = dg_ref[...], o_ref[...], z_ref[...]
        sz, dsz = _silu_pair(zv)
        dob = (dgt * sz).astype(do_ref.dtype)
        do_ref[...] = dob
        dz_ref[...] = (dgt * ov * dsz).astype(dz_ref.dtype)
        prod = dob.astype(F32) * (ov + olo_ref[...])
        for hh in range(hg):
            cs = slice(hh * HEAD_DIM, (hh + 1) * HEAD_DIM)
            dl_ref[:, cs] = jnp.broadcast_to(jnp.sum(prod[:, cs], axis=-1, keepdims=True), (ts, HEAD_DIM))

    blk = pl.BlockSpec((ts, w), lambda i, j: (i, j))
    return pl.pallas_call(
        body, name=name, grid=(s_len // ts, ng),
        in_specs=[blk, blk, blk, pl.BlockSpec((ts, w), lambda i, j: (i, 3 * ng + j))],
        out_specs=[blk, blk, blk],
        out_shape=[_sds((s_len, e), BF16), _sds((s_len, e), BF16), _sds((s_len, e), F32)],
        compiler_params=_params(24 * ts * w * 4),
    )(dgated, o, o_lo, proj)


LOG2E = 1.4426950408889634
ATTN_ROW_CHUNK = 128


def _tri_tables(n, k_major):
    pairs = [(i, j) for j in range(n) for i in range(j, n)] if k_major else [(i, j) for i in range(n) for j in range(i + 1)]
    return (jnp.asarray(np.array([p[0] for p in pairs], np.int32)), jnp.asarray(np.array([p[1] for p in pairs], np.int32)))


def _attn_logits2(s_raw, cr2, diag, row0, c1):
    s2 = s_raw * c1 - cr2
    if diag:
        rc, t = s_raw.shape
        row = lax.broadcasted_iota(jnp.int32, (rc, t), 0) + row0
        col = lax.broadcasted_iota(jnp.int32, (rc, t), 1)
        s2 = jnp.where(col <= row, s2, -jnp.inf)
    return s2


def _fox_fwd(qn, kn, vb, c_row, *, name, hg, t=512):
    s_len, e = qn.shape
    t = _tile(s_len, t)
    rc = _tile(t, ATTN_ROW_CHUNK)
    w = hg * HEAD_DIM
    ng, n = e // w, s_len // t
    c1 = HEAD_DIM ** -0.5 * LOG2E
    qi_tab, kj_tab = _tri_tables(n, k_major=False)

    def body(qi_ref, kj_ref, q_ref, k_ref, v_ref, cr_ref, o_ref, olo_ref, lse_ref, m_s, l_s, acc_s, lo_s, s_scr, p_scr,
             a_scr):
        pid = pl.program_id(1)
        i, j = qi_ref[pid], kj_ref[pid]

        @pl.when(j == 0)
        def _():
            m_s[...] = jnp.full_like(m_s, -jnp.inf)
            l_s[...] = jnp.zeros_like(l_s)
            acc_s[...] = jnp.zeros_like(acc_s)
            lo_s[...] = jnp.zeros_like(lo_s)

        def step(diag):
            for hh in range(hg):
                cs = slice(hh * HEAD_DIM, (hh + 1) * HEAD_DIM)
                s_scr[...] = lax.dot_general(q_ref[:, cs], k_ref[:, cs], (((1,), (1,)), ((), ())),
                                             preferred_element_type=F32)
                cr2 = cr_ref[hh] * LOG2E
                for r in range(t // rc):
                    rows = slice(r * rc, (r + 1) * rc)
                    s2 = _attn_logits2(s_scr[rows, :], cr2, diag, r * rc, c1)
                    m_prev = m_s[hh, rows]
                    m_new = jnp.maximum(m_prev, jnp.max(s2, axis=-1, keepdims=True))
                    alpha = jnp.exp2(m_prev - m_new)
                    p = jnp.exp2(s2 - jnp.tile(m_new, (1, t // LANES)))
                    l_s[hh, rows] = alpha * l_s[hh, rows] + jnp.sum(p, axis=-1, keepdims=True)
                    m_s[hh, rows] = m_new
                    a_scr[rows] = alpha
                    p_hi = p.astype(BF16)
                    p_scr[rows, :] = p_hi
                    p_scr[t + r * rc:t + (r + 1) * rc, :] = (p - p_hi.astype(F32)).astype(BF16)
                pv = jnp.dot(p_scr[...], v_ref[:, cs], preferred_element_type=F32)
                al = a_scr[...]
                acc_s[:, cs] = al * acc_s[:, cs] + pv[:t]
                lo_s[:, cs] = al * lo_s[:, cs] + pv[t:]

        @pl.when(j < i)
        def _():
            step(False)

        @pl.when(j == i)
        def _():
            step(True)
            for hh in range(hg):
                cs = slice(hh * HEAD_DIM, (hh + 1) * HEAD_DIM)
                o_ref[:, cs] = acc_s[:, cs] / l_s[hh]
                olo_ref[:, cs] = lo_s[:, cs] / l_s[hh]
                lse_ref[:, cs] = m_s[hh] + jnp.log2(l_s[hh])

    qspec = pl.BlockSpec((t, w), lambda g, p, qi, kj: (qi[p], g))
    kspec = pl.BlockSpec((t, w), lambda g, p, qi, kj: (kj[p], g))
    crow = pl.BlockSpec((hg, 1, t), lambda g, p, qi, kj: (g, 0, kj[p]))
    grid_spec = pltpu.PrefetchScalarGridSpec(
        num_scalar_prefetch=2, grid=(ng, int(qi_tab.shape[0])), in_specs=[qspec, kspec, kspec, crow],
        out_specs=[qspec, qspec, qspec],
        scratch_shapes=[pltpu.VMEM((hg, t, LANES), F32), pltpu.VMEM((hg, t, LANES), F32), pltpu.VMEM((t, w), F32),
                        pltpu.VMEM((t, w), F32), pltpu.VMEM((t, t), F32), pltpu.VMEM((2 * t, t), BF16),
                        pltpu.VMEM((t, LANES), F32)])
    return pl.pallas_call(
        body, name=name, grid_spec=grid_spec,
        out_shape=[_sds((s_len, e), F32), _sds((s_len, e), F32), _sds((s_len, e), F32)],
        compiler_params=_params(12 * t * t * 4 + 32 * t * w * 4),
    )(qi_tab, kj_tab, qn, kn, vb, c_row)


def _fox_bwd(qn, kn, vb, do, lse, delta, c_row, *, name, hg, t=512):
    s_len, e = qn.shape
    t = _tile(s_len, t)
    w = hg * HEAD_DIM
    ng, n = e // w, s_len // t
    rc = _tile(t, ATTN_ROW_CHUNK)
    scale = HEAD_DIM ** -0.5
    c1 = scale * LOG2E
    qi_tab, kj_tab = _tri_tables(n, k_major=True)

    def body(qi_ref, kj_ref, q_ref, k_ref, v_ref, do_ref, lse_ref, dl_ref, cr_ref, dq_ref, dk_ref, dv_ref, dc_ref,
             dk_s, dv_s, dc_s, s_scr, dp_scr, p_scr, ds_scr):
        pid = pl.program_id(1)
        i, j = qi_ref[pid], kj_ref[pid]

        @pl.when(pid == 0)
        def _():
            dq_ref[...] = jnp.zeros_like(dq_ref)

        @pl.when(i == j)
        def _():
            dk_s[...] = jnp.zeros_like(dk_s)
            dv_s[...] = jnp.zeros_like(dv_s)
            dc_s[...] = jnp.zeros_like(dc_s)

        def step(diag):
            qrows = pl.ds(pl.multiple_of(i * t, t), t)
            for hh in range(hg):
                cs = slice(hh * HEAD_DIM, (hh + 1) * HEAD_DIM)
                s_scr[...] = lax.dot_general(q_ref[:, cs], k_ref[:, cs], (((1,), (1,)), ((), ())),
                                             preferred_element_type=F32)
                dp_scr[...] = lax.dot_general(do_ref[:, cs], v_ref[:, cs], (((1,), (1,)), ((), ())),
                                              preferred_element_type=F32)
                cr2 = cr_ref[hh] * LOG2E
                dcol = jnp.zeros((SUBLANES, t), F32)
                for r in range(t // rc):
                    rows = slice(r * rc, (r + 1) * rc)
                    s2 = _attn_logits2(s_scr[rows, :], cr2, diag, r * rc, c1)
                    p = jnp.exp2(s2 - jnp.tile(lse_ref[rows, cs], (1, t // LANES)))
                    ds = p * (dp_scr[rows, :] - jnp.tile(dl_ref[rows, cs], (1, t // LANES)))
                    dcol = dcol + _rowsum8(ds)
                    p_scr[rows, :] = p.astype(BF16)
                    ds_scr[rows, :] = (ds * scale).astype(BF16)
                dc_s[hh] -= jnp.sum(dcol, axis=0, keepdims=True)
                dv_s[:, cs] += lax.dot_general(p_scr[...], do_ref[:, cs], (((0,), (0,)), ((), ())),
                                               preferred_element_type=F32)
                dk_s[:, cs] += lax.dot_general(ds_scr[...], q_ref[:, cs], (((0,), (0,)), ((), ())),
                                               preferred_element_type=F32)
                dq_ref[qrows, cs] += jnp.dot(ds_scr[...], k_ref[:, cs], preferred_element_type=F32)

        @pl.when(i > j)
        def _():
            step(False)

        @pl.when(i == j)
        def _():
            step(True)

        @pl.when(i == n - 1)
        def _():
            dk_ref[...] = dk_s[...]
            dv_ref[...] = dv_s[...]
            dc_ref[...] = dc_s[...]

    qspec = pl.BlockSpec((t, w), lambda g, p, qi, kj: (qi[p], g))
    kspec = pl.BlockSpec((t, w), lambda g, p, qi, kj: (kj[p], g))
    crow = pl.BlockSpec((hg, 1, t), lambda g, p, qi, kj: (g, 0, kj[p]))
    grid_spec = pltpu.PrefetchScalarGridSpec(
        num_scalar_prefetch=2, grid=(ng, int(qi_tab.shape[0])),
        in_specs=[qspec, kspec, kspec, qspec, qspec, qspec, crow],
        out_specs=[pl.BlockSpec((s_len, w), lambda g, p, qi, kj: (0, g)), kspec, kspec, crow],
        scratch_shapes=[pltpu.VMEM((t, w), F32), pltpu.VMEM((t, w), F32), pltpu.VMEM((hg, 1, t), F32),
                        pltpu.VMEM((t, t), F32), pltpu.VMEM((t, t), F32), pltpu.VMEM((t, t), BF16),
                        pltpu.VMEM((t, t), BF16)])
    return pl.pallas_call(
        body, name=name, grid_spec=grid_spec,
        out_shape=[_sds((s_len, e), F32), _sds((s_len, e), F32), _sds((s_len, e), F32), _sds((e // HEAD_DIM, 1, s_len), F32)],
        compiler_params=_params(2 * s_len * w * 4 + 16 * t * t * 4 + 24 * t * w * 4),
    )(qi_tab, kj_tab, qn, kn, vb, do, lse, delta, c_row)


def _adamw(w, gs, m, v, *, name, tr=256):
    r, c = w.shape
    tr = _tile(r, tr)
    n_g = len(gs)

    def body(*refs):
        w_ref, g_refs = refs[0], refs[1:1 + n_g]
        m_ref, v_ref, go_ref, d_ref, nm_ref, nv_ref = refs[1 + n_g:]
        gv = g_refs[0][...].astype(F32)
        for g_ref in g_refs[1:]:
            gv = gv + g_ref[...].astype(F32)
        go_ref[...] = gv
        m2 = ADAM_B1 * m_ref[...] + (1.0 - ADAM_B1) * gv
        v2 = ADAM_B2 * v_ref[...] + (1.0 - ADAM_B2) * (gv * gv)
        m_hat = m2 / (1.0 - ADAM_B1 ** ADAM_STEP)
        v_hat = v2 / (1.0 - ADAM_B2 ** ADAM_STEP)
        d_ref[...] = -ADAM_LR * (m_hat / (jnp.sqrt(v_hat) + ADAM_EPS) + ADAM_WD * w_ref[...])
        nm_ref[...] = m2
        nv_ref[...] = v2

    blk = pl.BlockSpec((tr, c), lambda i: (i, 0))
    return pl.pallas_call(
        body, name=name, grid=(r // tr,), in_specs=[blk] * (3 + n_g), out_specs=[blk] * 4,
        out_shape=[_sds((r, c), F32)] * 4, compiler_params=_params(24 * tr * c * 4),
    )(w, *gs, m, v)


def _adamw_halves(w, mine, other, m, v, c_arr, *, name, tr=256):
    r, c = w.shape
    half = r // 2
    tr = _tile(half, tr)
    nh = half // tr

    def body(c_ref, w_ref, mine_ref, other_ref, m_ref, v_ref, go_ref, d_ref, nm_ref, nv_ref):
        is_mine = pl.program_id(0) == c_ref[0]
        gv = jnp.where(is_mine, mine_ref[...], other_ref[...])
        go_ref[...] = gv
        m2 = ADAM_B1 * m_ref[...] + (1.0 - ADAM_B1) * gv
        v2 = ADAM_B2 * v_ref[...] + (1.0 - ADAM_B2) * (gv * gv)
        m_hat = m2 / (1.0 - ADAM_B1 ** ADAM_STEP)
        v_hat = v2 / (1.0 - ADAM_B2 ** ADAM_STEP)
        d_ref[...] = -ADAM_LR * (m_hat / (jnp.sqrt(v_hat) + ADAM_EPS) + ADAM_WD * w_ref[...])
        nm_ref[...] = m2
        nv_ref[...] = v2

    full = pl.BlockSpec((tr, c), lambda h, j, cref: (h * nh + j, 0))
    part = pl.BlockSpec((tr, c), lambda h, j, cref: (j, 0))
    grid_spec = pltpu.PrefetchScalarGridSpec(num_scalar_prefetch=1, grid=(2, nh), in_specs=[full, part, part, full, full],
                                             out_specs=[full] * 4)
    return pl.pallas_call(
        body, name=name, grid_spec=grid_spec, out_shape=[_sds((r, c), F32)] * 4,
        compiler_params=_params(28 * tr * c * 4),
    )(c_arr, w, mine, other, m, v)


def _pair_add(g, got, c_arr, *, name, tr=256):
    n, r, c = g.shape
    half = r // 2
    tr = _tile(half, tr)
    nh = half // tr

    def body(c_ref, g_ref, got_ref, o_ref):
        o_ref[...] = (g_ref[...].astype(F32) + got_ref[...].astype(F32)).astype(o_ref.dtype)

    grid_spec = pltpu.PrefetchScalarGridSpec(
        num_scalar_prefetch=1, grid=(n, nh),
        in_specs=[pl.BlockSpec((None, tr, c), lambda s, i, cref: (s, cref[0] * nh + i, 0)),
                  pl.BlockSpec((None, tr, c), lambda s, i, cref: (s, i, 0))],
        out_specs=pl.BlockSpec((None, tr, c), lambda s, i, cref: (s, i, 0)))
    return pl.pallas_call(
        body, name=name, grid_spec=grid_spec, out_shape=_sds((n, half, c), BF16), compiler_params=_params(16 * tr * c * 4),
    )(c_arr, g, got)


def _sum_own_recv(pair, recv, chip_arr, *, name, tr=256):
    _, r, c = pair.shape
    tr = _tile(r, tr)
    n_recv = recv.shape[0]

    def body(chip_ref, own_ref, recv_ref, o_ref):
        acc = own_ref[...].astype(F32)
        for k in range(n_recv):
            acc = acc + recv_ref[k].astype(F32)
        o_ref[...] = acc

    grid_spec = pltpu.PrefetchScalarGridSpec(
        num_scalar_prefetch=1, grid=(r // tr,),
        in_specs=[pl.BlockSpec((None, tr, c), lambda i, chip: (chip[0], i, 0)),
                  pl.BlockSpec((n_recv, tr, c), lambda i, chip: (0, i, 0))],
        out_specs=pl.BlockSpec((tr, c), lambda i, chip: (i, 0)))
    return pl.pallas_call(
        body, name=name, grid_spec=grid_spec, out_shape=_sds((r, c), F32), compiler_params=_params(16 * tr * c * 4),
    )(chip_arr, pair, recv)


_ANY = pl.BlockSpec(memory_space=pl.ANY)
DMA_CHUNK_BYTES = 512 << 10


def _chunks(parts):
    out = []
    for src_at, dst_at, rows, row_bytes in parts:
        step = max(16, DMA_CHUNK_BYTES // row_bytes // 16 * 16)
        for r0 in range(0, rows, step):
            n = min(step, rows - r0)
            out.append((src_at(r0, n), dst_at(r0, n)))
    return out


def _row_bytes(ref):
    return ref.shape[-1] * ref.dtype.itemsize


def _me():
    return lax.axis_index("x"), lax.axis_index("y"), lax.axis_index("c")


def _chip_peers(x, y):
    return [(2 * (1 - x) + y, (1 - x, y)), (2 * x + (1 - y), (x, 1 - y)), (2 * (1 - x) + (1 - y), (1 - x, 1 - y))]


def _exchange(name, ins, out_shapes, plan):
    n_in, n_out = len(ins), len(out_shapes)

    def body(*refs):
        in_refs, out_refs = refs[:n_in], refs[n_in:n_in + n_out]
        send_sems, recv_sems, loc_sems = refs[n_in + n_out:]
        remote, local = plan(in_refs, out_refs)
        starts, waits = [], []
        for k, (ws, wd, dev, parts) in enumerate(remote):
            def mk(s, d, k=k, dev=dev):
                return pltpu.make_async_remote_copy(src_ref=s, dst_ref=d, send_sem=send_sems.at[k],
                                                    recv_sem=recv_sems.at[k], device_id=dev, device_id_type=MESH_ID)
            starts += [mk(s, d) for s, d in _chunks(parts)]
            waits.append(mk(ws, wd))
        for k, (ws, wd, _, parts) in enumerate(local):
            def mk(s, d, k=k):
                return pltpu.make_async_copy(s, d, loc_sems.at[k])
            starts += [mk(s, d) for s, d in _chunks(parts)]
            waits.append(mk(ws, wd))
        for cp in starts:
            cp.start()
        for cp in waits:
            cp.wait()

    n_remote, n_local = plan.n_remote, plan.n_local
    return pl.pallas_call(
        body, name=name, in_specs=[_ANY] * n_in, out_specs=[_ANY] * n_out, out_shape=list(out_shapes),
        scratch_shapes=[pltpu.SemaphoreType.DMA((n_remote,)), pltpu.SemaphoreType.DMA((n_remote,)),
                        pltpu.SemaphoreType.DMA((max(n_local, 1),))],
    )(*ins)


def _gather_weights(shards, *, name):
    n_t = len(shards)

    def body(*refs):
        in_refs, out_refs = refs[:n_t], refs[n_t:2 * n_t]
        send_sems, recv_sems = refs[2 * n_t:]
        x, y, c = _me()
        chip = 2 * x + y
        peers = _chip_peers(x, y)
        local, first, passed = [], [], []
        for t in range(n_t):
            src, dst = in_refs[t], out_refs[t]
            n_rows = src.shape[0]
            half = n_rows // 2
            rb = _row_bytes(src)
            rows = pl.ds(c * half, half)

            def mk_own(s, d, t=t):
                return pltpu.make_async_remote_copy(
                    src_ref=s, dst_ref=d, send_sem=send_sems.at[7 * t + 6], recv_sem=recv_sems.at[7 * t + 6],
                    device_id=(x, y, 1 - c), device_id_type=MESH_ID)

            own = [(lambda r0, n, src=src: src.at[pl.ds(r0, n)],
                    lambda r0, n, dst=dst: dst.at[chip, pl.ds(r0, n)], n_rows, rb)]
            local.append((mk_own(src, dst.at[chip]), [mk_own(s, d) for s, d in _chunks(own)]))
            for k, (pchip, (px, py)) in enumerate(peers):
                def mk_ici(s, d, t=t, k=k, px=px, py=py):
                    return pltpu.make_async_remote_copy(
                        src_ref=s, dst_ref=d, send_sem=send_sems.at[7 * t + k], recv_sem=recv_sems.at[7 * t + k],
                        device_id=(px, py, c), device_id_type=MESH_ID)

                def mk_d2d(s, d, t=t, k=k):
                    return pltpu.make_async_remote_copy(
                        src_ref=s, dst_ref=d, send_sem=send_sems.at[7 * t + 3 + k], recv_sem=recv_sems.at[7 * t + 3 + k],
                        device_id=(x, y, 1 - c), device_id_type=MESH_ID)

                out_part = [(lambda r0, n, src=src: src.at[pl.ds(c * half + r0, n)],
                             lambda r0, n, dst=dst: dst.at[chip, pl.ds(c * half + r0, n)], half, rb)]
                fwd_part = [(lambda r0, n, dst=dst, pchip=pchip: dst.at[pchip, pl.ds(c * half + r0, n)],
                             lambda r0, n, dst=dst, pchip=pchip: dst.at[pchip, pl.ds(c * half + r0, n)], half, rb)]
                first.append((mk_ici(src.at[rows], dst.at[chip, rows]), [mk_ici(s, d) for s, d in _chunks(out_part)]))
                passed.append((mk_d2d(dst.at[pchip, rows], dst.at[pchip, rows]),
                               [mk_d2d(s, d) for s, d in _chunks(fwd_part)]))
        for _, chunk_copies in first + local:
            for cp in chunk_copies:
                cp.start()
        for (whole, _), (_, fwd_copies) in zip(first, passed):
            whole.wait_recv()
            for cp in fwd_copies:
                cp.start()
        for whole, _ in passed:
            whole.wait_recv()
        for whole, _ in first + passed:
            whole.wait_send()
        for whole, _ in local:
            whole.wait()

    outs = [_sds((N_CHIPS,) + s.shape, s.dtype) for s in shards]
    return pl.pallas_call(
        body, name=name, in_specs=[_ANY] * n_t, out_specs=[_ANY] * n_t, out_shape=outs,
        scratch_shapes=[pltpu.SemaphoreType.DMA((7 * n_t,)), pltpu.SemaphoreType.DMA((7 * n_t,))],
    )(*shards)


class _Plan:
    def __init__(self, fn, n_remote, n_local):
        self.fn, self.n_remote, self.n_local = fn, n_remote, n_local

    def __call__(self, in_refs, out_refs):
        return self.fn(in_refs, out_refs)


def _reduce_grads(grads, *, name):
    n_t = len(grads)
    c_arr = lax.axis_index("c").astype(jnp.int32).reshape(1)
    chip_arr = (2 * lax.axis_index("x") + lax.axis_index("y")).astype(jnp.int32).reshape(1)

    def plan1(in_refs, out_refs):
        x, y, c = _me()
        remote = []
        for t in range(n_t):
            src, got = in_refs[t], out_refs[t]
            half = src.shape[1] // 2
            send = [(lambda r0, n, s=s, src=src, half=half: src.at[s, pl.ds((1 - c) * half + r0, n)],
                     lambda r0, n, s=s, got=got: got.at[s, pl.ds(r0, n)], half, _row_bytes(src)) for s in range(N_CHIPS)]
            remote.append((src.at[:, pl.ds((1 - c) * half, half)], got, (x, y, 1 - c), send))
        return remote, []

    halves = [_sds((N_CHIPS, g.shape[1] // 2, g.shape[2]), g.dtype) for g in grads]
    got = _exchange(name + "_sib", grads, halves, _Plan(plan1, n_t, 0))
    pair = [_pair_add(grads[t], got[t], c_arr, name=f"{name}_pair{t}") for t in range(n_t)]

    def plan2(in_refs, out_refs):
        x, y, c = _me()
        remote = []
        for t in range(n_t):
            src, dst = in_refs[t], out_refs[t]
            rows, rb = src.shape[1], _row_bytes(src)
            for k, (pchip, (px, py)) in enumerate(_chip_peers(x, y)):
                part = [(lambda r0, n, src=src, pchip=pchip: src.at[pchip, pl.ds(r0, n)],
                         lambda r0, n, dst=dst, k=k: dst.at[k, pl.ds(r0, n)], rows, rb)]
                remote.append((src.at[pchip], dst.at[k], (px, py, c), part))
        return remote, []

    recv_shapes = [_sds((N_CHIPS - 1,) + h.shape[1:], h.dtype) for h in halves]
    recv = _exchange(name + "_ici", pair, recv_shapes, _Plan(plan2, 3 * n_t, 0))
    mine = [_sum_own_recv(pair[t], recv[t], chip_arr, name=f"{name}_sum{t}") for t in range(n_t)]

    def plan3(in_refs, out_refs):
        x, y, c = _me()
        remote = []
        for t in range(n_t):
            src, dst = in_refs[t], out_refs[t]
            rows = [(lambda r0, n, src=src: src.at[pl.ds(r0, n)], lambda r0, n, dst=dst: dst.at[pl.ds(r0, n)],
                     src.shape[0], _row_bytes(src))]
            remote.append((src, dst, (x, y, 1 - c), rows))
        return remote, []

    other = _exchange(name + "_swap", mine, [_sds(s.shape, s.dtype) for s in mine], _Plan(plan3, n_t, 0))
    return list(zip(mine, other)), c_arr


def _allreduce_small(pack, *, name):
    r, w = pack.shape

    def body(p_ref, o_ref, buf, send_sems, recv_sems):
        x, y, c = _me()
        me = 4 * x + 2 * y + c
        buf[me] = p_ref[...]
        copies = []
        for k in range(1, N_DEV):
            peer = (x ^ ((k >> 2) & 1), y ^ ((k >> 1) & 1), c ^ (k & 1))
            copies.append(pltpu.make_async_remote_copy(
                src_ref=p_ref, dst_ref=buf.at[me], send_sem=send_sems.at[k - 1], recv_sem=recv_sems.at[k - 1],
                device_id=peer, device_id_type=MESH_ID))
        for cp in copies:
            cp.start()
        for cp in copies:
            cp.wait()
        acc = buf[0]
        for k in range(1, N_DEV):
            acc = acc + buf[k]
        o_ref[...] = acc

    vm = pl.BlockSpec(memory_space=pltpu.VMEM)
    return pl.pallas_call(
        body, name=name, in_specs=[vm], out_specs=vm, out_shape=_sds((r, w), F32),
        scratch_shapes=[pltpu.VMEM((N_DEV, r, w), F32), pltpu.SemaphoreType.DMA((N_DEV - 1,)),
                        pltpu.SemaphoreType.DMA((N_DEV - 1,))],
        compiler_params=_params(12 * r * w * 4),
    )(pack)


def _pack(arrs, row_multiple=16):
    flat = jnp.concatenate([a.reshape(-1).astype(F32) for a in arrs])
    unit = row_multiple * LANES
    total = -(-flat.shape[0] // unit) * unit
    return jnp.pad(flat, (0, total - flat.shape[0])).reshape(total // LANES, LANES)


def _unpack(packed, shapes):
    flat = packed.reshape(-1)
    out, off = [], 0
    for shp in shapes:
        n = int(np.prod(shp))
        out.append(flat[off:off + n].reshape(shp))
        off += n
    return out


def _pad_cols(a, width):
    return jnp.pad(a, [(0, 0)] * (a.ndim - 1) + [(0, width - a.shape[-1])])


ATTN_HEADS_PER_STEP = 4
ATTN_FWD_HEADS_PER_STEP = 8
SMALL_SHARDED = ("a_norm", "a_conv_w", "a_conv_b", "a_ln_g", "a_ln_b", "c_norm", "c_conv_w")
SMALL_REPLICATED = ("b_norm", "b_f_bias", "b_q_norm", "b_k_norm")
BIG = ("a_w_in", "a_w_out", "b_w_in", "b_w_out", "c_w_in", "c_w_out")
WEIGHTS = ("a_norm", "a_w_in", "a_conv_w", "a_conv_b", "a_ln_g", "a_ln_b", "a_w_out", "b_norm", "b_w_in", "b_f_bias",
           "b_q_norm", "b_k_norm", "b_w_out", "c_norm", "c_w_in", "c_conv_w", "c_w_out")


def _mixer_a_fwd(x, p, l, tag):
    h = _rms_fwd(x, p["a_norm"][l][None], name=f"{tag}_rms")
    proj = _mm_nn(h, p["a_w_in"][l], name=f"{tag}_in", tn=p["a_w_in"][l].shape[2])
    u1 = _a_conv_fwd(proj, p["a_conv_w"][l], p["a_conv_b"][l][None], name=f"{tag}_conv")
    gated = _a_post_fwd(u1, proj, p["a_ln_g"][l][None], p["a_ln_b"][l][None], name=f"{tag}_post")
    y = _mm_nn(gated, p["a_w_out"][l], name=f"{tag}_out", tn=1024, add=x)
    return y, (x, h, proj, u1, gated)


def _mixer_a_bwd(dx, saved, p, l, tag):
    x, h, proj, u1, gated = saved
    g = {}
    g["a_w_out"] = _mm_tn(gated, dx, name=f"{tag}_dwout", out_dtype=BF16, tk=2048, tn=1024, ts=1024)
    dgated = _mm_nt(dx, p["a_w_out"][l], name=f"{tag}_dgated", tn=2048, tk=1024)
    du1, dz, g["a_ln_g"], g["a_ln_b"], g["a_conv_b"] = _a_post_bwd(
        dgated, u1, proj, p["a_ln_g"][l][None], p["a_ln_b"][l][None], name=f"{tag}_dpost")
    dproj, g["a_conv_w"] = _a_conv_bwd(du1, proj, dz, p["a_conv_w"][l], name=f"{tag}_dconv")
    g["a_w_in"] = _mm_tn(h, dproj, name=f"{tag}_dwin", out_dtype=BF16, out_width=p["a_w_in"][l].shape[2], ts=2048)
    dh = _mm_nt(dproj, p["a_w_in"][l], name=f"{tag}_dh", ksub=4)
    dx, g["a_norm"] = _rms_bwd(x, p["a_norm"][l][None], [dh], dx, name=f"{tag}_drms")
    return dx, g


def _mixer_c_fwd(x, p, tag):
    h = _rms_fwd(x, p["c_norm"][0][None], name=f"{tag}_rms")
    proj = _mm_nn(h, p["c_w_in"], name=f"{tag}_in", tn=1024)
    gated = _c_mid_fwd(proj, p["c_conv_w"][0], name=f"{tag}_mid")
    y = _mm_nn(gated, p["c_w_out"], name=f"{tag}_out", tn=1024, add=x)
    return y, (x, h, proj, gated)


def _mixer_c_bwd(dx, saved, p, tag):
    x, h, proj, gated = saved
    g = {}
    g["c_w_out"] = _mm_tn(gated, dx, name=f"{tag}_dwout", out_dtype=BF16, tk=2048, tn=1024, ts=1024)
    dgated = _mm_nt(dx, p["c_w_out"], name=f"{tag}_dgated", tn=2048, tk=1024)
    dproj, g["c_conv_w"] = _c_mid_bwd(dgated, proj, p["c_conv_w"][0], name=f"{tag}_dmid")
    g["c_w_in"] = _mm_tn(h, dproj, name=f"{tag}_dwin", out_dtype=BF16, out_width=p["c_w_in"].shape[2], tn=1024, ts=2048)
    dh = _mm_nt(dproj, p["c_w_in"], name=f"{tag}_dh", tk=1024, ksub=2)
    dx, g["c_norm"] = _rms_bwd(x, p["c_norm"][0][None], [dh], dx, name=f"{tag}_drms")
    return dx, g


def _mixer_b_fwd(x, p, tag):
    s_len = x.shape[0]
    n_heads = p["b_f_bias"].shape[1]
    hg = min(ATTN_FWD_HEADS_PER_STEP, n_heads)
    h = _rms_fwd(x, p["b_norm"], name=f"{tag}_rms")
    proj = _mm_nn(h, p["b_wq"], name=f"{tag}_in", tn=1024)
    fl = _mm_nn(h, p["b_wf"], name=f"{tag}_inf", tn=LANES)
    qn, kn, vb = _b_qk_fwd(proj, p["b_q_norm"], p["b_k_norm"], name=f"{tag}_qk")
    bias = _pad_cols(p["b_f_bias"], LANES)
    c = _b_cumsum(fl, bias, name=f"{tag}_cumsum")
    ch = c[:, :n_heads]
    c_row = ch.T.reshape(n_heads, 1, s_len)
    o, o_lo, lse = _fox_fwd(qn, kn, vb, c_row, name=f"{tag}_attn", hg=hg)
    gated = _b_gate_fwd(o, proj, name=f"{tag}_gate")
    y = _mm_nn(gated, p["b_w_out"], name=f"{tag}_out", tn=1024, add=x)
    return y, (x, h, proj, fl, bias, qn, kn, vb, c_row, o, o_lo, lse, gated)


def _mixer_b_bwd(dx, saved, p, tag):
    hg = ATTN_HEADS_PER_STEP
    x, h, proj, fl, bias, qn, kn, vb, c_row, o, o_lo, lse, gated = saved
    s_len = x.shape[0]
    n_heads = p["b_f_bias"].shape[1]
    g = {}
    g["b_w_out"] = _mm_tn(gated, dx, name=f"{tag}_dwout", out_dtype=BF16, tk=2048, tn=1024, ts=1024)
    dgated = _mm_nt(dx, p["b_w_out"], name=f"{tag}_dgated", tn=2048, tk=1024)
    do, dz, delta = _b_gate_bwd(dgated, o, o_lo, proj, name=f"{tag}_dgate", hg=hg)
    dqn, dkn, dv, dc = _fox_bwd(qn, kn, vb, do, lse, delta, c_row, name=f"{tag}_dattn", hg=hg)
    dc_pad = _pad_cols(dc.reshape(n_heads, s_len).T, LANES)
    dfl, dbias = _b_cumsum_bwd(dc_pad, fl, bias, name=f"{tag}_dcumsum")
    g["b_f_bias"] = dbias[:, :n_heads]
    dproj, g["b_q_norm"], g["b_k_norm"] = _b_qk_bwd(dqn, dkn, dv, dz, proj, p["b_q_norm"], p["b_k_norm"], name=f"{tag}_dqk")
    dwq = _mm_tn(h, dproj, name=f"{tag}_dwin", out_dtype=BF16, tn=1024, ts=2048)
    dwf = _mm_tn(h, dfl, name=f"{tag}_dwinf", out_dtype=BF16, tn=LANES)
    g["b_w_in"] = jnp.concatenate([dwq, dwf[:, :n_heads]], axis=1)
    dh = _mm_nt(dproj, p["b_wq"], name=f"{tag}_dh", tk=1024, ksub=2)
    dhf = _mm_nt(dfl, p["b_wf"], name=f"{tag}_dhf", tk=LANES)
    dx, g["b_norm"] = _rms_bwd(x, p["b_norm"], [dh, dhf], dx, name=f"{tag}_drms")
    return dx, g


def kernel(x, a_norm, a_w_in, a_conv_w, a_conv_b, a_ln_g, a_ln_b, a_w_out, b_norm, b_w_in, b_f_bias, b_q_norm, b_k_norm, b_w_out, c_norm, c_w_in, c_conv_w, c_w_out, loss_target, m_a_norm, m_a_w_in, m_a_conv_w, m_a_conv_b, m_a_ln_g, m_a_ln_b, m_a_w_out, m_b_norm, m_b_w_in, m_b_f_bias, m_b_q_norm, m_b_k_norm, m_b_w_out, m_c_norm, m_c_w_in, m_c_conv_w, m_c_w_out, v_a_norm, v_a_w_in, v_a_conv_w, v_a_conv_b, v_a_ln_g, v_a_ln_b, v_a_w_out, v_b_norm, v_b_w_in, v_b_f_bias, v_b_q_norm, v_b_k_norm, v_b_w_out, v_c_norm, v_c_w_in, v_c_conv_w, v_c_w_out):
    w_loc = dict(a_norm=a_norm, a_w_in=a_w_in, a_conv_w=a_conv_w, a_conv_b=a_conv_b, a_ln_g=a_ln_g, a_ln_b=a_ln_b,
                 a_w_out=a_w_out, b_norm=b_norm, b_w_in=b_w_in, b_f_bias=b_f_bias, b_q_norm=b_q_norm, b_k_norm=b_k_norm,
                 b_w_out=b_w_out, c_norm=c_norm, c_w_in=c_w_in, c_conv_w=c_conv_w, c_w_out=c_w_out)
    m_loc = dict(a_norm=m_a_norm, a_w_in=m_a_w_in, a_conv_w=m_a_conv_w, a_conv_b=m_a_conv_b, a_ln_g=m_a_ln_g,
                 a_ln_b=m_a_ln_b, a_w_out=m_a_w_out, b_norm=m_b_norm, b_w_in=m_b_w_in, b_f_bias=m_b_f_bias,
                 b_q_norm=m_b_q_norm, b_k_norm=m_b_k_norm, b_w_out=m_b_w_out, c_norm=m_c_norm, c_w_in=m_c_w_in,
                 c_conv_w=m_c_conv_w, c_w_out=m_c_w_out)
    v_loc = dict(a_norm=v_a_norm, a_w_in=v_a_w_in, a_conv_w=v_a_conv_w, a_conv_b=v_a_conv_b, a_ln_g=v_a_ln_g,
                 a_ln_b=v_a_ln_b, a_w_out=v_a_w_out, b_norm=v_b_norm, b_w_in=v_b_w_in, b_f_bias=v_b_f_bias,
                 b_q_norm=v_b_q_norm, b_k_norm=v_b_k_norm, b_w_out=v_b_w_out, c_norm=v_c_norm, c_w_in=v_c_w_in,
                 c_conv_w=v_c_conv_w, c_w_out=v_c_w_out)
    n_a = a_w_in.shape[0]
    d_model = x.shape[2]
    e_inner = a_w_out.shape[1] * N_CHIPS
    n_heads = b_f_bias.shape[1]
    nb_loc = b_w_in.shape[2]
    nb_pad = -(-nb_loc // LANES) * LANES
    chip = 2 * lax.axis_index("x") + lax.axis_index("y")

    big_shards = ([a_w_in[l].astype(BF16) for l in range(n_a)] + [a_w_out[l].astype(BF16) for l in range(n_a)]
                  + [_pad_cols(b_w_in[0], nb_pad).astype(BF16), b_w_out[0].astype(BF16), c_w_in[0].astype(BF16),
                     c_w_out[0].astype(BF16)])
    small_pack = _pack([w_loc[n] for n in SMALL_SHARDED])
    gathered = _gather_weights(big_shards + [small_pack], name="gather_weights")
    p = {}
    p["a_w_in"] = gathered[0:n_a]
    p["a_w_out"] = [g.reshape(e_inner, d_model) for g in gathered[n_a:2 * n_a]]
    gb, gbo, gci, gco, gsmall = gathered[2 * n_a:]
    wb_full = jnp.concatenate([gb[k, :, :nb_loc] for k in range(N_CHIPS)], axis=1)
    p["b_wq"] = wb_full[:, :4 * e_inner]
    p["b_wf"] = _pad_cols(wb_full[:, 4 * e_inner:], LANES)
    p["b_w_out"] = gbo.reshape(e_inner, d_model)
    p["c_w_in"] = gci
    p["c_w_out"] = gco.reshape(e_inner, d_model)
    small_shapes = [w_loc[n].shape for n in SMALL_SHARDED]
    per_chip = [_unpack(gsmall[k], small_shapes) for k in range(N_CHIPS)]
    for idx, n in enumerate(SMALL_SHARDED):
        p[n] = jnp.concatenate([per_chip[k][idx] for k in range(N_CHIPS)], axis=-1)
    for n in SMALL_REPLICATED:
        p[n] = w_loc[n]

    x0 = x[0]
    x1, sv0 = _mixer_a_fwd(x0, p, 0, "a0")
    x2, sv1 = _mixer_b_fwd(x1, p, "b0")
    x3, sv2 = _mixer_c_fwd(x2, p, "c0")
    x4, sv3 = _mixer_a_fwd(x3, p, 1, "a1")
    dy, loss_part = _loss_head(x4, loss_target[0], name="loss_head")
    loss = lax.psum(loss_part[0, 0], ("x", "y", "c"))

    dx, g3 = _mixer_a_bwd(dy, sv3, p, 1, "a1")
    dx, g2 = _mixer_c_bwd(dx, sv2, p, "c0")
    dx, g1 = _mixer_b_bwd(dx, sv1, p, "b0")
    dx, g0 = _mixer_a_bwd(dx, sv0, p, 0, "a0")
    grad_x = dx[None]

    half_rows = e_inner // N_CHIPS
    gb_full = g1["b_w_in"].reshape(d_model, N_CHIPS, nb_loc).transpose(1, 0, 2)
    big_grads = ([g0["a_w_in"], g3["a_w_in"]]
                 + [g0["a_w_out"].reshape(N_CHIPS, half_rows, d_model), g3["a_w_out"].reshape(N_CHIPS, half_rows, d_model)]
                 + [_pad_cols(gb_full, nb_pad), g1["b_w_out"].reshape(N_CHIPS, half_rows, d_model), g2["c_w_in"],
                    g2["c_w_out"].reshape(N_CHIPS, half_rows, d_model)])
    red, c_arr = _reduce_grads(big_grads, name="reduce_grads")
    big_order = [("a_w_in", 0), ("a_w_in", 1), ("a_w_out", 0), ("a_w_out", 1), ("b_w_in", 0), ("b_w_out", 0),
                 ("c_w_in", 0), ("c_w_out", 0)]

    grads, delta, new_m, new_v = {}, {}, {}, {}
    per_layer = {n: [] for n in BIG}
    for (n, l), (mine, other) in zip(big_order, red):
        cols = mine.shape[1]
        true_cols = w_loc[n].shape[-1]
        outs = _adamw_halves(_pad_cols(w_loc[n][l], cols), mine, other, _pad_cols(m_loc[n][l], cols),
                             _pad_cols(v_loc[n][l], cols), c_arr, name=f"adamw_{n}{l}")
        per_layer[n].append([a[:, :true_cols] for a in outs])
    for n in BIG:
        grads[n], delta[n], new_m[n], new_v[n] = [jnp.stack([layer[k] for layer in per_layer[n]]) for k in range(4)]

    small_full = {}
    for n in ("a_norm", "a_conv_w", "a_conv_b", "a_ln_g", "a_ln_b"):
        small_full[n] = jnp.stack([g0[n].reshape(p[n].shape[1:]), g3[n].reshape(p[n].shape[1:])])
    small_full["c_norm"] = g2["c_norm"].reshape(p["c_norm"].shape)
    small_full["c_conv_w"] = g2["c_conv_w"].reshape(p["c_conv_w"].shape)
    for n in SMALL_REPLICATED:
        small_full[n] = g1[n].reshape(w_loc[n].shape)
    small_names = SMALL_SHARDED + SMALL_REPLICATED
    summed = _unpack(_allreduce_small(_pack([small_full[n] for n in small_names], 8), name="reduce_small"),
                     [small_full[n].shape for n in small_names])
    for n, s in zip(small_names, summed):
        if n in SMALL_SHARDED:
            width = w_loc[n].shape[-1]
            grads[n] = lax.dynamic_slice_in_dim(s, chip * width, width, axis=s.ndim - 1)
        else:
            grads[n] = s

    small_shapes_all = [w_loc[n].shape for n in small_names]
    _, d_, m_, v_ = _adamw(_pack([w_loc[n] for n in small_names], 8), [_pack([grads[n] for n in small_names], 8)],
                           _pack([m_loc[n] for n in small_names], 8), _pack([v_loc[n] for n in small_names], 8),
                           name="adamw_small")
    for n, a, b, c_ in zip(small_names, _unpack(d_, small_shapes_all), _unpack(m_, small_shapes_all),
                           _unpack(v_, small_shapes_all)):
        delta[n], new_m[n], new_v[n] = a, b, c_

    return (loss, grad_x, *[grads[n] for n in WEIGHTS], *[delta[n] for n in WEIGHTS],
            *[new_m[n] for n in WEIGHTS], *[new_v[n] for n in WEIGHTS])
```

```python
import functools

import numpy as np
import jax
import jax.numpy as jnp
from jax import lax
from jax.experimental import pallas as pl
from jax.experimental.pallas import tpu as pltpu

F32 = jnp.float32
BF16 = jnp.bfloat16
NORM_EPS = 1e-6
HEAD_DIM = 128
LANES = 128
N_CHIPS = 4
N_DEV = 8
VMEM_CAP = 56 << 20
MESH_ID = pl.DeviceIdType.MESH

ADAM_LR = 0.001
ADAM_B1 = 0.9
ADAM_B2 = 0.999
ADAM_EPS = 1e-08
ADAM_WD = 0.01
ADAM_STEP = 10


def _sds(shape, dtype):
    return jax.ShapeDtypeStruct(tuple(shape), dtype)


def _tile(n, pref):
    if n <= pref:
        return n
    for t in range(pref - pref % 8, 0, -8):
        if n % t == 0:
            return t
    raise ValueError(f"no tile for {n} under {pref}")


def _params(vmem_bytes):
    return pltpu.CompilerParams(vmem_limit_bytes=int(min(max(vmem_bytes, 16 << 20), VMEM_CAP)))


def _sigmoid(v):
    return 1.0 / (1.0 + jnp.exp(-v))


def _silu(v):
    return v * _sigmoid(v)


def _silu_pair(v):
    s = _sigmoid(v)
    return v * s, s * (1.0 + v * (1.0 - s))


def _rowsum8(v):
    r, c = v.shape
    return jnp.sum(v.reshape(r // 8, 8, c), axis=0)


def _col_spec(grouped, rows_block, tile, width, row_of, col_of):
    if grouped:
        per = width // tile
        return pl.BlockSpec((None, rows_block, tile), lambda *ids: (col_of(*ids) // per, row_of(*ids), col_of(*ids) % per))
    return pl.BlockSpec((rows_block, tile), lambda *ids: (row_of(*ids), col_of(*ids)))


def _mm_nn(a, b, *, name, tm=1024, tn=512, out_dtype=F32, add=None):
    m, k = a.shape
    b_grouped = b.ndim == 3
    n = b.shape[0] * b.shape[2] if b_grouped else b.shape[1]
    width = b.shape[2] if b_grouped else n
    tm, tn = _tile(m, tm), _tile(width, tn)

    def body(*refs):
        if add is None:
            a_ref, b_ref, o_ref = refs
        else:
            a_ref, b_ref, r_ref, o_ref = refs
        acc = jnp.dot(a_ref[...].astype(BF16), b_ref[...].astype(BF16), preferred_element_type=F32)
        if add is not None:
            acc = acc + r_ref[...]
        o_ref[...] = acc.astype(o_ref.dtype)

    in_specs = [pl.BlockSpec((tm, k), lambda i, j: (i, 0)),
                _col_spec(b_grouped, k, tn, width, lambda i, j: 0, lambda i, j: j)]
    args = [a, b]
    if add is not None:
        in_specs.append(pl.BlockSpec((tm, tn), lambda i, j: (i, j)))
        args.append(add)
    vmem = 2 * (tm * k * a.dtype.itemsize + k * tn * 2 + tm * tn * 4 * (2 if add is not None else 1)) + tm * tn * 8 + tm * k * 2
    return pl.pallas_call(
        body, name=name, grid=(m // tm, n // tn), in_specs=in_specs,
        out_specs=pl.BlockSpec((tm, tn), lambda i, j: (i, j)),
        out_shape=_sds((m, n), out_dtype), compiler_params=_params(vmem + (4 << 20)),
    )(*args)


def _mm_nt(a, b, *, name, tm=1024, tn=1024, tk=512, ksub=1):
    a_grouped, b_grouped = a.ndim == 3, b.ndim == 3
    m = a.shape[1] if a_grouped else a.shape[0]
    n = a.shape[0] * a.shape[2] if a_grouped else a.shape[1]
    kk = b.shape[1] if b_grouped else b.shape[0]
    wa = a.shape[2] if a_grouped else n
    wb = b.shape[2] if b_grouped else n
    tm, tn = _tile(m, tm), _tile(kk, tn)
    tk = _tile(int(np.gcd(wa, wb)), tk)
    ksub = min(ksub, n // tk)
    assert (n // tk) % ksub == 0, (n, tk, ksub)
    steps = n // (tk * ksub)

    def body(*refs):
        a_refs, b_refs, o_ref = refs[:ksub], refs[ksub:2 * ksub], refs[2 * ksub]
        part = None
        for a_ref, b_ref in zip(a_refs, b_refs):
            d = lax.dot_general(a_ref[...].astype(BF16), b_ref[...].astype(BF16), (((1,), (1,)), ((), ())),
                                preferred_element_type=F32)
            part = d if part is None else part + d

        @pl.when(pl.program_id(2) == 0)
        def _():
            o_ref[...] = part

        @pl.when(pl.program_id(2) > 0)
        def _():
            o_ref[...] += part

    def sub(u):
        return lambda i, j, s: s * ksub + u

    in_specs = ([_col_spec(a_grouped, tm, tk, wa, lambda i, j, s: i, sub(u)) for u in range(ksub)]
                + [_col_spec(b_grouped, tn, tk, wb, lambda i, j, s: j, sub(u)) for u in range(ksub)])
    vmem = (2 * ksub * (tm * tk * a.dtype.itemsize + tn * tk * b.dtype.itemsize) + 2 * tm * tn * 4 + 2 * tm * tn * 4
            + (tm + tn) * tk * 2)
    return pl.pallas_call(
        body, name=name, grid=(m // tm, kk // tn, steps), in_specs=in_specs,
        out_specs=pl.BlockSpec((tm, tn), lambda i, j, s: (i, j)),
        out_shape=_sds((m, kk), F32), compiler_params=_params(vmem + (4 << 20)),
    )(*([a] * ksub), *([b] * ksub))


def _mm_tn(a, b, *, name, out_width=None, out_dtype=F32, tk=1024, tn=512, ts=512):
    s_len, k = a.shape
    b_grouped = b.ndim == 3
    n = b.shape[0] * b.shape[2] if b_grouped else b.shape[1]
    wb = b.shape[2] if b_grouped else n
    wo = out_width if out_width is not None else n
    tk, ts = _tile(k, tk), _tile(s_len, ts)
    tn = _tile(int(np.gcd(wb, wo)), tn)
    last = s_len // ts - 1
    direct = out_dtype == F32

    def body(a_ref, b_ref, o_ref, *scratch):
        acc = o_ref if direct else scratch[0]
        part = lax.dot_general(a_ref[...].astype(BF16), b_ref[...].astype(BF16), (((0,), (0,)), ((), ())),
                               preferred_element_type=F32)

        @pl.when(pl.program_id(2) == 0)
        def _():
            acc[...] = part

        @pl.when(pl.program_id(2) > 0)
        def _():
            acc[...] += part

        if not direct:
            @pl.when(pl.program_id(2) == last)
            def _():
                o_ref[...] = acc[...].astype(o_ref.dtype)

    in_specs = [pl.BlockSpec((ts, tk), lambda i, j, s: (s, i)),
                _col_spec(b_grouped, ts, tn, wb, lambda i, j, s: s, lambda i, j, s: j)]
    out_grouped = out_width is not None
    out_spec = _col_spec(out_grouped, tk, tn, wo, lambda i, j, s: i, lambda i, j, s: j)
    out_shape = _sds((n // wo, k, wo), out_dtype) if out_grouped else _sds((k, n), out_dtype)
    vmem = 2 * (ts * tk * a.dtype.itemsize + ts * tn * b.dtype.itemsize + tk * tn * 4) + 2 * tk * tn * 4 + ts * (tk + tn) * 4
    return pl.pallas_call(
        body, name=name, grid=(k // tk, n // tn, s_len // ts), in_specs=in_specs, out_specs=out_spec,
        out_shape=out_shape, scratch_shapes=[] if direct else [pltpu.VMEM((tk, tn), F32)],
        compiler_params=_params(vmem + (4 << 20)),
    )(a, b)


def _rms_fwd(x, g, *, name, ts=512):
    s_len, d = x.shape
    ts = _tile(s_len, ts)

    def body(x_ref, g_ref, h_ref):
        xf = x_ref[...]
        r = lax.rsqrt(jnp.mean(xf * xf, axis=-1, keepdims=True) + NORM_EPS)
        h_ref[...] = ((xf * r) * g_ref[...]).astype(h_ref.dtype)

    return pl.pallas_call(
        body, name=name, grid=(s_len // ts,),
        in_specs=[pl.BlockSpec((ts, d), lambda i: (i, 0)), pl.BlockSpec((1, d), lambda i: (0, 0))],
        out_specs=pl.BlockSpec((ts, d), lambda i: (i, 0)), out_shape=_sds((s_len, d), BF16),
        compiler_params=_params(8 * ts * d * 4),
    )(x, g)


def _rms_bwd(x, g, dhs, dres, *, name, ts=512):
    s_len, d = x.shape
    ts = _tile(s_len, ts)
    n_dh = len(dhs)
    last = s_len // ts - 1

    def body(*refs):
        x_ref, g_ref = refs[0], refs[1]
        dh_refs = refs[2:2 + n_dh]
        dres_ref, dx_ref, dg_ref, acc = refs[2 + n_dh:]
        i = pl.program_id(0)

        @pl.when(i == 0)
        def _():
            acc[...] = jnp.zeros_like(acc)

        xf = x_ref[...]
        dy = dh_refs[0][...]
        for r_ in dh_refs[1:]:
            dy = dy + r_[...]
        r = lax.rsqrt(jnp.mean(xf * xf, axis=-1, keepdims=True) + NORM_EPS)
        gd = dy * g_ref[...]
        dot = jnp.mean(xf * gd, axis=-1, keepdims=True)
        dx_ref[...] = dres_ref[...] + r * gd - xf * (r * r * r * dot)
        acc[...] += _rowsum8(dy * (xf * r))

        @pl.when(i == last)
        def _():
            dg_ref[...] = jnp.sum(acc[...], axis=0, keepdims=True)

    row = pl.BlockSpec((ts, d), lambda i: (i, 0))
    vec = pl.BlockSpec((1, d), lambda i: (0, 0))
    return pl.pallas_call(
        body, name=name, grid=(s_len // ts,),
        in_specs=[row, vec] + [row] * n_dh + [row],
        out_specs=[row, vec], out_shape=[_sds((s_len, d), F32), _sds((1, d), F32)],
        scratch_shapes=[pltpu.VMEM((8, d), F32)],
        compiler_params=_params((2 * (3 + n_dh) + 6) * ts * d * 4),
    )(x, g, *dhs, dres)


def _loss_head(y, target, *, name, ts=512):
    s_len, d = y.shape
    ts = _tile(s_len, ts)
    last = s_len // ts - 1

    def body(y_ref, t_ref, dy_ref, loss_ref, acc):
        i = pl.program_id(0)

        @pl.when(i == 0)
        def _():
            acc[...] = jnp.zeros_like(acc)

        err = y_ref[...] - t_ref[...]
        dy_ref[...] = err / d
        acc[...] += _rowsum8(err * err)

        @pl.when(i == last)
        def _():
            loss_ref[...] = (0.5 * jnp.sum(acc[...]) / d).reshape(1, 1)

    row = pl.BlockSpec((ts, d), lambda i: (i, 0))
    return pl.pallas_call(
        body, name=name, grid=(s_len // ts,), in_specs=[row, row],
        out_specs=[row, pl.BlockSpec((1, 1), lambda i: (0, 0))],
        out_shape=[_sds((s_len, d), F32), _sds((1, 1), F32)],
        scratch_shapes=[pltpu.VMEM((8, d), F32)],
        compiler_params=_params(10 * ts * d * 4),
    )(y, target)


CONV_ROWS = 64
CONV_COLS = 256


SUBLANES = 8


def _fill_shifts(buf, sh_ref):
    n = sh_ref.shape[1]
    for s in range(1, SUBLANES):
        sh_ref[s - 1, :, :] = buf[s:s + n, :]


def _shifted_rows(buf, sh_ref, start, rw, cs):
    s = start % SUBLANES
    if sh_ref is None or s == 0:
        return buf[start:start + rw, cs]
    return sh_ref[s - 1, start - s:start - s + rw, cs]


def _conv_taps(buf, sh_ref, w_ref, k_width, base, ts, tc, init, emit, reverse=False):
    cw = min(tc, CONV_COLS)
    rw = min(ts, CONV_ROWS)
    for cb in range(tc // cw):
        cs = slice(cb * cw, (cb + 1) * cw)
        for rb in range(ts // rw):
            acc = init(slice(rb * rw, (rb + 1) * rw), cs, (rw, cw))
            for k in range(k_width):
                sh = (k_width - 1 - k) if reverse else k
                acc = acc + w_ref[k:k + 1, cs] * _shifted_rows(buf, sh_ref, base + rb * rw + sh, rw, cs)
            emit(slice(rb * rw, (rb + 1) * rw), cs, acc)


def _conv_wgrad(buf, sh_ref, d_ref_val, acc_ref, k_width, base, ts, tc):
    cw = min(tc, CONV_COLS)
    rw = min(ts, CONV_ROWS)
    for cb in range(tc // cw):
        cs = slice(cb * cw, (cb + 1) * cw)
        for rb in range(ts // rw):
            dv = d_ref_val[rb * rw:(rb + 1) * rw, cs]
            for k in range(k_width):
                prod = dv * _shifted_rows(buf, sh_ref, base + rb * rw + k, rw, cs)
                acc_ref[8 * k:8 * k + 8, cs] += _rowsum8(prod)


A_HALO = 32


def _a_conv_fwd(proj, conv_w, conv_b, *, name, ts=1024, tc=128):
    s_len, e3 = proj.shape
    e = e3 // 3
    k_width = conv_w.shape[0]
    ts, tc = _tile(s_len, ts), _tile(e, tc)
    nc = e // tc
    kp = 32

    def body(val, gate, valh, gateh, w_ref, b_ref, u1_ref, buf, sh):
        i = pl.program_id(1)
        u0h = valh[...] * _sigmoid(gateh[...])
        buf[0:A_HALO, :] = jnp.where(i > 0, u0h, 0.0)
        buf[A_HALO:A_HALO + ts, :] = val[...] * _sigmoid(gate[...])
        _fill_shifts(buf, sh)

        def init(rows, cs, shape):
            return jnp.broadcast_to(b_ref[:, cs], shape)

        def emit(rows, cs, acc):
            u1_ref[rows, cs] = acc

        _conv_taps(buf, sh, w_ref, k_width, A_HALO - (k_width - 1), ts, tc, init, emit)

    hb = ts // A_HALO
    in_specs = [
        pl.BlockSpec((ts, tc), lambda j, i: (i, j)),
        pl.BlockSpec((ts, tc), lambda j, i: (i, nc + j)),
        pl.BlockSpec((A_HALO, tc), lambda j, i: (jnp.maximum(i * hb - 1, 0), j)),
        pl.BlockSpec((A_HALO, tc), lambda j, i: (jnp.maximum(i * hb - 1, 0), nc + j)),
        pl.BlockSpec((kp, tc), lambda j, i: (0, j)),
        pl.BlockSpec((1, tc), lambda j, i: (0, j)),
    ]
    w_pad = jnp.zeros((kp, e), F32).at[:k_width].set(conv_w)
    return pl.pallas_call(
        body, name=name, grid=(nc, s_len // ts), in_specs=in_specs,
        out_specs=pl.BlockSpec((ts, tc), lambda j, i: (i, j)), out_shape=_sds((s_len, e), F32),
        scratch_shapes=[pltpu.VMEM((A_HALO + ts, tc), F32), pltpu.VMEM((SUBLANES - 1, A_HALO + ts - SUBLANES, tc), F32)],
        compiler_params=_params(24 * ts * tc * 4),
    )(proj, proj, proj, proj, w_pad, conv_b)


def _ln_rows(u1, g, b):
    mu = jnp.mean(u1, axis=-1, keepdims=True)
    xc = u1 - mu
    var = jnp.mean(xc * xc, axis=-1, keepdims=True)
    rstd = lax.rsqrt(var + NORM_EPS)
    xhat = xc * rstd
    return xhat, rstd, xhat * g + b


def _a_post_fwd(u1, proj, ln_g, ln_b, *, name, ts=256):
    s_len, e = u1.shape
    ts = _tile(s_len, ts)

    def body(u1_ref, z_ref, g_ref, b_ref, o_ref):
        _, _, u2 = _ln_rows(u1_ref[...], g_ref[...], b_ref[...])
        o_ref[...] = (_silu(u2) * _silu(z_ref[...])).astype(o_ref.dtype)

    row = pl.BlockSpec((ts, e), lambda i: (i, 0))
    vec = pl.BlockSpec((1, e), lambda i: (0, 0))
    return pl.pallas_call(
        body, name=name, grid=(s_len // ts,),
        in_specs=[row, pl.BlockSpec((ts, e), lambda i: (i, 2)), vec, vec],
        out_specs=row, out_shape=_sds((s_len, e), BF16), compiler_params=_params(12 * ts * e * 4),
    )(u1, proj, ln_g, ln_b)


def _a_post_bwd(dgated, u1, proj, ln_g, ln_b, *, name, ts=256):
    s_len, e = u1.shape
    ts = _tile(s_len, ts)
    last = s_len // ts - 1

    def body(dg_ref, u1_ref, z_ref, g_ref, b_ref, du1_ref, dz_ref, dlg_ref, dlb_ref, dcb_ref, a_g, a_b, a_c):
        i = pl.program_id(0)

        @pl.when(i == 0)
        def _():
            a_g[...] = jnp.zeros_like(a_g)
            a_b[...] = jnp.zeros_like(a_b)
            a_c[...] = jnp.zeros_like(a_c)

        g = g_ref[...]
        xhat, rstd, u2 = _ln_rows(u1_ref[...], g, b_ref[...])
        z = z_ref[...]
        dgt = dg_ref[...]
        su, dsu = _silu_pair(u2)
        sz, dsz = _silu_pair(z)
        dz_ref[...] = (dgt * su * dsz).astype(dz_ref.dtype)
        du2 = dgt * sz * dsu
        a_g[...] += _rowsum8(du2 * xhat)
        a_b[...] += _rowsum8(du2)
        dxh = du2 * g
        m1 = jnp.mean(dxh, axis=-1, keepdims=True)
        m2 = jnp.mean(dxh * xhat, axis=-1, keepdims=True)
        du1 = rstd * (dxh - m1 - xhat * m2)
        du1_ref[...] = du1
        a_c[...] += _rowsum8(du1)

        @pl.when(i == last)
        def _():
            dlg_ref[...] = jnp.sum(a_g[...], axis=0, keepdims=True)
            dlb_ref[...] = jnp.sum(a_b[...], axis=0, keepdims=True)
            dcb_ref[...] = jnp.sum(a_c[...], axis=0, keepdims=True)

    row = pl.BlockSpec((ts, e), lambda i: (i, 0))
    vec = pl.BlockSpec((1, e), lambda i: (0, 0))
    return pl.pallas_call(
        body, name=name, grid=(s_len // ts,),
        in_specs=[row, row, pl.BlockSpec((ts, e), lambda i: (i, 2)), vec, vec],
        out_specs=[row, row, vec, vec, vec],
        out_shape=[_sds((s_len, e), F32), _sds((s_len, e), BF16), _sds((1, e), F32), _sds((1, e), F32), _sds((1, e), F32)],
        scratch_shapes=[pltpu.VMEM((8, e), F32)] * 3,
        compiler_params=_params(20 * ts * e * 4),
    )(dgated, u1, proj, ln_g, ln_b)


def _a_conv_bwd(du1, proj, dz, conv_w, *, name, ts=1024, tc=128):
    s_len, e3 = proj.shape
    e = e3 // 3
    k_width = conv_w.shape[0]
    ts, tc = _tile(s_len, ts), _tile(e, tc)
    nc, nr = e // tc, s_len // ts
    kp = 32
    hb = ts // A_HALO

    def body(val, gate, valh, gateh, d_ref, dh_ref, dz_ref, w_ref, dp_ref, dw_ref, buf_u, buf_d, du0, acc, sh_u, sh_d):
        i = pl.program_id(1)

        @pl.when(i == 0)
        def _():
            acc[...] = jnp.zeros_like(acc)

        buf_u[0:A_HALO, :] = jnp.where(i > 0, valh[...] * _sigmoid(gateh[...]), 0.0)
        buf_u[A_HALO:A_HALO + ts, :] = val[...] * _sigmoid(gate[...])
        buf_d[0:ts, :] = d_ref[...]
        buf_d[ts:ts + A_HALO, :] = jnp.where(i < nr - 1, dh_ref[...], 0.0)
        _fill_shifts(buf_u, sh_u)
        _fill_shifts(buf_d, sh_d)

        def init(rows, cs, shape):
            return jnp.zeros(shape, F32)

        def emit(rows, cs, a):
            du0[rows, cs] = a

        _conv_taps(buf_d, sh_d, w_ref, k_width, 0, ts, tc, init, emit, reverse=True)
        _conv_wgrad(buf_u, sh_u, buf_d, acc, k_width, A_HALO - (k_width - 1), ts, tc)
        rw = min(ts, CONV_ROWS)
        for rb in range(ts // rw):
            rows = slice(rb * rw, (rb + 1) * rw)
            d0 = du0[rows, :]
            sg = _sigmoid(gate[rows, :])
            dp_ref[0, rows, :] = (d0 * sg).astype(dp_ref.dtype)
            dp_ref[1, rows, :] = (d0 * val[rows, :] * sg * (1.0 - sg)).astype(dp_ref.dtype)
        dp_ref[2] = dz_ref[...]

        @pl.when(i == nr - 1)
        def _():
            for k in range(kp):
                dw_ref[k:k + 1, :] = jnp.sum(acc[8 * k:8 * k + 8, :], axis=0, keepdims=True)

    in_specs = [
        pl.BlockSpec((ts, tc), lambda j, i: (i, j)),
        pl.BlockSpec((ts, tc), lambda j, i: (i, nc + j)),
        pl.BlockSpec((A_HALO, tc), lambda j, i: (jnp.maximum(i * hb - 1, 0), j)),
        pl.BlockSpec((A_HALO, tc), lambda j, i: (jnp.maximum(i * hb - 1, 0), nc + j)),
        pl.BlockSpec((ts, tc), lambda j, i: (i, j)),
        pl.BlockSpec((A_HALO, tc), lambda j, i: (jnp.minimum((i + 1) * hb, nr * hb - 1), j)),
        pl.BlockSpec((ts, tc), lambda j, i: (i, j)),
        pl.BlockSpec((kp, tc), lambda j, i: (0, j)),
    ]
    w_pad = jnp.zeros((kp, e), F32).at[:k_width].set(conv_w)
    dproj, dw = pl.pallas_call(
        body, name=name, grid=(nc, nr), in_specs=in_specs,
        out_specs=[pl.BlockSpec((3, ts, tc), lambda j, i: (0, i, j)), pl.BlockSpec((kp, tc), lambda j, i: (0, j))],
        out_shape=[_sds((3, s_len, e), BF16), _sds((kp, e), F32)],
        scratch_shapes=[pltpu.VMEM((A_HALO + ts, tc), F32), pltpu.VMEM((ts + A_HALO, tc), F32),
                        pltpu.VMEM((ts, tc), F32), pltpu.VMEM((8 * kp, tc), F32),
                        pltpu.VMEM((SUBLANES - 1, A_HALO + ts - SUBLANES, tc), F32),
                        pltpu.VMEM((SUBLANES - 1, A_HALO + ts - SUBLANES, tc), F32)],
        compiler_params=_params(56 * ts * tc * 4),
    )(proj, proj, proj, proj, du1, du1, dz, w_pad)
    return dproj, dw[:k_width]


C_HALO = 8


def _c_mid_fwd(proj, conv_w, *, name, ts=1024, tc=128):
    s_len, e4 = proj.shape
    e = e4 // 4
    k_width = conv_w.shape[0]
    ts, tc = _tile(s_len, ts), _tile(e, tc)
    nc = e // tc
    hb = ts // C_HALO

    def body(u, bg, cg, z, uh, cgh, w_ref, o_ref, buf, y):
        i = pl.program_id(1)
        buf[0:C_HALO, :] = jnp.where(i > 0, uh[...] * cgh[...], 0.0)
        buf[C_HALO:C_HALO + ts, :] = u[...] * cg[...]

        def init(rows, cs, shape):
            return jnp.zeros(shape, F32)

        def emit(rows, cs, a):
            y[rows, cs] = a

        _conv_taps(buf, None, w_ref, k_width, C_HALO - (k_width - 1), ts, tc, init, emit)
        o_ref[...] = (bg[...] * y[...] * _silu(z[...])).astype(o_ref.dtype)

    def grp(g):
        return pl.BlockSpec((ts, tc), lambda j, i: (i, g * nc + j))

    def halo(g):
        return pl.BlockSpec((C_HALO, tc), lambda j, i: (jnp.maximum(i * hb - 1, 0), g * nc + j))

    w_pad = jnp.zeros((8, e), F32).at[:k_width].set(conv_w)
    return pl.pallas_call(
        body, name=name, grid=(nc, s_len // ts),
        in_specs=[grp(0), grp(1), grp(2), grp(3), halo(0), halo(2), pl.BlockSpec((8, tc), lambda j, i: (0, j))],
        out_specs=pl.BlockSpec((ts, tc), lambda j, i: (i, j)), out_shape=_sds((s_len, e), BF16),
        scratch_shapes=[pltpu.VMEM((C_HALO + ts, tc), F32), pltpu.VMEM((ts, tc), F32)],
        compiler_params=_params(16 * ts * tc * 4),
    )(proj, proj, proj, proj, proj, proj, w_pad)


def _c_mid_bwd(dgated, proj, conv_w, *, name, ts=1024, tc=128):
    s_len, e4 = proj.shape
    e = e4 // 4
    k_width = conv_w.shape[0]
    ts, tc = _tile(s_len, ts), _tile(e, tc)
    nc, nr = e // tc, s_len // ts
    hb = ts // C_HALO

    def body(u, bg, cg, z, uh, cgh, dg, dgh, bgh, zh, w_ref, dp_ref, dw_ref, buf_p, buf_d, y, dpv, acc):
        i = pl.program_id(1)

        @pl.when(i == 0)
        def _():
            acc[...] = jnp.zeros_like(acc)

        uv, bgv, cgv, zv, dgv = u[...], bg[...], cg[...], z[...], dg[...]
        buf_p[0:C_HALO, :] = jnp.where(i > 0, uh[...] * cgh[...], 0.0)
        buf_p[C_HALO:C_HALO + ts, :] = uv * cgv
        sz, dsz = _silu_pair(zv)
        buf_d[0:ts, :] = dgv * sz * bgv
        buf_d[ts:ts + C_HALO, :] = jnp.where(i < nr - 1, dgh[...] * _silu(zh[...]) * bgh[...], 0.0)

        def init(rows, cs, shape):
            return jnp.zeros(shape, F32)

        def emit_y(rows, cs, a):
            y[rows, cs] = a

        def emit_dp(rows, cs, a):
            dpv[rows, cs] = a

        _conv_taps(buf_p, None, w_ref, k_width, C_HALO - (k_width - 1), ts, tc, init, emit_y)
        _conv_taps(buf_d, None, w_ref, k_width, 0, ts, tc, init, emit_dp, reverse=True)
        _conv_wgrad(buf_p, None, buf_d, acc, k_width, C_HALO - (k_width - 1), ts, tc)
        yv, dp = y[...], dpv[...]
        dp_ref[0] = (dp * cgv).astype(dp_ref.dtype)
        dp_ref[1] = (dgv * sz * yv).astype(dp_ref.dtype)
        dp_ref[2] = (dp * uv).astype(dp_ref.dtype)
        dp_ref[3] = (dgv * bgv * yv * dsz).astype(dp_ref.dtype)

        @pl.when(i == nr - 1)
        def _():
            for k in range(8):
                dw_ref[k:k + 1, :] = jnp.sum(acc[8 * k:8 * k + 8, :], axis=0, keepdims=True)

    def grp(g):
        return pl.BlockSpec((ts, tc), lambda j, i: (i, g * nc + j))

    def prev(g):
        return pl.BlockSpec((C_HALO, tc), lambda j, i: (jnp.maximum(i * hb - 1, 0), g * nc + j))

    def nxt(g):
        return pl.BlockSpec((C_HALO, tc), lambda j, i: (jnp.minimum((i + 1) * hb, nr * hb - 1), g * nc + j))

    w_pad = jnp.zeros((8, e), F32).at[:k_width].set(conv_w)
    dproj, dw = pl.pallas_call(
        body, name=name, grid=(nc, nr),
        in_specs=[grp(0), grp(1), grp(2), grp(3), prev(0), prev(2),
                  pl.BlockSpec((ts, tc), lambda j, i: (i, j)),
                  pl.BlockSpec((C_HALO, tc), lambda j, i: (jnp.minimum((i + 1) * hb, nr * hb - 1), j)),
                  nxt(1), nxt(3), pl.BlockSpec((8, tc), lambda j, i: (0, j))],
        out_specs=[pl.BlockSpec((4, ts, tc), lambda j, i: (0, i, j)), pl.BlockSpec((8, tc), lambda j, i: (0, j))],
        out_shape=[_sds((4, s_len, e), BF16), _sds((8, e), F32)],
        scratch_shapes=[pltpu.VMEM((C_HALO + ts, tc), F32), pltpu.VMEM((ts + C_HALO, tc), F32),
                        pltpu.VMEM((ts, tc), F32), pltpu.VMEM((ts, tc), F32), pltpu.VMEM((64, tc), F32)],
        compiler_params=_params(32 * ts * tc * 4),
    )(proj, proj, proj, proj, proj, proj, dgated, dgated, proj, proj, w_pad)
    return dproj, dw[:k_width]


def _head_rms(xv, g):
    r = lax.rsqrt(jnp.mean(xv * xv, axis=-1, keepdims=True) + NORM_EPS)
    return r, xv * r * g


def _b_qk_fwd(proj, gq, gk, *, name, ts=512, tc=512):
    s_len, e4 = proj.shape
    e = e4 // 4
    ts, tc = _tile(s_len, ts), _tile(e, tc)
    nc = e // tc

    def body(q, k, v, gq_ref, gk_ref, qn, kn, vb):
        for h in range(tc // HEAD_DIM):
            cs = slice(h * HEAD_DIM, (h + 1) * HEAD_DIM)
            qn[:, cs] = _head_rms(q[:, cs], gq_ref[...])[1].astype(qn.dtype)
            kn[:, cs] = _head_rms(k[:, cs], gk_ref[...])[1].astype(kn.dtype)
        vb[...] = v[...].astype(vb.dtype)

    def grp(g):
        return pl.BlockSpec((ts, tc), lambda i, j: (i, g * nc + j))

    vec = pl.BlockSpec((1, HEAD_DIM), lambda i, j: (0, 0))
    out = pl.BlockSpec((ts, tc), lambda i, j: (i, j))
    return pl.pallas_call(
        body, name=name, grid=(s_len // ts, nc), in_specs=[grp(0), grp(1), grp(2), vec, vec],
        out_specs=[out, out, out], out_shape=[_sds((s_len, e), BF16)] * 3,
        compiler_params=_params(16 * ts * tc * 4),
    )(proj, proj, proj, gq, gk)


def _b_qk_bwd(dqn, dkn, dv, dz, proj, gq, gk, *, name, ts=512, tc=512):
    s_len, e4 = proj.shape
    e = e4 // 4
    ts, tc = _tile(s_len, ts), _tile(e, tc)
    nc, nr = e // tc, s_len // ts

    def body(dq_ref, dk_ref, dv_ref, dz_ref, q, k, gq_ref, gk_ref, dp_ref, dgq_ref, dgk_ref, a_q, a_k):
        i, j = pl.program_id(0), pl.program_id(1)

        @pl.when((i == 0) & (j == 0))
        def _():
            a_q[...] = jnp.zeros_like(a_q)
            a_k[...] = jnp.zeros_like(a_k)

        for h in range(tc // HEAD_DIM):
            cs = slice(h * HEAD_DIM, (h + 1) * HEAD_DIM)
            for slot, src, d_ref, g_ref, acc in ((0, q, dq_ref, gq_ref, a_q), (1, k, dk_ref, gk_ref, a_k)):
                xv = src[:, cs]
                dy = d_ref[:, cs]
                r = lax.rsqrt(jnp.mean(xv * xv, axis=-1, keepdims=True) + NORM_EPS)
                gd = dy * g_ref[...]
                dot = jnp.mean(xv * gd, axis=-1, keepdims=True)
                dp_ref[slot, :, cs] = (r * gd - xv * (r * r * r * dot)).astype(dp_ref.dtype)
                acc[...] += _rowsum8(dy * (xv * r))
        dp_ref[2] = dv_ref[...].astype(dp_ref.dtype)
        dp_ref[3] = dz_ref[...]

        @pl.when((i == nr - 1) & (j == nc - 1))
        def _():
            dgq_ref[...] = jnp.sum(a_q[...], axis=0, keepdims=True)
            dgk_ref[...] = jnp.sum(a_k[...], axis=0, keepdims=True)

    blk = pl.BlockSpec((ts, tc), lambda i, j: (i, j))
    vec = pl.BlockSpec((1, HEAD_DIM), lambda i, j: (0, 0))

    def grp(g):
        return pl.BlockSpec((ts, tc), lambda i, j: (i, g * nc + j))

    return pl.pallas_call(
        body, name=name, grid=(nr, nc), in_specs=[blk, blk, blk, blk, grp(0), grp(1), vec, vec],
        out_specs=[pl.BlockSpec((4, ts, tc), lambda i, j: (0, i, j)), vec, vec],
        out_shape=[_sds((4, s_len, e), BF16), _sds((1, HEAD_DIM), F32), _sds((1, HEAD_DIM), F32)],
        scratch_shapes=[pltpu.VMEM((8, HEAD_DIM), F32)] * 2,
        compiler_params=_params(24 * ts * tc * 4),
    )(dqn, dkn, dv, dz, proj, proj, gq, gk)


def _log_sigmoid(x):
    y = jnp.exp(-jnp.abs(x))
    u = 1.0 + y
    log1p = jnp.where(u == 1.0, y, jnp.log(u) * (y / jnp.where(u == 1.0, 1.0, u - 1.0)))
    return jnp.minimum(x, 0.0) - log1p


def _split3(v):
    hi = v.astype(BF16)
    r1 = v - hi.astype(F32)
    mid = r1.astype(BF16)
    lo = (r1 - mid.astype(F32)).astype(BF16)
    return hi, mid, lo


def _tri_matmul(tri, v):
    hi, mid, lo = _split3(v)
    return (jnp.dot(tri, hi, preferred_element_type=F32) + jnp.dot(tri, mid, preferred_element_type=F32)
            + jnp.dot(tri, lo, preferred_element_type=F32))


def _b_cumsum(fl, bias, *, name, t=512):
    s_len, w = fl.shape
    t = _tile(s_len, t)

    def body(fl_ref, b_ref, c_ref, carry):
        @pl.when(pl.program_id(0) == 0)
        def _():
            carry[...] = jnp.zeros_like(carry)

        logf = _log_sigmoid(fl_ref[...] + b_ref[...])
        row = lax.broadcasted_iota(jnp.int32, (t, t), 0)
        col = lax.broadcasted_iota(jnp.int32, (t, t), 1)
        tri = jnp.where(col <= row, 1.0, 0.0).astype(BF16)
        c = _tri_matmul(tri, logf) + carry[...]
        c_ref[...] = c
        carry[...] = c[t - 1:t, :]

    return pl.pallas_call(
        body, name=name, grid=(s_len // t,),
        in_specs=[pl.BlockSpec((t, w), lambda i: (i, 0)), pl.BlockSpec((1, w), lambda i: (0, 0))],
        out_specs=pl.BlockSpec((t, w), lambda i: (i, 0)), out_shape=_sds((s_len, w), F32),
        scratch_shapes=[pltpu.VMEM((1, w), F32)], compiler_params=_params(16 << 20),
    )(fl, bias)


def _b_cumsum_bwd(dc, fl, bias, *, name, t=512):
    s_len, w = fl.shape
    t = _tile(s_len, t)
    n = s_len // t

    def body(dc_ref, fl_ref, b_ref, dfl_ref, db_ref, carry, acc):
        i = pl.program_id(0)

        @pl.when(i == 0)
        def _():
            carry[...] = jnp.zeros_like(carry)
            acc[...] = jnp.zeros_like(acc)

        row = lax.broadcasted_iota(jnp.int32, (t, t), 0)
        col = lax.broadcasted_iota(jnp.int32, (t, t), 1)
        tri = jnp.where(col >= row, 1.0, 0.0).astype(BF16)
        dlogf = _tri_matmul(tri, dc_ref[...]) + carry[...]
        carry[...] = dlogf[0:1, :]
        dfl = dlogf * _sigmoid(-(fl_ref[...] + b_ref[...]))
        dfl_ref[...] = dfl
        acc[...] += _rowsum8(dfl)

        @pl.when(i == n - 1)
        def _():
            db_ref[...] = jnp.sum(acc[...], axis=0, keepdims=True)

    rev = pl.BlockSpec((t, w), lambda i: (n - 1 - i, 0))
    vec = pl.BlockSpec((1, w), lambda i: (0, 0))
    return pl.pallas_call(
        body, name=name, grid=(n,), in_specs=[rev, rev, vec], out_specs=[rev, vec],
        out_shape=[_sds((s_len, w), F32), _sds((1, w), F32)],
        scratch_shapes=[pltpu.VMEM((1, w), F32), pltpu.VMEM((8, w), F32)], compiler_params=_params(16 << 20),
    )(dc, fl, bias)


def _b_gate_fwd(o, proj, *, name, ts=512, tc=512):
    s_len, e = o.shape
    ts, tc = _tile(s_len, ts), _tile(e, tc)
    nc = e // tc

    def body(o_ref, z_ref, g_ref):
        g_ref[...] = (o_ref[...] * _silu(z_ref[...])).astype(g_ref.dtype)

    blk = pl.BlockSpec((ts, tc), lambda i, j: (i, j))
    return pl.pallas_call(
        body, name=name, grid=(s_len // ts, nc),
        in_specs=[blk, pl.BlockSpec((ts, tc), lambda i, j: (i, 3 * nc + j))], out_specs=blk,
        out_shape=_sds((s_len, e), BF16), compiler_params=_params(12 * ts * tc * 4),
    )(o, proj)


def _b_gate_bwd(dgated, o, o_lo, proj, *, name, hg, ts=512):
    s_len, e = o.shape
    ts = _tile(s_len, ts)
    w = hg * HEAD_DIM
    ng = e // w

    def body(dg_ref, o_ref, olo_ref, z_ref, do_ref, dz_ref, dl_ref):
        dgt, ov, zv = dg_ref[...], o_ref[...], z_ref[...]
        sz, dsz = _silu_pair(zv)
        dob = (dgt * sz).astype(do_ref.dtype)
        do_ref[...] = dob
        dz_ref[...] = (dgt * ov * dsz).astype(dz_ref.dtype)
        prod = dob.astype(F32) * (ov + olo_ref[...])
        for hh in range(hg):
            cs = slice(hh * HEAD_DIM, (hh + 1) * HEAD_DIM)
            dl_ref[:, cs] = jnp.broadcast_to(jnp.sum(prod[:, cs], axis=-1, keepdims=True), (ts, HEAD_DIM))

    blk = pl.BlockSpec((ts, w), lambda i, j: (i, j))
    return pl.pallas_call(
        body, name=name, grid=(s_len // ts, ng),
        in_specs=[blk, blk, blk, pl.BlockSpec((ts, w), lambda i, j: (i, 3 * ng + j))],
        out_specs=[blk, blk, blk],
        out_shape=[_sds((s_len, e), BF16), _sds((s_len, e), BF16), _sds((s_len, e), F32)],
        compiler_params=_params(24 * ts * w * 4),
    )(dgated, o, o_lo, proj)


LOG2E = 1.4426950408889634
ATTN_ROW_CHUNK = 128


def _tri_tables(n, k_major):
    pairs = [(i, j) for j in range(n) for i in range(j, n)] if k_major else [(i, j) for i in range(n) for j in range(i + 1)]
    return (jnp.asarray(np.array([p[0] for p in pairs], np.int32)), jnp.asarray(np.array([p[1] for p in pairs], np.int32)))


def _attn_logits2(s_raw, cr2, diag, row0, c1):
    s2 = s_raw * c1 - cr2
    if diag:
        rc, t = s_raw.shape
        row = lax.broadcasted_iota(jnp.int32, (rc, t), 0) + row0
        col = lax.broadcasted_iota(jnp.int32, (rc, t), 1)
        s2 = jnp.where(col <= row, s2, -jnp.inf)
    return s2


def _fox_fwd(qn, kn, vb, c_row, *, name, hg, t=512):
    s_len, e = qn.shape
    t = _tile(s_len, t)
    rc = _tile(t, ATTN_ROW_CHUNK)
    w = hg * HEAD_DIM
    ng, n = e // w, s_len // t
    c1 = HEAD_DIM ** -0.5 * LOG2E
    qi_tab, kj_tab = _tri_tables(n, k_major=False)

    def body(qi_ref, kj_ref, q_ref, k_ref, v_ref, cr_ref, o_ref, olo_ref, lse_ref, m_s, l_s, acc_s, lo_s, s_scr, p_scr,
             a_scr):
        pid = pl.program_id(1)
        i, j = qi_ref[pid], kj_ref[pid]

        @pl.when(j == 0)
        def _():
            m_s[...] = jnp.full_like(m_s, -jnp.inf)
            l_s[...] = jnp.zeros_like(l_s)
            acc_s[...] = jnp.zeros_like(acc_s)
            lo_s[...] = jnp.zeros_like(lo_s)

        def step(diag):
            for hh in range(hg):
                cs = slice(hh * HEAD_DIM, (hh + 1) * HEAD_DIM)
                s_scr[...] = lax.dot_general(q_ref[:, cs], k_ref[:, cs], (((1,), (1,)), ((), ())),
                                             preferred_element_type=F32)
                cr2 = cr_ref[hh] * LOG2E
                for r in range(t // rc):
                    rows = slice(r * rc, (r + 1) * rc)
                    s2 = _attn_logits2(s_scr[rows, :], cr2, diag, r * rc, c1)
                    m_prev = m_s[hh, rows]
                    m_new = jnp.maximum(m_prev, jnp.max(s2, axis=-1, keepdims=True))
                    alpha = jnp.exp2(m_prev - m_new)
                    p = jnp.exp2(s2 - jnp.tile(m_new, (1, t // LANES)))
                    l_s[hh, rows] = alpha * l_s[hh, rows] + jnp.sum(p, axis=-1, keepdims=True)
                    m_s[hh, rows] = m_new
                    a_scr[rows] = alpha
                    p_hi = p.astype(BF16)
                    p_scr[rows, :] = p_hi
                    p_scr[t + r * rc:t + (r + 1) * rc, :] = (p - p_hi.astype(F32)).astype(BF16)
                pv = jnp.dot(p_scr[...], v_ref[:, cs], preferred_element_type=F32)
                al = a_scr[...]
                acc_s[:, cs] = al * acc_s[:, cs] + pv[:t]
                lo_s[:, cs] = al * lo_s[:, cs] + pv[t:]

        @pl.when(j < i)
        def _():
            step(False)

        @pl.when(j == i)
        def _():
            step(True)
            for hh in range(hg):
                cs = slice(hh * HEAD_DIM, (hh + 1) * HEAD_DIM)
                o_ref[:, cs] = acc_s[:, cs] / l_s[hh]
                olo_ref[:, cs] = lo_s[:, cs] / l_s[hh]
                lse_ref[:, cs] = m_s[hh] + jnp.log2(l_s[hh])

    qspec = pl.BlockSpec((t, w), lambda g, p, qi, kj: (qi[p], g))
    kspec = pl.BlockSpec((t, w), lambda g, p, qi, kj: (kj[p], g))
    crow = pl.BlockSpec((hg, 1, t), lambda g, p, qi, kj: (g, 0, kj[p]))
    grid_spec = pltpu.PrefetchScalarGridSpec(
        num_scalar_prefetch=2, grid=(ng, int(qi_tab.shape[0])), in_specs=[qspec, kspec, kspec, crow],
        out_specs=[qspec, qspec, qspec],
        scratch_shapes=[pltpu.VMEM((hg, t, LANES), F32), pltpu.VMEM((hg, t, LANES), F32), pltpu.VMEM((t, w), F32),
                        pltpu.VMEM((t, w), F32), pltpu.VMEM((t, t), F32), pltpu.VMEM((2 * t, t), BF16),
                        pltpu.VMEM((t, LANES), F32)])
    return pl.pallas_call(
        body, name=name, grid_spec=grid_spec,
        out_shape=[_sds((s_len, e), F32), _sds((s_len, e), F32), _sds((s_len, e), F32)],
        compiler_params=_params(12 * t * t * 4 + 32 * t * w * 4),
    )(qi_tab, kj_tab, qn, kn, vb, c_row)


def _fox_bwd(qn, kn, vb, do, lse, delta, c_row, *, name, hg, t=512):
    s_len, e = qn.shape
    t = _tile(s_len, t)
    w = hg * HEAD_DIM
    ng, n = e // w, s_len // t
    rc = _tile(t, ATTN_ROW_CHUNK)
    scale = HEAD_DIM ** -0.5
    c1 = scale * LOG2E
    qi_tab, kj_tab = _tri_tables(n, k_major=True)

    def body(qi_ref, kj_ref, q_ref, k_ref, v_ref, do_ref, lse_ref, dl_ref, cr_ref, dq_ref, dk_ref, dv_ref, dc_ref,
             dk_s, dv_s, dc_s, s_scr, dp_scr, p_scr, ds_scr):
        pid = pl.program_id(1)
        i, j = qi_ref[pid], kj_ref[pid]

        @pl.when(pid == 0)
        def _():
            dq_ref[...] = jnp.zeros_like(dq_ref)

        @pl.when(i == j)
        def _():
            dk_s[...] = jnp.zeros_like(dk_s)
            dv_s[...] = jnp.zeros_like(dv_s)
            dc_s[...] = jnp.zeros_like(dc_s)

        def step(diag):
            qrows = pl.ds(pl.multiple_of(i * t, t), t)
            for hh in range(hg):
                cs = slice(hh * HEAD_DIM, (hh + 1) * HEAD_DIM)
                s_scr[...] = lax.dot_general(q_ref[:, cs], k_ref[:, cs], (((1,), (1,)), ((), ())),
                                             preferred_element_type=F32)
                dp_scr[...] = lax.dot_general(do_ref[:, cs], v_ref[:, cs], (((1,), (1,)), ((), ())),
                                              preferred_element_type=F32)
                cr2 = cr_ref[hh] * LOG2E
                dcol = jnp.zeros((SUBLANES, t), F32)
                for r in range(t // rc):
                    rows = slice(r * rc, (r + 1) * rc)
                    s2 = _attn_logits2(s_scr[rows, :], cr2, diag, r * rc, c1)
                    p = jnp.exp2(s2 - jnp.tile(lse_ref[rows, cs], (1, t // LANES)))
                    ds = p * (dp_scr[rows, :] - jnp.tile(dl_ref[rows, cs], (1, t // LANES)))
                    dcol = dcol + _rowsum8(ds)
                    p_scr[rows, :] = p.astype(BF16)
                    ds_scr[rows, :] = (ds * scale).astype(BF16)
                dc_s[hh] -= jnp.sum(dcol, axis=0, keepdims=True)
                dv_s[:, cs] += lax.dot_general(p_scr[...], do_ref[:, cs], (((0,), (0,)), ((), ())),
                                               preferred_element_type=F32)
                dk_s[:, cs] += lax.dot_general(ds_scr[...], q_ref[:, cs], (((0,), (0,)), ((), ())),
                                               preferred_element_type=F32)
                dq_ref[qrows, cs] += jnp.dot(ds_scr[...], k_ref[:, cs], preferred_element_type=F32)

        @pl.when(i > j)
        def _():
            step(False)

        @pl.when(i == j)
        def _():
            step(True)

        @pl.when(i == n - 1)
        def _():
            dk_ref[...] = dk_s[...]
            dv_ref[...] = dv_s[...]
            dc_ref[...] = dc_s[...]

    qspec = pl.BlockSpec((t, w), lambda g, p, qi, kj: (qi[p], g))
    kspec = pl.BlockSpec((t, w), lambda g, p, qi, kj: (kj[p], g))
    crow = pl.BlockSpec((hg, 1, t), lambda g, p, qi, kj: (g, 0, kj[p]))
    grid_spec = pltpu.PrefetchScalarGridSpec(
        num_scalar_prefetch=2, grid=(ng, int(qi_tab.shape[0])),
        in_specs=[qspec, kspec, kspec, qspec, qspec, qspec, crow],
        out_specs=[pl.BlockSpec((s_len, w), lambda g, p, qi, kj: (0, g)), kspec, kspec, crow],
        scratch_shapes=[pltpu.VMEM((t, w), F32), pltpu.VMEM((t, w), F32), pltpu.VMEM((hg, 1, t), F32),
                        pltpu.VMEM((t, t), F32), pltpu.VMEM((t, t), F32), pltpu.VMEM((t, t), BF16),
                        pltpu.VMEM((t, t), BF16)])
    return pl.pallas_call(
        body, name=name, grid_spec=grid_spec,
        out_shape=[_sds((s_len, e), F32), _sds((s_len, e), F32), _sds((s_len, e), F32), _sds((e // HEAD_DIM, 1, s_len), F32)],
        compiler_params=_params(2 * s_len * w * 4 + 16 * t * t * 4 + 24 * t * w * 4),
    )(qi_tab, kj_tab, qn, kn, vb, do, lse, delta, c_row)


def _adamw(w, gs, m, v, *, name, tr=256):
    r, c = w.shape
    tr = _tile(r, tr)
    n_g = len(gs)

    def body(*refs):
        w_ref, g_refs = refs[0], refs[1:1 + n_g]
        m_ref, v_ref, go_ref, d_ref, nm_ref, nv_ref = refs[1 + n_g:]
        gv = g_refs[0][...].astype(F32)
        for g_ref in g_refs[1:]:
            gv = gv + g_ref[...].astype(F32)
        go_ref[...] = gv
        m2 = ADAM_B1 * m_ref[...] + (1.0 - ADAM_B1) * gv
        v2 = ADAM_B2 * v_ref[...] + (1.0 - ADAM_B2) * (gv * gv)
        m_hat = m2 / (1.0 - ADAM_B1 ** ADAM_STEP)
        v_hat = v2 / (1.0 - ADAM_B2 ** ADAM_STEP)
        d_ref[...] = -ADAM_LR * (m_hat / (jnp.sqrt(v_hat) + ADAM_EPS) + ADAM_WD * w_ref[...])
        nm_ref[...] = m2
        nv_ref[...] = v2

    blk = pl.BlockSpec((tr, c), lambda i: (i, 0))
    return pl.pallas_call(
        body, name=name, grid=(r // tr,), in_specs=[blk] * (3 + n_g), out_specs=[blk] * 4,
        out_shape=[_sds((r, c), F32)] * 4, compiler_params=_params(24 * tr * c * 4),
    )(w, *gs, m, v)


def _adamw_halves(w, mine, other, m, v, c_arr, *, name, tr=256):
    r, c = w.shape
    half = r // 2
    tr = _tile(half, tr)
    nh = half // tr

    def body(c_ref, w_ref, mine_ref, other_ref, m_ref, v_ref, go_ref, d_ref, nm_ref, nv_ref):
        is_mine = pl.program_id(0) == c_ref[0]
        gv = jnp.where(is_mine, mine_ref[...], other_ref[...])
        go_ref[...] = gv
        m2 = ADAM_B1 * m_ref[...] + (1.0 - ADAM_B1) * gv
        v2 = ADAM_B2 * v_ref[...] + (1.0 - ADAM_B2) * (gv * gv)
        m_hat = m2 / (1.0 - ADAM_B1 ** ADAM_STEP)
        v_hat = v2 / (1.0 - ADAM_B2 ** ADAM_STEP)
        d_ref[...] = -ADAM_LR * (m_hat / (jnp.sqrt(v_hat) + ADAM_EPS) + ADAM_WD * w_ref[...])
        nm_ref[...] = m2
        nv_ref[...] = v2

    full = pl.BlockSpec((tr, c), lambda h, j, cref: (h * nh + j, 0))
    part = pl.BlockSpec((tr, c), lambda h, j, cref: (j, 0))
    grid_spec = pltpu.PrefetchScalarGridSpec(num_scalar_prefetch=1, grid=(2, nh), in_specs=[full, part, part, full, full],
                                             out_specs=[full] * 4)
    return pl.pallas_call(
        body, name=name, grid_spec=grid_spec, out_shape=[_sds((r, c), F32)] * 4,
        compiler_params=_params(28 * tr * c * 4),
    )(c_arr, w, mine, other, m, v)


def _pair_add(g, got, c_arr, *, name, tr=256):
    n, r, c = g.shape
    half = r // 2
    tr = _tile(half, tr)
    nh = half // tr

    def body(c_ref, g_ref, got_ref, o_ref):
        o_ref[...] = (g_ref[...].astype(F32) + got_ref[...].astype(F32)).astype(o_ref.dtype)

    grid_spec = pltpu.PrefetchScalarGridSpec(
        num_scalar_prefetch=1, grid=(n, nh),
        in_specs=[pl.BlockSpec((None, tr, c), lambda s, i, cref: (s, cref[0] * nh + i, 0)),
                  pl.BlockSpec((None, tr, c), lambda s, i, cref: (s, i, 0))],
        out_specs=pl.BlockSpec((None, tr, c), lambda s, i, cref: (s, i, 0)))
    return pl.pallas_call(
        body, name=name, grid_spec=grid_spec, out_shape=_sds((n, half, c), BF16), compiler_params=_params(16 * tr * c * 4),
    )(c_arr, g, got)


def _sum_own_recv(pair, recv, chip_arr, *, name, tr=256):
    _, r, c = pair.shape
    tr = _tile(r, tr)
    n_recv = recv.shape[0]

    def body(chip_ref, own_ref, recv_ref, o_ref):
        acc = own_ref[...].astype(F32)
        for k in range(n_recv):
            acc = acc + recv_ref[k].astype(F32)
        o_ref[...] = acc

    grid_spec = pltpu.PrefetchScalarGridSpec(
        num_scalar_prefetch=1, grid=(r // tr,),
        in_specs=[pl.BlockSpec((None, tr, c), lambda i, chip: (chip[0], i, 0)),
                  pl.BlockSpec((n_recv, tr, c), lambda i, chip: (0, i, 0))],
        out_specs=pl.BlockSpec((tr, c), lambda i, chip: (i, 0)))
    return pl.pallas_call(
        body, name=name, grid_spec=grid_spec, out_shape=_sds((r, c), F32), compiler_params=_params(16 * tr * c * 4),
    )(chip_arr, pair, recv)


_ANY = pl.BlockSpec(memory_space=pl.ANY)
DMA_CHUNK_BYTES = 512 << 10


def _chunks(parts):
    out = []
    for src_at, dst_at, rows, row_bytes in parts:
        step = max(16, DMA_CHUNK_BYTES // row_bytes // 16 * 16)
        for r0 in range(0, rows, step):
            n = min(step, rows - r0)
            out.append((src_at(r0, n), dst_at(r0, n)))
    return out


def _row_bytes(ref):
    return ref.shape[-1] * ref.dtype.itemsize


def _me():
    return lax.axis_index("x"), lax.axis_index("y"), lax.axis_index("c")


def _chip_peers(x, y):
    return [(2 * (1 - x) + y, (1 - x, y)), (2 * x + (1 - y), (x, 1 - y)), (2 * (1 - x) + (1 - y), (1 - x, 1 - y))]


def _exchange(name, ins, out_shapes, plan):
    n_in, n_out = len(ins), len(out_shapes)

    def body(*refs):
        in_refs, out_refs = refs[:n_in], refs[n_in:n_in + n_out]
        send_sems, recv_sems, loc_sems = refs[n_in + n_out:]
        remote, local = plan(in_refs, out_refs)
        starts, waits = [], []
        for k, (ws, wd, dev, parts) in enumerate(remote):
            def mk(s, d, k=k, dev=dev):
                return pltpu.make_async_remote_copy(src_ref=s, dst_ref=d, send_sem=send_sems.at[k],
                                                    recv_sem=recv_sems.at[k], device_id=dev, device_id_type=MESH_ID)
            starts += [mk(s, d) for s, d in _chunks(parts)]
            waits.append(mk(ws, wd))
        for k, (ws, wd, _, parts) in enumerate(local):
            def mk(s, d, k=k):
                return pltpu.make_async_copy(s, d, loc_sems.at[k])
            starts += [mk(s, d) for s, d in _chunks(parts)]
            waits.append(mk(ws, wd))
        for cp in starts:
            cp.start()
        for cp in waits:
            cp.wait()

    n_remote, n_local = plan.n_remote, plan.n_local
    return pl.pallas_call(
        body, name=name, in_specs=[_ANY] * n_in, out_specs=[_ANY] * n_out, out_shape=list(out_shapes),
        scratch_shapes=[pltpu.SemaphoreType.DMA((n_remote,)), pltpu.SemaphoreType.DMA((n_remote,)),
                        pltpu.SemaphoreType.DMA((max(n_local, 1),))],
    )(*ins)


def _gather_weights(shards, *, name):
    n_t = len(shards)

    def body(*refs):
        in_refs, out_refs = refs[:n_t], refs[n_t:2 * n_t]
        send_sems, recv_sems = refs[2 * n_t:]
        x, y, c = _me()
        chip = 2 * x + y
        peers = _chip_peers(x, y)
        local, first, passed = [], [], []
        for t in range(n_t):
            src, dst = in_refs[t], out_refs[t]
            n_rows = src.shape[0]
            half = n_rows // 2
            rb = _row_bytes(src)
            rows = pl.ds(c * half, half)

            def mk_own(s, d, t=t):
                return pltpu.make_async_remote_copy(
                    src_ref=s, dst_ref=d, send_sem=send_sems.at[7 * t + 6], recv_sem=recv_sems.at[7 * t + 6],
                    device_id=(x, y, 1 - c), device_id_type=MESH_ID)

            own = [(lambda r0, n, src=src: src.at[pl.ds(r0, n)],
                    lambda r0, n, dst=dst: dst.at[chip, pl.ds(r0, n)], n_rows, rb)]
            local.append((mk_own(src, dst.at[chip]), [mk_own(s, d) for s, d in _chunks(own)]))
            for k, (pchip, (px, py)) in enumerate(peers):
                def mk_ici(s, d, t=t, k=k, px=px, py=py):
                    return pltpu.make_async_remote_copy(
                        src_ref=s, dst_ref=d, send_sem=send_sems.at[7 * t + k], recv_sem=recv_sems.at[7 * t + k],
                        device_id=(px, py, c), device_id_type=MESH_ID)

                def mk_d2d(s, d, t=t, k=k):
                    return pltpu.make_async_remote_copy(
                        src_ref=s, dst_ref=d, send_sem=send_sems.at[7 * t + 3 + k], recv_sem=recv_sems.at[7 * t + 3 + k],
                        device_id=(x, y, 1 - c), device_id_type=MESH_ID)

                out_part = [(lambda r0, n, src=src: src.at[pl.ds(c * half + r0, n)],
                             lambda r0, n, dst=dst: dst.at[chip, pl.ds(c * half + r0, n)], half, rb)]
                fwd_part = [(lambda r0, n, dst=dst, pchip=pchip: dst.at[pchip, pl.ds(c * half + r0, n)],
                             lambda r0, n, dst=dst, pchip=pchip: dst.at[pchip, pl.ds(c * half + r0, n)], half, rb)]
                first.append((mk_ici(src.at[rows], dst.at[chip, rows]), [mk_ici(s, d) for s, d in _chunks(out_part)]))
                passed.append((mk_d2d(dst.at[pchip, rows], dst.at[pchip, rows]),
                               [mk_d2d(s, d) for s, d in _chunks(fwd_part)]))
        for _, chunk_copies in first + local:
            for cp in chunk_copies:
                cp.start()
        for (whole, _), (_, fwd_copies) in zip(first, passed):
            whole.wait_recv()
            for cp in fwd_copies:
                cp.start()
        for whole, _ in passed:
            whole.wait_recv()
        for whole, _ in first + passed:
            whole.wait_send()
        for whole, _ in local:
            whole.wait()

    outs = [_sds((N_CHIPS,) + s.shape, s.dtype) for s in shards]
    return pl.pallas_call(
        body, name=name, in_specs=[_ANY] * n_t, out_specs=[_ANY] * n_t, out_shape=outs,
        scratch_shapes=[pltpu.SemaphoreType.DMA((7 * n_t,)), pltpu.SemaphoreType.DMA((7 * n_t,))],
    )(*shards)


class _Plan:
    def __init__(self, fn, n_remote, n_local):
        self.fn, self.n_remote, self.n_local = fn, n_remote, n_local

    def __call__(self, in_refs, out_refs):
        return self.fn(in_refs, out_refs)


def _reduce_grads(grads, *, name):
    n_t = len(grads)
    c_arr = lax.axis_index("c").astype(jnp.int32).reshape(1)
    chip_arr = (2 * lax.axis_index("x") + lax.axis_index("y")).astype(jnp.int32).reshape(1)

    def plan1(in_refs, out_refs):
        x, y, c = _me()
        remote = []
        for t in range(n_t):
            src, got = in_refs[t], out_refs[t]
            half = src.shape[1] // 2
            send = [(lambda r0, n, s=s, src=src, half=half: src.at[s, pl.ds((1 - c) * half + r0, n)],
                     lambda r0, n, s=s, got=got: got.at[s, pl.ds(r0, n)], half, _row_bytes(src)) for s in range(N_CHIPS)]
            remote.append((src.at[:, pl.ds((1 - c) * half, half)], got, (x, y, 1 - c), send))
        return remote, []

    halves = [_sds((N_CHIPS, g.shape[1] // 2, g.shape[2]), g.dtype) for g in grads]
    got = _exchange(name + "_sib", grads, halves, _Plan(plan1, n_t, 0))
    pair = [_pair_add(grads[t], got[t], c_arr, name=f"{name}_pair{t}") for t in range(n_t)]

    def plan2(in_refs, out_refs):
        x, y, c = _me()
        remote = []
        for t in range(n_t):
            src, dst = in_refs[t], out_refs[t]
            rows, rb = src.shape[1], _row_bytes(src)
            for k, (pchip, (px, py)) in enumerate(_chip_peers(x, y)):
                part = [(lambda r0, n, src=src, pchip=pchip: src.at[pchip, pl.ds(r0, n)],
                         lambda r0, n, dst=dst, k=k: dst.at[k, pl.ds(r0, n)], rows, rb)]
                remote.append((src.at[pchip], dst.at[k], (px, py, c), part))
        return remote, []

    recv_shapes = [_sds((N_CHIPS - 1,) + h.shape[1:], h.dtype) for h in halves]
    recv = _exchange(name + "_ici", pair, recv_shapes, _Plan(plan2, 3 * n_t, 0))
    mine = [_sum_own_recv(pair[t], recv[t], chip_arr, name=f"{name}_sum{t}") for t in range(n_t)]

    def plan3(in_refs, out_refs):
        x, y, c = _me()
        remote = []
        for t in range(n_t):
            src, dst = in_refs[t], out_refs[t]
            rows = [(lambda r0, n, src=src: src.at[pl.ds(r0, n)], lambda r0, n, dst=dst: dst.at[pl.ds(r0, n)],
                     src.shape[0], _row_bytes(src))]
            remote.append((src, dst, (x, y, 1 - c), rows))
        return remote, []

    other = _exchange(name + "_swap", mine, [_sds(s.shape, s.dtype) for s in mine], _Plan(plan3, n_t, 0))
    return list(zip(mine, other)), c_arr


def _allreduce_small(pack, *, name):
    r, w = pack.shape

    def body(p_ref, o_ref, buf, send_sems, recv_sems):
        x, y, c = _me()
        me = 4 * x + 2 * y + c
        buf[me] = p_ref[...]
        copies = []
        for k in range(1, N_DEV):
            peer = (x ^ ((k >> 2) & 1), y ^ ((k >> 1) & 1), c ^ (k & 1))
            copies.append(pltpu.make_async_remote_copy(
                src_ref=p_ref, dst_ref=buf.at[me], send_sem=send_sems.at[k - 1], recv_sem=recv_sems.at[k - 1],
                device_id=peer, device_id_type=MESH_ID))
        for cp in copies:
            cp.start()
        for cp in copies:
            cp.wait()
        acc = buf[0]
        for k in range(1, N_DEV):
            acc = acc + buf[k]
        o_ref[...] = acc

    vm = pl.BlockSpec(memory_space=pltpu.VMEM)
    return pl.pallas_call(
        body, name=name, in_specs=[vm], out_specs=vm, out_shape=_sds((r, w), F32),
        scratch_shapes=[pltpu.VMEM((N_DEV, r, w), F32), pltpu.SemaphoreType.DMA((N_DEV - 1,)),
                        pltpu.SemaphoreType.DMA((N_DEV - 1,))],
        compiler_params=_params(12 * r * w * 4),
    )(pack)


def _pack(arrs, row_multiple=16):
    flat = jnp.concatenate([a.reshape(-1).astype(F32) for a in arrs])
    unit = row_multiple * LANES
    total = -(-flat.shape[0] // unit) * unit
    return jnp.pad(flat, (0, total - flat.shape[0])).reshape(total // LANES, LANES)


def _unpack(packed, shapes):
    flat = packed.reshape(-1)
    out, off = [], 0
    for shp in shapes:
        n = int(np.prod(shp))
        out.append(flat[off:off + n].reshape(shp))
        off += n
    return out


def _pad_cols(a, width):
    return jnp.pad(a, [(0, 0)] * (a.ndim - 1) + [(0, width - a.shape[-1])])


ATTN_HEADS_PER_STEP = 4
ATTN_FWD_HEADS_PER_STEP = 4
ATTN_FWD_TILE = 1024
SMALL_SHARDED = ("a_norm", "a_conv_w", "a_conv_b", "a_ln_g", "a_ln_b", "c_norm", "c_conv_w")
SMALL_REPLICATED = ("b_norm", "b_f_bias", "b_q_norm", "b_k_norm")
BIG = ("a_w_in", "a_w_out", "b_w_in", "b_w_out", "c_w_in", "c_w_out")
WEIGHTS = ("a_norm", "a_w_in", "a_conv_w", "a_conv_b", "a_ln_g", "a_ln_b", "a_w_out", "b_norm", "b_w_in", "b_f_bias",
           "b_q_norm", "b_k_norm", "b_w_out", "c_norm", "c_w_in", "c_conv_w", "c_w_out")


def _mixer_a_fwd(x, p, l, tag):
    h = _rms_fwd(x, p["a_norm"][l][None], name=f"{tag}_rms")
    proj = _mm_nn(h, p["a_w_in"][l], name=f"{tag}_in", tn=p["a_w_in"][l].shape[2])
    u1 = _a_conv_fwd(proj, p["a_conv_w"][l], p["a_conv_b"][l][None], name=f"{tag}_conv")
    gated = _a_post_fwd(u1, proj, p["a_ln_g"][l][None], p["a_ln_b"][l][None], name=f"{tag}_post")
    y = _mm_nn(gated, p["a_w_out"][l], name=f"{tag}_out", tn=1024, add=x)
    return y, (x, h, proj, u1, gated)


def _mixer_a_bwd(dx, saved, p, l, tag):
    x, h, proj, u1, gated = saved
    g = {}
    g["a_w_out"] = _mm_tn(gated, dx, name=f"{tag}_dwout", out_dtype=BF16, tk=2048, tn=1024, ts=1024)
    dgated = _mm_nt(dx, p["a_w_out"][l], name=f"{tag}_dgated", tn=2048, tk=1024)
    du1, dz, g["a_ln_g"], g["a_ln_b"], g["a_conv_b"] = _a_post_bwd(
        dgated, u1, proj, p["a_ln_g"][l][None], p["a_ln_b"][l][None], name=f"{tag}_dpost")
    dproj, g["a_conv_w"] = _a_conv_bwd(du1, proj, dz, p["a_conv_w"][l], name=f"{tag}_dconv")
    g["a_w_in"] = _mm_tn(h, dproj, name=f"{tag}_dwin", out_dtype=BF16, out_width=p["a_w_in"][l].shape[2], ts=2048)
    dh = _mm_nt(dproj, p["a_w_in"][l], name=f"{tag}_dh", ksub=4)
    dx, g["a_norm"] = _rms_bwd(x, p["a_norm"][l][None], [dh], dx, name=f"{tag}_drms")
    return dx, g


def _mixer_c_fwd(x, p, tag):
    h = _rms_fwd(x, p["c_norm"][0][None], name=f"{tag}_rms")
    proj = _mm_nn(h, p["c_w_in"], name=f"{tag}_in", tn=1024)
    gated = _c_mid_fwd(proj, p["c_conv_w"][0], name=f"{tag}_mid")
    y = _mm_nn(gated, p["c_w_out"], name=f"{tag}_out", tn=1024, add=x)
    return y, (x, h, proj, gated)


def _mixer_c_bwd(dx, saved, p, tag):
    x, h, proj, gated = saved
    g = {}
    g["c_w_out"] = _mm_tn(gated, dx, name=f"{tag}_dwout", out_dtype=BF16, tk=2048, tn=1024, ts=1024)
    dgated = _mm_nt(dx, p["c_w_out"], name=f"{tag}_dgated", tn=2048, tk=1024)
    dproj, g["c_conv_w"] = _c_mid_bwd(dgated, proj, p["c_conv_w"][0], name=f"{tag}_dmid")
    g["c_w_in"] = _mm_tn(h, dproj, name=f"{tag}_dwin", out_dtype=BF16, out_width=p["c_w_in"].shape[2], tn=1024, ts=2048)
    dh = _mm_nt(dproj, p["c_w_in"], name=f"{tag}_dh", tk=1024, ksub=2)
    dx, g["c_norm"] = _rms_bwd(x, p["c_norm"][0][None], [dh], dx, name=f"{tag}_drms")
    return dx, g


def _mixer_b_fwd(x, p, tag):
    s_len = x.shape[0]
    n_heads = p["b_f_bias"].shape[1]
    hg = min(ATTN_FWD_HEADS_PER_STEP, n_heads)
    h = _rms_fwd(x, p["b_norm"], name=f"{tag}_rms")
    proj = _mm_nn(h, p["b_wq"], name=f"{tag}_in", tn=1024)
    fl = _mm_nn(h, p["b_wf"], name=f"{tag}_inf", tn=LANES)
    qn, kn, vb = _b_qk_fwd(proj, p["b_q_norm"], p["b_k_norm"], name=f"{tag}_qk")
    bias = _pad_cols(p["b_f_bias"], LANES)
    c = _b_cumsum(fl, bias, name=f"{tag}_cumsum")
    ch = c[:, :n_heads]
    c_row = ch.T.reshape(n_heads, 1, s_len)
    o, o_lo, lse = _fox_fwd(qn, kn, vb, c_row, name=f"{tag}_attn", hg=hg, t=ATTN_FWD_TILE)
    gated = _b_gate_fwd(o, proj, name=f"{tag}_gate")
    y = _mm_nn(gated, p["b_w_out"], name=f"{tag}_out", tn=1024, add=x)
    return y, (x, h, proj, fl, bias, qn, kn, vb, c_row, o, o_lo, lse, gated)


def _mixer_b_bwd(dx, saved, p, tag):
    hg = ATTN_HEADS_PER_STEP
    x, h, proj, fl, bias, qn, kn, vb, c_row, o, o_lo, lse, gated = saved
    s_len = x.shape[0]
    n_heads = p["b_f_bias"].shape[1]
    g = {}
    g["b_w_out"] = _mm_tn(gated, dx, name=f"{tag}_dwout", out_dtype=BF16, tk=2048, tn=1024, ts=1024)
    dgated = _mm_nt(dx, p["b_w_out"], name=f"{tag}_dgated", tn=2048, tk=1024)
    do, dz, delta = _b_gate_bwd(dgated, o, o_lo, proj, name=f"{tag}_dgate", hg=hg)
    dqn, dkn, dv, dc = _fox_bwd(qn, kn, vb, do, lse, delta, c_row, name=f"{tag}_dattn", hg=hg)
    dc_pad = _pad_cols(dc.reshape(n_heads, s_len).T, LANES)
    dfl, dbias = _b_cumsum_bwd(dc_pad, fl, bias, name=f"{tag}_dcumsum")
    g["b_f_bias"] = dbias[:, :n_heads]
    dproj, g["b_q_norm"], g["b_k_norm"] = _b_qk_bwd(dqn, dkn, dv, dz, proj, p["b_q_norm"], p["b_k_norm"], name=f"{tag}_dqk")
    dwq = _mm_tn(h, dproj, name=f"{tag}_dwin", out_dtype=BF16, tn=1024, ts=2048)
    dwf = _mm_tn(h, dfl, name=f"{tag}_dwinf", out_dtype=BF16, tn=LANES)
    g["b_w_in"] = jnp.concatenate([dwq, dwf[:, :n_heads]], axis=1)
    dh = _mm_nt(dproj, p["b_wq"], name=f"{tag}_dh", tk=1024, ksub=2)
    dhf = _mm_nt(dfl, p["b_wf"], name=f"{tag}_dhf", tk=LANES)
    dx, g["b_norm"] = _rms_bwd(x, p["b_norm"], [dh, dhf], dx, name=f"{tag}_drms")
    return dx, g


def kernel(x, a_norm, a_w_in, a_conv_w, a_conv_b, a_ln_g, a_ln_b, a_w_out, b_norm, b_w_in, b_f_bias, b_q_norm, b_k_norm, b_w_out, c_norm, c_w_in, c_conv_w, c_w_out, loss_target, m_a_norm, m_a_w_in, m_a_conv_w, m_a_conv_b, m_a_ln_g, m_a_ln_b, m_a_w_out, m_b_norm, m_b_w_in, m_b_f_bias, m_b_q_norm, m_b_k_norm, m_b_w_out, m_c_norm, m_c_w_in, m_c_conv_w, m_c_w_out, v_a_norm, v_a_w_in, v_a_conv_w, v_a_conv_b, v_a_ln_g, v_a_ln_b, v_a_w_out, v_b_norm, v_b_w_in, v_b_f_bias, v_b_q_norm, v_b_k_norm, v_b_w_out, v_c_norm, v_c_w_in, v_c_conv_w, v_c_w_out):
    w_loc = dict(a_norm=a_norm, a_w_in=a_w_in, a_conv_w=a_conv_w, a_conv_b=a_conv_b, a_ln_g=a_ln_g, a_ln_b=a_ln_b,
                 a_w_out=a_w_out, b_norm=b_norm, b_w_in=b_w_in, b_f_bias=b_f_bias, b_q_norm=b_q_norm, b_k_norm=b_k_norm,
                 b_w_out=b_w_out, c_norm=c_norm, c_w_in=c_w_in, c_conv_w=c_conv_w, c_w_out=c_w_out)
    m_loc = dict(a_norm=m_a_norm, a_w_in=m_a_w_in, a_conv_w=m_a_conv_w, a_conv_b=m_a_conv_b, a_ln_g=m_a_ln_g,
                 a_ln_b=m_a_ln_b, a_w_out=m_a_w_out, b_norm=m_b_norm, b_w_in=m_b_w_in, b_f_bias=m_b_f_bias,
                 b_q_norm=m_b_q_norm, b_k_norm=m_b_k_norm, b_w_out=m_b_w_out, c_norm=m_c_norm, c_w_in=m_c_w_in,
                 c_conv_w=m_c_conv_w, c_w_out=m_c_w_out)
    v_loc = dict(a_norm=v_a_norm, a_w_in=v_a_w_in, a_conv_w=v_a_conv_w, a_conv_b=v_a_conv_b, a_ln_g=v_a_ln_g,
                 a_ln_b=v_a_ln_b, a_w_out=v_a_w_out, b_norm=v_b_norm, b_w_in=v_b_w_in, b_f_bias=v_b_f_bias,
                 b_q_norm=v_b_q_norm, b_k_norm=v_b_k_norm, b_w_out=v_b_w_out, c_norm=v_c_norm, c_w_in=v_c_w_in,
                 c_conv_w=v_c_conv_w, c_w_out=v_c_w_out)
    n_a = a_w_in.shape[0]
    d_model = x.shape[2]
    e_inner = a_w_out.shape[1] * N_CHIPS
    n_heads = b_f_bias.shape[1]
    nb_loc = b_w_in.shape[2]
    nb_pad = -(-nb_loc // LANES) * LANES
    chip = 2 * lax.axis_index("x") + lax.axis_index("y")

    big_shards = ([a_w_in[l].astype(BF16) for l in range(n_a)] + [a_w_out[l].astype(BF16) for l in range(n_a)]
                  + [_pad_cols(b_w_in[0], nb_pad).astype(BF16), b_w_out[0].astype(BF16), c_w_in[0].astype(BF16),
                     c_w_out[0].astype(BF16)])
    small_pack = _pack([w_loc[n] for n in SMALL_SHARDED])
    gathered = _gather_weights(big_shards + [small_pack], name="gather_weights")
    p = {}
    p["a_w_in"] = gathered[0:n_a]
    p["a_w_out"] = [g.reshape(e_inner, d_model) for g in gathered[n_a:2 * n_a]]
    gb, gbo, gci, gco, gsmall = gathered[2 * n_a:]
    wb_full = jnp.concatenate([gb[k, :, :nb_loc] for k in range(N_CHIPS)], axis=1)
    p["b_wq"] = wb_full[:, :4 * e_inner]
    p["b_wf"] = _pad_cols(wb_full[:, 4 * e_inner:], LANES)
    p["b_w_out"] = gbo.reshape(e_inner, d_model)
    p["c_w_in"] = gci
    p["c_w_out"] = gco.reshape(e_inner, d_model)
    small_shapes = [w_loc[n].shape for n in SMALL_SHARDED]
    per_chip = [_unpack(gsmall[k], small_shapes) for k in range(N_CHIPS)]
    for idx, n in enumerate(SMALL_SHARDED):
        p[n] = jnp.concatenate([per_chip[k][idx] for k in range(N_CHIPS)], axis=-1)
    for n in SMALL_REPLICATED:
        p[n] = w_loc[n]

    x0 = x[0]
    x1, sv0 = _mixer_a_fwd(x0, p, 0, "a0")
    x2, sv1 = _mixer_b_fwd(x1, p, "b0")
    x3, sv2 = _mixer_c_fwd(x2, p, "c0")
    x4, sv3 = _mixer_a_fwd(x3, p, 1, "a1")
    dy, loss_part = _loss_head(x4, loss_target[0], name="loss_head")
    loss = lax.psum(loss_part[0, 0], ("x", "y", "c"))

    dx, g3 = _mixer_a_bwd(dy, sv3, p, 1, "a1")
    dx, g2 = _mixer_c_bwd(dx, sv2, p, "c0")
    dx, g1 = _mixer_b_bwd(dx, sv1, p, "b0")
    dx, g0 = _mixer_a_bwd(dx, sv0, p, 0, "a0")
    grad_x = dx[None]

    half_rows = e_inner // N_CHIPS
    gb_full = g1["b_w_in"].reshape(d_model, N_CHIPS, nb_loc).transpose(1, 0, 2)
    big_grads = ([g0["a_w_in"], g3["a_w_in"]]
                 + [g0["a_w_out"].reshape(N_CHIPS, half_rows, d_model), g3["a_w_out"].reshape(N_CHIPS, half_rows, d_model)]
                 + [_pad_cols(gb_full, nb_pad), g1["b_w_out"].reshape(N_CHIPS, half_rows, d_model), g2["c_w_in"],
                    g2["c_w_out"].reshape(N_CHIPS, half_rows, d_model)])
    red, c_arr = _reduce_grads(big_grads, name="reduce_grads")
    big_order = [("a_w_in", 0), ("a_w_in", 1), ("a_w_out", 0), ("a_w_out", 1), ("b_w_in", 0), ("b_w_out", 0),
                 ("c_w_in", 0), ("c_w_out", 0)]

    grads, delta, new_m, new_v = {}, {}, {}, {}
    per_layer = {n: [] for n in BIG}
    for (n, l), (mine, other) in zip(big_order, red):
        cols = mine.shape[1]
        true_cols = w_loc[n].shape[-1]
        outs = _adamw_halves(_pad_cols(w_loc[n][l], cols), mine, other, _pad_cols(m_loc[n][l], cols),
                             _pad_cols(v_loc[n][l], cols), c_arr, name=f"adamw_{n}{l}")
        per_layer[n].append([a[:, :true_cols] for a in outs])
    for n in BIG:
        grads[n], delta[n], new_m[n], new_v[n] = [jnp.stack([layer[k] for layer in per_layer[n]]) for k in range(4)]

    small_full = {}
    for n in ("a_norm", "a_conv_w", "a_conv_b", "a_ln_g", "a_ln_b"):
        small_full[n] = jnp.stack([g0[n].reshape(p[n].shape[1:]), g3[n].reshape(p[n].shape[1:])])
    small_full["c_norm"] = g2["c_norm"].reshape(p["c_norm"].shape)
    small_full["c_conv_w"] = g2["c_conv_w"].reshape(p["c_conv_w"].shape)
    for n in SMALL_REPLICATED:
        small_full[n] = g1[n].reshape(w_loc[n].shape)
    small_names = SMALL_SHARDED + SMALL_REPLICATED
    summed = _unpack(_allreduce_small(_pack([small_full[n] for n in small_names], 8), name="reduce_small"),
                     [small_full[n].shape for n in small_names])
    for n, s in zip(small_names, summed):
        if n in SMALL_SHARDED:
            width = w_loc[n].shape[-1]
            grads[n] = lax.dynamic_slice_in_dim(s, chip * width, width, axis=s.ndim - 1)
        else:
            grads[n] = s

    small_shapes_all = [w_loc[n].shape for n in small_names]
    _, d_, m_, v_ = _adamw(_pack([w_loc[n] for n in small_names], 8), [_pack([grads[n] for n in small_names], 8)],
                           _pack([m_loc[n] for n in small_names], 8), _pack([v_loc[n] for n in small_names], 8),
                           name="adamw_small")
    for n, a, b, c_ in zip(small_names, _unpack(d_, small_shapes_all), _unpack(m_, small_shapes_all),
                           _unpack(v_, small_shapes_all)):
        delta[n], new_m[n], new_v[n] = a, b, c_

    return (loss, grad_x, *[grads[n] for n in WEIGHTS], *[delta[n] for n in WEIGHTS],
            *[new_m[n] for n in WEIGHTS], *[new_v[n] for n in WEIGHTS])
```

```python
import functools

import numpy as np
import jax
import jax.numpy as jnp
from jax import lax
from jax.experimental import pallas as pl
from jax.experimental.pallas import tpu as pltpu

F32 = jnp.float32
BF16 = jnp.bfloat16
NORM_EPS = 1e-6
HEAD_DIM = 128
LANES = 128
N_CHIPS = 4
N_DEV = 8
VMEM_CAP = 56 << 20
MESH_ID = pl.DeviceIdType.MESH

ADAM_LR = 0.001
ADAM_B1 = 0.9
ADAM_B2 = 0.999
ADAM_EPS = 1e-08
ADAM_WD = 0.01
ADAM_STEP = 10


def _sds(shape, dtype):
    return jax.ShapeDtypeStruct(tuple(shape), dtype)


def _tile(n, pref):
    if n <= pref:
        return n
    for t in range(pref - pref % 8, 0, -8):
        if n % t == 0:
            return t
    raise ValueError(f"no tile for {n} under {pref}")


def _params(vmem_bytes):
    return pltpu.CompilerParams(vmem_limit_bytes=int(min(max(vmem_bytes, 16 << 20), VMEM_CAP)))


def _sigmoid(v):
    return 1.0 / (1.0 + jnp.exp(-v))


def _silu(v):
    return v * _sigmoid(v)


def _silu_pair(v):
    s = _sigmoid(v)
    return v * s, s * (1.0 + v * (1.0 - s))


def _rowsum8(v):
    r, c = v.shape
    return jnp.sum(v.reshape(r // 8, 8, c), axis=0)


def _col_spec(grouped, rows_block, tile, width, row_of, col_of):
    if grouped:
        per = width // tile
        return pl.BlockSpec((None, rows_block, tile), lambda *ids: (col_of(*ids) // per, row_of(*ids), col_of(*ids) % per))
    return pl.BlockSpec((rows_block, tile), lambda *ids: (row_of(*ids), col_of(*ids)))


def _mm_nn(a, b, *, name, tm=1024, tn=512, out_dtype=F32, add=None):
    m, k = a.shape
    b_grouped = b.ndim == 3
    n = b.shape[0] * b.shape[2] if b_grouped else b.shape[1]
    width = b.shape[2] if b_grouped else n
    tm, tn = _tile(m, tm), _tile(width, tn)

    def body(*refs):
        if add is None:
            a_ref, b_ref, o_ref = refs
        else:
            a_ref, b_ref, r_ref, o_ref = refs
        acc = jnp.dot(a_ref[...].astype(BF16), b_ref[...].astype(BF16), preferred_element_type=F32)
        if add is not None:
            acc = acc + r_ref[...]
        o_ref[...] = acc.astype(o_ref.dtype)

    in_specs = [pl.BlockSpec((tm, k), lambda i, j: (i, 0)),
                _col_spec(b_grouped, k, tn, width, lambda i, j: 0, lambda i, j: j)]
    args = [a, b]
    if add is not None:
        in_specs.append(pl.BlockSpec((tm, tn), lambda i, j: (i, j)))
        args.append(add)
    vmem = 2 * (tm * k * a.dtype.itemsize + k * tn * 2 + tm * tn * 4 * (2 if add is not None else 1)) + tm * tn * 8 + tm * k * 2
    return pl.pallas_call(
        body, name=name, grid=(m // tm, n // tn), in_specs=in_specs,
        out_specs=pl.BlockSpec((tm, tn), lambda i, j: (i, j)),
        out_shape=_sds((m, n), out_dtype), compiler_params=_params(vmem + (4 << 20)),
    )(*args)


def _mm_nt(a, b, *, name, tm=1024, tn=1024, tk=512, ksub=1):
    a_grouped, b_grouped = a.ndim == 3, b.ndim == 3
    m = a.shape[1] if a_grouped else a.shape[0]
    n = a.shape[0] * a.shape[2] if a_grouped else a.shape[1]
    kk = b.shape[1] if b_grouped else b.shape[0]
    wa = a.shape[2] if a_grouped else n
    wb = b.shape[2] if b_grouped else n
    tm, tn = _tile(m, tm), _tile(kk, tn)
    tk = _tile(int(np.gcd(wa, wb)), tk)
    ksub = min(ksub, n // tk)
    assert (n // tk) % ksub == 0, (n, tk, ksub)
    steps = n // (tk * ksub)

    def body(*refs):
        a_refs, b_refs, o_ref = refs[:ksub], refs[ksub:2 * ksub], refs[2 * ksub]
        part = None
        for a_ref, b_ref in zip(a_refs, b_refs):
            d = lax.dot_general(a_ref[...].astype(BF16), b_ref[...].astype(BF16), (((1,), (1,)), ((), ())),
                                preferred_element_type=F32)
            part = d if part is None else part + d

        @pl.when(pl.program_id(2) == 0)
        def _():
            o_ref[...] = part

        @pl.when(pl.program_id(2) > 0)
        def _():
            o_ref[...] += part

    def sub(u):
        return lambda i, j, s: s * ksub + u

    in_specs = ([_col_spec(a_grouped, tm, tk, wa, lambda i, j, s: i, sub(u)) for u in range(ksub)]
                + [_col_spec(b_grouped, tn, tk, wb, lambda i, j, s: j, sub(u)) for u in range(ksub)])
    vmem = (2 * ksub * (tm * tk * a.dtype.itemsize + tn * tk * b.dtype.itemsize) + 2 * tm * tn * 4 + 2 * tm * tn * 4
            + (tm + tn) * tk * 2)
    return pl.pallas_call(
        body, name=name, grid=(m // tm, kk // tn, steps), in_specs=in_specs,
        out_specs=pl.BlockSpec((tm, tn), lambda i, j, s: (i, j)),
        out_shape=_sds((m, kk), F32), compiler_params=_params(vmem + (4 << 20)),
    )(*([a] * ksub), *([b] * ksub))


def _mm_tn(a, b, *, name, out_width=None, out_dtype=F32, tk=1024, tn=512, ts=512):
    s_len, k = a.shape
    b_grouped = b.ndim == 3
    n = b.shape[0] * b.shape[2] if b_grouped else b.shape[1]
    wb = b.shape[2] if b_grouped else n
    wo = out_width if out_width is not None else n
    tk, ts = _tile(k, tk), _tile(s_len, ts)
    tn = _tile(int(np.gcd(wb, wo)), tn)
    last = s_len // ts - 1
    direct = out_dtype == F32

    def body(a_ref, b_ref, o_ref, *scratch):
        acc = o_ref if direct else scratch[0]
        part = lax.dot_general(a_ref[...].astype(BF16), b_ref[...].astype(BF16), (((0,), (0,)), ((), ())),
                               preferred_element_type=F32)

        @pl.when(pl.program_id(2) == 0)
        def _():
            acc[...] = part

        @pl.when(pl.program_id(2) > 0)
        def _():
            acc[...] += part

        if not direct:
            @pl.when(pl.program_id(2) == last)
            def _():
                o_ref[...] = acc[...].astype(o_ref.dtype)

    in_specs = [pl.BlockSpec((ts, tk), lambda i, j, s: (s, i)),
                _col_spec(b_grouped, ts, tn, wb, lambda i, j, s: s, lambda i, j, s: j)]
    out_grouped = out_width is not None
    out_spec = _col_spec(out_grouped, tk, tn, wo, lambda i, j, s: i, lambda i, j, s: j)
    out_shape = _sds((n // wo, k, wo), out_dtype) if out_grouped else _sds((k, n), out_dtype)
    vmem = 2 * (ts * tk * a.dtype.itemsize + ts * tn * b.dtype.itemsize + tk * tn * 4) + 2 * tk * tn * 4 + ts * (tk + tn) * 4
    return pl.pallas_call(
        body, name=name, grid=(k // tk, n // tn, s_len // ts), in_specs=in_specs, out_specs=out_spec,
        out_shape=out_shape, scratch_shapes=[] if direct else [pltpu.VMEM((tk, tn), F32)],
        compiler_params=_params(vmem + (4 << 20)),
    )(a, b)


def _rms_fwd(x, g, *, name, ts=512):
    s_len, d = x.shape
    ts = _tile(s_len, ts)

    def body(x_ref, g_ref, h_ref):
        xf = x_ref[...]
        r = lax.rsqrt(jnp.mean(xf * xf, axis=-1, keepdims=True) + NORM_EPS)
        h_ref[...] = ((xf * r) * g_ref[...]).astype(h_ref.dtype)

    return pl.pallas_call(
        body, name=name, grid=(s_len // ts,),
        in_specs=[pl.BlockSpec((ts, d), lambda i: (i, 0)), pl.BlockSpec((1, d), lambda i: (0, 0))],
        out_specs=pl.BlockSpec((ts, d), lambda i: (i, 0)), out_shape=_sds((s_len, d), BF16),
        compiler_params=_params(8 * ts * d * 4),
    )(x, g)


def _rms_bwd(x, g, dhs, dres, *, name, ts=512):
    s_len, d = x.shape
    ts = _tile(s_len, ts)
    n_dh = len(dhs)
    last = s_len // ts - 1

    def body(*refs):
        x_ref, g_ref = refs[0], refs[1]
        dh_refs = refs[2:2 + n_dh]
        dres_ref, dx_ref, dg_ref, acc = refs[2 + n_dh:]
        i = pl.program_id(0)

        @pl.when(i == 0)
        def _():
            acc[...] = jnp.zeros_like(acc)

        xf = x_ref[...]
        dy = dh_refs[0][...]
        for r_ in dh_refs[1:]:
            dy = dy + r_[...]
        r = lax.rsqrt(jnp.mean(xf * xf, axis=-1, keepdims=True) + NORM_EPS)
        gd = dy * g_ref[...]
        dot = jnp.mean(xf * gd, axis=-1, keepdims=True)
        dx_ref[...] = dres_ref[...] + r * gd - xf * (r * r * r * dot)
        acc[...] += _rowsum8(dy * (xf * r))

        @pl.when(i == last)
        def _():
            dg_ref[...] = jnp.sum(acc[...], axis=0, keepdims=True)

    row = pl.BlockSpec((ts, d), lambda i: (i, 0))
    vec = pl.BlockSpec((1, d), lambda i: (0, 0))
    return pl.pallas_call(
        body, name=name, grid=(s_len // ts,),
        in_specs=[row, vec] + [row] * n_dh + [row],
        out_specs=[row, vec], out_shape=[_sds((s_len, d), F32), _sds((1, d), F32)],
        scratch_shapes=[pltpu.VMEM((8, d), F32)],
        compiler_params=_params((2 * (3 + n_dh) + 6) * ts * d * 4),
    )(x, g, *dhs, dres)


def _loss_head(y, target, *, name, ts=512):
    s_len, d = y.shape
    ts = _tile(s_len, ts)
    last = s_len // ts - 1

    def body(y_ref, t_ref, dy_ref, loss_ref, acc):
        i = pl.program_id(0)

        @pl.when(i == 0)
        def _():
            acc[...] = jnp.zeros_like(acc)

        err = y_ref[...] - t_ref[...]
        dy_ref[...] = err / d
        acc[...] += _rowsum8(err * err)

        @pl.when(i == last)
        def _():
            loss_ref[...] = (0.5 * jnp.sum(acc[...]) / d).reshape(1, 1)

    row = pl.BlockSpec((ts, d), lambda i: (i, 0))
    return pl.pallas_call(
        body, name=name, grid=(s_len // ts,), in_specs=[row, row],
        out_specs=[row, pl.BlockSpec((1, 1), lambda i: (0, 0))],
        out_shape=[_sds((s_len, d), F32), _sds((1, 1), F32)],
        scratch_shapes=[pltpu.VMEM((8, d), F32)],
        compiler_params=_params(10 * ts * d * 4),
    )(y, target)


CONV_ROWS = 64
CONV_COLS = 256


SUBLANES = 8


def _fill_shifts(buf, sh_ref):
    n = sh_ref.shape[1]
    for s in range(1, SUBLANES):
        sh_ref[s - 1, :, :] = buf[s:s + n, :]


def _shifted_rows(buf, sh_ref, start, rw, cs):
    s = start % SUBLANES
    if sh_ref is None or s == 0:
        return buf[start:start + rw, cs]
    return sh_ref[s - 1, start - s:start - s + rw, cs]


def _conv_taps(buf, sh_ref, w_ref, k_width, base, ts, tc, init, emit, reverse=False):
    cw = min(tc, CONV_COLS)
    rw = min(ts, CONV_ROWS)
    for cb in range(tc // cw):
        cs = slice(cb * cw, (cb + 1) * cw)
        for rb in range(ts // rw):
            acc = init(slice(rb * rw, (rb + 1) * rw), cs, (rw, cw))
            for k in range(k_width):
                sh = (k_width - 1 - k) if reverse else k
                acc = acc + w_ref[k:k + 1, cs] * _shifted_rows(buf, sh_ref, base + rb * rw + sh, rw, cs)
            emit(slice(rb * rw, (rb + 1) * rw), cs, acc)


def _conv_wgrad(buf, sh_ref, d_ref_val, acc_ref, k_width, base, ts, tc):
    cw = min(tc, CONV_COLS)
    rw = min(ts, CONV_ROWS)
    for cb in range(tc // cw):
        cs = slice(cb * cw, (cb + 1) * cw)
        for rb in range(ts // rw):
            dv = d_ref_val[rb * rw:(rb + 1) * rw, cs]
            for k in range(k_width):
                prod = dv * _shifted_rows(buf, sh_ref, base + rb * rw + k, rw, cs)
                acc_ref[8 * k:8 * k + 8, cs] += _rowsum8(prod)


A_HALO = 32


def _a_conv_fwd(proj, conv_w, conv_b, *, name, ts=1024, tc=128):
    s_len, e3 = proj.shape
    e = e3 // 3
    k_width = conv_w.shape[0]
    ts, tc = _tile(s_len, ts), _tile(e, tc)
    nc = e // tc
    kp = 32

    def body(val, gate, valh, gateh, w_ref, b_ref, u1_ref, buf, sh):
        i = pl.program_id(1)
        u0h = valh[...] * _sigmoid(gateh[...])
        buf[0:A_HALO, :] = jnp.where(i > 0, u0h, 0.0)
        buf[A_HALO:A_HALO + ts, :] = val[...] * _sigmoid(gate[...])
        _fill_shifts(buf, sh)

        def init(rows, cs, shape):
            return jnp.broadcast_to(b_ref[:, cs], shape)

        def emit(rows, cs, acc):
            u1_ref[rows, cs] = acc

        _conv_taps(buf, sh, w_ref, k_width, A_HALO - (k_width - 1), ts, tc, init, emit)

    hb = ts // A_HALO
    in_specs = [
        pl.BlockSpec((ts, tc), lambda j, i: (i, j)),
        pl.BlockSpec((ts, tc), lambda j, i: (i, nc + j)),
        pl.BlockSpec((A_HALO, tc), lambda j, i: (jnp.maximum(i * hb - 1, 0), j)),
        pl.BlockSpec((A_HALO, tc), lambda j, i: (jnp.maximum(i * hb - 1, 0), nc + j)),
        pl.BlockSpec((kp, tc), lambda j, i: (0, j)),
        pl.BlockSpec((1, tc), lambda j, i: (0, j)),
    ]
    w_pad = jnp.zeros((kp, e), F32).at[:k_width].set(conv_w)
    return pl.pallas_call(
        body, name=name, grid=(nc, s_len // ts), in_specs=in_specs,
        out_specs=pl.BlockSpec((ts, tc), lambda j, i: (i, j)), out_shape=_sds((s_len, e), F32),
        scratch_shapes=[pltpu.VMEM((A_HALO + ts, tc), F32), pltpu.VMEM((SUBLANES - 1, A_HALO + ts - SUBLANES, tc), F32)],
        compiler_params=_params(24 * ts * tc * 4),
    )(proj, proj, proj, proj, w_pad, conv_b)


def _ln_rows(u1, g, b):
    mu = jnp.mean(u1, axis=-1, keepdims=True)
    xc = u1 - mu
    var = jnp.mean(xc * xc, axis=-1, keepdims=True)
    rstd = lax.rsqrt(var + NORM_EPS)
    xhat = xc * rstd
    return xhat, rstd, xhat * g + b


def _a_post_fwd(u1, proj, ln_g, ln_b, *, name, ts=256):
    s_len, e = u1.shape
    ts = _tile(s_len, ts)

    def body(u1_ref, z_ref, g_ref, b_ref, o_ref):
        _, _, u2 = _ln_rows(u1_ref[...], g_ref[...], b_ref[...])
        o_ref[...] = (_silu(u2) * _silu(z_ref[...])).astype(o_ref.dtype)

    row = pl.BlockSpec((ts, e), lambda i: (i, 0))
    vec = pl.BlockSpec((1, e), lambda i: (0, 0))
    return pl.pallas_call(
        body, name=name, grid=(s_len // ts,),
        in_specs=[row, pl.BlockSpec((ts, e), lambda i: (i, 2)), vec, vec],
        out_specs=row, out_shape=_sds((s_len, e), BF16), compiler_params=_params(12 * ts * e * 4),
    )(u1, proj, ln_g, ln_b)


def _a_post_bwd(dgated, u1, proj, ln_g, ln_b, *, name, ts=256):
    s_len, e = u1.shape
    ts = _tile(s_len, ts)
    last = s_len // ts - 1

    def body(dg_ref, u1_ref, z_ref, g_ref, b_ref, du1_ref, dz_ref, dlg_ref, dlb_ref, dcb_ref, a_g, a_b, a_c):
        i = pl.program_id(0)

        @pl.when(i == 0)
        def _():
            a_g[...] = jnp.zeros_like(a_g)
            a_b[...] = jnp.zeros_like(a_b)
            a_c[...] = jnp.zeros_like(a_c)

        g = g_ref[...]
        xhat, rstd, u2 = _ln_rows(u1_ref[...], g, b_ref[...])
        z = z_ref[...]
        dgt = dg_ref[...]
        su, dsu = _silu_pair(u2)
        sz, dsz = _silu_pair(z)
        dz_ref[...] = (dgt * su * dsz).astype(dz_ref.dtype)
        du2 = dgt * sz * dsu
        a_g[...] += _rowsum8(du2 * xhat)
        a_b[...] += _rowsum8(du2)
        dxh = du2 * g
        m1 = jnp.mean(dxh, axis=-1, keepdims=True)
        m2 = jnp.mean(dxh * xhat, axis=-1, keepdims=True)
        du1 = rstd * (dxh - m1 - xhat * m2)
        du1_ref[...] = du1
        a_c[...] += _rowsum8(du1)

        @pl.when(i == last)
        def _():
            dlg_ref[...] = jnp.sum(a_g[...], axis=0, keepdims=True)
            dlb_ref[...] = jnp.sum(a_b[...], axis=0, keepdims=True)
            dcb_ref[...] = jnp.sum(a_c[...], axis=0, keepdims=True)

    row = pl.BlockSpec((ts, e), lambda i: (i, 0))
    vec = pl.BlockSpec((1, e), lambda i: (0, 0))
    return pl.pallas_call(
        body, name=name, grid=(s_len // ts,),
        in_specs=[row, row, pl.BlockSpec((ts, e), lambda i: (i, 2)), vec, vec],
        out_specs=[row, row, vec, vec, vec],
        out_shape=[_sds((s_len, e), F32), _sds((s_len, e), BF16), _sds((1, e), F32), _sds((1, e), F32), _sds((1, e), F32)],
        scratch_shapes=[pltpu.VMEM((8, e), F32)] * 3,
        compiler_params=_params(20 * ts * e * 4),
    )(dgated, u1, proj, ln_g, ln_b)


def _a_conv_bwd(du1, proj, dz, conv_w, *, name, ts=1024, tc=128):
    s_len, e3 = proj.shape
    e = e3 // 3
    k_width = conv_w.shape[0]
    ts, tc = _tile(s_len, ts), _tile(e, tc)
    nc, nr = e // tc, s_len // ts
    kp = 32
    hb = ts // A_HALO

    def body(val, gate, valh, gateh, d_ref, dh_ref, dz_ref, w_ref, dp_ref, dw_ref, buf_u, buf_d, du0, acc, sh_u, sh_d):
        i = pl.program_id(1)

        @pl.when(i == 0)
        def _():
            acc[...] = jnp.zeros_like(acc)

        buf_u[0:A_HALO, :] = jnp.where(i > 0, valh[...] * _sigmoid(gateh[...]), 0.0)
        buf_u[A_HALO:A_HALO + ts, :] = val[...] * _sigmoid(gate[...])
        buf_d[0:ts, :] = d_ref[...]
        buf_d[ts:ts + A_HALO, :] = jnp.where(i < nr - 1, dh_ref[...], 0.0)
        _fill_shifts(buf_u, sh_u)
        _fill_shifts(buf_d, sh_d)

        def init(rows, cs, shape):
            return jnp.zeros(shape, F32)

        def emit(rows, cs, a):
            du0[rows, cs] = a

        _conv_taps(buf_d, sh_d, w_ref, k_width, 0, ts, tc, init, emit, reverse=True)
        _conv_wgrad(buf_u, sh_u, buf_d, acc, k_width, A_HALO - (k_width - 1), ts, tc)
        rw = min(ts, CONV_ROWS)
        for rb in range(ts // rw):
            rows = slice(rb * rw, (rb + 1) * rw)
            d0 = du0[rows, :]
            sg = _sigmoid(gate[rows, :])
            dp_ref[0, rows, :] = (d0 * sg).astype(dp_ref.dtype)
            dp_ref[1, rows, :] = (d0 * val[rows, :] * sg * (1.0 - sg)).astype(dp_ref.dtype)
        dp_ref[2] = dz_ref[...]

        @pl.when(i == nr - 1)
        def _():
            for k in range(kp):
                dw_ref[k:k + 1, :] = jnp.sum(acc[8 * k:8 * k + 8, :], axis=0, keepdims=True)

    in_specs = [
        pl.BlockSpec((ts, tc), lambda j, i: (i, j)),
        pl.BlockSpec((ts, tc), lambda j, i: (i, nc + j)),
        pl.BlockSpec((A_HALO, tc), lambda j, i: (jnp.maximum(i * hb - 1, 0), j)),
        pl.BlockSpec((A_HALO, tc), lambda j, i: (jnp.maximum(i * hb - 1, 0), nc + j)),
        pl.BlockSpec((ts, tc), lambda j, i: (i, j)),
        pl.BlockSpec((A_HALO, tc), lambda j, i: (jnp.minimum((i + 1) * hb, nr * hb - 1), j)),
        pl.BlockSpec((ts, tc), lambda j, i: (i, j)),
        pl.BlockSpec((kp, tc), lambda j, i: (0, j)),
    ]
    w_pad = jnp.zeros((kp, e), F32).at[:k_width].set(conv_w)
    dproj, dw = pl.pallas_call(
        body, name=name, grid=(nc, nr), in_specs=in_specs,
        out_specs=[pl.BlockSpec((3, ts, tc), lambda j, i: (0, i, j)), pl.BlockSpec((kp, tc), lambda j, i: (0, j))],
        out_shape=[_sds((3, s_len, e), BF16), _sds((kp, e), F32)],
        scratch_shapes=[pltpu.VMEM((A_HALO + ts, tc), F32), pltpu.VMEM((ts + A_HALO, tc), F32),
                        pltpu.VMEM((ts, tc), F32), pltpu.VMEM((8 * kp, tc), F32),
                        pltpu.VMEM((SUBLANES - 1, A_HALO + ts - SUBLANES, tc), F32),
                        pltpu.VMEM((SUBLANES - 1, A_HALO + ts - SUBLANES, tc), F32)],
        compiler_params=_params(56 * ts * tc * 4),
    )(proj, proj, proj, proj, du1, du1, dz, w_pad)
    return dproj, dw[:k_width]


C_HALO = 8


def _c_mid_fwd(proj, conv_w, *, name, ts=1024, tc=128):
    s_len, e4 = proj.shape
    e = e4 // 4
    k_width = conv_w.shape[0]
    ts, tc = _tile(s_len, ts), _tile(e, tc)
    nc = e // tc
    hb = ts // C_HALO

    def body(u, bg, cg, z, uh, cgh, w_ref, o_ref, buf, y):
        i = pl.program_id(1)
        buf[0:C_HALO, :] = jnp.where(i > 0, uh[...] * cgh[...], 0.0)
        buf[C_HALO:C_HALO + ts, :] = u[...] * cg[...]

        def init(rows, cs, shape):
            return jnp.zeros(shape, F32)

        def emit(rows, cs, a):
            y[rows, cs] = a

        _conv_taps(buf, None, w_ref, k_width, C_HALO - (k_width - 1), ts, tc, init, emit)
        o_ref[...] = (bg[...] * y[...] * _silu(z[...])).astype(o_ref.dtype)

    def grp(g):
        return pl.BlockSpec((ts, tc), lambda j, i: (i, g * nc + j))

    def halo(g):
        return pl.BlockSpec((C_HALO, tc), lambda j, i: (jnp.maximum(i * hb - 1, 0), g * nc + j))

    w_pad = jnp.zeros((8, e), F32).at[:k_width].set(conv_w)
    return pl.pallas_call(
        body, name=name, grid=(nc, s_len // ts),
        in_specs=[grp(0), grp(1), grp(2), grp(3), halo(0), halo(2), pl.BlockSpec((8, tc), lambda j, i: (0, j))],
        out_specs=pl.BlockSpec((ts, tc), lambda j, i: (i, j)), out_shape=_sds((s_len, e), BF16),
        scratch_shapes=[pltpu.VMEM((C_HALO + ts, tc), F32), pltpu.VMEM((ts, tc), F32)],
        compiler_params=_params(16 * ts * tc * 4),
    )(proj, proj, proj, proj, proj, proj, w_pad)


def _c_mid_bwd(dgated, proj, conv_w, *, name, ts=1024, tc=128):
    s_len, e4 = proj.shape
    e = e4 // 4
    k_width = conv_w.shape[0]
    ts, tc = _tile(s_len, ts), _tile(e, tc)
    nc, nr = e // tc, s_len // ts
    hb = ts // C_HALO

    def body(u, bg, cg, z, uh, cgh, dg, dgh, bgh, zh, w_ref, dp_ref, dw_ref, buf_p, buf_d, y, dpv, acc):
        i = pl.program_id(1)

        @pl.when(i == 0)
        def _():
            acc[...] = jnp.zeros_like(acc)

        uv, bgv, cgv, zv, dgv = u[...], bg[...], cg[...], z[...], dg[...]
        buf_p[0:C_HALO, :] = jnp.where(i > 0, uh[...] * cgh[...], 0.0)
        buf_p[C_HALO:C_HALO + ts, :] = uv * cgv
        sz, dsz = _silu_pair(zv)
        buf_d[0:ts, :] = dgv * sz * bgv
        buf_d[ts:ts + C_HALO, :] = jnp.where(i < nr - 1, dgh[...] * _silu(zh[...]) * bgh[...], 0.0)

        def init(rows, cs, shape):
            return jnp.zeros(shape, F32)

        def emit_y(rows, cs, a):
            y[rows, cs] = a

        def emit_dp(rows, cs, a):
            dpv[rows, cs] = a

        _conv_taps(buf_p, None, w_ref, k_width, C_HALO - (k_width - 1), ts, tc, init, emit_y)
        _conv_taps(buf_d, None, w_ref, k_width, 0, ts, tc, init, emit_dp, reverse=True)
        _conv_wgrad(buf_p, None, buf_d, acc, k_width, C_HALO - (k_width - 1), ts, tc)
        yv, dp = y[...], dpv[...]
        dp_ref[0] = (dp * cgv).astype(dp_ref.dtype)
        dp_ref[1] = (dgv * sz * yv).astype(dp_ref.dtype)
        dp_ref[2] = (dp * uv).astype(dp_ref.dtype)
        dp_ref[3] = (dgv * bgv * yv * dsz).astype(dp_ref.dtype)

        @pl.when(i == nr - 1)
        def _():
            for k in range(8):
                dw_ref[k:k + 1, :] = jnp.sum(acc[8 * k:8 * k + 8, :], axis=0, keepdims=True)

    def grp(g):
        return pl.BlockSpec((ts, tc), lambda j, i: (i, g * nc + j))

    def prev(g):
        return pl.BlockSpec((C_HALO, tc), lambda j, i: (jnp.maximum(i * hb - 1, 0), g * nc + j))

    def nxt(g):
        return pl.BlockSpec((C_HALO, tc), lambda j, i: (jnp.minimum((i + 1) * hb, nr * hb - 1), g * nc + j))

    w_pad = jnp.zeros((8, e), F32).at[:k_width].set(conv_w)
    dproj, dw = pl.pallas_call(
        body, name=name, grid=(nc, nr),
        in_specs=[grp(0), grp(1), grp(2), grp(3), prev(0), prev(2),
                  pl.BlockSpec((ts, tc), lambda j, i: (i, j)),
                  pl.BlockSpec((C_HALO, tc), lambda j, i: (jnp.minimum((i + 1) * hb, nr * hb - 1), j)),
                  nxt(1), nxt(3), pl.BlockSpec((8, tc), lambda j, i: (0, j))],
        out_specs=[pl.BlockSpec((4, ts, tc), lambda j, i: (0, i, j)), pl.BlockSpec((8, tc), lambda j, i: (0, j))],
        out_shape=[_sds((4, s_len, e), BF16), _sds((8, e), F32)],
        scratch_shapes=[pltpu.VMEM((C_HALO + ts, tc), F32), pltpu.VMEM((ts + C_HALO, tc), F32),
                        pltpu.VMEM((ts, tc), F32), pltpu.VMEM((ts, tc), F32), pltpu.VMEM((64, tc), F32)],
        compiler_params=_params(32 * ts * tc * 4),
    )(proj, proj, proj, proj, proj, proj, dgated, dgated, proj, proj, w_pad)
    return dproj, dw[:k_width]


def _head_rms(xv, g):
    r = lax.rsqrt(jnp.mean(xv * xv, axis=-1, keepdims=True) + NORM_EPS)
    return r, xv * r * g


def _b_qk_fwd(proj, gq, gk, *, name, ts=512, tc=512):
    s_len, e4 = proj.shape
    e = e4 // 4
    ts, tc = _tile(s_len, ts), _tile(e, tc)
    nc = e // tc

    def body(q, k, v, gq_ref, gk_ref, qn, kn, vb):
        for h in range(tc // HEAD_DIM):
            cs = slice(h * HEAD_DIM, (h + 1) * HEAD_DIM)
            qn[:, cs] = _head_rms(q[:, cs], gq_ref[...])[1].astype(qn.dtype)
            kn[:, cs] = _head_rms(k[:, cs], gk_ref[...])[1].astype(kn.dtype)
        vb[...] = v[...].astype(vb.dtype)

    def grp(g):
        return pl.BlockSpec((ts, tc), lambda i, j: (i, g * nc + j))

    vec = pl.BlockSpec((1, HEAD_DIM), lambda i, j: (0, 0))
    out = pl.BlockSpec((ts, tc), lambda i, j: (i, j))
    return pl.pallas_call(
        body, name=name, grid=(s_len // ts, nc), in_specs=[grp(0), grp(1), grp(2), vec, vec],
        out_specs=[out, out, out], out_shape=[_sds((s_len, e), BF16)] * 3,
        compiler_params=_params(16 * ts * tc * 4),
    )(proj, proj, proj, gq, gk)


def _b_qk_bwd(dqn, dkn, dv, dz, proj, gq, gk, *, name, ts=512, tc=512):
    s_len, e4 = proj.shape
    e = e4 // 4
    ts, tc = _tile(s_len, ts), _tile(e, tc)
    nc, nr = e // tc, s_len // ts

    def body(dq_ref, dk_ref, dv_ref, dz_ref, q, k, gq_ref, gk_ref, dp_ref, dgq_ref, dgk_ref, a_q, a_k):
        i, j = pl.program_id(0), pl.program_id(1)

        @pl.when((i == 0) & (j == 0))
        def _():
            a_q[...] = jnp.zeros_like(a_q)
            a_k[...] = jnp.zeros_like(a_k)

        for h in range(tc // HEAD_DIM):
            cs = slice(h * HEAD_DIM, (h + 1) * HEAD_DIM)
            for slot, src, d_ref, g_ref, acc in ((0, q, dq_ref, gq_ref, a_q), (1, k, dk_ref, gk_ref, a_k)):
                xv = src[:, cs]
                dy = d_ref[:, cs]
                r = lax.rsqrt(jnp.mean(xv * xv, axis=-1, keepdims=True) + NORM_EPS)
                gd = dy * g_ref[...]
                dot = jnp.mean(xv * gd, axis=-1, keepdims=True)
                dp_ref[slot, :, cs] = (r * gd - xv * (r * r * r * dot)).astype(dp_ref.dtype)
                acc[...] += _rowsum8(dy * (xv * r))
        dp_ref[2] = dv_ref[...].astype(dp_ref.dtype)
        dp_ref[3] = dz_ref[...]

        @pl.when((i == nr - 1) & (j == nc - 1))
        def _():
            dgq_ref[...] = jnp.sum(a_q[...], axis=0, keepdims=True)
            dgk_ref[...] = jnp.sum(a_k[...], axis=0, keepdims=True)

    blk = pl.BlockSpec((ts, tc), lambda i, j: (i, j))
    vec = pl.BlockSpec((1, HEAD_DIM), lambda i, j: (0, 0))

    def grp(g):
        return pl.BlockSpec((ts, tc), lambda i, j: (i, g * nc + j))

    return pl.pallas_call(
        body, name=name, grid=(nr, nc), in_specs=[blk, blk, blk, blk, grp(0), grp(1), vec, vec],
        out_specs=[pl.BlockSpec((4, ts, tc), lambda i, j: (0, i, j)), vec, vec],
        out_shape=[_sds((4, s_len, e), BF16), _sds((1, HEAD_DIM), F32), _sds((1, HEAD_DIM), F32)],
        scratch_shapes=[pltpu.VMEM((8, HEAD_DIM), F32)] * 2,
        compiler_params=_params(24 * ts * tc * 4),
    )(dqn, dkn, dv, dz, proj, proj, gq, gk)


def _log_sigmoid(x):
    y = jnp.exp(-jnp.abs(x))
    u = 1.0 + y
    log1p = jnp.where(u == 1.0, y, jnp.log(u) * (y / jnp.where(u == 1.0, 1.0, u - 1.0)))
    return jnp.minimum(x, 0.0) - log1p


def _split3(v):
    hi = v.astype(BF16)
    r1 = v - hi.astype(F32)
    mid = r1.astype(BF16)
    lo = (r1 - mid.astype(F32)).astype(BF16)
    return hi, mid, lo


def _tri_matmul(tri, v):
    hi, mid, lo = _split3(v)
    return (jnp.dot(tri, hi, preferred_element_type=F32) + jnp.dot(tri, mid, preferred_element_type=F32)
            + jnp.dot(tri, lo, preferred_element_type=F32))


def _b_cumsum(fl, bias, *, name, t=512):
    s_len, w = fl.shape
    t = _tile(s_len, t)

    def body(fl_ref, b_ref, c_ref, carry):
        @pl.when(pl.program_id(0) == 0)
        def _():
            carry[...] = jnp.zeros_like(carry)

        logf = _log_sigmoid(fl_ref[...] + b_ref[...])
        row = lax.broadcasted_iota(jnp.int32, (t, t), 0)
        col = lax.broadcasted_iota(jnp.int32, (t, t), 1)
        tri = jnp.where(col <= row, 1.0, 0.0).astype(BF16)
        c = _tri_matmul(tri, logf) + carry[...]
        c_ref[...] = c
        carry[...] = c[t - 1:t, :]

    return pl.pallas_call(
        body, name=name, grid=(s_len // t,),
        in_specs=[pl.BlockSpec((t, w), lambda i: (i, 0)), pl.BlockSpec((1, w), lambda i: (0, 0))],
        out_specs=pl.BlockSpec((t, w), lambda i: (i, 0)), out_shape=_sds((s_len, w), F32),
        scratch_shapes=[pltpu.VMEM((1, w), F32)], compiler_params=_params(16 << 20),
    )(fl, bias)


def _b_cumsum_bwd(dc, fl, bias, *, name, t=512):
    s_len, w = fl.shape
    t = _tile(s_len, t)
    n = s_len // t

    def body(dc_ref, fl_ref, b_ref, dfl_ref, db_ref, carry, acc):
        i = pl.program_id(0)

        @pl.when(i == 0)
        def _():
            carry[...] = jnp.zeros_like(carry)
            acc[...] = jnp.zeros_like(acc)

        row = lax.broadcasted_iota(jnp.int32, (t, t), 0)
        col = lax.broadcasted_iota(jnp.int32, (t, t), 1)
        tri = jnp.where(col >= row, 1.0, 0.0).astype(BF16)
        dlogf = _tri_matmul(tri, dc_ref[...]) + carry[...]
        carry[...] = dlogf[0:1, :]
        dfl = dlogf * _sigmoid(-(fl_ref[...] + b_ref[...]))
        dfl_ref[...] = dfl
        acc[...] += _rowsum8(dfl)

        @pl.when(i == n - 1)
        def _():
            db_ref[...] = jnp.sum(acc[...], axis=0, keepdims=True)

    rev = pl.BlockSpec((t, w), lambda i: (n - 1 - i, 0))
    vec = pl.BlockSpec((1, w), lambda i: (0, 0))
    return pl.pallas_call(
        body, name=name, grid=(n,), in_specs=[rev, rev, vec], out_specs=[rev, vec],
        out_shape=[_sds((s_len, w), F32), _sds((1, w), F32)],
        scratch_shapes=[pltpu.VMEM((1, w), F32), pltpu.VMEM((8, w), F32)], compiler_params=_params(16 << 20),
    )(dc, fl, bias)


def _b_gate_fwd(o, proj, *, name, ts=512, tc=512):
    s_len, e = o.shape
    ts, tc = _tile(s_len, ts), _tile(e, tc)
    nc = e // tc

    def body(o_ref, z_ref, g_ref):
        g_ref[...] = (o_ref[...] * _silu(z_ref[...])).astype(g_ref.dtype)

    blk = pl.BlockSpec((ts, tc), lambda i, j: (i, j))
    return pl.pallas_call(
        body, name=name, grid=(s_len // ts, nc),
        in_specs=[blk, pl.BlockSpec((ts, tc), lambda i, j: (i, 3 * nc + j))], out_specs=blk,
        out_shape=_sds((s_len, e), BF16), compiler_params=_params(12 * ts * tc * 4),
    )(o, proj)


def _b_gate_bwd(dgated, o, o_lo, proj, *, name, hg, ts=512):
    s_len, e = o.shape
    ts = _tile(s_len, ts)
    w = hg * HEAD_DIM
    ng = e // w

    def body(dg_ref, o_ref, olo_ref, z_ref, do_ref, dz_ref, dl_ref):
        dgt, ov, zv = dg_ref[...], o_ref[...], z_ref[...]
        sz, dsz = _silu_pair(zv)
        dob = (dgt * sz).astype(do_ref.dtype)
        do_ref[...] = dob
        dz_ref[...] = (dgt * ov * dsz).astype(dz_ref.dtype)
        prod = dob.astype(F32) * (ov + olo_ref[...])
        for hh in range(hg):
            cs = slice(hh * HEAD_DIM, (hh + 1) * HEAD_DIM)
            dl_ref[:, cs] = jnp.broadcast_to(jnp.sum(prod[:, cs], axis=-1, keepdims=True), (ts, HEAD_DIM))

    blk = pl.BlockSpec((ts, w), lambda i, j: (i, j))
    return pl.pallas_call(
        body, name=name, grid=(s_len // ts, ng),
        in_specs=[blk, blk, blk, pl.BlockSpec((ts, w), lambda i, j: (i, 3 * ng + j))],
        out_specs=[blk, blk, blk],
        out_shape=[_sds((s_len, e), BF16), _sds((s_len, e), BF16), _sds((s_len, e), F32)],
        compiler_params=_params(24 * ts * w * 4),
    )(dgated, o, o_lo, proj)


LOG2E = 1.4426950408889634
ATTN_ROW_CHUNK = 128


def _tri_tables(n, k_major):
    pairs = [(i, j) for j in range(n) for i in range(j, n)] if k_major else [(i, j) for i in range(n) for j in range(i + 1)]
    return (jnp.asarray(np.array([p[0] for p in pairs], np.int32)), jnp.asarray(np.array([p[1] for p in pairs], np.int32)))


def _attn_logits2(s_raw, cr2, diag, row0, c1):
    s2 = s_raw * c1 - cr2
    if diag:
        rc, t = s_raw.shape
        row = lax.broadcasted_iota(jnp.int32, (rc, t), 0) + row0
        col = lax.broadcasted_iota(jnp.int32, (rc, t), 1)
        s2 = jnp.where(col <= row, s2, -jnp.inf)
    return s2


def _fox_fwd(qn, kn, vb, c_row, proj, *, name, hg, t=512):
    s_len, e = qn.shape
    t = _tile(s_len, t)
    rc = _tile(t, ATTN_ROW_CHUNK)
    w = hg * HEAD_DIM
    ng, n = e // w, s_len // t
    c1 = HEAD_DIM ** -0.5 * LOG2E
    qi_tab, kj_tab = _tri_tables(n, k_major=False)

    def body(qi_ref, kj_ref, q_ref, k_ref, v_ref, cr_ref, z_ref, o_ref, olo_ref, lse_ref, g_ref, m_s, l_s, acc_s, lo_s,
             s_scr, p_scr, a_scr):
        pid = pl.program_id(1)
        i, j = qi_ref[pid], kj_ref[pid]

        @pl.when(j == 0)
        def _():
            m_s[...] = jnp.full_like(m_s, -jnp.inf)
            l_s[...] = jnp.zeros_like(l_s)
            acc_s[...] = jnp.zeros_like(acc_s)
            lo_s[...] = jnp.zeros_like(lo_s)

        def step(diag):
            for hh in range(hg):
                cs = slice(hh * HEAD_DIM, (hh + 1) * HEAD_DIM)
                s_scr[...] = lax.dot_general(q_ref[:, cs], k_ref[:, cs], (((1,), (1,)), ((), ())),
                                             preferred_element_type=F32)
                cr2 = cr_ref[hh] * LOG2E
                for r in range(t // rc):
                    rows = slice(r * rc, (r + 1) * rc)
                    s2 = _attn_logits2(s_scr[rows, :], cr2, diag, r * rc, c1)
                    m_prev = m_s[hh, rows]
                    m_new = jnp.maximum(m_prev, jnp.max(s2, axis=-1, keepdims=True))
                    alpha = jnp.exp2(m_prev - m_new)
                    p = jnp.exp2(s2 - jnp.tile(m_new, (1, t // LANES)))
                    l_s[hh, rows] = alpha * l_s[hh, rows] + jnp.sum(p, axis=-1, keepdims=True)
                    m_s[hh, rows] = m_new
                    a_scr[rows] = alpha
                    p_hi = p.astype(BF16)
                    p_scr[rows, :] = p_hi
                    p_scr[t + r * rc:t + (r + 1) * rc, :] = (p - p_hi.astype(F32)).astype(BF16)
                pv = jnp.dot(p_scr[...], v_ref[:, cs], preferred_element_type=F32)
                al = a_scr[...]
                acc_s[:, cs] = al * acc_s[:, cs] + pv[:t]
                lo_s[:, cs] = al * lo_s[:, cs] + pv[t:]

        @pl.when(j < i)
        def _():
            step(False)

        @pl.when(j == i)
        def _():
            step(True)
            for hh in range(hg):
                cs = slice(hh * HEAD_DIM, (hh + 1) * HEAD_DIM)
                ov = acc_s[:, cs] / l_s[hh]
                o_ref[:, cs] = ov
                olo_ref[:, cs] = lo_s[:, cs] / l_s[hh]
                lse_ref[:, cs] = m_s[hh] + jnp.log2(l_s[hh])
                g_ref[:, cs] = (ov * _silu(z_ref[:, cs])).astype(g_ref.dtype)

    qspec = pl.BlockSpec((t, w), lambda g, p, qi, kj: (qi[p], g))
    kspec = pl.BlockSpec((t, w), lambda g, p, qi, kj: (kj[p], g))
    zspec = pl.BlockSpec((t, w), lambda g, p, qi, kj: (qi[p], 3 * ng + g))
    crow = pl.BlockSpec((hg, 1, t), lambda g, p, qi, kj: (g, 0, kj[p]))
    grid_spec = pltpu.PrefetchScalarGridSpec(
        num_scalar_prefetch=2, grid=(ng, int(qi_tab.shape[0])), in_specs=[qspec, kspec, kspec, crow, zspec],
        out_specs=[qspec, qspec, qspec, qspec],
        scratch_shapes=[pltpu.VMEM((hg, t, LANES), F32), pltpu.VMEM((hg, t, LANES), F32), pltpu.VMEM((t, w), F32),
                        pltpu.VMEM((t, w), F32), pltpu.VMEM((t, t), F32), pltpu.VMEM((2 * t, t), BF16),
                        pltpu.VMEM((t, LANES), F32)])
    return pl.pallas_call(
        body, name=name, grid_spec=grid_spec,
        out_shape=[_sds((s_len, e), F32), _sds((s_len, e), F32), _sds((s_len, e), F32), _sds((s_len, e), BF16)],
        compiler_params=_params(12 * t * t * 4 + 32 * t * w * 4),
    )(qi_tab, kj_tab, qn, kn, vb, c_row, proj)


def _fox_bwd(qn, kn, vb, do, lse, delta, c_row, *, name, hg, t=512):
    s_len, e = qn.shape
    t = _tile(s_len, t)
    w = hg * HEAD_DIM
    ng, n = e // w, s_len // t
    rc = _tile(t, ATTN_ROW_CHUNK)
    scale = HEAD_DIM ** -0.5
    c1 = scale * LOG2E
    qi_tab, kj_tab = _tri_tables(n, k_major=True)

    def body(qi_ref, kj_ref, q_ref, k_ref, v_ref, do_ref, lse_ref, dl_ref, cr_ref, dq_ref, dk_ref, dv_ref, dc_ref,
             dk_s, dv_s, dc_s, s_scr, dp_scr, p_scr, ds_scr):
        pid = pl.program_id(1)
        i, j = qi_ref[pid], kj_ref[pid]

        @pl.when(pid == 0)
        def _():
            dq_ref[...] = jnp.zeros_like(dq_ref)

        @pl.when(i == j)
        def _():
            dk_s[...] = jnp.zeros_like(dk_s)
            dv_s[...] = jnp.zeros_like(dv_s)
            dc_s[...] = jnp.zeros_like(dc_s)

        def step(diag):
            qrows = pl.ds(pl.multiple_of(i * t, t), t)
            for hh in range(hg):
                cs = slice(hh * HEAD_DIM, (hh + 1) * HEAD_DIM)
                s_scr[...] = lax.dot_general(q_ref[:, cs], k_ref[:, cs], (((1,), (1,)), ((), ())),
                                             preferred_element_type=F32)
                dp_scr[...] = lax.dot_general(do_ref[:, cs], v_ref[:, cs], (((1,), (1,)), ((), ())),
                                              preferred_element_type=F32)
                cr2 = cr_ref[hh] * LOG2E
                dcol = jnp.zeros((SUBLANES, t), F32)
                for r in range(t // rc):
                    rows = slice(r * rc, (r + 1) * rc)
                    s2 = _attn_logits2(s_scr[rows, :], cr2, diag, r * rc, c1)
                    p = jnp.exp2(s2 - jnp.tile(lse_ref[rows, cs], (1, t // LANES)))
                    ds = p * (dp_scr[rows, :] - jnp.tile(dl_ref[rows, cs], (1, t // LANES)))
                    dcol = dcol + _rowsum8(ds)
                    p_scr[rows, :] = p.astype(BF16)
                    ds_scr[rows, :] = (ds * scale).astype(BF16)
                dc_s[hh] -= jnp.sum(dcol, axis=0, keepdims=True)
                dv_s[:, cs] += lax.dot_general(p_scr[...], do_ref[:, cs], (((0,), (0,)), ((), ())),
                                               preferred_element_type=F32)
                dk_s[:, cs] += lax.dot_general(ds_scr[...], q_ref[:, cs], (((0,), (0,)), ((), ())),
                                               preferred_element_type=F32)
                dq_ref[qrows, cs] += jnp.dot(ds_scr[...], k_ref[:, cs], preferred_element_type=F32)

        @pl.when(i > j)
        def _():
            step(False)

        @pl.when(i == j)
        def _():
            step(True)

        @pl.when(i == n - 1)
        def _():
            dk_ref[...] = dk_s[...]
            dv_ref[...] = dv_s[...]
            dc_ref[...] = dc_s[...]

    qspec = pl.BlockSpec((t, w), lambda g, p, qi, kj: (qi[p], g))
    kspec = pl.BlockSpec((t, w), lambda g, p, qi, kj: (kj[p], g))
    crow = pl.BlockSpec((hg, 1, t), lambda g, p, qi, kj: (g, 0, kj[p]))
    grid_spec = pltpu.PrefetchScalarGridSpec(
        num_scalar_prefetch=2, grid=(ng, int(qi_tab.shape[0])),
        in_specs=[qspec, kspec, kspec, qspec, qspec, qspec, crow],
        out_specs=[pl.BlockSpec((s_len, w), lambda g, p, qi, kj: (0, g)), kspec, kspec, crow],
        scratch_shapes=[pltpu.VMEM((t, w), F32), pltpu.VMEM((t, w), F32), pltpu.VMEM((hg, 1, t), F32),
                        pltpu.VMEM((t, t), F32), pltpu.VMEM((t, t), F32), pltpu.VMEM((t, t), BF16),
                        pltpu.VMEM((t, t), BF16)])
    return pl.pallas_call(
        body, name=name, grid_spec=grid_spec,
        out_shape=[_sds((s_len, e), F32), _sds((s_len, e), F32), _sds((s_len, e), F32), _sds((e // HEAD_DIM, 1, s_len), F32)],
        compiler_params=_params(2 * s_len * w * 4 + 16 * t * t * 4 + 24 * t * w * 4),
    )(qi_tab, kj_tab, qn, kn, vb, do, lse, delta, c_row)


def _adamw(w, gs, m, v, *, name, tr=256):
    r, c = w.shape
    tr = _tile(r, tr)
    n_g = len(gs)

    def body(*refs):
        w_ref, g_refs = refs[0], refs[1:1 + n_g]
        m_ref, v_ref, go_ref, d_ref, nm_ref, nv_ref = refs[1 + n_g:]
        gv = g_refs[0][...].astype(F32)
        for g_ref in g_refs[1:]:
            gv = gv + g_ref[...].astype(F32)
        go_ref[...] = gv
        m2 = ADAM_B1 * m_ref[...] + (1.0 - ADAM_B1) * gv
        v2 = ADAM_B2 * v_ref[...] + (1.0 - ADAM_B2) * (gv * gv)
        m_hat = m2 / (1.0 - ADAM_B1 ** ADAM_STEP)
        v_hat = v2 / (1.0 - ADAM_B2 ** ADAM_STEP)
        d_ref[...] = -ADAM_LR * (m_hat / (jnp.sqrt(v_hat) + ADAM_EPS) + ADAM_WD * w_ref[...])
        nm_ref[...] = m2
        nv_ref[...] = v2

    blk = pl.BlockSpec((tr, c), lambda i: (i, 0))
    return pl.pallas_call(
        body, name=name, grid=(r // tr,), in_specs=[blk] * (3 + n_g), out_specs=[blk] * 4,
        out_shape=[_sds((r, c), F32)] * 4, compiler_params=_params(24 * tr * c * 4),
    )(w, *gs, m, v)


def _adamw_halves(w, mine, other, m, v, c_arr, *, name, tr=256):
    r, c = w.shape
    half = r // 2
    tr = _tile(half, tr)
    nh = half // tr

    def body(c_ref, w_ref, mine_ref, other_ref, m_ref, v_ref, go_ref, d_ref, nm_ref, nv_ref):
        is_mine = pl.program_id(0) == c_ref[0]
        gv = jnp.where(is_mine, mine_ref[...], other_ref[...])
        go_ref[...] = gv
        m2 = ADAM_B1 * m_ref[...] + (1.0 - ADAM_B1) * gv
        v2 = ADAM_B2 * v_ref[...] + (1.0 - ADAM_B2) * (gv * gv)
        m_hat = m2 / (1.0 - ADAM_B1 ** ADAM_STEP)
        v_hat = v2 / (1.0 - ADAM_B2 ** ADAM_STEP)
        d_ref[...] = -ADAM_LR * (m_hat / (jnp.sqrt(v_hat) + ADAM_EPS) + ADAM_WD * w_ref[...])
        nm_ref[...] = m2
        nv_ref[...] = v2

    full = pl.BlockSpec((tr, c), lambda h, j, cref: (h * nh + j, 0))
    part = pl.BlockSpec((tr, c), lambda h, j, cref: (j, 0))
    grid_spec = pltpu.PrefetchScalarGridSpec(num_scalar_prefetch=1, grid=(2, nh), in_specs=[full, part, part, full, full],
                                             out_specs=[full] * 4)
    return pl.pallas_call(
        body, name=name, grid_spec=grid_spec, out_shape=[_sds((r, c), F32)] * 4,
        compiler_params=_params(28 * tr * c * 4),
    )(c_arr, w, mine, other, m, v)


def _pair_add(g, got, c_arr, *, name, tr=256):
    n, r, c = g.shape
    half = r // 2
    tr = _tile(half, tr)
    nh = half // tr

    def body(c_ref, g_ref, got_ref, o_ref):
        o_ref[...] = (g_ref[...].astype(F32) + got_ref[...].astype(F32)).astype(o_ref.dtype)

    grid_spec = pltpu.PrefetchScalarGridSpec(
        num_scalar_prefetch=1, grid=(n, nh),
        in_specs=[pl.BlockSpec((None, tr, c), lambda s, i, cref: (s, cref[0] * nh + i, 0)),
                  pl.BlockSpec((None, tr, c), lambda s, i, cref: (s, i, 0))],
        out_specs=pl.BlockSpec((None, tr, c), lambda s, i, cref: (s, i, 0)))
    return pl.pallas_call(
        body, name=name, grid_spec=grid_spec, out_shape=_sds((n, half, c), BF16), compiler_params=_params(16 * tr * c * 4),
    )(c_arr, g, got)


def _sum_own_recv(pair, recv, chip_arr, *, name, tr=256):
    _, r, c = pair.shape
    tr = _tile(r, tr)
    n_recv = recv.shape[0]

    def body(chip_ref, own_ref, recv_ref, o_ref):
        acc = own_ref[...].astype(F32)
        for k in range(n_recv):
            acc = acc + recv_ref[k].astype(F32)
        o_ref[...] = acc

    grid_spec = pltpu.PrefetchScalarGridSpec(
        num_scalar_prefetch=1, grid=(r // tr,),
        in_specs=[pl.BlockSpec((None, tr, c), lambda i, chip: (chip[0], i, 0)),
                  pl.BlockSpec((n_recv, tr, c), lambda i, chip: (0, i, 0))],
        out_specs=pl.BlockSpec((tr, c), lambda i, chip: (i, 0)))
    return pl.pallas_call(
        body, name=name, grid_spec=grid_spec, out_shape=_sds((r, c), F32), compiler_params=_params(16 * tr * c * 4),
    )(chip_arr, pair, recv)


_ANY = pl.BlockSpec(memory_space=pl.ANY)
DMA_CHUNK_BYTES = 512 << 10


def _chunks(parts):
    out = []
    for src_at, dst_at, rows, row_bytes in parts:
        step = max(16, DMA_CHUNK_BYTES // row_bytes // 16 * 16)
        for r0 in range(0, rows, step):
            n = min(step, rows - r0)
            out.append((src_at(r0, n), dst_at(r0, n)))
    return out


def _row_bytes(ref):
    return ref.shape[-1] * ref.dtype.itemsize


def _me():
    return lax.axis_index("x"), lax.axis_index("y"), lax.axis_index("c")


def _chip_peers(x, y):
    return [(2 * (1 - x) + y, (1 - x, y)), (2 * x + (1 - y), (x, 1 - y)), (2 * (1 - x) + (1 - y), (1 - x, 1 - y))]


def _exchange(name, ins, out_shapes, plan):
    n_in, n_out = len(ins), len(out_shapes)

    def body(*refs):
        in_refs, out_refs = refs[:n_in], refs[n_in:n_in + n_out]
        send_sems, recv_sems, loc_sems = refs[n_in + n_out:]
        remote, local = plan(in_refs, out_refs)
        starts, waits = [], []
        for k, (ws, wd, dev, parts) in enumerate(remote):
            def mk(s, d, k=k, dev=dev):
                return pltpu.make_async_remote_copy(src_ref=s, dst_ref=d, send_sem=send_sems.at[k],
                                                    recv_sem=recv_sems.at[k], device_id=dev, device_id_type=MESH_ID)
            starts += [mk(s, d) for s, d in _chunks(parts)]
            waits.append(mk(ws, wd))
        for k, (ws, wd, _, parts) in enumerate(local):
            def mk(s, d, k=k):
                return pltpu.make_async_copy(s, d, loc_sems.at[k])
            starts += [mk(s, d) for s, d in _chunks(parts)]
            waits.append(mk(ws, wd))
        for cp in starts:
            cp.start()
        for cp in waits:
            cp.wait()

    n_remote, n_local = plan.n_remote, plan.n_local
    return pl.pallas_call(
        body, name=name, in_specs=[_ANY] * n_in, out_specs=[_ANY] * n_out, out_shape=list(out_shapes),
        scratch_shapes=[pltpu.SemaphoreType.DMA((n_remote,)), pltpu.SemaphoreType.DMA((n_remote,)),
                        pltpu.SemaphoreType.DMA((max(n_local, 1),))],
    )(*ins)


def _gather_weights(shards, *, name):
    n_t = len(shards)

    def body(*refs):
        in_refs, out_refs = refs[:n_t], refs[n_t:2 * n_t]
        send_sems, recv_sems = refs[2 * n_t:]
        x, y, c = _me()
        chip = 2 * x + y
        peers = _chip_peers(x, y)
        local, first, passed = [], [], []
        for t in range(n_t):
            src, dst = in_refs[t], out_refs[t]
            n_rows = src.shape[0]
            half = n_rows // 2
            rb = _row_bytes(src)
            rows = pl.ds(c * half, half)

            def mk_own(s, d, t=t):
                return pltpu.make_async_remote_copy(
                    src_ref=s, dst_ref=d, send_sem=send_sems.at[7 * t + 6], recv_sem=recv_sems.at[7 * t + 6],
                    device_id=(x, y, 1 - c), device_id_type=MESH_ID)

            own = [(lambda r0, n, src=src: src.at[pl.ds(r0, n)],
                    lambda r0, n, dst=dst: dst.at[chip, pl.ds(r0, n)], n_rows, rb)]
            local.append((mk_own(src, dst.at[chip]), [mk_own(s, d) for s, d in _chunks(own)]))
            for k, (pchip, (px, py)) in enumerate(peers):
                def mk_ici(s, d, t=t, k=k, px=px, py=py):
                    return pltpu.make_async_remote_copy(
                        src_ref=s, dst_ref=d, send_sem=send_sems.at[7 * t + k], recv_sem=recv_sems.at[7 * t + k],
                        device_id=(px, py, c), device_id_type=MESH_ID)

                def mk_d2d(s, d, t=t, k=k):
                    return pltpu.make_async_remote_copy(
                        src_ref=s, dst_ref=d, send_sem=send_sems.at[7 * t + 3 + k], recv_sem=recv_sems.at[7 * t + 3 + k],
                        device_id=(x, y, 1 - c), device_id_type=MESH_ID)

                out_part = [(lambda r0, n, src=src: src.at[pl.ds(c * half + r0, n)],
                             lambda r0, n, dst=dst: dst.at[chip, pl.ds(c * half + r0, n)], half, rb)]
                fwd_part = [(lambda r0, n, dst=dst, pchip=pchip: dst.at[pchip, pl.ds(c * half + r0, n)],
                             lambda r0, n, dst=dst, pchip=pchip: dst.at[pchip, pl.ds(c * half + r0, n)], half, rb)]
                first.append((mk_ici(src.at[rows], dst.at[chip, rows]), [mk_ici(s, d) for s, d in _chunks(out_part)]))
                passed.append((mk_d2d(dst.at[pchip, rows], dst.at[pchip, rows]),
                               [mk_d2d(s, d) for s, d in _chunks(fwd_part)]))
        for _, chunk_copies in first + local:
            for cp in chunk_copies:
                cp.start()
        for (whole, _), (_, fwd_copies) in zip(first, passed):
            whole.wait_recv()
            for cp in fwd_copies:
                cp.start()
        for whole, _ in passed:
            whole.wait_recv()
        for whole, _ in first + passed:
            whole.wait_send()
        for whole, _ in local:
            whole.wait()

    outs = [_sds((N_CHIPS,) + s.shape, s.dtype) for s in shards]
    return pl.pallas_call(
        body, name=name, in_specs=[_ANY] * n_t, out_specs=[_ANY] * n_t, out_shape=outs,
        scratch_shapes=[pltpu.SemaphoreType.DMA((7 * n_t,)), pltpu.SemaphoreType.DMA((7 * n_t,))],
    )(*shards)


class _Plan:
    def __init__(self, fn, n_remote, n_local):
        self.fn, self.n_remote, self.n_local = fn, n_remote, n_local

    def __call__(self, in_refs, out_refs):
        return self.fn(in_refs, out_refs)


def _reduce_grads(grads, *, name):
    n_t = len(grads)
    c_arr = lax.axis_index("c").astype(jnp.int32).reshape(1)
    chip_arr = (2 * lax.axis_index("x") + lax.axis_index("y")).astype(jnp.int32).reshape(1)

    def plan1(in_refs, out_refs):
        x, y, c = _me()
        remote = []
        for t in range(n_t):
            src, got = in_refs[t], out_refs[t]
            half = src.shape[1] // 2
            send = [(lambda r0, n, s=s, src=src, half=half: src.at[s, pl.ds((1 - c) * half + r0, n)],
                     lambda r0, n, s=s, got=got: got.at[s, pl.ds(r0, n)], half, _row_bytes(src)) for s in range(N_CHIPS)]
            remote.append((src.at[:, pl.ds((1 - c) * half, half)], got, (x, y, 1 - c), send))
        return remote, []

    halves = [_sds((N_CHIPS, g.shape[1] // 2, g.shape[2]), g.dtype) for g in grads]
    got = _exchange(name + "_sib", grads, halves, _Plan(plan1, n_t, 0))
    pair = [_pair_add(grads[t], got[t], c_arr, name=f"{name}_pair{t}") for t in range(n_t)]

    def plan2(in_refs, out_refs):
        x, y, c = _me()
        remote = []
        for t in range(n_t):
            src, dst = in_refs[t], out_refs[t]
            rows, rb = src.shape[1], _row_bytes(src)
            for k, (pchip, (px, py)) in enumerate(_chip_peers(x, y)):
                part = [(lambda r0, n, src=src, pchip=pchip: src.at[pchip, pl.ds(r0, n)],
                         lambda r0, n, dst=dst, k=k: dst.at[k, pl.ds(r0, n)], rows, rb)]
                remote.append((src.at[pchip], dst.at[k], (px, py, c), part))
        return remote, []

    recv_shapes = [_sds((N_CHIPS - 1,) + h.shape[1:], h.dtype) for h in halves]
    recv = _exchange(name + "_ici", pair, recv_shapes, _Plan(plan2, 3 * n_t, 0))
    mine = [_sum_own_recv(pair[t], recv[t], chip_arr, name=f"{name}_sum{t}") for t in range(n_t)]

    def plan3(in_refs, out_refs):
        x, y, c = _me()
        remote = []
        for t in range(n_t):
            src, dst = in_refs[t], out_refs[t]
            rows = [(lambda r0, n, src=src: src.at[pl.ds(r0, n)], lambda r0, n, dst=dst: dst.at[pl.ds(r0, n)],
                     src.shape[0], _row_bytes(src))]
            remote.append((src, dst, (x, y, 1 - c), rows))
        return remote, []

    other = _exchange(name + "_swap", mine, [_sds(s.shape, s.dtype) for s in mine], _Plan(plan3, n_t, 0))
    return list(zip(mine, other)), c_arr


def _allreduce_small(pack, *, name):
    r, w = pack.shape

    def body(p_ref, o_ref, buf, send_sems, recv_sems):
        x, y, c = _me()
        me = 4 * x + 2 * y + c
        buf[me] = p_ref[...]
        copies = []
        for k in range(1, N_DEV):
            peer = (x ^ ((k >> 2) & 1), y ^ ((k >> 1) & 1), c ^ (k & 1))
            copies.append(pltpu.make_async_remote_copy(
                src_ref=p_ref, dst_ref=buf.at[me], send_sem=send_sems.at[k - 1], recv_sem=recv_sems.at[k - 1],
                device_id=peer, device_id_type=MESH_ID))
        for cp in copies:
            cp.start()
        for cp in copies:
            cp.wait()
        acc = buf[0]
        for k in range(1, N_DEV):
            acc = acc + buf[k]
        o_ref[...] = acc

    vm = pl.BlockSpec(memory_space=pltpu.VMEM)
    return pl.pallas_call(
        body, name=name, in_specs=[vm], out_specs=vm, out_shape=_sds((r, w), F32),
        scratch_shapes=[pltpu.VMEM((N_DEV, r, w), F32), pltpu.SemaphoreType.DMA((N_DEV - 1,)),
                        pltpu.SemaphoreType.DMA((N_DEV - 1,))],
        compiler_params=_params(12 * r * w * 4),
    )(pack)


def _pack(arrs, row_multiple=16):
    flat = jnp.concatenate([a.reshape(-1).astype(F32) for a in arrs])
    unit = row_multiple * LANES
    total = -(-flat.shape[0] // unit) * unit
    return jnp.pad(flat, (0, total - flat.shape[0])).reshape(total // LANES, LANES)


def _unpack(packed, shapes):
    flat = packed.reshape(-1)
    out, off = [], 0
    for shp in shapes:
        n = int(np.prod(shp))
        out.append(flat[off:off + n].reshape(shp))
        off += n
    return out


def _pad_cols(a, width):
    return jnp.pad(a, [(0, 0)] * (a.ndim - 1) + [(0, width - a.shape[-1])])


ATTN_HEADS_PER_STEP = 4
ATTN_FWD_HEADS_PER_STEP = 4
ATTN_FWD_TILE = 1024
SMALL_SHARDED = ("a_norm", "a_conv_w", "a_conv_b", "a_ln_g", "a_ln_b", "c_norm", "c_conv_w")
SMALL_REPLICATED = ("b_norm", "b_f_bias", "b_q_norm", "b_k_norm")
BIG = ("a_w_in", "a_w_out", "b_w_in", "b_w_out", "c_w_in", "c_w_out")
WEIGHTS = ("a_norm", "a_w_in", "a_conv_w", "a_conv_b", "a_ln_g", "a_ln_b", "a_w_out", "b_norm", "b_w_in", "b_f_bias",
           "b_q_norm", "b_k_norm", "b_w_out", "c_norm", "c_w_in", "c_conv_w", "c_w_out")


def _mixer_a_fwd(x, p, l, tag):
    h = _rms_fwd(x, p["a_norm"][l][None], name=f"{tag}_rms")
    proj = _mm_nn(h, p["a_w_in"][l], name=f"{tag}_in", tn=p["a_w_in"][l].shape[2])
    u1 = _a_conv_fwd(proj, p["a_conv_w"][l], p["a_conv_b"][l][None], name=f"{tag}_conv")
    gated = _a_post_fwd(u1, proj, p["a_ln_g"][l][None], p["a_ln_b"][l][None], name=f"{tag}_post")
    y = _mm_nn(gated, p["a_w_out"][l], name=f"{tag}_out", tn=1024, add=x)
    return y, (x, h, proj, u1, gated)


def _mixer_a_bwd(dx, saved, p, l, tag):
    x, h, proj, u1, gated = saved
    g = {}
    g["a_w_out"] = _mm_tn(gated, dx, name=f"{tag}_dwout", out_dtype=BF16, tk=2048, tn=1024, ts=1024)
    dgated = _mm_nt(dx, p["a_w_out"][l], name=f"{tag}_dgated", tn=2048, tk=1024)
    du1, dz, g["a_ln_g"], g["a_ln_b"], g["a_conv_b"] = _a_post_bwd(
        dgated, u1, proj, p["a_ln_g"][l][None], p["a_ln_b"][l][None], name=f"{tag}_dpost")
    dproj, g["a_conv_w"] = _a_conv_bwd(du1, proj, dz, p["a_conv_w"][l], name=f"{tag}_dconv")
    g["a_w_in"] = _mm_tn(h, dproj, name=f"{tag}_dwin", out_dtype=BF16, out_width=p["a_w_in"][l].shape[2], ts=2048)
    dh = _mm_nt(dproj, p["a_w_in"][l], name=f"{tag}_dh", ksub=4)
    dx, g["a_norm"] = _rms_bwd(x, p["a_norm"][l][None], [dh], dx, name=f"{tag}_drms")
    return dx, g


def _mixer_c_fwd(x, p, tag):
    h = _rms_fwd(x, p["c_norm"][0][None], name=f"{tag}_rms")
    proj = _mm_nn(h, p["c_w_in"], name=f"{tag}_in", tn=1024)
    gated = _c_mid_fwd(proj, p["c_conv_w"][0], name=f"{tag}_mid")
    y = _mm_nn(gated, p["c_w_out"], name=f"{tag}_out", tn=1024, add=x)
    return y, (x, h, proj, gated)


def _mixer_c_bwd(dx, saved, p, tag):
    x, h, proj, gated = saved
    g = {}
    g["c_w_out"] = _mm_tn(gated, dx, name=f"{tag}_dwout", out_dtype=BF16, tk=2048, tn=1024, ts=1024)
    dgated = _mm_nt(dx, p["c_w_out"], name=f"{tag}_dgated", tn=2048, tk=1024)
    dproj, g["c_conv_w"] = _c_mid_bwd(dgated, proj, p["c_conv_w"][0], name=f"{tag}_dmid")
    g["c_w_in"] = _mm_tn(h, dproj, name=f"{tag}_dwin", out_dtype=BF16, out_width=p["c_w_in"].shape[2], tn=1024, ts=2048)
    dh = _mm_nt(dproj, p["c_w_in"], name=f"{tag}_dh", tk=1024, ksub=2)
    dx, g["c_norm"] = _rms_bwd(x, p["c_norm"][0][None], [dh], dx, name=f"{tag}_drms")
    return dx, g


def _mixer_b_fwd(x, p, tag):
    s_len = x.shape[0]
    n_heads = p["b_f_bias"].shape[1]
    hg = min(ATTN_FWD_HEADS_PER_STEP, n_heads)
    h = _rms_fwd(x, p["b_norm"], name=f"{tag}_rms")
    proj = _mm_nn(h, p["b_wq"], name=f"{tag}_in", tn=1024)
    fl = _mm_nn(h, p["b_wf"], name=f"{tag}_inf", tn=LANES)
    qn, kn, vb = _b_qk_fwd(proj, p["b_q_norm"], p["b_k_norm"], name=f"{tag}_qk")
    bias = _pad_cols(p["b_f_bias"], LANES)
    c = _b_cumsum(fl, bias, name=f"{tag}_cumsum")
    ch = c[:, :n_heads]
    c_row = ch.T.reshape(n_heads, 1, s_len)
    o, o_lo, lse, gated = _fox_fwd(qn, kn, vb, c_row, proj, name=f"{tag}_attn", hg=hg, t=ATTN_FWD_TILE)
    y = _mm_nn(gated, p["b_w_out"], name=f"{tag}_out", tn=1024, add=x)
    return y, (x, h, proj, fl, bias, qn, kn, vb, c_row, o, o_lo, lse, gated)


def _mixer_b_bwd(dx, saved, p, tag):
    hg = ATTN_HEADS_PER_STEP
    x, h, proj, fl, bias, qn, kn, vb, c_row, o, o_lo, lse, gated = saved
    s_len = x.shape[0]
    n_heads = p["b_f_bias"].shape[1]
    g = {}
    g["b_w_out"] = _mm_tn(gated, dx, name=f"{tag}_dwout", out_dtype=BF16, tk=2048, tn=1024, ts=1024)
    dgated = _mm_nt(dx, p["b_w_out"], name=f"{tag}_dgated", tn=2048, tk=1024)
    do, dz, delta = _b_gate_bwd(dgated, o, o_lo, proj, name=f"{tag}_dgate", hg=hg)
    dqn, dkn, dv, dc = _fox_bwd(qn, kn, vb, do, lse, delta, c_row, name=f"{tag}_dattn", hg=hg)
    dc_pad = _pad_cols(dc.reshape(n_heads, s_len).T, LANES)
    dfl, dbias = _b_cumsum_bwd(dc_pad, fl, bias, name=f"{tag}_dcumsum")
    g["b_f_bias"] = dbias[:, :n_heads]
    dproj, g["b_q_norm"], g["b_k_norm"] = _b_qk_bwd(dqn, dkn, dv, dz, proj, p["b_q_norm"], p["b_k_norm"], name=f"{tag}_dqk")
    dwq = _mm_tn(h, dproj, name=f"{tag}_dwin", out_dtype=BF16, tn=1024, ts=2048)
    dwf = _mm_tn(h, dfl, name=f"{tag}_dwinf", out_dtype=BF16, tn=LANES)
    g["b_w_in"] = jnp.concatenate([dwq, dwf[:, :n_heads]], axis=1)
    dh = _mm_nt(dproj, p["b_wq"], name=f"{tag}_dh", tk=1024, ksub=2)
    dhf = _mm_nt(dfl, p["b_wf"], name=f"{tag}_dhf", tk=LANES)
    dx, g["b_norm"] = _rms_bwd(x, p["b_norm"], [dh, dhf], dx, name=f"{tag}_drms")
    return dx, g


def kernel(x, a_norm, a_w_in, a_conv_w, a_conv_b, a_ln_g, a_ln_b, a_w_out, b_norm, b_w_in, b_f_bias, b_q_norm, b_k_norm, b_w_out, c_norm, c_w_in, c_conv_w, c_w_out, loss_target, m_a_norm, m_a_w_in, m_a_conv_w, m_a_conv_b, m_a_ln_g, m_a_ln_b, m_a_w_out, m_b_norm, m_b_w_in, m_b_f_bias, m_b_q_norm, m_b_k_norm, m_b_w_out, m_c_norm, m_c_w_in, m_c_conv_w, m_c_w_out, v_a_norm, v_a_w_in, v_a_conv_w, v_a_conv_b, v_a_ln_g, v_a_ln_b, v_a_w_out, v_b_norm, v_b_w_in, v_b_f_bias, v_b_q_norm, v_b_k_norm, v_b_w_out, v_c_norm, v_c_w_in, v_c_conv_w, v_c_w_out):
    w_loc = dict(a_norm=a_norm, a_w_in=a_w_in, a_conv_w=a_conv_w, a_conv_b=a_conv_b, a_ln_g=a_ln_g, a_ln_b=a_ln_b,
                 a_w_out=a_w_out, b_norm=b_norm, b_w_in=b_w_in, b_f_bias=b_f_bias, b_q_norm=b_q_norm, b_k_norm=b_k_norm,
                 b_w_out=b_w_out, c_norm=c_norm, c_w_in=c_w_in, c_conv_w=c_conv_w, c_w_out=c_w_out)
    m_loc = dict(a_norm=m_a_norm, a_w_in=m_a_w_in, a_conv_w=m_a_conv_w, a_conv_b=m_a_conv_b, a_ln_g=m_a_ln_g,
                 a_ln_b=m_a_ln_b, a_w_out=m_a_w_out, b_norm=m_b_norm, b_w_in=m_b_w_in, b_f_bias=m_b_f_bias,
                 b_q_norm=m_b_q_norm, b_k_norm=m_b_k_norm, b_w_out=m_b_w_out, c_norm=m_c_norm, c_w_in=m_c_w_in,
                 c_conv_w=m_c_conv_w, c_w_out=m_c_w_out)
    v_loc = dict(a_norm=v_a_norm, a_w_in=v_a_w_in, a_conv_w=v_a_conv_w, a_conv_b=v_a_conv_b, a_ln_g=v_a_ln_g,
                 a_ln_b=v_a_ln_b, a_w_out=v_a_w_out, b_norm=v_b_norm, b_w_in=v_b_w_in, b_f_bias=v_b_f_bias,
                 b_q_norm=v_b_q_norm, b_k_norm=v_b_k_norm, b_w_out=v_b_w_out, c_norm=v_c_norm, c_w_in=v_c_w_in,
                 c_conv_w=v_c_conv_w, c_w_out=v_c_w_out)
    n_a = a_w_in.shape[0]
    d_model = x.shape[2]
    e_inner = a_w_out.shape[1] * N_CHIPS
    n_heads = b_f_bias.shape[1]
    nb_loc = b_w_in.shape[2]
    nb_pad = -(-nb_loc // LANES) * LANES
    chip = 2 * lax.axis_index("x") + lax.axis_index("y")

    big_shards = ([a_w_in[l].astype(BF16) for l in range(n_a)] + [a_w_out[l].astype(BF16) for l in range(n_a)]
                  + [_pad_cols(b_w_in[0], nb_pad).astype(BF16), b_w_out[0].astype(BF16), c_w_in[0].astype(BF16),
                     c_w_out[0].astype(BF16)])
    small_pack = _pack([w_loc[n] for n in SMALL_SHARDED])
    gathered = _gather_weights(big_shards + [small_pack], name="gather_weights")
    p = {}
    p["a_w_in"] = gathered[0:n_a]
    p["a_w_out"] = [g.reshape(e_inner, d_model) for g in gathered[n_a:2 * n_a]]
    gb, gbo, gci, gco, gsmall = gathered[2 * n_a:]
    wb_full = jnp.concatenate([gb[k, :, :nb_loc] for k in range(N_CHIPS)], axis=1)
    p["b_wq"] = wb_full[:, :4 * e_inner]
    p["b_wf"] = _pad_cols(wb_full[:, 4 * e_inner:], LANES)
    p["b_w_out"] = gbo.reshape(e_inner, d_model)
    p["c_w_in"] = gci
    p["c_w_out"] = gco.reshape(e_inner, d_model)
    small_shapes = [w_loc[n].shape for n in SMALL_SHARDED]
    per_chip = [_unpack(gsmall[k], small_shapes) for k in range(N_CHIPS)]
    for idx, n in enumerate(SMALL_SHARDED):
        p[n] = jnp.concatenate([per_chip[k][idx] for k in range(N_CHIPS)], axis=-1)
    for n in SMALL_REPLICATED:
        p[n] = w_loc[n]

    x0 = x[0]
    x1, sv0 = _mixer_a_fwd(x0, p, 0, "a0")
    x2, sv1 = _mixer_b_fwd(x1, p, "b0")
    x3, sv2 = _mixer_c_fwd(x2, p, "c0")
    x4, sv3 = _mixer_a_fwd(x3, p, 1, "a1")
    dy, loss_part = _loss_head(x4, loss_target[0], name="loss_head")
    loss = lax.psum(loss_part[0, 0], ("x", "y", "c"))

    dx, g3 = _mixer_a_bwd(dy, sv3, p, 1, "a1")
    dx, g2 = _mixer_c_bwd(dx, sv2, p, "c0")
    dx, g1 = _mixer_b_bwd(dx, sv1, p, "b0")
    dx, g0 = _mixer_a_bwd(dx, sv0, p, 0, "a0")
    grad_x = dx[None]

    half_rows = e_inner // N_CHIPS
    gb_full = g1["b_w_in"].reshape(d_model, N_CHIPS, nb_loc).transpose(1, 0, 2)
    big_grads = ([g0["a_w_in"], g3["a_w_in"]]
                 + [g0["a_w_out"].reshape(N_CHIPS, half_rows, d_model), g3["a_w_out"].reshape(N_CHIPS, half_rows, d_model)]
                 + [_pad_cols(gb_full, nb_pad), g1["b_w_out"].reshape(N_CHIPS, half_rows, d_model), g2["c_w_in"],
                    g2["c_w_out"].reshape(N_CHIPS, half_rows, d_model)])
    red, c_arr = _reduce_grads(big_grads, name="reduce_grads")
    big_order = [("a_w_in", 0), ("a_w_in", 1), ("a_w_out", 0), ("a_w_out", 1), ("b_w_in", 0), ("b_w_out", 0),
                 ("c_w_in", 0), ("c_w_out", 0)]

    grads, delta, new_m, new_v = {}, {}, {}, {}
    per_layer = {n: [] for n in BIG}
    for (n, l), (mine, other) in zip(big_order, red):
        cols = mine.shape[1]
        true_cols = w_loc[n].shape[-1]
        outs = _adamw_halves(_pad_cols(w_loc[n][l], cols), mine, other, _pad_cols(m_loc[n][l], cols),
                             _pad_cols(v_loc[n][l], cols), c_arr, name=f"adamw_{n}{l}")
        per_layer[n].append([a[:, :true_cols] for a in outs])
    for n in BIG:
        grads[n], delta[n], new_m[n], new_v[n] = [jnp.stack([layer[k] for layer in per_layer[n]]) for k in range(4)]

    small_full = {}
    for n in ("a_norm", "a_conv_w", "a_conv_b", "a_ln_g", "a_ln_b"):
        small_full[n] = jnp.stack([g0[n].reshape(p[n].shape[1:]), g3[n].reshape(p[n].shape[1:])])
    small_full["c_norm"] = g2["c_norm"].reshape(p["c_norm"].shape)
    small_full["c_conv_w"] = g2["c_conv_w"].reshape(p["c_conv_w"].shape)
    for n in SMALL_REPLICATED:
        small_full[n] = g1[n].reshape(w_loc[n].shape)
    small_names = SMALL_SHARDED + SMALL_REPLICATED
    summed = _unpack(_allreduce_small(_pack([small_full[n] for n in small_names], 8), name="reduce_small"),
                     [small_full[n].shape for n in small_names])
    for n, s in zip(small_names, summed):
        if n in SMALL_SHARDED:
            width = w_loc[n].shape[-1]
            grads[n] = lax.dynamic_slice_in_dim(s, chip * width, width, axis=s.ndim - 1)
        else:
            grads[n] = s

    small_shapes_all = [w_loc[n].shape for n in small_names]
    _, d_, m_, v_ = _adamw(_pack([w_loc[n] for n in small_names], 8), [_pack([grads[n] for n in small_names], 8)],
                           _pack([m_loc[n] for n in small_names], 8), _pack([v_loc[n] for n in small_names], 8),
                           name="adamw_small")
    for n, a, b, c_ in zip(small_names, _unpack(d_, small_shapes_all), _unpack(m_, small_shapes_all),
                           _unpack(v_, small_shapes_all)):
        delta[n], new_m[n], new_v[n] = a, b, c_

    return (loss, grad_x, *[grads[n] for n in WEIGHTS], *[delta[n] for n in WEIGHTS],
            *[new_m[n] for n in WEIGHTS], *[new_v[n] for n in WEIGHTS])
```
